```python
import jax, jax.numpy as jnp
from jax import lax
import numpy as np

D_MODEL = 1024
BATCH = 8
SEQ = 2048
DEPTH = 1

LRU_WIDTH = 1024
LRU_BLOCKS = 16
LRU_BLOCK_W = LRU_WIDTH // LRU_BLOCKS
CONV_WIDTH = 4
LRU_C = 8.0
HEAD_DIM = 64
N_Q_HEADS = 16
N_KV_HEADS = 4
Q_PER_KV = N_Q_HEADS // N_KV_HEADS
ATTN_WIDTH = N_Q_HEADS * HEAD_DIM
KV_WIDTH = N_KV_HEADS * HEAD_DIM
WINDOW = 128
BLOCK = 128
ROPE_THETA = 10000.0
D_FF = 2816
MACARON_SCALE = 0.5
NORM_EPS = 1e-6
MASK_VALUE = -1e30
IN_WIDTH = 2 * LRU_WIDTH + ATTN_WIDTH + 2 * KV_WIDTH + 2 * D_MODEL
IN_SPLITS = (
    LRU_WIDTH,
    2 * LRU_WIDTH,
    2 * LRU_WIDTH + ATTN_WIDTH,
    2 * LRU_WIDTH + ATTN_WIDTH + KV_WIDTH,
    2 * LRU_WIDTH + ATTN_WIDTH + 2 * KV_WIDTH,
    2 * LRU_WIDTH + ATTN_WIDTH + 2 * KV_WIDTH + D_MODEL,
)

kernel_name = "hybrid_rglru_swa_sink_macaron_block"


def rms_norm(x, g):
    xf = x.astype(jnp.float32)
    y = xf * lax.rsqrt(jnp.mean(xf * xf, axis=-1, keepdims=True) + NORM_EPS)
    return (y * g.astype(jnp.float32)).astype(x.dtype)


def swiglu(x, w_gu, w_down):
    g, u = jnp.split(x @ w_gu, 2, axis=-1)
    return (jax.nn.silu(g) * u) @ w_down


def rope_tables(seq_len):
    half = HEAD_DIM // 2
    inv_freq = ROPE_THETA ** (-jnp.arange(half, dtype=jnp.float32) / half)
    ang = jnp.arange(seq_len, dtype=jnp.float32)[:, None] * inv_freq[None, :]
    return jnp.cos(ang), jnp.sin(ang)


def apply_rope(x, cos, sin):
    half = HEAD_DIM // 2
    xf = x.astype(jnp.float32)
    x1, x2 = xf[..., :half], xf[..., half:]
    c = cos[None, :, None, :]
    s = sin[None, :, None, :]
    return jnp.concatenate([x1 * c - x2 * s, x2 * c + x1 * s], axis=-1).astype(x.dtype)


def causal_depthwise_conv(x, w, b):
    s = x.shape[1]
    xp = jnp.pad(x, ((0, 0), (CONV_WIDTH - 1, 0), (0, 0)))
    y = xp[:, 0:s] * w[0]
    for k in range(1, CONV_WIDTH):
        y = y + xp[:, k:k + s] * w[k]
    return y + b


def block_diag_linear(x, w, b):
    bsz, s, _ = x.shape
    xb = x.reshape(bsz, s, LRU_BLOCKS, LRU_BLOCK_W)
    y = jnp.einsum('bsnc,ncd->bsnd', xb, w).reshape(bsz, s, LRU_WIDTH)
    return y + b


def rg_lru(x, w_a, b_a, w_x, b_x, lam):
    xf = x.astype(jnp.float32)
    r = jax.nn.sigmoid(block_diag_linear(x, w_a, b_a).astype(jnp.float32))
    i = jax.nn.sigmoid(block_diag_linear(x, w_x, b_x).astype(jnp.float32))
    log_a = -LRU_C * r * jax.nn.softplus(-lam.astype(jnp.float32))
    a = jnp.exp(log_a)
    mult = jnp.sqrt(jnp.maximum(-jnp.expm1(2.0 * log_a), 0.0))
    bterm = mult * (i * xf)

    def combine(left, right):
        a1, b1 = left
        a2, b2 = right
        return a1 * a2, a2 * b1 + b2

    _, h = lax.associative_scan(combine, (a, bterm), axis=1)
    return h.astype(x.dtype)


def band_blocks(t):
    bsz, s, h, d = t.shape
    tb = t.reshape(bsz, s // BLOCK, BLOCK, h, d)
    prev = jnp.pad(tb, ((0, 0), (1, 0), (0, 0), (0, 0), (0, 0)))[:, :-1]
    return jnp.concatenate([prev, tb], axis=2)


def sliding_window_attention_with_sinks(q, k, v, sinks):
    bsz, s = q.shape[0], q.shape[1]
    nb = s // BLOCK
    qb = q.reshape(bsz, nb, BLOCK, N_KV_HEADS, Q_PER_KV, HEAD_DIM)
    kb = band_blocks(k)
    vb = band_blocks(v)
    scores = jnp.einsum('bnqhgd,bnshd->bhgnqs', qb, kb).astype(jnp.float32) * (HEAD_DIM ** -0.5)
    qi = jnp.arange(BLOCK)[:, None]
    si = jnp.arange(2 * BLOCK)[None, :]
    diff = BLOCK + qi - si
    key_pos = (jnp.arange(nb)[:, None, None] - 1) * BLOCK + si[None]
    mask = ((diff >= 0) & (diff < WINDOW))[None] & (key_pos >= 0)
    scores = jnp.where(mask, scores, MASK_VALUE)
    sink = sinks.astype(jnp.float32).reshape(1, N_KV_HEADS, Q_PER_KV, 1, 1, 1)
    sink = jnp.broadcast_to(sink, scores.shape[:-1] + (1,))
    probs = jax.nn.softmax(jnp.concatenate([scores, sink], axis=-1), axis=-1)[..., :-1]
    out = jnp.einsum('bhgnqs,bnshd->bnqhgd', probs.astype(v.dtype), vb)
    return out.reshape(bsz, s, ATTN_WIDTH)


def hybrid_mixer(u, cos, sin, w_in, conv_w, conv_b, lru_w_a, lru_b_a, lru_w_x, lru_b_x,
                 lru_lambda, attn_sinks, w_proj_lru, w_proj_attn, w_out):
    bsz, s, _ = u.shape
    z = u @ w_in
    gate_br, x_br, q, k, v, g_lru, g_attn = jnp.split(z, IN_SPLITS, axis=-1)
    xc = causal_depthwise_conv(x_br, conv_w, conv_b)
    y_lru = rg_lru(xc, lru_w_a, lru_b_a, lru_w_x, lru_b_x, lru_lambda) * jax.nn.gelu(gate_br)
    q = apply_rope(q.reshape(bsz, s, N_Q_HEADS, HEAD_DIM), cos, sin)
    k = apply_rope(k.reshape(bsz, s, N_KV_HEADS, HEAD_DIM), cos, sin)
    v = v.reshape(bsz, s, N_KV_HEADS, HEAD_DIM)
    y_attn = sliding_window_attention_with_sinks(q, k, v, attn_sinks)
    merged = jax.nn.sigmoid(g_lru) * (y_lru @ w_proj_lru) + jax.nn.sigmoid(g_attn) * (y_attn @ w_proj_attn)
    return merged @ w_out


def _fwd_setup_inputs(seed: int = 0) -> dict:
    key = jax.random.key(seed)
    ks = jax.random.split(key, 24)

    def nrm(k, shape, scale):
        return jax.random.normal(k, shape, jnp.float32) * scale

    def gain(k, shape):
        return 1.0 + 0.05 * jax.random.normal(k, shape, jnp.float32)

    u = jax.random.uniform(ks[13], (DEPTH, LRU_WIDTH), jnp.float32, minval=0.9, maxval=0.999)
    a0 = u ** (1.0 / LRU_C)
    lam = jnp.log(a0) - jnp.log1p(-a0)
    return {
        "x": nrm(ks[0], (BATCH, SEQ, D_MODEL), 1.0),
        "ffn1_pre_g": gain(ks[1], (DEPTH, D_MODEL)),
        "ffn1_w_gu": nrm(ks[2], (DEPTH, D_MODEL, 2 * D_FF), D_MODEL ** -0.5),
        "ffn1_w_down": nrm(ks[3], (DEPTH, D_FF, D_MODEL), D_FF ** -0.5),
        "ffn1_post_g": gain(ks[4], (DEPTH, D_MODEL)),
        "mix_pre_g": gain(ks[5], (DEPTH, D_MODEL)),
        "w_in": nrm(ks[6], (DEPTH, D_MODEL, IN_WIDTH), D_MODEL ** -0.5),
        "conv_w": nrm(ks[7], (DEPTH, CONV_WIDTH, LRU_WIDTH), CONV_WIDTH ** -0.5),
        "conv_b": nrm(ks[8], (DEPTH, LRU_WIDTH), 0.01),
        "lru_w_a": nrm(ks[9], (DEPTH, LRU_BLOCKS, LRU_BLOCK_W, LRU_BLOCK_W), LRU_BLOCK_W ** -0.5),
        "lru_b_a": nrm(ks[10], (DEPTH, LRU_WIDTH), 0.01),
        "lru_w_x": nrm(ks[11], (DEPTH, LRU_BLOCKS, LRU_BLOCK_W, LRU_BLOCK_W), LRU_BLOCK_W ** -0.5),
        "lru_b_x": nrm(ks[12], (DEPTH, LRU_WIDTH), 0.01),
        "lru_lambda": lam,
        "attn_sinks": nrm(ks[14], (DEPTH, N_Q_HEADS), 0.5),
        "w_proj_lru": nrm(ks[15], (DEPTH, LRU_WIDTH, D_MODEL), LRU_WIDTH ** -0.5),
        "w_proj_attn": nrm(ks[16], (DEPTH, ATTN_WIDTH, D_MODEL), ATTN_WIDTH ** -0.5),
        "w_out": nrm(ks[17], (DEPTH, D_MODEL, D_MODEL), D_MODEL ** -0.5),
        "mix_post_g": gain(ks[18], (DEPTH, D_MODEL)),
        "ffn2_pre_g": gain(ks[19], (DEPTH, D_MODEL)),
        "ffn2_w_gu": nrm(ks[20], (DEPTH, D_MODEL, 2 * D_FF), D_MODEL ** -0.5),
        "ffn2_w_down": nrm(ks[21], (DEPTH, D_FF, D_MODEL), D_FF ** -0.5),
        "ffn2_post_g": gain(ks[22], (DEPTH, D_MODEL)),
    }


def _fwd_reference(x, ffn1_pre_g, ffn1_w_gu, ffn1_w_down, ffn1_post_g,
              mix_pre_g, w_in, conv_w, conv_b, lru_w_a, lru_b_a, lru_w_x, lru_b_x,
              lru_lambda, attn_sinks, w_proj_lru, w_proj_attn, w_out, mix_post_g,
              ffn2_pre_g, ffn2_w_gu, ffn2_w_down, ffn2_post_g):
    cos, sin = rope_tables(x.shape[1])
    h = x
    for l in range(DEPTH):
        f = swiglu(rms_norm(h, ffn1_pre_g[l]), ffn1_w_gu[l], ffn1_w_down[l])
        h = h + MACARON_SCALE * rms_norm(f, ffn1_post_g[l])
        m = hybrid_mixer(rms_norm(h, mix_pre_g[l]), cos, sin, w_in[l], conv_w[l], conv_b[l],
                         lru_w_a[l], lru_b_a[l], lru_w_x[l], lru_b_x[l], lru_lambda[l],
                         attn_sinks[l], w_proj_lru[l], w_proj_attn[l], w_out[l])
        h = h + rms_norm(m, mix_post_g[l])
        f = swiglu(rms_norm(h, ffn2_pre_g[l]), ffn2_w_gu[l], ffn2_w_down[l])
        h = h + MACARON_SCALE * rms_norm(f, ffn2_post_g[l])
    return h


import jax as _jax
import jax.numpy as _jnp

TWIN_FORMAT = 'train_step'
FWD_PARAMS = ['x', 'ffn1_pre_g', 'ffn1_w_gu', 'ffn1_w_down', 'ffn1_post_g', 'mix_pre_g', 'w_in', 'conv_w', 'conv_b', 'lru_w_a', 'lru_b_a', 'lru_w_x', 'lru_b_x', 'lru_lambda', 'attn_sinks', 'w_proj_lru', 'w_proj_attn', 'w_out', 'mix_post_g', 'ffn2_pre_g', 'ffn2_w_gu', 'ffn2_w_down', 'ffn2_post_g']
TWIN_WEIGHTS = ['ffn1_pre_g', 'ffn1_w_gu', 'ffn1_w_down', 'ffn1_post_g', 'mix_pre_g', 'w_in', 'conv_w', 'conv_b', 'lru_w_a', 'lru_b_a', 'lru_w_x', 'lru_b_x', 'lru_lambda', 'attn_sinks', 'w_proj_lru', 'w_proj_attn', 'w_out', 'mix_post_g', 'ffn2_pre_g', 'ffn2_w_gu', 'ffn2_w_down', 'ffn2_post_g']
TWIN_DIFF_INPUT = 'x'
TWIN_INPUTS = ['x', 'ffn1_pre_g', 'ffn1_w_gu', 'ffn1_w_down', 'ffn1_post_g', 'mix_pre_g', 'w_in', 'conv_w', 'conv_b', 'lru_w_a', 'lru_b_a', 'lru_w_x', 'lru_b_x', 'lru_lambda', 'attn_sinks', 'w_proj_lru', 'w_proj_attn', 'w_out', 'mix_post_g', 'ffn2_pre_g', 'ffn2_w_gu', 'ffn2_w_down', 'ffn2_post_g', 'loss_target', 'm_ffn1_pre_g', 'm_ffn1_w_gu', 'm_ffn1_w_down', 'm_ffn1_post_g', 'm_mix_pre_g', 'm_w_in', 'm_conv_w', 'm_conv_b', 'm_lru_w_a', 'm_lru_b_a', 'm_lru_w_x', 'm_lru_b_x', 'm_lru_lambda', 'm_attn_sinks', 'm_w_proj_lru', 'm_w_proj_attn', 'm_w_out', 'm_mix_post_g', 'm_ffn2_pre_g', 'm_ffn2_w_gu', 'm_ffn2_w_down', 'm_ffn2_post_g', 'v_ffn1_pre_g', 'v_ffn1_w_gu', 'v_ffn1_w_down', 'v_ffn1_post_g', 'v_mix_pre_g', 'v_w_in', 'v_conv_w', 'v_conv_b', 'v_lru_w_a', 'v_lru_b_a', 'v_lru_w_x', 'v_lru_b_x', 'v_lru_lambda', 'v_attn_sinks', 'v_w_proj_lru', 'v_w_proj_attn', 'v_w_out', 'v_mix_post_g', 'v_ffn2_pre_g', 'v_ffn2_w_gu', 'v_ffn2_w_down', 'v_ffn2_post_g']
TWIN_OUTPUTS = ['loss', 'grad_x', 'grad_ffn1_pre_g', 'grad_ffn1_w_gu', 'grad_ffn1_w_down', 'grad_ffn1_post_g', 'grad_mix_pre_g', 'grad_w_in', 'grad_conv_w', 'grad_conv_b', 'grad_lru_w_a', 'grad_lru_b_a', 'grad_lru_w_x', 'grad_lru_b_x', 'grad_lru_lambda', 'grad_attn_sinks', 'grad_w_proj_lru', 'grad_w_proj_attn', 'grad_w_out', 'grad_mix_post_g', 'grad_ffn2_pre_g', 'grad_ffn2_w_gu', 'grad_ffn2_w_down', 'grad_ffn2_post_g', 'delta_ffn1_pre_g', 'delta_ffn1_w_gu', 'delta_ffn1_w_down', 'delta_ffn1_post_g', 'delta_mix_pre_g', 'delta_w_in', 'delta_conv_w', 'delta_conv_b', 'delta_lru_w_a', 'delta_lru_b_a', 'delta_lru_w_x', 'delta_lru_b_x', 'delta_lru_lambda', 'delta_attn_sinks', 'delta_w_proj_lru', 'delta_w_proj_attn', 'delta_w_out', 'delta_mix_post_g', 'delta_ffn2_pre_g', 'delta_ffn2_w_gu', 'delta_ffn2_w_down', 'delta_ffn2_post_g', 'new_m_ffn1_pre_g', 'new_m_ffn1_w_gu', 'new_m_ffn1_w_down', 'new_m_ffn1_post_g', 'new_m_mix_pre_g', 'new_m_w_in', 'new_m_conv_w', 'new_m_conv_b', 'new_m_lru_w_a', 'new_m_lru_b_a', 'new_m_lru_w_x', 'new_m_lru_b_x', 'new_m_lru_lambda', 'new_m_attn_sinks', 'new_m_w_proj_lru', 'new_m_w_proj_attn', 'new_m_w_out', 'new_m_mix_post_g', 'new_m_ffn2_pre_g', 'new_m_ffn2_w_gu', 'new_m_ffn2_w_down', 'new_m_ffn2_post_g', 'new_v_ffn1_pre_g', 'new_v_ffn1_w_gu', 'new_v_ffn1_w_down', 'new_v_ffn1_post_g', 'new_v_mix_pre_g', 'new_v_w_in', 'new_v_conv_w', 'new_v_conv_b', 'new_v_lru_w_a', 'new_v_lru_b_a', 'new_v_lru_w_x', 'new_v_lru_b_x', 'new_v_lru_lambda', 'new_v_attn_sinks', 'new_v_w_proj_lru', 'new_v_w_proj_attn', 'new_v_w_out', 'new_v_mix_post_g', 'new_v_ffn2_pre_g', 'new_v_ffn2_w_gu', 'new_v_ffn2_w_down', 'new_v_ffn2_post_g']
TWIN_LEAF_KINDS = {'loss': 'loss', 'grad_x': 'grad_x', 'grad_ffn1_pre_g': 'grad_w', 'grad_ffn1_w_gu': 'grad_w', 'grad_ffn1_w_down': 'grad_w', 'grad_ffn1_post_g': 'grad_w', 'grad_mix_pre_g': 'grad_w', 'grad_w_in': 'grad_w', 'grad_conv_w': 'grad_w', 'grad_conv_b': 'grad_w', 'grad_lru_w_a': 'grad_w', 'grad_lru_b_a': 'grad_w', 'grad_lru_w_x': 'grad_w', 'grad_lru_b_x': 'grad_w', 'grad_lru_lambda': 'grad_w', 'grad_attn_sinks': 'grad_w', 'grad_w_proj_lru': 'grad_w', 'grad_w_proj_attn': 'grad_w', 'grad_w_out': 'grad_w', 'grad_mix_post_g': 'grad_w', 'grad_ffn2_pre_g': 'grad_w', 'grad_ffn2_w_gu': 'grad_w', 'grad_ffn2_w_down': 'grad_w', 'grad_ffn2_post_g': 'grad_w', 'delta_ffn1_pre_g': 'delta_w', 'delta_ffn1_w_gu': 'delta_w', 'delta_ffn1_w_down': 'delta_w', 'delta_ffn1_post_g': 'delta_w', 'delta_mix_pre_g': 'delta_w', 'delta_w_in': 'delta_w', 'delta_conv_w': 'delta_w', 'delta_conv_b': 'delta_w', 'delta_lru_w_a': 'delta_w', 'delta_lru_b_a': 'delta_w', 'delta_lru_w_x': 'delta_w', 'delta_lru_b_x': 'delta_w', 'delta_lru_lambda': 'delta_w', 'delta_attn_sinks': 'delta_w', 'delta_w_proj_lru': 'delta_w', 'delta_w_proj_attn': 'delta_w', 'delta_w_out': 'delta_w', 'delta_mix_post_g': 'delta_w', 'delta_ffn2_pre_g': 'delta_w', 'delta_ffn2_w_gu': 'delta_w', 'delta_ffn2_w_down': 'delta_w', 'delta_ffn2_post_g': 'delta_w', 'new_m_ffn1_pre_g': 'new_m', 'new_m_ffn1_w_gu': 'new_m', 'new_m_ffn1_w_down': 'new_m', 'new_m_ffn1_post_g': 'new_m', 'new_m_mix_pre_g': 'new_m', 'new_m_w_in': 'new_m', 'new_m_conv_w': 'new_m', 'new_m_conv_b': 'new_m', 'new_m_lru_w_a': 'new_m', 'new_m_lru_b_a': 'new_m', 'new_m_lru_w_x': 'new_m', 'new_m_lru_b_x': 'new_m', 'new_m_lru_lambda': 'new_m', 'new_m_attn_sinks': 'new_m', 'new_m_w_proj_lru': 'new_m', 'new_m_w_proj_attn': 'new_m', 'new_m_w_out': 'new_m', 'new_m_mix_post_g': 'new_m', 'new_m_ffn2_pre_g': 'new_m', 'new_m_ffn2_w_gu': 'new_m', 'new_m_ffn2_w_down': 'new_m', 'new_m_ffn2_post_g': 'new_m', 'new_v_ffn1_pre_g': 'new_v', 'new_v_ffn1_w_gu': 'new_v', 'new_v_ffn1_w_down': 'new_v', 'new_v_ffn1_post_g': 'new_v', 'new_v_mix_pre_g': 'new_v', 'new_v_w_in': 'new_v', 'new_v_conv_w': 'new_v', 'new_v_conv_b': 'new_v', 'new_v_lru_w_a': 'new_v', 'new_v_lru_b_a': 'new_v', 'new_v_lru_w_x': 'new_v', 'new_v_lru_b_x': 'new_v', 'new_v_lru_lambda': 'new_v', 'new_v_attn_sinks': 'new_v', 'new_v_w_proj_lru': 'new_v', 'new_v_w_proj_attn': 'new_v', 'new_v_w_out': 'new_v', 'new_v_mix_post_g': 'new_v', 'new_v_ffn2_pre_g': 'new_v', 'new_v_ffn2_w_gu': 'new_v', 'new_v_ffn2_w_down': 'new_v', 'new_v_ffn2_post_g': 'new_v'}


def _forward(args):
    return _fwd_reference(*[args[k] for k in FWD_PARAMS])


def _output_shape():
    out = _jax.eval_shape(lambda: _forward(_fwd_setup_inputs(0)))
    return out.shape, out.dtype

N_MICROBATCH = 1
ADAM_LR = 0.001
ADAM_B1 = 0.9
ADAM_B2 = 0.999
ADAM_EPS = 1e-08
ADAM_WD = 0.01
ADAM_STEP = 10
PER_EXAMPLE_BATCH_AXIS = {'x': 0, 'loss_target': 0}
SHARED_INPUTS = []
_WEIGHT_DTYPES = {'ffn1_pre_g': _jnp.float32, 'ffn1_w_gu': _jnp.float32, 'ffn1_w_down': _jnp.float32, 'ffn1_post_g': _jnp.float32, 'mix_pre_g': _jnp.float32, 'w_in': _jnp.float32, 'conv_w': _jnp.float32, 'conv_b': _jnp.float32, 'lru_w_a': _jnp.float32, 'lru_b_a': _jnp.float32, 'lru_w_x': _jnp.float32, 'lru_b_x': _jnp.float32, 'lru_lambda': _jnp.float32, 'attn_sinks': _jnp.float32, 'w_proj_lru': _jnp.float32, 'w_proj_attn': _jnp.float32, 'w_out': _jnp.float32, 'mix_post_g': _jnp.float32, 'ffn2_pre_g': _jnp.float32, 'ffn2_w_gu': _jnp.float32, 'ffn2_w_down': _jnp.float32, 'ffn2_post_g': _jnp.float32}
MOMENT_SCALE = {'ffn1_pre_g': 3.112577e-01, 'ffn1_w_gu': 1.264380e-01, 'ffn1_w_down': 2.135934e-01, 'ffn1_post_g': 3.971065e+00, 'mix_pre_g': 3.717434e-01, 'w_in': 1.586259e-01, 'conv_w': 2.912760e-01, 'conv_b': 4.490399e+00, 'lru_w_a': 1.539435e-01, 'lru_b_a': 7.713827e-02, 'lru_w_x': 2.867170e-01, 'lru_b_x': 9.231618e-02, 'lru_lambda': 1.381643e-01, 'attn_sinks': 6.552714e-02, 'w_proj_lru': 3.442947e-01, 'w_proj_attn': 1.129227e-01, 'w_out': 3.483790e-01, 'mix_post_g': 1.611976e+01, 'ffn2_pre_g': 2.337851e-01, 'ffn2_w_gu': 1.056466e-01, 'ffn2_w_down': 2.025640e-01, 'ffn2_post_g': 4.014975e+00}


def _to_microbatches(a, axis):
    t = _jnp.moveaxis(a, axis, 0)
    t = t.reshape((N_MICROBATCH, t.shape[0] // N_MICROBATCH) + t.shape[1:])
    return _jnp.moveaxis(t, 1, axis + 1)


def setup_inputs(seed: int = 0) -> dict:
    inp = _fwd_setup_inputs(seed)
    key = _jax.random.fold_in(_jax.random.key(seed), 7919)
    shape, _ = _output_shape()
    out = dict(inp)
    out["loss_target"] = _jax.random.normal(_jax.random.fold_in(key, 0), shape, _jnp.float32)
    for i, name in enumerate(TWIN_WEIGHTS):
        w = inp[name].astype(_jnp.float32)
        if MOMENT_SCALE is None:
            s = _jnp.sqrt(_jnp.mean(_jnp.square(w)) + 1e-30)
        else:
            s = MOMENT_SCALE[name]
        km, kv = _jax.random.split(_jax.random.fold_in(key, i + 1))
        out[name] = w
        out["m_" + name] = s * _jax.random.normal(km, w.shape, _jnp.float32)
        out["v_" + name] = (s * s) * _jax.random.uniform(kv, w.shape, _jnp.float32, 0.5, 1.5)
    if N_MICROBATCH > 1:
        for name, axis in PER_EXAMPLE_BATCH_AXIS.items():
            out[name] = _to_microbatches(out[name], axis)
    return {'x': out['x'], 'ffn1_pre_g': out['ffn1_pre_g'], 'ffn1_w_gu': out['ffn1_w_gu'], 'ffn1_w_down': out['ffn1_w_down'], 'ffn1_post_g': out['ffn1_post_g'], 'mix_pre_g': out['mix_pre_g'], 'w_in': out['w_in'], 'conv_w': out['conv_w'], 'conv_b': out['conv_b'], 'lru_w_a': out['lru_w_a'], 'lru_b_a': out['lru_b_a'], 'lru_w_x': out['lru_w_x'], 'lru_b_x': out['lru_b_x'], 'lru_lambda': out['lru_lambda'], 'attn_sinks': out['attn_sinks'], 'w_proj_lru': out['w_proj_lru'], 'w_proj_attn': out['w_proj_attn'], 'w_out': out['w_out'], 'mix_post_g': out['mix_post_g'], 'ffn2_pre_g': out['ffn2_pre_g'], 'ffn2_w_gu': out['ffn2_w_gu'], 'ffn2_w_down': out['ffn2_w_down'], 'ffn2_post_g': out['ffn2_post_g'], 'loss_target': out['loss_target'], 'm_ffn1_pre_g': out['m_ffn1_pre_g'], 'm_ffn1_w_gu': out['m_ffn1_w_gu'], 'm_ffn1_w_down': out['m_ffn1_w_down'], 'm_ffn1_post_g': out['m_ffn1_post_g'], 'm_mix_pre_g': out['m_mix_pre_g'], 'm_w_in': out['m_w_in'], 'm_conv_w': out['m_conv_w'], 'm_conv_b': out['m_conv_b'], 'm_lru_w_a': out['m_lru_w_a'], 'm_lru_b_a': out['m_lru_b_a'], 'm_lru_w_x': out['m_lru_w_x'], 'm_lru_b_x': out['m_lru_b_x'], 'm_lru_lambda': out['m_lru_lambda'], 'm_attn_sinks': out['m_attn_sinks'], 'm_w_proj_lru': out['m_w_proj_lru'], 'm_w_proj_attn': out['m_w_proj_attn'], 'm_w_out': out['m_w_out'], 'm_mix_post_g': out['m_mix_post_g'], 'm_ffn2_pre_g': out['m_ffn2_pre_g'], 'm_ffn2_w_gu': out['m_ffn2_w_gu'], 'm_ffn2_w_down': out['m_ffn2_w_down'], 'm_ffn2_post_g': out['m_ffn2_post_g'], 'v_ffn1_pre_g': out['v_ffn1_pre_g'], 'v_ffn1_w_gu': out['v_ffn1_w_gu'], 'v_ffn1_w_down': out['v_ffn1_w_down'], 'v_ffn1_post_g': out['v_ffn1_post_g'], 'v_mix_pre_g': out['v_mix_pre_g'], 'v_w_in': out['v_w_in'], 'v_conv_w': out['v_conv_w'], 'v_conv_b': out['v_conv_b'], 'v_lru_w_a': out['v_lru_w_a'], 'v_lru_b_a': out['v_lru_b_a'], 'v_lru_w_x': out['v_lru_w_x'], 'v_lru_b_x': out['v_lru_b_x'], 'v_lru_lambda': out['v_lru_lambda'], 'v_attn_sinks': out['v_attn_sinks'], 'v_w_proj_lru': out['v_w_proj_lru'], 'v_w_proj_attn': out['v_w_proj_attn'], 'v_w_out': out['v_w_out'], 'v_mix_post_g': out['v_mix_post_g'], 'v_ffn2_pre_g': out['v_ffn2_pre_g'], 'v_ffn2_w_gu': out['v_ffn2_w_gu'], 'v_ffn2_w_down': out['v_ffn2_w_down'], 'v_ffn2_post_g': out['v_ffn2_post_g']}


def _loss(weights, diff, rest, loss_target):
    with _jax.named_scope("forward"):
        args = {**rest, TWIN_DIFF_INPUT: diff, **{k: w.astype(_WEIGHT_DTYPES[k]) for k, w in weights.items()}}
        y = _forward(args)
    with _jax.named_scope("loss_head"):
        err = _jnp.square(y.astype(_jnp.float32) - loss_target)
        return 0.5 * _jnp.sum(_jnp.mean(err, axis=-1)) if err.ndim else 0.5 * err


def _adamw(w, g, m, v):
    m = ADAM_B1 * m + (1.0 - ADAM_B1) * g
    v = ADAM_B2 * v + (1.0 - ADAM_B2) * _jnp.square(g)
    m_hat = m / (1.0 - ADAM_B1 ** ADAM_STEP)
    v_hat = v / (1.0 - ADAM_B2 ** ADAM_STEP)
    delta = -ADAM_LR * (m_hat / (_jnp.sqrt(v_hat) + ADAM_EPS) + ADAM_WD * w)
    return delta, m, v


def reference(x, ffn1_pre_g, ffn1_w_gu, ffn1_w_down, ffn1_post_g, mix_pre_g, w_in, conv_w, conv_b, lru_w_a, lru_b_a, lru_w_x, lru_b_x, lru_lambda, attn_sinks, w_proj_lru, w_proj_attn, w_out, mix_post_g, ffn2_pre_g, ffn2_w_gu, ffn2_w_down, ffn2_post_g, loss_target, m_ffn1_pre_g, m_ffn1_w_gu, m_ffn1_w_down, m_ffn1_post_g, m_mix_pre_g, m_w_in, m_conv_w, m_conv_b, m_lru_w_a, m_lru_b_a, m_lru_w_x, m_lru_b_x, m_lru_lambda, m_attn_sinks, m_w_proj_lru, m_w_proj_attn, m_w_out, m_mix_post_g, m_ffn2_pre_g, m_ffn2_w_gu, m_ffn2_w_down, m_ffn2_post_g, v_ffn1_pre_g, v_ffn1_w_gu, v_ffn1_w_down, v_ffn1_post_g, v_mix_pre_g, v_w_in, v_conv_w, v_conv_b, v_lru_w_a, v_lru_b_a, v_lru_w_x, v_lru_b_x, v_lru_lambda, v_attn_sinks, v_w_proj_lru, v_w_proj_attn, v_w_out, v_mix_post_g, v_ffn2_pre_g, v_ffn2_w_gu, v_ffn2_w_down, v_ffn2_post_g):
    given = dict(x=x, ffn1_pre_g=ffn1_pre_g, ffn1_w_gu=ffn1_w_gu, ffn1_w_down=ffn1_w_down, ffn1_post_g=ffn1_post_g, mix_pre_g=mix_pre_g, w_in=w_in, conv_w=conv_w, conv_b=conv_b, lru_w_a=lru_w_a, lru_b_a=lru_b_a, lru_w_x=lru_w_x, lru_b_x=lru_b_x, lru_lambda=lru_lambda, attn_sinks=attn_sinks, w_proj_lru=w_proj_lru, w_proj_attn=w_proj_attn, w_out=w_out, mix_post_g=mix_post_g, ffn2_pre_g=ffn2_pre_g, ffn2_w_gu=ffn2_w_gu, ffn2_w_down=ffn2_w_down, ffn2_post_g=ffn2_post_g, loss_target=loss_target, m_ffn1_pre_g=m_ffn1_pre_g, m_ffn1_w_gu=m_ffn1_w_gu, m_ffn1_w_down=m_ffn1_w_down, m_ffn1_post_g=m_ffn1_post_g, m_mix_pre_g=m_mix_pre_g, m_w_in=m_w_in, m_conv_w=m_conv_w, m_conv_b=m_conv_b, m_lru_w_a=m_lru_w_a, m_lru_b_a=m_lru_b_a, m_lru_w_x=m_lru_w_x, m_lru_b_x=m_lru_b_x, m_lru_lambda=m_lru_lambda, m_attn_sinks=m_attn_sinks, m_w_proj_lru=m_w_proj_lru, m_w_proj_attn=m_w_proj_attn, m_w_out=m_w_out, m_mix_post_g=m_mix_post_g, m_ffn2_pre_g=m_ffn2_pre_g, m_ffn2_w_gu=m_ffn2_w_gu, m_ffn2_w_down=m_ffn2_w_down, m_ffn2_post_g=m_ffn2_post_g, v_ffn1_pre_g=v_ffn1_pre_g, v_ffn1_w_gu=v_ffn1_w_gu, v_ffn1_w_down=v_ffn1_w_down, v_ffn1_post_g=v_ffn1_post_g, v_mix_pre_g=v_mix_pre_g, v_w_in=v_w_in, v_conv_w=v_conv_w, v_conv_b=v_conv_b, v_lru_w_a=v_lru_w_a, v_lru_b_a=v_lru_b_a, v_lru_w_x=v_lru_w_x, v_lru_b_x=v_lru_b_x, v_lru_lambda=v_lru_lambda, v_attn_sinks=v_attn_sinks, v_w_proj_lru=v_w_proj_lru, v_w_proj_attn=v_w_proj_attn, v_w_out=v_w_out, v_mix_post_g=v_mix_post_g, v_ffn2_pre_g=v_ffn2_pre_g, v_ffn2_w_gu=v_ffn2_w_gu, v_ffn2_w_down=v_ffn2_w_down, v_ffn2_post_g=v_ffn2_post_g)
    weights = {n: given[n] for n in TWIN_WEIGHTS}
    shared = {n: given[n] for n in SHARED_INPUTS}
    per_example = {n: given[n] for n in ['x']}
    grad_fn = _jax.value_and_grad(_loss, argnums=(0, 1))

    def one_microbatch(ex, loss_target):
        ex = dict(ex)
        diff = ex.pop(TWIN_DIFF_INPUT)
        return grad_fn(weights, diff, {**shared, **ex}, loss_target)

    if N_MICROBATCH == 1:
        loss, (grad_w, grad_x) = one_microbatch(per_example, given["loss_target"])
    else:
        def body(carry, xs):
            loss_sum, grad_sum = carry
            l_k, (gw_k, gx_k) = one_microbatch(xs[0], xs[1])
            with _jax.named_scope("update"):
                return (loss_sum + l_k, _jax.tree.map(_jnp.add, grad_sum, gw_k)), gx_k

        init = (_jnp.zeros((), _jnp.float32), _jax.tree.map(_jnp.zeros_like, weights))
        (loss, grad_w), grad_x = _jax.lax.scan(body, init, (per_example, given["loss_target"]))
    with _jax.named_scope("update"):
        delta_w, new_m, new_v = {}, {}, {}
        for n in TWIN_WEIGHTS:
            delta_w[n], new_m[n], new_v[n] = _adamw(weights[n], grad_w[n], given["m_" + n], given["v_" + n])
    return (loss, grad_x, *[grad_w[n] for n in TWIN_WEIGHTS], *[delta_w[n] for n in TWIN_WEIGHTS],
            *[new_m[n] for n in TWIN_WEIGHTS], *[new_v[n] for n in TWIN_WEIGHTS])
```

```python
import jax
import jax.numpy as jnp
from jax import lax
from jax.experimental import pallas as pl
from jax.experimental.pallas import tpu as pltpu

F32 = jnp.float32
BF16 = jnp.bfloat16

SEQ = 2048
D_MODEL = 1024
D_FF = 2816
LRU_W = 1024
LRU_BLOCK_W = 64
HEAD_DIM = 64
N_Q_HEADS = 16
N_KV_HEADS = 4
KV_W = N_KV_HEADS * HEAD_DIM
ATTN_BLOCK = 128
N_ATTN_BLOCKS = SEQ // ATTN_BLOCK
IN_SEGS = (1024, 1024, 1024, 256, 256, 1024, 1024)
IN_W = sum(IN_SEGS)
NORM_EPS = 1e-6
MASK_VALUE = -1e30
ROPE_THETA = 10000.0
LRU_C = 8.0
MACARON = 0.5
ADAM_LR = 0.001
ADAM_B1 = 0.9
ADAM_B2 = 0.999
ADAM_EPS = 1e-08
ADAM_WD = 0.01
ADAM_STEP = 10

N_CHIPS = 4
N_DEV = 8
VMEM_LIMIT = 56 * 1024 * 1024
MESH = pl.DeviceIdType.MESH
ANY = pl.BlockSpec(memory_space=pl.ANY)

PACK = (('ffn1_w_gu', 1408, True), ('w_in', 1408, True), ('ffn2_w_gu', 1408, True),
        ('ffn1_w_down', 704, False), ('ffn2_w_down', 704, False),
        ('w_proj_lru', 256, False), ('w_proj_attn', 256, False), ('w_out', 256, False))
PACK_ROWS = sum(r for _, r, _ in PACK)
HALF_ROWS = PACK_ROWS // 2
PACK_OFF = {}
_o = 0
for _n, _r, _t in PACK:
    PACK_OFF[_n] = _o
    _o += _r
AG_CHUNKS = 4

SMALL_VECS = ('ffn1_pre_g', 'ffn1_post_g', 'mix_pre_g', 'conv_b', 'lru_b_a', 'lru_b_x', 'lru_lambda',
              'mix_post_g', 'ffn2_pre_g', 'ffn2_post_g')
SMALL_ROWS = 144
ROW_SINKS, ROW_CONV, ROW_WA, ROW_WX = 10, 11, 16, 80


def _dot(a, b):
    return jnp.dot(a, b, preferred_element_type=F32)


def _dot_nt(a, b):
    return lax.dot_general(a, b, (((1,), (1,)), ((), ())), preferred_element_type=F32)


def _dot_tn(a, b):
    return lax.dot_general(a, b, (((0,), (0,)), ((), ())), preferred_element_type=F32)


def _params(n_grid):
    return pltpu.CompilerParams(dimension_semantics=("arbitrary",) * n_grid, vmem_limit_bytes=VMEM_LIMIT)


def _sigmoid(x):
    return 1.0 / (1.0 + jnp.exp(-x))


def _rsqrt_mean_sq(x):
    return lax.rsqrt(jnp.mean(x * x, axis=-1, keepdims=True) + NORM_EPS)


def _expm1(x):
    poly = x * (1.0 + x * (0.5 + x * (1.0 / 6.0 + x * (1.0 / 24.0 + x * (1.0 / 120.0)))))
    return jnp.where(jnp.abs(x) < 0.1, poly, jnp.exp(x) - 1.0)


_GELU_K = 0.7978845608028654
_GELU_C = 0.044715


def _gelu(x):
    t = jnp.tanh(_GELU_K * (x + _GELU_C * x * x * x))
    return 0.5 * x * (1.0 + t), t


def _gelu_grad(x, t):
    return 0.5 * (1.0 + t) + 0.5 * x * (1.0 - t * t) * _GELU_K * (1.0 + 3.0 * _GELU_C * x * x)


def _load_weight(wf_ref, name, dst_ref, sem):
    rows = dst_ref.shape[0] // N_CHIPS
    off = PACK_OFF[name]
    cps = [pltpu.make_async_copy(wf_ref.at[q, pl.ds(off, rows)], dst_ref.at[pl.ds(q * rows, rows)], sem.at[q])
           for q in range(N_CHIPS)]
    for cp in cps:
        cp.start()
    for cp in cps:
        cp.wait()


def _weight_scratch(rows_total):
    return [pltpu.VMEM((rows_total, D_MODEL), BF16), pltpu.SemaphoreType.DMA((N_CHIPS,))]


_ROW = lambda tm: pl.BlockSpec((tm, D_MODEL), lambda i: (i, 0))
_VEC = pl.BlockSpec((1, D_MODEL), lambda i: (0, 0))


def ffn_fwd_a(x, g_pre, wf, wname, name):
    tm, tn = 256, 256

    def body(x_ref, gp_ref, wf_ref, n_ref, g_ref, u_ref, a_ref, wt_ref, sem):
        @pl.when(pl.program_id(0) == 0)
        def _():
            _load_weight(wf_ref, wname, wt_ref, sem)

        xv = x_ref[...]
        n = (xv * _rsqrt_mean_sq(xv) * gp_ref[...]).astype(BF16)
        n_ref[...] = n
        for j in range(D_FF // tn):
            g = _dot_nt(n, wt_ref[j * tn:(j + 1) * tn, :])
            u = _dot_nt(n, wt_ref[D_FF + j * tn:D_FF + (j + 1) * tn, :])
            g_ref[:, j * tn:(j + 1) * tn] = g.astype(BF16)
            u_ref[:, j * tn:(j + 1) * tn] = u.astype(BF16)
            a_ref[:, j * tn:(j + 1) * tn] = (g * _sigmoid(g) * u).astype(BF16)

    wide = pl.BlockSpec((tm, D_FF), lambda i: (i, 0))
    return pl.pallas_call(
        body, name=name, grid=(SEQ // tm,),
        in_specs=[_ROW(tm), _VEC, ANY],
        out_specs=[_ROW(tm), wide, wide, wide],
        out_shape=[jax.ShapeDtypeStruct((SEQ, D_MODEL), BF16)] + [jax.ShapeDtypeStruct((SEQ, D_FF), BF16)] * 3,
        scratch_shapes=_weight_scratch(2 * D_FF),
        compiler_params=_params(1),
    )(x, g_pre, wf)


def ffn_fwd_b(a, wf, wname, g_post, h_in, name, target=None):
    tm = 256
    final = target is not None

    def body(*refs):
        if final:
            a_ref, wf_ref, gp_ref, h_ref, t_ref, f_ref, o_ref, loss_ref, wd_ref, sem = refs
        else:
            a_ref, wf_ref, gp_ref, h_ref, f_ref, o_ref, wd_ref, sem = refs

        @pl.when(pl.program_id(0) == 0)
        def _():
            _load_weight(wf_ref, wname, wd_ref, sem)
            if final:
                loss_ref[...] = jnp.zeros_like(loss_ref)

        f = _dot(a_ref[...], wd_ref[...])
        f_ref[...] = f
        y = h_ref[...] + MACARON * (f * _rsqrt_mean_sq(f) * gp_ref[...])
        if final:
            err = y - t_ref[...]
            o_ref[...] = err * (1.0 / D_MODEL)
            loss_ref[...] += 0.5 * jnp.sum(err * err) * (1.0 / D_MODEL)
        else:
            o_ref[...] = y

    row = _ROW(tm)
    in_specs = [pl.BlockSpec((tm, D_FF), lambda i: (i, 0)), ANY, _VEC, row]
    out_specs = [row, row]
    out_shape = [jax.ShapeDtypeStruct((SEQ, D_MODEL), F32)] * 2
    args = [a, wf, g_post, h_in]
    if final:
        in_specs.append(row)
        args.append(target)
        out_specs.append(pl.BlockSpec((8, 128), lambda i: (0, 0)))
        out_shape.append(jax.ShapeDtypeStruct((8, 128), F32))
    return pl.pallas_call(body, name=name, grid=(SEQ // tm,), in_specs=in_specs, out_specs=out_specs,
                          out_shape=out_shape, scratch_shapes=_weight_scratch(D_FF),
                          compiler_params=_params(1))(*args)


def ffn_bwd_a(d_out, f, g_post, wf, wname, g, u, name):
    tm = 256

    def body(do_ref, f_ref, gp_ref, wf_ref, g_ref, u_ref, df_ref, dgu_ref, dgp_ref, wd_ref, sem):
        @pl.when(pl.program_id(0) == 0)
        def _():
            _load_weight(wf_ref, wname, wd_ref, sem)
            dgp_ref[...] = jnp.zeros_like(dgp_ref)

        fv = f_ref[...]
        rf = _rsqrt_mean_sq(fv)
        fh = fv * rf
        dn = MACARON * do_ref[...]
        dgp_ref[...] += jnp.sum(dn * fh, axis=0, keepdims=True)
        t = dn * gp_ref[...]
        df = (rf * (t - fh * jnp.mean(t * fh, axis=-1, keepdims=True))).astype(BF16)
        df_ref[...] = df
        da = _dot_nt(df, wd_ref[...])
        gv = g_ref[...].astype(F32)
        uv = u_ref[...].astype(F32)
        s = _sigmoid(gv)
        dgu_ref[:, :D_FF] = (da * uv * s * (1.0 + gv * (1.0 - s))).astype(BF16)
        dgu_ref[:, D_FF:] = (da * gv * s).astype(BF16)

    row = _ROW(tm)
    wide = pl.BlockSpec((tm, D_FF), lambda i: (i, 0))
    return pl.pallas_call(
        body, name=name, grid=(SEQ // tm,),
        in_specs=[row, row, _VEC, ANY, wide, wide],
        out_specs=[row, pl.BlockSpec((tm, 2 * D_FF), lambda i: (i, 0)), _VEC],
        out_shape=[jax.ShapeDtypeStruct((SEQ, D_MODEL), BF16), jax.ShapeDtypeStruct((SEQ, 2 * D_FF), BF16),
                   jax.ShapeDtypeStruct((1, D_MODEL), F32)],
        scratch_shapes=_weight_scratch(D_FF),
        compiler_params=_params(1),
    )(d_out, f, g_post, wf, g, u)


def norm_bwd(pieces, wf, wname, x, g_pre, d_res, name):
    tm = 256
    widths = [p.shape[1] for p in pieces]
    offs = [sum(widths[:k]) for k in range(len(widths))]
    n_p = len(pieces)

    def body(*refs):
        p_refs = refs[:n_p]
        wf_ref, x_ref, g_ref, r_ref, dx_ref, dg_ref, wt_ref, sem = refs[n_p:]

        @pl.when(pl.program_id(0) == 0)
        def _():
            _load_weight(wf_ref, wname, wt_ref, sem)
            dg_ref[...] = jnp.zeros_like(dg_ref)

        dn = None
        for p_ref, lo, wd in zip(p_refs, offs, widths):
            part = _dot(p_ref[...], wt_ref[lo:lo + wd, :])
            dn = part if dn is None else dn + part
        xv = x_ref[...]
        r = _rsqrt_mean_sq(xv)
        xh = xv * r
        dg_ref[...] += jnp.sum(dn * xh, axis=0, keepdims=True)
        t = dn * g_ref[...]
        dx_ref[...] = r_ref[...] + r * (t - xh * jnp.mean(t * xh, axis=-1, keepdims=True))

    row = _ROW(tm)
    return pl.pallas_call(
        body, name=name, grid=(SEQ // tm,),
        in_specs=[pl.BlockSpec((tm, wd), lambda i: (i, 0)) for wd in widths] + [ANY, row, _VEC, row],
        out_specs=[row, _VEC],
        out_shape=[jax.ShapeDtypeStruct((SEQ, D_MODEL), F32), jax.ShapeDtypeStruct((1, D_MODEL), F32)],
        scratch_shapes=_weight_scratch(sum(widths)),
        compiler_params=_params(1),
    )(*pieces, wf, x, g_pre, d_res)


def mm_tn(pieces, b, tm, name):
    widths = [p.shape[1] for p in pieces]
    m_total = sum(widths)
    n_p = len(pieces)
    starts = [sum(widths[:k]) // tm for k in range(n_p)]
    counts = [wd // tm for wd in widths]

    def body(*refs):
        p_refs = refs[:n_p]
        b_ref, o_ref = refs[n_p:]
        i = pl.program_id(0)
        for p_ref, st, ct in zip(p_refs, starts, counts):
            @pl.when((i >= st) & (i < st + ct))
            def _(p_ref=p_ref):
                o_ref[...] = _dot_tn(p_ref[...], b_ref[...]).astype(BF16)

    def piece_spec(st, ct):
        return pl.BlockSpec((SEQ, tm), lambda i: (0, jnp.clip(i - st, 0, ct - 1)))

    return pl.pallas_call(
        body, name=name, grid=(m_total // tm,),
        in_specs=[piece_spec(st, ct) for st, ct in zip(starts, counts)] + [pl.BlockSpec((SEQ, D_MODEL), lambda i: (0, 0))],
        out_specs=pl.BlockSpec((tm, D_MODEL), lambda i: (i, 0)),
        out_shape=jax.ShapeDtypeStruct((m_total, D_MODEL), BF16),
        compiler_params=_params(1),
    )(*pieces, b)


def mix_in(h, g_pre, wf, name):
    tm = 256
    offs = [sum(IN_SEGS[:k]) for k in range(len(IN_SEGS))]
    dts = [F32, F32, F32, F32, BF16, F32, F32]
    n_o = len(IN_SEGS)

    def body(*refs):
        h_ref, g_ref, wf_ref, um_ref = refs[:4]
        o_refs = refs[4:4 + n_o]
        wt_ref, sem = refs[4 + n_o:]

        @pl.when(pl.program_id(0) == 0)
        def _():
            _load_weight(wf_ref, 'w_in', wt_ref, sem)

        hv = h_ref[...]
        um = (hv * _rsqrt_mean_sq(hv) * g_ref[...]).astype(BF16)
        um_ref[...] = um
        for o_ref, lo, wd in zip(o_refs, offs, IN_SEGS):
            for c0 in range(0, wd, 256):
                o_ref[:, c0:c0 + 256] = _dot_nt(um, wt_ref[lo + c0:lo + c0 + 256, :]).astype(o_ref.dtype)

    return pl.pallas_call(
        body, name=name, grid=(SEQ // tm,),
        in_specs=[_ROW(tm), _VEC, ANY],
        out_specs=[_ROW(tm)] + [pl.BlockSpec((tm, wd), lambda i: (i, 0)) for wd in IN_SEGS],
        out_shape=[jax.ShapeDtypeStruct((SEQ, D_MODEL), BF16)]
        + [jax.ShapeDtypeStruct((SEQ, wd), dt) for wd, dt in zip(IN_SEGS, dts)],
        scratch_shapes=_weight_scratch(IN_W),
        compiler_params=_params(1),
    )(h, g_pre, wf)


LRU_TC = 256


def _conv_fwd(xb, cw, cb, tt):
    xc = xb * cw[3:4, :] + cb
    shifted = []
    for s in (1, 2, 3):
        sh = jnp.where(tt >= s, pltpu.roll(xb, s, 0), 0.0)
        shifted.append(sh)
        xc = xc + sh * cw[3 - s:4 - s, :]
    return xc, shifted


def _lru_gates(xc, wa, ba, wx, bx, lam):
    xcb = xc.astype(BF16)
    r = _sigmoid(_dot(xcb, wa) + ba)
    i = _sigmoid(_dot(xcb, wx) + bx)
    nl = -lam
    sp = jnp.maximum(nl, 0.0) + jnp.log1p(jnp.exp(-jnp.abs(nl)))
    la = (-LRU_C * r) * sp
    a = jnp.exp(la)
    mult = jnp.sqrt(jnp.maximum(-_expm1(2.0 * la), 0.0))
    return xcb, r, i, sp, a, mult


def _scan(a, b, tt, reverse):
    n = a.shape[0]
    s = 1
    while s < n:
        if reverse:
            keep = tt < n - s
            shift = n - s
        else:
            keep = tt >= s
            shift = s
        b = a * jnp.where(keep, pltpu.roll(b, shift, 0), 0.0) + b
        if 2 * s < n:
            a = a * jnp.where(keep, pltpu.roll(a, shift, 0), 1.0)
        s *= 2
    return b


def _lru_specs():
    col = pl.BlockSpec((SEQ, LRU_TC), lambda j: (0, j))
    vec = pl.BlockSpec((1, LRU_TC), lambda j: (0, j))
    bd = pl.BlockSpec((1, LRU_TC, LRU_TC), lambda j: (j, 0, 0))
    cw = pl.BlockSpec((4, LRU_TC), lambda j: (0, j))
    return col, vec, bd, cw


def lru_fwd(gate, xbr, conv_w, conv_b, wa_bd, b_a, wx_bd, b_x, lam, name):
    col, vec, bd, cw = _lru_specs()

    def body(gate_ref, xbr_ref, cw_ref, cb_ref, wa_ref, ba_ref, wx_ref, bx_ref, lam_ref, y_ref, h_ref):
        tt = lax.broadcasted_iota(jnp.int32, (SEQ, LRU_TC), 0)
        xc, _ = _conv_fwd(xbr_ref[...], cw_ref[...], cb_ref[...], tt)
        _, r, i, sp, a, mult = _lru_gates(xc, wa_ref[0], ba_ref[...], wx_ref[0], bx_ref[...], lam_ref[...])
        h = _scan(a, mult * (i * xc), tt, reverse=False)
        h_ref[...] = h
        gl, _ = _gelu(gate_ref[...])
        y_ref[...] = (h * gl).astype(BF16)

    return pl.pallas_call(
        body, name=name, grid=(LRU_W // LRU_TC,),
        in_specs=[col, col, cw, vec, bd, vec, bd, vec, vec],
        out_specs=[col, col],
        out_shape=[jax.ShapeDtypeStruct((SEQ, LRU_W), BF16), jax.ShapeDtypeStruct((SEQ, LRU_W), F32)],
        compiler_params=_params(1),
    )(gate, xbr, conv_w, conv_b, wa_bd, b_a, wx_bd, b_x, lam)


def lru_bwd(gate, xbr, h, dy, conv_w, conv_b, wa_bd, b_a, wx_bd, b_x, lam, name):
    col, vec, bd, cw = _lru_specs()

    def body(gate_ref, xbr_ref, h_ref, dy_ref, cw_ref, cb_ref, wa_ref, ba_ref, wx_ref, bx_ref, lam_ref,
             dgate_ref, dxbr_ref, vecs_ref, dwa_ref, dwx_ref):
        tt = lax.broadcasted_iota(jnp.int32, (SEQ, LRU_TC), 0)
        cwv = cw_ref[...]
        lam = lam_ref[...]
        xb = xbr_ref[...]
        xc, shifted = _conv_fwd(xb, cwv, cb_ref[...], tt)
        wa = wa_ref[0]
        wx = wx_ref[0]
        xcb, r, i, sp, a, mult = _lru_gates(xc, wa, ba_ref[...], wx, bx_ref[...], lam)
        hv = h_ref[...]
        dyv = dy_ref[...]
        gv = gate_ref[...]
        gl, th = _gelu(gv)
        dgate_ref[...] = (dyv * hv * _gelu_grad(gv, th)).astype(BF16)
        a_next = jnp.where(tt < SEQ - 1, pltpu.roll(a, SEQ - 1, 0), 0.0)
        gsum = _scan(a_next, dyv * gl, tt, reverse=True)
        h_prev = jnp.where(tt >= 1, pltpu.roll(hv, 1, 0), 0.0)
        d_mult = gsum * i * xc
        d_i = gsum * mult * xc
        d_xc = gsum * mult * i
        d_la = gsum * h_prev * a - d_mult * (a * a) / mult
        d_pr = (d_la * (-LRU_C * sp)) * r * (1.0 - r)
        d_pi = d_i * i * (1.0 - i)
        d_lam = jnp.sum(d_la * r, axis=0, keepdims=True) * (LRU_C * _sigmoid(-lam))
        d_prb = d_pr.astype(BF16)
        d_pib = d_pi.astype(BF16)
        d_xc = d_xc + _dot_nt(d_prb, wa) + _dot_nt(d_pib, wx)
        dwa_ref[0] = _dot_tn(xcb, d_prb)
        dwx_ref[0] = _dot_tn(xcb, d_pib)
        rows = [jnp.sum(d_xc * shifted[2], axis=0, keepdims=True),
                jnp.sum(d_xc * shifted[1], axis=0, keepdims=True),
                jnp.sum(d_xc * shifted[0], axis=0, keepdims=True),
                jnp.sum(d_xc * xb, axis=0, keepdims=True),
                jnp.sum(d_xc, axis=0, keepdims=True),
                jnp.sum(d_pr, axis=0, keepdims=True),
                jnp.sum(d_pi, axis=0, keepdims=True),
                d_lam]
        ri = lax.broadcasted_iota(jnp.int32, (8, LRU_TC), 0)
        acc = jnp.zeros((8, LRU_TC), F32)
        for k, rv in enumerate(rows):
            acc = jnp.where(ri == k, rv, acc)
        vecs_ref[...] = acc
        d_xb = d_xc * cwv[3:4, :]
        for s in (1, 2, 3):
            d_xb = d_xb + jnp.where(tt < SEQ - s, pltpu.roll(d_xc, SEQ - s, 0), 0.0) * cwv[3 - s:4 - s, :]
        dxbr_ref[...] = d_xb.astype(BF16)

    return pl.pallas_call(
        body, name=name, grid=(LRU_W // LRU_TC,),
        in_specs=[col, col, col, col, cw, vec, bd, vec, bd, vec, vec],
        out_specs=[col, col, pl.BlockSpec((8, LRU_TC), lambda j: (0, j)), bd, bd],
        out_shape=[jax.ShapeDtypeStruct((SEQ, LRU_W), BF16), jax.ShapeDtypeStruct((SEQ, LRU_W), BF16),
                   jax.ShapeDtypeStruct((8, LRU_W), F32),
                   jax.ShapeDtypeStruct((LRU_W // LRU_TC, LRU_TC, LRU_TC), F32),
                   jax.ShapeDtypeStruct((LRU_W // LRU_TC, LRU_TC, LRU_TC), F32)],
        compiler_params=_params(1),
    )(gate, xbr, h, dy, conv_w, conv_b, wa_bd, b_a, wx_bd, b_x, lam)


def _rope(x, cos, sin_signed):
    w = x.shape[1]
    reps = w // 128
    if reps > 1:
        cos = jnp.tile(cos, (1, reps))
        sin_signed = jnp.tile(sin_signed, (1, reps))
    lane = lax.broadcasted_iota(jnp.int32, x.shape, 1)
    first = (lane & 63) < 32
    partner = jnp.where(first, pltpu.roll(x, w - 32, 1), pltpu.roll(x, 32, 1))
    return x * cos + partner * sin_signed


def _both_halves(t, odd):
    lo = lax.broadcasted_iota(jnp.int32, t.shape, 1) < 64
    rolled = pltpu.roll(t, 64, 1)
    return jnp.where(lo, rolled, t) if odd else jnp.where(lo, t, rolled)


def _stack_heads(ta, tb):
    lo = lax.broadcasted_iota(jnp.int32, ta.shape, 1) < 64
    return jnp.concatenate([jnp.where(lo, ta, 0.0), jnp.where(lo, 0.0, ta),
                            jnp.where(lo, tb, 0.0), jnp.where(lo, 0.0, tb)], axis=0)


def _unstack_heads(o):
    lo = lax.broadcasted_iota(jnp.int32, (ATTN_BLOCK, 128), 1) < 64
    return (jnp.where(lo, o[0:128], o[128:256]), jnp.where(lo, o[256:384], o[384:512]))


def _attn_probs(qs, kd, sinks_ref, hk, first_block):
    s = _dot_nt(qs, kd) * (HEAD_DIM ** -0.5)
    row = lax.broadcasted_iota(jnp.int32, s.shape, 0)
    si = lax.broadcasted_iota(jnp.int32, s.shape, 1)
    diff = ATTN_BLOCK + (row & (ATTN_BLOCK - 1)) - si
    valid = (diff >= 0) & (diff < ATTN_BLOCK) & ((si >= ATTN_BLOCK) | jnp.logical_not(first_block))
    s = jnp.where(valid, s, MASK_VALUE)
    rg = lax.broadcasted_iota(jnp.int32, (4 * ATTN_BLOCK, 1), 0) >> 7
    sink = jnp.where(rg == 0, sinks_ref[4 * hk],
                     jnp.where(rg == 1, sinks_ref[4 * hk + 1],
                               jnp.where(rg == 2, sinks_ref[4 * hk + 2], sinks_ref[4 * hk + 3])))
    m = jnp.maximum(jnp.max(s, axis=1, keepdims=True), sink)
    e = jnp.exp(s - m)
    es = jnp.exp(sink - m)
    inv = 1.0 / (jnp.sum(e, axis=1, keepdims=True) + es)
    return e * inv, es * inv


def _prev(i):
    return jnp.maximum(i - 1, 0)


def attn_fwd(q, k, v, cos, sin_signed, sinks, name):
    nb = ATTN_BLOCK

    def body(q_ref, kc_ref, kp_ref, vc_ref, vp_ref, cc_ref, sc_ref, cp_ref, sp_ref, sinks_ref,
             qr_ref, kr_ref, y_ref):
        first_block = pl.program_id(0) == 0
        qr = _rope(q_ref[...], cc_ref[...], sc_ref[...])
        kc = _rope(kc_ref[...], cc_ref[...], sc_ref[...])
        kp = _rope(kp_ref[...], cp_ref[...], sp_ref[...])
        qr_ref[...] = qr.astype(BF16)
        kr_ref[...] = kc.astype(BF16)
        k2 = jnp.concatenate([kp, kc], axis=0)
        v2 = jnp.concatenate([vp_ref[...].astype(F32), vc_ref[...].astype(F32)], axis=0)
        for hk in range(N_KV_HEADS):
            kt = hk // 2
            kd = _both_halves(k2[:, kt * 128:(kt + 1) * 128], hk % 2).astype(BF16)
            vd = _both_halves(v2[:, kt * 128:(kt + 1) * 128], hk % 2).astype(BF16)
            qs = _stack_heads(qr[:, (2 * hk) * 128:(2 * hk + 1) * 128],
                              qr[:, (2 * hk + 1) * 128:(2 * hk + 2) * 128]).astype(BF16)
            p, _ = _attn_probs(qs, kd, sinks_ref, hk, first_block)
            ta, tb = _unstack_heads(_dot(p.astype(BF16), vd))
            y_ref[:, (2 * hk) * 128:(2 * hk + 1) * 128] = ta.astype(BF16)
            y_ref[:, (2 * hk + 1) * 128:(2 * hk + 2) * 128] = tb.astype(BF16)

    cur = lambda w: pl.BlockSpec((nb, w), lambda i: (i, 0))
    prv = lambda w: pl.BlockSpec((nb, w), lambda i: (_prev(i), 0))
    return pl.pallas_call(
        body, name=name, grid=(N_ATTN_BLOCKS,),
        in_specs=[cur(D_MODEL), cur(KV_W), prv(KV_W), cur(KV_W), prv(KV_W), cur(128), cur(128), prv(128), prv(128),
                  pl.BlockSpec(memory_space=pltpu.SMEM)],
        out_specs=[cur(D_MODEL), cur(KV_W), cur(D_MODEL)],
        out_shape=[jax.ShapeDtypeStruct((SEQ, D_MODEL), BF16), jax.ShapeDtypeStruct((SEQ, KV_W), BF16),
                   jax.ShapeDtypeStruct((SEQ, D_MODEL), BF16)],
        compiler_params=_params(1),
    )(q, k, k, v, v, cos, sin_signed, cos, sin_signed, sinks)


def attn_bwd(qr, kr, v, dy, cos, sin_signed, sinks, name):
    nb = ATTN_BLOCK
    n_steps = N_ATTN_BLOCKS + 1
    scale = HEAD_DIM ** -0.5

    def body(q_ref, kc_ref, kp_ref, vc_ref, vp_ref, dy_ref, cc_ref, sc_ref, cp_ref, sp_ref, sinks_ref,
             dq_ref, dk_ref, dv_ref, dsk_ref, ck_ref, cv_ref):
        i = pl.program_id(0)

        @pl.when(i == 0)
        def _():
            dsk_ref[...] = jnp.zeros_like(dsk_ref)
            ck_ref[...] = jnp.zeros_like(ck_ref)
            cv_ref[...] = jnp.zeros_like(cv_ref)

        @pl.when(i < N_ATTN_BLOCKS)
        def _():
            qv = q_ref[...].astype(F32)
            dov = dy_ref[...].astype(F32)
            k2 = jnp.concatenate([kp_ref[...].astype(F32), kc_ref[...].astype(F32)], axis=0)
            v2 = jnp.concatenate([vp_ref[...].astype(F32), vc_ref[...].astype(F32)], axis=0)
            lane = lax.broadcasted_iota(jnp.int32, (8, 128), 1)
            lo = lax.broadcasted_iota(jnp.int32, (2 * nb, 128), 1) < 64
            dsk = jnp.zeros((8, 128), F32)
            dk_tiles = []
            dv_tiles = []
            for hk in range(N_KV_HEADS):
                kt = hk // 2
                kd = _both_halves(k2[:, kt * 128:(kt + 1) * 128], hk % 2).astype(BF16)
                vd = _both_halves(v2[:, kt * 128:(kt + 1) * 128], hk % 2).astype(BF16)
                qs = _stack_heads(qv[:, (2 * hk) * 128:(2 * hk + 1) * 128],
                                  qv[:, (2 * hk + 1) * 128:(2 * hk + 2) * 128]).astype(BF16)
                dos = _stack_heads(dov[:, (2 * hk) * 128:(2 * hk + 1) * 128],
                                   dov[:, (2 * hk + 1) * 128:(2 * hk + 2) * 128]).astype(BF16)
                p, ps = _attn_probs(qs, kd, sinks_ref, hk, i == 0)
                dp = _dot_nt(dos, vd)
                delta = jnp.sum(p * dp, axis=1, keepdims=True)
                ds = (p * (dp - delta)).astype(BF16)
                dsink = -ps * delta
                for g in range(4):
                    dsk = dsk + jnp.where(lane == 4 * hk + g, jnp.sum(dsink[g * nb:(g + 1) * nb]), 0.0)
                ta, tb = _unstack_heads(_dot(ds, kd) * scale)
                dq_a = (2 * hk) * 128
                dq_ref[:, dq_a:dq_a + 128] = _rope(ta, cc_ref[...], -sc_ref[...]).astype(BF16)
                dq_ref[:, dq_a + 128:dq_a + 256] = _rope(tb, cc_ref[...], -sc_ref[...]).astype(BF16)
                rk = _dot_tn(ds, qs) * scale
                rv = _dot_tn(p.astype(BF16), dos)
                dk_tiles.append(rk + pltpu.roll(rk, 64, 1))
                dv_tiles.append(rv + pltpu.roll(rv, 64, 1))
            dsk_ref[...] += dsk
            dk_full = jnp.concatenate([jnp.where(lo, dk_tiles[0], dk_tiles[1]),
                                       jnp.where(lo, dk_tiles[2], dk_tiles[3])], axis=1)
            dv_full = jnp.concatenate([jnp.where(lo, dv_tiles[0], dv_tiles[1]),
                                       jnp.where(lo, dv_tiles[2], dv_tiles[3])], axis=1)
            dk_ref[...] = _rope(ck_ref[...] + dk_full[0:nb], cp_ref[...], -sp_ref[...]).astype(BF16)
            dv_ref[...] = (cv_ref[...] + dv_full[0:nb]).astype(BF16)
            ck_ref[...] = dk_full[nb:2 * nb]
            cv_ref[...] = dv_full[nb:2 * nb]

        @pl.when(i == N_ATTN_BLOCKS)
        def _():
            dk_ref[...] = _rope(ck_ref[...], cp_ref[...], -sp_ref[...]).astype(BF16)
            dv_ref[...] = cv_ref[...].astype(BF16)

    qi = lambda i: jnp.minimum(i, N_ATTN_BLOCKS - 1)
    cur = lambda w: pl.BlockSpec((nb, w), lambda i: (qi(i), 0))
    prv = lambda w: pl.BlockSpec((nb, w), lambda i: (_prev(qi(i)), 0))
    out_prev = lambda w: pl.BlockSpec((nb, w), lambda i: (_prev(i), 0))
    return pl.pallas_call(
        body, name=name, grid=(n_steps,),
        in_specs=[cur(D_MODEL), cur(KV_W), prv(KV_W), cur(KV_W), prv(KV_W), cur(D_MODEL),
                  cur(128), cur(128), out_prev(128), out_prev(128), pl.BlockSpec(memory_space=pltpu.SMEM)],
        out_specs=[cur(D_MODEL), out_prev(KV_W), out_prev(KV_W), pl.BlockSpec((8, 128), lambda i: (0, 0))],
        out_shape=[jax.ShapeDtypeStruct((SEQ, D_MODEL), BF16), jax.ShapeDtypeStruct((SEQ, KV_W), BF16),
                   jax.ShapeDtypeStruct((SEQ, KV_W), BF16), jax.ShapeDtypeStruct((8, 128), F32)],
        scratch_shapes=[pltpu.VMEM((nb, KV_W), F32), pltpu.VMEM((nb, KV_W), F32)],
        compiler_params=_params(1),
    )(qr, kr, kr, v, v, dy, cos, sin_signed, cos, sin_signed, sinks)


def _proj_scratch():
    return [pltpu.VMEM((D_MODEL, D_MODEL), BF16)] * 3 + [pltpu.SemaphoreType.DMA((3 * N_CHIPS,))]


def _load_projs(wf_ref, wl_ref, wa_ref, wo_ref, sem):
    _load_weight(wf_ref, 'w_proj_lru', wl_ref, sem.at[pl.ds(0, N_CHIPS)])
    _load_weight(wf_ref, 'w_proj_attn', wa_ref, sem.at[pl.ds(N_CHIPS, N_CHIPS)])
    _load_weight(wf_ref, 'w_out', wo_ref, sem.at[pl.ds(2 * N_CHIPS, N_CHIPS)])


def merge_fwd(y_lru, y_attn, g_lru, g_attn, wf, g_post, h_in, name):
    tm = 256

    def body(yl_ref, ya_ref, gl_ref, ga_ref, wf_ref, gp_ref, h_ref,
             pl_ref, pa_ref, mg_ref, m_ref, o_ref, wl_ref, wa_ref, wo_ref, sem):
        @pl.when(pl.program_id(0) == 0)
        def _():
            _load_projs(wf_ref, wl_ref, wa_ref, wo_ref, sem)

        p_l = _dot(yl_ref[...], wl_ref[...])
        p_a = _dot(ya_ref[...], wa_ref[...])
        pl_ref[...] = p_l.astype(BF16)
        pa_ref[...] = p_a.astype(BF16)
        merged = (_sigmoid(gl_ref[...]) * p_l + _sigmoid(ga_ref[...]) * p_a).astype(BF16)
        mg_ref[...] = merged
        m = _dot(merged, wo_ref[...])
        m_ref[...] = m
        o_ref[...] = h_ref[...] + m * _rsqrt_mean_sq(m) * gp_ref[...]

    row = _ROW(tm)
    return pl.pallas_call(
        body, name=name, grid=(SEQ // tm,),
        in_specs=[row, row, row, row, ANY, _VEC, row],
        out_specs=[row] * 5,
        out_shape=[jax.ShapeDtypeStruct((SEQ, D_MODEL), BF16)] * 3 + [jax.ShapeDtypeStruct((SEQ, D_MODEL), F32)] * 2,
        scratch_shapes=_proj_scratch(),
        compiler_params=_params(1),
    )(y_lru, y_attn, g_lru, g_attn, wf, g_post, h_in)


def merge_bwd(d_out, m, g_post, wf, g_lru, g_attn, p_l, p_a, name):
    tm = 256

    def body(do_ref, m_ref, gp_ref, wf_ref, gl_ref, ga_ref, pl_ref, pa_ref,
             dm_ref, dpl_ref, dpa_ref, dgl_ref, dga_ref, dya_ref, dyl_ref, dgp_ref, wl_ref, wa_ref, wo_ref, sem):
        @pl.when(pl.program_id(0) == 0)
        def _():
            _load_projs(wf_ref, wl_ref, wa_ref, wo_ref, sem)
            dgp_ref[...] = jnp.zeros_like(dgp_ref)

        mv = m_ref[...]
        rm = _rsqrt_mean_sq(mv)
        mh = mv * rm
        dn = do_ref[...]
        dgp_ref[...] += jnp.sum(dn * mh, axis=0, keepdims=True)
        t = dn * gp_ref[...]
        dm = (rm * (t - mh * jnp.mean(t * mh, axis=-1, keepdims=True))).astype(BF16)
        dm_ref[...] = dm
        dmg = _dot_nt(dm, wo_ref[...])
        sl = _sigmoid(gl_ref[...])
        sa = _sigmoid(ga_ref[...])
        dpl = (dmg * sl).astype(BF16)
        dpa = (dmg * sa).astype(BF16)
        dpl_ref[...] = dpl
        dpa_ref[...] = dpa
        dgl_ref[...] = (dmg * pl_ref[...].astype(F32) * sl * (1.0 - sl)).astype(BF16)
        dga_ref[...] = (dmg * pa_ref[...].astype(F32) * sa * (1.0 - sa)).astype(BF16)
        dyl_ref[...] = _dot_nt(dpl, wl_ref[...])
        dya_ref[...] = _dot_nt(dpa, wa_ref[...]).astype(BF16)

    row = _ROW(tm)
    return pl.pallas_call(
        body, name=name, grid=(SEQ // tm,),
        in_specs=[row, row, _VEC, ANY, row, row, row, row],
        out_specs=[row] * 7 + [_VEC],
        out_shape=[jax.ShapeDtypeStruct((SEQ, D_MODEL), BF16)] * 6 + [jax.ShapeDtypeStruct((SEQ, D_MODEL), F32),
                                                                       jax.ShapeDtypeStruct((1, D_MODEL), F32)],
        scratch_shapes=_proj_scratch(),
        compiler_params=_params(1),
    )(d_out, m, g_post, wf, g_lru, g_attn, p_l, p_a)


def _rope_tables():
    half = HEAD_DIM // 2
    inv_freq = ROPE_THETA ** (-jnp.arange(half, dtype=F32) / half)
    ang = jnp.arange(SEQ, dtype=F32)[:, None] * inv_freq[None, :]
    cos, sin = jnp.cos(ang), jnp.sin(ang)
    return jnp.tile(jnp.concatenate([cos, cos], axis=1), (1, 2)), jnp.tile(jnp.concatenate([-sin, sin], axis=1), (1, 2))


def _block_diag(w):
    per = LRU_TC // LRU_BLOCK_W
    w4 = w.reshape(LRU_W // LRU_TC, per, LRU_BLOCK_W, LRU_BLOCK_W)
    eye = jnp.eye(per, dtype=w.dtype)
    return jnp.einsum('jacd,ab->jacbd', w4, eye).reshape(LRU_W // LRU_TC, LRU_TC, LRU_TC).astype(BF16)


def _diag_blocks(p):
    per = LRU_TC // LRU_BLOCK_W
    p5 = p.reshape(LRU_W // LRU_TC, per, LRU_BLOCK_W, per, LRU_BLOCK_W)
    return jnp.stack([p5[:, a, :, a, :] for a in range(per)], axis=1).reshape(LRU_W // LRU_BLOCK_W, LRU_BLOCK_W, LRU_BLOCK_W)


def local_step(x, target, sw, wf):
    cos, sin_signed = _rope_tables()
    wa_bd = _block_diag(sw['lru_w_a'])
    wx_bd = _block_diag(sw['lru_w_x'])
    sinks = sw['attn_sinks'].reshape(N_Q_HEADS)

    n1, g1, u1, a1 = ffn_fwd_a(x, sw['ffn1_pre_g'], wf, 'ffn1_w_gu', 'ffn1_fwd_a')
    f1, h1 = ffn_fwd_b(a1, wf, 'ffn1_w_down', sw['ffn1_post_g'], x, 'ffn1_fwd_b')
    um, gate, xbr, q, k, v, g_lru, g_attn = mix_in(h1, sw['mix_pre_g'], wf, 'mix_in')
    y_lru, h_lru = lru_fwd(gate, xbr, sw['conv_w'], sw['conv_b'], wa_bd, sw['lru_b_a'], wx_bd, sw['lru_b_x'],
                           sw['lru_lambda'], 'lru_fwd')
    qr, kr, y_attn = attn_fwd(q, k, v, cos, sin_signed, sinks, 'attn_fwd')
    p_l, p_a, merged, m, h2 = merge_fwd(y_lru, y_attn, g_lru, g_attn, wf, sw['mix_post_g'], h1, 'merge_fwd')
    n2, g2, u2, a2 = ffn_fwd_a(h2, sw['ffn2_pre_g'], wf, 'ffn2_w_gu', 'ffn2_fwd_a')
    f2, dy, loss_blk = ffn_fwd_b(a2, wf, 'ffn2_w_down', sw['ffn2_post_g'], h2, 'ffn2_fwd_b', target=target)

    gs, gb = {}, {}
    df2, dgu2, gs['ffn2_post_g'] = ffn_bwd_a(dy, f2, sw['ffn2_post_g'], wf, 'ffn2_w_down', g2, u2, 'ffn2_bwd_a')
    gb['ffn2_w_down'] = mm_tn([a2], df2, 256, 'ffn2_dw_down')
    gb['ffn2_w_gu'] = mm_tn([dgu2], n2, 512, 'ffn2_dw_gu')
    dh2, gs['ffn2_pre_g'] = norm_bwd([dgu2], wf, 'ffn2_w_gu', h2, sw['ffn2_pre_g'], dy, 'ffn2_bwd_b')

    dm, dpl, dpa, dgl, dga, dya, dyl, gs['mix_post_g'] = merge_bwd(
        dh2, m, sw['mix_post_g'], wf, g_lru, g_attn, p_l, p_a, 'merge_bwd')
    gb['w_out'] = mm_tn([merged], dm, 512, 'dw_out')
    gb['w_proj_lru'] = mm_tn([y_lru], dpl, 512, 'dw_proj_lru')
    gb['w_proj_attn'] = mm_tn([y_attn], dpa, 512, 'dw_proj_attn')
    dq, dk, dv, dsk = attn_bwd(qr, kr, v, dya, cos, sin_signed, sinks, 'attn_bwd')
    gs['attn_sinks'] = dsk[0:1, 0:N_Q_HEADS]
    dgate, dxbr, vecs, dwa, dwx = lru_bwd(gate, xbr, h_lru, dyl, sw['conv_w'], sw['conv_b'], wa_bd, sw['lru_b_a'],
                                           wx_bd, sw['lru_b_x'], sw['lru_lambda'], 'lru_bwd')
    gs['conv_w'] = vecs[0:4]
    gs['conv_b'], gs['lru_b_a'], gs['lru_b_x'], gs['lru_lambda'] = vecs[4:5], vecs[5:6], vecs[6:7], vecs[7:8]
    gs['lru_w_a'] = _diag_blocks(dwa)
    gs['lru_w_x'] = _diag_blocks(dwx)
    dz = [dgate, dxbr, dq, dk, dv, dgl, dga]
    gb['w_in'] = mm_tn(dz, um, 256, 'dw_in')
    dh1, gs['mix_pre_g'] = norm_bwd(dz, wf, 'w_in', h1, sw['mix_pre_g'], dh2, 'mix_bwd_in')

    df1, dgu1, gs['ffn1_post_g'] = ffn_bwd_a(dh1, f1, sw['ffn1_post_g'], wf, 'ffn1_w_down', g1, u1, 'ffn1_bwd_a')
    gb['ffn1_w_down'] = mm_tn([a1], df1, 256, 'ffn1_dw_down')
    gb['ffn1_w_gu'] = mm_tn([dgu1], n1, 512, 'ffn1_dw_gu')
    dx, gs['ffn1_pre_g'] = norm_bwd([dgu1], wf, 'ffn1_w_gu', x, sw['ffn1_pre_g'], dh1, 'ffn1_bwd_b')
    return loss_blk, dx, gs, gb


def _place():
    x, y, c = lax.axis_index('x'), lax.axis_index('y'), lax.axis_index('c')
    chips = [(1 - x, y), (x, 1 - y), (1 - x, 1 - y)]
    return x, y, c, chips


def _rcopy(src, dst, send_sem, recv_sem, to):
    return pltpu.make_async_remote_copy(src_ref=src, dst_ref=dst, send_sem=send_sem, recv_sem=recv_sem,
                                        device_id=to, device_id_type=MESH)


def ag_pack(pack):
    ch = HALF_ROWS // AG_CHUNKS
    n_ici = 3 * AG_CHUNKS

    def body(p_ref, o_ref, send, recv, lsem):
        x, y, c, chips = _place()
        me_q = 2 * x + y
        sib = (x, y, 1 - c)

        def rows(cc, k):
            return pl.ds(pl.multiple_of(cc * HALF_ROWS + k * ch, 16), ch)

        own = pltpu.make_async_copy(p_ref, o_ref.at[me_q], lsem)
        own.start()
        sends = []
        for k in range(AG_CHUNKS):
            for j, (cx, cy) in enumerate(chips):
                cp = _rcopy(p_ref.at[rows(c, k)], o_ref.at[me_q, rows(c, k)],
                            send.at[j * AG_CHUNKS + k], recv.at[j * AG_CHUNKS + k], (cx, cy, c))
                cp.start()
                sends.append(cp)
        for k in range(AG_CHUNKS):
            for j, (cx, cy) in enumerate(chips):
                blk = o_ref.at[2 * cx + cy, rows(c, k)]
                _rcopy(blk, blk, send.at[j * AG_CHUNKS + k], recv.at[j * AG_CHUNKS + k], (cx, cy, c)).wait_recv()
                cp = _rcopy(blk, blk, send.at[n_ici + j * AG_CHUNKS + k], recv.at[n_ici + j * AG_CHUNKS + k], sib)
                cp.start()
                sends.append(cp)
        for k in range(AG_CHUNKS):
            for j, (cx, cy) in enumerate(chips):
                blk = o_ref.at[2 * cx + cy, rows(1 - c, k)]
                _rcopy(blk, blk, send.at[n_ici + j * AG_CHUNKS + k], recv.at[n_ici + j * AG_CHUNKS + k], sib).wait_recv()
        for cp in sends:
            cp.wait_send()
        own.wait()

    return pl.pallas_call(
        body, name='ag_pack',
        out_shape=jax.ShapeDtypeStruct((N_CHIPS, PACK_ROWS, D_MODEL), BF16),
        in_specs=[ANY], out_specs=ANY,
        scratch_shapes=[pltpu.SemaphoreType.DMA((2 * n_ici,)), pltpu.SemaphoreType.DMA((2 * n_ici,)),
                        pltpu.SemaphoreType.DMA],
    )(pack)


def ag_small(blk, name):
    def body(x_ref, o_ref, send, recv, lsem):
        x, y, c, chips = _place()
        sib = (x, y, 1 - c)

        def slot(px, py, pc):
            return o_ref.at[4 * px + 2 * py + pc]

        own = pltpu.make_async_copy(x_ref, slot(x, y, c), lsem)
        own.start()
        first = [_rcopy(x_ref, slot(x, y, c), send.at[0], recv.at[0], sib)]
        first += [_rcopy(x_ref, slot(x, y, c), send.at[1 + j], recv.at[1 + j], (cx, cy, c))
                  for j, (cx, cy) in enumerate(chips)]
        for cp in first:
            cp.start()
        passed = []
        for j, (cx, cy) in enumerate(chips):
            blk_ref = slot(cx, cy, c)
            _rcopy(blk_ref, blk_ref, send.at[1 + j], recv.at[1 + j], (cx, cy, c)).wait_recv()
            cp = _rcopy(blk_ref, blk_ref, send.at[4 + j], recv.at[4 + j], sib)
            cp.start()
            passed.append(cp)
        sb = slot(x, y, 1 - c)
        _rcopy(sb, sb, send.at[0], recv.at[0], sib).wait_recv()
        for j, (cx, cy) in enumerate(chips):
            blk_ref = slot(cx, cy, 1 - c)
            _rcopy(blk_ref, blk_ref, send.at[4 + j], recv.at[4 + j], sib).wait_recv()
        for cp in first + passed:
            cp.wait_send()
        own.wait()

    return pl.pallas_call(
        body, name=name,
        out_shape=jax.ShapeDtypeStruct((N_DEV,) + blk.shape, blk.dtype),
        in_specs=[ANY], out_specs=ANY,
        scratch_shapes=[pltpu.SemaphoreType.DMA((7,)), pltpu.SemaphoreType.DMA((7,)), pltpu.SemaphoreType.DMA],
    )(blk)


def rs_pair(grads):
    names = [n for n, _, _ in PACK]

    def body(*refs):
        g_refs = refs[:len(names)]
        land_ref, send, recv = refs[len(names):]
        x, y, c, _ = _place()
        sib = (x, y, 1 - c)
        cps = []
        for w, (name, rows, _) in enumerate(PACK):
            hr = rows // 2
            o2 = PACK_OFF[name] // 2
            for q in range(N_CHIPS):
                cp = _rcopy(g_refs[w].at[q, pl.ds(1 - c, 1)], land_ref.at[q, :, pl.ds(o2, hr)],
                            send.at[w * N_CHIPS + q], recv.at[w * N_CHIPS + q], sib)
                cp.start()
                cps.append(cp)
        for cp in cps:
            cp.wait()

    n_cp = len(names) * N_CHIPS
    return pl.pallas_call(
        body, name='rs_pair',
        out_shape=jax.ShapeDtypeStruct((N_CHIPS, 1, HALF_ROWS, D_MODEL), BF16),
        in_specs=[ANY] * len(names), out_specs=ANY,
        scratch_shapes=[pltpu.SemaphoreType.DMA((n_cp,)), pltpu.SemaphoreType.DMA((n_cp,))],
    )(*[grads[n] for n in names])


def pair_sum(g4, land, c_arr, off2, name):
    hr = g4.shape[2]

    def body(c_ref, g_ref, l_ref, o_ref):
        o_ref[0] = (g_ref[0, 0].astype(F32) + l_ref[0, 0].astype(F32)).astype(BF16)

    return pl.pallas_call(
        body, name=name,
        grid_spec=pltpu.PrefetchScalarGridSpec(
            num_scalar_prefetch=1, grid=(N_CHIPS,),
            in_specs=[pl.BlockSpec((1, 1, hr, D_MODEL), lambda q, c: (q, c[0], 0, 0)),
                      pl.BlockSpec((1, 1, hr, D_MODEL), lambda q, c: (q, 0, off2 // hr, 0))],
            out_specs=pl.BlockSpec((1, hr, D_MODEL), lambda q, c: (q, 0, 0))),
        out_shape=jax.ShapeDtypeStruct((N_CHIPS, hr, D_MODEL), BF16),
        compiler_params=_params(1),
    )(c_arr, g4, land)


def rs_chips(sums):
    names = [n for n, _, _ in PACK]

    def body(*refs):
        s_refs = refs[:len(names)]
        land_ref, send, recv, lsem = refs[len(names):]
        x, y, c, chips = _place()
        me_q = 2 * x + y
        cps, locs = [], []
        for w, (name, rows, _) in enumerate(PACK):
            hr = rows // 2
            o2 = PACK_OFF[name] // 2
            loc = pltpu.make_async_copy(s_refs[w].at[me_q], land_ref.at[me_q, pl.ds(o2, hr)], lsem.at[w])
            loc.start()
            locs.append(loc)
            for j, (cx, cy) in enumerate(chips):
                cp = _rcopy(s_refs[w].at[2 * cx + cy], land_ref.at[me_q, pl.ds(o2, hr)],
                            send.at[w * 3 + j], recv.at[w * 3 + j], (cx, cy, c))
                cp.start()
                cps.append(cp)
        for w, (name, rows, _) in enumerate(PACK):
            hr = rows // 2
            o2 = PACK_OFF[name] // 2
            for j, (cx, cy) in enumerate(chips):
                blk = land_ref.at[2 * cx + cy, pl.ds(o2, hr)]
                _rcopy(blk, blk, send.at[w * 3 + j], recv.at[w * 3 + j], (cx, cy, c)).wait_recv()
        for cp in cps:
            cp.wait_send()
        for loc in locs:
            loc.wait()

    n_cp = len(names) * 3
    return pl.pallas_call(
        body, name='rs_chips',
        out_shape=jax.ShapeDtypeStruct((N_CHIPS, HALF_ROWS, D_MODEL), BF16),
        in_specs=[ANY] * len(names), out_specs=ANY,
        scratch_shapes=[pltpu.SemaphoreType.DMA((n_cp,)), pltpu.SemaphoreType.DMA((n_cp,)),
                        pltpu.SemaphoreType.DMA((len(names),))],
    )(*[sums[n] for n in names])


def chip_sum(land):
    tr = 128

    def body(l_ref, o_ref):
        acc = l_ref[0].astype(F32)
        for s in range(1, N_CHIPS):
            acc = acc + l_ref[s].astype(F32)
        o_ref[...] = acc

    return pl.pallas_call(
        body, name='chip_sum', grid=(HALF_ROWS // tr,),
        in_specs=[pl.BlockSpec((N_CHIPS, tr, D_MODEL), lambda i: (0, i, 0))],
        out_specs=pl.BlockSpec((tr, D_MODEL), lambda i: (i, 0)),
        out_shape=jax.ShapeDtypeStruct((HALF_ROWS, D_MODEL), F32),
        compiler_params=_params(1),
    )(land)


def ag_pair(half):
    def body(h_ref, o_ref, send, recv, lsem):
        x, y, c, _ = _place()
        own = pltpu.make_async_copy(h_ref, o_ref.at[c], lsem)
        own.start()
        cp = _rcopy(h_ref, o_ref.at[c], send, recv, (x, y, 1 - c))
        cp.start()
        cp.wait()
        own.wait()

    return pl.pallas_call(
        body, name='ag_pair',
        out_shape=jax.ShapeDtypeStruct((2, HALF_ROWS, D_MODEL), F32),
        in_specs=[ANY], out_specs=ANY,
        scratch_shapes=[pltpu.SemaphoreType.DMA, pltpu.SemaphoreType.DMA, pltpu.SemaphoreType.DMA],
    )(half)


def small_sum(parts):
    def body(p_ref, o_ref):
        acc = p_ref[0]
        for s in range(1, N_DEV):
            acc = acc + p_ref[s]
        o_ref[...] = acc

    return pl.pallas_call(
        body, name='small_sum', grid=(1,),
        in_specs=[pl.BlockSpec(parts.shape, lambda i: (0, 0, 0))],
        out_specs=pl.BlockSpec(parts.shape[1:], lambda i: (0, 0)),
        out_shape=jax.ShapeDtypeStruct(parts.shape[1:], F32),
        compiler_params=_params(1),
    )(parts)


def _adam_math(w, g, m, v):
    m2 = ADAM_B1 * m + (1.0 - ADAM_B1) * g
    v2 = ADAM_B2 * v + (1.0 - ADAM_B2) * (g * g)
    m_hat = m2 / (1.0 - ADAM_B1 ** ADAM_STEP)
    v_hat = v2 / (1.0 - ADAM_B2 ** ADAM_STEP)
    delta = -ADAM_LR * (m_hat / (jnp.sqrt(v_hat) + ADAM_EPS) + ADAM_WD * w)
    return delta, m2, v2


def _adam_body(transposed):
    def body(g_ref, w_ref, m_ref, v_ref, go_ref, d_ref, mo_ref, vo_ref):
        if transposed:
            gt = g_ref[...]
            g = gt.reshape(gt.shape[0] * gt.shape[1], gt.shape[2]).T
        else:
            g = g_ref[0]
        go_ref[...] = g
        d_ref[...], mo_ref[...], vo_ref[...] = _adam_math(w_ref[...], g, m_ref[...], v_ref[...])
    return body


def adam_rows(full, name, w, m, v):
    hr = w.shape[0] // 2
    ob = (PACK_OFF[name] // 2) // hr
    blk = pl.BlockSpec((hr, D_MODEL), lambda h: (h, 0))
    return pl.pallas_call(
        _adam_body(False), name='adam_' + name, grid=(2,),
        in_specs=[pl.BlockSpec((1, hr, D_MODEL), lambda h: (h, ob, 0)), blk, blk, blk],
        out_specs=[blk] * 4,
        out_shape=[jax.ShapeDtypeStruct(w.shape, F32)] * 4,
        compiler_params=_params(1),
    )(full, w, m, v)


def adam_cols(full, name, w, m, v):
    cols = w.shape[1]
    hr = cols // 2
    ob = (PACK_OFF[name] // 2) // hr
    tr = 128
    blk = pl.BlockSpec((tr, cols), lambda i: (i, 0))
    return pl.pallas_call(
        _adam_body(True), name='adam_' + name, grid=(D_MODEL // tr,),
        in_specs=[pl.BlockSpec((2, hr, tr), lambda i: (0, ob, i)), blk, blk, blk],
        out_specs=[blk] * 4,
        out_shape=[jax.ShapeDtypeStruct(w.shape, F32)] * 4,
        compiler_params=_params(1),
    )(full, w, m, v)


def adam_small(g, w, m, v):
    def body(g_ref, w_ref, m_ref, v_ref, d_ref, mo_ref, vo_ref):
        d_ref[...], mo_ref[...], vo_ref[...] = _adam_math(w_ref[...], g_ref[...], m_ref[...], v_ref[...])

    blk = pl.BlockSpec(w.shape, lambda i: (0, 0))
    return pl.pallas_call(
        body, name='adam_small', grid=(1,), in_specs=[blk] * 4, out_specs=[blk] * 3,
        out_shape=[jax.ShapeDtypeStruct(w.shape, F32)] * 3, compiler_params=_params(1),
    )(g, w, m, v)


WEIGHTS = ('ffn1_pre_g', 'ffn1_w_gu', 'ffn1_w_down', 'ffn1_post_g', 'mix_pre_g', 'w_in', 'conv_w', 'conv_b',
           'lru_w_a', 'lru_b_a', 'lru_w_x', 'lru_b_x', 'lru_lambda', 'attn_sinks', 'w_proj_lru', 'w_proj_attn',
           'w_out', 'mix_post_g', 'ffn2_pre_g', 'ffn2_w_gu', 'ffn2_w_down', 'ffn2_post_g')
SMALL = tuple(n for n in WEIGHTS if n not in PACK_OFF)


def _pack_small(d, conv_rows):
    sinks = jnp.pad(d['attn_sinks'].reshape(1, N_Q_HEADS), ((0, 0), (0, D_MODEL - N_Q_HEADS)))
    conv = jnp.pad(conv_rows, ((0, ROW_WA - ROW_CONV - conv_rows.shape[0]), (0, 0)))
    return jnp.concatenate([d[n].reshape(1, D_MODEL) for n in SMALL_VECS] + [sinks, conv]
                           + [d['lru_w_a'].reshape(64, D_MODEL), d['lru_w_x'].reshape(64, D_MODEL)], axis=0)


def _unpack_small(p, shapes):
    out = {n: p[k:k + 1].reshape(shapes[n]) for k, n in enumerate(SMALL_VECS)}
    out['attn_sinks'] = p[ROW_SINKS:ROW_SINKS + 1, :N_Q_HEADS].reshape(shapes['attn_sinks'])
    out['conv_w'] = p[ROW_CONV:ROW_CONV + 1].reshape(shapes['conv_w'])
    out['lru_w_a'] = p[ROW_WA:ROW_WA + 64].reshape(shapes['lru_w_a'])
    out['lru_w_x'] = p[ROW_WX:ROW_WX + 64].reshape(shapes['lru_w_x'])
    return out


def kernel(x, ffn1_pre_g, ffn1_w_gu, ffn1_w_down, ffn1_post_g, mix_pre_g, w_in, conv_w, conv_b, lru_w_a, lru_b_a, lru_w_x, lru_b_x, lru_lambda, attn_sinks, w_proj_lru, w_proj_attn, w_out, mix_post_g, ffn2_pre_g, ffn2_w_gu, ffn2_w_down, ffn2_post_g, loss_target, m_ffn1_pre_g, m_ffn1_w_gu, m_ffn1_w_down, m_ffn1_post_g, m_mix_pre_g, m_w_in, m_conv_w, m_conv_b, m_lru_w_a, m_lru_b_a, m_lru_w_x, m_lru_b_x, m_lru_lambda, m_attn_sinks, m_w_proj_lru, m_w_proj_attn, m_w_out, m_mix_post_g, m_ffn2_pre_g, m_ffn2_w_gu, m_ffn2_w_down, m_ffn2_post_g, v_ffn1_pre_g, v_ffn1_w_gu, v_ffn1_w_down, v_ffn1_post_g, v_mix_pre_g, v_w_in, v_conv_w, v_conv_b, v_lru_w_a, v_lru_b_a, v_lru_w_x, v_lru_b_x, v_lru_lambda, v_attn_sinks, v_w_proj_lru, v_w_proj_attn, v_w_out, v_mix_post_g, v_ffn2_pre_g, v_ffn2_w_gu, v_ffn2_w_down, v_ffn2_post_g):
    given = dict(locals())
    w = {n: given[n] for n in WEIGHTS}
    mom = {n: given['m_' + n] for n in WEIGHTS}
    var = {n: given['v_' + n] for n in WEIGHTS}
    shapes = {n: w[n].shape for n in WEIGHTS}
    xq = lax.axis_index('x')
    yq = lax.axis_index('y')
    cq = lax.axis_index('c')
    me_q = 2 * xq + yq

    pack = jnp.concatenate([(w[n][0].T if t else w[n][0]) for n, _, t in PACK], axis=0).astype(BF16)
    wf = ag_pack(pack)
    conv_all = ag_small(jnp.pad(w['conv_w'][0], ((0, 4), (0, 0))), 'ag_conv')
    conv_full = jnp.transpose(conv_all[0::2, :4, :], (1, 0, 2)).reshape(4, LRU_W)

    sw = {n: (w[n][0] if w[n].ndim > 2 else w[n]) for n in SMALL}
    sw['conv_w'] = conv_full
    loss_blk, dx, gs, gb = local_step(x[0], loss_target[0], sw, wf)
    loss = lax.psum(loss_blk[0, 0], ('x', 'y', 'c'))

    g4 = {n: gb[n].reshape(N_CHIPS, 2, r // 2, D_MODEL) for n, r, _ in PACK}
    land = rs_pair(g4)
    c_arr = cq.reshape(1).astype(jnp.int32)
    sums = {n: pair_sum(g4[n], land, c_arr, PACK_OFF[n] // 2, 'pair_sum_' + n) for n, _, _ in PACK}
    full = ag_pair(chip_sum(rs_chips(sums)))

    out_g, out_d, out_m, out_v = {}, {}, {}, {}
    for n, _, t in PACK:
        fn = adam_cols if t else adam_rows
        g_, d_, m_, v_ = fn(full, n, w[n][0], mom[n][0], var[n][0])
        out_g[n], out_d[n], out_m[n], out_v[n] = g_[None], d_[None], m_[None], v_[None]

    tot = small_sum(ag_small(_pack_small(gs, gs['conv_w']), 'ag_small_grads'))
    conv_g = lax.dynamic_slice(tot[ROW_CONV:ROW_CONV + 4], (0, me_q * (LRU_W // N_CHIPS)), (4, LRU_W // N_CHIPS))
    small_g = _unpack_small(tot, shapes)
    small_g['conv_w'] = conv_g.reshape(shapes['conv_w'])
    g_pack = jnp.concatenate([tot[:ROW_CONV], conv_g.reshape(1, D_MODEL), jnp.zeros((ROW_WA - ROW_CONV - 1, D_MODEL), F32),
                              tot[ROW_WA:]], axis=0)
    packs = [_pack_small({n: d[n] for n in SMALL}, d['conv_w'].reshape(1, D_MODEL)) for d in (w, mom, var)]
    d_p, m_p, v_p = adam_small(g_pack, *packs)
    for n in SMALL:
        out_g[n] = small_g[n]
    for dst, p in ((out_d, d_p), (out_m, m_p), (out_v, v_p)):
        dst.update(_unpack_small(p, shapes))

    return (loss, dx[None], *[out_g[n] for n in WEIGHTS], *[out_d[n] for n in WEIGHTS],
            *[out_m[n] for n in WEIGHTS], *[out_v[n] for n in WEIGHTS])
```

```python
import jax
import jax.numpy as jnp
from jax import lax
from jax.experimental import pallas as pl
from jax.experimental.pallas import tpu as pltpu

F32 = jnp.float32
BF16 = jnp.bfloat16

SEQ = 2048
D_MODEL = 1024
D_FF = 2816
LRU_W = 1024
LRU_BLOCK_W = 64
HEAD_DIM = 64
N_Q_HEADS = 16
N_KV_HEADS = 4
KV_W = N_KV_HEADS * HEAD_DIM
ATTN_BLOCK = 128
N_ATTN_BLOCKS = SEQ // ATTN_BLOCK
IN_SEGS = (1024, 1024, 1024, 256, 256, 1024, 1024)
IN_W = sum(IN_SEGS)
NORM_EPS = 1e-6
MASK_VALUE = -1e30
ROPE_THETA = 10000.0
LRU_C = 8.0
MACARON = 0.5
ADAM_LR = 0.001
ADAM_B1 = 0.9
ADAM_B2 = 0.999
ADAM_EPS = 1e-08
ADAM_WD = 0.01
ADAM_STEP = 10

N_CHIPS = 4
N_DEV = 8
VMEM_LIMIT = 56 * 1024 * 1024
MESH = pl.DeviceIdType.MESH
ANY = pl.BlockSpec(memory_space=pl.ANY)

PACK = (('ffn1_w_gu', 1408, True), ('w_in', 1408, True), ('ffn2_w_gu', 1408, True),
        ('ffn1_w_down', 704, False), ('ffn2_w_down', 704, False),
        ('w_proj_lru', 256, False), ('w_proj_attn', 256, False), ('w_out', 256, False))
PACK_ROWS = sum(r for _, r, _ in PACK)
HALF_ROWS = PACK_ROWS // 2
PACK_OFF = {}
_o = 0
for _n, _r, _t in PACK:
    PACK_OFF[_n] = _o
    _o += _r
AG_CHUNKS = 20
PAIR_CHUNKS = 16

SMALL_VECS = ('ffn1_pre_g', 'ffn1_post_g', 'mix_pre_g', 'conv_b', 'lru_b_a', 'lru_b_x', 'lru_lambda',
              'mix_post_g', 'ffn2_pre_g', 'ffn2_post_g')
SMALL_ROWS = 144
ROW_SINKS, ROW_CONV, ROW_WA, ROW_WX = 10, 11, 16, 80


def _dot(a, b):
    return jnp.dot(a, b, preferred_element_type=F32)


def _dot_nt(a, b):
    return lax.dot_general(a, b, (((1,), (1,)), ((), ())), preferred_element_type=F32)


def _dot_tn(a, b):
    return lax.dot_general(a, b, (((0,), (0,)), ((), ())), preferred_element_type=F32)


def _params(n_grid):
    return pltpu.CompilerParams(dimension_semantics=("arbitrary",) * n_grid, vmem_limit_bytes=VMEM_LIMIT)


def _sigmoid(x):
    return 1.0 / (1.0 + jnp.exp(-x))


def _rsqrt_mean_sq(x):
    return lax.rsqrt(jnp.mean(x * x, axis=-1, keepdims=True) + NORM_EPS)


def _expm1(x):
    poly = x * (1.0 + x * (0.5 + x * (1.0 / 6.0 + x * (1.0 / 24.0 + x * (1.0 / 120.0)))))
    return jnp.where(jnp.abs(x) < 0.1, poly, jnp.exp(x) - 1.0)


_GELU_K = 0.7978845608028654
_GELU_C = 0.044715


def _gelu(x):
    t = jnp.tanh(_GELU_K * (x + _GELU_C * x * x * x))
    return 0.5 * x * (1.0 + t), t


def _gelu_grad(x, t):
    return 0.5 * (1.0 + t) + 0.5 * x * (1.0 - t * t) * _GELU_K * (1.0 + 3.0 * _GELU_C * x * x)


def _load_weight(wf_ref, name, dst_ref, sem):
    rows = dst_ref.shape[0] // N_CHIPS
    off = PACK_OFF[name]
    cps = [pltpu.make_async_copy(wf_ref.at[q, pl.ds(off, rows)], dst_ref.at[pl.ds(q * rows, rows)], sem.at[q])
           for q in range(N_CHIPS)]
    for cp in cps:
        cp.start()
    for cp in cps:
        cp.wait()


def _weight_scratch(rows_total):
    return [pltpu.VMEM((rows_total, D_MODEL), BF16), pltpu.SemaphoreType.DMA((N_CHIPS,))]


_ROW = lambda tm: pl.BlockSpec((tm, D_MODEL), lambda i: (i, 0))
_VEC = pl.BlockSpec((1, D_MODEL), lambda i: (0, 0))


def ffn_fwd_a(x, g_pre, wf, wname, name):
    tm, tn = 256, 256

    def body(x_ref, gp_ref, wf_ref, n_ref, g_ref, u_ref, a_ref, wt_ref, sem):
        @pl.when(pl.program_id(0) == 0)
        def _():
            _load_weight(wf_ref, wname, wt_ref, sem)

        xv = x_ref[...]
        n = (xv * _rsqrt_mean_sq(xv) * gp_ref[...]).astype(BF16)
        n_ref[...] = n
        for j in range(D_FF // tn):
            g = _dot_nt(n, wt_ref[j * tn:(j + 1) * tn, :])
            u = _dot_nt(n, wt_ref[D_FF + j * tn:D_FF + (j + 1) * tn, :])
            g_ref[:, j * tn:(j + 1) * tn] = g.astype(BF16)
            u_ref[:, j * tn:(j + 1) * tn] = u.astype(BF16)
            a_ref[:, j * tn:(j + 1) * tn] = (g * _sigmoid(g) * u).astype(BF16)

    wide = pl.BlockSpec((tm, D_FF), lambda i: (i, 0))
    return pl.pallas_call(
        body, name=name, grid=(SEQ // tm,),
        in_specs=[_ROW(tm), _VEC, ANY],
        out_specs=[_ROW(tm), wide, wide, wide],
        out_shape=[jax.ShapeDtypeStruct((SEQ, D_MODEL), BF16)] + [jax.ShapeDtypeStruct((SEQ, D_FF), BF16)] * 3,
        scratch_shapes=_weight_scratch(2 * D_FF),
        compiler_params=_params(1),
    )(x, g_pre, wf)


def ffn_fwd_b(a, wf, wname, g_post, h_in, name, target=None):
    tm = 256
    final = target is not None

    def body(*refs):
        if final:
            a_ref, wf_ref, gp_ref, h_ref, t_ref, f_ref, o_ref, loss_ref, wd_ref, sem = refs
        else:
            a_ref, wf_ref, gp_ref, h_ref, f_ref, o_ref, wd_ref, sem = refs

        @pl.when(pl.program_id(0) == 0)
        def _():
            _load_weight(wf_ref, wname, wd_ref, sem)
            if final:
                loss_ref[...] = jnp.zeros_like(loss_ref)

        f = _dot(a_ref[...], wd_ref[...])
        f_ref[...] = f
        y = h_ref[...] + MACARON * (f * _rsqrt_mean_sq(f) * gp_ref[...])
        if final:
            err = y - t_ref[...]
            o_ref[...] = err * (1.0 / D_MODEL)
            loss_ref[...] += 0.5 * jnp.sum(err * err) * (1.0 / D_MODEL)
        else:
            o_ref[...] = y

    row = _ROW(tm)
    in_specs = [pl.BlockSpec((tm, D_FF), lambda i: (i, 0)), ANY, _VEC, row]
    out_specs = [row, row]
    out_shape = [jax.ShapeDtypeStruct((SEQ, D_MODEL), F32)] * 2
    args = [a, wf, g_post, h_in]
    if final:
        in_specs.append(row)
        args.append(target)
        out_specs.append(pl.BlockSpec((8, 128), lambda i: (0, 0)))
        out_shape.append(jax.ShapeDtypeStruct((8, 128), F32))
    return pl.pallas_call(body, name=name, grid=(SEQ // tm,), in_specs=in_specs, out_specs=out_specs,
                          out_shape=out_shape, scratch_shapes=_weight_scratch(D_FF),
                          compiler_params=_params(1))(*args)


def ffn_bwd_a(d_out, f, g_post, wf, wname, g, u, name):
    tm = 256

    def body(do_ref, f_ref, gp_ref, wf_ref, g_ref, u_ref, df_ref, dgu_ref, dgp_ref, wd_ref, sem):
        @pl.when(pl.program_id(0) == 0)
        def _():
            _load_weight(wf_ref, wname, wd_ref, sem)
            dgp_ref[...] = jnp.zeros_like(dgp_ref)

        fv = f_ref[...]
        rf = _rsqrt_mean_sq(fv)
        fh = fv * rf
        dn = MACARON * do_ref[...]
        dgp_ref[...] += jnp.sum(dn * fh, axis=0, keepdims=True)
        t = dn * gp_ref[...]
        df = (rf * (t - fh * jnp.mean(t * fh, axis=-1, keepdims=True))).astype(BF16)
        df_ref[...] = df
        da = _dot_nt(df, wd_ref[...])
        gv = g_ref[...].astype(F32)
        uv = u_ref[...].astype(F32)
        s = _sigmoid(gv)
        dgu_ref[:, :D_FF] = (da * uv * s * (1.0 + gv * (1.0 - s))).astype(BF16)
        dgu_ref[:, D_FF:] = (da * gv * s).astype(BF16)

    row = _ROW(tm)
    wide = pl.BlockSpec((tm, D_FF), lambda i: (i, 0))
    return pl.pallas_call(
        body, name=name, grid=(SEQ // tm,),
        in_specs=[row, row, _VEC, ANY, wide, wide],
        out_specs=[row, pl.BlockSpec((tm, 2 * D_FF), lambda i: (i, 0)), _VEC],
        out_shape=[jax.ShapeDtypeStruct((SEQ, D_MODEL), BF16), jax.ShapeDtypeStruct((SEQ, 2 * D_FF), BF16),
                   jax.ShapeDtypeStruct((1, D_MODEL), F32)],
        scratch_shapes=_weight_scratch(D_FF),
        compiler_params=_params(1),
    )(d_out, f, g_post, wf, g, u)


def norm_bwd(pieces, wf, wname, x, g_pre, d_res, name):
    tm = 256
    widths = [p.shape[1] for p in pieces]
    offs = [sum(widths[:k]) for k in range(len(widths))]
    n_p = len(pieces)

    def body(*refs):
        p_refs = refs[:n_p]
        wf_ref, x_ref, g_ref, r_ref, dx_ref, dg_ref, wt_ref, sem = refs[n_p:]

        @pl.when(pl.program_id(0) == 0)
        def _():
            _load_weight(wf_ref, wname, wt_ref, sem)
            dg_ref[...] = jnp.zeros_like(dg_ref)

        dn = None
        for p_ref, lo, wd in zip(p_refs, offs, widths):
            part = _dot(p_ref[...], wt_ref[lo:lo + wd, :])
            dn = part if dn is None else dn + part
        xv = x_ref[...]
        r = _rsqrt_mean_sq(xv)
        xh = xv * r
        dg_ref[...] += jnp.sum(dn * xh, axis=0, keepdims=True)
        t = dn * g_ref[...]
        dx_ref[...] = r_ref[...] + r * (t - xh * jnp.mean(t * xh, axis=-1, keepdims=True))

    row = _ROW(tm)
    return pl.pallas_call(
        body, name=name, grid=(SEQ // tm,),
        in_specs=[pl.BlockSpec((tm, wd), lambda i: (i, 0)) for wd in widths] + [ANY, row, _VEC, row],
        out_specs=[row, _VEC],
        out_shape=[jax.ShapeDtypeStruct((SEQ, D_MODEL), F32), jax.ShapeDtypeStruct((1, D_MODEL), F32)],
        scratch_shapes=_weight_scratch(sum(widths)),
        compiler_params=_params(1),
    )(*pieces, wf, x, g_pre, d_res)


def mm_tn(pieces, b, tm, name):
    widths = [p.shape[1] for p in pieces]
    m_total = sum(widths)
    n_p = len(pieces)
    starts = [sum(widths[:k]) // tm for k in range(n_p)]
    counts = [wd // tm for wd in widths]

    def body(*refs):
        p_refs = refs[:n_p]
        b_ref, o_ref = refs[n_p:]
        i = pl.program_id(0)
        for p_ref, st, ct in zip(p_refs, starts, counts):
            @pl.when((i >= st) & (i < st + ct))
            def _(p_ref=p_ref):
                o_ref[...] = _dot_tn(p_ref[...], b_ref[...]).astype(BF16)

    def piece_spec(st, ct):
        return pl.BlockSpec((SEQ, tm), lambda i: (0, jnp.clip(i - st, 0, ct - 1)))

    return pl.pallas_call(
        body, name=name, grid=(m_total // tm,),
        in_specs=[piece_spec(st, ct) for st, ct in zip(starts, counts)] + [pl.BlockSpec((SEQ, D_MODEL), lambda i: (0, 0))],
        out_specs=pl.BlockSpec((tm, D_MODEL), lambda i: (i, 0)),
        out_shape=jax.ShapeDtypeStruct((m_total, D_MODEL), BF16),
        compiler_params=_params(1),
    )(*pieces, b)


def mix_in(h, g_pre, wf, name):
    tm = 256
    offs = [sum(IN_SEGS[:k]) for k in range(len(IN_SEGS))]
    dts = [F32, F32, F32, F32, BF16, F32, F32]
    n_o = len(IN_SEGS)

    def body(*refs):
        h_ref, g_ref, wf_ref, um_ref = refs[:4]
        o_refs = refs[4:4 + n_o]
        wt_ref, sem = refs[4 + n_o:]

        @pl.when(pl.program_id(0) == 0)
        def _():
            _load_weight(wf_ref, 'w_in', wt_ref, sem)

        hv = h_ref[...]
        um = (hv * _rsqrt_mean_sq(hv) * g_ref[...]).astype(BF16)
        um_ref[...] = um
        for o_ref, lo, wd in zip(o_refs, offs, IN_SEGS):
            for c0 in range(0, wd, 256):
                o_ref[:, c0:c0 + 256] = _dot_nt(um, wt_ref[lo + c0:lo + c0 + 256, :]).astype(o_ref.dtype)

    return pl.pallas_call(
        body, name=name, grid=(SEQ // tm,),
        in_specs=[_ROW(tm), _VEC, ANY],
        out_specs=[_ROW(tm)] + [pl.BlockSpec((tm, wd), lambda i: (i, 0)) for wd in IN_SEGS],
        out_shape=[jax.ShapeDtypeStruct((SEQ, D_MODEL), BF16)]
        + [jax.ShapeDtypeStruct((SEQ, wd), dt) for wd, dt in zip(IN_SEGS, dts)],
        scratch_shapes=_weight_scratch(IN_W),
        compiler_params=_params(1),
    )(h, g_pre, wf)


LRU_TC = 256


def _conv_fwd(xb, cw, cb, tt):
    xc = xb * cw[3:4, :] + cb
    shifted = []
    for s in (1, 2, 3):
        sh = jnp.where(tt >= s, pltpu.roll(xb, s, 0), 0.0)
        shifted.append(sh)
        xc = xc + sh * cw[3 - s:4 - s, :]
    return xc, shifted


def _lru_gates(xc, wa, ba, wx, bx, lam):
    xcb = xc.astype(BF16)
    r = _sigmoid(_dot(xcb, wa) + ba)
    i = _sigmoid(_dot(xcb, wx) + bx)
    nl = -lam
    sp = jnp.maximum(nl, 0.0) + jnp.log1p(jnp.exp(-jnp.abs(nl)))
    la = (-LRU_C * r) * sp
    a = jnp.exp(la)
    mult = jnp.sqrt(jnp.maximum(-_expm1(2.0 * la), 0.0))
    return xcb, r, i, sp, a, mult


def _scan(a, b, tt, reverse):
    n = a.shape[0]
    s = 1
    while s < n:
        if reverse:
            keep = tt < n - s
            shift = n - s
        else:
            keep = tt >= s
            shift = s
        b = a * jnp.where(keep, pltpu.roll(b, shift, 0), 0.0) + b
        if 2 * s < n:
            a = a * jnp.where(keep, pltpu.roll(a, shift, 0), 1.0)
        s *= 2
    return b


def _lru_specs():
    col = pl.BlockSpec((SEQ, LRU_TC), lambda j: (0, j))
    vec = pl.BlockSpec((1, LRU_TC), lambda j: (0, j))
    bd = pl.BlockSpec((1, LRU_TC, LRU_TC), lambda j: (j, 0, 0))
    cw = pl.BlockSpec((4, LRU_TC), lambda j: (0, j))
    return col, vec, bd, cw


def lru_fwd(gate, xbr, conv_w, conv_b, wa_bd, b_a, wx_bd, b_x, lam, name):
    col, vec, bd, cw = _lru_specs()

    def body(gate_ref, xbr_ref, cw_ref, cb_ref, wa_ref, ba_ref, wx_ref, bx_ref, lam_ref, y_ref, h_ref):
        tt = lax.broadcasted_iota(jnp.int32, (SEQ, LRU_TC), 0)
        xc, _ = _conv_fwd(xbr_ref[...], cw_ref[...], cb_ref[...], tt)
        _, r, i, sp, a, mult = _lru_gates(xc, wa_ref[0], ba_ref[...], wx_ref[0], bx_ref[...], lam_ref[...])
        h = _scan(a, mult * (i * xc), tt, reverse=False)
        h_ref[...] = h
        gl, _ = _gelu(gate_ref[...])
        y_ref[...] = (h * gl).astype(BF16)

    return pl.pallas_call(
        body, name=name, grid=(LRU_W // LRU_TC,),
        in_specs=[col, col, cw, vec, bd, vec, bd, vec, vec],
        out_specs=[col, col],
        out_shape=[jax.ShapeDtypeStruct((SEQ, LRU_W), BF16), jax.ShapeDtypeStruct((SEQ, LRU_W), F32)],
        compiler_params=_params(1),
    )(gate, xbr, conv_w, conv_b, wa_bd, b_a, wx_bd, b_x, lam)


def lru_bwd(gate, xbr, h, dy, conv_w, conv_b, wa_bd, b_a, wx_bd, b_x, lam, name):
    col, vec, bd, cw = _lru_specs()

    def body(gate_ref, xbr_ref, h_ref, dy_ref, cw_ref, cb_ref, wa_ref, ba_ref, wx_ref, bx_ref, lam_ref,
             dgate_ref, dxbr_ref, vecs_ref, dwa_ref, dwx_ref):
        tt = lax.broadcasted_iota(jnp.int32, (SEQ, LRU_TC), 0)
        cwv = cw_ref[...]
        lam = lam_ref[...]
        xb = xbr_ref[...]
        xc, shifted = _conv_fwd(xb, cwv, cb_ref[...], tt)
        wa = wa_ref[0]
        wx = wx_ref[0]
        xcb, r, i, sp, a, mult = _lru_gates(xc, wa, ba_ref[...], wx, bx_ref[...], lam)
        hv = h_ref[...]
        dyv = dy_ref[...]
        gv = gate_ref[...]
        gl, th = _gelu(gv)
        dgate_ref[...] = (dyv * hv * _gelu_grad(gv, th)).astype(BF16)
        a_next = jnp.where(tt < SEQ - 1, pltpu.roll(a, SEQ - 1, 0), 0.0)
        gsum = _scan(a_next, dyv * gl, tt, reverse=True)
        h_prev = jnp.where(tt >= 1, pltpu.roll(hv, 1, 0), 0.0)
        d_mult = gsum * i * xc
        d_i = gsum * mult * xc
        d_xc = gsum * mult * i
        d_la = gsum * h_prev * a - d_mult * (a * a) / mult
        d_pr = (d_la * (-LRU_C * sp)) * r * (1.0 - r)
        d_pi = d_i * i * (1.0 - i)
        d_lam = jnp.sum(d_la * r, axis=0, keepdims=True) * (LRU_C * _sigmoid(-lam))
        d_prb = d_pr.astype(BF16)
        d_pib = d_pi.astype(BF16)
        d_xc = d_xc + _dot_nt(d_prb, wa) + _dot_nt(d_pib, wx)
        dwa_ref[0] = _dot_tn(xcb, d_prb)
        dwx_ref[0] = _dot_tn(xcb, d_pib)
        rows = [jnp.sum(d_xc * shifted[2], axis=0, keepdims=True),
                jnp.sum(d_xc * shifted[1], axis=0, keepdims=True),
                jnp.sum(d_xc * shifted[0], axis=0, keepdims=True),
                jnp.sum(d_xc * xb, axis=0, keepdims=True),
                jnp.sum(d_xc, axis=0, keepdims=True),
                jnp.sum(d_pr, axis=0, keepdims=True),
                jnp.sum(d_pi, axis=0, keepdims=True),
                d_lam]
        ri = lax.broadcasted_iota(jnp.int32, (8, LRU_TC), 0)
        acc = jnp.zeros((8, LRU_TC), F32)
        for k, rv in enumerate(rows):
            acc = jnp.where(ri == k, rv, acc)
        vecs_ref[...] = acc
        d_xb = d_xc * cwv[3:4, :]
        for s in (1, 2, 3):
            d_xb = d_xb + jnp.where(tt < SEQ - s, pltpu.roll(d_xc, SEQ - s, 0), 0.0) * cwv[3 - s:4 - s, :]
        dxbr_ref[...] = d_xb.astype(BF16)

    return pl.pallas_call(
        body, name=name, grid=(LRU_W // LRU_TC,),
        in_specs=[col, col, col, col, cw, vec, bd, vec, bd, vec, vec],
        out_specs=[col, col, pl.BlockSpec((8, LRU_TC), lambda j: (0, j)), bd, bd],
        out_shape=[jax.ShapeDtypeStruct((SEQ, LRU_W), BF16), jax.ShapeDtypeStruct((SEQ, LRU_W), BF16),
                   jax.ShapeDtypeStruct((8, LRU_W), F32),
                   jax.ShapeDtypeStruct((LRU_W // LRU_TC, LRU_TC, LRU_TC), F32),
                   jax.ShapeDtypeStruct((LRU_W // LRU_TC, LRU_TC, LRU_TC), F32)],
        compiler_params=_params(1),
    )(gate, xbr, h, dy, conv_w, conv_b, wa_bd, b_a, wx_bd, b_x, lam)


def _rope(x, cos, sin_signed):
    w = x.shape[1]
    reps = w // 128
    if reps > 1:
        cos = jnp.tile(cos, (1, reps))
        sin_signed = jnp.tile(sin_signed, (1, reps))
    lane = lax.broadcasted_iota(jnp.int32, x.shape, 1)
    first = (lane & 63) < 32
    partner = jnp.where(first, pltpu.roll(x, w - 32, 1), pltpu.roll(x, 32, 1))
    return x * cos + partner * sin_signed


def _both_halves(t, odd):
    lo = lax.broadcasted_iota(jnp.int32, t.shape, 1) < 64
    rolled = pltpu.roll(t, 64, 1)
    return jnp.where(lo, rolled, t) if odd else jnp.where(lo, t, rolled)


def _stack_heads(ta, tb):
    lo = lax.broadcasted_iota(jnp.int32, ta.shape, 1) < 64
    return jnp.concatenate([jnp.where(lo, ta, 0.0), jnp.where(lo, 0.0, ta),
                            jnp.where(lo, tb, 0.0), jnp.where(lo, 0.0, tb)], axis=0)


def _unstack_heads(o):
    lo = lax.broadcasted_iota(jnp.int32, (ATTN_BLOCK, 128), 1) < 64
    return (jnp.where(lo, o[0:128], o[128:256]), jnp.where(lo, o[256:384], o[384:512]))


def _attn_probs(qs, kd, sinks_ref, hk, first_block):
    s = _dot_nt(qs, kd) * (HEAD_DIM ** -0.5)
    row = lax.broadcasted_iota(jnp.int32, s.shape, 0)
    si = lax.broadcasted_iota(jnp.int32, s.shape, 1)
    diff = ATTN_BLOCK + (row & (ATTN_BLOCK - 1)) - si
    valid = (diff >= 0) & (diff < ATTN_BLOCK) & ((si >= ATTN_BLOCK) | jnp.logical_not(first_block))
    s = jnp.where(valid, s, MASK_VALUE)
    rg = lax.broadcasted_iota(jnp.int32, (4 * ATTN_BLOCK, 1), 0) >> 7
    sink = jnp.where(rg == 0, sinks_ref[4 * hk],
                     jnp.where(rg == 1, sinks_ref[4 * hk + 1],
                               jnp.where(rg == 2, sinks_ref[4 * hk + 2], sinks_ref[4 * hk + 3])))
    m = jnp.maximum(jnp.max(s, axis=1, keepdims=True), sink)
    e = jnp.exp(s - m)
    es = jnp.exp(sink - m)
    inv = 1.0 / (jnp.sum(e, axis=1, keepdims=True) + es)
    return e * inv, es * inv


def _prev(i):
    return jnp.maximum(i - 1, 0)


def attn_fwd(q, k, v, cos, sin_signed, sinks, name):
    nb = ATTN_BLOCK

    def body(q_ref, kc_ref, kp_ref, vc_ref, vp_ref, cc_ref, sc_ref, cp_ref, sp_ref, sinks_ref,
             qr_ref, kr_ref, y_ref):
        first_block = pl.program_id(0) == 0
        qr = _rope(q_ref[...], cc_ref[...], sc_ref[...])
        kc = _rope(kc_ref[...], cc_ref[...], sc_ref[...])
        kp = _rope(kp_ref[...], cp_ref[...], sp_ref[...])
        qr_ref[...] = qr.astype(BF16)
        kr_ref[...] = kc.astype(BF16)
        k2 = jnp.concatenate([kp, kc], axis=0)
        v2 = jnp.concatenate([vp_ref[...].astype(F32), vc_ref[...].astype(F32)], axis=0)
        for hk in range(N_KV_HEADS):
            kt = hk // 2
            kd = _both_halves(k2[:, kt * 128:(kt + 1) * 128], hk % 2).astype(BF16)
            vd = _both_halves(v2[:, kt * 128:(kt + 1) * 128], hk % 2).astype(BF16)
            qs = _stack_heads(qr[:, (2 * hk) * 128:(2 * hk + 1) * 128],
                              qr[:, (2 * hk + 1) * 128:(2 * hk + 2) * 128]).astype(BF16)
            p, _ = _attn_probs(qs, kd, sinks_ref, hk, first_block)
            ta, tb = _unstack_heads(_dot(p.astype(BF16), vd))
            y_ref[:, (2 * hk) * 128:(2 * hk + 1) * 128] = ta.astype(BF16)
            y_ref[:, (2 * hk + 1) * 128:(2 * hk + 2) * 128] = tb.astype(BF16)

    cur = lambda w: pl.BlockSpec((nb, w), lambda i: (i, 0))
    prv = lambda w: pl.BlockSpec((nb, w), lambda i: (_prev(i), 0))
    return pl.pallas_call(
        body, name=name, grid=(N_ATTN_BLOCKS,),
        in_specs=[cur(D_MODEL), cur(KV_W), prv(KV_W), cur(KV_W), prv(KV_W), cur(128), cur(128), prv(128), prv(128),
                  pl.BlockSpec(memory_space=pltpu.SMEM)],
        out_specs=[cur(D_MODEL), cur(KV_W), cur(D_MODEL)],
        out_shape=[jax.ShapeDtypeStruct((SEQ, D_MODEL), BF16), jax.ShapeDtypeStruct((SEQ, KV_W), BF16),
                   jax.ShapeDtypeStruct((SEQ, D_MODEL), BF16)],
        compiler_params=_params(1),
    )(q, k, k, v, v, cos, sin_signed, cos, sin_signed, sinks)


def attn_bwd(qr, kr, v, dy, cos, sin_signed, sinks, name):
    nb = ATTN_BLOCK
    n_steps = N_ATTN_BLOCKS + 1
    scale = HEAD_DIM ** -0.5

    def body(q_ref, kc_ref, kp_ref, vc_ref, vp_ref, dy_ref, cc_ref, sc_ref, cp_ref, sp_ref, sinks_ref,
             dq_ref, dk_ref, dv_ref, dsk_ref, ck_ref, cv_ref):
        i = pl.program_id(0)

        @pl.when(i == 0)
        def _():
            dsk_ref[...] = jnp.zeros_like(dsk_ref)
            ck_ref[...] = jnp.zeros_like(ck_ref)
            cv_ref[...] = jnp.zeros_like(cv_ref)

        @pl.when(i < N_ATTN_BLOCKS)
        def _():
            qv = q_ref[...].astype(F32)
            dov = dy_ref[...].astype(F32)
            k2 = jnp.concatenate([kp_ref[...].astype(F32), kc_ref[...].astype(F32)], axis=0)
            v2 = jnp.concatenate([vp_ref[...].astype(F32), vc_ref[...].astype(F32)], axis=0)
            lane = lax.broadcasted_iota(jnp.int32, (8, 128), 1)
            lo = lax.broadcasted_iota(jnp.int32, (2 * nb, 128), 1) < 64
            dsk = jnp.zeros((8, 128), F32)
            dk_tiles = []
            dv_tiles = []
            for hk in range(N_KV_HEADS):
                kt = hk // 2
                kd = _both_halves(k2[:, kt * 128:(kt + 1) * 128], hk % 2).astype(BF16)
                vd = _both_halves(v2[:, kt * 128:(kt + 1) * 128], hk % 2).astype(BF16)
                qs = _stack_heads(qv[:, (2 * hk) * 128:(2 * hk + 1) * 128],
                                  qv[:, (2 * hk + 1) * 128:(2 * hk + 2) * 128]).astype(BF16)
                dos = _stack_heads(dov[:, (2 * hk) * 128:(2 * hk + 1) * 128],
                                   dov[:, (2 * hk + 1) * 128:(2 * hk + 2) * 128]).astype(BF16)
                p, ps = _attn_probs(qs, kd, sinks_ref, hk, i == 0)
                dp = _dot_nt(dos, vd)
                delta = jnp.sum(p * dp, axis=1, keepdims=True)
                ds = (p * (dp - delta)).astype(BF16)
                dsink = -ps * delta
                for g in range(4):
                    dsk = dsk + jnp.where(lane == 4 * hk + g, jnp.sum(dsink[g * nb:(g + 1) * nb]), 0.0)
                ta, tb = _unstack_heads(_dot(ds, kd) * scale)
                dq_a = (2 * hk) * 128
                dq_ref[:, dq_a:dq_a + 128] = _rope(ta, cc_ref[...], -sc_ref[...]).astype(BF16)
                dq_ref[:, dq_a + 128:dq_a + 256] = _rope(tb, cc_ref[...], -sc_ref[...]).astype(BF16)
                rk = _dot_tn(ds, qs) * scale
                rv = _dot_tn(p.astype(BF16), dos)
                dk_tiles.append(rk + pltpu.roll(rk, 64, 1))
                dv_tiles.append(rv + pltpu.roll(rv, 64, 1))
            dsk_ref[...] += dsk
            dk_full = jnp.concatenate([jnp.where(lo, dk_tiles[0], dk_tiles[1]),
                                       jnp.where(lo, dk_tiles[2], dk_tiles[3])], axis=1)
            dv_full = jnp.concatenate([jnp.where(lo, dv_tiles[0], dv_tiles[1]),
                                       jnp.where(lo, dv_tiles[2], dv_tiles[3])], axis=1)
            dk_ref[...] = _rope(ck_ref[...] + dk_full[0:nb], cp_ref[...], -sp_ref[...]).astype(BF16)
            dv_ref[...] = (cv_ref[...] + dv_full[0:nb]).astype(BF16)
            ck_ref[...] = dk_full[nb:2 * nb]
            cv_ref[...] = dv_full[nb:2 * nb]

        @pl.when(i == N_ATTN_BLOCKS)
        def _():
            dk_ref[...] = _rope(ck_ref[...], cp_ref[...], -sp_ref[...]).astype(BF16)
            dv_ref[...] = cv_ref[...].astype(BF16)

    qi = lambda i: jnp.minimum(i, N_ATTN_BLOCKS - 1)
    cur = lambda w: pl.BlockSpec((nb, w), lambda i: (qi(i), 0))
    prv = lambda w: pl.BlockSpec((nb, w), lambda i: (_prev(qi(i)), 0))
    out_prev = lambda w: pl.BlockSpec((nb, w), lambda i: (_prev(i), 0))
    return pl.pallas_call(
        body, name=name, grid=(n_steps,),
        in_specs=[cur(D_MODEL), cur(KV_W), prv(KV_W), cur(KV_W), prv(KV_W), cur(D_MODEL),
                  cur(128), cur(128), out_prev(128), out_prev(128), pl.BlockSpec(memory_space=pltpu.SMEM)],
        out_specs=[cur(D_MODEL), out_prev(KV_W), out_prev(KV_W), pl.BlockSpec((8, 128), lambda i: (0, 0))],
        out_shape=[jax.ShapeDtypeStruct((SEQ, D_MODEL), BF16), jax.ShapeDtypeStruct((SEQ, KV_W), BF16),
                   jax.ShapeDtypeStruct((SEQ, KV_W), BF16), jax.ShapeDtypeStruct((8, 128), F32)],
        scratch_shapes=[pltpu.VMEM((nb, KV_W), F32), pltpu.VMEM((nb, KV_W), F32)],
        compiler_params=_params(1),
    )(qr, kr, kr, v, v, dy, cos, sin_signed, cos, sin_signed, sinks)


def _proj_scratch():
    return [pltpu.VMEM((D_MODEL, D_MODEL), BF16)] * 3 + [pltpu.SemaphoreType.DMA((3 * N_CHIPS,))]


def _load_projs(wf_ref, wl_ref, wa_ref, wo_ref, sem):
    _load_weight(wf_ref, 'w_proj_lru', wl_ref, sem.at[pl.ds(0, N_CHIPS)])
    _load_weight(wf_ref, 'w_proj_attn', wa_ref, sem.at[pl.ds(N_CHIPS, N_CHIPS)])
    _load_weight(wf_ref, 'w_out', wo_ref, sem.at[pl.ds(2 * N_CHIPS, N_CHIPS)])


def merge_fwd(y_lru, y_attn, g_lru, g_attn, wf, g_post, h_in, name):
    tm = 256

    def body(yl_ref, ya_ref, gl_ref, ga_ref, wf_ref, gp_ref, h_ref,
             pl_ref, pa_ref, mg_ref, m_ref, o_ref, wl_ref, wa_ref, wo_ref, sem):
        @pl.when(pl.program_id(0) == 0)
        def _():
            _load_projs(wf_ref, wl_ref, wa_ref, wo_ref, sem)

        p_l = _dot(yl_ref[...], wl_ref[...])
        p_a = _dot(ya_ref[...], wa_ref[...])
        pl_ref[...] = p_l.astype(BF16)
        pa_ref[...] = p_a.astype(BF16)
        merged = (_sigmoid(gl_ref[...]) * p_l + _sigmoid(ga_ref[...]) * p_a).astype(BF16)
        mg_ref[...] = merged
        m = _dot(merged, wo_ref[...])
        m_ref[...] = m
        o_ref[...] = h_ref[...] + m * _rsqrt_mean_sq(m) * gp_ref[...]

    row = _ROW(tm)
    return pl.pallas_call(
        body, name=name, grid=(SEQ // tm,),
        in_specs=[row, row, row, row, ANY, _VEC, row],
        out_specs=[row] * 5,
        out_shape=[jax.ShapeDtypeStruct((SEQ, D_MODEL), BF16)] * 3 + [jax.ShapeDtypeStruct((SEQ, D_MODEL), F32)] * 2,
        scratch_shapes=_proj_scratch(),
        compiler_params=_params(1),
    )(y_lru, y_attn, g_lru, g_attn, wf, g_post, h_in)


def merge_bwd(d_out, m, g_post, wf, g_lru, g_attn, p_l, p_a, name):
    tm = 256

    def body(do_ref, m_ref, gp_ref, wf_ref, gl_ref, ga_ref, pl_ref, pa_ref,
             dm_ref, dpl_ref, dpa_ref, dgl_ref, dga_ref, dya_ref, dyl_ref, dgp_ref, wl_ref, wa_ref, wo_ref, sem):
        @pl.when(pl.program_id(0) == 0)
        def _():
            _load_projs(wf_ref, wl_ref, wa_ref, wo_ref, sem)
            dgp_ref[...] = jnp.zeros_like(dgp_ref)

        mv = m_ref[...]
        rm = _rsqrt_mean_sq(mv)
        mh = mv * rm
        dn = do_ref[...]
        dgp_ref[...] += jnp.sum(dn * mh, axis=0, keepdims=True)
        t = dn * gp_ref[...]
        dm = (rm * (t - mh * jnp.mean(t * mh, axis=-1, keepdims=True))).astype(BF16)
        dm_ref[...] = dm
        dmg = _dot_nt(dm, wo_ref[...])
        sl = _sigmoid(gl_ref[...])
        sa = _sigmoid(ga_ref[...])
        dpl = (dmg * sl).astype(BF16)
        dpa = (dmg * sa).astype(BF16)
        dpl_ref[...] = dpl
        dpa_ref[...] = dpa
        dgl_ref[...] = (dmg * pl_ref[...].astype(F32) * sl * (1.0 - sl)).astype(BF16)
        dga_ref[...] = (dmg * pa_ref[...].astype(F32) * sa * (1.0 - sa)).astype(BF16)
        dyl_ref[...] = _dot_nt(dpl, wl_ref[...])
        dya_ref[...] = _dot_nt(dpa, wa_ref[...]).astype(BF16)

    row = _ROW(tm)
    return pl.pallas_call(
        body, name=name, grid=(SEQ // tm,),
        in_specs=[row, row, _VEC, ANY, row, row, row, row],
        out_specs=[row] * 7 + [_VEC],
        out_shape=[jax.ShapeDtypeStruct((SEQ, D_MODEL), BF16)] * 6 + [jax.ShapeDtypeStruct((SEQ, D_MODEL), F32),
                                                                       jax.ShapeDtypeStruct((1, D_MODEL), F32)],
        scratch_shapes=_proj_scratch(),
        compiler_params=_params(1),
    )(d_out, m, g_post, wf, g_lru, g_attn, p_l, p_a)


def _rope_tables():
    half = HEAD_DIM // 2
    inv_freq = ROPE_THETA ** (-jnp.arange(half, dtype=F32) / half)
    ang = jnp.arange(SEQ, dtype=F32)[:, None] * inv_freq[None, :]
    cos, sin = jnp.cos(ang), jnp.sin(ang)
    return jnp.tile(jnp.concatenate([cos, cos], axis=1), (1, 2)), jnp.tile(jnp.concatenate([-sin, sin], axis=1), (1, 2))


def _block_diag(w):
    per = LRU_TC // LRU_BLOCK_W
    w4 = w.reshape(LRU_W // LRU_TC, per, LRU_BLOCK_W, LRU_BLOCK_W)
    eye = jnp.eye(per, dtype=w.dtype)
    return jnp.einsum('jacd,ab->jacbd', w4, eye).reshape(LRU_W // LRU_TC, LRU_TC, LRU_TC).astype(BF16)


def _diag_blocks(p):
    per = LRU_TC // LRU_BLOCK_W
    p5 = p.reshape(LRU_W // LRU_TC, per, LRU_BLOCK_W, per, LRU_BLOCK_W)
    return jnp.stack([p5[:, a, :, a, :] for a in range(per)], axis=1).reshape(LRU_W // LRU_BLOCK_W, LRU_BLOCK_W, LRU_BLOCK_W)


def local_step(x, target, sw, wf):
    cos, sin_signed = _rope_tables()
    wa_bd = _block_diag(sw['lru_w_a'])
    wx_bd = _block_diag(sw['lru_w_x'])
    sinks = sw['attn_sinks'].reshape(N_Q_HEADS)

    n1, g1, u1, a1 = ffn_fwd_a(x, sw['ffn1_pre_g'], wf, 'ffn1_w_gu', 'ffn1_fwd_a')
    f1, h1 = ffn_fwd_b(a1, wf, 'ffn1_w_down', sw['ffn1_post_g'], x, 'ffn1_fwd_b')
    um, gate, xbr, q, k, v, g_lru, g_attn = mix_in(h1, sw['mix_pre_g'], wf, 'mix_in')
    y_lru, h_lru = lru_fwd(gate, xbr, sw['conv_w'], sw['conv_b'], wa_bd, sw['lru_b_a'], wx_bd, sw['lru_b_x'],
                           sw['lru_lambda'], 'lru_fwd')
    qr, kr, y_attn = attn_fwd(q, k, v, cos, sin_signed, sinks, 'attn_fwd')
    p_l, p_a, merged, m, h2 = merge_fwd(y_lru, y_attn, g_lru, g_attn, wf, sw['mix_post_g'], h1, 'merge_fwd')
    n2, g2, u2, a2 = ffn_fwd_a(h2, sw['ffn2_pre_g'], wf, 'ffn2_w_gu', 'ffn2_fwd_a')
    f2, dy, loss_blk = ffn_fwd_b(a2, wf, 'ffn2_w_down', sw['ffn2_post_g'], h2, 'ffn2_fwd_b', target=target)

    gs, gb = {}, {}
    df2, dgu2, gs['ffn2_post_g'] = ffn_bwd_a(dy, f2, sw['ffn2_post_g'], wf, 'ffn2_w_down', g2, u2, 'ffn2_bwd_a')
    gb['ffn2_w_down'] = mm_tn([a2], df2, 256, 'ffn2_dw_down')
    gb['ffn2_w_gu'] = mm_tn([dgu2], n2, 512, 'ffn2_dw_gu')
    dh2, gs['ffn2_pre_g'] = norm_bwd([dgu2], wf, 'ffn2_w_gu', h2, sw['ffn2_pre_g'], dy, 'ffn2_bwd_b')

    dm, dpl, dpa, dgl, dga, dya, dyl, gs['mix_post_g'] = merge_bwd(
        dh2, m, sw['mix_post_g'], wf, g_lru, g_attn, p_l, p_a, 'merge_bwd')
    gb['w_out'] = mm_tn([merged], dm, 512, 'dw_out')
    gb['w_proj_lru'] = mm_tn([y_lru], dpl, 512, 'dw_proj_lru')
    gb['w_proj_attn'] = mm_tn([y_attn], dpa, 512, 'dw_proj_attn')
    dq, dk, dv, dsk = attn_bwd(qr, kr, v, dya, cos, sin_signed, sinks, 'attn_bwd')
    gs['attn_sinks'] = dsk[0:1, 0:N_Q_HEADS]
    dgate, dxbr, vecs, dwa, dwx = lru_bwd(gate, xbr, h_lru, dyl, sw['conv_w'], sw['conv_b'], wa_bd, sw['lru_b_a'],
                                           wx_bd, sw['lru_b_x'], sw['lru_lambda'], 'lru_bwd')
    gs['conv_w'] = vecs[0:4]
    gs['conv_b'], gs['lru_b_a'], gs['lru_b_x'], gs['lru_lambda'] = vecs[4:5], vecs[5:6], vecs[6:7], vecs[7:8]
    gs['lru_w_a'] = _diag_blocks(dwa)
    gs['lru_w_x'] = _diag_blocks(dwx)
    dz = [dgate, dxbr, dq, dk, dv, dgl, dga]
    gb['w_in'] = mm_tn(dz, um, 256, 'dw_in')
    dh1, gs['mix_pre_g'] = norm_bwd(dz, wf, 'w_in', h1, sw['mix_pre_g'], dh2, 'mix_bwd_in')

    df1, dgu1, gs['ffn1_post_g'] = ffn_bwd_a(dh1, f1, sw['ffn1_post_g'], wf, 'ffn1_w_down', g1, u1, 'ffn1_bwd_a')
    gb['ffn1_w_down'] = mm_tn([a1], df1, 256, 'ffn1_dw_down')
    gb['ffn1_w_gu'] = mm_tn([dgu1], n1, 512, 'ffn1_dw_gu')
    dx, gs['ffn1_pre_g'] = norm_bwd([dgu1], wf, 'ffn1_w_gu', x, sw['ffn1_pre_g'], dh1, 'ffn1_bwd_b')
    return loss_blk, dx, gs, gb


def _place():
    x, y, c = lax.axis_index('x'), lax.axis_index('y'), lax.axis_index('c')
    chips = [(1 - x, y), (x, 1 - y), (1 - x, 1 - y)]
    return x, y, c, chips


def _rcopy(src, dst, send_sem, recv_sem, to):
    return pltpu.make_async_remote_copy(src_ref=src, dst_ref=dst, send_sem=send_sem, recv_sem=recv_sem,
                                        device_id=to, device_id_type=MESH)


def ag_pack(pack):
    ch = HALF_ROWS // AG_CHUNKS
    n_ici = 3 * AG_CHUNKS

    def body(p_ref, o_ref, send, recv, lsem):
        x, y, c, chips = _place()
        me_q = 2 * x + y
        sib = (x, y, 1 - c)

        def rows(cc, k):
            return pl.ds(pl.multiple_of(cc * HALF_ROWS + k * ch, 16), ch)

        own = pltpu.make_async_copy(p_ref, o_ref.at[me_q], lsem)
        own.start()
        sends = []
        for k in range(AG_CHUNKS):
            for j, (cx, cy) in enumerate(chips):
                cp = _rcopy(p_ref.at[rows(c, k)], o_ref.at[me_q, rows(c, k)],
                            send.at[j * AG_CHUNKS + k], recv.at[j * AG_CHUNKS + k], (cx, cy, c))
                cp.start()
                sends.append(cp)
        for k in range(AG_CHUNKS):
            for j, (cx, cy) in enumerate(chips):
                blk = o_ref.at[2 * cx + cy, rows(c, k)]
                _rcopy(blk, blk, send.at[j * AG_CHUNKS + k], recv.at[j * AG_CHUNKS + k], (cx, cy, c)).wait_recv()
                cp = _rcopy(blk, blk, send.at[n_ici + j * AG_CHUNKS + k], recv.at[n_ici + j * AG_CHUNKS + k], sib)
                cp.start()
                sends.append(cp)
        for k in range(AG_CHUNKS):
            for j, (cx, cy) in enumerate(chips):
                blk = o_ref.at[2 * cx + cy, rows(1 - c, k)]
                _rcopy(blk, blk, send.at[n_ici + j * AG_CHUNKS + k], recv.at[n_ici + j * AG_CHUNKS + k], sib).wait_recv()
        for cp in sends:
            cp.wait_send()
        own.wait()

    return pl.pallas_call(
        body, name='ag_pack',
        out_shape=jax.ShapeDtypeStruct((N_CHIPS, PACK_ROWS, D_MODEL), BF16),
        in_specs=[ANY], out_specs=ANY,
        scratch_shapes=[pltpu.SemaphoreType.DMA((2 * n_ici,)), pltpu.SemaphoreType.DMA((2 * n_ici,)),
                        pltpu.SemaphoreType.DMA],
    )(pack)


def ag_small(blk, name):
    def body(x_ref, o_ref, send, recv, lsem):
        x, y, c, chips = _place()
        sib = (x, y, 1 - c)

        def slot(px, py, pc):
            return o_ref.at[4 * px + 2 * py + pc]

        own = pltpu.make_async_copy(x_ref, slot(x, y, c), lsem)
        own.start()
        first = [_rcopy(x_ref, slot(x, y, c), send.at[0], recv.at[0], sib)]
        first += [_rcopy(x_ref, slot(x, y, c), send.at[1 + j], recv.at[1 + j], (cx, cy, c))
                  for j, (cx, cy) in enumerate(chips)]
        for cp in first:
            cp.start()
        passed = []
        for j, (cx, cy) in enumerate(chips):
            blk_ref = slot(cx, cy, c)
            _rcopy(blk_ref, blk_ref, send.at[1 + j], recv.at[1 + j], (cx, cy, c)).wait_recv()
            cp = _rcopy(blk_ref, blk_ref, send.at[4 + j], recv.at[4 + j], sib)
            cp.start()
            passed.append(cp)
        sb = slot(x, y, 1 - c)
        _rcopy(sb, sb, send.at[0], recv.at[0], sib).wait_recv()
        for j, (cx, cy) in enumerate(chips):
            blk_ref = slot(cx, cy, 1 - c)
            _rcopy(blk_ref, blk_ref, send.at[4 + j], recv.at[4 + j], sib).wait_recv()
        for cp in first + passed:
            cp.wait_send()
        own.wait()

    return pl.pallas_call(
        body, name=name,
        out_shape=jax.ShapeDtypeStruct((N_DEV,) + blk.shape, blk.dtype),
        in_specs=[ANY], out_specs=ANY,
        scratch_shapes=[pltpu.SemaphoreType.DMA((7,)), pltpu.SemaphoreType.DMA((7,)), pltpu.SemaphoreType.DMA],
    )(blk)


def rs_pair(grads):
    names = [n for n, _, _ in PACK]

    def body(*refs):
        g_refs = refs[:len(names)]
        land_ref, send, recv = refs[len(names):]
        x, y, c, _ = _place()
        sib = (x, y, 1 - c)
        cps = []
        for w, (name, rows, _) in enumerate(PACK):
            hr = rows // 2
            o2 = PACK_OFF[name] // 2
            for q in range(N_CHIPS):
                cp = _rcopy(g_refs[w].at[q, pl.ds(1 - c, 1)], land_ref.at[q, :, pl.ds(o2, hr)],
                            send.at[w * N_CHIPS + q], recv.at[w * N_CHIPS + q], sib)
                cp.start()
                cps.append(cp)
        for cp in cps:
            cp.wait()

    n_cp = len(names) * N_CHIPS
    return pl.pallas_call(
        body, name='rs_pair',
        out_shape=jax.ShapeDtypeStruct((N_CHIPS, 1, HALF_ROWS, D_MODEL), BF16),
        in_specs=[ANY] * len(names), out_specs=ANY,
        scratch_shapes=[pltpu.SemaphoreType.DMA((n_cp,)), pltpu.SemaphoreType.DMA((n_cp,))],
    )(*[grads[n] for n in names])


def pair_sum(g4, land, c_arr, off2, name):
    hr = g4.shape[2]

    def body(c_ref, g_ref, l_ref, o_ref):
        o_ref[0] = (g_ref[0, 0].astype(F32) + l_ref[0, 0].astype(F32)).astype(BF16)

    return pl.pallas_call(
        body, name=name,
        grid_spec=pltpu.PrefetchScalarGridSpec(
            num_scalar_prefetch=1, grid=(N_CHIPS,),
            in_specs=[pl.BlockSpec((1, 1, hr, D_MODEL), lambda q, c: (q, c[0], 0, 0)),
                      pl.BlockSpec((1, 1, hr, D_MODEL), lambda q, c: (q, 0, off2 // hr, 0))],
            out_specs=pl.BlockSpec((1, hr, D_MODEL), lambda q, c: (q, 0, 0))),
        out_shape=jax.ShapeDtypeStruct((N_CHIPS, hr, D_MODEL), BF16),
        compiler_params=_params(1),
    )(c_arr, g4, land)


def rs_chips(sums):
    names = [n for n, _, _ in PACK]

    def body(*refs):
        s_refs = refs[:len(names)]
        land_ref, send, recv, lsem = refs[len(names):]
        x, y, c, chips = _place()
        me_q = 2 * x + y
        cps, locs = [], []
        for w, (name, rows, _) in enumerate(PACK):
            hr = rows // 2
            o2 = PACK_OFF[name] // 2
            loc = pltpu.make_async_copy(s_refs[w].at[me_q], land_ref.at[me_q, pl.ds(o2, hr)], lsem.at[w])
            loc.start()
            locs.append(loc)
            for j, (cx, cy) in enumerate(chips):
                cp = _rcopy(s_refs[w].at[2 * cx + cy], land_ref.at[me_q, pl.ds(o2, hr)],
                            send.at[w * 3 + j], recv.at[w * 3 + j], (cx, cy, c))
                cp.start()
                cps.append(cp)
        for w, (name, rows, _) in enumerate(PACK):
            hr = rows // 2
            o2 = PACK_OFF[name] // 2
            for j, (cx, cy) in enumerate(chips):
                blk = land_ref.at[2 * cx + cy, pl.ds(o2, hr)]
                _rcopy(blk, blk, send.at[w * 3 + j], recv.at[w * 3 + j], (cx, cy, c)).wait_recv()
        for cp in cps:
            cp.wait_send()
        for loc in locs:
            loc.wait()

    n_cp = len(names) * 3
    return pl.pallas_call(
        body, name='rs_chips',
        out_shape=jax.ShapeDtypeStruct((N_CHIPS, HALF_ROWS, D_MODEL), BF16),
        in_specs=[ANY] * len(names), out_specs=ANY,
        scratch_shapes=[pltpu.SemaphoreType.DMA((n_cp,)), pltpu.SemaphoreType.DMA((n_cp,)),
                        pltpu.SemaphoreType.DMA((len(names),))],
    )(*[sums[n] for n in names])


def chip_sum(land):
    tr = 128

    def body(l_ref, o_ref):
        acc = l_ref[0].astype(F32)
        for s in range(1, N_CHIPS):
            acc = acc + l_ref[s].astype(F32)
        o_ref[...] = acc

    return pl.pallas_call(
        body, name='chip_sum', grid=(HALF_ROWS // tr,),
        in_specs=[pl.BlockSpec((N_CHIPS, tr, D_MODEL), lambda i: (0, i, 0))],
        out_specs=pl.BlockSpec((tr, D_MODEL), lambda i: (i, 0)),
        out_shape=jax.ShapeDtypeStruct((HALF_ROWS, D_MODEL), F32),
        compiler_params=_params(1),
    )(land)


def ag_pair(half):
    n_ch = PAIR_CHUNKS
    ch = HALF_ROWS // n_ch

    def body(h_ref, o_ref, send, recv, lsem):
        x, y, c, _ = _place()
        cps = []
        for k in range(n_ch):
            rows = pl.ds(k * ch, ch)
            own = pltpu.make_async_copy(h_ref.at[rows], o_ref.at[c, rows], lsem.at[k])
            own.start()
            cp = _rcopy(h_ref.at[rows], o_ref.at[c, rows], send.at[k], recv.at[k], (x, y, 1 - c))
            cp.start()
            cps += [own, cp]
        for cp in cps:
            cp.wait()

    return pl.pallas_call(
        body, name='ag_pair',
        out_shape=jax.ShapeDtypeStruct((2, HALF_ROWS, D_MODEL), F32),
        in_specs=[ANY], out_specs=ANY,
        scratch_shapes=[pltpu.SemaphoreType.DMA((n_ch,)), pltpu.SemaphoreType.DMA((n_ch,)),
                        pltpu.SemaphoreType.DMA((n_ch,))],
    )(half)


def small_sum(parts):
    def body(p_ref, o_ref):
        acc = p_ref[0]
        for s in range(1, N_DEV):
            acc = acc + p_ref[s]
        o_ref[...] = acc

    return pl.pallas_call(
        body, name='small_sum', grid=(1,),
        in_specs=[pl.BlockSpec(parts.shape, lambda i: (0, 0, 0))],
        out_specs=pl.BlockSpec(parts.shape[1:], lambda i: (0, 0)),
        out_shape=jax.ShapeDtypeStruct(parts.shape[1:], F32),
        compiler_params=_params(1),
    )(parts)


def _adam_math(w, g, m, v):
    m2 = ADAM_B1 * m + (1.0 - ADAM_B1) * g
    v2 = ADAM_B2 * v + (1.0 - ADAM_B2) * (g * g)
    m_hat = m2 / (1.0 - ADAM_B1 ** ADAM_STEP)
    v_hat = v2 / (1.0 - ADAM_B2 ** ADAM_STEP)
    delta = -ADAM_LR * (m_hat / (jnp.sqrt(v_hat) + ADAM_EPS) + ADAM_WD * w)
    return delta, m2, v2


def _adam_body(transposed):
    def body(g_ref, w_ref, m_ref, v_ref, go_ref, d_ref, mo_ref, vo_ref):
        if transposed:
            gt = g_ref[...]
            g = gt.reshape(gt.shape[0] * gt.shape[1], gt.shape[2]).T
        else:
            g = g_ref[0]
        go_ref[...] = g
        d_ref[...], mo_ref[...], vo_ref[...] = _adam_math(w_ref[...], g, m_ref[...], v_ref[...])
    return body


def adam_rows(full, name, w, m, v):
    hr = w.shape[0] // 2
    ob = (PACK_OFF[name] // 2) // hr
    blk = pl.BlockSpec((hr, D_MODEL), lambda h: (h, 0))
    return pl.pallas_call(
        _adam_body(False), name='adam_' + name, grid=(2,),
        in_specs=[pl.BlockSpec((1, hr, D_MODEL), lambda h: (h, ob, 0)), blk, blk, blk],
        out_specs=[blk] * 4,
        out_shape=[jax.ShapeDtypeStruct(w.shape, F32)] * 4,
        compiler_params=_params(1),
    )(full, w, m, v)


def adam_cols(full, name, w, m, v):
    cols = w.shape[1]
    hr = cols // 2
    ob = (PACK_OFF[name] // 2) // hr
    tr = 128
    blk = pl.BlockSpec((tr, cols), lambda i: (i, 0))
    return pl.pallas_call(
        _adam_body(True), name='adam_' + name, grid=(D_MODEL // tr,),
        in_specs=[pl.BlockSpec((2, hr, tr), lambda i: (0, ob, i)), blk, blk, blk],
        out_specs=[blk] * 4,
        out_shape=[jax.ShapeDtypeStruct(w.shape, F32)] * 4,
        compiler_params=_params(1),
    )(full, w, m, v)


def adam_small(g, w, m, v):
    def body(g_ref, w_ref, m_ref, v_ref, d_ref, mo_ref, vo_ref):
        d_ref[...], mo_ref[...], vo_ref[...] = _adam_math(w_ref[...], g_ref[...], m_ref[...], v_ref[...])

    blk = pl.BlockSpec(w.shape, lambda i: (0, 0))
    return pl.pallas_call(
        body, name='adam_small', grid=(1,), in_specs=[blk] * 4, out_specs=[blk] * 3,
        out_shape=[jax.ShapeDtypeStruct(w.shape, F32)] * 3, compiler_params=_params(1),
    )(g, w, m, v)


WEIGHTS = ('ffn1_pre_g', 'ffn1_w_gu', 'ffn1_w_down', 'ffn1_post_g', 'mix_pre_g', 'w_in', 'conv_w', 'conv_b',
           'lru_w_a', 'lru_b_a', 'lru_w_x', 'lru_b_x', 'lru_lambda', 'attn_sinks', 'w_proj_lru', 'w_proj_attn',
           'w_out', 'mix_post_g', 'ffn2_pre_g', 'ffn2_w_gu', 'ffn2_w_down', 'ffn2_post_g')
SMALL = tuple(n for n in WEIGHTS if n not in PACK_OFF)


def _pack_small(d, conv_rows):
    sinks = jnp.pad(d['attn_sinks'].reshape(1, N_Q_HEADS), ((0, 0), (0, D_MODEL - N_Q_HEADS)))
    conv = jnp.pad(conv_rows, ((0, ROW_WA - ROW_CONV - conv_rows.shape[0]), (0, 0)))
    return jnp.concatenate([d[n].reshape(1, D_MODEL) for n in SMALL_VECS] + [sinks, conv]
                           + [d['lru_w_a'].reshape(64, D_MODEL), d['lru_w_x'].reshape(64, D_MODEL)], axis=0)


def _unpack_small(p, shapes):
    out = {n: p[k:k + 1].reshape(shapes[n]) for k, n in enumerate(SMALL_VECS)}
    out['attn_sinks'] = p[ROW_SINKS:ROW_SINKS + 1, :N_Q_HEADS].reshape(shapes['attn_sinks'])
    out['conv_w'] = p[ROW_CONV:ROW_CONV + 1].reshape(shapes['conv_w'])
    out['lru_w_a'] = p[ROW_WA:ROW_WA + 64].reshape(shapes['lru_w_a'])
    out['lru_w_x'] = p[ROW_WX:ROW_WX + 64].reshape(shapes['lru_w_x'])
    return out


def kernel(x, ffn1_pre_g, ffn1_w_gu, ffn1_w_down, ffn1_post_g, mix_pre_g, w_in, conv_w, conv_b, lru_w_a, lru_b_a, lru_w_x, lru_b_x, lru_lambda, attn_sinks, w_proj_lru, w_proj_attn, w_out, mix_post_g, ffn2_pre_g, ffn2_w_gu, ffn2_w_down, ffn2_post_g, loss_target, m_ffn1_pre_g, m_ffn1_w_gu, m_ffn1_w_down, m_ffn1_post_g, m_mix_pre_g, m_w_in, m_conv_w, m_conv_b, m_lru_w_a, m_lru_b_a, m_lru_w_x, m_lru_b_x, m_lru_lambda, m_attn_sinks, m_w_proj_lru, m_w_proj_attn, m_w_out, m_mix_post_g, m_ffn2_pre_g, m_ffn2_w_gu, m_ffn2_w_down, m_ffn2_post_g, v_ffn1_pre_g, v_ffn1_w_gu, v_ffn1_w_down, v_ffn1_post_g, v_mix_pre_g, v_w_in, v_conv_w, v_conv_b, v_lru_w_a, v_lru_b_a, v_lru_w_x, v_lru_b_x, v_lru_lambda, v_attn_sinks, v_w_proj_lru, v_w_proj_attn, v_w_out, v_mix_post_g, v_ffn2_pre_g, v_ffn2_w_gu, v_ffn2_w_down, v_ffn2_post_g):
    given = dict(locals())
    w = {n: given[n] for n in WEIGHTS}
    mom = {n: given['m_' + n] for n in WEIGHTS}
    var = {n: given['v_' + n] for n in WEIGHTS}
    shapes = {n: w[n].shape for n in WEIGHTS}
    xq = lax.axis_index('x')
    yq = lax.axis_index('y')
    cq = lax.axis_index('c')
    me_q = 2 * xq + yq

    pack = jnp.concatenate([(w[n][0].T if t else w[n][0]) for n, _, t in PACK], axis=0).astype(BF16)
    wf = ag_pack(pack)
    conv_all = ag_small(jnp.pad(w['conv_w'][0], ((0, 4), (0, 0))), 'ag_conv')
    conv_full = jnp.transpose(conv_all[0::2, :4, :], (1, 0, 2)).reshape(4, LRU_W)

    sw = {n: (w[n][0] if w[n].ndim > 2 else w[n]) for n in SMALL}
    sw['conv_w'] = conv_full
    loss_blk, dx, gs, gb = local_step(x[0], loss_target[0], sw, wf)
    loss = lax.psum(loss_blk[0, 0], ('x', 'y', 'c'))

    g4 = {n: gb[n].reshape(N_CHIPS, 2, r // 2, D_MODEL) for n, r, _ in PACK}
    land = rs_pair(g4)
    c_arr = cq.reshape(1).astype(jnp.int32)
    sums = {n: pair_sum(g4[n], land, c_arr, PACK_OFF[n] // 2, 'pair_sum_' + n) for n, _, _ in PACK}
    full = ag_pair(chip_sum(rs_chips(sums)))

    out_g, out_d, out_m, out_v = {}, {}, {}, {}
    for n, _, t in PACK:
        fn = adam_cols if t else adam_rows
        g_, d_, m_, v_ = fn(full, n, w[n][0], mom[n][0], var[n][0])
        out_g[n], out_d[n], out_m[n], out_v[n] = g_[None], d_[None], m_[None], v_[None]

    tot = small_sum(ag_small(_pack_small(gs, gs['conv_w']), 'ag_small_grads'))
    conv_g = lax.dynamic_slice(tot[ROW_CONV:ROW_CONV + 4], (0, me_q * (LRU_W // N_CHIPS)), (4, LRU_W // N_CHIPS))
    small_g = _unpack_small(tot, shapes)
    small_g['conv_w'] = conv_g.reshape(shapes['conv_w'])
    g_pack = jnp.concatenate([tot[:ROW_CONV], conv_g.reshape(1, D_MODEL), jnp.zeros((ROW_WA - ROW_CONV - 1, D_MODEL), F32),
                              tot[ROW_WA:]], axis=0)
    packs = [_pack_small({n: d[n] for n in SMALL}, d['conv_w'].reshape(1, D_MODEL)) for d in (w, mom, var)]
    d_p, m_p, v_p = adam_small(g_pack, *packs)
    for n in SMALL:
        out_g[n] = small_g[n]
    for dst, p in ((out_d, d_p), (out_m, m_p), (out_v, v_p)):
        dst.update(_unpack_small(p, shapes))

    return (loss, dx[None], *[out_g[n] for n in WEIGHTS], *[out_d[n] for n in WEIGHTS],
            *[out_m[n] for n in WEIGHTS], *[out_v[n] for n in WEIGHTS])
```

```python
import jax
import jax.numpy as jnp
from jax import lax
from jax.experimental import pallas as pl
from jax.experimental.pallas import tpu as pltpu

F32 = jnp.float32
BF16 = jnp.bfloat16

SEQ = 2048
D_MODEL = 1024
D_FF = 2816
LRU_W = 1024
LRU_BLOCK_W = 64
HEAD_DIM = 64
N_Q_HEADS = 16
N_KV_HEADS = 4
KV_W = N_KV_HEADS * HEAD_DIM
ATTN_BLOCK = 128
N_ATTN_BLOCKS = SEQ // ATTN_BLOCK
IN_SEGS = (1024, 1024, 1024, 256, 256, 1024, 1024)
IN_W = sum(IN_SEGS)
NORM_EPS = 1e-6
MASK_VALUE = -1e30
ROPE_THETA = 10000.0
LRU_C = 8.0
MACARON = 0.5
ADAM_LR = 0.001
ADAM_B1 = 0.9
ADAM_B2 = 0.999
ADAM_EPS = 1e-08
ADAM_WD = 0.01
ADAM_STEP = 10

N_CHIPS = 4
N_DEV = 8
VMEM_LIMIT = 56 * 1024 * 1024
MESH = pl.DeviceIdType.MESH
ANY = pl.BlockSpec(memory_space=pl.ANY)

PACK = (('ffn1_w_gu', 1408, True), ('w_in', 1408, True), ('ffn2_w_gu', 1408, True),
        ('ffn1_w_down', 704, False), ('ffn2_w_down', 704, False),
        ('w_proj_lru', 256, False), ('w_proj_attn', 256, False), ('w_out', 256, False))
PACK_ROWS = sum(r for _, r, _ in PACK)
HALF_ROWS = PACK_ROWS // 2
PACK_OFF = {}
_o = 0
for _n, _r, _t in PACK:
    PACK_OFF[_n] = _o
    _o += _r
AG_CHUNKS = 20
PAIR_CHUNKS = 16

SMALL_VECS = ('ffn1_pre_g', 'ffn1_post_g', 'mix_pre_g', 'conv_b', 'lru_b_a', 'lru_b_x', 'lru_lambda',
              'mix_post_g', 'ffn2_pre_g', 'ffn2_post_g')
SMALL_ROWS = 144
ROW_SINKS, ROW_CONV, ROW_WA, ROW_WX = 10, 11, 16, 80


def _dot(a, b):
    return jnp.dot(a, b, preferred_element_type=F32)


def _dot_nt(a, b):
    return lax.dot_general(a, b, (((1,), (1,)), ((), ())), preferred_element_type=F32)


def _dot_tn(a, b):
    return lax.dot_general(a, b, (((0,), (0,)), ((), ())), preferred_element_type=F32)


def _params(n_grid):
    return pltpu.CompilerParams(dimension_semantics=("arbitrary",) * n_grid, vmem_limit_bytes=VMEM_LIMIT)


def _sigmoid(x):
    return 1.0 / (1.0 + jnp.exp(-x))


def _rsqrt_mean_sq(x):
    return lax.rsqrt(jnp.mean(x * x, axis=-1, keepdims=True) + NORM_EPS)


def _expm1(x):
    poly = x * (1.0 + x * (0.5 + x * (1.0 / 6.0 + x * (1.0 / 24.0 + x * (1.0 / 120.0)))))
    return jnp.where(jnp.abs(x) < 0.1, poly, jnp.exp(x) - 1.0)


_GELU_K = 0.7978845608028654
_GELU_C = 0.044715


def _gelu(x):
    t = jnp.tanh(_GELU_K * (x + _GELU_C * x * x * x))
    return 0.5 * x * (1.0 + t), t


def _gelu_grad(x, t):
    return 0.5 * (1.0 + t) + 0.5 * x * (1.0 - t * t) * _GELU_K * (1.0 + 3.0 * _GELU_C * x * x)


def _load_weight(wf_ref, name, dst_ref, sem):
    rows = dst_ref.shape[0] // N_CHIPS
    off = PACK_OFF[name]
    cps = [pltpu.make_async_copy(wf_ref.at[q, pl.ds(off, rows)], dst_ref.at[pl.ds(q * rows, rows)], sem.at[q])
           for q in range(N_CHIPS)]
    for cp in cps:
        cp.start()
    for cp in cps:
        cp.wait()


def _weight_scratch(rows_total):
    return [pltpu.VMEM((rows_total, D_MODEL), BF16), pltpu.SemaphoreType.DMA((N_CHIPS,))]


_ROW = lambda tm: pl.BlockSpec((tm, D_MODEL), lambda i: (i, 0))
_VEC = pl.BlockSpec((1, D_MODEL), lambda i: (0, 0))


def ffn_fwd_a(x, g_pre, wf, wname, name):
    tm, tn = 256, 256

    def body(x_ref, gp_ref, wf_ref, n_ref, g_ref, u_ref, a_ref, wt_ref, sem):
        @pl.when(pl.program_id(0) == 0)
        def _():
            _load_weight(wf_ref, wname, wt_ref, sem)

        xv = x_ref[...]
        n = (xv * _rsqrt_mean_sq(xv) * gp_ref[...]).astype(BF16)
        n_ref[...] = n
        for j in range(D_FF // tn):
            g = _dot_nt(n, wt_ref[j * tn:(j + 1) * tn, :])
            u = _dot_nt(n, wt_ref[D_FF + j * tn:D_FF + (j + 1) * tn, :])
            g_ref[:, j * tn:(j + 1) * tn] = g.astype(BF16)
            u_ref[:, j * tn:(j + 1) * tn] = u.astype(BF16)
            a_ref[:, j * tn:(j + 1) * tn] = (g * _sigmoid(g) * u).astype(BF16)

    wide = pl.BlockSpec((tm, D_FF), lambda i: (i, 0))
    return pl.pallas_call(
        body, name=name, grid=(SEQ // tm,),
        in_specs=[_ROW(tm), _VEC, ANY],
        out_specs=[_ROW(tm), wide, wide, wide],
        out_shape=[jax.ShapeDtypeStruct((SEQ, D_MODEL), BF16)] + [jax.ShapeDtypeStruct((SEQ, D_FF), BF16)] * 3,
        scratch_shapes=_weight_scratch(2 * D_FF),
        compiler_params=_params(1),
    )(x, g_pre, wf)


def ffn_fwd_b(a, wf, wname, g_post, h_in, name, target=None):
    tm = 256
    final = target is not None

    def body(*refs):
        if final:
            a_ref, wf_ref, gp_ref, h_ref, t_ref, f_ref, o_ref, loss_ref, wd_ref, sem = refs
        else:
            a_ref, wf_ref, gp_ref, h_ref, f_ref, o_ref, wd_ref, sem = refs

        @pl.when(pl.program_id(0) == 0)
        def _():
            _load_weight(wf_ref, wname, wd_ref, sem)
            if final:
                loss_ref[...] = jnp.zeros_like(loss_ref)

        f = _dot(a_ref[...], wd_ref[...])
        f_ref[...] = f
        y = h_ref[...] + MACARON * (f * _rsqrt_mean_sq(f) * gp_ref[...])
        if final:
            err = y - t_ref[...]
            o_ref[...] = err * (1.0 / D_MODEL)
            loss_ref[...] += 0.5 * jnp.sum(err * err) * (1.0 / D_MODEL)
        else:
            o_ref[...] = y

    row = _ROW(tm)
    in_specs = [pl.BlockSpec((tm, D_FF), lambda i: (i, 0)), ANY, _VEC, row]
    out_specs = [row, row]
    out_shape = [jax.ShapeDtypeStruct((SEQ, D_MODEL), F32)] * 2
    args = [a, wf, g_post, h_in]
    if final:
        in_specs.append(row)
        args.append(target)
        out_specs.append(pl.BlockSpec((8, 128), lambda i: (0, 0)))
        out_shape.append(jax.ShapeDtypeStruct((8, 128), F32))
    return pl.pallas_call(body, name=name, grid=(SEQ // tm,), in_specs=in_specs, out_specs=out_specs,
                          out_shape=out_shape, scratch_shapes=_weight_scratch(D_FF),
                          compiler_params=_params(1))(*args)


def ffn_bwd_a(d_out, f, g_post, wf, wname, g, u, name):
    tm = 256

    def body(do_ref, f_ref, gp_ref, wf_ref, g_ref, u_ref, df_ref, dgu_ref, dgp_ref, wd_ref, sem):
        @pl.when(pl.program_id(0) == 0)
        def _():
            _load_weight(wf_ref, wname, wd_ref, sem)
            dgp_ref[...] = jnp.zeros_like(dgp_ref)

        fv = f_ref[...]
        rf = _rsqrt_mean_sq(fv)
        fh = fv * rf
        dn = MACARON * do_ref[...]
        dgp_ref[...] += jnp.sum(dn * fh, axis=0, keepdims=True)
        t = dn * gp_ref[...]
        df = (rf * (t - fh * jnp.mean(t * fh, axis=-1, keepdims=True))).astype(BF16)
        df_ref[...] = df
        da = _dot_nt(df, wd_ref[...])
        gv = g_ref[...].astype(F32)
        uv = u_ref[...].astype(F32)
        s = _sigmoid(gv)
        dgu_ref[:, :D_FF] = (da * uv * s * (1.0 + gv * (1.0 - s))).astype(BF16)
        dgu_ref[:, D_FF:] = (da * gv * s).astype(BF16)

    row = _ROW(tm)
    wide = pl.BlockSpec((tm, D_FF), lambda i: (i, 0))
    return pl.pallas_call(
        body, name=name, grid=(SEQ // tm,),
        in_specs=[row, row, _VEC, ANY, wide, wide],
        out_specs=[row, pl.BlockSpec((tm, 2 * D_FF), lambda i: (i, 0)), _VEC],
        out_shape=[jax.ShapeDtypeStruct((SEQ, D_MODEL), BF16), jax.ShapeDtypeStruct((SEQ, 2 * D_FF), BF16),
                   jax.ShapeDtypeStruct((1, D_MODEL), F32)],
        scratch_shapes=_weight_scratch(D_FF),
        compiler_params=_params(1),
    )(d_out, f, g_post, wf, g, u)


def norm_bwd(pieces, wf, wname, x, g_pre, d_res, name):
    tm = 256
    widths = [p.shape[1] for p in pieces]
    offs = [sum(widths[:k]) for k in range(len(widths))]
    n_p = len(pieces)

    def body(*refs):
        p_refs = refs[:n_p]
        wf_ref, x_ref, g_ref, r_ref, dx_ref, dg_ref, wt_ref, sem = refs[n_p:]

        @pl.when(pl.program_id(0) == 0)
        def _():
            _load_weight(wf_ref, wname, wt_ref, sem)
            dg_ref[...] = jnp.zeros_like(dg_ref)

        dn = None
        for p_ref, lo, wd in zip(p_refs, offs, widths):
            part = _dot(p_ref[...], wt_ref[lo:lo + wd, :])
            dn = part if dn is None else dn + part
        xv = x_ref[...]
        r = _rsqrt_mean_sq(xv)
        xh = xv * r
        dg_ref[...] += jnp.sum(dn * xh, axis=0, keepdims=True)
        t = dn * g_ref[...]
        dx_ref[...] = r_ref[...] + r * (t - xh * jnp.mean(t * xh, axis=-1, keepdims=True))

    row = _ROW(tm)
    return pl.pallas_call(
        body, name=name, grid=(SEQ // tm,),
        in_specs=[pl.BlockSpec((tm, wd), lambda i: (i, 0)) for wd in widths] + [ANY, row, _VEC, row],
        out_specs=[row, _VEC],
        out_shape=[jax.ShapeDtypeStruct((SEQ, D_MODEL), F32), jax.ShapeDtypeStruct((1, D_MODEL), F32)],
        scratch_shapes=_weight_scratch(sum(widths)),
        compiler_params=_params(1),
    )(*pieces, wf, x, g_pre, d_res)


def mm_tn(pieces, b, tm, name):
    widths = [p.shape[1] for p in pieces]
    m_total = sum(widths)
    n_p = len(pieces)
    starts = [sum(widths[:k]) // tm for k in range(n_p)]
    counts = [wd // tm for wd in widths]

    def body(*refs):
        p_refs = refs[:n_p]
        b_ref, o_ref = refs[n_p:]
        i = pl.program_id(0)
        for p_ref, st, ct in zip(p_refs, starts, counts):
            @pl.when((i >= st) & (i < st + ct))
            def _(p_ref=p_ref):
                o_ref[...] = _dot_tn(p_ref[...], b_ref[...]).astype(BF16)

    def piece_spec(st, ct):
        return pl.BlockSpec((SEQ, tm), lambda i: (0, jnp.clip(i - st, 0, ct - 1)))

    return pl.pallas_call(
        body, name=name, grid=(m_total // tm,),
        in_specs=[piece_spec(st, ct) for st, ct in zip(starts, counts)] + [pl.BlockSpec((SEQ, D_MODEL), lambda i: (0, 0))],
        out_specs=pl.BlockSpec((tm, D_MODEL), lambda i: (i, 0)),
        out_shape=jax.ShapeDtypeStruct((m_total, D_MODEL), BF16),
        compiler_params=_params(1),
    )(*pieces, b)


def mix_in(h, g_pre, wf, name):
    tm = 256
    offs = [sum(IN_SEGS[:k]) for k in range(len(IN_SEGS))]
    dts = [F32, F32, F32, F32, BF16, F32, F32]
    n_o = len(IN_SEGS)

    def body(*refs):
        h_ref, g_ref, wf_ref, um_ref = refs[:4]
        o_refs = refs[4:4 + n_o]
        wt_ref, sem = refs[4 + n_o:]

        @pl.when(pl.program_id(0) == 0)
        def _():
            _load_weight(wf_ref, 'w_in', wt_ref, sem)

        hv = h_ref[...]
        um = (hv * _rsqrt_mean_sq(hv) * g_ref[...]).astype(BF16)
        um_ref[...] = um
        for o_ref, lo, wd in zip(o_refs, offs, IN_SEGS):
            for c0 in range(0, wd, 256):
                o_ref[:, c0:c0 + 256] = _dot_nt(um, wt_ref[lo + c0:lo + c0 + 256, :]).astype(o_ref.dtype)

    return pl.pallas_call(
        body, name=name, grid=(SEQ // tm,),
        in_specs=[_ROW(tm), _VEC, ANY],
        out_specs=[_ROW(tm)] + [pl.BlockSpec((tm, wd), lambda i: (i, 0)) for wd in IN_SEGS],
        out_shape=[jax.ShapeDtypeStruct((SEQ, D_MODEL), BF16)]
        + [jax.ShapeDtypeStruct((SEQ, wd), dt) for wd, dt in zip(IN_SEGS, dts)],
        scratch_shapes=_weight_scratch(IN_W),
        compiler_params=_params(1),
    )(h, g_pre, wf)


LRU_TC = 256


def _conv_fwd(xb, cw, cb, tt):
    xc = xb * cw[3:4, :] + cb
    shifted = []
    for s in (1, 2, 3):
        sh = jnp.where(tt >= s, pltpu.roll(xb, s, 0), 0.0)
        shifted.append(sh)
        xc = xc + sh * cw[3 - s:4 - s, :]
    return xc, shifted


def _lru_gates(xc, wa, ba, wx, bx, lam):
    xcb = xc.astype(BF16)
    r = _sigmoid(_dot(xcb, wa) + ba)
    i = _sigmoid(_dot(xcb, wx) + bx)
    nl = -lam
    sp = jnp.maximum(nl, 0.0) + jnp.log1p(jnp.exp(-jnp.abs(nl)))
    la = (-LRU_C * r) * sp
    a = jnp.exp(la)
    mult = jnp.sqrt(jnp.maximum(-_expm1(2.0 * la), 0.0))
    return xcb, r, i, sp, a, mult


def _scan(a, b, tt, reverse):
    n = a.shape[0]
    s = 1
    while s < n:
        if reverse:
            keep = tt < n - s
            shift = n - s
        else:
            keep = tt >= s
            shift = s
        b = a * jnp.where(keep, pltpu.roll(b, shift, 0), 0.0) + b
        if 2 * s < n:
            a = a * jnp.where(keep, pltpu.roll(a, shift, 0), 1.0)
        s *= 2
    return b


def _lru_specs():
    col = pl.BlockSpec((SEQ, LRU_TC), lambda j: (0, j))
    vec = pl.BlockSpec((1, LRU_TC), lambda j: (0, j))
    bd = pl.BlockSpec((1, LRU_TC, LRU_TC), lambda j: (j, 0, 0))
    cw = pl.BlockSpec((4, LRU_TC), lambda j: (0, j))
    return col, vec, bd, cw


def lru_fwd(gate, xbr, conv_w, conv_b, wa_bd, b_a, wx_bd, b_x, lam, name):
    col, vec, bd, cw = _lru_specs()

    def body(gate_ref, xbr_ref, cw_ref, cb_ref, wa_ref, ba_ref, wx_ref, bx_ref, lam_ref, y_ref, h_ref):
        tt = lax.broadcasted_iota(jnp.int32, (SEQ, LRU_TC), 0)
        xc, _ = _conv_fwd(xbr_ref[...], cw_ref[...], cb_ref[...], tt)
        _, r, i, sp, a, mult = _lru_gates(xc, wa_ref[0], ba_ref[...], wx_ref[0], bx_ref[...], lam_ref[...])
        h = _scan(a, mult * (i * xc), tt, reverse=False)
        h_ref[...] = h
        gl, _ = _gelu(gate_ref[...])
        y_ref[...] = (h * gl).astype(BF16)

    return pl.pallas_call(
        body, name=name, grid=(LRU_W // LRU_TC,),
        in_specs=[col, col, cw, vec, bd, vec, bd, vec, vec],
        out_specs=[col, col],
        out_shape=[jax.ShapeDtypeStruct((SEQ, LRU_W), BF16), jax.ShapeDtypeStruct((SEQ, LRU_W), F32)],
        compiler_params=_params(1),
    )(gate, xbr, conv_w, conv_b, wa_bd, b_a, wx_bd, b_x, lam)


def lru_bwd(gate, xbr, h, dy, conv_w, conv_b, wa_bd, b_a, wx_bd, b_x, lam, name):
    col, vec, bd, cw = _lru_specs()

    def body(gate_ref, xbr_ref, h_ref, dy_ref, cw_ref, cb_ref, wa_ref, ba_ref, wx_ref, bx_ref, lam_ref,
             dgate_ref, dxbr_ref, vecs_ref, dwa_ref, dwx_ref):
        tt = lax.broadcasted_iota(jnp.int32, (SEQ, LRU_TC), 0)
        cwv = cw_ref[...]
        lam = lam_ref[...]
        xb = xbr_ref[...]
        xc, shifted = _conv_fwd(xb, cwv, cb_ref[...], tt)
        wa = wa_ref[0]
        wx = wx_ref[0]
        xcb, r, i, sp, a, mult = _lru_gates(xc, wa, ba_ref[...], wx, bx_ref[...], lam)
        hv = h_ref[...]
        dyv = dy_ref[...]
        gv = gate_ref[...]
        gl, th = _gelu(gv)
        dgate_ref[...] = (dyv * hv * _gelu_grad(gv, th)).astype(BF16)
        a_next = jnp.where(tt < SEQ - 1, pltpu.roll(a, SEQ - 1, 0), 0.0)
        gsum = _scan(a_next, dyv * gl, tt, reverse=True)
        h_prev = jnp.where(tt >= 1, pltpu.roll(hv, 1, 0), 0.0)
        d_mult = gsum * i * xc
        d_i = gsum * mult * xc
        d_xc = gsum * mult * i
        d_la = gsum * h_prev * a - d_mult * (a * a) / mult
        d_pr = (d_la * (-LRU_C * sp)) * r * (1.0 - r)
        d_pi = d_i * i * (1.0 - i)
        d_lam = jnp.sum(d_la * r, axis=0, keepdims=True) * (LRU_C * _sigmoid(-lam))
        d_prb = d_pr.astype(BF16)
        d_pib = d_pi.astype(BF16)
        d_xc = d_xc + _dot_nt(d_prb, wa) + _dot_nt(d_pib, wx)
        dwa_ref[0] = _dot_tn(xcb, d_prb)
        dwx_ref[0] = _dot_tn(xcb, d_pib)
        rows = [jnp.sum(d_xc * shifted[2], axis=0, keepdims=True),
                jnp.sum(d_xc * shifted[1], axis=0, keepdims=True),
                jnp.sum(d_xc * shifted[0], axis=0, keepdims=True),
                jnp.sum(d_xc * xb, axis=0, keepdims=True),
                jnp.sum(d_xc, axis=0, keepdims=True),
                jnp.sum(d_pr, axis=0, keepdims=True),
                jnp.sum(d_pi, axis=0, keepdims=True),
                d_lam]
        ri = lax.broadcasted_iota(jnp.int32, (8, LRU_TC), 0)
        acc = jnp.zeros((8, LRU_TC), F32)
        for k, rv in enumerate(rows):
            acc = jnp.where(ri == k, rv, acc)
        vecs_ref[...] = acc
        d_xb = d_xc * cwv[3:4, :]
        for s in (1, 2, 3):
            d_xb = d_xb + jnp.where(tt < SEQ - s, pltpu.roll(d_xc, SEQ - s, 0), 0.0) * cwv[3 - s:4 - s, :]
        dxbr_ref[...] = d_xb.astype(BF16)

    return pl.pallas_call(
        body, name=name, grid=(LRU_W // LRU_TC,),
        in_specs=[col, col, col, col, cw, vec, bd, vec, bd, vec, vec],
        out_specs=[col, col, pl.BlockSpec((8, LRU_TC), lambda j: (0, j)), bd, bd],
        out_shape=[jax.ShapeDtypeStruct((SEQ, LRU_W), BF16), jax.ShapeDtypeStruct((SEQ, LRU_W), BF16),
                   jax.ShapeDtypeStruct((8, LRU_W), F32),
                   jax.ShapeDtypeStruct((LRU_W // LRU_TC, LRU_TC, LRU_TC), F32),
                   jax.ShapeDtypeStruct((LRU_W // LRU_TC, LRU_TC, LRU_TC), F32)],
        compiler_params=_params(1),
    )(gate, xbr, h, dy, conv_w, conv_b, wa_bd, b_a, wx_bd, b_x, lam)


def _rope(x, cos, sin_signed):
    w = x.shape[1]
    reps = w // 128
    if reps > 1:
        cos = jnp.tile(cos, (1, reps))
        sin_signed = jnp.tile(sin_signed, (1, reps))
    lane = lax.broadcasted_iota(jnp.int32, x.shape, 1)
    first = (lane & 63) < 32
    partner = jnp.where(first, pltpu.roll(x, w - 32, 1), pltpu.roll(x, 32, 1))
    return x * cos + partner * sin_signed


def _both_halves(t, odd):
    lo = lax.broadcasted_iota(jnp.int32, t.shape, 1) < 64
    rolled = pltpu.roll(t, 64, 1)
    return jnp.where(lo, rolled, t) if odd else jnp.where(lo, t, rolled)


def _stack_heads(ta, tb):
    lo = lax.broadcasted_iota(jnp.int32, ta.shape, 1) < 64
    return jnp.concatenate([jnp.where(lo, ta, 0.0), jnp.where(lo, 0.0, ta),
                            jnp.where(lo, tb, 0.0), jnp.where(lo, 0.0, tb)], axis=0)


def _unstack_heads(o):
    lo = lax.broadcasted_iota(jnp.int32, (ATTN_BLOCK, 128), 1) < 64
    return (jnp.where(lo, o[0:128], o[128:256]), jnp.where(lo, o[256:384], o[384:512]))


def _attn_probs(qs, kd, sinks_ref, hk, first_block):
    s = _dot_nt(qs, kd) * (HEAD_DIM ** -0.5)
    row = lax.broadcasted_iota(jnp.int32, s.shape, 0)
    si = lax.broadcasted_iota(jnp.int32, s.shape, 1)
    diff = ATTN_BLOCK + (row & (ATTN_BLOCK - 1)) - si
    valid = (diff >= 0) & (diff < ATTN_BLOCK) & ((si >= ATTN_BLOCK) | jnp.logical_not(first_block))
    s = jnp.where(valid, s, MASK_VALUE)
    rg = lax.broadcasted_iota(jnp.int32, (4 * ATTN_BLOCK, 1), 0) >> 7
    sink = jnp.where(rg == 0, sinks_ref[4 * hk],
                     jnp.where(rg == 1, sinks_ref[4 * hk + 1],
                               jnp.where(rg == 2, sinks_ref[4 * hk + 2], sinks_ref[4 * hk + 3])))
    m = jnp.maximum(jnp.max(s, axis=1, keepdims=True), sink)
    e = jnp.exp(s - m)
    es = jnp.exp(sink - m)
    inv = 1.0 / (jnp.sum(e, axis=1, keepdims=True) + es)
    return e * inv, es * inv


def _prev(i):
    return jnp.maximum(i - 1, 0)


def attn_fwd(q, k, v, cos, sin_signed, sinks, name):
    nb = ATTN_BLOCK

    def body(q_ref, kc_ref, kp_ref, vc_ref, vp_ref, cc_ref, sc_ref, cp_ref, sp_ref, sinks_ref,
             qr_ref, kr_ref, y_ref):
        first_block = pl.program_id(0) == 0
        qr = _rope(q_ref[...], cc_ref[...], sc_ref[...])
        kc = _rope(kc_ref[...], cc_ref[...], sc_ref[...])
        kp = _rope(kp_ref[...], cp_ref[...], sp_ref[...])
        qr_ref[...] = qr.astype(BF16)
        kr_ref[...] = kc.astype(BF16)
        k2 = jnp.concatenate([kp, kc], axis=0)
        v2 = jnp.concatenate([vp_ref[...].astype(F32), vc_ref[...].astype(F32)], axis=0)
        for hk in range(N_KV_HEADS):
            kt = hk // 2
            kd = _both_halves(k2[:, kt * 128:(kt + 1) * 128], hk % 2).astype(BF16)
            vd = _both_halves(v2[:, kt * 128:(kt + 1) * 128], hk % 2).astype(BF16)
            qs = _stack_heads(qr[:, (2 * hk) * 128:(2 * hk + 1) * 128],
                              qr[:, (2 * hk + 1) * 128:(2 * hk + 2) * 128]).astype(BF16)
            p, _ = _attn_probs(qs, kd, sinks_ref, hk, first_block)
            ta, tb = _unstack_heads(_dot(p.astype(BF16), vd))
            y_ref[:, (2 * hk) * 128:(2 * hk + 1) * 128] = ta.astype(BF16)
            y_ref[:, (2 * hk + 1) * 128:(2 * hk + 2) * 128] = tb.astype(BF16)

    cur = lambda w: pl.BlockSpec((nb, w), lambda i: (i, 0))
    prv = lambda w: pl.BlockSpec((nb, w), lambda i: (_prev(i), 0))
    return pl.pallas_call(
        body, name=name, grid=(N_ATTN_BLOCKS,),
        in_specs=[cur(D_MODEL), cur(KV_W), prv(KV_W), cur(KV_W), prv(KV_W), cur(128), cur(128), prv(128), prv(128),
                  pl.BlockSpec(memory_space=pltpu.SMEM)],
        out_specs=[cur(D_MODEL), cur(KV_W), cur(D_MODEL)],
        out_shape=[jax.ShapeDtypeStruct((SEQ, D_MODEL), BF16), jax.ShapeDtypeStruct((SEQ, KV_W), BF16),
                   jax.ShapeDtypeStruct((SEQ, D_MODEL), BF16)],
        compiler_params=_params(1),
    )(q, k, k, v, v, cos, sin_signed, cos, sin_signed, sinks)


def attn_bwd(qr, kr, v, dy, cos, sin_signed, sinks, name):
    nb = ATTN_BLOCK
    n_steps = N_ATTN_BLOCKS + 1
    scale = HEAD_DIM ** -0.5

    def body(q_ref, kc_ref, kp_ref, vc_ref, vp_ref, dy_ref, cc_ref, sc_ref, cp_ref, sp_ref, sinks_ref,
             dq_ref, dk_ref, dv_ref, dsk_ref, ck_ref, cv_ref):
        i = pl.program_id(0)

        @pl.when(i == 0)
        def _():
            dsk_ref[...] = jnp.zeros_like(dsk_ref)
            ck_ref[...] = jnp.zeros_like(ck_ref)
            cv_ref[...] = jnp.zeros_like(cv_ref)

        @pl.when(i < N_ATTN_BLOCKS)
        def _():
            qv = q_ref[...].astype(F32)
            dov = dy_ref[...].astype(F32)
            k2 = jnp.concatenate([kp_ref[...].astype(F32), kc_ref[...].astype(F32)], axis=0)
            v2 = jnp.concatenate([vp_ref[...].astype(F32), vc_ref[...].astype(F32)], axis=0)
            lane = lax.broadcasted_iota(jnp.int32, (8, 128), 1)
            lo = lax.broadcasted_iota(jnp.int32, (2 * nb, 128), 1) < 64
            dsk = jnp.zeros((8, 128), F32)
            dk_tiles = []
            dv_tiles = []
            for hk in range(N_KV_HEADS):
                kt = hk // 2
                kd = _both_halves(k2[:, kt * 128:(kt + 1) * 128], hk % 2).astype(BF16)
                vd = _both_halves(v2[:, kt * 128:(kt + 1) * 128], hk % 2).astype(BF16)
                qs = _stack_heads(qv[:, (2 * hk) * 128:(2 * hk + 1) * 128],
                                  qv[:, (2 * hk + 1) * 128:(2 * hk + 2) * 128]).astype(BF16)
                dos = _stack_heads(dov[:, (2 * hk) * 128:(2 * hk + 1) * 128],
                                   dov[:, (2 * hk + 1) * 128:(2 * hk + 2) * 128]).astype(BF16)
                p, ps = _attn_probs(qs, kd, sinks_ref, hk, i == 0)
                dp = _dot_nt(dos, vd)
                delta = jnp.sum(p * dp, axis=1, keepdims=True)
                ds = (p * (dp - delta)).astype(BF16)
                dsink = -ps * delta
                for g in range(4):
                    dsk = dsk + jnp.where(lane == 4 * hk + g, jnp.sum(dsink[g * nb:(g + 1) * nb]), 0.0)
                ta, tb = _unstack_heads(_dot(ds, kd) * scale)
                dq_a = (2 * hk) * 128
                dq_ref[:, dq_a:dq_a + 128] = _rope(ta, cc_ref[...], -sc_ref[...]).astype(BF16)
                dq_ref[:, dq_a + 128:dq_a + 256] = _rope(tb, cc_ref[...], -sc_ref[...]).astype(BF16)
                rk = _dot_tn(ds, qs) * scale
                rv = _dot_tn(p.astype(BF16), dos)
                dk_tiles.append(rk + pltpu.roll(rk, 64, 1))
                dv_tiles.append(rv + pltpu.roll(rv, 64, 1))
            dsk_ref[...] += dsk
            dk_full = jnp.concatenate([jnp.where(lo, dk_tiles[0], dk_tiles[1]),
                                       jnp.where(lo, dk_tiles[2], dk_tiles[3])], axis=1)
            dv_full = jnp.concatenate([jnp.where(lo, dv_tiles[0], dv_tiles[1]),
                                       jnp.where(lo, dv_tiles[2], dv_tiles[3])], axis=1)
            dk_ref[...] = _rope(ck_ref[...] + dk_full[0:nb], cp_ref[...], -sp_ref[...]).astype(BF16)
            dv_ref[...] = (cv_ref[...] + dv_full[0:nb]).astype(BF16)
            ck_ref[...] = dk_full[nb:2 * nb]
            cv_ref[...] = dv_full[nb:2 * nb]

        @pl.when(i == N_ATTN_BLOCKS)
        def _():
            dk_ref[...] = _rope(ck_ref[...], cp_ref[...], -sp_ref[...]).astype(BF16)
            dv_ref[...] = cv_ref[...].astype(BF16)

    qi = lambda i: jnp.minimum(i, N_ATTN_BLOCKS - 1)
    cur = lambda w: pl.BlockSpec((nb, w), lambda i: (qi(i), 0))
    prv = lambda w: pl.BlockSpec((nb, w), lambda i: (_prev(qi(i)), 0))
    out_prev = lambda w: pl.BlockSpec((nb, w), lambda i: (_prev(i), 0))
    return pl.pallas_call(
        body, name=name, grid=(n_steps,),
        in_specs=[cur(D_MODEL), cur(KV_W), prv(KV_W), cur(KV_W), prv(KV_W), cur(D_MODEL),
                  cur(128), cur(128), out_prev(128), out_prev(128), pl.BlockSpec(memory_space=pltpu.SMEM)],
        out_specs=[cur(D_MODEL), out_prev(KV_W), out_prev(KV_W), pl.BlockSpec((8, 128), lambda i: (0, 0))],
        out_shape=[jax.ShapeDtypeStruct((SEQ, D_MODEL), BF16), jax.ShapeDtypeStruct((SEQ, KV_W), BF16),
                   jax.ShapeDtypeStruct((SEQ, KV_W), BF16), jax.ShapeDtypeStruct((8, 128), F32)],
        scratch_shapes=[pltpu.VMEM((nb, KV_W), F32), pltpu.VMEM((nb, KV_W), F32)],
        compiler_params=_params(1),
    )(qr, kr, kr, v, v, dy, cos, sin_signed, cos, sin_signed, sinks)


def _proj_scratch():
    return [pltpu.VMEM((D_MODEL, D_MODEL), BF16)] * 3 + [pltpu.SemaphoreType.DMA((3 * N_CHIPS,))]


def _load_projs(wf_ref, wl_ref, wa_ref, wo_ref, sem):
    _load_weight(wf_ref, 'w_proj_lru', wl_ref, sem.at[pl.ds(0, N_CHIPS)])
    _load_weight(wf_ref, 'w_proj_attn', wa_ref, sem.at[pl.ds(N_CHIPS, N_CHIPS)])
    _load_weight(wf_ref, 'w_out', wo_ref, sem.at[pl.ds(2 * N_CHIPS, N_CHIPS)])


def merge_fwd(y_lru, y_attn, g_lru, g_attn, wf, g_post, h_in, name):
    tm = 256

    def body(yl_ref, ya_ref, gl_ref, ga_ref, wf_ref, gp_ref, h_ref,
             pl_ref, pa_ref, mg_ref, m_ref, o_ref, wl_ref, wa_ref, wo_ref, sem):
        @pl.when(pl.program_id(0) == 0)
        def _():
            _load_projs(wf_ref, wl_ref, wa_ref, wo_ref, sem)

        p_l = _dot(yl_ref[...], wl_ref[...])
        p_a = _dot(ya_ref[...], wa_ref[...])
        pl_ref[...] = p_l.astype(BF16)
        pa_ref[...] = p_a.astype(BF16)
        merged = (_sigmoid(gl_ref[...]) * p_l + _sigmoid(ga_ref[...]) * p_a).astype(BF16)
        mg_ref[...] = merged
        m = _dot(merged, wo_ref[...])
        m_ref[...] = m
        o_ref[...] = h_ref[...] + m * _rsqrt_mean_sq(m) * gp_ref[...]

    row = _ROW(tm)
    return pl.pallas_call(
        body, name=name, grid=(SEQ // tm,),
        in_specs=[row, row, row, row, ANY, _VEC, row],
        out_specs=[row] * 5,
        out_shape=[jax.ShapeDtypeStruct((SEQ, D_MODEL), BF16)] * 3 + [jax.ShapeDtypeStruct((SEQ, D_MODEL), F32)] * 2,
        scratch_shapes=_proj_scratch(),
        compiler_params=_params(1),
    )(y_lru, y_attn, g_lru, g_attn, wf, g_post, h_in)


def merge_bwd(d_out, m, g_post, wf, g_lru, g_attn, p_l, p_a, name):
    tm = 256

    def body(do_ref, m_ref, gp_ref, wf_ref, gl_ref, ga_ref, pl_ref, pa_ref,
             dm_ref, dpl_ref, dpa_ref, dgl_ref, dga_ref, dya_ref, dyl_ref, dgp_ref, wl_ref, wa_ref, wo_ref, sem):
        @pl.when(pl.program_id(0) == 0)
        def _():
            _load_projs(wf_ref, wl_ref, wa_ref, wo_ref, sem)
            dgp_ref[...] = jnp.zeros_like(dgp_ref)

        mv = m_ref[...]
        rm = _rsqrt_mean_sq(mv)
        mh = mv * rm
        dn = do_ref[...]
        dgp_ref[...] += jnp.sum(dn * mh, axis=0, keepdims=True)
        t = dn * gp_ref[...]
        dm = (rm * (t - mh * jnp.mean(t * mh, axis=-1, keepdims=True))).astype(BF16)
        dm_ref[...] = dm
        dmg = _dot_nt(dm, wo_ref[...])
        sl = _sigmoid(gl_ref[...])
        sa = _sigmoid(ga_ref[...])
        dpl = (dmg * sl).astype(BF16)
        dpa = (dmg * sa).astype(BF16)
        dpl_ref[...] = dpl
        dpa_ref[...] = dpa
        dgl_ref[...] = (dmg * pl_ref[...].astype(F32) * sl * (1.0 - sl)).astype(BF16)
        dga_ref[...] = (dmg * pa_ref[...].astype(F32) * sa * (1.0 - sa)).astype(BF16)
        dyl_ref[...] = _dot_nt(dpl, wl_ref[...])
        dya_ref[...] = _dot_nt(dpa, wa_ref[...]).astype(BF16)

    row = _ROW(tm)
    return pl.pallas_call(
        body, name=name, grid=(SEQ // tm,),
        in_specs=[row, row, _VEC, ANY, row, row, row, row],
        out_specs=[row] * 7 + [_VEC],
        out_shape=[jax.ShapeDtypeStruct((SEQ, D_MODEL), BF16)] * 6 + [jax.ShapeDtypeStruct((SEQ, D_MODEL), F32),
                                                                       jax.ShapeDtypeStruct((1, D_MODEL), F32)],
        scratch_shapes=_proj_scratch(),
        compiler_params=_params(1),
    )(d_out, m, g_post, wf, g_lru, g_attn, p_l, p_a)


def _rope_tables():
    half = HEAD_DIM // 2
    inv_freq = ROPE_THETA ** (-jnp.arange(half, dtype=F32) / half)
    ang = jnp.arange(SEQ, dtype=F32)[:, None] * inv_freq[None, :]
    cos, sin = jnp.cos(ang), jnp.sin(ang)
    return jnp.tile(jnp.concatenate([cos, cos], axis=1), (1, 2)), jnp.tile(jnp.concatenate([-sin, sin], axis=1), (1, 2))


def _block_diag(w):
    per = LRU_TC // LRU_BLOCK_W
    w4 = w.reshape(LRU_W // LRU_TC, per, LRU_BLOCK_W, LRU_BLOCK_W)
    eye = jnp.eye(per, dtype=w.dtype)
    return jnp.einsum('jacd,ab->jacbd', w4, eye).reshape(LRU_W // LRU_TC, LRU_TC, LRU_TC).astype(BF16)


def _diag_blocks(p):
    per = LRU_TC // LRU_BLOCK_W
    p5 = p.reshape(LRU_W // LRU_TC, per, LRU_BLOCK_W, per, LRU_BLOCK_W)
    return jnp.stack([p5[:, a, :, a, :] for a in range(per)], axis=1).reshape(LRU_W // LRU_BLOCK_W, LRU_BLOCK_W, LRU_BLOCK_W)


def local_step(x, target, sw, wf):
    cos, sin_signed = _rope_tables()
    wa_bd = _block_diag(sw['lru_w_a'])
    wx_bd = _block_diag(sw['lru_w_x'])
    sinks = sw['attn_sinks'].reshape(N_Q_HEADS)

    n1, g1, u1, a1 = ffn_fwd_a(x, sw['ffn1_pre_g'], wf, 'ffn1_w_gu', 'ffn1_fwd_a')
    f1, h1 = ffn_fwd_b(a1, wf, 'ffn1_w_down', sw['ffn1_post_g'], x, 'ffn1_fwd_b')
    um, gate, xbr, q, k, v, g_lru, g_attn = mix_in(h1, sw['mix_pre_g'], wf, 'mix_in')
    y_lru, h_lru = lru_fwd(gate, xbr, sw['conv_w'], sw['conv_b'], wa_bd, sw['lru_b_a'], wx_bd, sw['lru_b_x'],
                           sw['lru_lambda'], 'lru_fwd')
    qr, kr, y_attn = attn_fwd(q, k, v, cos, sin_signed, sinks, 'attn_fwd')
    p_l, p_a, merged, m, h2 = merge_fwd(y_lru, y_attn, g_lru, g_attn, wf, sw['mix_post_g'], h1, 'merge_fwd')
    n2, g2, u2, a2 = ffn_fwd_a(h2, sw['ffn2_pre_g'], wf, 'ffn2_w_gu', 'ffn2_fwd_a')
    f2, dy, loss_blk = ffn_fwd_b(a2, wf, 'ffn2_w_down', sw['ffn2_post_g'], h2, 'ffn2_fwd_b', target=target)

    gs, gb = {}, {}
    df2, dgu2, gs['ffn2_post_g'] = ffn_bwd_a(dy, f2, sw['ffn2_post_g'], wf, 'ffn2_w_down', g2, u2, 'ffn2_bwd_a')
    gb['ffn2_w_down'] = mm_tn([a2], df2, 256, 'ffn2_dw_down')
    gb['ffn2_w_gu'] = mm_tn([dgu2], n2, 512, 'ffn2_dw_gu')
    dh2, gs['ffn2_pre_g'] = norm_bwd([dgu2], wf, 'ffn2_w_gu', h2, sw['ffn2_pre_g'], dy, 'ffn2_bwd_b')

    dm, dpl, dpa, dgl, dga, dya, dyl, gs['mix_post_g'] = merge_bwd(
        dh2, m, sw['mix_post_g'], wf, g_lru, g_attn, p_l, p_a, 'merge_bwd')
    gb['w_out'] = mm_tn([merged], dm, 512, 'dw_out')
    gb['w_proj_lru'] = mm_tn([y_lru], dpl, 512, 'dw_proj_lru')
    gb['w_proj_attn'] = mm_tn([y_attn], dpa, 512, 'dw_proj_attn')
    dq, dk, dv, dsk = attn_bwd(qr, kr, v, dya, cos, sin_signed, sinks, 'attn_bwd')
    gs['attn_sinks'] = dsk[0:1, 0:N_Q_HEADS]
    dgate, dxbr, vecs, dwa, dwx = lru_bwd(gate, xbr, h_lru, dyl, sw['conv_w'], sw['conv_b'], wa_bd, sw['lru_b_a'],
                                           wx_bd, sw['lru_b_x'], sw['lru_lambda'], 'lru_bwd')
    gs['conv_w'] = vecs[0:4]
    gs['conv_b'], gs['lru_b_a'], gs['lru_b_x'], gs['lru_lambda'] = vecs[4:5], vecs[5:6], vecs[6:7], vecs[7:8]
    gs['lru_w_a'] = _diag_blocks(dwa)
    gs['lru_w_x'] = _diag_blocks(dwx)
    dz = [dgate, dxbr, dq, dk, dv, dgl, dga]
    gb['w_in'] = mm_tn(dz, um, 256, 'dw_in')
    dh1, gs['mix_pre_g'] = norm_bwd(dz, wf, 'w_in', h1, sw['mix_pre_g'], dh2, 'mix_bwd_in')

    df1, dgu1, gs['ffn1_post_g'] = ffn_bwd_a(dh1, f1, sw['ffn1_post_g'], wf, 'ffn1_w_down', g1, u1, 'ffn1_bwd_a')
    gb['ffn1_w_down'] = mm_tn([a1], df1, 256, 'ffn1_dw_down')
    gb['ffn1_w_gu'] = mm_tn([dgu1], n1, 512, 'ffn1_dw_gu')
    dx, gs['ffn1_pre_g'] = norm_bwd([dgu1], wf, 'ffn1_w_gu', x, sw['ffn1_pre_g'], dh1, 'ffn1_bwd_b')
    return loss_blk, dx, gs, gb


def _place():
    x, y, c = lax.axis_index('x'), lax.axis_index('y'), lax.axis_index('c')
    chips = [(1 - x, y), (x, 1 - y), (1 - x, 1 - y)]
    return x, y, c, chips


def _rcopy(src, dst, send_sem, recv_sem, to):
    return pltpu.make_async_remote_copy(src_ref=src, dst_ref=dst, send_sem=send_sem, recv_sem=recv_sem,
                                        device_id=to, device_id_type=MESH)


def _local_loads(pairs, sem):
    loads = [pltpu.make_async_copy(src, buf, sem.at[2 * k]) for k, (src, buf) in enumerate(pairs)]
    for ld in loads:
        ld.start()
    return loads


def _local_stores(loads, pairs, sem):
    stores = []
    for k, (ld, (buf, dst)) in enumerate(zip(loads, pairs)):
        ld.wait()
        st = pltpu.make_async_copy(buf, dst, sem.at[2 * k + 1])
        st.start()
        stores.append(st)
    return stores


_COMM_PARAMS = pltpu.CompilerParams(vmem_limit_bytes=VMEM_LIMIT)
LOCAL_CHUNKS = 8


def ag_pack(pack):
    ch = HALF_ROWS // AG_CHUNKS
    n_ici = 3 * AG_CHUNKS
    lch = PACK_ROWS // LOCAL_CHUNKS

    def body(p_ref, o_ref, buf, send, recv, lsem):
        x, y, c, chips = _place()
        me_q = 2 * x + y
        sib = (x, y, 1 - c)

        def rows(cc, k):
            return pl.ds(pl.multiple_of(cc * HALF_ROWS + k * ch, 16), ch)

        loads = _local_loads([(p_ref.at[pl.ds(k * lch, lch)], buf.at[pl.ds(k * lch, lch)])
                              for k in range(LOCAL_CHUNKS)], lsem)
        sends = []
        for k in range(AG_CHUNKS):
            for j, (cx, cy) in enumerate(chips):
                cp = _rcopy(p_ref.at[rows(c, k)], o_ref.at[me_q, rows(c, k)],
                            send.at[j * AG_CHUNKS + k], recv.at[j * AG_CHUNKS + k], (cx, cy, c))
                cp.start()
                sends.append(cp)
        stores = _local_stores(loads, [(buf.at[pl.ds(k * lch, lch)], o_ref.at[me_q, pl.ds(k * lch, lch)])
                                       for k in range(LOCAL_CHUNKS)], lsem)
        for k in range(AG_CHUNKS):
            for j, (cx, cy) in enumerate(chips):
                blk = o_ref.at[2 * cx + cy, rows(c, k)]
                _rcopy(blk, blk, send.at[j * AG_CHUNKS + k], recv.at[j * AG_CHUNKS + k], (cx, cy, c)).wait_recv()
                cp = _rcopy(blk, blk, send.at[n_ici + j * AG_CHUNKS + k], recv.at[n_ici + j * AG_CHUNKS + k], sib)
                cp.start()
                sends.append(cp)
        for k in range(AG_CHUNKS):
            for j, (cx, cy) in enumerate(chips):
                blk = o_ref.at[2 * cx + cy, rows(1 - c, k)]
                _rcopy(blk, blk, send.at[n_ici + j * AG_CHUNKS + k], recv.at[n_ici + j * AG_CHUNKS + k], sib).wait_recv()
        for cp in sends:
            cp.wait_send()
        for st in stores:
            st.wait()

    return pl.pallas_call(
        body, name='ag_pack',
        out_shape=jax.ShapeDtypeStruct((N_CHIPS, PACK_ROWS, D_MODEL), BF16),
        in_specs=[ANY], out_specs=ANY,
        scratch_shapes=[pltpu.VMEM((PACK_ROWS, D_MODEL), BF16),
                        pltpu.SemaphoreType.DMA((2 * n_ici,)), pltpu.SemaphoreType.DMA((2 * n_ici,)),
                        pltpu.SemaphoreType.DMA((2 * LOCAL_CHUNKS,))],
        compiler_params=_COMM_PARAMS,
    )(pack)


def ag_small(blk, name):
    def body(x_ref, o_ref, buf, send, recv, lsem):
        x, y, c, chips = _place()
        sib = (x, y, 1 - c)

        def slot(px, py, pc):
            return o_ref.at[4 * px + 2 * py + pc]

        loads = _local_loads([(x_ref, buf)], lsem)
        first = [_rcopy(x_ref, slot(x, y, c), send.at[0], recv.at[0], sib)]
        first += [_rcopy(x_ref, slot(x, y, c), send.at[1 + j], recv.at[1 + j], (cx, cy, c))
                  for j, (cx, cy) in enumerate(chips)]
        for cp in first:
            cp.start()
        stores = _local_stores(loads, [(buf, slot(x, y, c))], lsem)
        passed = []
        for j, (cx, cy) in enumerate(chips):
            blk_ref = slot(cx, cy, c)
            _rcopy(blk_ref, blk_ref, send.at[1 + j], recv.at[1 + j], (cx, cy, c)).wait_recv()
            cp = _rcopy(blk_ref, blk_ref, send.at[4 + j], recv.at[4 + j], sib)
            cp.start()
            passed.append(cp)
        sb = slot(x, y, 1 - c)
        _rcopy(sb, sb, send.at[0], recv.at[0], sib).wait_recv()
        for j, (cx, cy) in enumerate(chips):
            blk_ref = slot(cx, cy, 1 - c)
            _rcopy(blk_ref, blk_ref, send.at[4 + j], recv.at[4 + j], sib).wait_recv()
        for cp in first + passed:
            cp.wait_send()
        stores[0].wait()

    return pl.pallas_call(
        body, name=name,
        out_shape=jax.ShapeDtypeStruct((N_DEV,) + blk.shape, blk.dtype),
        in_specs=[ANY], out_specs=ANY,
        scratch_shapes=[pltpu.VMEM(blk.shape, blk.dtype), pltpu.SemaphoreType.DMA((7,)), pltpu.SemaphoreType.DMA((7,)),
                        pltpu.SemaphoreType.DMA((2,))],
        compiler_params=_COMM_PARAMS,
    )(blk)


def rs_pair(grads):
    names = [n for n, _, _ in PACK]

    def body(*refs):
        g_refs = refs[:len(names)]
        land_ref, send, recv = refs[len(names):]
        x, y, c, _ = _place()
        sib = (x, y, 1 - c)
        cps = []
        for w, (name, rows, _) in enumerate(PACK):
            hr = rows // 2
            o2 = PACK_OFF[name] // 2
            for q in range(N_CHIPS):
                cp = _rcopy(g_refs[w].at[q, pl.ds(1 - c, 1)], land_ref.at[q, :, pl.ds(o2, hr)],
                            send.at[w * N_CHIPS + q], recv.at[w * N_CHIPS + q], sib)
                cp.start()
                cps.append(cp)
        for cp in cps:
            cp.wait()

    n_cp = len(names) * N_CHIPS
    return pl.pallas_call(
        body, name='rs_pair',
        out_shape=jax.ShapeDtypeStruct((N_CHIPS, 1, HALF_ROWS, D_MODEL), BF16),
        in_specs=[ANY] * len(names), out_specs=ANY,
        scratch_shapes=[pltpu.SemaphoreType.DMA((n_cp,)), pltpu.SemaphoreType.DMA((n_cp,))],
    )(*[grads[n] for n in names])


def pair_sum(g4, land, c_arr, off2, name):
    hr = g4.shape[2]

    def body(c_ref, g_ref, l_ref, o_ref):
        o_ref[0] = (g_ref[0, 0].astype(F32) + l_ref[0, 0].astype(F32)).astype(BF16)

    return pl.pallas_call(
        body, name=name,
        grid_spec=pltpu.PrefetchScalarGridSpec(
            num_scalar_prefetch=1, grid=(N_CHIPS,),
            in_specs=[pl.BlockSpec((1, 1, hr, D_MODEL), lambda q, c: (q, c[0], 0, 0)),
                      pl.BlockSpec((1, 1, hr, D_MODEL), lambda q, c: (q, 0, off2 // hr, 0))],
            out_specs=pl.BlockSpec((1, hr, D_MODEL), lambda q, c: (q, 0, 0))),
        out_shape=jax.ShapeDtypeStruct((N_CHIPS, hr, D_MODEL), BF16),
        compiler_params=_params(1),
    )(c_arr, g4, land)


def rs_chips(sums):
    names = [n for n, _, _ in PACK]

    def body(*refs):
        s_refs = refs[:len(names)]
        land_ref, buf, send, recv, lsem = refs[len(names):]
        x, y, c, chips = _place()
        me_q = 2 * x + y
        halves = [(PACK_OFF[name] // 2, rows // 2) for name, rows, _ in PACK]
        loads = _local_loads([(s_refs[w].at[me_q], buf.at[pl.ds(o2, hr)]) for w, (o2, hr) in enumerate(halves)], lsem)
        cps = []
        for w, (o2, hr) in enumerate(halves):
            for j, (cx, cy) in enumerate(chips):
                cp = _rcopy(s_refs[w].at[2 * cx + cy], land_ref.at[me_q, pl.ds(o2, hr)],
                            send.at[w * 3 + j], recv.at[w * 3 + j], (cx, cy, c))
                cp.start()
                cps.append(cp)
        locs = _local_stores(loads, [(buf.at[pl.ds(o2, hr)], land_ref.at[me_q, pl.ds(o2, hr)]) for o2, hr in halves], lsem)
        for w, (name, rows, _) in enumerate(PACK):
            hr = rows // 2
            o2 = PACK_OFF[name] // 2
            for j, (cx, cy) in enumerate(chips):
                blk = land_ref.at[2 * cx + cy, pl.ds(o2, hr)]
                _rcopy(blk, blk, send.at[w * 3 + j], recv.at[w * 3 + j], (cx, cy, c)).wait_recv()
        for cp in cps:
            cp.wait_send()
        for loc in locs:
            loc.wait()

    n_cp = len(names) * 3
    return pl.pallas_call(
        body, name='rs_chips',
        out_shape=jax.ShapeDtypeStruct((N_CHIPS, HALF_ROWS, D_MODEL), BF16),
        in_specs=[ANY] * len(names), out_specs=ANY,
        scratch_shapes=[pltpu.VMEM((HALF_ROWS, D_MODEL), BF16),
                        pltpu.SemaphoreType.DMA((n_cp,)), pltpu.SemaphoreType.DMA((n_cp,)),
                        pltpu.SemaphoreType.DMA((2 * len(names),))],
        compiler_params=_COMM_PARAMS,
    )(*[sums[n] for n in names])


def chip_sum(land):
    tr = 128

    def body(l_ref, o_ref):
        acc = l_ref[0].astype(F32)
        for s in range(1, N_CHIPS):
            acc = acc + l_ref[s].astype(F32)
        o_ref[...] = acc

    return pl.pallas_call(
        body, name='chip_sum', grid=(HALF_ROWS // tr,),
        in_specs=[pl.BlockSpec((N_CHIPS, tr, D_MODEL), lambda i: (0, i, 0))],
        out_specs=pl.BlockSpec((tr, D_MODEL), lambda i: (i, 0)),
        out_shape=jax.ShapeDtypeStruct((HALF_ROWS, D_MODEL), F32),
        compiler_params=_params(1),
    )(land)


def ag_pair(half):
    n_ch = PAIR_CHUNKS
    ch = HALF_ROWS // n_ch

    def body(h_ref, o_ref, buf, send, recv, lsem):
        x, y, c, _ = _place()
        chunks = [pl.ds(k * ch, ch) for k in range(n_ch)]
        loads = _local_loads([(h_ref.at[rows], buf.at[rows]) for rows in chunks], lsem)
        cps = []
        for k, rows in enumerate(chunks):
            cp = _rcopy(h_ref.at[rows], o_ref.at[c, rows], send.at[k], recv.at[k], (x, y, 1 - c))
            cp.start()
            cps.append(cp)
        cps += _local_stores(loads, [(buf.at[rows], o_ref.at[c, rows]) for rows in chunks], lsem)
        for cp in cps:
            cp.wait()

    return pl.pallas_call(
        body, name='ag_pair',
        out_shape=jax.ShapeDtypeStruct((2, HALF_ROWS, D_MODEL), F32),
        in_specs=[ANY], out_specs=ANY,
        scratch_shapes=[pltpu.VMEM((HALF_ROWS, D_MODEL), F32),
                        pltpu.SemaphoreType.DMA((n_ch,)), pltpu.SemaphoreType.DMA((n_ch,)),
                        pltpu.SemaphoreType.DMA((2 * n_ch,))],
        compiler_params=_COMM_PARAMS,
    )(half)


def small_sum(parts):
    def body(p_ref, o_ref):
        acc = p_ref[0]
        for s in range(1, N_DEV):
            acc = acc + p_ref[s]
        o_ref[...] = acc

    return pl.pallas_call(
        body, name='small_sum', grid=(1,),
        in_specs=[pl.BlockSpec(parts.shape, lambda i: (0, 0, 0))],
        out_specs=pl.BlockSpec(parts.shape[1:], lambda i: (0, 0)),
        out_shape=jax.ShapeDtypeStruct(parts.shape[1:], F32),
        compiler_params=_params(1),
    )(parts)


def _adam_math(w, g, m, v):
    m2 = ADAM_B1 * m + (1.0 - ADAM_B1) * g
    v2 = ADAM_B2 * v + (1.0 - ADAM_B2) * (g * g)
    m_hat = m2 / (1.0 - ADAM_B1 ** ADAM_STEP)
    v_hat = v2 / (1.0 - ADAM_B2 ** ADAM_STEP)
    delta = -ADAM_LR * (m_hat / (jnp.sqrt(v_hat) + ADAM_EPS) + ADAM_WD * w)
    return delta, m2, v2


def _adam_body(transposed):
    def body(g_ref, w_ref, m_ref, v_ref, go_ref, d_ref, mo_ref, vo_ref):
        if transposed:
            gt = g_ref[...]
            g = gt.reshape(gt.shape[0] * gt.shape[1], gt.shape[2]).T
        else:
            g = g_ref[0]
        go_ref[...] = g
        d_ref[...], mo_ref[...], vo_ref[...] = _adam_math(w_ref[...], g, m_ref[...], v_ref[...])
    return body


def adam_rows(full, name, w, m, v):
    hr = w.shape[0] // 2
    ob = (PACK_OFF[name] // 2) // hr
    blk = pl.BlockSpec((hr, D_MODEL), lambda h: (h, 0))
    return pl.pallas_call(
        _adam_body(False), name='adam_' + name, grid=(2,),
        in_specs=[pl.BlockSpec((1, hr, D_MODEL), lambda h: (h, ob, 0)), blk, blk, blk],
        out_specs=[blk] * 4,
        out_shape=[jax.ShapeDtypeStruct(w.shape, F32)] * 4,
        compiler_params=_params(1),
    )(full, w, m, v)


def adam_cols(full, name, w, m, v):
    cols = w.shape[1]
    hr = cols // 2
    ob = (PACK_OFF[name] // 2) // hr
    tr = 128
    blk = pl.BlockSpec((tr, cols), lambda i: (i, 0))
    return pl.pallas_call(
        _adam_body(True), name='adam_' + name, grid=(D_MODEL // tr,),
        in_specs=[pl.BlockSpec((2, hr, tr), lambda i: (0, ob, i)), blk, blk, blk],
        out_specs=[blk] * 4,
        out_shape=[jax.ShapeDtypeStruct(w.shape, F32)] * 4,
        compiler_params=_params(1),
    )(full, w, m, v)


def adam_small(g, w, m, v):
    def body(g_ref, w_ref, m_ref, v_ref, d_ref, mo_ref, vo_ref):
        d_ref[...], mo_ref[...], vo_ref[...] = _adam_math(w_ref[...], g_ref[...], m_ref[...], v_ref[...])

    blk = pl.BlockSpec(w.shape, lambda i: (0, 0))
    return pl.pallas_call(
        body, name='adam_small', grid=(1,), in_specs=[blk] * 4, out_specs=[blk] * 3,
        out_shape=[jax.ShapeDtypeStruct(w.shape, F32)] * 3, compiler_params=_params(1),
    )(g, w, m, v)


WEIGHTS = ('ffn1_pre_g', 'ffn1_w_gu', 'ffn1_w_down', 'ffn1_post_g', 'mix_pre_g', 'w_in', 'conv_w', 'conv_b',
           'lru_w_a', 'lru_b_a', 'lru_w_x', 'lru_b_x', 'lru_lambda', 'attn_sinks', 'w_proj_lru', 'w_proj_attn',
           'w_out', 'mix_post_g', 'ffn2_pre_g', 'ffn2_w_gu', 'ffn2_w_down', 'ffn2_post_g')
SMALL = tuple(n for n in WEIGHTS if n not in PACK_OFF)


def _pack_small(d, conv_rows):
    sinks = jnp.pad(d['attn_sinks'].reshape(1, N_Q_HEADS), ((0, 0), (0, D_MODEL - N_Q_HEADS)))
    conv = jnp.pad(conv_rows, ((0, ROW_WA - ROW_CONV - conv_rows.shape[0]), (0, 0)))
    return jnp.concatenate([d[n].reshape(1, D_MODEL) for n in SMALL_VECS] + [sinks, conv]
                           + [d['lru_w_a'].reshape(64, D_MODEL), d['lru_w_x'].reshape(64, D_MODEL)], axis=0)


def _unpack_small(p, shapes):
    out = {n: p[k:k + 1].reshape(shapes[n]) for k, n in enumerate(SMALL_VECS)}
    out['attn_sinks'] = p[ROW_SINKS:ROW_SINKS + 1, :N_Q_HEADS].reshape(shapes['attn_sinks'])
    out['conv_w'] = p[ROW_CONV:ROW_CONV + 1].reshape(shapes['conv_w'])
    out['lru_w_a'] = p[ROW_WA:ROW_WA + 64].reshape(shapes['lru_w_a'])
    out['lru_w_x'] = p[ROW_WX:ROW_WX + 64].reshape(shapes['lru_w_x'])
    return out


def kernel(x, ffn1_pre_g, ffn1_w_gu, ffn1_w_down, ffn1_post_g, mix_pre_g, w_in, conv_w, conv_b, lru_w_a, lru_b_a, lru_w_x, lru_b_x, lru_lambda, attn_sinks, w_proj_lru, w_proj_attn, w_out, mix_post_g, ffn2_pre_g, ffn2_w_gu, ffn2_w_down, ffn2_post_g, loss_target, m_ffn1_pre_g, m_ffn1_w_gu, m_ffn1_w_down, m_ffn1_post_g, m_mix_pre_g, m_w_in, m_conv_w, m_conv_b, m_lru_w_a, m_lru_b_a, m_lru_w_x, m_lru_b_x, m_lru_lambda, m_attn_sinks, m_w_proj_lru, m_w_proj_attn, m_w_out, m_mix_post_g, m_ffn2_pre_g, m_ffn2_w_gu, m_ffn2_w_down, m_ffn2_post_g, v_ffn1_pre_g, v_ffn1_w_gu, v_ffn1_w_down, v_ffn1_post_g, v_mix_pre_g, v_w_in, v_conv_w, v_conv_b, v_lru_w_a, v_lru_b_a, v_lru_w_x, v_lru_b_x, v_lru_lambda, v_attn_sinks, v_w_proj_lru, v_w_proj_attn, v_w_out, v_mix_post_g, v_ffn2_pre_g, v_ffn2_w_gu, v_ffn2_w_down, v_ffn2_post_g):
    given = dict(locals())
    w = {n: given[n] for n in WEIGHTS}
    mom = {n: given['m_' + n] for n in WEIGHTS}
    var = {n: given['v_' + n] for n in WEIGHTS}
    shapes = {n: w[n].shape for n in WEIGHTS}
    xq = lax.axis_index('x')
    yq = lax.axis_index('y')
    cq = lax.axis_index('c')
    me_q = 2 * xq + yq

    pack = jnp.concatenate([(w[n][0].T if t else w[n][0]) for n, _, t in PACK], axis=0).astype(BF16)
    wf = ag_pack(pack)
    conv_all = ag_small(jnp.pad(w['conv_w'][0], ((0, 4), (0, 0))), 'ag_conv')
    conv_full = jnp.transpose(conv_all[0::2, :4, :], (1, 0, 2)).reshape(4, LRU_W)

    sw = {n: (w[n][0] if w[n].ndim > 2 else w[n]) for n in SMALL}
    sw['conv_w'] = conv_full
    loss_blk, dx, gs, gb = local_step(x[0], loss_target[0], sw, wf)
    loss = lax.psum(loss_blk[0, 0], ('x', 'y', 'c'))

    g4 = {n: gb[n].reshape(N_CHIPS, 2, r // 2, D_MODEL) for n, r, _ in PACK}
    land = rs_pair(g4)
    c_arr = cq.reshape(1).astype(jnp.int32)
    sums = {n: pair_sum(g4[n], land, c_arr, PACK_OFF[n] // 2, 'pair_sum_' + n) for n, _, _ in PACK}
    full = ag_pair(chip_sum(rs_chips(sums)))

    out_g, out_d, out_m, out_v = {}, {}, {}, {}
    for n, _, t in PACK:
        fn = adam_cols if t else adam_rows
        g_, d_, m_, v_ = fn(full, n, w[n][0], mom[n][0], var[n][0])
        out_g[n], out_d[n], out_m[n], out_v[n] = g_[None], d_[None], m_[None], v_[None]

    tot = small_sum(ag_small(_pack_small(gs, gs['conv_w']), 'ag_small_grads'))
    conv_g = lax.dynamic_slice(tot[ROW_CONV:ROW_CONV + 4], (0, me_q * (LRU_W // N_CHIPS)), (4, LRU_W // N_CHIPS))
    small_g = _unpack_small(tot, shapes)
    small_g['conv_w'] = conv_g.reshape(shapes['conv_w'])
    g_pack = jnp.concatenate([tot[:ROW_CONV], conv_g.reshape(1, D_MODEL), jnp.zeros((ROW_WA - ROW_CONV - 1, D_MODEL), F32),
                              tot[ROW_WA:]], axis=0)
    packs = [_pack_small({n: d[n] for n in SMALL}, d['conv_w'].reshape(1, D_MODEL)) for d in (w, mom, var)]
    d_p, m_p, v_p = adam_small(g_pack, *packs)
    for n in SMALL:
        out_g[n] = small_g[n]
    for dst, p in ((out_d, d_p), (out_m, m_p), (out_v, v_p)):
        dst.update(_unpack_small(p, shapes))

    return (loss, dx[None], *[out_g[n] for n in WEIGHTS], *[out_d[n] for n in WEIGHTS],
            *[out_m[n] for n in WEIGHTS], *[out_v[n] for n in WEIGHTS])
```

```python
import jax
import jax.numpy as jnp
from jax import lax
from jax.experimental import pallas as pl
from jax.experimental.pallas import tpu as pltpu

F32 = jnp.float32
BF16 = jnp.bfloat16

SEQ = 2048
D_MODEL = 1024
D_FF = 2816
LRU_W = 1024
LRU_BLOCK_W = 64
HEAD_DIM = 64
N_Q_HEADS = 16
N_KV_HEADS = 4
KV_W = N_KV_HEADS * HEAD_DIM
ATTN_BLOCK = 128
N_ATTN_BLOCKS = SEQ // ATTN_BLOCK
IN_SEGS = (1024, 1024, 1024, 256, 256, 1024, 1024)
IN_W = sum(IN_SEGS)
NORM_EPS = 1e-6
MASK_VALUE = -1e30
ROPE_THETA = 10000.0
LRU_C = 8.0
MACARON = 0.5
ADAM_LR = 0.001
ADAM_B1 = 0.9
ADAM_B2 = 0.999
ADAM_EPS = 1e-08
ADAM_WD = 0.01
ADAM_STEP = 10

N_CHIPS = 4
N_DEV = 8
VMEM_LIMIT = 56 * 1024 * 1024
MESH = pl.DeviceIdType.MESH
ANY = pl.BlockSpec(memory_space=pl.ANY)

PACK = (('ffn1_w_gu', 1408, True), ('w_in', 1408, True), ('ffn2_w_gu', 1408, True),
        ('ffn1_w_down', 704, False), ('ffn2_w_down', 704, False),
        ('w_proj_lru', 256, False), ('w_proj_attn', 256, False), ('w_out', 256, False))
PACK_ROWS_OF = {n: r for n, r, _ in PACK}
PACK_OFF = {}
_o = 0
for _n, _r, _t in PACK:
    PACK_OFF[_n] = _o
    _o += _r

SMALL_VECS = ('ffn1_pre_g', 'ffn1_post_g', 'mix_pre_g', 'conv_b', 'lru_b_a', 'lru_b_x', 'lru_lambda',
              'mix_post_g', 'ffn2_pre_g', 'ffn2_post_g')
SMALL_ROWS = 144
ROW_SINKS, ROW_CONV, ROW_WA, ROW_WX = 10, 11, 16, 80


def _dot(a, b):
    return jnp.dot(a, b, preferred_element_type=F32)


def _dot_nt(a, b):
    return lax.dot_general(a, b, (((1,), (1,)), ((), ())), preferred_element_type=F32)


def _dot_tn(a, b):
    return lax.dot_general(a, b, (((0,), (0,)), ((), ())), preferred_element_type=F32)


def _params(n_grid):
    return pltpu.CompilerParams(dimension_semantics=("arbitrary",) * n_grid, vmem_limit_bytes=VMEM_LIMIT)


def _sigmoid(x):
    return 1.0 / (1.0 + jnp.exp(-x))


def _rsqrt_mean_sq(x):
    return lax.rsqrt(jnp.mean(x * x, axis=-1, keepdims=True) + NORM_EPS)


def _expm1(x):
    poly = x * (1.0 + x * (0.5 + x * (1.0 / 6.0 + x * (1.0 / 24.0 + x * (1.0 / 120.0)))))
    return jnp.where(jnp.abs(x) < 0.1, poly, jnp.exp(x) - 1.0)


_GELU_K = 0.7978845608028654
_GELU_C = 0.044715


def _gelu(x):
    t = jnp.tanh(_GELU_K * (x + _GELU_C * x * x * x))
    return 0.5 * x * (1.0 + t), t


def _gelu_grad(x, t):
    return 0.5 * (1.0 + t) + 0.5 * x * (1.0 - t * t) * _GELU_K * (1.0 + 3.0 * _GELU_C * x * x)


def _load_weight(w_ref, dst_ref, sem):
    rows = dst_ref.shape[0] // N_CHIPS
    cps = [pltpu.make_async_copy(w_ref.at[q], dst_ref.at[pl.ds(q * rows, rows)], sem.at[q]) for q in range(N_CHIPS)]
    for cp in cps:
        cp.start()
    for cp in cps:
        cp.wait()


def _weight_scratch(rows_total):
    return [pltpu.VMEM((rows_total, D_MODEL), BF16), pltpu.SemaphoreType.DMA((N_CHIPS,))]


_ROW = lambda tm: pl.BlockSpec((tm, D_MODEL), lambda i: (i, 0))
_VEC = pl.BlockSpec((1, D_MODEL), lambda i: (0, 0))


def _call(body, *, name, grid, in_specs, out_specs, out_shape, args, scratch_shapes=(), stages=()):
    in_specs, out_specs, out_shape, scratch_shapes = list(in_specs), list(out_specs), list(out_shape), list(scratch_shapes)
    n_in, n_out, n_sc = len(in_specs), len(out_specs), len(scratch_shapes)
    k_in = [len(s.inputs) for s in stages]
    k_out = [len(s.out_shape) for s in stages]
    k_sc = [len(s.scratch) for s in stages]
    last = grid[0] - 1

    def split(refs, counts):
        parts, pos = [], 0
        for k in counts:
            parts.append(refs[pos:pos + k])
            pos += k
        return parts

    def full(*refs):
        ins, s_ins, outs, s_outs, scr, s_scr = split(refs, [n_in, sum(k_in), n_out, sum(k_out), n_sc, sum(k_sc)])
        per_stage = list(zip(stages, split(s_ins, k_in), split(s_outs, k_out), split(s_scr, k_sc)))
        i = pl.program_id(0)
        if stages:
            @pl.when(i == 0)
            def _():
                for s, a, b, c in per_stage:
                    s.start(a, b, c)

        body(*ins, *outs, *scr)
        if stages:
            @pl.when(i == max(last - 1, 0))
            def _():
                for s, a, b, c in per_stage:
                    s.mid(a, b, c)

            @pl.when(i == last)
            def _():
                for s, a, b, c in per_stage:
                    s.end(a, b, c)

    res = pl.pallas_call(
        full, name=name, grid=grid,
        in_specs=in_specs + [ANY] * sum(k_in),
        out_specs=out_specs + [ANY] * sum(k_out),
        out_shape=out_shape + [o for s in stages for o in s.out_shape],
        scratch_shapes=scratch_shapes + [x for s in stages for x in s.scratch],
        compiler_params=_params(1),
    )(*args, *[a for s in stages for a in s.inputs])
    return list(res[:n_out]), split(list(res[n_out:]), k_out)


def ffn_fwd_a(x, g_pre, w_gu_t, name, stages=()):
    tm, tn = 256, 256

    def body(x_ref, gp_ref, wf_ref, n_ref, g_ref, u_ref, a_ref, wt_ref, sem):
        @pl.when(pl.program_id(0) == 0)
        def _():
            _load_weight(wf_ref, wt_ref, sem)

        xv = x_ref[...]
        n = (xv * _rsqrt_mean_sq(xv) * gp_ref[...]).astype(BF16)
        n_ref[...] = n
        for j in range(D_FF // tn):
            g = _dot_nt(n, wt_ref[j * tn:(j + 1) * tn, :])
            u = _dot_nt(n, wt_ref[D_FF + j * tn:D_FF + (j + 1) * tn, :])
            g_ref[:, j * tn:(j + 1) * tn] = g.astype(BF16)
            u_ref[:, j * tn:(j + 1) * tn] = u.astype(BF16)
            a_ref[:, j * tn:(j + 1) * tn] = (g * _sigmoid(g) * u).astype(BF16)

    wide = pl.BlockSpec((tm, D_FF), lambda i: (i, 0))
    return _call(
        body, name=name, grid=(SEQ // tm,),
        in_specs=[_ROW(tm), _VEC, ANY],
        out_specs=[_ROW(tm), wide, wide, wide],
        out_shape=[jax.ShapeDtypeStruct((SEQ, D_MODEL), BF16)] + [jax.ShapeDtypeStruct((SEQ, D_FF), BF16)] * 3,
        scratch_shapes=_weight_scratch(2 * D_FF),
        args=[x, g_pre, w_gu_t], stages=stages)


def ffn_fwd_b(a, w_down, g_post, h_in, name, target=None, stages=()):
    tm = 256
    final = target is not None

    def body(*refs):
        if final:
            a_ref, wf_ref, gp_ref, h_ref, t_ref, f_ref, o_ref, loss_ref, wd_ref, sem = refs
        else:
            a_ref, wf_ref, gp_ref, h_ref, f_ref, o_ref, wd_ref, sem = refs

        @pl.when(pl.program_id(0) == 0)
        def _():
            _load_weight(wf_ref, wd_ref, sem)
            if final:
                loss_ref[...] = jnp.zeros_like(loss_ref)

        f = _dot(a_ref[...], wd_ref[...])
        f_ref[...] = f
        y = h_ref[...] + MACARON * (f * _rsqrt_mean_sq(f) * gp_ref[...])
        if final:
            err = y - t_ref[...]
            o_ref[...] = err * (1.0 / D_MODEL)
            loss_ref[...] += 0.5 * jnp.sum(err * err) * (1.0 / D_MODEL)
        else:
            o_ref[...] = y

    row = _ROW(tm)
    in_specs = [pl.BlockSpec((tm, D_FF), lambda i: (i, 0)), ANY, _VEC, row]
    out_specs = [row, row]
    out_shape = [jax.ShapeDtypeStruct((SEQ, D_MODEL), F32)] * 2
    args = [a, w_down, g_post, h_in]
    if final:
        in_specs.append(row)
        args.append(target)
        out_specs.append(pl.BlockSpec((8, 128), lambda i: (0, 0)))
        out_shape.append(jax.ShapeDtypeStruct((8, 128), F32))
    return _call(body, name=name, grid=(SEQ // tm,), in_specs=in_specs, out_specs=out_specs,
                 out_shape=out_shape, scratch_shapes=_weight_scratch(D_FF), args=args, stages=stages)


def ffn_bwd_a(d_out, f, g_post, w_down, g, u, name, stages=()):
    tm = 256

    def body(do_ref, f_ref, gp_ref, wf_ref, g_ref, u_ref, df_ref, dgu_ref, dgp_ref, wd_ref, sem):
        @pl.when(pl.program_id(0) == 0)
        def _():
            _load_weight(wf_ref, wd_ref, sem)
            dgp_ref[...] = jnp.zeros_like(dgp_ref)

        fv = f_ref[...]
        rf = _rsqrt_mean_sq(fv)
        fh = fv * rf
        dn = MACARON * do_ref[...]
        dgp_ref[...] += jnp.sum(dn * fh, axis=0, keepdims=True)
        t = dn * gp_ref[...]
        df = (rf * (t - fh * jnp.mean(t * fh, axis=-1, keepdims=True))).astype(BF16)
        df_ref[...] = df
        da = _dot_nt(df, wd_ref[...])
        gv = g_ref[...].astype(F32)
        uv = u_ref[...].astype(F32)
        s = _sigmoid(gv)
        dgu_ref[:, :D_FF] = (da * uv * s * (1.0 + gv * (1.0 - s))).astype(BF16)
        dgu_ref[:, D_FF:] = (da * gv * s).astype(BF16)

    row = _ROW(tm)
    wide = pl.BlockSpec((tm, D_FF), lambda i: (i, 0))
    return _call(
        body, name=name, grid=(SEQ // tm,),
        in_specs=[row, row, _VEC, ANY, wide, wide],
        out_specs=[row, pl.BlockSpec((tm, 2 * D_FF), lambda i: (i, 0)), _VEC],
        out_shape=[jax.ShapeDtypeStruct((SEQ, D_MODEL), BF16), jax.ShapeDtypeStruct((SEQ, 2 * D_FF), BF16),
                   jax.ShapeDtypeStruct((1, D_MODEL), F32)],
        scratch_shapes=_weight_scratch(D_FF),
        args=[d_out, f, g_post, w_down, g, u], stages=stages)


def norm_bwd(pieces, w_t, x, g_pre, d_res, name, stages=()):
    tm = 256
    widths = [p.shape[1] for p in pieces]
    offs = [sum(widths[:k]) for k in range(len(widths))]
    n_p = len(pieces)

    def body(*refs):
        p_refs = refs[:n_p]
        wf_ref, x_ref, g_ref, r_ref, dx_ref, dg_ref, wt_ref, sem = refs[n_p:]

        @pl.when(pl.program_id(0) == 0)
        def _():
            _load_weight(wf_ref, wt_ref, sem)
            dg_ref[...] = jnp.zeros_like(dg_ref)

        dn = None
        for p_ref, lo, wd in zip(p_refs, offs, widths):
            part = _dot(p_ref[...], wt_ref[lo:lo + wd, :])
            dn = part if dn is None else dn + part
        xv = x_ref[...]
        r = _rsqrt_mean_sq(xv)
        xh = xv * r
        dg_ref[...] += jnp.sum(dn * xh, axis=0, keepdims=True)
        t = dn * g_ref[...]
        dx_ref[...] = r_ref[...] + r * (t - xh * jnp.mean(t * xh, axis=-1, keepdims=True))

    row = _ROW(tm)
    return _call(
        body, name=name, grid=(SEQ // tm,),
        in_specs=[pl.BlockSpec((tm, wd), lambda i: (i, 0)) for wd in widths] + [ANY, row, _VEC, row],
        out_specs=[row, _VEC],
        out_shape=[jax.ShapeDtypeStruct((SEQ, D_MODEL), F32), jax.ShapeDtypeStruct((1, D_MODEL), F32)],
        scratch_shapes=_weight_scratch(sum(widths)),
        args=[*pieces, w_t, x, g_pre, d_res], stages=stages)


def mm_tn(pieces, b, tm, name, stages=()):
    widths = [p.shape[1] for p in pieces]
    m_total = sum(widths)
    n_p = len(pieces)
    starts = [sum(widths[:k]) // tm for k in range(n_p)]
    counts = [wd // tm for wd in widths]

    def body(*refs):
        p_refs = refs[:n_p]
        b_ref, o_ref = refs[n_p:]
        i = pl.program_id(0)
        for p_ref, st, ct in zip(p_refs, starts, counts):
            @pl.when((i >= st) & (i < st + ct))
            def _(p_ref=p_ref):
                o_ref[...] = _dot_tn(p_ref[...], b_ref[...]).astype(BF16)

    def piece_spec(st, ct):
        return pl.BlockSpec((SEQ, tm), lambda i: (0, jnp.clip(i - st, 0, ct - 1)))

    (out,), stage_out = _call(
        body, name=name, grid=(m_total // tm,),
        in_specs=[piece_spec(st, ct) for st, ct in zip(starts, counts)] + [pl.BlockSpec((SEQ, D_MODEL), lambda i: (0, 0))],
        out_specs=[pl.BlockSpec((tm, D_MODEL), lambda i: (i, 0))],
        out_shape=[jax.ShapeDtypeStruct((m_total, D_MODEL), BF16)],
        args=[*pieces, b], stages=stages)
    return out, stage_out


def mix_in(h, g_pre, w_in_t, name, stages=()):
    tm = 256
    offs = [sum(IN_SEGS[:k]) for k in range(len(IN_SEGS))]
    dts = [F32, F32, F32, F32, BF16, F32, F32]
    n_o = len(IN_SEGS)

    def body(*refs):
        h_ref, g_ref, wf_ref, um_ref = refs[:4]
        o_refs = refs[4:4 + n_o]
        wt_ref, sem = refs[4 + n_o:]

        @pl.when(pl.program_id(0) == 0)
        def _():
            _load_weight(wf_ref, wt_ref, sem)

        hv = h_ref[...]
        um = (hv * _rsqrt_mean_sq(hv) * g_ref[...]).astype(BF16)
        um_ref[...] = um
        for o_ref, lo, wd in zip(o_refs, offs, IN_SEGS):
            for c0 in range(0, wd, 256):
                o_ref[:, c0:c0 + 256] = _dot_nt(um, wt_ref[lo + c0:lo + c0 + 256, :]).astype(o_ref.dtype)

    return _call(
        body, name=name, grid=(SEQ // tm,),
        in_specs=[_ROW(tm), _VEC, ANY],
        out_specs=[_ROW(tm)] + [pl.BlockSpec((tm, wd), lambda i: (i, 0)) for wd in IN_SEGS],
        out_shape=[jax.ShapeDtypeStruct((SEQ, D_MODEL), BF16)]
        + [jax.ShapeDtypeStruct((SEQ, wd), dt) for wd, dt in zip(IN_SEGS, dts)],
        scratch_shapes=_weight_scratch(IN_W),
        args=[h, g_pre, w_in_t], stages=stages)


LRU_TC = 256


def _conv_fwd(xb, cw, cb, tt):
    xc = xb * cw[3:4, :] + cb
    shifted = []
    for s in (1, 2, 3):
        sh = jnp.where(tt >= s, pltpu.roll(xb, s, 0), 0.0)
        shifted.append(sh)
        xc = xc + sh * cw[3 - s:4 - s, :]
    return xc, shifted


def _lru_gates(xc, wa, ba, wx, bx, lam):
    xcb = xc.astype(BF16)
    r = _sigmoid(_dot(xcb, wa) + ba)
    i = _sigmoid(_dot(xcb, wx) + bx)
    nl = -lam
    sp = jnp.maximum(nl, 0.0) + jnp.log1p(jnp.exp(-jnp.abs(nl)))
    la = (-LRU_C * r) * sp
    a = jnp.exp(la)
    mult = jnp.sqrt(jnp.maximum(-_expm1(2.0 * la), 0.0))
    return xcb, r, i, sp, a, mult


def _scan(a, b, tt, reverse):
    n = a.shape[0]
    s = 1
    while s < n:
        if reverse:
            keep = tt < n - s
            shift = n - s
        else:
            keep = tt >= s
            shift = s
        b = a * jnp.where(keep, pltpu.roll(b, shift, 0), 0.0) + b
        if 2 * s < n:
            a = a * jnp.where(keep, pltpu.roll(a, shift, 0), 1.0)
        s *= 2
    return b


def _lru_specs():
    col = pl.BlockSpec((SEQ, LRU_TC), lambda j: (0, j))
    vec = pl.BlockSpec((1, LRU_TC), lambda j: (0, j))
    bd = pl.BlockSpec((1, LRU_TC, LRU_TC), lambda j: (j, 0, 0))
    cw = pl.BlockSpec((4, LRU_TC), lambda j: (0, j))
    return col, vec, bd, cw


def lru_fwd(gate, xbr, conv_w, conv_b, wa_bd, b_a, wx_bd, b_x, lam, name, stages=()):
    col, vec, bd, cw = _lru_specs()

    def body(gate_ref, xbr_ref, cw_ref, cb_ref, wa_ref, ba_ref, wx_ref, bx_ref, lam_ref, y_ref, h_ref):
        tt = lax.broadcasted_iota(jnp.int32, (SEQ, LRU_TC), 0)
        xc, _ = _conv_fwd(xbr_ref[...], cw_ref[...], cb_ref[...], tt)
        _, r, i, sp, a, mult = _lru_gates(xc, wa_ref[0], ba_ref[...], wx_ref[0], bx_ref[...], lam_ref[...])
        h = _scan(a, mult * (i * xc), tt, reverse=False)
        h_ref[...] = h
        gl, _ = _gelu(gate_ref[...])
        y_ref[...] = (h * gl).astype(BF16)

    return _call(
        body, name=name, grid=(LRU_W // LRU_TC,),
        in_specs=[col, col, cw, vec, bd, vec, bd, vec, vec],
        out_specs=[col, col],
        out_shape=[jax.ShapeDtypeStruct((SEQ, LRU_W), BF16), jax.ShapeDtypeStruct((SEQ, LRU_W), F32)],
        args=[gate, xbr, conv_w, conv_b, wa_bd, b_a, wx_bd, b_x, lam], stages=stages)


def lru_bwd(gate, xbr, h, dy, conv_w, conv_b, wa_bd, b_a, wx_bd, b_x, lam, name, stages=()):
    col, vec, bd, cw = _lru_specs()

    def body(gate_ref, xbr_ref, h_ref, dy_ref, cw_ref, cb_ref, wa_ref, ba_ref, wx_ref, bx_ref, lam_ref,
             dgate_ref, dxbr_ref, vecs_ref, dwa_ref, dwx_ref):
        tt = lax.broadcasted_iota(jnp.int32, (SEQ, LRU_TC), 0)
        cwv = cw_ref[...]
        lam = lam_ref[...]
        xb = xbr_ref[...]
        xc, shifted = _conv_fwd(xb, cwv, cb_ref[...], tt)
        wa = wa_ref[0]
        wx = wx_ref[0]
        xcb, r, i, sp, a, mult = _lru_gates(xc, wa, ba_ref[...], wx, bx_ref[...], lam)
        hv = h_ref[...]
        dyv = dy_ref[...]
        gv = gate_ref[...]
        gl, th = _gelu(gv)
        dgate_ref[...] = (dyv * hv * _gelu_grad(gv, th)).astype(BF16)
        a_next = jnp.where(tt < SEQ - 1, pltpu.roll(a, SEQ - 1, 0), 0.0)
        gsum = _scan(a_next, dyv * gl, tt, reverse=True)
        h_prev = jnp.where(tt >= 1, pltpu.roll(hv, 1, 0), 0.0)
        d_mult = gsum * i * xc
        d_i = gsum * mult * xc
        d_xc = gsum * mult * i
        d_la = gsum * h_prev * a - d_mult * (a * a) / mult
        d_pr = (d_la * (-LRU_C * sp)) * r * (1.0 - r)
        d_pi = d_i * i * (1.0 - i)
        d_lam = jnp.sum(d_la * r, axis=0, keepdims=True) * (LRU_C * _sigmoid(-lam))
        d_prb = d_pr.astype(BF16)
        d_pib = d_pi.astype(BF16)
        d_xc = d_xc + _dot_nt(d_prb, wa) + _dot_nt(d_pib, wx)
        dwa_ref[0] = _dot_tn(xcb, d_prb)
        dwx_ref[0] = _dot_tn(xcb, d_pib)
        rows = [jnp.sum(d_xc * shifted[2], axis=0, keepdims=True),
                jnp.sum(d_xc * shifted[1], axis=0, keepdims=True),
                jnp.sum(d_xc * shifted[0], axis=0, keepdims=True),
                jnp.sum(d_xc * xb, axis=0, keepdims=True),
                jnp.sum(d_xc, axis=0, keepdims=True),
                jnp.sum(d_pr, axis=0, keepdims=True),
                jnp.sum(d_pi, axis=0, keepdims=True),
                d_lam]
        ri = lax.broadcasted_iota(jnp.int32, (8, LRU_TC), 0)
        acc = jnp.zeros((8, LRU_TC), F32)
        for k, rv in enumerate(rows):
            acc = jnp.where(ri == k, rv, acc)
        vecs_ref[...] = acc
        d_xb = d_xc * cwv[3:4, :]
        for s in (1, 2, 3):
            d_xb = d_xb + jnp.where(tt < SEQ - s, pltpu.roll(d_xc, SEQ - s, 0), 0.0) * cwv[3 - s:4 - s, :]
        dxbr_ref[...] = d_xb.astype(BF16)

    return _call(
        body, name=name, grid=(LRU_W // LRU_TC,),
        in_specs=[col, col, col, col, cw, vec, bd, vec, bd, vec, vec],
        out_specs=[col, col, pl.BlockSpec((8, LRU_TC), lambda j: (0, j)), bd, bd],
        out_shape=[jax.ShapeDtypeStruct((SEQ, LRU_W), BF16), jax.ShapeDtypeStruct((SEQ, LRU_W), BF16),
                   jax.ShapeDtypeStruct((8, LRU_W), F32),
                   jax.ShapeDtypeStruct((LRU_W // LRU_TC, LRU_TC, LRU_TC), F32),
                   jax.ShapeDtypeStruct((LRU_W // LRU_TC, LRU_TC, LRU_TC), F32)],
        args=[gate, xbr, h, dy, conv_w, conv_b, wa_bd, b_a, wx_bd, b_x, lam], stages=stages)


def _rope(x, cos, sin_signed):
    w = x.shape[1]
    reps = w // 128
    if reps > 1:
        cos = jnp.tile(cos, (1, reps))
        sin_signed = jnp.tile(sin_signed, (1, reps))
    lane = lax.broadcasted_iota(jnp.int32, x.shape, 1)
    first = (lane & 63) < 32
    partner = jnp.where(first, pltpu.roll(x, w - 32, 1), pltpu.roll(x, 32, 1))
    return x * cos + partner * sin_signed


def _both_halves(t, odd):
    lo = lax.broadcasted_iota(jnp.int32, t.shape, 1) < 64
    rolled = pltpu.roll(t, 64, 1)
    return jnp.where(lo, rolled, t) if odd else jnp.where(lo, t, rolled)


def _stack_heads(ta, tb):
    lo = lax.broadcasted_iota(jnp.int32, ta.shape, 1) < 64
    return jnp.concatenate([jnp.where(lo, ta, 0.0), jnp.where(lo, 0.0, ta),
                            jnp.where(lo, tb, 0.0), jnp.where(lo, 0.0, tb)], axis=0)


def _unstack_heads(o):
    lo = lax.broadcasted_iota(jnp.int32, (ATTN_BLOCK, 128), 1) < 64
    return (jnp.where(lo, o[0:128], o[128:256]), jnp.where(lo, o[256:384], o[384:512]))


def _attn_probs(qs, kd, sinks_ref, hk, first_block):
    s = _dot_nt(qs, kd) * (HEAD_DIM ** -0.5)
    row = lax.broadcasted_iota(jnp.int32, s.shape, 0)
    si = lax.broadcasted_iota(jnp.int32, s.shape, 1)
    diff = ATTN_BLOCK + (row & (ATTN_BLOCK - 1)) - si
    valid = (diff >= 0) & (diff < ATTN_BLOCK) & ((si >= ATTN_BLOCK) | jnp.logical_not(first_block))
    s = jnp.where(valid, s, MASK_VALUE)
    rg = lax.broadcasted_iota(jnp.int32, (4 * ATTN_BLOCK, 1), 0) >> 7
    sink = jnp.where(rg == 0, sinks_ref[4 * hk],
                     jnp.where(rg == 1, sinks_ref[4 * hk + 1],
                               jnp.where(rg == 2, sinks_ref[4 * hk + 2], sinks_ref[4 * hk + 3])))
    m = jnp.maximum(jnp.max(s, axis=1, keepdims=True), sink)
    e = jnp.exp(s - m)
    es = jnp.exp(sink - m)
    inv = 1.0 / (jnp.sum(e, axis=1, keepdims=True) + es)
    return e * inv, es * inv


def _prev(i):
    return jnp.maximum(i - 1, 0)


def attn_fwd(q, k, v, cos, sin_signed, sinks, name, stages=()):
    nb = ATTN_BLOCK

    def body(q_ref, kc_ref, kp_ref, vc_ref, vp_ref, cc_ref, sc_ref, cp_ref, sp_ref, sinks_ref,
             qr_ref, kr_ref, y_ref):
        first_block = pl.program_id(0) == 0
        qr = _rope(q_ref[...], cc_ref[...], sc_ref[...])
        kc = _rope(kc_ref[...], cc_ref[...], sc_ref[...])
        kp = _rope(kp_ref[...], cp_ref[...], sp_ref[...])
        qr_ref[...] = qr.astype(BF16)
        kr_ref[...] = kc.astype(BF16)
        k2 = jnp.concatenate([kp, kc], axis=0)
        v2 = jnp.concatenate([vp_ref[...].astype(F32), vc_ref[...].astype(F32)], axis=0)
        for hk in range(N_KV_HEADS):
            kt = hk // 2
            kd = _both_halves(k2[:, kt * 128:(kt + 1) * 128], hk % 2).astype(BF16)
            vd = _both_halves(v2[:, kt * 128:(kt + 1) * 128], hk % 2).astype(BF16)
            qs = _stack_heads(qr[:, (2 * hk) * 128:(2 * hk + 1) * 128],
                              qr[:, (2 * hk + 1) * 128:(2 * hk + 2) * 128]).astype(BF16)
            p, _ = _attn_probs(qs, kd, sinks_ref, hk, first_block)
            ta, tb = _unstack_heads(_dot(p.astype(BF16), vd))
            y_ref[:, (2 * hk) * 128:(2 * hk + 1) * 128] = ta.astype(BF16)
            y_ref[:, (2 * hk + 1) * 128:(2 * hk + 2) * 128] = tb.astype(BF16)

    cur = lambda w: pl.BlockSpec((nb, w), lambda i: (i, 0))
    prv = lambda w: pl.BlockSpec((nb, w), lambda i: (_prev(i), 0))
    return _call(
        body, name=name, grid=(N_ATTN_BLOCKS,),
        in_specs=[cur(D_MODEL), cur(KV_W), prv(KV_W), cur(KV_W), prv(KV_W), cur(128), cur(128), prv(128), prv(128),
                  pl.BlockSpec(memory_space=pltpu.SMEM)],
        out_specs=[cur(D_MODEL), cur(KV_W), cur(D_MODEL)],
        out_shape=[jax.ShapeDtypeStruct((SEQ, D_MODEL), BF16), jax.ShapeDtypeStruct((SEQ, KV_W), BF16),
                   jax.ShapeDtypeStruct((SEQ, D_MODEL), BF16)],
        args=[q, k, k, v, v, cos, sin_signed, cos, sin_signed, sinks], stages=stages)


def attn_bwd(qr, kr, v, dy, cos, sin_signed, sinks, name, stages=()):
    nb = ATTN_BLOCK
    n_steps = N_ATTN_BLOCKS + 1
    scale = HEAD_DIM ** -0.5

    def body(q_ref, kc_ref, kp_ref, vc_ref, vp_ref, dy_ref, cc_ref, sc_ref, cp_ref, sp_ref, sinks_ref,
             dq_ref, dk_ref, dv_ref, dsk_ref, ck_ref, cv_ref):
        i = pl.program_id(0)

        @pl.when(i == 0)
        def _():
            dsk_ref[...] = jnp.zeros_like(dsk_ref)
            ck_ref[...] = jnp.zeros_like(ck_ref)
            cv_ref[...] = jnp.zeros_like(cv_ref)

        @pl.when(i < N_ATTN_BLOCKS)
        def _():
            qv = q_ref[...].astype(F32)
            dov = dy_ref[...].astype(F32)
            k2 = jnp.concatenate([kp_ref[...].astype(F32), kc_ref[...].astype(F32)], axis=0)
            v2 = jnp.concatenate([vp_ref[...].astype(F32), vc_ref[...].astype(F32)], axis=0)
            lane = lax.broadcasted_iota(jnp.int32, (8, 128), 1)
            lo = lax.broadcasted_iota(jnp.int32, (2 * nb, 128), 1) < 64
            dsk = jnp.zeros((8, 128), F32)
            dk_tiles = []
            dv_tiles = []
            for hk in range(N_KV_HEADS):
                kt = hk // 2
                kd = _both_halves(k2[:, kt * 128:(kt + 1) * 128], hk % 2).astype(BF16)
                vd = _both_halves(v2[:, kt * 128:(kt + 1) * 128], hk % 2).astype(BF16)
                qs = _stack_heads(qv[:, (2 * hk) * 128:(2 * hk + 1) * 128],
                                  qv[:, (2 * hk + 1) * 128:(2 * hk + 2) * 128]).astype(BF16)
                dos = _stack_heads(dov[:, (2 * hk) * 128:(2 * hk + 1) * 128],
                                   dov[:, (2 * hk + 1) * 128:(2 * hk + 2) * 128]).astype(BF16)
                p, ps = _attn_probs(qs, kd, sinks_ref, hk, i == 0)
                dp = _dot_nt(dos, vd)
                delta = jnp.sum(p * dp, axis=1, keepdims=True)
                ds = (p * (dp - delta)).astype(BF16)
                dsink = -ps * delta
                for g in range(4):
                    dsk = dsk + jnp.where(lane == 4 * hk + g, jnp.sum(dsink[g * nb:(g + 1) * nb]), 0.0)
                ta, tb = _unstack_heads(_dot(ds, kd) * scale)
                dq_a = (2 * hk) * 128
                dq_ref[:, dq_a:dq_a + 128] = _rope(ta, cc_ref[...], -sc_ref[...]).astype(BF16)
                dq_ref[:, dq_a + 128:dq_a + 256] = _rope(tb, cc_ref[...], -sc_ref[...]).astype(BF16)
                rk = _dot_tn(ds, qs) * scale
                rv = _dot_tn(p.astype(BF16), dos)
                dk_tiles.append(rk + pltpu.roll(rk, 64, 1))
                dv_tiles.append(rv + pltpu.roll(rv, 64, 1))
            dsk_ref[...] += dsk
            dk_full = jnp.concatenate([jnp.where(lo, dk_tiles[0], dk_tiles[1]),
                                       jnp.where(lo, dk_tiles[2], dk_tiles[3])], axis=1)
            dv_full = jnp.concatenate([jnp.where(lo, dv_tiles[0], dv_tiles[1]),
                                       jnp.where(lo, dv_tiles[2], dv_tiles[3])], axis=1)
            dk_ref[...] = _rope(ck_ref[...] + dk_full[0:nb], cp_ref[...], -sp_ref[...]).astype(BF16)
            dv_ref[...] = (cv_ref[...] + dv_full[0:nb]).astype(BF16)
            ck_ref[...] = dk_full[nb:2 * nb]
            cv_ref[...] = dv_full[nb:2 * nb]

        @pl.when(i == N_ATTN_BLOCKS)
        def _():
            dk_ref[...] = _rope(ck_ref[...], cp_ref[...], -sp_ref[...]).astype(BF16)
            dv_ref[...] = cv_ref[...].astype(BF16)

    qi = lambda i: jnp.minimum(i, N_ATTN_BLOCKS - 1)
    cur = lambda w: pl.BlockSpec((nb, w), lambda i: (qi(i), 0))
    prv = lambda w: pl.BlockSpec((nb, w), lambda i: (_prev(qi(i)), 0))
    out_prev = lambda w: pl.BlockSpec((nb, w), lambda i: (_prev(i), 0))
    return _call(
        body, name=name, grid=(n_steps,),
        in_specs=[cur(D_MODEL), cur(KV_W), prv(KV_W), cur(KV_W), prv(KV_W), cur(D_MODEL),
                  cur(128), cur(128), out_prev(128), out_prev(128), pl.BlockSpec(memory_space=pltpu.SMEM)],
        out_specs=[cur(D_MODEL), out_prev(KV_W), out_prev(KV_W), pl.BlockSpec((8, 128), lambda i: (0, 0))],
        out_shape=[jax.ShapeDtypeStruct((SEQ, D_MODEL), BF16), jax.ShapeDtypeStruct((SEQ, KV_W), BF16),
                   jax.ShapeDtypeStruct((SEQ, KV_W), BF16), jax.ShapeDtypeStruct((8, 128), F32)],
        scratch_shapes=[pltpu.VMEM((nb, KV_W), F32), pltpu.VMEM((nb, KV_W), F32)],
        args=[qr, kr, kr, v, v, dy, cos, sin_signed, cos, sin_signed, sinks], stages=stages)


def _proj_scratch():
    return [pltpu.VMEM((D_MODEL, D_MODEL), BF16)] * 3 + [pltpu.SemaphoreType.DMA((3 * N_CHIPS,))]


def _load_projs(w_refs, wl_ref, wa_ref, wo_ref, sem):
    for k, (w_ref, dst) in enumerate(zip(w_refs, (wl_ref, wa_ref, wo_ref))):
        _load_weight(w_ref, dst, sem.at[pl.ds(k * N_CHIPS, N_CHIPS)])


def merge_fwd(y_lru, y_attn, g_lru, g_attn, projs, g_post, h_in, name, stages=()):
    tm = 256

    def body(yl_ref, ya_ref, gl_ref, ga_ref, w1_ref, w2_ref, w3_ref, gp_ref, h_ref,
             pl_ref, pa_ref, mg_ref, m_ref, o_ref, wl_ref, wa_ref, wo_ref, sem):
        @pl.when(pl.program_id(0) == 0)
        def _():
            _load_projs((w1_ref, w2_ref, w3_ref), wl_ref, wa_ref, wo_ref, sem)

        p_l = _dot(yl_ref[...], wl_ref[...])
        p_a = _dot(ya_ref[...], wa_ref[...])
        pl_ref[...] = p_l.astype(BF16)
        pa_ref[...] = p_a.astype(BF16)
        merged = (_sigmoid(gl_ref[...]) * p_l + _sigmoid(ga_ref[...]) * p_a).astype(BF16)
        mg_ref[...] = merged
        m = _dot(merged, wo_ref[...])
        m_ref[...] = m
        o_ref[...] = h_ref[...] + m * _rsqrt_mean_sq(m) * gp_ref[...]

    row = _ROW(tm)
    return _call(
        body, name=name, grid=(SEQ // tm,),
        in_specs=[row, row, row, row, ANY, ANY, ANY, _VEC, row],
        out_specs=[row] * 5,
        out_shape=[jax.ShapeDtypeStruct((SEQ, D_MODEL), BF16)] * 3 + [jax.ShapeDtypeStruct((SEQ, D_MODEL), F32)] * 2,
        scratch_shapes=_proj_scratch(),
        args=[y_lru, y_attn, g_lru, g_attn, *projs, g_post, h_in], stages=stages)


def merge_bwd(d_out, m, g_post, projs, g_lru, g_attn, p_l, p_a, name, stages=()):
    tm = 256

    def body(do_ref, m_ref, gp_ref, w1_ref, w2_ref, w3_ref, gl_ref, ga_ref, pl_ref, pa_ref,
             dm_ref, dpl_ref, dpa_ref, dgl_ref, dga_ref, dya_ref, dyl_ref, dgp_ref, wl_ref, wa_ref, wo_ref, sem):
        @pl.when(pl.program_id(0) == 0)
        def _():
            _load_projs((w1_ref, w2_ref, w3_ref), wl_ref, wa_ref, wo_ref, sem)
            dgp_ref[...] = jnp.zeros_like(dgp_ref)

        mv = m_ref[...]
        rm = _rsqrt_mean_sq(mv)
        mh = mv * rm
        dn = do_ref[...]
        dgp_ref[...] += jnp.sum(dn * mh, axis=0, keepdims=True)
        t = dn * gp_ref[...]
        dm = (rm * (t - mh * jnp.mean(t * mh, axis=-1, keepdims=True))).astype(BF16)
        dm_ref[...] = dm
        dmg = _dot_nt(dm, wo_ref[...])
        sl = _sigmoid(gl_ref[...])
        sa = _sigmoid(ga_ref[...])
        dpl = (dmg * sl).astype(BF16)
        dpa = (dmg * sa).astype(BF16)
        dpl_ref[...] = dpl
        dpa_ref[...] = dpa
        dgl_ref[...] = (dmg * pl_ref[...].astype(F32) * sl * (1.0 - sl)).astype(BF16)
        dga_ref[...] = (dmg * pa_ref[...].astype(F32) * sa * (1.0 - sa)).astype(BF16)
        dyl_ref[...] = _dot_nt(dpl, wl_ref[...])
        dya_ref[...] = _dot_nt(dpa, wa_ref[...]).astype(BF16)

    row = _ROW(tm)
    return _call(
        body, name=name, grid=(SEQ // tm,),
        in_specs=[row, row, _VEC, ANY, ANY, ANY, row, row, row, row],
        out_specs=[row] * 7 + [_VEC],
        out_shape=[jax.ShapeDtypeStruct((SEQ, D_MODEL), BF16)] * 6 + [jax.ShapeDtypeStruct((SEQ, D_MODEL), F32),
                                                                       jax.ShapeDtypeStruct((1, D_MODEL), F32)],
        scratch_shapes=_proj_scratch(),
        args=[d_out, m, g_post, *projs, g_lru, g_attn, p_l, p_a], stages=stages)


def _rope_tables():
    half = HEAD_DIM // 2
    inv_freq = ROPE_THETA ** (-jnp.arange(half, dtype=F32) / half)
    ang = jnp.arange(SEQ, dtype=F32)[:, None] * inv_freq[None, :]
    cos, sin = jnp.cos(ang), jnp.sin(ang)
    return jnp.tile(jnp.concatenate([cos, cos], axis=1), (1, 2)), jnp.tile(jnp.concatenate([-sin, sin], axis=1), (1, 2))


def _block_diag(w):
    per = LRU_TC // LRU_BLOCK_W
    w4 = w.reshape(LRU_W // LRU_TC, per, LRU_BLOCK_W, LRU_BLOCK_W)
    eye = jnp.eye(per, dtype=w.dtype)
    return jnp.einsum('jacd,ab->jacbd', w4, eye).reshape(LRU_W // LRU_TC, LRU_TC, LRU_TC).astype(BF16)


def _diag_blocks(p):
    per = LRU_TC // LRU_BLOCK_W
    p5 = p.reshape(LRU_W // LRU_TC, per, LRU_BLOCK_W, per, LRU_BLOCK_W)
    return jnp.stack([p5[:, a, :, a, :] for a in range(per)], axis=1).reshape(LRU_W // LRU_BLOCK_W, LRU_BLOCK_W, LRU_BLOCK_W)


def _place():
    x, y, c = lax.axis_index('x'), lax.axis_index('y'), lax.axis_index('c')
    chips = [(1 - x, y), (x, 1 - y), (1 - x, 1 - y)]
    return x, y, c, chips


def _rcopy(src, dst, send_sem, recv_sem, to):
    return pltpu.make_async_remote_copy(src_ref=src, dst_ref=dst, send_sem=send_sem, recv_sem=recv_sem,
                                        device_id=to, device_id_type=MESH)


class _Stage:
    inputs, out_shape, scratch = (), (), ()

    def start(self, ins, outs, scr):
        plan = self._plan(ins, outs, scr)
        for ld in plan['loads']:
            ld.start()
        for cp in plan['sends']:
            cp.start()

    def mid(self, ins, outs, scr):
        plan = self._plan(ins, outs, scr)
        for ld, st in zip(plan['loads'], plan['stores']):
            ld.wait()
            st.start()
        for arrived, onward in zip(plan['arrivals'], plan['forwards']):
            arrived.wait_recv()
            onward.start()

    def end(self, ins, outs, scr):
        plan = self._plan(ins, outs, scr)
        for st in plan['stores']:
            st.wait()
        for arrived in (plan['final_arrivals'] if plan['forwards'] else plan['arrivals']):
            arrived.wait_recv()
        for cp in plan['sends'] + plan['forwards']:
            cp.wait_send()


def _empty_plan():
    return dict(loads=[], stores=[], sends=[], arrivals=[], forwards=[], final_arrivals=[])


class GatherStage(_Stage):
    SUB = 2

    def __init__(self, names, pack):
        self.names = list(names)
        self.rows = [PACK_ROWS_OF[n] for n in self.names]
        self.inputs = [pack]
        self.out_shape = [jax.ShapeDtypeStruct((N_CHIPS, r, D_MODEL), BF16) for r in self.rows]
        self.n_ici = 3 * self.SUB * len(self.names)
        self.scratch = [pltpu.VMEM((sum(self.rows), D_MODEL), BF16), pltpu.SemaphoreType.DMA((2 * self.n_ici,)),
                        pltpu.SemaphoreType.DMA((2 * self.n_ici,)), pltpu.SemaphoreType.DMA((2 * len(self.names),))]

    def _plan(self, ins, outs, scr):
        (p_ref,), (buf, send, recv, lsem) = ins, scr
        x, y, c, chips = _place()
        me_q = 2 * x + y
        sib = (x, y, 1 - c)
        plan = _empty_plan()
        boff = 0
        for w, (name, rows, o_ref) in enumerate(zip(self.names, self.rows, outs)):
            off, hr = PACK_OFF[name], rows // 2
            ch = hr // self.SUB
            plan['loads'].append(pltpu.make_async_copy(p_ref.at[pl.ds(off, rows)], buf.at[pl.ds(boff, rows)], lsem.at[2 * w]))
            plan['stores'].append(pltpu.make_async_copy(buf.at[pl.ds(boff, rows)], o_ref.at[me_q], lsem.at[2 * w + 1]))
            boff += rows
            for k in range(self.SUB):
                mine = pl.ds(pl.multiple_of(c * hr + k * ch, 16), ch)
                theirs = pl.ds(pl.multiple_of((1 - c) * hr + k * ch, 16), ch)
                src = p_ref.at[pl.ds(pl.multiple_of(off + c * hr + k * ch, 16), ch)]
                for j, (cx, cy) in enumerate(chips):
                    i = (w * self.SUB + k) * 3 + j
                    got = o_ref.at[2 * cx + cy, mine]
                    got_sib = o_ref.at[2 * cx + cy, theirs]
                    plan['sends'].append(_rcopy(src, o_ref.at[me_q, mine], send.at[i], recv.at[i], (cx, cy, c)))
                    plan['arrivals'].append(_rcopy(got, got, send.at[i], recv.at[i], (cx, cy, c)))
                    plan['forwards'].append(_rcopy(got, got, send.at[self.n_ici + i], recv.at[self.n_ici + i], sib))
                    plan['final_arrivals'].append(
                        _rcopy(got_sib, got_sib, send.at[self.n_ici + i], recv.at[self.n_ici + i], sib))
        return plan


class PairStage(_Stage):
    def __init__(self, grads):
        self.inputs = list(grads)
        self.out_shape = [jax.ShapeDtypeStruct((N_CHIPS, 1) + g.shape[2:], BF16) for g in grads]
        n_cp = N_CHIPS * len(grads)
        self.scratch = [pltpu.SemaphoreType.DMA((n_cp,)), pltpu.SemaphoreType.DMA((n_cp,))]

    def _plan(self, ins, outs, scr):
        send, recv = scr
        x, y, c, _ = _place()
        plan = _empty_plan()
        for w, (g_ref, l_ref) in enumerate(zip(ins, outs)):
            for q in range(N_CHIPS):
                i = w * N_CHIPS + q
                plan['sends'].append(_rcopy(g_ref.at[q, pl.ds(1 - c, 1)], l_ref.at[q], send.at[i], recv.at[i], (x, y, 1 - c)))
        plan['arrivals'] = plan['sends']
        return plan


class ChipStage(_Stage):
    def __init__(self, sums):
        self.inputs = list(sums)
        self.out_shape = [jax.ShapeDtypeStruct(s.shape, BF16) for s in sums]
        n_cp = 3 * len(sums)
        self.scratch = [pltpu.VMEM((sum(s.shape[1] for s in sums), D_MODEL), BF16), pltpu.SemaphoreType.DMA((n_cp,)),
                        pltpu.SemaphoreType.DMA((n_cp,)), pltpu.SemaphoreType.DMA((2 * len(sums),))]

    def _plan(self, ins, outs, scr):
        buf, send, recv, lsem = scr
        x, y, c, chips = _place()
        me_q = 2 * x + y
        plan = _empty_plan()
        boff = 0
        for w, (s_ref, l_ref) in enumerate(zip(ins, outs)):
            hr = s_ref.shape[1]
            plan['loads'].append(pltpu.make_async_copy(s_ref.at[me_q], buf.at[pl.ds(boff, hr)], lsem.at[2 * w]))
            plan['stores'].append(pltpu.make_async_copy(buf.at[pl.ds(boff, hr)], l_ref.at[me_q], lsem.at[2 * w + 1]))
            boff += hr
            for j, (cx, cy) in enumerate(chips):
                i = w * 3 + j
                got = l_ref.at[2 * cx + cy]
                plan['sends'].append(_rcopy(s_ref.at[2 * cx + cy], l_ref.at[me_q], send.at[i], recv.at[i], (cx, cy, c)))
                plan['arrivals'].append(_rcopy(got, got, send.at[i], recv.at[i], (cx, cy, c)))
        return plan


class SwapStage(_Stage):
    def __init__(self, halves):
        self.inputs = list(halves)
        self.out_shape = [jax.ShapeDtypeStruct((2,) + h.shape, F32) for h in halves]
        self.scratch = [pltpu.VMEM((sum(h.shape[0] for h in halves), D_MODEL), F32), pltpu.SemaphoreType.DMA((len(halves),)),
                        pltpu.SemaphoreType.DMA((len(halves),)), pltpu.SemaphoreType.DMA((2 * len(halves),))]

    def _plan(self, ins, outs, scr):
        buf, send, recv, lsem = scr
        x, y, c, _ = _place()
        plan = _empty_plan()
        boff = 0
        for w, (h_ref, o_ref) in enumerate(zip(ins, outs)):
            hr = h_ref.shape[0]
            plan['loads'].append(pltpu.make_async_copy(h_ref, buf.at[pl.ds(boff, hr)], lsem.at[2 * w]))
            plan['stores'].append(pltpu.make_async_copy(buf.at[pl.ds(boff, hr)], o_ref.at[c], lsem.at[2 * w + 1]))
            boff += hr
            got = o_ref.at[1 - c]
            plan['sends'].append(_rcopy(h_ref, o_ref.at[c], send.at[w], recv.at[w], (x, y, 1 - c)))
            plan['arrivals'].append(_rcopy(got, got, send.at[w], recv.at[w], (x, y, 1 - c)))
        return plan


class SmallGatherStage(_Stage):
    def __init__(self, blk):
        self.inputs = [blk]
        self.out_shape = [jax.ShapeDtypeStruct((N_DEV,) + blk.shape, blk.dtype)]
        self.scratch = [pltpu.VMEM(blk.shape, blk.dtype), pltpu.SemaphoreType.DMA((7,)), pltpu.SemaphoreType.DMA((7,)),
                        pltpu.SemaphoreType.DMA((2,))]

    def _plan(self, ins, outs, scr):
        (x_ref,), (o_ref,), (buf, send, recv, lsem) = ins, outs, scr
        x, y, c, chips = _place()
        sib = (x, y, 1 - c)

        def slot(px, py, pc):
            return o_ref.at[4 * px + 2 * py + pc]

        plan = _empty_plan()
        plan['loads'].append(pltpu.make_async_copy(x_ref, buf, lsem.at[0]))
        plan['stores'].append(pltpu.make_async_copy(buf, slot(x, y, c), lsem.at[1]))
        from_sib = slot(x, y, 1 - c)
        plan['sends'].append(_rcopy(x_ref, slot(x, y, c), send.at[0], recv.at[0], sib))
        plan['final_arrivals'].append(_rcopy(from_sib, from_sib, send.at[0], recv.at[0], sib))
        for j, (cx, cy) in enumerate(chips):
            got, got_sib = slot(cx, cy, c), slot(cx, cy, 1 - c)
            plan['sends'].append(_rcopy(x_ref, slot(x, y, c), send.at[1 + j], recv.at[1 + j], (cx, cy, c)))
            plan['arrivals'].append(_rcopy(got, got, send.at[1 + j], recv.at[1 + j], (cx, cy, c)))
            plan['forwards'].append(_rcopy(got, got, send.at[4 + j], recv.at[4 + j], sib))
            plan['final_arrivals'].append(_rcopy(got_sib, got_sib, send.at[4 + j], recv.at[4 + j], sib))
        return plan


def comm_call(name, stages):
    def body():
        pass

    return _call(body, name=name, grid=(1,), in_specs=[], out_specs=[], out_shape=[], args=[], stages=stages)[1]


def pair_sum(g4, land, c_arr, name):
    hr = g4.shape[2]

    def body(c_ref, g_ref, l_ref, o_ref):
        o_ref[0] = (g_ref[0, 0].astype(F32) + l_ref[0, 0].astype(F32)).astype(BF16)

    return pl.pallas_call(
        body, name=name,
        grid_spec=pltpu.PrefetchScalarGridSpec(
            num_scalar_prefetch=1, grid=(N_CHIPS,),
            in_specs=[pl.BlockSpec((1, 1, hr, D_MODEL), lambda q, c: (q, c[0], 0, 0)),
                      pl.BlockSpec((1, 1, hr, D_MODEL), lambda q, c: (q, 0, 0, 0))],
            out_specs=pl.BlockSpec((1, hr, D_MODEL), lambda q, c: (q, 0, 0))),
        out_shape=jax.ShapeDtypeStruct((N_CHIPS, hr, D_MODEL), BF16),
        compiler_params=_params(1),
    )(c_arr, g4, land)


def chip_sum(land, name):
    hr = land.shape[1]
    tr = hr if hr <= 352 else hr // 2

    def body(l_ref, o_ref):
        acc = l_ref[0].astype(F32)
        for s in range(1, N_CHIPS):
            acc = acc + l_ref[s].astype(F32)
        o_ref[...] = acc

    return pl.pallas_call(
        body, name=name, grid=(hr // tr,),
        in_specs=[pl.BlockSpec((N_CHIPS, tr, D_MODEL), lambda i: (0, i, 0))],
        out_specs=pl.BlockSpec((tr, D_MODEL), lambda i: (i, 0)),
        out_shape=jax.ShapeDtypeStruct((hr, D_MODEL), F32),
        compiler_params=_params(1),
    )(land)


def small_sum(parts):
    def body(p_ref, o_ref):
        acc = p_ref[0]
        for s in range(1, N_DEV):
            acc = acc + p_ref[s]
        o_ref[...] = acc

    return pl.pallas_call(
        body, name='small_sum', grid=(1,),
        in_specs=[pl.BlockSpec(parts.shape, lambda i: (0, 0, 0))],
        out_specs=pl.BlockSpec(parts.shape[1:], lambda i: (0, 0)),
        out_shape=jax.ShapeDtypeStruct(parts.shape[1:], F32),
        compiler_params=_params(1),
    )(parts)


def _adam_math(w, g, m, v):
    m2 = ADAM_B1 * m + (1.0 - ADAM_B1) * g
    v2 = ADAM_B2 * v + (1.0 - ADAM_B2) * (g * g)
    m_hat = m2 / (1.0 - ADAM_B1 ** ADAM_STEP)
    v_hat = v2 / (1.0 - ADAM_B2 ** ADAM_STEP)
    delta = -ADAM_LR * (m_hat / (jnp.sqrt(v_hat) + ADAM_EPS) + ADAM_WD * w)
    return delta, m2, v2


def _adam_body(transposed):
    def body(g_ref, w_ref, m_ref, v_ref, go_ref, d_ref, mo_ref, vo_ref):
        if transposed:
            gt = g_ref[...]
            g = gt.reshape(gt.shape[0] * gt.shape[1], gt.shape[2]).T
        else:
            g = g_ref[0]
        go_ref[...] = g
        d_ref[...], mo_ref[...], vo_ref[...] = _adam_math(w_ref[...], g, m_ref[...], v_ref[...])
    return body


def adam_rows(full, name, w, m, v, stages=()):
    hr = w.shape[0] // 2
    blk = pl.BlockSpec((hr, D_MODEL), lambda h: (h, 0))
    return _call(
        _adam_body(False), name='adam_' + name, grid=(2,),
        in_specs=[pl.BlockSpec((1, hr, D_MODEL), lambda h: (h, 0, 0)), blk, blk, blk],
        out_specs=[blk] * 4,
        out_shape=[jax.ShapeDtypeStruct(w.shape, F32)] * 4,
        args=[full, w, m, v], stages=stages)


def adam_cols(full, name, w, m, v, stages=()):
    cols = w.shape[1]
    hr = cols // 2
    tr = 128
    blk = pl.BlockSpec((tr, cols), lambda i: (i, 0))
    return _call(
        _adam_body(True), name='adam_' + name, grid=(D_MODEL // tr,),
        in_specs=[pl.BlockSpec((2, hr, tr), lambda i: (0, 0, i)), blk, blk, blk],
        out_specs=[blk] * 4,
        out_shape=[jax.ShapeDtypeStruct(w.shape, F32)] * 4,
        args=[full, w, m, v], stages=stages)


def adam_small(g, w, m, v):
    def body(g_ref, w_ref, m_ref, v_ref, d_ref, mo_ref, vo_ref):
        d_ref[...], mo_ref[...], vo_ref[...] = _adam_math(w_ref[...], g_ref[...], m_ref[...], v_ref[...])

    blk = pl.BlockSpec(w.shape, lambda i: (0, 0))
    return pl.pallas_call(
        body, name='adam_small', grid=(1,), in_specs=[blk] * 4, out_specs=[blk] * 3,
        out_shape=[jax.ShapeDtypeStruct(w.shape, F32)] * 3, compiler_params=_params(1),
    )(g, w, m, v)


WEIGHTS = ('ffn1_pre_g', 'ffn1_w_gu', 'ffn1_w_down', 'ffn1_post_g', 'mix_pre_g', 'w_in', 'conv_w', 'conv_b',
           'lru_w_a', 'lru_b_a', 'lru_w_x', 'lru_b_x', 'lru_lambda', 'attn_sinks', 'w_proj_lru', 'w_proj_attn',
           'w_out', 'mix_post_g', 'ffn2_pre_g', 'ffn2_w_gu', 'ffn2_w_down', 'ffn2_post_g')
SMALL = tuple(n for n in WEIGHTS if n not in PACK_OFF)


def _pack_small(d, conv_rows):
    sinks = jnp.pad(d['attn_sinks'].reshape(1, N_Q_HEADS), ((0, 0), (0, D_MODEL - N_Q_HEADS)))
    conv = jnp.pad(conv_rows, ((0, ROW_WA - ROW_CONV - conv_rows.shape[0]), (0, 0)))
    return jnp.concatenate([d[n].reshape(1, D_MODEL) for n in SMALL_VECS] + [sinks, conv]
                           + [d['lru_w_a'].reshape(64, D_MODEL), d['lru_w_x'].reshape(64, D_MODEL)], axis=0)


def _unpack_small(p, shapes):
    out = {n: p[k:k + 1].reshape(shapes[n]) for k, n in enumerate(SMALL_VECS)}
    out['attn_sinks'] = p[ROW_SINKS:ROW_SINKS + 1, :N_Q_HEADS].reshape(shapes['attn_sinks'])
    out['conv_w'] = p[ROW_CONV:ROW_CONV + 1].reshape(shapes['conv_w'])
    out['lru_w_a'] = p[ROW_WA:ROW_WA + 64].reshape(shapes['lru_w_a'])
    out['lru_w_x'] = p[ROW_WX:ROW_WX + 64].reshape(shapes['lru_w_x'])
    return out


def kernel(x, ffn1_pre_g, ffn1_w_gu, ffn1_w_down, ffn1_post_g, mix_pre_g, w_in, conv_w, conv_b, lru_w_a, lru_b_a, lru_w_x, lru_b_x, lru_lambda, attn_sinks, w_proj_lru, w_proj_attn, w_out, mix_post_g, ffn2_pre_g, ffn2_w_gu, ffn2_w_down, ffn2_post_g, loss_target, m_ffn1_pre_g, m_ffn1_w_gu, m_ffn1_w_down, m_ffn1_post_g, m_mix_pre_g, m_w_in, m_conv_w, m_conv_b, m_lru_w_a, m_lru_b_a, m_lru_w_x, m_lru_b_x, m_lru_lambda, m_attn_sinks, m_w_proj_lru, m_w_proj_attn, m_w_out, m_mix_post_g, m_ffn2_pre_g, m_ffn2_w_gu, m_ffn2_w_down, m_ffn2_post_g, v_ffn1_pre_g, v_ffn1_w_gu, v_ffn1_w_down, v_ffn1_post_g, v_mix_pre_g, v_w_in, v_conv_w, v_conv_b, v_lru_w_a, v_lru_b_a, v_lru_w_x, v_lru_b_x, v_lru_lambda, v_attn_sinks, v_w_proj_lru, v_w_proj_attn, v_w_out, v_mix_post_g, v_ffn2_pre_g, v_ffn2_w_gu, v_ffn2_w_down, v_ffn2_post_g):
    given = dict(locals())
    w = {n: given[n] for n in WEIGHTS}
    mom = {n: given['m_' + n] for n in WEIGHTS}
    var = {n: given['v_' + n] for n in WEIGHTS}
    shapes = {n: w[n].shape for n in WEIGHTS}
    xq = lax.axis_index('x')
    yq = lax.axis_index('y')
    cq = lax.axis_index('c')
    me_q = 2 * xq + yq

    c_arr = cq.reshape(1).astype(jnp.int32)
    xs, target = x[0], loss_target[0]
    sw = {n: (w[n][0] if w[n].ndim > 2 else w[n]) for n in SMALL}
    cos, sin_signed = _rope_tables()
    wa_bd = _block_diag(sw['lru_w_a'])
    wx_bd = _block_diag(sw['lru_w_x'])
    sinks = sw['attn_sinks'].reshape(N_Q_HEADS)

    pack = jnp.concatenate([(w[n][0].T if t else w[n][0]) for n, _, t in PACK], axis=0).astype(BF16)
    gw = {}
    conv_pad = jnp.pad(w['conv_w'][0], ((0, 4), (0, 0)))
    (gw['ffn1_w_gu'],), (conv_all,) = comm_call('gather_first', [GatherStage(['ffn1_w_gu'], pack), SmallGatherStage(conv_pad)])
    sw['conv_w'] = jnp.transpose(conv_all[0::2, :4, :], (1, 0, 2)).reshape(4, LRU_W)

    def gather(names):
        return [GatherStage(names, pack)]

    def reduce_start(names):
        g4 = [gb[n].reshape(N_CHIPS, 2, PACK_ROWS_OF[n] // 2, D_MODEL) for n in names]
        return PairStage(g4), g4

    def pair_sums(names, g4, lands):
        return [pair_sum(g, l, c_arr, 'pair_sum_' + n) for n, g, l in zip(names, g4, lands)]

    def chip_sums(names, lands):
        return [chip_sum(l, 'chip_sum_' + n) for n, l in zip(names, lands)]

    (n1, g1, u1, a1), ((gw['ffn1_w_down'],),) = ffn_fwd_a(xs, sw['ffn1_pre_g'], gw['ffn1_w_gu'], 'ffn1_fwd_a',
                                                           stages=gather(['ffn1_w_down']))
    (f1, h1), ((gw['w_in'],),) = ffn_fwd_b(a1, gw['ffn1_w_down'], sw['ffn1_post_g'], xs, 'ffn1_fwd_b', stages=gather(['w_in']))
    proj_names = ['w_proj_lru', 'w_proj_attn', 'w_out']
    (um, gate, xbr, q, k, v, g_lru, g_attn), (projs,) = mix_in(h1, sw['mix_pre_g'], gw['w_in'], 'mix_in', stages=gather(proj_names))
    (y_lru, h_lru), ((gw['ffn2_w_gu'],),) = lru_fwd(gate, xbr, sw['conv_w'], sw['conv_b'], wa_bd, sw['lru_b_a'], wx_bd,
                                                     sw['lru_b_x'], sw['lru_lambda'], 'lru_fwd', stages=gather(['ffn2_w_gu']))
    (qr, kr, y_attn), ((gw['ffn2_w_down'],),) = attn_fwd(q, k, v, cos, sin_signed, sinks, 'attn_fwd',
                                                          stages=gather(['ffn2_w_down']))
    (p_l, p_a, merged, m, h2), _ = merge_fwd(y_lru, y_attn, g_lru, g_attn, projs, sw['mix_post_g'], h1, 'merge_fwd')
    (n2, g2, u2, a2), _ = ffn_fwd_a(h2, sw['ffn2_pre_g'], gw['ffn2_w_gu'], 'ffn2_fwd_a')
    (f2, dy, loss_blk), _ = ffn_fwd_b(a2, gw['ffn2_w_down'], sw['ffn2_post_g'], h2, 'ffn2_fwd_b', target=target)
    loss = lax.psum(loss_blk[0, 0], ('x', 'y', 'c'))

    gs, gb, full = {}, {}, {}
    (df2, dgu2, gs['ffn2_post_g']), _ = ffn_bwd_a(dy, f2, sw['ffn2_post_g'], gw['ffn2_w_down'], g2, u2, 'ffn2_bwd_a')
    gb['ffn2_w_down'], _ = mm_tn([a2], df2, 256, 'ffn2_dw_down')
    gb['ffn2_w_gu'], _ = mm_tn([dgu2], n2, 512, 'ffn2_dw_gu')
    grp_c = ['ffn2_w_gu', 'ffn2_w_down']
    st, g4 = reduce_start(grp_c)
    (dh2, gs['ffn2_pre_g']), (lands,) = norm_bwd([dgu2], gw['ffn2_w_gu'], h2, sw['ffn2_pre_g'], dy, 'ffn2_bwd_b', stages=[st])
    sums_c = pair_sums(grp_c, g4, lands)

    (dm, dpl, dpa, dgl, dga, dya, dyl, gs['mix_post_g']), ((land_c1,),) = merge_bwd(
        dh2, m, sw['mix_post_g'], projs, g_lru, g_attn, p_l, p_a, 'merge_bwd', stages=[ChipStage(sums_c[1:])])
    gb['w_out'], _ = mm_tn([merged], dm, 512, 'dw_out')
    gb['w_proj_lru'], _ = mm_tn([y_lru], dpl, 512, 'dw_proj_lru')
    gb['w_proj_attn'], _ = mm_tn([y_attn], dpa, 512, 'dw_proj_attn')
    st, g4 = reduce_start(proj_names)
    (dq, dk, dv, dsk), ((land_c0,), lands) = attn_bwd(qr, kr, v, dya, cos, sin_signed, sinks, 'attn_bwd',
                                                      stages=[ChipStage(sums_c[:1]), st])
    gs['attn_sinks'] = dsk[0:1, 0:N_Q_HEADS]
    sums_b = pair_sums(proj_names, g4, lands)
    halves_c = chip_sums(grp_c, [land_c0, land_c1])
    (dgate, dxbr, vecs, dwa, dwx), (lands_b, fulls) = lru_bwd(
        gate, xbr, h_lru, dyl, sw['conv_w'], sw['conv_b'], wa_bd, sw['lru_b_a'], wx_bd, sw['lru_b_x'], sw['lru_lambda'],
        'lru_bwd', stages=[ChipStage(sums_b), SwapStage(halves_c)])
    full.update(zip(grp_c, fulls))
    gs['conv_w'] = vecs[0:4]
    gs['conv_b'], gs['lru_b_a'], gs['lru_b_x'], gs['lru_lambda'] = vecs[4:5], vecs[5:6], vecs[6:7], vecs[7:8]
    gs['lru_w_a'] = _diag_blocks(dwa)
    gs['lru_w_x'] = _diag_blocks(dwx)
    dz = [dgate, dxbr, dq, dk, dv, dgl, dga]
    gb['w_in'], _ = mm_tn(dz, um, 256, 'dw_in')
    halves_b = chip_sums(proj_names, lands_b)
    st, g4 = reduce_start(['w_in'])
    (dh1, gs['mix_pre_g']), (lands, fulls) = norm_bwd(dz, gw['w_in'], h1, sw['mix_pre_g'], dh2, 'mix_bwd_in',
                                                      stages=[st, SwapStage(halves_b)])
    full.update(zip(proj_names, fulls))
    sums_in = pair_sums(['w_in'], g4, lands)

    (df1, dgu1, gs['ffn1_post_g']), (lands_in,) = ffn_bwd_a(dh1, f1, sw['ffn1_post_g'], gw['ffn1_w_down'], g1, u1,
                                                            'ffn1_bwd_a', stages=[ChipStage(sums_in)])
    gb['ffn1_w_down'], _ = mm_tn([a1], df1, 256, 'ffn1_dw_down')
    gb['ffn1_w_gu'], _ = mm_tn([dgu1], n1, 512, 'ffn1_dw_gu')
    halves_in = chip_sums(['w_in'], lands_in)
    grp_a = ['ffn1_w_gu', 'ffn1_w_down']
    st, g4 = reduce_start(grp_a)
    (dx, gs['ffn1_pre_g']), (lands, fulls) = norm_bwd([dgu1], gw['ffn1_w_gu'], xs, sw['ffn1_pre_g'], dh1, 'ffn1_bwd_b',
                                                      stages=[st, SwapStage(halves_in)])
    full['w_in'] = fulls[0]
    sums_a = pair_sums(grp_a, g4, lands)
    small_blk = _pack_small(gs, gs['conv_w'])
    lands_a, (small_all,) = comm_call('reduce_last', [ChipStage(sums_a), SmallGatherStage(small_blk)])
    fulls = comm_call('swap_last', [SwapStage(chip_sums(grp_a, lands_a))])[0]
    full.update(zip(grp_a, fulls))

    out_g, out_d, out_m, out_v = {}, {}, {}, {}
    for n, _, t in PACK:
        fn = adam_cols if t else adam_rows
        (g_, d_, m_, v_), _ = fn(full[n], n, w[n][0], mom[n][0], var[n][0])
        out_g[n], out_d[n], out_m[n], out_v[n] = g_[None], d_[None], m_[None], v_[None]

    tot = small_sum(small_all)
    conv_g = lax.dynamic_slice(tot[ROW_CONV:ROW_CONV + 4], (0, me_q * (LRU_W // N_CHIPS)), (4, LRU_W // N_CHIPS))
    small_g = _unpack_small(tot, shapes)
    small_g['conv_w'] = conv_g.reshape(shapes['conv_w'])
    g_pack = jnp.concatenate([tot[:ROW_CONV], conv_g.reshape(1, D_MODEL), jnp.zeros((ROW_WA - ROW_CONV - 1, D_MODEL), F32),
                              tot[ROW_WA:]], axis=0)
    packs = [_pack_small({n: d[n] for n in SMALL}, d['conv_w'].reshape(1, D_MODEL)) for d in (w, mom, var)]
    d_p, m_p, v_p = adam_small(g_pack, *packs)
    for n in SMALL:
        out_g[n] = small_g[n]
    for dst, p in ((out_d, d_p), (out_m, m_p), (out_v, v_p)):
        dst.update(_unpack_small(p, shapes))

    return (loss, dx[None], *[out_g[n] for n in WEIGHTS], *[out_d[n] for n in WEIGHTS],
            *[out_m[n] for n in WEIGHTS], *[out_v[n] for n in WEIGHTS])
```

```python
import jax
import jax.numpy as jnp
from jax import lax
from jax.experimental import pallas as pl
from jax.experimental.pallas import tpu as pltpu

F32 = jnp.float32
BF16 = jnp.bfloat16

SEQ = 2048
D_MODEL = 1024
D_FF = 2816
LRU_W = 1024
LRU_BLOCK_W = 64
HEAD_DIM = 64
N_Q_HEADS = 16
N_KV_HEADS = 4
KV_W = N_KV_HEADS * HEAD_DIM
ATTN_BLOCK = 128
N_ATTN_BLOCKS = SEQ // ATTN_BLOCK
IN_SEGS = (1024, 1024, 1024, 256, 256, 1024, 1024)
IN_W = sum(IN_SEGS)
NORM_EPS = 1e-6
MASK_VALUE = -1e30
ROPE_THETA = 10000.0
LRU_C = 8.0
MACARON = 0.5
ADAM_LR = 0.001
ADAM_B1 = 0.9
ADAM_B2 = 0.999
ADAM_EPS = 1e-08
ADAM_WD = 0.01
ADAM_STEP = 10

N_CHIPS = 4
N_DEV = 8
VMEM_LIMIT = 56 * 1024 * 1024
MESH = pl.DeviceIdType.MESH
ANY = pl.BlockSpec(memory_space=pl.ANY)

PACK = (('ffn1_w_gu', 1408, True), ('w_in', 1408, True), ('ffn2_w_gu', 1408, True),
        ('ffn1_w_down', 704, False), ('ffn2_w_down', 704, False),
        ('w_proj_lru', 256, False), ('w_proj_attn', 256, False), ('w_out', 256, False))
PACK_ROWS_OF = {n: r for n, r, _ in PACK}
PACK_OFF = {}
_o = 0
for _n, _r, _t in PACK:
    PACK_OFF[_n] = _o
    _o += _r

SMALL_VECS = ('ffn1_pre_g', 'ffn1_post_g', 'mix_pre_g', 'conv_b', 'lru_b_a', 'lru_b_x', 'lru_lambda',
              'mix_post_g', 'ffn2_pre_g', 'ffn2_post_g')
SMALL_ROWS = 144
ROW_SINKS, ROW_CONV, ROW_WA, ROW_WX = 10, 11, 16, 80


def _dot(a, b):
    return jnp.dot(a, b, preferred_element_type=F32)


def _dot_nt(a, b):
    return lax.dot_general(a, b, (((1,), (1,)), ((), ())), preferred_element_type=F32)


def _dot_tn(a, b):
    return lax.dot_general(a, b, (((0,), (0,)), ((), ())), preferred_element_type=F32)


def _params(n_grid):
    return pltpu.CompilerParams(dimension_semantics=("arbitrary",) * n_grid, vmem_limit_bytes=VMEM_LIMIT)


def _sigmoid(x):
    return 1.0 / (1.0 + jnp.exp(-x))


def _rsqrt_mean_sq(x):
    return lax.rsqrt(jnp.mean(x * x, axis=-1, keepdims=True) + NORM_EPS)


def _expm1(x):
    poly = x * (1.0 + x * (0.5 + x * (1.0 / 6.0 + x * (1.0 / 24.0 + x * (1.0 / 120.0)))))
    return jnp.where(jnp.abs(x) < 0.1, poly, jnp.exp(x) - 1.0)


_GELU_K = 0.7978845608028654
_GELU_C = 0.044715


def _gelu(x):
    t = jnp.tanh(_GELU_K * (x + _GELU_C * x * x * x))
    return 0.5 * x * (1.0 + t), t


def _gelu_grad(x, t):
    return 0.5 * (1.0 + t) + 0.5 * x * (1.0 - t * t) * _GELU_K * (1.0 + 3.0 * _GELU_C * x * x)


def _load_weight(w_refs, dst_ref, sem):
    w_refs = list(w_refs) if isinstance(w_refs, (list, tuple)) else [w_refs]
    rows = dst_ref.shape[0] // N_CHIPS
    rp = rows // len(w_refs)
    cps = [pltpu.make_async_copy(w_ref.at[q], dst_ref.at[pl.ds(q * rows + p * rp, rp)], sem.at[p * N_CHIPS + q])
           for p, w_ref in enumerate(w_refs) for q in range(N_CHIPS)]
    for cp in cps:
        cp.start()
    for cp in cps:
        cp.wait()


def _weight_scratch(rows_total, parts=1):
    return [pltpu.VMEM((rows_total, D_MODEL), BF16), pltpu.SemaphoreType.DMA((N_CHIPS * parts,))]


_ROW = lambda tm: pl.BlockSpec((tm, D_MODEL), lambda i: (i, 0))
_VEC = pl.BlockSpec((1, D_MODEL), lambda i: (0, 0))


def _call(body, *, name, grid, in_specs, out_specs, out_shape, args, scratch_shapes=(), stages=()):
    in_specs, out_specs, out_shape, scratch_shapes = list(in_specs), list(out_specs), list(out_shape), list(scratch_shapes)
    n_in, n_out, n_sc = len(in_specs), len(out_specs), len(scratch_shapes)
    k_in = [len(s.inputs) for s in stages]
    k_out = [len(s.out_shape) for s in stages]
    k_sc = [len(s.scratch) for s in stages]
    last = grid[0] - 1

    def split(refs, counts):
        parts, pos = [], 0
        for k in counts:
            parts.append(refs[pos:pos + k])
            pos += k
        return parts

    def full(*refs):
        ins, s_ins, outs, s_outs, scr, s_scr = split(refs, [n_in, sum(k_in), n_out, sum(k_out), n_sc, sum(k_sc)])
        per_stage = list(zip(stages, split(s_ins, k_in), split(s_outs, k_out), split(s_scr, k_sc)))
        i = pl.program_id(0)
        if stages:
            @pl.when(i == 0)
            def _():
                for s, a, b, c in per_stage:
                    s.start(a, b, c)

        body(*ins, *outs, *scr)
        if stages:
            @pl.when(i == max(last - 1, 0))
            def _():
                for s, a, b, c in per_stage:
                    s.mid(a, b, c)

            @pl.when(i == last)
            def _():
                for s, a, b, c in per_stage:
                    s.end(a, b, c)

    res = pl.pallas_call(
        full, name=name, grid=grid,
        in_specs=in_specs + [ANY] * sum(k_in),
        out_specs=out_specs + [ANY] * sum(k_out),
        out_shape=out_shape + [o for s in stages for o in s.out_shape],
        scratch_shapes=scratch_shapes + [x for s in stages for x in s.scratch],
        compiler_params=_params(1),
    )(*args, *[a for s in stages for a in s.inputs])
    return list(res[:n_out]), split(list(res[n_out:]), k_out)


def ffn_fwd_a(x, g_pre, w_gu_t, name, stages=()):
    tm, tn = 256, 256
    n_w = len(w_gu_t)

    def body(x_ref, gp_ref, *refs):
        w_refs = refs[:n_w]
        n_ref, g_ref, u_ref, a_ref, wt_ref, sem = refs[n_w:]

        @pl.when(pl.program_id(0) == 0)
        def _():
            _load_weight(w_refs, wt_ref, sem)

        xv = x_ref[...]
        n = (xv * _rsqrt_mean_sq(xv) * gp_ref[...]).astype(BF16)
        n_ref[...] = n
        for j in range(D_FF // tn):
            g = _dot_nt(n, wt_ref[j * tn:(j + 1) * tn, :])
            u = _dot_nt(n, wt_ref[D_FF + j * tn:D_FF + (j + 1) * tn, :])
            g_ref[:, j * tn:(j + 1) * tn] = g.astype(BF16)
            u_ref[:, j * tn:(j + 1) * tn] = u.astype(BF16)
            a_ref[:, j * tn:(j + 1) * tn] = (g * _sigmoid(g) * u).astype(BF16)

    wide = pl.BlockSpec((tm, D_FF), lambda i: (i, 0))
    return _call(
        body, name=name, grid=(SEQ // tm,),
        in_specs=[_ROW(tm), _VEC] + [ANY] * n_w,
        out_specs=[_ROW(tm), wide, wide, wide],
        out_shape=[jax.ShapeDtypeStruct((SEQ, D_MODEL), BF16)] + [jax.ShapeDtypeStruct((SEQ, D_FF), BF16)] * 3,
        scratch_shapes=_weight_scratch(2 * D_FF, n_w),
        args=[x, g_pre, *w_gu_t], stages=stages)


def ffn_fwd_b(a, w_down, g_post, h_in, name, target=None, stages=()):
    tm = 256
    final = target is not None

    def body(*refs):
        if final:
            a_ref, wf_ref, gp_ref, h_ref, t_ref, f_ref, o_ref, loss_ref, wd_ref, sem = refs
        else:
            a_ref, wf_ref, gp_ref, h_ref, f_ref, o_ref, wd_ref, sem = refs

        @pl.when(pl.program_id(0) == 0)
        def _():
            _load_weight(wf_ref, wd_ref, sem)
            if final:
                loss_ref[...] = jnp.zeros_like(loss_ref)

        f = _dot(a_ref[...], wd_ref[...])
        f_ref[...] = f
        y = h_ref[...] + MACARON * (f * _rsqrt_mean_sq(f) * gp_ref[...])
        if final:
            err = y - t_ref[...]
            o_ref[...] = err * (1.0 / D_MODEL)
            loss_ref[...] += 0.5 * jnp.sum(err * err) * (1.0 / D_MODEL)
        else:
            o_ref[...] = y

    row = _ROW(tm)
    in_specs = [pl.BlockSpec((tm, D_FF), lambda i: (i, 0)), ANY, _VEC, row]
    out_specs = [row, row]
    out_shape = [jax.ShapeDtypeStruct((SEQ, D_MODEL), F32)] * 2
    args = [a, w_down, g_post, h_in]
    if final:
        in_specs.append(row)
        args.append(target)
        out_specs.append(pl.BlockSpec((8, 128), lambda i: (0, 0)))
        out_shape.append(jax.ShapeDtypeStruct((8, 128), F32))
    return _call(body, name=name, grid=(SEQ // tm,), in_specs=in_specs, out_specs=out_specs,
                 out_shape=out_shape, scratch_shapes=_weight_scratch(D_FF), args=args, stages=stages)


def ffn_bwd_a(d_out, f, g_post, w_down, g, u, name, stages=()):
    tm = 256

    def body(do_ref, f_ref, gp_ref, wf_ref, g_ref, u_ref, df_ref, dgu_ref, dgp_ref, wd_ref, sem):
        @pl.when(pl.program_id(0) == 0)
        def _():
            _load_weight(wf_ref, wd_ref, sem)
            dgp_ref[...] = jnp.zeros_like(dgp_ref)

        fv = f_ref[...]
        rf = _rsqrt_mean_sq(fv)
        fh = fv * rf
        dn = MACARON * do_ref[...]
        dgp_ref[...] += jnp.sum(dn * fh, axis=0, keepdims=True)
        t = dn * gp_ref[...]
        df = (rf * (t - fh * jnp.mean(t * fh, axis=-1, keepdims=True))).astype(BF16)
        df_ref[...] = df
        da = _dot_nt(df, wd_ref[...])
        gv = g_ref[...].astype(F32)
        uv = u_ref[...].astype(F32)
        s = _sigmoid(gv)
        dgu_ref[:, :D_FF] = (da * uv * s * (1.0 + gv * (1.0 - s))).astype(BF16)
        dgu_ref[:, D_FF:] = (da * gv * s).astype(BF16)

    row = _ROW(tm)
    wide = pl.BlockSpec((tm, D_FF), lambda i: (i, 0))
    return _call(
        body, name=name, grid=(SEQ // tm,),
        in_specs=[row, row, _VEC, ANY, wide, wide],
        out_specs=[row, pl.BlockSpec((tm, 2 * D_FF), lambda i: (i, 0)), _VEC],
        out_shape=[jax.ShapeDtypeStruct((SEQ, D_MODEL), BF16), jax.ShapeDtypeStruct((SEQ, 2 * D_FF), BF16),
                   jax.ShapeDtypeStruct((1, D_MODEL), F32)],
        scratch_shapes=_weight_scratch(D_FF),
        args=[d_out, f, g_post, w_down, g, u], stages=stages)


def norm_bwd(pieces, w_t, x, g_pre, d_res, name, stages=()):
    tm = 256
    widths = [p.shape[1] for p in pieces]
    offs = [sum(widths[:k]) for k in range(len(widths))]
    n_p = len(pieces)
    n_w = len(w_t)

    def body(*refs):
        p_refs = refs[:n_p]
        w_refs = refs[n_p:n_p + n_w]
        x_ref, g_ref, r_ref, dx_ref, dg_ref, wt_ref, sem = refs[n_p + n_w:]

        @pl.when(pl.program_id(0) == 0)
        def _():
            _load_weight(w_refs, wt_ref, sem)
            dg_ref[...] = jnp.zeros_like(dg_ref)

        dn = None
        for p_ref, lo, wd in zip(p_refs, offs, widths):
            part = _dot(p_ref[...], wt_ref[lo:lo + wd, :])
            dn = part if dn is None else dn + part
        xv = x_ref[...]
        r = _rsqrt_mean_sq(xv)
        xh = xv * r
        dg_ref[...] += jnp.sum(dn * xh, axis=0, keepdims=True)
        t = dn * g_ref[...]
        dx_ref[...] = r_ref[...] + r * (t - xh * jnp.mean(t * xh, axis=-1, keepdims=True))

    row = _ROW(tm)
    return _call(
        body, name=name, grid=(SEQ // tm,),
        in_specs=[pl.BlockSpec((tm, wd), lambda i: (i, 0)) for wd in widths] + [ANY] * n_w + [row, _VEC, row],
        out_specs=[row, _VEC],
        out_shape=[jax.ShapeDtypeStruct((SEQ, D_MODEL), F32), jax.ShapeDtypeStruct((1, D_MODEL), F32)],
        scratch_shapes=_weight_scratch(sum(widths), n_w),
        args=[*pieces, *w_t, x, g_pre, d_res], stages=stages)


def mm_tn(pieces, b, tm, name, stages=()):
    widths = [p.shape[1] for p in pieces]
    m_total = sum(widths)
    n_p = len(pieces)
    starts = [sum(widths[:k]) // tm for k in range(n_p)]
    counts = [wd // tm for wd in widths]

    def body(*refs):
        p_refs = refs[:n_p]
        b_ref, o_ref = refs[n_p:]
        i = pl.program_id(0)
        for p_ref, st, ct in zip(p_refs, starts, counts):
            @pl.when((i >= st) & (i < st + ct))
            def _(p_ref=p_ref):
                o_ref[...] = _dot_tn(p_ref[...], b_ref[...]).astype(BF16)

    def piece_spec(st, ct):
        return pl.BlockSpec((SEQ, tm), lambda i: (0, jnp.clip(i - st, 0, ct - 1)))

    (out,), stage_out = _call(
        body, name=name, grid=(m_total // tm,),
        in_specs=[piece_spec(st, ct) for st, ct in zip(starts, counts)] + [pl.BlockSpec((SEQ, D_MODEL), lambda i: (0, 0))],
        out_specs=[pl.BlockSpec((tm, D_MODEL), lambda i: (i, 0))],
        out_shape=[jax.ShapeDtypeStruct((m_total, D_MODEL), BF16)],
        args=[*pieces, b], stages=stages)
    return out, stage_out


def mix_in(h, g_pre, w_in_t, name, stages=()):
    tm = 256
    offs = [sum(IN_SEGS[:k]) for k in range(len(IN_SEGS))]
    dts = [F32, F32, F32, F32, BF16, F32, F32]
    n_o = len(IN_SEGS)

    def body(*refs):
        h_ref, g_ref, wf_ref, um_ref = refs[:4]
        o_refs = refs[4:4 + n_o]
        wt_ref, sem = refs[4 + n_o:]

        @pl.when(pl.program_id(0) == 0)
        def _():
            _load_weight(wf_ref, wt_ref, sem)

        hv = h_ref[...]
        um = (hv * _rsqrt_mean_sq(hv) * g_ref[...]).astype(BF16)
        um_ref[...] = um
        for o_ref, lo, wd in zip(o_refs, offs, IN_SEGS):
            for c0 in range(0, wd, 256):
                o_ref[:, c0:c0 + 256] = _dot_nt(um, wt_ref[lo + c0:lo + c0 + 256, :]).astype(o_ref.dtype)

    return _call(
        body, name=name, grid=(SEQ // tm,),
        in_specs=[_ROW(tm), _VEC, ANY],
        out_specs=[_ROW(tm)] + [pl.BlockSpec((tm, wd), lambda i: (i, 0)) for wd in IN_SEGS],
        out_shape=[jax.ShapeDtypeStruct((SEQ, D_MODEL), BF16)]
        + [jax.ShapeDtypeStruct((SEQ, wd), dt) for wd, dt in zip(IN_SEGS, dts)],
        scratch_shapes=_weight_scratch(IN_W),
        args=[h, g_pre, w_in_t], stages=stages)


LRU_TC = 256


def _conv_fwd(xb, cw, cb, tt):
    xc = xb * cw[3:4, :] + cb
    shifted = []
    for s in (1, 2, 3):
        sh = jnp.where(tt >= s, pltpu.roll(xb, s, 0), 0.0)
        shifted.append(sh)
        xc = xc + sh * cw[3 - s:4 - s, :]
    return xc, shifted


def _lru_gates(xc, wa, ba, wx, bx, lam):
    xcb = xc.astype(BF16)
    r = _sigmoid(_dot(xcb, wa) + ba)
    i = _sigmoid(_dot(xcb, wx) + bx)
    nl = -lam
    sp = jnp.maximum(nl, 0.0) + jnp.log1p(jnp.exp(-jnp.abs(nl)))
    la = (-LRU_C * r) * sp
    a = jnp.exp(la)
    mult = jnp.sqrt(jnp.maximum(-_expm1(2.0 * la), 0.0))
    return xcb, r, i, sp, a, mult


def _scan(a, b, tt, reverse):
    n = a.shape[0]
    s = 1
    while s < n:
        if reverse:
            keep = tt < n - s
            shift = n - s
        else:
            keep = tt >= s
            shift = s
        b = a * jnp.where(keep, pltpu.roll(b, shift, 0), 0.0) + b
        if 2 * s < n:
            a = a * jnp.where(keep, pltpu.roll(a, shift, 0), 1.0)
        s *= 2
    return b


def _lru_specs():
    col = pl.BlockSpec((SEQ, LRU_TC), lambda j: (0, j))
    vec = pl.BlockSpec((1, LRU_TC), lambda j: (0, j))
    bd = pl.BlockSpec((1, LRU_TC, LRU_TC), lambda j: (j, 0, 0))
    cw = pl.BlockSpec((4, LRU_TC), lambda j: (0, j))
    return col, vec, bd, cw


def lru_fwd(gate, xbr, conv_w, conv_b, wa_bd, b_a, wx_bd, b_x, lam, name, stages=()):
    col, vec, bd, cw = _lru_specs()

    def body(gate_ref, xbr_ref, cw_ref, cb_ref, wa_ref, ba_ref, wx_ref, bx_ref, lam_ref, y_ref, h_ref):
        tt = lax.broadcasted_iota(jnp.int32, (SEQ, LRU_TC), 0)
        xc, _ = _conv_fwd(xbr_ref[...], cw_ref[...], cb_ref[...], tt)
        _, r, i, sp, a, mult = _lru_gates(xc, wa_ref[0], ba_ref[...], wx_ref[0], bx_ref[...], lam_ref[...])
        h = _scan(a, mult * (i * xc), tt, reverse=False)
        h_ref[...] = h
        gl, _ = _gelu(gate_ref[...])
        y_ref[...] = (h * gl).astype(BF16)

    return _call(
        body, name=name, grid=(LRU_W // LRU_TC,),
        in_specs=[col, col, cw, vec, bd, vec, bd, vec, vec],
        out_specs=[col, col],
        out_shape=[jax.ShapeDtypeStruct((SEQ, LRU_W), BF16), jax.ShapeDtypeStruct((SEQ, LRU_W), F32)],
        args=[gate, xbr, conv_w, conv_b, wa_bd, b_a, wx_bd, b_x, lam], stages=stages)


def lru_bwd(gate, xbr, h, dy, conv_w, conv_b, wa_bd, b_a, wx_bd, b_x, lam, name, stages=()):
    col, vec, bd, cw = _lru_specs()

    def body(gate_ref, xbr_ref, h_ref, dy_ref, cw_ref, cb_ref, wa_ref, ba_ref, wx_ref, bx_ref, lam_ref,
             dgate_ref, dxbr_ref, vecs_ref, dwa_ref, dwx_ref):
        tt = lax.broadcasted_iota(jnp.int32, (SEQ, LRU_TC), 0)
        cwv = cw_ref[...]
        lam = lam_ref[...]
        xb = xbr_ref[...]
        xc, shifted = _conv_fwd(xb, cwv, cb_ref[...], tt)
        wa = wa_ref[0]
        wx = wx_ref[0]
        xcb, r, i, sp, a, mult = _lru_gates(xc, wa, ba_ref[...], wx, bx_ref[...], lam)
        hv = h_ref[...]
        dyv = dy_ref[...]
        gv = gate_ref[...]
        gl, th = _gelu(gv)
        dgate_ref[...] = (dyv * hv * _gelu_grad(gv, th)).astype(BF16)
        a_next = jnp.where(tt < SEQ - 1, pltpu.roll(a, SEQ - 1, 0), 0.0)
        gsum = _scan(a_next, dyv * gl, tt, reverse=True)
        h_prev = jnp.where(tt >= 1, pltpu.roll(hv, 1, 0), 0.0)
        d_mult = gsum * i * xc
        d_i = gsum * mult * xc
        d_xc = gsum * mult * i
        d_la = gsum * h_prev * a - d_mult * (a * a) / mult
        d_pr = (d_la * (-LRU_C * sp)) * r * (1.0 - r)
        d_pi = d_i * i * (1.0 - i)
        d_lam = jnp.sum(d_la * r, axis=0, keepdims=True) * (LRU_C * _sigmoid(-lam))
        d_prb = d_pr.astype(BF16)
        d_pib = d_pi.astype(BF16)
        d_xc = d_xc + _dot_nt(d_prb, wa) + _dot_nt(d_pib, wx)
        dwa_ref[0] = _dot_tn(xcb, d_prb)
        dwx_ref[0] = _dot_tn(xcb, d_pib)
        rows = [jnp.sum(d_xc * shifted[2], axis=0, keepdims=True),
                jnp.sum(d_xc * shifted[1], axis=0, keepdims=True),
                jnp.sum(d_xc * shifted[0], axis=0, keepdims=True),
                jnp.sum(d_xc * xb, axis=0, keepdims=True),
                jnp.sum(d_xc, axis=0, keepdims=True),
                jnp.sum(d_pr, axis=0, keepdims=True),
                jnp.sum(d_pi, axis=0, keepdims=True),
                d_lam]
        ri = lax.broadcasted_iota(jnp.int32, (8, LRU_TC), 0)
        acc = jnp.zeros((8, LRU_TC), F32)
        for k, rv in enumerate(rows):
            acc = jnp.where(ri == k, rv, acc)
        vecs_ref[...] = acc
        d_xb = d_xc * cwv[3:4, :]
        for s in (1, 2, 3):
            d_xb = d_xb + jnp.where(tt < SEQ - s, pltpu.roll(d_xc, SEQ - s, 0), 0.0) * cwv[3 - s:4 - s, :]
        dxbr_ref[...] = d_xb.astype(BF16)

    return _call(
        body, name=name, grid=(LRU_W // LRU_TC,),
        in_specs=[col, col, col, col, cw, vec, bd, vec, bd, vec, vec],
        out_specs=[col, col, pl.BlockSpec((8, LRU_TC), lambda j: (0, j)), bd, bd],
        out_shape=[jax.ShapeDtypeStruct((SEQ, LRU_W), BF16), jax.ShapeDtypeStruct((SEQ, LRU_W), BF16),
                   jax.ShapeDtypeStruct((8, LRU_W), F32),
                   jax.ShapeDtypeStruct((LRU_W // LRU_TC, LRU_TC, LRU_TC), F32),
                   jax.ShapeDtypeStruct((LRU_W // LRU_TC, LRU_TC, LRU_TC), F32)],
        args=[gate, xbr, h, dy, conv_w, conv_b, wa_bd, b_a, wx_bd, b_x, lam], stages=stages)


def _rope(x, cos, sin_signed):
    w = x.shape[1]
    reps = w // 128
    if reps > 1:
        cos = jnp.tile(cos, (1, reps))
        sin_signed = jnp.tile(sin_signed, (1, reps))
    lane = lax.broadcasted_iota(jnp.int32, x.shape, 1)
    first = (lane & 63) < 32
    partner = jnp.where(first, pltpu.roll(x, w - 32, 1), pltpu.roll(x, 32, 1))
    return x * cos + partner * sin_signed


def _both_halves(t, odd):
    lo = lax.broadcasted_iota(jnp.int32, t.shape, 1) < 64
    rolled = pltpu.roll(t, 64, 1)
    return jnp.where(lo, rolled, t) if odd else jnp.where(lo, t, rolled)


def _stack_heads(ta, tb):
    lo = lax.broadcasted_iota(jnp.int32, ta.shape, 1) < 64
    return jnp.concatenate([jnp.where(lo, ta, 0.0), jnp.where(lo, 0.0, ta),
                            jnp.where(lo, tb, 0.0), jnp.where(lo, 0.0, tb)], axis=0)


def _unstack_heads(o):
    lo = lax.broadcasted_iota(jnp.int32, (ATTN_BLOCK, 128), 1) < 64
    return (jnp.where(lo, o[0:128], o[128:256]), jnp.where(lo, o[256:384], o[384:512]))


def _attn_probs(qs, kd, sinks_ref, hk, first_block):
    s = _dot_nt(qs, kd) * (HEAD_DIM ** -0.5)
    row = lax.broadcasted_iota(jnp.int32, s.shape, 0)
    si = lax.broadcasted_iota(jnp.int32, s.shape, 1)
    diff = ATTN_BLOCK + (row & (ATTN_BLOCK - 1)) - si
    valid = (diff >= 0) & (diff < ATTN_BLOCK) & ((si >= ATTN_BLOCK) | jnp.logical_not(first_block))
    s = jnp.where(valid, s, MASK_VALUE)
    rg = lax.broadcasted_iota(jnp.int32, (4 * ATTN_BLOCK, 1), 0) >> 7
    sink = jnp.where(rg == 0, sinks_ref[4 * hk],
                     jnp.where(rg == 1, sinks_ref[4 * hk + 1],
                               jnp.where(rg == 2, sinks_ref[4 * hk + 2], sinks_ref[4 * hk + 3])))
    m = jnp.maximum(jnp.max(s, axis=1, keepdims=True), sink)
    e = jnp.exp(s - m)
    es = jnp.exp(sink - m)
    inv = 1.0 / (jnp.sum(e, axis=1, keepdims=True) + es)
    return e * inv, es * inv


def _prev(i):
    return jnp.maximum(i - 1, 0)


def attn_fwd(q, k, v, cos, sin_signed, sinks, name, stages=()):
    nb = ATTN_BLOCK

    def body(q_ref, kc_ref, kp_ref, vc_ref, vp_ref, cc_ref, sc_ref, cp_ref, sp_ref, sinks_ref,
             qr_ref, kr_ref, y_ref):
        first_block = pl.program_id(0) == 0
        qr = _rope(q_ref[...], cc_ref[...], sc_ref[...])
        kc = _rope(kc_ref[...], cc_ref[...], sc_ref[...])
        kp = _rope(kp_ref[...], cp_ref[...], sp_ref[...])
        qr_ref[...] = qr.astype(BF16)
        kr_ref[...] = kc.astype(BF16)
        k2 = jnp.concatenate([kp, kc], axis=0)
        v2 = jnp.concatenate([vp_ref[...].astype(F32), vc_ref[...].astype(F32)], axis=0)
        for hk in range(N_KV_HEADS):
            kt = hk // 2
            kd = _both_halves(k2[:, kt * 128:(kt + 1) * 128], hk % 2).astype(BF16)
            vd = _both_halves(v2[:, kt * 128:(kt + 1) * 128], hk % 2).astype(BF16)
            qs = _stack_heads(qr[:, (2 * hk) * 128:(2 * hk + 1) * 128],
                              qr[:, (2 * hk + 1) * 128:(2 * hk + 2) * 128]).astype(BF16)
            p, _ = _attn_probs(qs, kd, sinks_ref, hk, first_block)
            ta, tb = _unstack_heads(_dot(p.astype(BF16), vd))
            y_ref[:, (2 * hk) * 128:(2 * hk + 1) * 128] = ta.astype(BF16)
            y_ref[:, (2 * hk + 1) * 128:(2 * hk + 2) * 128] = tb.astype(BF16)

    cur = lambda w: pl.BlockSpec((nb, w), lambda i: (i, 0))
    prv = lambda w: pl.BlockSpec((nb, w), lambda i: (_prev(i), 0))
    return _call(
        body, name=name, grid=(N_ATTN_BLOCKS,),
        in_specs=[cur(D_MODEL), cur(KV_W), prv(KV_W), cur(KV_W), prv(KV_W), cur(128), cur(128), prv(128), prv(128),
                  pl.BlockSpec(memory_space=pltpu.SMEM)],
        out_specs=[cur(D_MODEL), cur(KV_W), cur(D_MODEL)],
        out_shape=[jax.ShapeDtypeStruct((SEQ, D_MODEL), BF16), jax.ShapeDtypeStruct((SEQ, KV_W), BF16),
                   jax.ShapeDtypeStruct((SEQ, D_MODEL), BF16)],
        args=[q, k, k, v, v, cos, sin_signed, cos, sin_signed, sinks], stages=stages)


def attn_bwd(qr, kr, v, dy, cos, sin_signed, sinks, name, stages=()):
    nb = ATTN_BLOCK
    n_steps = N_ATTN_BLOCKS + 1
    scale = HEAD_DIM ** -0.5

    def body(q_ref, kc_ref, kp_ref, vc_ref, vp_ref, dy_ref, cc_ref, sc_ref, cp_ref, sp_ref, sinks_ref,
             dq_ref, dk_ref, dv_ref, dsk_ref, ck_ref, cv_ref):
        i = pl.program_id(0)

        @pl.when(i == 0)
        def _():
            dsk_ref[...] = jnp.zeros_like(dsk_ref)
            ck_ref[...] = jnp.zeros_like(ck_ref)
            cv_ref[...] = jnp.zeros_like(cv_ref)

        @pl.when(i < N_ATTN_BLOCKS)
        def _():
            qv = q_ref[...].astype(F32)
            dov = dy_ref[...].astype(F32)
            k2 = jnp.concatenate([kp_ref[...].astype(F32), kc_ref[...].astype(F32)], axis=0)
            v2 = jnp.concatenate([vp_ref[...].astype(F32), vc_ref[...].astype(F32)], axis=0)
            lane = lax.broadcasted_iota(jnp.int32, (8, 128), 1)
            lo = lax.broadcasted_iota(jnp.int32, (2 * nb, 128), 1) < 64
            dsk = jnp.zeros((8, 128), F32)
            dk_tiles = []
            dv_tiles = []
            for hk in range(N_KV_HEADS):
                kt = hk // 2
                kd = _both_halves(k2[:, kt * 128:(kt + 1) * 128], hk % 2).astype(BF16)
                vd = _both_halves(v2[:, kt * 128:(kt + 1) * 128], hk % 2).astype(BF16)
                qs = _stack_heads(qv[:, (2 * hk) * 128:(2 * hk + 1) * 128],
                                  qv[:, (2 * hk + 1) * 128:(2 * hk + 2) * 128]).astype(BF16)
                dos = _stack_heads(dov[:, (2 * hk) * 128:(2 * hk + 1) * 128],
                                   dov[:, (2 * hk + 1) * 128:(2 * hk + 2) * 128]).astype(BF16)
                p, ps = _attn_probs(qs, kd, sinks_ref, hk, i == 0)
                dp = _dot_nt(dos, vd)
                delta = jnp.sum(p * dp, axis=1, keepdims=True)
                ds = (p * (dp - delta)).astype(BF16)
                dsink = -ps * delta
                for g in range(4):
                    dsk = dsk + jnp.where(lane == 4 * hk + g, jnp.sum(dsink[g * nb:(g + 1) * nb]), 0.0)
                ta, tb = _unstack_heads(_dot(ds, kd) * scale)
                dq_a = (2 * hk) * 128
                dq_ref[:, dq_a:dq_a + 128] = _rope(ta, cc_ref[...], -sc_ref[...]).astype(BF16)
                dq_ref[:, dq_a + 128:dq_a + 256] = _rope(tb, cc_ref[...], -sc_ref[...]).astype(BF16)
                rk = _dot_tn(ds, qs) * scale
                rv = _dot_tn(p.astype(BF16), dos)
                dk_tiles.append(rk + pltpu.roll(rk, 64, 1))
                dv_tiles.append(rv + pltpu.roll(rv, 64, 1))
            dsk_ref[...] += dsk
            dk_full = jnp.concatenate([jnp.where(lo, dk_tiles[0], dk_tiles[1]),
                                       jnp.where(lo, dk_tiles[2], dk_tiles[3])], axis=1)
            dv_full = jnp.concatenate([jnp.where(lo, dv_tiles[0], dv_tiles[1]),
                                       jnp.where(lo, dv_tiles[2], dv_tiles[3])], axis=1)
            dk_ref[...] = _rope(ck_ref[...] + dk_full[0:nb], cp_ref[...], -sp_ref[...]).astype(BF16)
            dv_ref[...] = (cv_ref[...] + dv_full[0:nb]).astype(BF16)
            ck_ref[...] = dk_full[nb:2 * nb]
            cv_ref[...] = dv_full[nb:2 * nb]

        @pl.when(i == N_ATTN_BLOCKS)
        def _():
            dk_ref[...] = _rope(ck_ref[...], cp_ref[...], -sp_ref[...]).astype(BF16)
            dv_ref[...] = cv_ref[...].astype(BF16)

    qi = lambda i: jnp.minimum(i, N_ATTN_BLOCKS - 1)
    cur = lambda w: pl.BlockSpec((nb, w), lambda i: (qi(i), 0))
    prv = lambda w: pl.BlockSpec((nb, w), lambda i: (_prev(qi(i)), 0))
    out_prev = lambda w: pl.BlockSpec((nb, w), lambda i: (_prev(i), 0))
    return _call(
        body, name=name, grid=(n_steps,),
        in_specs=[cur(D_MODEL), cur(KV_W), prv(KV_W), cur(KV_W), prv(KV_W), cur(D_MODEL),
                  cur(128), cur(128), out_prev(128), out_prev(128), pl.BlockSpec(memory_space=pltpu.SMEM)],
        out_specs=[cur(D_MODEL), out_prev(KV_W), out_prev(KV_W), pl.BlockSpec((8, 128), lambda i: (0, 0))],
        out_shape=[jax.ShapeDtypeStruct((SEQ, D_MODEL), BF16), jax.ShapeDtypeStruct((SEQ, KV_W), BF16),
                   jax.ShapeDtypeStruct((SEQ, KV_W), BF16), jax.ShapeDtypeStruct((8, 128), F32)],
        scratch_shapes=[pltpu.VMEM((nb, KV_W), F32), pltpu.VMEM((nb, KV_W), F32)],
        args=[qr, kr, kr, v, v, dy, cos, sin_signed, cos, sin_signed, sinks], stages=stages)


def _proj_scratch():
    return [pltpu.VMEM((D_MODEL, D_MODEL), BF16)] * 3 + [pltpu.SemaphoreType.DMA((3 * N_CHIPS,))]


def _load_projs(w_refs, wl_ref, wa_ref, wo_ref, sem):
    for k, (w_ref, dst) in enumerate(zip(w_refs, (wl_ref, wa_ref, wo_ref))):
        _load_weight(w_ref, dst, sem.at[pl.ds(k * N_CHIPS, N_CHIPS)])


def merge_fwd(y_lru, y_attn, g_lru, g_attn, projs, g_post, h_in, name, stages=()):
    tm = 256

    def body(yl_ref, ya_ref, gl_ref, ga_ref, w1_ref, w2_ref, w3_ref, gp_ref, h_ref,
             pl_ref, pa_ref, mg_ref, m_ref, o_ref, wl_ref, wa_ref, wo_ref, sem):
        @pl.when(pl.program_id(0) == 0)
        def _():
            _load_projs((w1_ref, w2_ref, w3_ref), wl_ref, wa_ref, wo_ref, sem)

        p_l = _dot(yl_ref[...], wl_ref[...])
        p_a = _dot(ya_ref[...], wa_ref[...])
        pl_ref[...] = p_l.astype(BF16)
        pa_ref[...] = p_a.astype(BF16)
        merged = (_sigmoid(gl_ref[...]) * p_l + _sigmoid(ga_ref[...]) * p_a).astype(BF16)
        mg_ref[...] = merged
        m = _dot(merged, wo_ref[...])
        m_ref[...] = m
        o_ref[...] = h_ref[...] + m * _rsqrt_mean_sq(m) * gp_ref[...]

    row = _ROW(tm)
    return _call(
        body, name=name, grid=(SEQ // tm,),
        in_specs=[row, row, row, row, ANY, ANY, ANY, _VEC, row],
        out_specs=[row] * 5,
        out_shape=[jax.ShapeDtypeStruct((SEQ, D_MODEL), BF16)] * 3 + [jax.ShapeDtypeStruct((SEQ, D_MODEL), F32)] * 2,
        scratch_shapes=_proj_scratch(),
        args=[y_lru, y_attn, g_lru, g_attn, *projs, g_post, h_in], stages=stages)


def merge_bwd(d_out, m, g_post, projs, g_lru, g_attn, p_l, p_a, name, stages=()):
    tm = 256

    def body(do_ref, m_ref, gp_ref, w1_ref, w2_ref, w3_ref, gl_ref, ga_ref, pl_ref, pa_ref,
             dm_ref, dpl_ref, dpa_ref, dgl_ref, dga_ref, dya_ref, dyl_ref, dgp_ref, wl_ref, wa_ref, wo_ref, sem):
        @pl.when(pl.program_id(0) == 0)
        def _():
            _load_projs((w1_ref, w2_ref, w3_ref), wl_ref, wa_ref, wo_ref, sem)
            dgp_ref[...] = jnp.zeros_like(dgp_ref)

        mv = m_ref[...]
        rm = _rsqrt_mean_sq(mv)
        mh = mv * rm
        dn = do_ref[...]
        dgp_ref[...] += jnp.sum(dn * mh, axis=0, keepdims=True)
        t = dn * gp_ref[...]
        dm = (rm * (t - mh * jnp.mean(t * mh, axis=-1, keepdims=True))).astype(BF16)
        dm_ref[...] = dm
        dmg = _dot_nt(dm, wo_ref[...])
        sl = _sigmoid(gl_ref[...])
        sa = _sigmoid(ga_ref[...])
        dpl = (dmg * sl).astype(BF16)
        dpa = (dmg * sa).astype(BF16)
        dpl_ref[...] = dpl
        dpa_ref[...] = dpa
        dgl_ref[...] = (dmg * pl_ref[...].astype(F32) * sl * (1.0 - sl)).astype(BF16)
        dga_ref[...] = (dmg * pa_ref[...].astype(F32) * sa * (1.0 - sa)).astype(BF16)
        dyl_ref[...] = _dot_nt(dpl, wl_ref[...])
        dya_ref[...] = _dot_nt(dpa, wa_ref[...]).astype(BF16)

    row = _ROW(tm)
    return _call(
        body, name=name, grid=(SEQ // tm,),
        in_specs=[row, row, _VEC, ANY, ANY, ANY, row, row, row, row],
        out_specs=[row] * 7 + [_VEC],
        out_shape=[jax.ShapeDtypeStruct((SEQ, D_MODEL), BF16)] * 6 + [jax.ShapeDtypeStruct((SEQ, D_MODEL), F32),
                                                                       jax.ShapeDtypeStruct((1, D_MODEL), F32)],
        scratch_shapes=_proj_scratch(),
        args=[d_out, m, g_post, *projs, g_lru, g_attn, p_l, p_a], stages=stages)


def _rope_tables():
    half = HEAD_DIM // 2
    inv_freq = ROPE_THETA ** (-jnp.arange(half, dtype=F32) / half)
    ang = jnp.arange(SEQ, dtype=F32)[:, None] * inv_freq[None, :]
    cos, sin = jnp.cos(ang), jnp.sin(ang)
    return jnp.tile(jnp.concatenate([cos, cos], axis=1), (1, 2)), jnp.tile(jnp.concatenate([-sin, sin], axis=1), (1, 2))


def _block_diag(w):
    per = LRU_TC // LRU_BLOCK_W
    w4 = w.reshape(LRU_W // LRU_TC, per, LRU_BLOCK_W, LRU_BLOCK_W)
    eye = jnp.eye(per, dtype=w.dtype)
    return jnp.einsum('jacd,ab->jacbd', w4, eye).reshape(LRU_W // LRU_TC, LRU_TC, LRU_TC).astype(BF16)


def _diag_blocks(p):
    per = LRU_TC // LRU_BLOCK_W
    p5 = p.reshape(LRU_W // LRU_TC, per, LRU_BLOCK_W, per, LRU_BLOCK_W)
    return jnp.stack([p5[:, a, :, a, :] for a in range(per)], axis=1).reshape(LRU_W // LRU_BLOCK_W, LRU_BLOCK_W, LRU_BLOCK_W)


def _place():
    x, y, c = lax.axis_index('x'), lax.axis_index('y'), lax.axis_index('c')
    chips = [(1 - x, y), (x, 1 - y), (1 - x, 1 - y)]
    return x, y, c, chips


def _rcopy(src, dst, send_sem, recv_sem, to):
    return pltpu.make_async_remote_copy(src_ref=src, dst_ref=dst, send_sem=send_sem, recv_sem=recv_sem,
                                        device_id=to, device_id_type=MESH)


class _Stage:
    inputs, out_shape, scratch = (), (), ()

    def start(self, ins, outs, scr):
        plan = self._plan(ins, outs, scr)
        for ld in plan['loads']:
            ld.start()
        for cp in plan['sends']:
            cp.start()

    def mid(self, ins, outs, scr):
        plan = self._plan(ins, outs, scr)
        for ld, st in zip(plan['loads'], plan['stores']):
            ld.wait()
            st.start()
        for arrived, onward in zip(plan['arrivals'], plan['forwards']):
            arrived.wait_recv()
            onward.start()

    def end(self, ins, outs, scr):
        plan = self._plan(ins, outs, scr)
        for st in plan['stores']:
            st.wait()
        for arrived in (plan['final_arrivals'] if plan['forwards'] else plan['arrivals']):
            arrived.wait_recv()
        for cp in plan['sends'] + plan['forwards']:
            cp.wait_send()


def _empty_plan():
    return dict(loads=[], stores=[], sends=[], arrivals=[], forwards=[], final_arrivals=[])


class GatherStage(_Stage):
    SUB = 2

    def __init__(self, items):
        self.ranges = [(off, rows) for _, off, rows in items]
        self.inputs = [src for src, _, _ in items]
        self.out_shape = [jax.ShapeDtypeStruct((N_CHIPS, rows, D_MODEL), BF16) for _, rows in self.ranges]
        self.n_ici = 3 * self.SUB * len(items)
        self.scratch = [pltpu.VMEM((sum(r for _, r in self.ranges), D_MODEL), BF16), pltpu.SemaphoreType.DMA((2 * self.n_ici,)),
                        pltpu.SemaphoreType.DMA((2 * self.n_ici,)), pltpu.SemaphoreType.DMA((2 * len(items),))]

    def _plan(self, ins, outs, scr):
        buf, send, recv, lsem = scr
        x, y, c, chips = _place()
        me_q = 2 * x + y
        sib = (x, y, 1 - c)
        plan = _empty_plan()
        boff = 0
        for w, ((off, rows), p_ref, o_ref) in enumerate(zip(self.ranges, ins, outs)):
            hr = rows // 2
            ch = hr // self.SUB
            plan['loads'].append(pltpu.make_async_copy(p_ref.at[pl.ds(off, rows)], buf.at[pl.ds(boff, rows)], lsem.at[2 * w]))
            plan['stores'].append(pltpu.make_async_copy(buf.at[pl.ds(boff, rows)], o_ref.at[me_q], lsem.at[2 * w + 1]))
            boff += rows
            for k in range(self.SUB):
                mine = pl.ds(pl.multiple_of(c * hr + k * ch, 16), ch)
                theirs = pl.ds(pl.multiple_of((1 - c) * hr + k * ch, 16), ch)
                src = p_ref.at[pl.ds(pl.multiple_of(off + c * hr + k * ch, 16), ch)]
                for j, (cx, cy) in enumerate(chips):
                    i = (w * self.SUB + k) * 3 + j
                    got = o_ref.at[2 * cx + cy, mine]
                    got_sib = o_ref.at[2 * cx + cy, theirs]
                    plan['sends'].append(_rcopy(src, o_ref.at[me_q, mine], send.at[i], recv.at[i], (cx, cy, c)))
                    plan['arrivals'].append(_rcopy(got, got, send.at[i], recv.at[i], (cx, cy, c)))
                    plan['forwards'].append(_rcopy(got, got, send.at[self.n_ici + i], recv.at[self.n_ici + i], sib))
                    plan['final_arrivals'].append(
                        _rcopy(got_sib, got_sib, send.at[self.n_ici + i], recv.at[self.n_ici + i], sib))
        return plan


class PairStage(_Stage):
    def __init__(self, grads):
        self.inputs = list(grads)
        self.out_shape = [jax.ShapeDtypeStruct((N_CHIPS, 1) + g.shape[2:], BF16) for g in grads]
        n_cp = N_CHIPS * len(grads)
        self.scratch = [pltpu.SemaphoreType.DMA((n_cp,)), pltpu.SemaphoreType.DMA((n_cp,))]

    def _plan(self, ins, outs, scr):
        send, recv = scr
        x, y, c, _ = _place()
        plan = _empty_plan()
        for w, (g_ref, l_ref) in enumerate(zip(ins, outs)):
            for q in range(N_CHIPS):
                i = w * N_CHIPS + q
                plan['sends'].append(_rcopy(g_ref.at[q, pl.ds(1 - c, 1)], l_ref.at[q], send.at[i], recv.at[i], (x, y, 1 - c)))
        plan['arrivals'] = plan['sends']
        return plan


class ChipStage(_Stage):
    def __init__(self, items):
        self.ranges = [(off, n) for _, off, n in items]
        self.inputs = [s for s, _, _ in items]
        self.out_shape = [jax.ShapeDtypeStruct((N_CHIPS, n, D_MODEL), BF16) for _, n in self.ranges]
        n_cp = 3 * len(items)
        self.scratch = [pltpu.VMEM((sum(n for _, n in self.ranges), D_MODEL), BF16), pltpu.SemaphoreType.DMA((n_cp,)),
                        pltpu.SemaphoreType.DMA((n_cp,)), pltpu.SemaphoreType.DMA((2 * len(items),))]

    def _plan(self, ins, outs, scr):
        buf, send, recv, lsem = scr
        x, y, c, chips = _place()
        me_q = 2 * x + y
        plan = _empty_plan()
        boff = 0
        for w, ((off, n), s_ref, l_ref) in enumerate(zip(self.ranges, ins, outs)):
            rows = pl.ds(off, n)
            plan['loads'].append(pltpu.make_async_copy(s_ref.at[me_q, rows], buf.at[pl.ds(boff, n)], lsem.at[2 * w]))
            plan['stores'].append(pltpu.make_async_copy(buf.at[pl.ds(boff, n)], l_ref.at[me_q], lsem.at[2 * w + 1]))
            boff += n
            for j, (cx, cy) in enumerate(chips):
                i = w * 3 + j
                got = l_ref.at[2 * cx + cy]
                plan['sends'].append(_rcopy(s_ref.at[2 * cx + cy, rows], l_ref.at[me_q], send.at[i], recv.at[i], (cx, cy, c)))
                plan['arrivals'].append(_rcopy(got, got, send.at[i], recv.at[i], (cx, cy, c)))
        return plan


class SwapStage(_Stage):
    def __init__(self, halves):
        self.inputs = list(halves)
        self.out_shape = [jax.ShapeDtypeStruct((2,) + h.shape, F32) for h in halves]
        self.scratch = [pltpu.VMEM((sum(h.shape[0] for h in halves), D_MODEL), F32), pltpu.SemaphoreType.DMA((len(halves),)),
                        pltpu.SemaphoreType.DMA((len(halves),)), pltpu.SemaphoreType.DMA((2 * len(halves),))]

    def _plan(self, ins, outs, scr):
        buf, send, recv, lsem = scr
        x, y, c, _ = _place()
        plan = _empty_plan()
        boff = 0
        for w, (h_ref, o_ref) in enumerate(zip(ins, outs)):
            hr = h_ref.shape[0]
            plan['loads'].append(pltpu.make_async_copy(h_ref, buf.at[pl.ds(boff, hr)], lsem.at[2 * w]))
            plan['stores'].append(pltpu.make_async_copy(buf.at[pl.ds(boff, hr)], o_ref.at[c], lsem.at[2 * w + 1]))
            boff += hr
            got = o_ref.at[1 - c]
            plan['sends'].append(_rcopy(h_ref, o_ref.at[c], send.at[w], recv.at[w], (x, y, 1 - c)))
            plan['arrivals'].append(_rcopy(got, got, send.at[w], recv.at[w], (x, y, 1 - c)))
        return plan


class SmallGatherStage(_Stage):
    def __init__(self, blk):
        self.inputs = [blk]
        self.out_shape = [jax.ShapeDtypeStruct((N_DEV,) + blk.shape, blk.dtype)]
        self.scratch = [pltpu.VMEM(blk.shape, blk.dtype), pltpu.SemaphoreType.DMA((7,)), pltpu.SemaphoreType.DMA((7,)),
                        pltpu.SemaphoreType.DMA((2,))]

    def _plan(self, ins, outs, scr):
        (x_ref,), (o_ref,), (buf, send, recv, lsem) = ins, outs, scr
        x, y, c, chips = _place()
        sib = (x, y, 1 - c)

        def slot(px, py, pc):
            return o_ref.at[4 * px + 2 * py + pc]

        plan = _empty_plan()
        plan['loads'].append(pltpu.make_async_copy(x_ref, buf, lsem.at[0]))
        plan['stores'].append(pltpu.make_async_copy(buf, slot(x, y, c), lsem.at[1]))
        from_sib = slot(x, y, 1 - c)
        plan['sends'].append(_rcopy(x_ref, slot(x, y, c), send.at[0], recv.at[0], sib))
        plan['final_arrivals'].append(_rcopy(from_sib, from_sib, send.at[0], recv.at[0], sib))
        for j, (cx, cy) in enumerate(chips):
            got, got_sib = slot(cx, cy, c), slot(cx, cy, 1 - c)
            plan['sends'].append(_rcopy(x_ref, slot(x, y, c), send.at[1 + j], recv.at[1 + j], (cx, cy, c)))
            plan['arrivals'].append(_rcopy(got, got, send.at[1 + j], recv.at[1 + j], (cx, cy, c)))
            plan['forwards'].append(_rcopy(got, got, send.at[4 + j], recv.at[4 + j], sib))
            plan['final_arrivals'].append(_rcopy(got_sib, got_sib, send.at[4 + j], recv.at[4 + j], sib))
        return plan


def comm_call(name, stages):
    def body():
        pass

    return _call(body, name=name, grid=(1,), in_specs=[], out_specs=[], out_shape=[], args=[], stages=stages)[1]


def pair_sum(g4, land, c_arr, name):
    hr = g4.shape[2]

    def body(c_ref, g_ref, l_ref, o_ref):
        o_ref[0] = (g_ref[0, 0].astype(F32) + l_ref[0, 0].astype(F32)).astype(BF16)

    return pl.pallas_call(
        body, name=name,
        grid_spec=pltpu.PrefetchScalarGridSpec(
            num_scalar_prefetch=1, grid=(N_CHIPS,),
            in_specs=[pl.BlockSpec((1, 1, hr, D_MODEL), lambda q, c: (q, c[0], 0, 0)),
                      pl.BlockSpec((1, 1, hr, D_MODEL), lambda q, c: (q, 0, 0, 0))],
            out_specs=pl.BlockSpec((1, hr, D_MODEL), lambda q, c: (q, 0, 0))),
        out_shape=jax.ShapeDtypeStruct((N_CHIPS, hr, D_MODEL), BF16),
        compiler_params=_params(1),
    )(c_arr, g4, land)


def chip_sum(land, name):
    hr = land.shape[1]
    tr = hr if hr <= 352 else hr // 2

    def body(l_ref, o_ref):
        acc = l_ref[0].astype(F32)
        for s in range(1, N_CHIPS):
            acc = acc + l_ref[s].astype(F32)
        o_ref[...] = acc

    return pl.pallas_call(
        body, name=name, grid=(hr // tr,),
        in_specs=[pl.BlockSpec((N_CHIPS, tr, D_MODEL), lambda i: (0, i, 0))],
        out_specs=pl.BlockSpec((tr, D_MODEL), lambda i: (i, 0)),
        out_shape=jax.ShapeDtypeStruct((hr, D_MODEL), F32),
        compiler_params=_params(1),
    )(land)


def small_sum(parts):
    def body(p_ref, o_ref):
        acc = p_ref[0]
        for s in range(1, N_DEV):
            acc = acc + p_ref[s]
        o_ref[...] = acc

    return pl.pallas_call(
        body, name='small_sum', grid=(1,),
        in_specs=[pl.BlockSpec(parts.shape, lambda i: (0, 0, 0))],
        out_specs=pl.BlockSpec(parts.shape[1:], lambda i: (0, 0)),
        out_shape=jax.ShapeDtypeStruct(parts.shape[1:], F32),
        compiler_params=_params(1),
    )(parts)


def _adam_math(w, g, m, v):
    m2 = ADAM_B1 * m + (1.0 - ADAM_B1) * g
    v2 = ADAM_B2 * v + (1.0 - ADAM_B2) * (g * g)
    m_hat = m2 / (1.0 - ADAM_B1 ** ADAM_STEP)
    v_hat = v2 / (1.0 - ADAM_B2 ** ADAM_STEP)
    delta = -ADAM_LR * (m_hat / (jnp.sqrt(v_hat) + ADAM_EPS) + ADAM_WD * w)
    return delta, m2, v2


def _adam_body(n_parts, transposed):
    def body(*refs):
        g_refs = refs[:n_parts]
        w_ref, m_ref, v_ref, go_ref, d_ref, mo_ref, vo_ref = refs[n_parts:]
        if transposed:
            gt = jnp.concatenate([g_ref[h] for h in range(2) for g_ref in g_refs], axis=0)
            g = gt.T
        else:
            g = jnp.concatenate([g_ref[0] for g_ref in g_refs], axis=0) if n_parts > 1 else g_refs[0][0]
        go_ref[...] = g
        d_ref[...], mo_ref[...], vo_ref[...] = _adam_math(w_ref[...], g, m_ref[...], v_ref[...])
    return body


def adam_rows(fulls, name, w, m, v):
    hr = w.shape[0] // 2
    blk = pl.BlockSpec((hr, D_MODEL), lambda h: (h, 0))
    return pl.pallas_call(
        _adam_body(len(fulls), False), name='adam_' + name, grid=(2,),
        in_specs=[pl.BlockSpec((1, f.shape[1], D_MODEL), lambda h: (h, 0, 0)) for f in fulls] + [blk, blk, blk],
        out_specs=[blk] * 4,
        out_shape=[jax.ShapeDtypeStruct(w.shape, F32)] * 4,
        compiler_params=_params(1),
    )(*fulls, w, m, v)


def adam_cols(fulls, name, w, m, v):
    cols = w.shape[1]
    tr = 128
    blk = pl.BlockSpec((tr, cols), lambda i: (i, 0))
    return pl.pallas_call(
        _adam_body(len(fulls), True), name='adam_' + name, grid=(D_MODEL // tr,),
        in_specs=[pl.BlockSpec((2, f.shape[1], tr), lambda i: (0, 0, i)) for f in fulls] + [blk, blk, blk],
        out_specs=[blk] * 4,
        out_shape=[jax.ShapeDtypeStruct(w.shape, F32)] * 4,
        compiler_params=_params(1),
    )(*fulls, w, m, v)


def adam_small(g, w, m, v):
    def body(g_ref, w_ref, m_ref, v_ref, d_ref, mo_ref, vo_ref):
        d_ref[...], mo_ref[...], vo_ref[...] = _adam_math(w_ref[...], g_ref[...], m_ref[...], v_ref[...])

    blk = pl.BlockSpec(w.shape, lambda i: (0, 0))
    return pl.pallas_call(
        body, name='adam_small', grid=(1,), in_specs=[blk] * 4, out_specs=[blk] * 3,
        out_shape=[jax.ShapeDtypeStruct(w.shape, F32)] * 3, compiler_params=_params(1),
    )(g, w, m, v)


WEIGHTS = ('ffn1_pre_g', 'ffn1_w_gu', 'ffn1_w_down', 'ffn1_post_g', 'mix_pre_g', 'w_in', 'conv_w', 'conv_b',
           'lru_w_a', 'lru_b_a', 'lru_w_x', 'lru_b_x', 'lru_lambda', 'attn_sinks', 'w_proj_lru', 'w_proj_attn',
           'w_out', 'mix_post_g', 'ffn2_pre_g', 'ffn2_w_gu', 'ffn2_w_down', 'ffn2_post_g')
SMALL = tuple(n for n in WEIGHTS if n not in PACK_OFF)


def _pack_small(d, conv_rows):
    sinks = jnp.pad(d['attn_sinks'].reshape(1, N_Q_HEADS), ((0, 0), (0, D_MODEL - N_Q_HEADS)))
    conv = jnp.pad(conv_rows, ((0, ROW_WA - ROW_CONV - conv_rows.shape[0]), (0, 0)))
    return jnp.concatenate([d[n].reshape(1, D_MODEL) for n in SMALL_VECS] + [sinks, conv]
                           + [d['lru_w_a'].reshape(64, D_MODEL), d['lru_w_x'].reshape(64, D_MODEL)], axis=0)


def _unpack_small(p, shapes):
    out = {n: p[k:k + 1].reshape(shapes[n]) for k, n in enumerate(SMALL_VECS)}
    out['attn_sinks'] = p[ROW_SINKS:ROW_SINKS + 1, :N_Q_HEADS].reshape(shapes['attn_sinks'])
    out['conv_w'] = p[ROW_CONV:ROW_CONV + 1].reshape(shapes['conv_w'])
    out['lru_w_a'] = p[ROW_WA:ROW_WA + 64].reshape(shapes['lru_w_a'])
    out['lru_w_x'] = p[ROW_WX:ROW_WX + 64].reshape(shapes['lru_w_x'])
    return out


def kernel(x, ffn1_pre_g, ffn1_w_gu, ffn1_w_down, ffn1_post_g, mix_pre_g, w_in, conv_w, conv_b, lru_w_a, lru_b_a, lru_w_x, lru_b_x, lru_lambda, attn_sinks, w_proj_lru, w_proj_attn, w_out, mix_post_g, ffn2_pre_g, ffn2_w_gu, ffn2_w_down, ffn2_post_g, loss_target, m_ffn1_pre_g, m_ffn1_w_gu, m_ffn1_w_down, m_ffn1_post_g, m_mix_pre_g, m_w_in, m_conv_w, m_conv_b, m_lru_w_a, m_lru_b_a, m_lru_w_x, m_lru_b_x, m_lru_lambda, m_attn_sinks, m_w_proj_lru, m_w_proj_attn, m_w_out, m_mix_post_g, m_ffn2_pre_g, m_ffn2_w_gu, m_ffn2_w_down, m_ffn2_post_g, v_ffn1_pre_g, v_ffn1_w_gu, v_ffn1_w_down, v_ffn1_post_g, v_mix_pre_g, v_w_in, v_conv_w, v_conv_b, v_lru_w_a, v_lru_b_a, v_lru_w_x, v_lru_b_x, v_lru_lambda, v_attn_sinks, v_w_proj_lru, v_w_proj_attn, v_w_out, v_mix_post_g, v_ffn2_pre_g, v_ffn2_w_gu, v_ffn2_w_down, v_ffn2_post_g):
    given = dict(locals())
    w = {n: given[n] for n in WEIGHTS}
    mom = {n: given['m_' + n] for n in WEIGHTS}
    var = {n: given['v_' + n] for n in WEIGHTS}
    shapes = {n: w[n].shape for n in WEIGHTS}
    xq = lax.axis_index('x')
    yq = lax.axis_index('y')
    cq = lax.axis_index('c')
    me_q = 2 * xq + yq

    c_arr = cq.reshape(1).astype(jnp.int32)
    xs, target = x[0], loss_target[0]
    sw = {n: (w[n][0] if w[n].ndim > 2 else w[n]) for n in SMALL}
    cos, sin_signed = _rope_tables()
    wa_bd = _block_diag(sw['lru_w_a'])
    wx_bd = _block_diag(sw['lru_w_x'])
    sinks = sw['attn_sinks'].reshape(N_Q_HEADS)

    shard = {n: (w[n][0].T if t else w[n][0]).astype(BF16) for n, _, t in PACK}
    conv_pad = jnp.pad(w['conv_w'][0], ((0, 4), (0, 0)))

    def whole(name):
        return (shard[name], 0, PACK_ROWS_OF[name])

    def part(name, p, n_parts=2):
        rows = PACK_ROWS_OF[name] // n_parts
        return (shard[name], p * rows, rows)

    (w_gu1,), (conv_all,) = comm_call('gather_first', [GatherStage([whole('ffn1_w_gu')]), SmallGatherStage(conv_pad)])
    sw['conv_w'] = jnp.transpose(conv_all[0::2, :4, :], (1, 0, 2)).reshape(4, LRU_W)
    proj_names = ['w_proj_lru', 'w_proj_attn', 'w_out']

    (n1, g1, u1, a1), ((w_down1,),) = ffn_fwd_a(xs, sw['ffn1_pre_g'], [w_gu1], 'ffn1_fwd_a',
                                                 stages=[GatherStage([whole('ffn1_w_down')])])
    (f1, h1), ((w_in_t,),) = ffn_fwd_b(a1, w_down1, sw['ffn1_post_g'], xs, 'ffn1_fwd_b', stages=[GatherStage([whole('w_in')])])
    (um, gate, xbr, q, k, v, g_lru, g_attn), ((w_gu2a,),) = mix_in(h1, sw['mix_pre_g'], w_in_t, 'mix_in',
                                                                   stages=[GatherStage([part('ffn2_w_gu', 0)])])
    (y_lru, h_lru), ((w_gu2b,),) = lru_fwd(gate, xbr, sw['conv_w'], sw['conv_b'], wa_bd, sw['lru_b_a'], wx_bd, sw['lru_b_x'],
                                           sw['lru_lambda'], 'lru_fwd', stages=[GatherStage([part('ffn2_w_gu', 1)])])
    (qr, kr, y_attn), (projs,) = attn_fwd(q, k, v, cos, sin_signed, sinks, 'attn_fwd',
                                          stages=[GatherStage([whole(n) for n in proj_names])])
    (p_l, p_a, merged, m, h2), ((w_down2,),) = merge_fwd(y_lru, y_attn, g_lru, g_attn, projs, sw['mix_post_g'], h1, 'merge_fwd',
                                                         stages=[GatherStage([whole('ffn2_w_down')])])
    w_gu2 = [w_gu2a, w_gu2b]
    (n2, g2, u2, a2), _ = ffn_fwd_a(h2, sw['ffn2_pre_g'], w_gu2, 'ffn2_fwd_a')
    (f2, dy, loss_blk), _ = ffn_fwd_b(a2, w_down2, sw['ffn2_post_g'], h2, 'ffn2_fwd_b', target=target)
    loss = lax.psum(loss_blk[0, 0], ('x', 'y', 'c'))

    gs, full = {}, {}

    def pair_stage(names, grads):
        g4 = [g.reshape(N_CHIPS, 2, PACK_ROWS_OF[n] // 2, D_MODEL) for n, g in zip(names, grads)]
        return PairStage(g4), g4

    def pair_sums(names, g4, lands):
        return [pair_sum(g, l, c_arr, 'pair_sum_' + n) for n, g, l in zip(names, g4, lands)]

    def chip_sums(names, lands):
        return [chip_sum(l, 'chip_sum_' + n) for n, l in zip(names, lands)]

    def halves(s, n_parts=2):
        n = s.shape[1] // n_parts
        return [(s, p * n, n) for p in range(n_parts)]

    (df2, dgu2, gs['ffn2_post_g']), _ = ffn_bwd_a(dy, f2, sw['ffn2_post_g'], w_down2, g2, u2, 'ffn2_bwd_a')
    g_down2, _ = mm_tn([a2], df2, 256, 'ffn2_dw_down')
    st, g4 = pair_stage(['ffn2_w_down'], [g_down2])
    g_gu2, (lands,) = mm_tn([dgu2], n2, 512, 'ffn2_dw_gu', stages=[st])
    (s_down2,) = pair_sums(['ffn2_w_down'], g4, lands)
    st, g4 = pair_stage(['ffn2_w_gu'], [g_gu2])
    (dh2, gs['ffn2_pre_g']), ((l_down2,), lands) = norm_bwd([dgu2], w_gu2, h2, sw['ffn2_pre_g'], dy, 'ffn2_bwd_b',
                                                            stages=[ChipStage([(s_down2, 0, s_down2.shape[1])]), st])
    (s_gu2,) = pair_sums(['ffn2_w_gu'], g4, lands)
    (h_down2,) = chip_sums(['ffn2_w_down'], [l_down2])

    (dm, dpl, dpa, dgl, dga, dya, dyl, gs['mix_post_g']), ((l_gu2a,),) = merge_bwd(
        dh2, m, sw['mix_post_g'], projs, g_lru, g_attn, p_l, p_a, 'merge_bwd', stages=[ChipStage(halves(s_gu2)[:1])])
    g_projs = [mm_tn([merged if n == 'w_out' else (y_lru if n == 'w_proj_lru' else y_attn)],
                     dm if n == 'w_out' else (dpl if n == 'w_proj_lru' else dpa), 512, 'd' + n)[0] for n in proj_names]
    st, g4 = pair_stage(proj_names, g_projs)
    (dq, dk, dv, dsk), ((l_gu2b,), lands, (full['ffn2_w_down'],)) = attn_bwd(
        qr, kr, v, dya, cos, sin_signed, sinks, 'attn_bwd', stages=[ChipStage(halves(s_gu2)[1:]), st, SwapStage([h_down2])])
    full['ffn2_w_down'] = [full['ffn2_w_down']]
    gs['attn_sinks'] = dsk[0:1, 0:N_Q_HEADS]
    s_projs = pair_sums(proj_names, g4, lands)
    h_gu2 = chip_sums(['ffn2_w_gu_a', 'ffn2_w_gu_b'], [l_gu2a, l_gu2b])
    (dgate, dxbr, vecs, dwa, dwx), (l_projs, full['ffn2_w_gu']) = lru_bwd(
        gate, xbr, h_lru, dyl, sw['conv_w'], sw['conv_b'], wa_bd, sw['lru_b_a'], wx_bd, sw['lru_b_x'], sw['lru_lambda'],
        'lru_bwd', stages=[ChipStage([(s, 0, s.shape[1]) for s in s_projs]), SwapStage(h_gu2)])
    gs['conv_w'] = vecs[0:4]
    gs['conv_b'], gs['lru_b_a'], gs['lru_b_x'], gs['lru_lambda'] = vecs[4:5], vecs[5:6], vecs[6:7], vecs[7:8]
    gs['lru_w_a'] = _diag_blocks(dwa)
    gs['lru_w_x'] = _diag_blocks(dwx)
    dz = [dgate, dxbr, dq, dk, dv, dgl, dga]
    g_in, _ = mm_tn(dz, um, 256, 'dw_in')
    h_projs = chip_sums(proj_names, l_projs)
    st, g4 = pair_stage(['w_in'], [g_in])
    (dh1, gs['mix_pre_g']), (lands, f_projs) = norm_bwd(dz, [w_in_t], h1, sw['mix_pre_g'], dh2, 'mix_bwd_in',
                                                        stages=[st, SwapStage(h_projs)])
    for n, f in zip(proj_names, f_projs):
        full[n] = [f]
    (s_in,) = pair_sums(['w_in'], g4, lands)

    (df1, dgu1, gs['ffn1_post_g']), ((l_in_a,),) = ffn_bwd_a(dh1, f1, sw['ffn1_post_g'], w_down1, g1, u1, 'ffn1_bwd_a',
                                                             stages=[ChipStage(halves(s_in)[:1])])
    g_down1, _ = mm_tn([a1], df1, 256, 'ffn1_dw_down')
    st, g4 = pair_stage(['ffn1_w_down'], [g_down1])
    g_gu1, ((l_in_b,), lands) = mm_tn([dgu1], n1, 512, 'ffn1_dw_gu', stages=[ChipStage(halves(s_in)[1:]), st])
    (s_down1,) = pair_sums(['ffn1_w_down'], g4, lands)
    h_in = chip_sums(['w_in_a', 'w_in_b'], [l_in_a, l_in_b])
    st, g4 = pair_stage(['ffn1_w_gu'], [g_gu1])
    (dx, gs['ffn1_pre_g']), ((l_down1,), lands, full['w_in']) = norm_bwd(
        [dgu1], [w_gu1], xs, sw['ffn1_pre_g'], dh1, 'ffn1_bwd_b',
        stages=[ChipStage([(s_down1, 0, s_down1.shape[1])]), st, SwapStage(h_in)])
    (s_gu1,) = pair_sums(['ffn1_w_gu'], g4, lands)
    (h_down1,) = chip_sums(['ffn1_w_down'], [l_down1])
    small_blk = _pack_small(gs, gs['conv_w'])
    (l_gu1,), (small_all,), (f_down1,) = comm_call(
        'reduce_last', [ChipStage([(s_gu1, 0, s_gu1.shape[1])]), SmallGatherStage(small_blk), SwapStage([h_down1])])
    full['ffn1_w_down'] = [f_down1]
    full['ffn1_w_gu'] = comm_call('swap_last', [SwapStage(chip_sums(['ffn1_w_gu'], [l_gu1]))])[0]

    out_g, out_d, out_m, out_v = {}, {}, {}, {}
    for n, _, t in PACK:
        fn = adam_cols if t else adam_rows
        g_, d_, m_, v_ = fn(full[n], n, w[n][0], mom[n][0], var[n][0])
        out_g[n], out_d[n], out_m[n], out_v[n] = g_[None], d_[None], m_[None], v_[None]

    tot = small_sum(small_all)
    conv_g = lax.dynamic_slice(tot[ROW_CONV:ROW_CONV + 4], (0, me_q * (LRU_W // N_CHIPS)), (4, LRU_W // N_CHIPS))
    small_g = _unpack_small(tot, shapes)
    small_g['conv_w'] = conv_g.reshape(shapes['conv_w'])
    g_pack = jnp.concatenate([tot[:ROW_CONV], conv_g.reshape(1, D_MODEL), jnp.zeros((ROW_WA - ROW_CONV - 1, D_MODEL), F32),
                              tot[ROW_WA:]], axis=0)
    packs = [_pack_small({n: d[n] for n in SMALL}, d['conv_w'].reshape(1, D_MODEL)) for d in (w, mom, var)]
    d_p, m_p, v_p = adam_small(g_pack, *packs)
    for n in SMALL:
        out_g[n] = small_g[n]
    for dst, p in ((out_d, d_p), (out_m, m_p), (out_v, v_p)):
        dst.update(_unpack_small(p, shapes))

    return (loss, dx[None], *[out_g[n] for n in WEIGHTS], *[out_d[n] for n in WEIGHTS],
            *[out_m[n] for n in WEIGHTS], *[out_v[n] for n in WEIGHTS])
```

```python
import jax
import jax.numpy as jnp
from jax import lax
from jax.experimental import pallas as pl
from jax.experimental.pallas import tpu as pltpu

F32 = jnp.float32
BF16 = jnp.bfloat16

SEQ = 2048
D_MODEL = 1024
D_FF = 2816
LRU_W = 1024
LRU_BLOCK_W = 64
HEAD_DIM = 64
N_Q_HEADS = 16
N_KV_HEADS = 4
KV_W = N_KV_HEADS * HEAD_DIM
ATTN_BLOCK = 128
N_ATTN_BLOCKS = SEQ // ATTN_BLOCK
IN_SEGS = (1024, 1024, 1024, 256, 256, 1024, 1024)
IN_W = sum(IN_SEGS)
NORM_EPS = 1e-6
MASK_VALUE = -1e30
ROPE_THETA = 10000.0
LRU_C = 8.0
MACARON = 0.5
ADAM_LR = 0.001
ADAM_B1 = 0.9
ADAM_B2 = 0.999
ADAM_EPS = 1e-08
ADAM_WD = 0.01
ADAM_STEP = 10

N_CHIPS = 4
N_DEV = 8
VMEM_LIMIT = 56 * 1024 * 1024
MM_ROWS = 512
MESH = pl.DeviceIdType.MESH
ANY = pl.BlockSpec(memory_space=pl.ANY)

PACK = (('ffn1_w_gu', 1408, True), ('w_in', 1408, True), ('ffn2_w_gu', 1408, True),
        ('ffn1_w_down', 704, False), ('ffn2_w_down', 704, False),
        ('w_proj_lru', 256, False), ('w_proj_attn', 256, False), ('w_out', 256, False))
PACK_ROWS_OF = {n: r for n, r, _ in PACK}
PACK_OFF = {}
_o = 0
for _n, _r, _t in PACK:
    PACK_OFF[_n] = _o
    _o += _r

SMALL_VECS = ('ffn1_pre_g', 'ffn1_post_g', 'mix_pre_g', 'conv_b', 'lru_b_a', 'lru_b_x', 'lru_lambda',
              'mix_post_g', 'ffn2_pre_g', 'ffn2_post_g')
SMALL_ROWS = 144
ROW_SINKS, ROW_CONV, ROW_WA, ROW_WX = 10, 11, 16, 80


def _dot(a, b):
    return jnp.dot(a, b, preferred_element_type=F32)


def _dot_nt(a, b):
    return lax.dot_general(a, b, (((1,), (1,)), ((), ())), preferred_element_type=F32)


def _dot_tn(a, b):
    return lax.dot_general(a, b, (((0,), (0,)), ((), ())), preferred_element_type=F32)


def _params(n_grid):
    return pltpu.CompilerParams(dimension_semantics=("arbitrary",) * n_grid, vmem_limit_bytes=VMEM_LIMIT)


def _sigmoid(x):
    return 1.0 / (1.0 + jnp.exp(-x))


def _rsqrt_mean_sq(x):
    return lax.rsqrt(jnp.mean(x * x, axis=-1, keepdims=True) + NORM_EPS)


def _expm1(x):
    poly = x * (1.0 + x * (0.5 + x * (1.0 / 6.0 + x * (1.0 / 24.0 + x * (1.0 / 120.0)))))
    return jnp.where(jnp.abs(x) < 0.1, poly, jnp.exp(x) - 1.0)


_GELU_K = 0.7978845608028654
_GELU_C = 0.044715


def _gelu(x):
    t = jnp.tanh(_GELU_K * (x + _GELU_C * x * x * x))
    return 0.5 * x * (1.0 + t), t


def _gelu_grad(x, t):
    return 0.5 * (1.0 + t) + 0.5 * x * (1.0 - t * t) * _GELU_K * (1.0 + 3.0 * _GELU_C * x * x)


def _load_weight(w_refs, dst_ref, sem):
    w_refs = list(w_refs) if isinstance(w_refs, (list, tuple)) else [w_refs]
    rows = dst_ref.shape[0] // N_CHIPS
    rp = rows // len(w_refs)
    cps = [pltpu.make_async_copy(w_ref.at[q], dst_ref.at[pl.ds(q * rows + p * rp, rp)], sem.at[p * N_CHIPS + q])
           for p, w_ref in enumerate(w_refs) for q in range(N_CHIPS)]
    for cp in cps:
        cp.start()
    for cp in cps:
        cp.wait()


def _weight_scratch(rows_total, parts=1):
    return [pltpu.VMEM((rows_total, D_MODEL), BF16), pltpu.SemaphoreType.DMA((N_CHIPS * parts,))]


_ROW = lambda tm: pl.BlockSpec((tm, D_MODEL), lambda i: (i, 0))
_VEC = pl.BlockSpec((1, D_MODEL), lambda i: (0, 0))


def _call(body, *, name, grid, in_specs, out_specs, out_shape, args, scratch_shapes=(), stages=()):
    in_specs, out_specs, out_shape, scratch_shapes = list(in_specs), list(out_specs), list(out_shape), list(scratch_shapes)
    n_in, n_out, n_sc = len(in_specs), len(out_specs), len(scratch_shapes)
    k_in = [len(s.inputs) for s in stages]
    k_out = [len(s.out_shape) for s in stages]
    k_sc = [len(s.scratch) for s in stages]
    last = grid[0] - 1

    def split(refs, counts):
        parts, pos = [], 0
        for k in counts:
            parts.append(refs[pos:pos + k])
            pos += k
        return parts

    def full(*refs):
        ins, s_ins, outs, s_outs, scr, s_scr = split(refs, [n_in, sum(k_in), n_out, sum(k_out), n_sc, sum(k_sc)])
        per_stage = list(zip(stages, split(s_ins, k_in), split(s_outs, k_out), split(s_scr, k_sc)))
        i = pl.program_id(0)
        if stages:
            @pl.when(i == 0)
            def _():
                for s, a, b, c in per_stage:
                    s.start(a, b, c)

        body(*ins, *outs, *scr)
        if stages:
            @pl.when(i == max(last - 1, 0))
            def _():
                for s, a, b, c in per_stage:
                    s.mid(a, b, c)

            @pl.when(i == last)
            def _():
                for s, a, b, c in per_stage:
                    s.end(a, b, c)

    res = pl.pallas_call(
        full, name=name, grid=grid,
        in_specs=in_specs + [ANY] * sum(k_in),
        out_specs=out_specs + [ANY] * sum(k_out),
        out_shape=out_shape + [o for s in stages for o in s.out_shape],
        scratch_shapes=scratch_shapes + [x for s in stages for x in s.scratch],
        compiler_params=_params(1),
    )(*args, *[a for s in stages for a in s.inputs])
    return list(res[:n_out]), split(list(res[n_out:]), k_out)


def ffn_fwd_a(x, g_pre, w_gu_t, name, stages=()):
    tm, tn = MM_ROWS, 256
    n_w = len(w_gu_t)

    def body(x_ref, gp_ref, *refs):
        w_refs = refs[:n_w]
        n_ref, g_ref, u_ref, a_ref, wt_ref, sem = refs[n_w:]

        @pl.when(pl.program_id(0) == 0)
        def _():
            _load_weight(w_refs, wt_ref, sem)

        xv = x_ref[...]
        n = (xv * _rsqrt_mean_sq(xv) * gp_ref[...]).astype(BF16)
        n_ref[...] = n
        for j in range(D_FF // tn):
            g = _dot_nt(n, wt_ref[j * tn:(j + 1) * tn, :])
            u = _dot_nt(n, wt_ref[D_FF + j * tn:D_FF + (j + 1) * tn, :])
            g_ref[:, j * tn:(j + 1) * tn] = g.astype(BF16)
            u_ref[:, j * tn:(j + 1) * tn] = u.astype(BF16)
            a_ref[:, j * tn:(j + 1) * tn] = (g * _sigmoid(g) * u).astype(BF16)

    wide = pl.BlockSpec((tm, D_FF), lambda i: (i, 0))
    return _call(
        body, name=name, grid=(SEQ // tm,),
        in_specs=[_ROW(tm), _VEC] + [ANY] * n_w,
        out_specs=[_ROW(tm), wide, wide, wide],
        out_shape=[jax.ShapeDtypeStruct((SEQ, D_MODEL), BF16)] + [jax.ShapeDtypeStruct((SEQ, D_FF), BF16)] * 3,
        scratch_shapes=_weight_scratch(2 * D_FF, n_w),
        args=[x, g_pre, *w_gu_t], stages=stages)


def ffn_fwd_b(a, w_down, g_post, h_in, name, target=None, stages=()):
    tm = MM_ROWS
    final = target is not None

    def body(*refs):
        if final:
            a_ref, wf_ref, gp_ref, h_ref, t_ref, f_ref, o_ref, loss_ref, wd_ref, sem = refs
        else:
            a_ref, wf_ref, gp_ref, h_ref, f_ref, o_ref, wd_ref, sem = refs

        @pl.when(pl.program_id(0) == 0)
        def _():
            _load_weight(wf_ref, wd_ref, sem)
            if final:
                loss_ref[...] = jnp.zeros_like(loss_ref)

        f = _dot(a_ref[...], wd_ref[...])
        f_ref[...] = f
        y = h_ref[...] + MACARON * (f * _rsqrt_mean_sq(f) * gp_ref[...])
        if final:
            err = y - t_ref[...]
            o_ref[...] = err * (1.0 / D_MODEL)
            loss_ref[...] += 0.5 * jnp.sum(err * err) * (1.0 / D_MODEL)
        else:
            o_ref[...] = y

    row = _ROW(tm)
    in_specs = [pl.BlockSpec((tm, D_FF), lambda i: (i, 0)), ANY, _VEC, row]
    out_specs = [row, row]
    out_shape = [jax.ShapeDtypeStruct((SEQ, D_MODEL), F32)] * 2
    args = [a, w_down, g_post, h_in]
    if final:
        in_specs.append(row)
        args.append(target)
        out_specs.append(pl.BlockSpec((8, 128), lambda i: (0, 0)))
        out_shape.append(jax.ShapeDtypeStruct((8, 128), F32))
    return _call(body, name=name, grid=(SEQ // tm,), in_specs=in_specs, out_specs=out_specs,
                 out_shape=out_shape, scratch_shapes=_weight_scratch(D_FF), args=args, stages=stages)


def ffn_bwd_a(d_out, f, g_post, w_down, g, u, name, stages=()):
    tm = MM_ROWS
    tc = 256

    def body(do_ref, f_ref, gp_ref, wf_ref, g_ref, u_ref, df_ref, dgu_ref, dgp_ref, wd_ref, sem):
        @pl.when(pl.program_id(0) == 0)
        def _():
            _load_weight(wf_ref, wd_ref, sem)
            dgp_ref[...] = jnp.zeros_like(dgp_ref)

        fv = f_ref[...]
        rf = _rsqrt_mean_sq(fv)
        fh = fv * rf
        dn = MACARON * do_ref[...]
        dgp_ref[...] += jnp.sum(dn * fh, axis=0, keepdims=True)
        t = dn * gp_ref[...]
        df = (rf * (t - fh * jnp.mean(t * fh, axis=-1, keepdims=True))).astype(BF16)
        df_ref[...] = df
        for c0 in range(0, D_FF, tc):
            da = _dot_nt(df, wd_ref[c0:c0 + tc, :])
            gv = g_ref[:, c0:c0 + tc].astype(F32)
            uv = u_ref[:, c0:c0 + tc].astype(F32)
            s = _sigmoid(gv)
            dgu_ref[:, c0:c0 + tc] = (da * uv * s * (1.0 + gv * (1.0 - s))).astype(BF16)
            dgu_ref[:, D_FF + c0:D_FF + c0 + tc] = (da * gv * s).astype(BF16)

    row = _ROW(tm)
    wide = pl.BlockSpec((tm, D_FF), lambda i: (i, 0))
    return _call(
        body, name=name, grid=(SEQ // tm,),
        in_specs=[row, row, _VEC, ANY, wide, wide],
        out_specs=[row, pl.BlockSpec((tm, 2 * D_FF), lambda i: (i, 0)), _VEC],
        out_shape=[jax.ShapeDtypeStruct((SEQ, D_MODEL), BF16), jax.ShapeDtypeStruct((SEQ, 2 * D_FF), BF16),
                   jax.ShapeDtypeStruct((1, D_MODEL), F32)],
        scratch_shapes=_weight_scratch(D_FF),
        args=[d_out, f, g_post, w_down, g, u], stages=stages)


def norm_bwd(pieces, w_t, x, g_pre, d_res, name, stages=()):
    tm = MM_ROWS
    widths = [p.shape[1] for p in pieces]
    offs = [sum(widths[:k]) for k in range(len(widths))]
    n_p = len(pieces)
    n_w = len(w_t)

    def body(*refs):
        p_refs = refs[:n_p]
        w_refs = refs[n_p:n_p + n_w]
        x_ref, g_ref, r_ref, dx_ref, dg_ref, wt_ref, sem = refs[n_p + n_w:]

        @pl.when(pl.program_id(0) == 0)
        def _():
            _load_weight(w_refs, wt_ref, sem)
            dg_ref[...] = jnp.zeros_like(dg_ref)

        dn = None
        for p_ref, lo, wd in zip(p_refs, offs, widths):
            part = _dot(p_ref[...], wt_ref[lo:lo + wd, :])
            dn = part if dn is None else dn + part
        xv = x_ref[...]
        r = _rsqrt_mean_sq(xv)
        xh = xv * r
        dg_ref[...] += jnp.sum(dn * xh, axis=0, keepdims=True)
        t = dn * g_ref[...]
        dx_ref[...] = r_ref[...] + r * (t - xh * jnp.mean(t * xh, axis=-1, keepdims=True))

    row = _ROW(tm)
    return _call(
        body, name=name, grid=(SEQ // tm,),
        in_specs=[pl.BlockSpec((tm, wd), lambda i: (i, 0)) for wd in widths] + [ANY] * n_w + [row, _VEC, row],
        out_specs=[row, _VEC],
        out_shape=[jax.ShapeDtypeStruct((SEQ, D_MODEL), F32), jax.ShapeDtypeStruct((1, D_MODEL), F32)],
        scratch_shapes=_weight_scratch(sum(widths), n_w),
        args=[*pieces, *w_t, x, g_pre, d_res], stages=stages)


def mm_tn(pieces, b, tm, name, stages=()):
    widths = [p.shape[1] for p in pieces]
    m_total = sum(widths)
    n_p = len(pieces)
    starts = [sum(widths[:k]) // tm for k in range(n_p)]
    counts = [wd // tm for wd in widths]

    def body(*refs):
        p_refs = refs[:n_p]
        b_ref, o_ref = refs[n_p:]
        i = pl.program_id(0)
        for p_ref, st, ct in zip(p_refs, starts, counts):
            @pl.when((i >= st) & (i < st + ct))
            def _(p_ref=p_ref):
                o_ref[...] = _dot_tn(p_ref[...], b_ref[...]).astype(BF16)

    def piece_spec(st, ct):
        return pl.BlockSpec((SEQ, tm), lambda i: (0, jnp.clip(i - st, 0, ct - 1)))

    (out,), stage_out = _call(
        body, name=name, grid=(m_total // tm,),
        in_specs=[piece_spec(st, ct) for st, ct in zip(starts, counts)] + [pl.BlockSpec((SEQ, D_MODEL), lambda i: (0, 0))],
        out_specs=[pl.BlockSpec((tm, D_MODEL), lambda i: (i, 0))],
        out_shape=[jax.ShapeDtypeStruct((m_total, D_MODEL), BF16)],
        args=[*pieces, b], stages=stages)
    return out, stage_out


def mix_in(h, g_pre, w_in_t, name, stages=()):
    tm = MM_ROWS
    offs = [sum(IN_SEGS[:k]) for k in range(len(IN_SEGS))]
    dts = [F32, F32, F32, F32, BF16, F32, F32]
    n_o = len(IN_SEGS)

    def body(*refs):
        h_ref, g_ref, wf_ref, um_ref = refs[:4]
        o_refs = refs[4:4 + n_o]
        wt_ref, sem = refs[4 + n_o:]

        @pl.when(pl.program_id(0) == 0)
        def _():
            _load_weight(wf_ref, wt_ref, sem)

        hv = h_ref[...]
        um = (hv * _rsqrt_mean_sq(hv) * g_ref[...]).astype(BF16)
        um_ref[...] = um
        for o_ref, lo, wd in zip(o_refs, offs, IN_SEGS):
            for c0 in range(0, wd, 256):
                o_ref[:, c0:c0 + 256] = _dot_nt(um, wt_ref[lo + c0:lo + c0 + 256, :]).astype(o_ref.dtype)

    return _call(
        body, name=name, grid=(SEQ // tm,),
        in_specs=[_ROW(tm), _VEC, ANY],
        out_specs=[_ROW(tm)] + [pl.BlockSpec((tm, wd), lambda i: (i, 0)) for wd in IN_SEGS],
        out_shape=[jax.ShapeDtypeStruct((SEQ, D_MODEL), BF16)]
        + [jax.ShapeDtypeStruct((SEQ, wd), dt) for wd, dt in zip(IN_SEGS, dts)],
        scratch_shapes=_weight_scratch(IN_W),
        args=[h, g_pre, w_in_t], stages=stages)


LRU_TC = 256


def _conv_fwd(xb, cw, cb, tt):
    xc = xb * cw[3:4, :] + cb
    shifted = []
    for s in (1, 2, 3):
        sh = jnp.where(tt >= s, pltpu.roll(xb, s, 0), 0.0)
        shifted.append(sh)
        xc = xc + sh * cw[3 - s:4 - s, :]
    return xc, shifted


def _lru_gates(xc, wa, ba, wx, bx, lam):
    xcb = xc.astype(BF16)
    r = _sigmoid(_dot(xcb, wa) + ba)
    i = _sigmoid(_dot(xcb, wx) + bx)
    nl = -lam
    sp = jnp.maximum(nl, 0.0) + jnp.log1p(jnp.exp(-jnp.abs(nl)))
    la = (-LRU_C * r) * sp
    a = jnp.exp(la)
    mult = jnp.sqrt(jnp.maximum(-_expm1(2.0 * la), 0.0))
    return xcb, r, i, sp, a, mult


def _scan(a, b, tt, reverse):
    n = a.shape[0]
    s = 1
    while s < n:
        more = 2 * s < n
        if s < 8:
            if reverse:
                keep = tt < n - s
                shift = n - s
            else:
                keep = tt >= s
                shift = s
            b = a * jnp.where(keep, pltpu.roll(b, shift, 0), 0.0) + b
            if more:
                a = a * jnp.where(keep, pltpu.roll(a, shift, 0), 1.0)
        elif reverse:
            b = jnp.concatenate([a[:n - s] * b[s:] + b[:n - s], b[n - s:]], axis=0)
            if more:
                a = jnp.concatenate([a[:n - s] * a[s:], a[n - s:]], axis=0)
        else:
            b = jnp.concatenate([b[:s], a[s:] * b[:n - s] + b[s:]], axis=0)
            if more:
                a = jnp.concatenate([a[:s], a[s:] * a[:n - s]], axis=0)
        s *= 2
    return b


def _lru_specs():
    col = pl.BlockSpec((SEQ, LRU_TC), lambda j: (0, j))
    vec = pl.BlockSpec((1, LRU_TC), lambda j: (0, j))
    bd = pl.BlockSpec((1, LRU_TC, LRU_TC), lambda j: (j, 0, 0))
    cw = pl.BlockSpec((4, LRU_TC), lambda j: (0, j))
    return col, vec, bd, cw


def lru_fwd(gate, xbr, conv_w, conv_b, wa_bd, b_a, wx_bd, b_x, lam, name, stages=()):
    col, vec, bd, cw = _lru_specs()

    def body(gate_ref, xbr_ref, cw_ref, cb_ref, wa_ref, ba_ref, wx_ref, bx_ref, lam_ref, y_ref, h_ref):
        tt = lax.broadcasted_iota(jnp.int32, (SEQ, LRU_TC), 0)
        xc, _ = _conv_fwd(xbr_ref[...], cw_ref[...], cb_ref[...], tt)
        _, r, i, sp, a, mult = _lru_gates(xc, wa_ref[0], ba_ref[...], wx_ref[0], bx_ref[...], lam_ref[...])
        h = _scan(a, mult * (i * xc), tt, reverse=False)
        h_ref[...] = h
        gl, _ = _gelu(gate_ref[...])
        y_ref[...] = (h * gl).astype(BF16)

    return _call(
        body, name=name, grid=(LRU_W // LRU_TC,),
        in_specs=[col, col, cw, vec, bd, vec, bd, vec, vec],
        out_specs=[col, col],
        out_shape=[jax.ShapeDtypeStruct((SEQ, LRU_W), BF16), jax.ShapeDtypeStruct((SEQ, LRU_W), F32)],
        args=[gate, xbr, conv_w, conv_b, wa_bd, b_a, wx_bd, b_x, lam], stages=stages)


def lru_bwd(gate, xbr, h, dy, conv_w, conv_b, wa_bd, b_a, wx_bd, b_x, lam, name, stages=()):
    col, vec, bd, cw = _lru_specs()

    def body(gate_ref, xbr_ref, h_ref, dy_ref, cw_ref, cb_ref, wa_ref, ba_ref, wx_ref, bx_ref, lam_ref,
             dgate_ref, dxbr_ref, vecs_ref, dwa_ref, dwx_ref):
        tt = lax.broadcasted_iota(jnp.int32, (SEQ, LRU_TC), 0)
        cwv = cw_ref[...]
        lam = lam_ref[...]
        xb = xbr_ref[...]
        xc, shifted = _conv_fwd(xb, cwv, cb_ref[...], tt)
        wa = wa_ref[0]
        wx = wx_ref[0]
        xcb, r, i, sp, a, mult = _lru_gates(xc, wa, ba_ref[...], wx, bx_ref[...], lam)
        hv = h_ref[...]
        dyv = dy_ref[...]
        gv = gate_ref[...]
        gl, th = _gelu(gv)
        dgate_ref[...] = (dyv * hv * _gelu_grad(gv, th)).astype(BF16)
        a_next = jnp.where(tt < SEQ - 1, pltpu.roll(a, SEQ - 1, 0), 0.0)
        gsum = _scan(a_next, dyv * gl, tt, reverse=True)
        h_prev = jnp.where(tt >= 1, pltpu.roll(hv, 1, 0), 0.0)
        d_mult = gsum * i * xc
        d_i = gsum * mult * xc
        d_xc = gsum * mult * i
        d_la = gsum * h_prev * a - d_mult * (a * a) / mult
        d_pr = (d_la * (-LRU_C * sp)) * r * (1.0 - r)
        d_pi = d_i * i * (1.0 - i)
        d_lam = jnp.sum(d_la * r, axis=0, keepdims=True) * (LRU_C * _sigmoid(-lam))
        d_prb = d_pr.astype(BF16)
        d_pib = d_pi.astype(BF16)
        d_xc = d_xc + _dot_nt(d_prb, wa) + _dot_nt(d_pib, wx)
        dwa_ref[0] = _dot_tn(xcb, d_prb)
        dwx_ref[0] = _dot_tn(xcb, d_pib)
        rows = [jnp.sum(d_xc * shifted[2], axis=0, keepdims=True),
                jnp.sum(d_xc * shifted[1], axis=0, keepdims=True),
                jnp.sum(d_xc * shifted[0], axis=0, keepdims=True),
                jnp.sum(d_xc * xb, axis=0, keepdims=True),
                jnp.sum(d_xc, axis=0, keepdims=True),
                jnp.sum(d_pr, axis=0, keepdims=True),
                jnp.sum(d_pi, axis=0, keepdims=True),
                d_lam]
        ri = lax.broadcasted_iota(jnp.int32, (8, LRU_TC), 0)
        acc = jnp.zeros((8, LRU_TC), F32)
        for k, rv in enumerate(rows):
            acc = jnp.where(ri == k, rv, acc)
        vecs_ref[...] = acc
        d_xb = d_xc * cwv[3:4, :]
        for s in (1, 2, 3):
            d_xb = d_xb + jnp.where(tt < SEQ - s, pltpu.roll(d_xc, SEQ - s, 0), 0.0) * cwv[3 - s:4 - s, :]
        dxbr_ref[...] = d_xb.astype(BF16)

    return _call(
        body, name=name, grid=(LRU_W // LRU_TC,),
        in_specs=[col, col, col, col, cw, vec, bd, vec, bd, vec, vec],
        out_specs=[col, col, pl.BlockSpec((8, LRU_TC), lambda j: (0, j)), bd, bd],
        out_shape=[jax.ShapeDtypeStruct((SEQ, LRU_W), BF16), jax.ShapeDtypeStruct((SEQ, LRU_W), BF16),
                   jax.ShapeDtypeStruct((8, LRU_W), F32),
                   jax.ShapeDtypeStruct((LRU_W // LRU_TC, LRU_TC, LRU_TC), F32),
                   jax.ShapeDtypeStruct((LRU_W // LRU_TC, LRU_TC, LRU_TC), F32)],
        args=[gate, xbr, h, dy, conv_w, conv_b, wa_bd, b_a, wx_bd, b_x, lam], stages=stages)


def _rope(x, cos, sin_signed):
    w = x.shape[1]
    reps = w // 128
    if reps > 1:
        cos = jnp.tile(cos, (1, reps))
        sin_signed = jnp.tile(sin_signed, (1, reps))
    lane = lax.broadcasted_iota(jnp.int32, x.shape, 1)
    first = (lane & 63) < 32
    partner = jnp.where(first, pltpu.roll(x, w - 32, 1), pltpu.roll(x, 32, 1))
    return x * cos + partner * sin_signed


def _both_halves(t, odd):
    lo = lax.broadcasted_iota(jnp.int32, t.shape, 1) < 64
    rolled = pltpu.roll(t, 64, 1)
    return jnp.where(lo, rolled, t) if odd else jnp.where(lo, t, rolled)


def _stack_heads(ta, tb):
    lo = lax.broadcasted_iota(jnp.int32, ta.shape, 1) < 64
    return jnp.concatenate([jnp.where(lo, ta, 0.0), jnp.where(lo, 0.0, ta),
                            jnp.where(lo, tb, 0.0), jnp.where(lo, 0.0, tb)], axis=0)


def _unstack_heads(o):
    lo = lax.broadcasted_iota(jnp.int32, (ATTN_BLOCK, 128), 1) < 64
    return (jnp.where(lo, o[0:128], o[128:256]), jnp.where(lo, o[256:384], o[384:512]))


def _attn_probs(qs, kd, sinks_ref, hk, first_block):
    s = _dot_nt(qs, kd) * (HEAD_DIM ** -0.5)
    row = lax.broadcasted_iota(jnp.int32, s.shape, 0)
    si = lax.broadcasted_iota(jnp.int32, s.shape, 1)
    diff = ATTN_BLOCK + (row & (ATTN_BLOCK - 1)) - si
    valid = (diff >= 0) & (diff < ATTN_BLOCK) & ((si >= ATTN_BLOCK) | jnp.logical_not(first_block))
    s = jnp.where(valid, s, MASK_VALUE)
    rg = lax.broadcasted_iota(jnp.int32, (4 * ATTN_BLOCK, 1), 0) >> 7
    sink = jnp.where(rg == 0, sinks_ref[4 * hk],
                     jnp.where(rg == 1, sinks_ref[4 * hk + 1],
                               jnp.where(rg == 2, sinks_ref[4 * hk + 2], sinks_ref[4 * hk + 3])))
    m = jnp.maximum(jnp.max(s, axis=1, keepdims=True), sink)
    e = jnp.exp(s - m)
    es = jnp.exp(sink - m)
    inv = 1.0 / (jnp.sum(e, axis=1, keepdims=True) + es)
    return e * inv, es * inv


def _prev(i):
    return jnp.maximum(i - 1, 0)


def attn_fwd(q, k, v, cos, sin_signed, sinks, name, stages=()):
    nb = ATTN_BLOCK

    def body(q_ref, kc_ref, kp_ref, vc_ref, vp_ref, cc_ref, sc_ref, cp_ref, sp_ref, sinks_ref,
             qr_ref, kr_ref, y_ref):
        first_block = pl.program_id(0) == 0
        qr = _rope(q_ref[...], cc_ref[...], sc_ref[...])
        kc = _rope(kc_ref[...], cc_ref[...], sc_ref[...])
        kp = _rope(kp_ref[...], cp_ref[...], sp_ref[...])
        qr_ref[...] = qr.astype(BF16)
        kr_ref[...] = kc.astype(BF16)
        k2 = jnp.concatenate([kp, kc], axis=0)
        v2 = jnp.concatenate([vp_ref[...].astype(F32), vc_ref[...].astype(F32)], axis=0)
        for hk in range(N_KV_HEADS):
            kt = hk // 2
            kd = _both_halves(k2[:, kt * 128:(kt + 1) * 128], hk % 2).astype(BF16)
            vd = _both_halves(v2[:, kt * 128:(kt + 1) * 128], hk % 2).astype(BF16)
            qs = _stack_heads(qr[:, (2 * hk) * 128:(2 * hk + 1) * 128],
                              qr[:, (2 * hk + 1) * 128:(2 * hk + 2) * 128]).astype(BF16)
            p, _ = _attn_probs(qs, kd, sinks_ref, hk, first_block)
            ta, tb = _unstack_heads(_dot(p.astype(BF16), vd))
            y_ref[:, (2 * hk) * 128:(2 * hk + 1) * 128] = ta.astype(BF16)
            y_ref[:, (2 * hk + 1) * 128:(2 * hk + 2) * 128] = tb.astype(BF16)

    cur = lambda w: pl.BlockSpec((nb, w), lambda i: (i, 0))
    prv = lambda w: pl.BlockSpec((nb, w), lambda i: (_prev(i), 0))
    return _call(
        body, name=name, grid=(N_ATTN_BLOCKS,),
        in_specs=[cur(D_MODEL), cur(KV_W), prv(KV_W), cur(KV_W), prv(KV_W), cur(128), cur(128), prv(128), prv(128),
                  pl.BlockSpec(memory_space=pltpu.SMEM)],
        out_specs=[cur(D_MODEL), cur(KV_W), cur(D_MODEL)],
        out_shape=[jax.ShapeDtypeStruct((SEQ, D_MODEL), BF16), jax.ShapeDtypeStruct((SEQ, KV_W), BF16),
                   jax.ShapeDtypeStruct((SEQ, D_MODEL), BF16)],
        args=[q, k, k, v, v, cos, sin_signed, cos, sin_signed, sinks], stages=stages)


def attn_bwd(qr, kr, v, dy, cos, sin_signed, sinks, name, stages=()):
    nb = ATTN_BLOCK
    n_steps = N_ATTN_BLOCKS + 1
    scale = HEAD_DIM ** -0.5

    def body(q_ref, kc_ref, kp_ref, vc_ref, vp_ref, dy_ref, cc_ref, sc_ref, cp_ref, sp_ref, sinks_ref,
             dq_ref, dkv_ref, dsk_ref, ck_ref, cv_ref):
        dk_ref = dkv_ref.at[:, pl.ds(0, KV_W)]
        dv_ref = dkv_ref.at[:, pl.ds(KV_W, KV_W)]
        i = pl.program_id(0)

        @pl.when(i == 0)
        def _():
            dsk_ref[...] = jnp.zeros_like(dsk_ref)
            ck_ref[...] = jnp.zeros_like(ck_ref)
            cv_ref[...] = jnp.zeros_like(cv_ref)

        @pl.when(i < N_ATTN_BLOCKS)
        def _():
            qv = q_ref[...].astype(F32)
            dov = dy_ref[...].astype(F32)
            k2 = jnp.concatenate([kp_ref[...].astype(F32), kc_ref[...].astype(F32)], axis=0)
            v2 = jnp.concatenate([vp_ref[...].astype(F32), vc_ref[...].astype(F32)], axis=0)
            lane = lax.broadcasted_iota(jnp.int32, (8, 128), 1)
            lo = lax.broadcasted_iota(jnp.int32, (2 * nb, 128), 1) < 64
            dsk = jnp.zeros((8, 128), F32)
            dk_tiles = []
            dv_tiles = []
            for hk in range(N_KV_HEADS):
                kt = hk // 2
                kd = _both_halves(k2[:, kt * 128:(kt + 1) * 128], hk % 2).astype(BF16)
                vd = _both_halves(v2[:, kt * 128:(kt + 1) * 128], hk % 2).astype(BF16)
                qs = _stack_heads(qv[:, (2 * hk) * 128:(2 * hk + 1) * 128],
                                  qv[:, (2 * hk + 1) * 128:(2 * hk + 2) * 128]).astype(BF16)
                dos = _stack_heads(dov[:, (2 * hk) * 128:(2 * hk + 1) * 128],
                                   dov[:, (2 * hk + 1) * 128:(2 * hk + 2) * 128]).astype(BF16)
                p, ps = _attn_probs(qs, kd, sinks_ref, hk, i == 0)
                dp = _dot_nt(dos, vd)
                delta = jnp.sum(p * dp, axis=1, keepdims=True)
                ds = (p * (dp - delta)).astype(BF16)
                dsink = -ps * delta
                for g in range(4):
                    dsk = dsk + jnp.where(lane == 4 * hk + g, jnp.sum(dsink[g * nb:(g + 1) * nb]), 0.0)
                ta, tb = _unstack_heads(_dot(ds, kd) * scale)
                dq_a = (2 * hk) * 128
                dq_ref[:, dq_a:dq_a + 128] = _rope(ta, cc_ref[...], -sc_ref[...]).astype(BF16)
                dq_ref[:, dq_a + 128:dq_a + 256] = _rope(tb, cc_ref[...], -sc_ref[...]).astype(BF16)
                rk = _dot_tn(ds, qs) * scale
                rv = _dot_tn(p.astype(BF16), dos)
                dk_tiles.append(rk + pltpu.roll(rk, 64, 1))
                dv_tiles.append(rv + pltpu.roll(rv, 64, 1))
            dsk_ref[...] += dsk
            dk_full = jnp.concatenate([jnp.where(lo, dk_tiles[0], dk_tiles[1]),
                                       jnp.where(lo, dk_tiles[2], dk_tiles[3])], axis=1)
            dv_full = jnp.concatenate([jnp.where(lo, dv_tiles[0], dv_tiles[1]),
                                       jnp.where(lo, dv_tiles[2], dv_tiles[3])], axis=1)
            dk_ref[...] = _rope(ck_ref[...] + dk_full[0:nb], cp_ref[...], -sp_ref[...]).astype(BF16)
            dv_ref[...] = (cv_ref[...] + dv_full[0:nb]).astype(BF16)
            ck_ref[...] = dk_full[nb:2 * nb]
            cv_ref[...] = dv_full[nb:2 * nb]

        @pl.when(i == N_ATTN_BLOCKS)
        def _():
            dk_ref[...] = _rope(ck_ref[...], cp_ref[...], -sp_ref[...]).astype(BF16)
            dv_ref[...] = cv_ref[...].astype(BF16)

    qi = lambda i: jnp.minimum(i, N_ATTN_BLOCKS - 1)
    cur = lambda w: pl.BlockSpec((nb, w), lambda i: (qi(i), 0))
    prv = lambda w: pl.BlockSpec((nb, w), lambda i: (_prev(qi(i)), 0))
    out_prev = lambda w: pl.BlockSpec((nb, w), lambda i: (_prev(i), 0))
    return _call(
        body, name=name, grid=(n_steps,),
        in_specs=[cur(D_MODEL), cur(KV_W), prv(KV_W), cur(KV_W), prv(KV_W), cur(D_MODEL),
                  cur(128), cur(128), out_prev(128), out_prev(128), pl.BlockSpec(memory_space=pltpu.SMEM)],
        out_specs=[cur(D_MODEL), out_prev(2 * KV_W), pl.BlockSpec((8, 128), lambda i: (0, 0))],
        out_shape=[jax.ShapeDtypeStruct((SEQ, D_MODEL), BF16), jax.ShapeDtypeStruct((SEQ, 2 * KV_W), BF16),
                   jax.ShapeDtypeStruct((8, 128), F32)],
        scratch_shapes=[pltpu.VMEM((nb, KV_W), F32), pltpu.VMEM((nb, KV_W), F32)],
        args=[qr, kr, kr, v, v, dy, cos, sin_signed, cos, sin_signed, sinks], stages=stages)


def _proj_scratch():
    return [pltpu.VMEM((D_MODEL, D_MODEL), BF16)] * 3 + [pltpu.SemaphoreType.DMA((3 * N_CHIPS,))]


def _load_projs(w_refs, wl_ref, wa_ref, wo_ref, sem):
    for k, (w_ref, dst) in enumerate(zip(w_refs, (wl_ref, wa_ref, wo_ref))):
        _load_weight(w_ref, dst, sem.at[pl.ds(k * N_CHIPS, N_CHIPS)])


def merge_fwd(y_lru, y_attn, g_lru, g_attn, projs, g_post, h_in, name, stages=()):
    tm = MM_ROWS

    def body(yl_ref, ya_ref, gl_ref, ga_ref, w1_ref, w2_ref, w3_ref, gp_ref, h_ref,
             pl_ref, pa_ref, mg_ref, m_ref, o_ref, wl_ref, wa_ref, wo_ref, sem):
        @pl.when(pl.program_id(0) == 0)
        def _():
            _load_projs((w1_ref, w2_ref, w3_ref), wl_ref, wa_ref, wo_ref, sem)

        p_l = _dot(yl_ref[...], wl_ref[...])
        p_a = _dot(ya_ref[...], wa_ref[...])
        pl_ref[...] = p_l.astype(BF16)
        pa_ref[...] = p_a.astype(BF16)
        merged = (_sigmoid(gl_ref[...]) * p_l + _sigmoid(ga_ref[...]) * p_a).astype(BF16)
        mg_ref[...] = merged
        m = _dot(merged, wo_ref[...])
        m_ref[...] = m
        o_ref[...] = h_ref[...] + m * _rsqrt_mean_sq(m) * gp_ref[...]

    row = _ROW(tm)
    return _call(
        body, name=name, grid=(SEQ // tm,),
        in_specs=[row, row, row, row, ANY, ANY, ANY, _VEC, row],
        out_specs=[row] * 5,
        out_shape=[jax.ShapeDtypeStruct((SEQ, D_MODEL), BF16)] * 3 + [jax.ShapeDtypeStruct((SEQ, D_MODEL), F32)] * 2,
        scratch_shapes=_proj_scratch(),
        args=[y_lru, y_attn, g_lru, g_attn, *projs, g_post, h_in], stages=stages)


def merge_bwd(d_out, m, g_post, projs, g_lru, g_attn, p_l, p_a, name, stages=()):
    tm = 256

    def body(do_ref, m_ref, gp_ref, w1_ref, w2_ref, w3_ref, gl_ref, ga_ref, pl_ref, pa_ref,
             dm_ref, dpl_ref, dpa_ref, dgl_ref, dga_ref, dya_ref, dyl_ref, dgp_ref, wl_ref, wa_ref, wo_ref, sem):
        @pl.when(pl.program_id(0) == 0)
        def _():
            _load_projs((w1_ref, w2_ref, w3_ref), wl_ref, wa_ref, wo_ref, sem)
            dgp_ref[...] = jnp.zeros_like(dgp_ref)

        mv = m_ref[...]
        rm = _rsqrt_mean_sq(mv)
        mh = mv * rm
        dn = do_ref[...]
        dgp_ref[...] += jnp.sum(dn * mh, axis=0, keepdims=True)
        t = dn * gp_ref[...]
        dm = (rm * (t - mh * jnp.mean(t * mh, axis=-1, keepdims=True))).astype(BF16)
        dm_ref[...] = dm
        dmg = _dot_nt(dm, wo_ref[...])
        sl = _sigmoid(gl_ref[...])
        sa = _sigmoid(ga_ref[...])
        dpl = (dmg * sl).astype(BF16)
        dpa = (dmg * sa).astype(BF16)
        dpl_ref[...] = dpl
        dpa_ref[...] = dpa
        dgl_ref[...] = (dmg * pl_ref[...].astype(F32) * sl * (1.0 - sl)).astype(BF16)
        dga_ref[...] = (dmg * pa_ref[...].astype(F32) * sa * (1.0 - sa)).astype(BF16)
        dyl_ref[...] = _dot_nt(dpl, wl_ref[...])
        dya_ref[...] = _dot_nt(dpa, wa_ref[...]).astype(BF16)

    row = _ROW(tm)
    return _call(
        body, name=name, grid=(SEQ // tm,),
        in_specs=[row, row, _VEC, ANY, ANY, ANY, row, row, row, row],
        out_specs=[row] * 7 + [_VEC],
        out_shape=[jax.ShapeDtypeStruct((SEQ, D_MODEL), BF16)] * 6 + [jax.ShapeDtypeStruct((SEQ, D_MODEL), F32),
                                                                       jax.ShapeDtypeStruct((1, D_MODEL), F32)],
        scratch_shapes=_proj_scratch(),
        args=[d_out, m, g_post, *projs, g_lru, g_attn, p_l, p_a], stages=stages)


def _rope_tables():
    half = HEAD_DIM // 2
    inv_freq = ROPE_THETA ** (-jnp.arange(half, dtype=F32) / half)
    ang = jnp.arange(SEQ, dtype=F32)[:, None] * inv_freq[None, :]
    cos, sin = jnp.cos(ang), jnp.sin(ang)
    return jnp.tile(jnp.concatenate([cos, cos], axis=1), (1, 2)), jnp.tile(jnp.concatenate([-sin, sin], axis=1), (1, 2))


def _block_diag(w):
    per = LRU_TC // LRU_BLOCK_W
    w4 = w.reshape(LRU_W // LRU_TC, per, LRU_BLOCK_W, LRU_BLOCK_W)
    eye = jnp.eye(per, dtype=w.dtype)
    return jnp.einsum('jacd,ab->jacbd', w4, eye).reshape(LRU_W // LRU_TC, LRU_TC, LRU_TC).astype(BF16)


def _diag_blocks(p):
    per = LRU_TC // LRU_BLOCK_W
    p5 = p.reshape(LRU_W // LRU_TC, per, LRU_BLOCK_W, per, LRU_BLOCK_W)
    return jnp.stack([p5[:, a, :, a, :] for a in range(per)], axis=1).reshape(LRU_W // LRU_BLOCK_W, LRU_BLOCK_W, LRU_BLOCK_W)


def _place():
    x, y, c = lax.axis_index('x'), lax.axis_index('y'), lax.axis_index('c')
    chips = [(1 - x, y), (x, 1 - y), (1 - x, 1 - y)]
    return x, y, c, chips


def _rcopy(src, dst, send_sem, recv_sem, to):
    return pltpu.make_async_remote_copy(src_ref=src, dst_ref=dst, send_sem=send_sem, recv_sem=recv_sem,
                                        device_id=to, device_id_type=MESH)


class _Stage:
    inputs, out_shape, scratch = (), (), ()

    def start(self, ins, outs, scr):
        plan = self._plan(ins, outs, scr)
        for ld in plan['loads']:
            ld.start()
        for cp in plan['sends']:
            cp.start()

    def mid(self, ins, outs, scr):
        plan = self._plan(ins, outs, scr)
        for ld, st in zip(plan['loads'], plan['stores']):
            ld.wait()
            st.start()
        for arrived, onward in zip(plan['arrivals'], plan['forwards']):
            arrived.wait_recv()
            onward.start()

    def end(self, ins, outs, scr):
        plan = self._plan(ins, outs, scr)
        for st in plan['stores']:
            st.wait()
        for arrived in (plan['final_arrivals'] if plan['forwards'] else plan['arrivals']):
            arrived.wait_recv()
        for cp in plan['sends'] + plan['forwards']:
            cp.wait_send()


def _empty_plan():
    return dict(loads=[], stores=[], sends=[], arrivals=[], forwards=[], final_arrivals=[])


class GatherStage(_Stage):
    SUB = 2

    def __init__(self, items):
        self.ranges = [(off, rows) for _, off, rows in items]
        self.inputs = [src for src, _, _ in items]
        self.out_shape = [jax.ShapeDtypeStruct((N_CHIPS, rows, D_MODEL), BF16) for _, rows in self.ranges]
        self.n_ici = 3 * self.SUB * len(items)
        self.scratch = [pltpu.VMEM((sum(r for _, r in self.ranges), D_MODEL), BF16), pltpu.SemaphoreType.DMA((2 * self.n_ici,)),
                        pltpu.SemaphoreType.DMA((2 * self.n_ici,)), pltpu.SemaphoreType.DMA((2 * len(items),))]

    def _plan(self, ins, outs, scr):
        buf, send, recv, lsem = scr
        x, y, c, chips = _place()
        me_q = 2 * x + y
        sib = (x, y, 1 - c)
        plan = _empty_plan()
        boff = 0
        for w, ((off, rows), p_ref, o_ref) in enumerate(zip(self.ranges, ins, outs)):
            hr = rows // 2
            ch = hr // self.SUB
            plan['loads'].append(pltpu.make_async_copy(p_ref.at[pl.ds(off, rows)], buf.at[pl.ds(boff, rows)], lsem.at[2 * w]))
            plan['stores'].append(pltpu.make_async_copy(buf.at[pl.ds(boff, rows)], o_ref.at[me_q], lsem.at[2 * w + 1]))
            boff += rows
            for k in range(self.SUB):
                mine = pl.ds(pl.multiple_of(c * hr + k * ch, 16), ch)
                theirs = pl.ds(pl.multiple_of((1 - c) * hr + k * ch, 16), ch)
                src = p_ref.at[pl.ds(pl.multiple_of(off + c * hr + k * ch, 16), ch)]
                for j, (cx, cy) in enumerate(chips):
                    i = (w * self.SUB + k) * 3 + j
                    got = o_ref.at[2 * cx + cy, mine]
                    got_sib = o_ref.at[2 * cx + cy, theirs]
                    plan['sends'].append(_rcopy(src, o_ref.at[me_q, mine], send.at[i], recv.at[i], (cx, cy, c)))
                    plan['arrivals'].append(_rcopy(got, got, send.at[i], recv.at[i], (cx, cy, c)))
                    plan['forwards'].append(_rcopy(got, got, send.at[self.n_ici + i], recv.at[self.n_ici + i], sib))
                    plan['final_arrivals'].append(
                        _rcopy(got_sib, got_sib, send.at[self.n_ici + i], recv.at[self.n_ici + i], sib))
        return plan


class PairStage(_Stage):
    def __init__(self, grads):
        self.inputs = list(grads)
        self.out_shape = [jax.ShapeDtypeStruct((N_CHIPS, 1) + g.shape[2:], BF16) for g in grads]
        n_cp = N_CHIPS * len(grads)
        self.scratch = [pltpu.SemaphoreType.DMA((n_cp,)), pltpu.SemaphoreType.DMA((n_cp,))]

    def _plan(self, ins, outs, scr):
        send, recv = scr
        x, y, c, _ = _place()
        plan = _empty_plan()
        for w, (g_ref, l_ref) in enumerate(zip(ins, outs)):
            for q in range(N_CHIPS):
                i = w * N_CHIPS + q
                plan['sends'].append(_rcopy(g_ref.at[q, pl.ds(1 - c, 1)], l_ref.at[q], send.at[i], recv.at[i], (x, y, 1 - c)))
        plan['arrivals'] = plan['sends']
        return plan


class ChipStage(_Stage):
    def __init__(self, items):
        self.ranges = [(off, n) for _, off, n in items]
        self.inputs = [s for s, _, _ in items]
        self.out_shape = [jax.ShapeDtypeStruct((N_CHIPS, n, D_MODEL), BF16) for _, n in self.ranges]
        n_cp = 3 * len(items)
        self.scratch = [pltpu.VMEM((sum(n for _, n in self.ranges), D_MODEL), BF16), pltpu.SemaphoreType.DMA((n_cp,)),
                        pltpu.SemaphoreType.DMA((n_cp,)), pltpu.SemaphoreType.DMA((2 * len(items),))]

    def _plan(self, ins, outs, scr):
        buf, send, recv, lsem = scr
        x, y, c, chips = _place()
        me_q = 2 * x + y
        plan = _empty_plan()
        boff = 0
        for w, ((off, n), s_ref, l_ref) in enumerate(zip(self.ranges, ins, outs)):
            rows = pl.ds(off, n)
            plan['loads'].append(pltpu.make_async_copy(s_ref.at[me_q, rows], buf.at[pl.ds(boff, n)], lsem.at[2 * w]))
            plan['stores'].append(pltpu.make_async_copy(buf.at[pl.ds(boff, n)], l_ref.at[me_q], lsem.at[2 * w + 1]))
            boff += n
            for j, (cx, cy) in enumerate(chips):
                i = w * 3 + j
                got = l_ref.at[2 * cx + cy]
                plan['sends'].append(_rcopy(s_ref.at[2 * cx + cy, rows], l_ref.at[me_q], send.at[i], recv.at[i], (cx, cy, c)))
                plan['arrivals'].append(_rcopy(got, got, send.at[i], recv.at[i], (cx, cy, c)))
        return plan


class SwapStage(_Stage):
    def __init__(self, halves):
        self.inputs = list(halves)
        self.out_shape = [jax.ShapeDtypeStruct((2,) + h.shape, F32) for h in halves]
        self.scratch = [pltpu.VMEM((sum(h.shape[0] for h in halves), D_MODEL), F32), pltpu.SemaphoreType.DMA((len(halves),)),
                        pltpu.SemaphoreType.DMA((len(halves),)), pltpu.SemaphoreType.DMA((2 * len(halves),))]

    def _plan(self, ins, outs, scr):
        buf, send, recv, lsem = scr
        x, y, c, _ = _place()
        plan = _empty_plan()
        boff = 0
        for w, (h_ref, o_ref) in enumerate(zip(ins, outs)):
            hr = h_ref.shape[0]
            plan['loads'].append(pltpu.make_async_copy(h_ref, buf.at[pl.ds(boff, hr)], lsem.at[2 * w]))
            plan['stores'].append(pltpu.make_async_copy(buf.at[pl.ds(boff, hr)], o_ref.at[c], lsem.at[2 * w + 1]))
            boff += hr
            got = o_ref.at[1 - c]
            plan['sends'].append(_rcopy(h_ref, o_ref.at[c], send.at[w], recv.at[w], (x, y, 1 - c)))
            plan['arrivals'].append(_rcopy(got, got, send.at[w], recv.at[w], (x, y, 1 - c)))
        return plan


class SmallGatherStage(_Stage):
    def __init__(self, blk):
        self.inputs = [blk]
        self.out_shape = [jax.ShapeDtypeStruct((N_DEV,) + blk.shape, blk.dtype)]
        self.scratch = [pltpu.VMEM(blk.shape, blk.dtype), pltpu.SemaphoreType.DMA((7,)), pltpu.SemaphoreType.DMA((7,)),
                        pltpu.SemaphoreType.DMA((2,))]

    def _plan(self, ins, outs, scr):
        (x_ref,), (o_ref,), (buf, send, recv, lsem) = ins, outs, scr
        x, y, c, chips = _place()
        sib = (x, y, 1 - c)

        def slot(px, py, pc):
            return o_ref.at[4 * px + 2 * py + pc]

        plan = _empty_plan()
        plan['loads'].append(pltpu.make_async_copy(x_ref, buf, lsem.at[0]))
        plan['stores'].append(pltpu.make_async_copy(buf, slot(x, y, c), lsem.at[1]))
        from_sib = slot(x, y, 1 - c)
        plan['sends'].append(_rcopy(x_ref, slot(x, y, c), send.at[0], recv.at[0], sib))
        plan['final_arrivals'].append(_rcopy(from_sib, from_sib, send.at[0], recv.at[0], sib))
        for j, (cx, cy) in enumerate(chips):
            got, got_sib = slot(cx, cy, c), slot(cx, cy, 1 - c)
            plan['sends'].append(_rcopy(x_ref, slot(x, y, c), send.at[1 + j], recv.at[1 + j], (cx, cy, c)))
            plan['arrivals'].append(_rcopy(got, got, send.at[1 + j], recv.at[1 + j], (cx, cy, c)))
            plan['forwards'].append(_rcopy(got, got, send.at[4 + j], recv.at[4 + j], sib))
            plan['final_arrivals'].append(_rcopy(got_sib, got_sib, send.at[4 + j], recv.at[4 + j], sib))
        return plan


def comm_call(name, stages):
    def body():
        pass

    return _call(body, name=name, grid=(1,), in_specs=[], out_specs=[], out_shape=[], args=[], stages=stages)[1]


def pair_sum(g4, land, c_arr, name):
    hr = g4.shape[2]

    def body(c_ref, g_ref, l_ref, o_ref):
        o_ref[0] = (g_ref[0, 0].astype(F32) + l_ref[0, 0].astype(F32)).astype(BF16)

    return pl.pallas_call(
        body, name=name,
        grid_spec=pltpu.PrefetchScalarGridSpec(
            num_scalar_prefetch=1, grid=(N_CHIPS,),
            in_specs=[pl.BlockSpec((1, 1, hr, D_MODEL), lambda q, c: (q, c[0], 0, 0)),
                      pl.BlockSpec((1, 1, hr, D_MODEL), lambda q, c: (q, 0, 0, 0))],
            out_specs=pl.BlockSpec((1, hr, D_MODEL), lambda q, c: (q, 0, 0))),
        out_shape=jax.ShapeDtypeStruct((N_CHIPS, hr, D_MODEL), BF16),
        compiler_params=_params(1),
    )(c_arr, g4, land)


def chip_sum(land, name):
    hr = land.shape[1]
    tr = hr if hr <= 352 else hr // 2

    def body(l_ref, o_ref):
        acc = l_ref[0].astype(F32)
        for s in range(1, N_CHIPS):
            acc = acc + l_ref[s].astype(F32)
        o_ref[...] = acc

    return pl.pallas_call(
        body, name=name, grid=(hr // tr,),
        in_specs=[pl.BlockSpec((N_CHIPS, tr, D_MODEL), lambda i: (0, i, 0))],
        out_specs=pl.BlockSpec((tr, D_MODEL), lambda i: (i, 0)),
        out_shape=jax.ShapeDtypeStruct((hr, D_MODEL), F32),
        compiler_params=_params(1),
    )(land)


def small_sum(parts):
    def body(p_ref, o_ref):
        acc = p_ref[0]
        for s in range(1, N_DEV):
            acc = acc + p_ref[s]
        o_ref[...] = acc

    return pl.pallas_call(
        body, name='small_sum', grid=(1,),
        in_specs=[pl.BlockSpec(parts.shape, lambda i: (0, 0, 0))],
        out_specs=pl.BlockSpec(parts.shape[1:], lambda i: (0, 0)),
        out_shape=jax.ShapeDtypeStruct(parts.shape[1:], F32),
        compiler_params=_params(1),
    )(parts)


def _adam_math(w, g, m, v):
    m2 = ADAM_B1 * m + (1.0 - ADAM_B1) * g
    v2 = ADAM_B2 * v + (1.0 - ADAM_B2) * (g * g)
    m_hat = m2 / (1.0 - ADAM_B1 ** ADAM_STEP)
    v_hat = v2 / (1.0 - ADAM_B2 ** ADAM_STEP)
    delta = -ADAM_LR * (m_hat / (jnp.sqrt(v_hat) + ADAM_EPS) + ADAM_WD * w)
    return delta, m2, v2


def _adam_body(n_parts, transposed):
    def body(*refs):
        g_refs = refs[:n_parts]
        w_ref, m_ref, v_ref, go_ref, d_ref, mo_ref, vo_ref = refs[n_parts:]
        if transposed:
            gt = jnp.concatenate([g_ref[h] for h in range(2) for g_ref in g_refs], axis=0)
            g = gt.T
        else:
            g = jnp.concatenate([g_ref[0] for g_ref in g_refs], axis=0) if n_parts > 1 else g_refs[0][0]
        go_ref[...] = g
        d_ref[...], mo_ref[...], vo_ref[...] = _adam_math(w_ref[...], g, m_ref[...], v_ref[...])
    return body


def adam_rows(fulls, name, w, m, v):
    hr = w.shape[0] // 2
    blk = pl.BlockSpec((hr, D_MODEL), lambda h: (h, 0))
    return pl.pallas_call(
        _adam_body(len(fulls), False), name='adam_' + name, grid=(2,),
        in_specs=[pl.BlockSpec((1, f.shape[1], D_MODEL), lambda h: (h, 0, 0)) for f in fulls] + [blk, blk, blk],
        out_specs=[blk] * 4,
        out_shape=[jax.ShapeDtypeStruct(w.shape, F32)] * 4,
        compiler_params=_params(1),
    )(*fulls, w, m, v)


def adam_cols(fulls, name, w, m, v):
    cols = w.shape[1]
    tr = 128
    blk = pl.BlockSpec((tr, cols), lambda i: (i, 0))
    return pl.pallas_call(
        _adam_body(len(fulls), True), name='adam_' + name, grid=(D_MODEL // tr,),
        in_specs=[pl.BlockSpec((2, f.shape[1], tr), lambda i: (0, 0, i)) for f in fulls] + [blk, blk, blk],
        out_specs=[blk] * 4,
        out_shape=[jax.ShapeDtypeStruct(w.shape, F32)] * 4,
        compiler_params=_params(1),
    )(*fulls, w, m, v)


def adam_small(g, w, m, v):
    def body(g_ref, w_ref, m_ref, v_ref, d_ref, mo_ref, vo_ref):
        d_ref[...], mo_ref[...], vo_ref[...] = _adam_math(w_ref[...], g_ref[...], m_ref[...], v_ref[...])

    blk = pl.BlockSpec(w.shape, lambda i: (0, 0))
    return pl.pallas_call(
        body, name='adam_small', grid=(1,), in_specs=[blk] * 4, out_specs=[blk] * 3,
        out_shape=[jax.ShapeDtypeStruct(w.shape, F32)] * 3, compiler_params=_params(1),
    )(g, w, m, v)


WEIGHTS = ('ffn1_pre_g', 'ffn1_w_gu', 'ffn1_w_down', 'ffn1_post_g', 'mix_pre_g', 'w_in', 'conv_w', 'conv_b',
           'lru_w_a', 'lru_b_a', 'lru_w_x', 'lru_b_x', 'lru_lambda', 'attn_sinks', 'w_proj_lru', 'w_proj_attn',
           'w_out', 'mix_post_g', 'ffn2_pre_g', 'ffn2_w_gu', 'ffn2_w_down', 'ffn2_post_g')
SMALL = tuple(n for n in WEIGHTS if n not in PACK_OFF)


def _pack_small(d, conv_rows):
    sinks = jnp.pad(d['attn_sinks'].reshape(1, N_Q_HEADS), ((0, 0), (0, D_MODEL - N_Q_HEADS)))
    conv = jnp.pad(conv_rows, ((0, ROW_WA - ROW_CONV - conv_rows.shape[0]), (0, 0)))
    return jnp.concatenate([d[n].reshape(1, D_MODEL) for n in SMALL_VECS] + [sinks, conv]
                           + [d['lru_w_a'].reshape(64, D_MODEL), d['lru_w_x'].reshape(64, D_MODEL)], axis=0)


def _unpack_small(p, shapes):
    out = {n: p[k:k + 1].reshape(shapes[n]) for k, n in enumerate(SMALL_VECS)}
    out['attn_sinks'] = p[ROW_SINKS:ROW_SINKS + 1, :N_Q_HEADS].reshape(shapes['attn_sinks'])
    out['conv_w'] = p[ROW_CONV:ROW_CONV + 1].reshape(shapes['conv_w'])
    out['lru_w_a'] = p[ROW_WA:ROW_WA + 64].reshape(shapes['lru_w_a'])
    out['lru_w_x'] = p[ROW_WX:ROW_WX + 64].reshape(shapes['lru_w_x'])
    return out


def kernel(x, ffn1_pre_g, ffn1_w_gu, ffn1_w_down, ffn1_post_g, mix_pre_g, w_in, conv_w, conv_b, lru_w_a, lru_b_a, lru_w_x, lru_b_x, lru_lambda, attn_sinks, w_proj_lru, w_proj_attn, w_out, mix_post_g, ffn2_pre_g, ffn2_w_gu, ffn2_w_down, ffn2_post_g, loss_target, m_ffn1_pre_g, m_ffn1_w_gu, m_ffn1_w_down, m_ffn1_post_g, m_mix_pre_g, m_w_in, m_conv_w, m_conv_b, m_lru_w_a, m_lru_b_a, m_lru_w_x, m_lru_b_x, m_lru_lambda, m_attn_sinks, m_w_proj_lru, m_w_proj_attn, m_w_out, m_mix_post_g, m_ffn2_pre_g, m_ffn2_w_gu, m_ffn2_w_down, m_ffn2_post_g, v_ffn1_pre_g, v_ffn1_w_gu, v_ffn1_w_down, v_ffn1_post_g, v_mix_pre_g, v_w_in, v_conv_w, v_conv_b, v_lru_w_a, v_lru_b_a, v_lru_w_x, v_lru_b_x, v_lru_lambda, v_attn_sinks, v_w_proj_lru, v_w_proj_attn, v_w_out, v_mix_post_g, v_ffn2_pre_g, v_ffn2_w_gu, v_ffn2_w_down, v_ffn2_post_g):
    given = dict(locals())
    w = {n: given[n] for n in WEIGHTS}
    mom = {n: given['m_' + n] for n in WEIGHTS}
    var = {n: given['v_' + n] for n in WEIGHTS}
    shapes = {n: w[n].shape for n in WEIGHTS}
    xq = lax.axis_index('x')
    yq = lax.axis_index('y')
    cq = lax.axis_index('c')
    me_q = 2 * xq + yq

    c_arr = cq.reshape(1).astype(jnp.int32)
    xs, target = x[0], loss_target[0]
    sw = {n: (w[n][0] if w[n].ndim > 2 else w[n]) for n in SMALL}
    cos, sin_signed = _rope_tables()
    wa_bd = _block_diag(sw['lru_w_a'])
    wx_bd = _block_diag(sw['lru_w_x'])
    sinks = sw['attn_sinks'].reshape(N_Q_HEADS)

    shard = {n: (w[n][0].T if t else w[n][0]).astype(BF16) for n, _, t in PACK}
    conv_pad = jnp.pad(w['conv_w'][0], ((0, 4), (0, 0)))

    def whole(name):
        return (shard[name], 0, PACK_ROWS_OF[name])

    def part(name, p, n_parts=2):
        rows = PACK_ROWS_OF[name] // n_parts
        return (shard[name], p * rows, rows)

    (w_gu1,), (conv_all,) = comm_call('gather_first', [GatherStage([whole('ffn1_w_gu')]), SmallGatherStage(conv_pad)])
    sw['conv_w'] = jnp.transpose(conv_all[0::2, :4, :], (1, 0, 2)).reshape(4, LRU_W)
    proj_names = ['w_proj_lru', 'w_proj_attn', 'w_out']

    (n1, g1, u1, a1), ((w_down1,),) = ffn_fwd_a(xs, sw['ffn1_pre_g'], [w_gu1], 'ffn1_fwd_a',
                                                 stages=[GatherStage([whole('ffn1_w_down')])])
    (f1, h1), ((w_in_t,),) = ffn_fwd_b(a1, w_down1, sw['ffn1_post_g'], xs, 'ffn1_fwd_b', stages=[GatherStage([whole('w_in')])])
    (um, gate, xbr, q, k, v, g_lru, g_attn), ((w_gu2a,),) = mix_in(h1, sw['mix_pre_g'], w_in_t, 'mix_in',
                                                                   stages=[GatherStage([part('ffn2_w_gu', 0)])])
    (y_lru, h_lru), ((w_gu2b,),) = lru_fwd(gate, xbr, sw['conv_w'], sw['conv_b'], wa_bd, sw['lru_b_a'], wx_bd, sw['lru_b_x'],
                                           sw['lru_lambda'], 'lru_fwd', stages=[GatherStage([part('ffn2_w_gu', 1)])])
    (qr, kr, y_attn), (projs,) = attn_fwd(q, k, v, cos, sin_signed, sinks, 'attn_fwd',
                                          stages=[GatherStage([whole(n) for n in proj_names])])
    (p_l, p_a, merged, m, h2), ((w_down2,),) = merge_fwd(y_lru, y_attn, g_lru, g_attn, projs, sw['mix_post_g'], h1, 'merge_fwd',
                                                         stages=[GatherStage([whole('ffn2_w_down')])])
    w_gu2 = [w_gu2a, w_gu2b]
    (n2, g2, u2, a2), _ = ffn_fwd_a(h2, sw['ffn2_pre_g'], w_gu2, 'ffn2_fwd_a')
    (f2, dy, loss_blk), _ = ffn_fwd_b(a2, w_down2, sw['ffn2_post_g'], h2, 'ffn2_fwd_b', target=target)
    loss = lax.psum(loss_blk[0, 0], ('x', 'y', 'c'))

    gs, full = {}, {}

    def pair_stage(names, grads):
        g4 = [g.reshape(N_CHIPS, 2, PACK_ROWS_OF[n] // 2, D_MODEL) for n, g in zip(names, grads)]
        return PairStage(g4), g4

    def pair_sums(names, g4, lands):
        return [pair_sum(g, l, c_arr, 'pair_sum_' + n) for n, g, l in zip(names, g4, lands)]

    def chip_sums(names, lands):
        return [chip_sum(l, 'chip_sum_' + n) for n, l in zip(names, lands)]

    def halves(s, n_parts=2):
        n = s.shape[1] // n_parts
        return [(s, p * n, n) for p in range(n_parts)]

    (df2, dgu2, gs['ffn2_post_g']), _ = ffn_bwd_a(dy, f2, sw['ffn2_post_g'], w_down2, g2, u2, 'ffn2_bwd_a')
    g_down2, _ = mm_tn([a2], df2, 1408, 'ffn2_dw_down')
    st, g4 = pair_stage(['ffn2_w_down'], [g_down2])
    g_gu2, (lands,) = mm_tn([dgu2], n2, 1408, 'ffn2_dw_gu', stages=[st])
    (s_down2,) = pair_sums(['ffn2_w_down'], g4, lands)
    st, g4 = pair_stage(['ffn2_w_gu'], [g_gu2])
    (dh2, gs['ffn2_pre_g']), ((l_down2,), lands) = norm_bwd([dgu2], w_gu2, h2, sw['ffn2_pre_g'], dy, 'ffn2_bwd_b',
                                                            stages=[ChipStage([(s_down2, 0, s_down2.shape[1])]), st])
    (s_gu2,) = pair_sums(['ffn2_w_gu'], g4, lands)
    (h_down2,) = chip_sums(['ffn2_w_down'], [l_down2])

    (dm, dpl, dpa, dgl, dga, dya, dyl, gs['mix_post_g']), ((l_gu2a,),) = merge_bwd(
        dh2, m, sw['mix_post_g'], projs, g_lru, g_attn, p_l, p_a, 'merge_bwd', stages=[ChipStage(halves(s_gu2)[:1])])
    g_projs = [mm_tn([merged if n == 'w_out' else (y_lru if n == 'w_proj_lru' else y_attn)],
                     dm if n == 'w_out' else (dpl if n == 'w_proj_lru' else dpa), D_MODEL, 'd' + n)[0] for n in proj_names]
    st, g4 = pair_stage(proj_names, g_projs)
    (dq, dkv, dsk), ((l_gu2b,), lands, (full['ffn2_w_down'],)) = attn_bwd(
        qr, kr, v, dya, cos, sin_signed, sinks, 'attn_bwd', stages=[ChipStage(halves(s_gu2)[1:]), st, SwapStage([h_down2])])
    full['ffn2_w_down'] = [full['ffn2_w_down']]
    gs['attn_sinks'] = dsk[0:1, 0:N_Q_HEADS]
    s_projs = pair_sums(proj_names, g4, lands)
    h_gu2 = chip_sums(['ffn2_w_gu_a', 'ffn2_w_gu_b'], [l_gu2a, l_gu2b])
    (dgate, dxbr, vecs, dwa, dwx), (l_projs, full['ffn2_w_gu']) = lru_bwd(
        gate, xbr, h_lru, dyl, sw['conv_w'], sw['conv_b'], wa_bd, sw['lru_b_a'], wx_bd, sw['lru_b_x'], sw['lru_lambda'],
        'lru_bwd', stages=[ChipStage([(s, 0, s.shape[1]) for s in s_projs]), SwapStage(h_gu2)])
    gs['conv_w'] = vecs[0:4]
    gs['conv_b'], gs['lru_b_a'], gs['lru_b_x'], gs['lru_lambda'] = vecs[4:5], vecs[5:6], vecs[6:7], vecs[7:8]
    gs['lru_w_a'] = _diag_blocks(dwa)
    gs['lru_w_x'] = _diag_blocks(dwx)
    dz = [dgate, dxbr, dq, dkv, dgl, dga]
    g_in, _ = mm_tn(dz, um, 512, 'dw_in')
    h_projs = chip_sums(proj_names, l_projs)
    st, g4 = pair_stage(['w_in'], [g_in])
    (dh1, gs['mix_pre_g']), (lands, f_projs) = norm_bwd(dz, [w_in_t], h1, sw['mix_pre_g'], dh2, 'mix_bwd_in',
                                                        stages=[st, SwapStage(h_projs)])
    for n, f in zip(proj_names, f_projs):
        full[n] = [f]
    (s_in,) = pair_sums(['w_in'], g4, lands)

    (df1, dgu1, gs['ffn1_post_g']), ((l_in_a,),) = ffn_bwd_a(dh1, f1, sw['ffn1_post_g'], w_down1, g1, u1, 'ffn1_bwd_a',
                                                             stages=[ChipStage(halves(s_in)[:1])])
    g_down1, _ = mm_tn([a1], df1, 1408, 'ffn1_dw_down')
    st, g4 = pair_stage(['ffn1_w_down'], [g_down1])
    g_gu1, ((l_in_b,), lands) = mm_tn([dgu1], n1, 1408, 'ffn1_dw_gu', stages=[ChipStage(halves(s_in)[1:]), st])
    (s_down1,) = pair_sums(['ffn1_w_down'], g4, lands)
    h_in = chip_sums(['w_in_a', 'w_in_b'], [l_in_a, l_in_b])
    st, g4 = pair_stage(['ffn1_w_gu'], [g_gu1])
    (dx, gs['ffn1_pre_g']), ((l_down1,), lands, full['w_in']) = norm_bwd(
        [dgu1], [w_gu1], xs, sw['ffn1_pre_g'], dh1, 'ffn1_bwd_b',
        stages=[ChipStage([(s_down1, 0, s_down1.shape[1])]), st, SwapStage(h_in)])
    (s_gu1,) = pair_sums(['ffn1_w_gu'], g4, lands)
    (h_down1,) = chip_sums(['ffn1_w_down'], [l_down1])
    small_blk = _pack_small(gs, gs['conv_w'])
    (l_gu1,), (small_all,), (f_down1,) = comm_call(
        'reduce_last', [ChipStage([(s_gu1, 0, s_gu1.shape[1])]), SmallGatherStage(small_blk), SwapStage([h_down1])])
    full['ffn1_w_down'] = [f_down1]
    full['ffn1_w_gu'] = comm_call('swap_last', [SwapStage(chip_sums(['ffn1_w_gu'], [l_gu1]))])[0]

    out_g, out_d, out_m, out_v = {}, {}, {}, {}
    for n, _, t in PACK:
        fn = adam_cols if t else adam_rows
        g_, d_, m_, v_ = fn(full[n], n, w[n][0], mom[n][0], var[n][0])
        out_g[n], out_d[n], out_m[n], out_v[n] = g_[None], d_[None], m_[None], v_[None]

    tot = small_sum(small_all)
    conv_g = lax.dynamic_slice(tot[ROW_CONV:ROW_CONV + 4], (0, me_q * (LRU_W // N_CHIPS)), (4, LRU_W // N_CHIPS))
    small_g = _unpack_small(tot, shapes)
    small_g['conv_w'] = conv_g.reshape(shapes['conv_w'])
    g_pack = jnp.concatenate([tot[:ROW_CONV], conv_g.reshape(1, D_MODEL), jnp.zeros((ROW_WA - ROW_CONV - 1, D_MODEL), F32),
                              tot[ROW_WA:]], axis=0)
    packs = [_pack_small({n: d[n] for n in SMALL}, d['conv_w'].reshape(1, D_MODEL)) for d in (w, mom, var)]
    d_p, m_p, v_p = adam_small(g_pack, *packs)
    for n in SMALL:
        out_g[n] = small_g[n]
    for dst, p in ((out_d, d_p), (out_m, m_p), (out_v, v_p)):
        dst.update(_unpack_small(p, shapes))

    return (loss, dx[None], *[out_g[n] for n in WEIGHTS], *[out_d[n] for n in WEIGHTS],
            *[out_m[n] for n in WEIGHTS], *[out_v[n] for n in WEIGHTS])
```

```python
import jax
import jax.numpy as jnp
from jax import lax
from jax.experimental import pallas as pl
from jax.experimental.pallas import tpu as pltpu

F32 = jnp.float32
BF16 = jnp.bfloat16

SEQ = 2048
D_MODEL = 1024
D_FF = 2816
LRU_W = 1024
LRU_BLOCK_W = 64
HEAD_DIM = 64
N_Q_HEADS = 16
N_KV_HEADS = 4
KV_W = N_KV_HEADS * HEAD_DIM
ATTN_BLOCK = 128
N_ATTN_BLOCKS = SEQ // ATTN_BLOCK
IN_SEGS = (1024, 1024, 1024, 256, 256, 1024, 1024)
IN_W = sum(IN_SEGS)
NORM_EPS = 1e-6
MASK_VALUE = -1e30
ROPE_THETA = 10000.0
LRU_C = 8.0
MACARON = 0.5
ADAM_LR = 0.001
ADAM_B1 = 0.9
ADAM_B2 = 0.999
ADAM_EPS = 1e-08
ADAM_WD = 0.01
ADAM_STEP = 10

N_CHIPS = 4
N_DEV = 8
VMEM_LIMIT = 56 * 1024 * 1024
MM_ROWS = 256
MESH = pl.DeviceIdType.MESH
ANY = pl.BlockSpec(memory_space=pl.ANY)

PACK = (('ffn1_w_gu', 1408, True), ('w_in', 1408, True), ('ffn2_w_gu', 1408, True),
        ('ffn1_w_down', 704, False), ('ffn2_w_down', 704, False),
        ('w_proj_lru', 256, False), ('w_proj_attn', 256, False), ('w_out', 256, False))
PACK_ROWS_OF = {n: r for n, r, _ in PACK}
PACK_OFF = {}
_o = 0
for _n, _r, _t in PACK:
    PACK_OFF[_n] = _o
    _o += _r

SMALL_VECS = ('ffn1_pre_g', 'ffn1_post_g', 'mix_pre_g', 'conv_b', 'lru_b_a', 'lru_b_x', 'lru_lambda',
              'mix_post_g', 'ffn2_pre_g', 'ffn2_post_g')
SMALL_ROWS = 144
ROW_SINKS, ROW_CONV, ROW_WA, ROW_WX = 10, 11, 16, 80


def _dot(a, b):
    return jnp.dot(a, b, preferred_element_type=F32)


def _dot_nt(a, b):
    return lax.dot_general(a, b, (((1,), (1,)), ((), ())), preferred_element_type=F32)


def _dot_tn(a, b):
    return lax.dot_general(a, b, (((0,), (0,)), ((), ())), preferred_element_type=F32)


def _params(n_grid):
    return pltpu.CompilerParams(dimension_semantics=("arbitrary",) * n_grid, vmem_limit_bytes=VMEM_LIMIT)


def _sigmoid(x):
    return 1.0 / (1.0 + jnp.exp(-x))


def _rsqrt_mean_sq(x):
    return lax.rsqrt(jnp.mean(x * x, axis=-1, keepdims=True) + NORM_EPS)


def _expm1(x):
    poly = x * (1.0 + x * (0.5 + x * (1.0 / 6.0 + x * (1.0 / 24.0 + x * (1.0 / 120.0)))))
    return jnp.where(jnp.abs(x) < 0.1, poly, jnp.exp(x) - 1.0)


_GELU_K = 0.7978845608028654
_GELU_C = 0.044715


def _gelu(x):
    t = jnp.tanh(_GELU_K * (x + _GELU_C * x * x * x))
    return 0.5 * x * (1.0 + t), t


def _gelu_grad(x, t):
    return 0.5 * (1.0 + t) + 0.5 * x * (1.0 - t * t) * _GELU_K * (1.0 + 3.0 * _GELU_C * x * x)


def _load_weight(w_refs, dst_ref, sem):
    w_refs = list(w_refs) if isinstance(w_refs, (list, tuple)) else [w_refs]
    rows = dst_ref.shape[0] // N_CHIPS
    rp = rows // len(w_refs)
    cps = [pltpu.make_async_copy(w_ref.at[q], dst_ref.at[pl.ds(q * rows + p * rp, rp)], sem.at[p * N_CHIPS + q])
           for p, w_ref in enumerate(w_refs) for q in range(N_CHIPS)]
    for cp in cps:
        cp.start()
    for cp in cps:
        cp.wait()


def _weight_scratch(rows_total, parts=1):
    return [pltpu.VMEM((rows_total, D_MODEL), BF16), pltpu.SemaphoreType.DMA((N_CHIPS * parts,))]


_ROW = lambda tm: pl.BlockSpec((tm, D_MODEL), lambda i: (i, 0))
_VEC = pl.BlockSpec((1, D_MODEL), lambda i: (0, 0))


def _call(body, *, name, grid, in_specs, out_specs, out_shape, args, scratch_shapes=(), stages=()):
    in_specs, out_specs, out_shape, scratch_shapes = list(in_specs), list(out_specs), list(out_shape), list(scratch_shapes)
    n_in, n_out, n_sc = len(in_specs), len(out_specs), len(scratch_shapes)
    k_in = [len(s.inputs) for s in stages]
    k_out = [len(s.out_shape) for s in stages]
    k_sc = [len(s.scratch) for s in stages]
    last = grid[0] - 1

    def split(refs, counts):
        parts, pos = [], 0
        for k in counts:
            parts.append(refs[pos:pos + k])
            pos += k
        return parts

    def full(*refs):
        ins, s_ins, outs, s_outs, scr, s_scr = split(refs, [n_in, sum(k_in), n_out, sum(k_out), n_sc, sum(k_sc)])
        per_stage = list(zip(stages, split(s_ins, k_in), split(s_outs, k_out), split(s_scr, k_sc)))
        i = pl.program_id(0)
        if stages:
            @pl.when(i == 0)
            def _():
                for s, a, b, c in per_stage:
                    s.start(a, b, c)

        body(*ins, *outs, *scr)
        if stages:
            @pl.when(i == max(last - 1, 0))
            def _():
                for s, a, b, c in per_stage:
                    s.mid(a, b, c)

            @pl.when(i == last)
            def _():
                for s, a, b, c in per_stage:
                    s.end(a, b, c)

    res = pl.pallas_call(
        full, name=name, grid=grid,
        in_specs=in_specs + [ANY] * sum(k_in),
        out_specs=out_specs + [ANY] * sum(k_out),
        out_shape=out_shape + [o for s in stages for o in s.out_shape],
        scratch_shapes=scratch_shapes + [x for s in stages for x in s.scratch],
        compiler_params=_params(1),
    )(*args, *[a for s in stages for a in s.inputs])
    return list(res[:n_out]), split(list(res[n_out:]), k_out)


def ffn_fwd_a(x, g_pre, w_gu_t, name, stages=()):
    tm, tn = MM_ROWS, 256
    n_w = len(w_gu_t)

    def body(x_ref, gp_ref, *refs):
        w_refs = refs[:n_w]
        n_ref, g_ref, u_ref, a_ref, wt_ref, sem = refs[n_w:]

        @pl.when(pl.program_id(0) == 0)
        def _():
            _load_weight(w_refs, wt_ref, sem)

        xv = x_ref[...]
        n = (xv * _rsqrt_mean_sq(xv) * gp_ref[...]).astype(BF16)
        n_ref[...] = n
        for j in range(D_FF // tn):
            g = _dot_nt(n, wt_ref[j * tn:(j + 1) * tn, :])
            u = _dot_nt(n, wt_ref[D_FF + j * tn:D_FF + (j + 1) * tn, :])
            g_ref[:, j * tn:(j + 1) * tn] = g.astype(BF16)
            u_ref[:, j * tn:(j + 1) * tn] = u.astype(BF16)
            a_ref[:, j * tn:(j + 1) * tn] = (g * _sigmoid(g) * u).astype(BF16)

    wide = pl.BlockSpec((tm, D_FF), lambda i: (i, 0))
    return _call(
        body, name=name, grid=(SEQ // tm,),
        in_specs=[_ROW(tm), _VEC] + [ANY] * n_w,
        out_specs=[_ROW(tm), wide, wide, wide],
        out_shape=[jax.ShapeDtypeStruct((SEQ, D_MODEL), BF16)] + [jax.ShapeDtypeStruct((SEQ, D_FF), BF16)] * 3,
        scratch_shapes=_weight_scratch(2 * D_FF, n_w),
        args=[x, g_pre, *w_gu_t], stages=stages)


def ffn_fwd_b(a, w_down, g_post, h_in, name, target=None, stages=()):
    tm = MM_ROWS
    final = target is not None

    def body(*refs):
        if final:
            a_ref, wf_ref, gp_ref, h_ref, t_ref, f_ref, o_ref, loss_ref, wd_ref, sem = refs
        else:
            a_ref, wf_ref, gp_ref, h_ref, f_ref, o_ref, wd_ref, sem = refs

        @pl.when(pl.program_id(0) == 0)
        def _():
            _load_weight(wf_ref, wd_ref, sem)
            if final:
                loss_ref[...] = jnp.zeros_like(loss_ref)

        f = _dot(a_ref[...], wd_ref[...])
        f_ref[...] = f
        y = h_ref[...] + MACARON * (f * _rsqrt_mean_sq(f) * gp_ref[...])
        if final:
            err = y - t_ref[...]
            o_ref[...] = err * (1.0 / D_MODEL)
            loss_ref[...] += 0.5 * jnp.sum(err * err) * (1.0 / D_MODEL)
        else:
            o_ref[...] = y

    row = _ROW(tm)
    in_specs = [pl.BlockSpec((tm, D_FF), lambda i: (i, 0)), ANY, _VEC, row]
    out_specs = [row, row]
    out_shape = [jax.ShapeDtypeStruct((SEQ, D_MODEL), F32)] * 2
    args = [a, w_down, g_post, h_in]
    if final:
        in_specs.append(row)
        args.append(target)
        out_specs.append(pl.BlockSpec((8, 128), lambda i: (0, 0)))
        out_shape.append(jax.ShapeDtypeStruct((8, 128), F32))
    return _call(body, name=name, grid=(SEQ // tm,), in_specs=in_specs, out_specs=out_specs,
                 out_shape=out_shape, scratch_shapes=_weight_scratch(D_FF), args=args, stages=stages)


def ffn_bwd_a(d_out, f, g_post, w_down, g, u, name, stages=()):
    tm = MM_ROWS
    tc = 256

    def body(do_ref, f_ref, gp_ref, wf_ref, g_ref, u_ref, df_ref, dgu_ref, dgp_ref, wd_ref, sem):
        @pl.when(pl.program_id(0) == 0)
        def _():
            _load_weight(wf_ref, wd_ref, sem)
            dgp_ref[...] = jnp.zeros_like(dgp_ref)

        fv = f_ref[...]
        rf = _rsqrt_mean_sq(fv)
        fh = fv * rf
        dn = MACARON * do_ref[...]
        dgp_ref[...] += jnp.sum(dn * fh, axis=0, keepdims=True)
        t = dn * gp_ref[...]
        df = (rf * (t - fh * jnp.mean(t * fh, axis=-1, keepdims=True))).astype(BF16)
        df_ref[...] = df
        for c0 in range(0, D_FF, tc):
            da = _dot_nt(df, wd_ref[c0:c0 + tc, :])
            gv = g_ref[:, c0:c0 + tc].astype(F32)
            uv = u_ref[:, c0:c0 + tc].astype(F32)
            s = _sigmoid(gv)
            dgu_ref[:, c0:c0 + tc] = (da * uv * s * (1.0 + gv * (1.0 - s))).astype(BF16)
            dgu_ref[:, D_FF + c0:D_FF + c0 + tc] = (da * gv * s).astype(BF16)

    row = _ROW(tm)
    wide = pl.BlockSpec((tm, D_FF), lambda i: (i, 0))
    return _call(
        body, name=name, grid=(SEQ // tm,),
        in_specs=[row, row, _VEC, ANY, wide, wide],
        out_specs=[row, pl.BlockSpec((tm, 2 * D_FF), lambda i: (i, 0)), _VEC],
        out_shape=[jax.ShapeDtypeStruct((SEQ, D_MODEL), BF16), jax.ShapeDtypeStruct((SEQ, 2 * D_FF), BF16),
                   jax.ShapeDtypeStruct((1, D_MODEL), F32)],
        scratch_shapes=_weight_scratch(D_FF),
        args=[d_out, f, g_post, w_down, g, u], stages=stages)


def norm_bwd(pieces, w_t, x, g_pre, d_res, name, stages=()):
    tm = MM_ROWS
    widths = [p.shape[1] for p in pieces]
    offs = [sum(widths[:k]) for k in range(len(widths))]
    n_p = len(pieces)
    n_w = len(w_t)

    def body(*refs):
        p_refs = refs[:n_p]
        w_refs = refs[n_p:n_p + n_w]
        x_ref, g_ref, r_ref, dx_ref, dg_ref, wt_ref, sem = refs[n_p + n_w:]

        @pl.when(pl.program_id(0) == 0)
        def _():
            _load_weight(w_refs, wt_ref, sem)
            dg_ref[...] = jnp.zeros_like(dg_ref)

        dn = None
        for p_ref, lo, wd in zip(p_refs, offs, widths):
            part = _dot(p_ref[...], wt_ref[lo:lo + wd, :])
            dn = part if dn is None else dn + part
        xv = x_ref[...]
        r = _rsqrt_mean_sq(xv)
        xh = xv * r
        dg_ref[...] += jnp.sum(dn * xh, axis=0, keepdims=True)
        t = dn * g_ref[...]
        dx_ref[...] = r_ref[...] + r * (t - xh * jnp.mean(t * xh, axis=-1, keepdims=True))

    row = _ROW(tm)
    return _call(
        body, name=name, grid=(SEQ // tm,),
        in_specs=[pl.BlockSpec((tm, wd), lambda i: (i, 0)) for wd in widths] + [ANY] * n_w + [row, _VEC, row],
        out_specs=[row, _VEC],
        out_shape=[jax.ShapeDtypeStruct((SEQ, D_MODEL), F32), jax.ShapeDtypeStruct((1, D_MODEL), F32)],
        scratch_shapes=_weight_scratch(sum(widths), n_w),
        args=[*pieces, *w_t, x, g_pre, d_res], stages=stages)


def mm_tn(pieces, b, tm, name, stages=()):
    widths = [p.shape[1] for p in pieces]
    m_total = sum(widths)
    n_p = len(pieces)
    starts = [sum(widths[:k]) // tm for k in range(n_p)]
    counts = [wd // tm for wd in widths]

    def body(*refs):
        p_refs = refs[:n_p]
        b_ref, o_ref = refs[n_p:]
        i = pl.program_id(0)
        for p_ref, st, ct in zip(p_refs, starts, counts):
            @pl.when((i >= st) & (i < st + ct))
            def _(p_ref=p_ref):
                o_ref[...] = _dot_tn(p_ref[...], b_ref[...]).astype(BF16)

    def piece_spec(st, ct):
        return pl.BlockSpec((SEQ, tm), lambda i: (0, jnp.clip(i - st, 0, ct - 1)))

    (out,), stage_out = _call(
        body, name=name, grid=(m_total // tm,),
        in_specs=[piece_spec(st, ct) for st, ct in zip(starts, counts)] + [pl.BlockSpec((SEQ, D_MODEL), lambda i: (0, 0))],
        out_specs=[pl.BlockSpec((tm, D_MODEL), lambda i: (i, 0))],
        out_shape=[jax.ShapeDtypeStruct((m_total, D_MODEL), BF16)],
        args=[*pieces, b], stages=stages)
    return out, stage_out


def mix_in(h, g_pre, w_in_t, name, stages=()):
    tm = MM_ROWS
    offs = [sum(IN_SEGS[:k]) for k in range(len(IN_SEGS))]
    dts = [F32, F32, F32, F32, BF16, F32, F32]
    n_o = len(IN_SEGS)

    def body(*refs):
        h_ref, g_ref, wf_ref, um_ref = refs[:4]
        o_refs = refs[4:4 + n_o]
        wt_ref, sem = refs[4 + n_o:]

        @pl.when(pl.program_id(0) == 0)
        def _():
            _load_weight(wf_ref, wt_ref, sem)

        hv = h_ref[...]
        um = (hv * _rsqrt_mean_sq(hv) * g_ref[...]).astype(BF16)
        um_ref[...] = um
        for o_ref, lo, wd in zip(o_refs, offs, IN_SEGS):
            for c0 in range(0, wd, 256):
                o_ref[:, c0:c0 + 256] = _dot_nt(um, wt_ref[lo + c0:lo + c0 + 256, :]).astype(o_ref.dtype)

    return _call(
        body, name=name, grid=(SEQ // tm,),
        in_specs=[_ROW(tm), _VEC, ANY],
        out_specs=[_ROW(tm)] + [pl.BlockSpec((tm, wd), lambda i: (i, 0)) for wd in IN_SEGS],
        out_shape=[jax.ShapeDtypeStruct((SEQ, D_MODEL), BF16)]
        + [jax.ShapeDtypeStruct((SEQ, wd), dt) for wd, dt in zip(IN_SEGS, dts)],
        scratch_shapes=_weight_scratch(IN_W),
        args=[h, g_pre, w_in_t], stages=stages)


LRU_TC = 256


def _conv_fwd(xb, cw, cb, tt):
    xc = xb * cw[3:4, :] + cb
    shifted = []
    for s in (1, 2, 3):
        sh = jnp.where(tt >= s, pltpu.roll(xb, s, 0), 0.0)
        shifted.append(sh)
        xc = xc + sh * cw[3 - s:4 - s, :]
    return xc, shifted


def _lru_gates(xc, wa, ba, wx, bx, lam):
    xcb = xc.astype(BF16)
    r = _sigmoid(_dot(xcb, wa) + ba)
    i = _sigmoid(_dot(xcb, wx) + bx)
    nl = -lam
    sp = jnp.maximum(nl, 0.0) + jnp.log1p(jnp.exp(-jnp.abs(nl)))
    la = (-LRU_C * r) * sp
    a = jnp.exp(la)
    mult = jnp.sqrt(jnp.maximum(-_expm1(2.0 * la), 0.0))
    return xcb, r, i, sp, a, mult


def _scan(a, b, tt, reverse):
    n = a.shape[0]
    s = 1
    while s < n:
        more = 2 * s < n
        if s < 8:
            if reverse:
                keep = tt < n - s
                shift = n - s
            else:
                keep = tt >= s
                shift = s
            b = a * jnp.where(keep, pltpu.roll(b, shift, 0), 0.0) + b
            if more:
                a = a * jnp.where(keep, pltpu.roll(a, shift, 0), 1.0)
        elif reverse:
            b = jnp.concatenate([a[:n - s] * b[s:] + b[:n - s], b[n - s:]], axis=0)
            if more:
                a = jnp.concatenate([a[:n - s] * a[s:], a[n - s:]], axis=0)
        else:
            b = jnp.concatenate([b[:s], a[s:] * b[:n - s] + b[s:]], axis=0)
            if more:
                a = jnp.concatenate([a[:s], a[s:] * a[:n - s]], axis=0)
        s *= 2
    return b


def _lru_specs():
    col = pl.BlockSpec((SEQ, LRU_TC), lambda j: (0, j))
    vec = pl.BlockSpec((1, LRU_TC), lambda j: (0, j))
    bd = pl.BlockSpec((1, LRU_TC, LRU_TC), lambda j: (j, 0, 0))
    cw = pl.BlockSpec((4, LRU_TC), lambda j: (0, j))
    return col, vec, bd, cw


def lru_fwd(gate, xbr, conv_w, conv_b, wa_bd, b_a, wx_bd, b_x, lam, name, stages=()):
    col, vec, bd, cw = _lru_specs()

    def body(gate_ref, xbr_ref, cw_ref, cb_ref, wa_ref, ba_ref, wx_ref, bx_ref, lam_ref, y_ref, h_ref):
        tt = lax.broadcasted_iota(jnp.int32, (SEQ, LRU_TC), 0)
        xc, _ = _conv_fwd(xbr_ref[...], cw_ref[...], cb_ref[...], tt)
        _, r, i, sp, a, mult = _lru_gates(xc, wa_ref[0], ba_ref[...], wx_ref[0], bx_ref[...], lam_ref[...])
        h = _scan(a, mult * (i * xc), tt, reverse=False)
        h_ref[...] = h
        gl, _ = _gelu(gate_ref[...])
        y_ref[...] = (h * gl).astype(BF16)

    return _call(
        body, name=name, grid=(LRU_W // LRU_TC,),
        in_specs=[col, col, cw, vec, bd, vec, bd, vec, vec],
        out_specs=[col, col],
        out_shape=[jax.ShapeDtypeStruct((SEQ, LRU_W), BF16), jax.ShapeDtypeStruct((SEQ, LRU_W), F32)],
        args=[gate, xbr, conv_w, conv_b, wa_bd, b_a, wx_bd, b_x, lam], stages=stages)


def lru_bwd(gate, xbr, h, dy, conv_w, conv_b, wa_bd, b_a, wx_bd, b_x, lam, name, stages=()):
    col, vec, bd, cw = _lru_specs()

    def body(gate_ref, xbr_ref, h_ref, dy_ref, cw_ref, cb_ref, wa_ref, ba_ref, wx_ref, bx_ref, lam_ref,
             dgate_ref, dxbr_ref, vecs_ref, dwa_ref, dwx_ref):
        tt = lax.broadcasted_iota(jnp.int32, (SEQ, LRU_TC), 0)
        cwv = cw_ref[...]
        lam = lam_ref[...]
        xb = xbr_ref[...]
        xc, shifted = _conv_fwd(xb, cwv, cb_ref[...], tt)
        wa = wa_ref[0]
        wx = wx_ref[0]
        xcb, r, i, sp, a, mult = _lru_gates(xc, wa, ba_ref[...], wx, bx_ref[...], lam)
        hv = h_ref[...]
        dyv = dy_ref[...]
        gv = gate_ref[...]
        gl, th = _gelu(gv)
        dgate_ref[...] = (dyv * hv * _gelu_grad(gv, th)).astype(BF16)
        a_next = jnp.where(tt < SEQ - 1, pltpu.roll(a, SEQ - 1, 0), 0.0)
        gsum = _scan(a_next, dyv * gl, tt, reverse=True)
        h_prev = jnp.where(tt >= 1, pltpu.roll(hv, 1, 0), 0.0)
        d_mult = gsum * i * xc
        d_i = gsum * mult * xc
        d_xc = gsum * mult * i
        d_la = gsum * h_prev * a - d_mult * (a * a) / mult
        d_pr = (d_la * (-LRU_C * sp)) * r * (1.0 - r)
        d_pi = d_i * i * (1.0 - i)
        d_lam = jnp.sum(d_la * r, axis=0, keepdims=True) * (LRU_C * _sigmoid(-lam))
        d_prb = d_pr.astype(BF16)
        d_pib = d_pi.astype(BF16)
        d_xc = d_xc + _dot_nt(d_prb, wa) + _dot_nt(d_pib, wx)
        dwa_ref[0] = _dot_tn(xcb, d_prb)
        dwx_ref[0] = _dot_tn(xcb, d_pib)
        rows = [jnp.sum(d_xc * shifted[2], axis=0, keepdims=True),
                jnp.sum(d_xc * shifted[1], axis=0, keepdims=True),
                jnp.sum(d_xc * shifted[0], axis=0, keepdims=True),
                jnp.sum(d_xc * xb, axis=0, keepdims=True),
                jnp.sum(d_xc, axis=0, keepdims=True),
                jnp.sum(d_pr, axis=0, keepdims=True),
                jnp.sum(d_pi, axis=0, keepdims=True),
                d_lam]
        ri = lax.broadcasted_iota(jnp.int32, (8, LRU_TC), 0)
        acc = jnp.zeros((8, LRU_TC), F32)
        for k, rv in enumerate(rows):
            acc = jnp.where(ri == k, rv, acc)
        vecs_ref[...] = acc
        d_xb = d_xc * cwv[3:4, :]
        for s in (1, 2, 3):
            d_xb = d_xb + jnp.where(tt < SEQ - s, pltpu.roll(d_xc, SEQ - s, 0), 0.0) * cwv[3 - s:4 - s, :]
        dxbr_ref[...] = d_xb.astype(BF16)

    return _call(
        body, name=name, grid=(LRU_W // LRU_TC,),
        in_specs=[col, col, col, col, cw, vec, bd, vec, bd, vec, vec],
        out_specs=[col, col, pl.BlockSpec((8, LRU_TC), lambda j: (0, j)), bd, bd],
        out_shape=[jax.ShapeDtypeStruct((SEQ, LRU_W), BF16), jax.ShapeDtypeStruct((SEQ, LRU_W), BF16),
                   jax.ShapeDtypeStruct((8, LRU_W), F32),
                   jax.ShapeDtypeStruct((LRU_W // LRU_TC, LRU_TC, LRU_TC), F32),
                   jax.ShapeDtypeStruct((LRU_W // LRU_TC, LRU_TC, LRU_TC), F32)],
        args=[gate, xbr, h, dy, conv_w, conv_b, wa_bd, b_a, wx_bd, b_x, lam], stages=stages)


def _rope(x, cos, sin_signed):
    w = x.shape[1]
    reps = w // 128
    if reps > 1:
        cos = jnp.tile(cos, (1, reps))
        sin_signed = jnp.tile(sin_signed, (1, reps))
    lane = lax.broadcasted_iota(jnp.int32, x.shape, 1)
    first = (lane & 63) < 32
    partner = jnp.where(first, pltpu.roll(x, w - 32, 1), pltpu.roll(x, 32, 1))
    return x * cos + partner * sin_signed


def _both_halves(t, odd):
    lo = lax.broadcasted_iota(jnp.int32, t.shape, 1) < 64
    rolled = pltpu.roll(t, 64, 1)
    return jnp.where(lo, rolled, t) if odd else jnp.where(lo, t, rolled)


def _stack_heads(ta, tb):
    lo = lax.broadcasted_iota(jnp.int32, ta.shape, 1) < 64
    return jnp.concatenate([jnp.where(lo, ta, 0.0), jnp.where(lo, 0.0, ta),
                            jnp.where(lo, tb, 0.0), jnp.where(lo, 0.0, tb)], axis=0)


def _unstack_heads(o):
    lo = lax.broadcasted_iota(jnp.int32, (ATTN_BLOCK, 128), 1) < 64
    return (jnp.where(lo, o[0:128], o[128:256]), jnp.where(lo, o[256:384], o[384:512]))


def _attn_probs(qs, kd, sinks_ref, hk, first_block):
    s = _dot_nt(qs, kd) * (HEAD_DIM ** -0.5)
    row = lax.broadcasted_iota(jnp.int32, s.shape, 0)
    si = lax.broadcasted_iota(jnp.int32, s.shape, 1)
    diff = ATTN_BLOCK + (row & (ATTN_BLOCK - 1)) - si
    valid = (diff >= 0) & (diff < ATTN_BLOCK) & ((si >= ATTN_BLOCK) | jnp.logical_not(first_block))
    s = jnp.where(valid, s, MASK_VALUE)
    rg = lax.broadcasted_iota(jnp.int32, (4 * ATTN_BLOCK, 1), 0) >> 7
    sink = jnp.where(rg == 0, sinks_ref[4 * hk],
                     jnp.where(rg == 1, sinks_ref[4 * hk + 1],
                               jnp.where(rg == 2, sinks_ref[4 * hk + 2], sinks_ref[4 * hk + 3])))
    m = jnp.maximum(jnp.max(s, axis=1, keepdims=True), sink)
    e = jnp.exp(s - m)
    es = jnp.exp(sink - m)
    inv = 1.0 / (jnp.sum(e, axis=1, keepdims=True) + es)
    return e * inv, es * inv


def _prev(i):
    return jnp.maximum(i - 1, 0)


def attn_fwd(q, k, v, cos, sin_signed, sinks, name, stages=()):
    nb = ATTN_BLOCK

    def body(q_ref, kc_ref, kp_ref, vc_ref, vp_ref, cc_ref, sc_ref, cp_ref, sp_ref, sinks_ref,
             qr_ref, kr_ref, y_ref):
        first_block = pl.program_id(0) == 0
        qr = _rope(q_ref[...], cc_ref[...], sc_ref[...])
        kc = _rope(kc_ref[...], cc_ref[...], sc_ref[...])
        kp = _rope(kp_ref[...], cp_ref[...], sp_ref[...])
        qr_ref[...] = qr.astype(BF16)
        kr_ref[...] = kc.astype(BF16)
        k2 = jnp.concatenate([kp, kc], axis=0)
        v2 = jnp.concatenate([vp_ref[...].astype(F32), vc_ref[...].astype(F32)], axis=0)
        for hk in range(N_KV_HEADS):
            kt = hk // 2
            kd = _both_halves(k2[:, kt * 128:(kt + 1) * 128], hk % 2).astype(BF16)
            vd = _both_halves(v2[:, kt * 128:(kt + 1) * 128], hk % 2).astype(BF16)
            qs = _stack_heads(qr[:, (2 * hk) * 128:(2 * hk + 1) * 128],
                              qr[:, (2 * hk + 1) * 128:(2 * hk + 2) * 128]).astype(BF16)
            p, _ = _attn_probs(qs, kd, sinks_ref, hk, first_block)
            ta, tb = _unstack_heads(_dot(p.astype(BF16), vd))
            y_ref[:, (2 * hk) * 128:(2 * hk + 1) * 128] = ta.astype(BF16)
            y_ref[:, (2 * hk + 1) * 128:(2 * hk + 2) * 128] = tb.astype(BF16)

    cur = lambda w: pl.BlockSpec((nb, w), lambda i: (i, 0))
    prv = lambda w: pl.BlockSpec((nb, w), lambda i: (_prev(i), 0))
    return _call(
        body, name=name, grid=(N_ATTN_BLOCKS,),
        in_specs=[cur(D_MODEL), cur(KV_W), prv(KV_W), cur(KV_W), prv(KV_W), cur(128), cur(128), prv(128), prv(128),
                  pl.BlockSpec(memory_space=pltpu.SMEM)],
        out_specs=[cur(D_MODEL), cur(KV_W), cur(D_MODEL)],
        out_shape=[jax.ShapeDtypeStruct((SEQ, D_MODEL), BF16), jax.ShapeDtypeStruct((SEQ, KV_W), BF16),
                   jax.ShapeDtypeStruct((SEQ, D_MODEL), BF16)],
        args=[q, k, k, v, v, cos, sin_signed, cos, sin_signed, sinks], stages=stages)


def attn_bwd(qr, kr, v, dy, cos, sin_signed, sinks, name, stages=()):
    nb = ATTN_BLOCK
    n_steps = N_ATTN_BLOCKS + 1
    scale = HEAD_DIM ** -0.5

    def body(q_ref, kc_ref, kp_ref, vc_ref, vp_ref, dy_ref, cc_ref, sc_ref, cp_ref, sp_ref, sinks_ref,
             dq_ref, dkv_ref, dsk_ref, ck_ref, cv_ref):
        dk_ref = dkv_ref.at[:, pl.ds(0, KV_W)]
        dv_ref = dkv_ref.at[:, pl.ds(KV_W, KV_W)]
        i = pl.program_id(0)

        @pl.when(i == 0)
        def _():
            dsk_ref[...] = jnp.zeros_like(dsk_ref)
            ck_ref[...] = jnp.zeros_like(ck_ref)
            cv_ref[...] = jnp.zeros_like(cv_ref)

        @pl.when(i < N_ATTN_BLOCKS)
        def _():
            qv = q_ref[...].astype(F32)
            dov = dy_ref[...].astype(F32)
            k2 = jnp.concatenate([kp_ref[...].astype(F32), kc_ref[...].astype(F32)], axis=0)
            v2 = jnp.concatenate([vp_ref[...].astype(F32), vc_ref[...].astype(F32)], axis=0)
            lane = lax.broadcasted_iota(jnp.int32, (8, 128), 1)
            lo = lax.broadcasted_iota(jnp.int32, (2 * nb, 128), 1) < 64
            dsk = jnp.zeros((8, 128), F32)
            dk_tiles = []
            dv_tiles = []
            for hk in range(N_KV_HEADS):
                kt = hk // 2
                kd = _both_halves(k2[:, kt * 128:(kt + 1) * 128], hk % 2).astype(BF16)
                vd = _both_halves(v2[:, kt * 128:(kt + 1) * 128], hk % 2).astype(BF16)
                qs = _stack_heads(qv[:, (2 * hk) * 128:(2 * hk + 1) * 128],
                                  qv[:, (2 * hk + 1) * 128:(2 * hk + 2) * 128]).astype(BF16)
                dos = _stack_heads(dov[:, (2 * hk) * 128:(2 * hk + 1) * 128],
                                   dov[:, (2 * hk + 1) * 128:(2 * hk + 2) * 128]).astype(BF16)
                p, ps = _attn_probs(qs, kd, sinks_ref, hk, i == 0)
                dp = _dot_nt(dos, vd)
                delta = jnp.sum(p * dp, axis=1, keepdims=True)
                ds = (p * (dp - delta)).astype(BF16)
                dsink = -ps * delta
                for g in range(4):
                    dsk = dsk + jnp.where(lane == 4 * hk + g, jnp.sum(dsink[g * nb:(g + 1) * nb]), 0.0)
                ta, tb = _unstack_heads(_dot(ds, kd) * scale)
                dq_a = (2 * hk) * 128
                dq_ref[:, dq_a:dq_a + 128] = _rope(ta, cc_ref[...], -sc_ref[...]).astype(BF16)
                dq_ref[:, dq_a + 128:dq_a + 256] = _rope(tb, cc_ref[...], -sc_ref[...]).astype(BF16)
                rk = _dot_tn(ds, qs) * scale
                rv = _dot_tn(p.astype(BF16), dos)
                dk_tiles.append(rk + pltpu.roll(rk, 64, 1))
                dv_tiles.append(rv + pltpu.roll(rv, 64, 1))
            dsk_ref[...] += dsk
            dk_full = jnp.concatenate([jnp.where(lo, dk_tiles[0], dk_tiles[1]),
                                       jnp.where(lo, dk_tiles[2], dk_tiles[3])], axis=1)
            dv_full = jnp.concatenate([jnp.where(lo, dv_tiles[0], dv_tiles[1]),
                                       jnp.where(lo, dv_tiles[2], dv_tiles[3])], axis=1)
            dk_ref[...] = _rope(ck_ref[...] + dk_full[0:nb], cp_ref[...], -sp_ref[...]).astype(BF16)
            dv_ref[...] = (cv_ref[...] + dv_full[0:nb]).astype(BF16)
            ck_ref[...] = dk_full[nb:2 * nb]
            cv_ref[...] = dv_full[nb:2 * nb]

        @pl.when(i == N_ATTN_BLOCKS)
        def _():
            dk_ref[...] = _rope(ck_ref[...], cp_ref[...], -sp_ref[...]).astype(BF16)
            dv_ref[...] = cv_ref[...].astype(BF16)

    qi = lambda i: jnp.minimum(i, N_ATTN_BLOCKS - 1)
    cur = lambda w: pl.BlockSpec((nb, w), lambda i: (qi(i), 0))
    prv = lambda w: pl.BlockSpec((nb, w), lambda i: (_prev(qi(i)), 0))
    out_prev = lambda w: pl.BlockSpec((nb, w), lambda i: (_prev(i), 0))
    return _call(
        body, name=name, grid=(n_steps,),
        in_specs=[cur(D_MODEL), cur(KV_W), prv(KV_W), cur(KV_W), prv(KV_W), cur(D_MODEL),
                  cur(128), cur(128), out_prev(128), out_prev(128), pl.BlockSpec(memory_space=pltpu.SMEM)],
        out_specs=[cur(D_MODEL), out_prev(2 * KV_W), pl.BlockSpec((8, 128), lambda i: (0, 0))],
        out_shape=[jax.ShapeDtypeStruct((SEQ, D_MODEL), BF16), jax.ShapeDtypeStruct((SEQ, 2 * KV_W), BF16),
                   jax.ShapeDtypeStruct((8, 128), F32)],
        scratch_shapes=[pltpu.VMEM((nb, KV_W), F32), pltpu.VMEM((nb, KV_W), F32)],
        args=[qr, kr, kr, v, v, dy, cos, sin_signed, cos, sin_signed, sinks], stages=stages)


def _proj_scratch():
    return [pltpu.VMEM((D_MODEL, D_MODEL), BF16)] * 3 + [pltpu.SemaphoreType.DMA((3 * N_CHIPS,))]


def _load_projs(w_refs, wl_ref, wa_ref, wo_ref, sem):
    for k, (w_ref, dst) in enumerate(zip(w_refs, (wl_ref, wa_ref, wo_ref))):
        _load_weight(w_ref, dst, sem.at[pl.ds(k * N_CHIPS, N_CHIPS)])


def merge_fwd(y_lru, y_attn, g_lru, g_attn, projs, g_post, h_in, name, stages=()):
    tm = MM_ROWS

    def body(yl_ref, ya_ref, gl_ref, ga_ref, w1_ref, w2_ref, w3_ref, gp_ref, h_ref,
             pl_ref, pa_ref, mg_ref, m_ref, o_ref, wl_ref, wa_ref, wo_ref, sem):
        @pl.when(pl.program_id(0) == 0)
        def _():
            _load_projs((w1_ref, w2_ref, w3_ref), wl_ref, wa_ref, wo_ref, sem)

        p_l = _dot(yl_ref[...], wl_ref[...])
        p_a = _dot(ya_ref[...], wa_ref[...])
        pl_ref[...] = p_l.astype(BF16)
        pa_ref[...] = p_a.astype(BF16)
        merged = (_sigmoid(gl_ref[...]) * p_l + _sigmoid(ga_ref[...]) * p_a).astype(BF16)
        mg_ref[...] = merged
        m = _dot(merged, wo_ref[...])
        m_ref[...] = m
        o_ref[...] = h_ref[...] + m * _rsqrt_mean_sq(m) * gp_ref[...]

    row = _ROW(tm)
    return _call(
        body, name=name, grid=(SEQ // tm,),
        in_specs=[row, row, row, row, ANY, ANY, ANY, _VEC, row],
        out_specs=[row] * 5,
        out_shape=[jax.ShapeDtypeStruct((SEQ, D_MODEL), BF16)] * 3 + [jax.ShapeDtypeStruct((SEQ, D_MODEL), F32)] * 2,
        scratch_shapes=_proj_scratch(),
        args=[y_lru, y_attn, g_lru, g_attn, *projs, g_post, h_in], stages=stages)


def merge_bwd(d_out, m, g_post, projs, g_lru, g_attn, p_l, p_a, name, stages=()):
    tm = 256

    def body(do_ref, m_ref, gp_ref, w1_ref, w2_ref, w3_ref, gl_ref, ga_ref, pl_ref, pa_ref,
             dm_ref, dpl_ref, dpa_ref, dgl_ref, dga_ref, dya_ref, dyl_ref, dgp_ref, wl_ref, wa_ref, wo_ref, sem):
        @pl.when(pl.program_id(0) == 0)
        def _():
            _load_projs((w1_ref, w2_ref, w3_ref), wl_ref, wa_ref, wo_ref, sem)
            dgp_ref[...] = jnp.zeros_like(dgp_ref)

        mv = m_ref[...]
        rm = _rsqrt_mean_sq(mv)
        mh = mv * rm
        dn = do_ref[...]
        dgp_ref[...] += jnp.sum(dn * mh, axis=0, keepdims=True)
        t = dn * gp_ref[...]
        dm = (rm * (t - mh * jnp.mean(t * mh, axis=-1, keepdims=True))).astype(BF16)
        dm_ref[...] = dm
        dmg = _dot_nt(dm, wo_ref[...])
        sl = _sigmoid(gl_ref[...])
        sa = _sigmoid(ga_ref[...])
        dpl = (dmg * sl).astype(BF16)
        dpa = (dmg * sa).astype(BF16)
        dpl_ref[...] = dpl
        dpa_ref[...] = dpa
        dgl_ref[...] = (dmg * pl_ref[...].astype(F32) * sl * (1.0 - sl)).astype(BF16)
        dga_ref[...] = (dmg * pa_ref[...].astype(F32) * sa * (1.0 - sa)).astype(BF16)
        dyl_ref[...] = _dot_nt(dpl, wl_ref[...])
        dya_ref[...] = _dot_nt(dpa, wa_ref[...]).astype(BF16)

    row = _ROW(tm)
    return _call(
        body, name=name, grid=(SEQ // tm,),
        in_specs=[row, row, _VEC, ANY, ANY, ANY, row, row, row, row],
        out_specs=[row] * 7 + [_VEC],
        out_shape=[jax.ShapeDtypeStruct((SEQ, D_MODEL), BF16)] * 6 + [jax.ShapeDtypeStruct((SEQ, D_MODEL), F32),
                                                                       jax.ShapeDtypeStruct((1, D_MODEL), F32)],
        scratch_shapes=_proj_scratch(),
        args=[d_out, m, g_post, *projs, g_lru, g_attn, p_l, p_a], stages=stages)


def _rope_tables():
    half = HEAD_DIM // 2
    inv_freq = ROPE_THETA ** (-jnp.arange(half, dtype=F32) / half)
    ang = jnp.arange(SEQ, dtype=F32)[:, None] * inv_freq[None, :]
    cos, sin = jnp.cos(ang), jnp.sin(ang)
    return jnp.tile(jnp.concatenate([cos, cos], axis=1), (1, 2)), jnp.tile(jnp.concatenate([-sin, sin], axis=1), (1, 2))


def _block_diag(w):
    per = LRU_TC // LRU_BLOCK_W
    w4 = w.reshape(LRU_W // LRU_TC, per, LRU_BLOCK_W, LRU_BLOCK_W)
    eye = jnp.eye(per, dtype=w.dtype)
    return jnp.einsum('jacd,ab->jacbd', w4, eye).reshape(LRU_W // LRU_TC, LRU_TC, LRU_TC).astype(BF16)


def _diag_blocks(p):
    per = LRU_TC // LRU_BLOCK_W
    p5 = p.reshape(LRU_W // LRU_TC, per, LRU_BLOCK_W, per, LRU_BLOCK_W)
    return jnp.stack([p5[:, a, :, a, :] for a in range(per)], axis=1).reshape(LRU_W // LRU_BLOCK_W, LRU_BLOCK_W, LRU_BLOCK_W)


def _place():
    x, y, c = lax.axis_index('x'), lax.axis_index('y'), lax.axis_index('c')
    chips = [(1 - x, y), (x, 1 - y), (1 - x, 1 - y)]
    return x, y, c, chips


def _rcopy(src, dst, send_sem, recv_sem, to):
    return pltpu.make_async_remote_copy(src_ref=src, dst_ref=dst, send_sem=send_sem, recv_sem=recv_sem,
                                        device_id=to, device_id_type=MESH)


class _Stage:
    inputs, out_shape, scratch = (), (), ()

    def start(self, ins, outs, scr):
        plan = self._plan(ins, outs, scr)
        for ld in plan['loads']:
            ld.start()
        for cp in plan['sends']:
            cp.start()

    def mid(self, ins, outs, scr):
        plan = self._plan(ins, outs, scr)
        for ld, st in zip(plan['loads'], plan['stores']):
            ld.wait()
            st.start()
        for arrived, onward in zip(plan['arrivals'], plan['forwards']):
            arrived.wait_recv()
            onward.start()

    def end(self, ins, outs, scr):
        plan = self._plan(ins, outs, scr)
        for st in plan['stores']:
            st.wait()
        for arrived in (plan['final_arrivals'] if plan['forwards'] else plan['arrivals']):
            arrived.wait_recv()
        for cp in plan['sends'] + plan['forwards']:
            cp.wait_send()


def _empty_plan():
    return dict(loads=[], stores=[], sends=[], arrivals=[], forwards=[], final_arrivals=[])


class GatherStage(_Stage):
    SUB = 2

    def __init__(self, items):
        self.ranges = [(off, rows) for _, off, rows in items]
        self.inputs = [src for src, _, _ in items]
        self.out_shape = [jax.ShapeDtypeStruct((N_CHIPS, rows, D_MODEL), BF16) for _, rows in self.ranges]
        self.n_ici = 3 * self.SUB * len(items)
        self.scratch = [pltpu.VMEM((sum(r for _, r in self.ranges), D_MODEL), BF16), pltpu.SemaphoreType.DMA((2 * self.n_ici,)),
                        pltpu.SemaphoreType.DMA((2 * self.n_ici,)), pltpu.SemaphoreType.DMA((2 * len(items),))]

    def _plan(self, ins, outs, scr):
        buf, send, recv, lsem = scr
        x, y, c, chips = _place()
        me_q = 2 * x + y
        sib = (x, y, 1 - c)
        plan = _empty_plan()
        boff = 0
        for w, ((off, rows), p_ref, o_ref) in enumerate(zip(self.ranges, ins, outs)):
            hr = rows // 2
            ch = hr // self.SUB
            plan['loads'].append(pltpu.make_async_copy(p_ref.at[pl.ds(off, rows)], buf.at[pl.ds(boff, rows)], lsem.at[2 * w]))
            plan['stores'].append(pltpu.make_async_copy(buf.at[pl.ds(boff, rows)], o_ref.at[me_q], lsem.at[2 * w + 1]))
            boff += rows
            for k in range(self.SUB):
                mine = pl.ds(pl.multiple_of(c * hr + k * ch, 16), ch)
                theirs = pl.ds(pl.multiple_of((1 - c) * hr + k * ch, 16), ch)
                src = p_ref.at[pl.ds(pl.multiple_of(off + c * hr + k * ch, 16), ch)]
                for j, (cx, cy) in enumerate(chips):
                    i = (w * self.SUB + k) * 3 + j
                    got = o_ref.at[2 * cx + cy, mine]
                    got_sib = o_ref.at[2 * cx + cy, theirs]
                    plan['sends'].append(_rcopy(src, o_ref.at[me_q, mine], send.at[i], recv.at[i], (cx, cy, c)))
                    plan['arrivals'].append(_rcopy(got, got, send.at[i], recv.at[i], (cx, cy, c)))
                    plan['forwards'].append(_rcopy(got, got, send.at[self.n_ici + i], recv.at[self.n_ici + i], sib))
                    plan['final_arrivals'].append(
                        _rcopy(got_sib, got_sib, send.at[self.n_ici + i], recv.at[self.n_ici + i], sib))
        return plan


class PairStage(_Stage):
    def __init__(self, grads):
        self.inputs = list(grads)
        self.out_shape = [jax.ShapeDtypeStruct((N_CHIPS, 1) + g.shape[2:], BF16) for g in grads]
        n_cp = N_CHIPS * len(grads)
        self.scratch = [pltpu.SemaphoreType.DMA((n_cp,)), pltpu.SemaphoreType.DMA((n_cp,))]

    def _plan(self, ins, outs, scr):
        send, recv = scr
        x, y, c, _ = _place()
        plan = _empty_plan()
        for w, (g_ref, l_ref) in enumerate(zip(ins, outs)):
            for q in range(N_CHIPS):
                i = w * N_CHIPS + q
                plan['sends'].append(_rcopy(g_ref.at[q, pl.ds(1 - c, 1)], l_ref.at[q], send.at[i], recv.at[i], (x, y, 1 - c)))
        plan['arrivals'] = plan['sends']
        return plan


class ChipStage(_Stage):
    def __init__(self, items):
        self.ranges = [(off, n) for _, off, n in items]
        self.inputs = [s for s, _, _ in items]
        self.out_shape = [jax.ShapeDtypeStruct((N_CHIPS, n, D_MODEL), BF16) for _, n in self.ranges]
        n_cp = 3 * len(items)
        self.scratch = [pltpu.VMEM((sum(n for _, n in self.ranges), D_MODEL), BF16), pltpu.SemaphoreType.DMA((n_cp,)),
                        pltpu.SemaphoreType.DMA((n_cp,)), pltpu.SemaphoreType.DMA((2 * len(items),))]

    def _plan(self, ins, outs, scr):
        buf, send, recv, lsem = scr
        x, y, c, chips = _place()
        me_q = 2 * x + y
        plan = _empty_plan()
        boff = 0
        for w, ((off, n), s_ref, l_ref) in enumerate(zip(self.ranges, ins, outs)):
            rows = pl.ds(off, n)
            plan['loads'].append(pltpu.make_async_copy(s_ref.at[me_q, rows], buf.at[pl.ds(boff, n)], lsem.at[2 * w]))
            plan['stores'].append(pltpu.make_async_copy(buf.at[pl.ds(boff, n)], l_ref.at[me_q], lsem.at[2 * w + 1]))
            boff += n
            for j, (cx, cy) in enumerate(chips):
                i = w * 3 + j
                got = l_ref.at[2 * cx + cy]
                plan['sends'].append(_rcopy(s_ref.at[2 * cx + cy, rows], l_ref.at[me_q], send.at[i], recv.at[i], (cx, cy, c)))
                plan['arrivals'].append(_rcopy(got, got, send.at[i], recv.at[i], (cx, cy, c)))
        return plan


class SwapStage(_Stage):
    def __init__(self, items):
        n = len(items)
        self.inputs = list(items)
        self.out_shape = [jax.ShapeDtypeStruct((2,) + a.shape, a.dtype) for a in items]
        self.scratch = [pltpu.VMEM(a.shape, a.dtype) for a in items] + [
            pltpu.SemaphoreType.DMA((n,)), pltpu.SemaphoreType.DMA((n,)), pltpu.SemaphoreType.DMA((2 * n,))]

    def _plan(self, ins, outs, scr):
        bufs, (send, recv, lsem) = scr[:len(ins)], scr[len(ins):]
        x, y, c, _ = _place()
        plan = _empty_plan()
        for w, (h_ref, o_ref, buf) in enumerate(zip(ins, outs, bufs)):
            plan['loads'].append(pltpu.make_async_copy(h_ref, buf, lsem.at[2 * w]))
            plan['stores'].append(pltpu.make_async_copy(buf, o_ref.at[c], lsem.at[2 * w + 1]))
            got = o_ref.at[1 - c]
            plan['sends'].append(_rcopy(h_ref, o_ref.at[c], send.at[w], recv.at[w], (x, y, 1 - c)))
            plan['arrivals'].append(_rcopy(got, got, send.at[w], recv.at[w], (x, y, 1 - c)))
        return plan


class SmallGatherStage(_Stage):
    def __init__(self, blk):
        self.inputs = [blk]
        self.out_shape = [jax.ShapeDtypeStruct((N_DEV,) + blk.shape, blk.dtype)]
        self.scratch = [pltpu.VMEM(blk.shape, blk.dtype), pltpu.SemaphoreType.DMA((7,)), pltpu.SemaphoreType.DMA((7,)),
                        pltpu.SemaphoreType.DMA((2,))]

    def _plan(self, ins, outs, scr):
        (x_ref,), (o_ref,), (buf, send, recv, lsem) = ins, outs, scr
        x, y, c, chips = _place()
        sib = (x, y, 1 - c)

        def slot(px, py, pc):
            return o_ref.at[4 * px + 2 * py + pc]

        plan = _empty_plan()
        plan['loads'].append(pltpu.make_async_copy(x_ref, buf, lsem.at[0]))
        plan['stores'].append(pltpu.make_async_copy(buf, slot(x, y, c), lsem.at[1]))
        from_sib = slot(x, y, 1 - c)
        plan['sends'].append(_rcopy(x_ref, slot(x, y, c), send.at[0], recv.at[0], sib))
        plan['final_arrivals'].append(_rcopy(from_sib, from_sib, send.at[0], recv.at[0], sib))
        for j, (cx, cy) in enumerate(chips):
            got, got_sib = slot(cx, cy, c), slot(cx, cy, 1 - c)
            plan['sends'].append(_rcopy(x_ref, slot(x, y, c), send.at[1 + j], recv.at[1 + j], (cx, cy, c)))
            plan['arrivals'].append(_rcopy(got, got, send.at[1 + j], recv.at[1 + j], (cx, cy, c)))
            plan['forwards'].append(_rcopy(got, got, send.at[4 + j], recv.at[4 + j], sib))
            plan['final_arrivals'].append(_rcopy(got_sib, got_sib, send.at[4 + j], recv.at[4 + j], sib))
        return plan


def comm_call(name, stages):
    def body():
        pass

    return _call(body, name=name, grid=(1,), in_specs=[], out_specs=[], out_shape=[], args=[], stages=stages)[1]


def pair_sum(g4, land, c_arr, name):
    hr = g4.shape[2]

    def body(c_ref, g_ref, l_ref, o_ref):
        o_ref[0] = (g_ref[0, 0].astype(F32) + l_ref[0, 0].astype(F32)).astype(BF16)

    return pl.pallas_call(
        body, name=name,
        grid_spec=pltpu.PrefetchScalarGridSpec(
            num_scalar_prefetch=1, grid=(N_CHIPS,),
            in_specs=[pl.BlockSpec((1, 1, hr, D_MODEL), lambda q, c: (q, c[0], 0, 0)),
                      pl.BlockSpec((1, 1, hr, D_MODEL), lambda q, c: (q, 0, 0, 0))],
            out_specs=pl.BlockSpec((1, hr, D_MODEL), lambda q, c: (q, 0, 0))),
        out_shape=jax.ShapeDtypeStruct((N_CHIPS, hr, D_MODEL), BF16),
        compiler_params=_params(1),
    )(c_arr, g4, land)


def small_sum(vec_parts, lru_parts):
    def body(v_ref, l_ref, o_ref):
        for p_ref, lo, n in ((v_ref, 0, ROW_WA), (l_ref, ROW_WA, SMALL_ROWS - ROW_WA)):
            acc = p_ref[0]
            for s in range(1, N_DEV):
                acc = acc + p_ref[s]
            o_ref[lo:lo + n, :] = acc

    return pl.pallas_call(
        body, name='small_sum', grid=(1,),
        in_specs=[pl.BlockSpec(vec_parts.shape, lambda i: (0, 0, 0)), pl.BlockSpec(lru_parts.shape, lambda i: (0, 0, 0))],
        out_specs=pl.BlockSpec((SMALL_ROWS, D_MODEL), lambda i: (0, 0)),
        out_shape=jax.ShapeDtypeStruct((SMALL_ROWS, D_MODEL), F32),
        compiler_params=_params(1),
    )(vec_parts, lru_parts)


def _adam_math(w, g, m, v):
    m2 = ADAM_B1 * m + (1.0 - ADAM_B1) * g
    v2 = ADAM_B2 * v + (1.0 - ADAM_B2) * (g * g)
    m_hat = m2 / (1.0 - ADAM_B1 ** ADAM_STEP)
    v_hat = v2 / (1.0 - ADAM_B2 ** ADAM_STEP)
    delta = -ADAM_LR * (m_hat / (jnp.sqrt(v_hat) + ADAM_EPS) + ADAM_WD * w)
    return delta, m2, v2


def _adam_body(n_parts, transposed):
    def body(*refs):
        g_refs = refs[:n_parts]
        w_ref, m_ref, v_ref, go_ref, d_ref, mo_ref, vo_ref = refs[n_parts:]
        def chips_added(blk):
            acc = blk[0].astype(F32)
            for s in range(1, N_CHIPS):
                acc = acc + blk[s].astype(F32)
            return acc

        if transposed:
            g = jnp.concatenate([chips_added(g_ref[h]) for h in range(2) for g_ref in g_refs], axis=0).T
        else:
            rows = [chips_added(g_ref[0]) for g_ref in g_refs]
            g = jnp.concatenate(rows, axis=0) if n_parts > 1 else rows[0]
        go_ref[...] = g
        d_ref[...], mo_ref[...], vo_ref[...] = _adam_math(w_ref[...], g, m_ref[...], v_ref[...])
    return body


def adam_rows(fulls, name, w, m, v):
    hr = w.shape[0] // 2
    blk = pl.BlockSpec((hr, D_MODEL), lambda h: (h, 0))
    return pl.pallas_call(
        _adam_body(len(fulls), False), name='adam_' + name, grid=(2,),
        in_specs=[pl.BlockSpec((1, N_CHIPS, f.shape[2], D_MODEL), lambda h: (h, 0, 0, 0)) for f in fulls] + [blk, blk, blk],
        out_specs=[blk] * 4,
        out_shape=[jax.ShapeDtypeStruct(w.shape, F32)] * 4,
        compiler_params=_params(1),
    )(*fulls, w, m, v)


def adam_cols(fulls, name, w, m, v):
    cols = w.shape[1]
    tr = 128
    blk = pl.BlockSpec((tr, cols), lambda i: (i, 0))
    return pl.pallas_call(
        _adam_body(len(fulls), True), name='adam_' + name, grid=(D_MODEL // tr,),
        in_specs=[pl.BlockSpec((2, N_CHIPS, f.shape[2], tr), lambda i: (0, 0, 0, i)) for f in fulls] + [blk, blk, blk],
        out_specs=[blk] * 4,
        out_shape=[jax.ShapeDtypeStruct(w.shape, F32)] * 4,
        compiler_params=_params(1),
    )(*fulls, w, m, v)


def adam_small(g, w, m, v):
    def body(g_ref, w_ref, m_ref, v_ref, d_ref, mo_ref, vo_ref):
        d_ref[...], mo_ref[...], vo_ref[...] = _adam_math(w_ref[...], g_ref[...], m_ref[...], v_ref[...])

    blk = pl.BlockSpec(w.shape, lambda i: (0, 0))
    return pl.pallas_call(
        body, name='adam_small', grid=(1,), in_specs=[blk] * 4, out_specs=[blk] * 3,
        out_shape=[jax.ShapeDtypeStruct(w.shape, F32)] * 3, compiler_params=_params(1),
    )(g, w, m, v)


WEIGHTS = ('ffn1_pre_g', 'ffn1_w_gu', 'ffn1_w_down', 'ffn1_post_g', 'mix_pre_g', 'w_in', 'conv_w', 'conv_b',
           'lru_w_a', 'lru_b_a', 'lru_w_x', 'lru_b_x', 'lru_lambda', 'attn_sinks', 'w_proj_lru', 'w_proj_attn',
           'w_out', 'mix_post_g', 'ffn2_pre_g', 'ffn2_w_gu', 'ffn2_w_down', 'ffn2_post_g')
SMALL = tuple(n for n in WEIGHTS if n not in PACK_OFF)


def _pack_vecs(d, conv_rows):
    sinks = jnp.pad(d['attn_sinks'].reshape(1, N_Q_HEADS), ((0, 0), (0, D_MODEL - N_Q_HEADS)))
    conv = jnp.pad(conv_rows, ((0, ROW_WA - ROW_CONV - conv_rows.shape[0]), (0, 0)))
    return jnp.concatenate([d[n].reshape(1, D_MODEL) for n in SMALL_VECS] + [sinks, conv], axis=0)


def _pack_lru(d):
    return jnp.concatenate([d['lru_w_a'].reshape(64, D_MODEL), d['lru_w_x'].reshape(64, D_MODEL)], axis=0)


def _pack_small(d, conv_rows):
    return jnp.concatenate([_pack_vecs(d, conv_rows), _pack_lru(d)], axis=0)


def _unpack_small(p, shapes):
    out = {n: p[k:k + 1].reshape(shapes[n]) for k, n in enumerate(SMALL_VECS)}
    out['attn_sinks'] = p[ROW_SINKS:ROW_SINKS + 1, :N_Q_HEADS].reshape(shapes['attn_sinks'])
    out['conv_w'] = p[ROW_CONV:ROW_CONV + 1].reshape(shapes['conv_w'])
    out['lru_w_a'] = p[ROW_WA:ROW_WA + 64].reshape(shapes['lru_w_a'])
    out['lru_w_x'] = p[ROW_WX:ROW_WX + 64].reshape(shapes['lru_w_x'])
    return out


def kernel(x, ffn1_pre_g, ffn1_w_gu, ffn1_w_down, ffn1_post_g, mix_pre_g, w_in, conv_w, conv_b, lru_w_a, lru_b_a, lru_w_x, lru_b_x, lru_lambda, attn_sinks, w_proj_lru, w_proj_attn, w_out, mix_post_g, ffn2_pre_g, ffn2_w_gu, ffn2_w_down, ffn2_post_g, loss_target, m_ffn1_pre_g, m_ffn1_w_gu, m_ffn1_w_down, m_ffn1_post_g, m_mix_pre_g, m_w_in, m_conv_w, m_conv_b, m_lru_w_a, m_lru_b_a, m_lru_w_x, m_lru_b_x, m_lru_lambda, m_attn_sinks, m_w_proj_lru, m_w_proj_attn, m_w_out, m_mix_post_g, m_ffn2_pre_g, m_ffn2_w_gu, m_ffn2_w_down, m_ffn2_post_g, v_ffn1_pre_g, v_ffn1_w_gu, v_ffn1_w_down, v_ffn1_post_g, v_mix_pre_g, v_w_in, v_conv_w, v_conv_b, v_lru_w_a, v_lru_b_a, v_lru_w_x, v_lru_b_x, v_lru_lambda, v_attn_sinks, v_w_proj_lru, v_w_proj_attn, v_w_out, v_mix_post_g, v_ffn2_pre_g, v_ffn2_w_gu, v_ffn2_w_down, v_ffn2_post_g):
    given = dict(locals())
    w = {n: given[n] for n in WEIGHTS}
    mom = {n: given['m_' + n] for n in WEIGHTS}
    var = {n: given['v_' + n] for n in WEIGHTS}
    shapes = {n: w[n].shape for n in WEIGHTS}
    xq = lax.axis_index('x')
    yq = lax.axis_index('y')
    cq = lax.axis_index('c')
    me_q = 2 * xq + yq

    c_arr = cq.reshape(1).astype(jnp.int32)
    xs, target = x[0], loss_target[0]
    sw = {n: (w[n][0] if w[n].ndim > 2 else w[n]) for n in SMALL}
    cos, sin_signed = _rope_tables()
    wa_bd = _block_diag(sw['lru_w_a'])
    wx_bd = _block_diag(sw['lru_w_x'])
    sinks = sw['attn_sinks'].reshape(N_Q_HEADS)

    shard = {n: (w[n][0].T if t else w[n][0]).astype(BF16) for n, _, t in PACK}
    conv_pad = jnp.pad(w['conv_w'][0], ((0, 4), (0, 0)))

    def whole(name):
        return (shard[name], 0, PACK_ROWS_OF[name])

    def part(name, p, n_parts=2):
        rows = PACK_ROWS_OF[name] // n_parts
        return (shard[name], p * rows, rows)

    (w_gu1,), (conv_all,) = comm_call('gather_first', [GatherStage([whole('ffn1_w_gu')]), SmallGatherStage(conv_pad)])
    sw['conv_w'] = jnp.transpose(conv_all[0::2, :4, :], (1, 0, 2)).reshape(4, LRU_W)
    proj_names = ['w_proj_lru', 'w_proj_attn', 'w_out']

    (n1, g1, u1, a1), ((w_down1,),) = ffn_fwd_a(xs, sw['ffn1_pre_g'], [w_gu1], 'ffn1_fwd_a',
                                                 stages=[GatherStage([whole('ffn1_w_down')])])
    (f1, h1), ((w_in_t,),) = ffn_fwd_b(a1, w_down1, sw['ffn1_post_g'], xs, 'ffn1_fwd_b', stages=[GatherStage([whole('w_in')])])
    (um, gate, xbr, q, k, v, g_lru, g_attn), ((w_gu2a,),) = mix_in(h1, sw['mix_pre_g'], w_in_t, 'mix_in',
                                                                   stages=[GatherStage([part('ffn2_w_gu', 0)])])
    (y_lru, h_lru), ((w_gu2b,),) = lru_fwd(gate, xbr, sw['conv_w'], sw['conv_b'], wa_bd, sw['lru_b_a'], wx_bd, sw['lru_b_x'],
                                           sw['lru_lambda'], 'lru_fwd', stages=[GatherStage([part('ffn2_w_gu', 1)])])
    (qr, kr, y_attn), (projs,) = attn_fwd(q, k, v, cos, sin_signed, sinks, 'attn_fwd',
                                          stages=[GatherStage([whole(n) for n in proj_names])])
    (p_l, p_a, merged, m, h2), ((w_down2,),) = merge_fwd(y_lru, y_attn, g_lru, g_attn, projs, sw['mix_post_g'], h1, 'merge_fwd',
                                                         stages=[GatherStage([whole('ffn2_w_down')])])
    w_gu2 = [w_gu2a, w_gu2b]
    (n2, g2, u2, a2), _ = ffn_fwd_a(h2, sw['ffn2_pre_g'], w_gu2, 'ffn2_fwd_a')
    (f2, dy, loss_blk), _ = ffn_fwd_b(a2, w_down2, sw['ffn2_post_g'], h2, 'ffn2_fwd_b', target=target)

    gs, full = {}, {}

    def pair_stage(names, grads):
        g4 = [g.reshape(N_CHIPS, 2, PACK_ROWS_OF[n] // 2, D_MODEL) for n, g in zip(names, grads)]
        return PairStage(g4), g4

    def pair_sums(names, g4, lands):
        return [pair_sum(g, l, c_arr, 'pair_sum_' + n) for n, g, l in zip(names, g4, lands)]

    def halves(s, n_parts=2):
        n = s.shape[1] // n_parts
        return [(s, p * n, n) for p in range(n_parts)]

    (df2, dgu2, gs['ffn2_post_g']), _ = ffn_bwd_a(dy, f2, sw['ffn2_post_g'], w_down2, g2, u2, 'ffn2_bwd_a')
    g_down2, _ = mm_tn([a2], df2, 1408, 'ffn2_dw_down')
    st, g4 = pair_stage(['ffn2_w_down'], [g_down2])
    g_gu2, (lands,) = mm_tn([dgu2], n2, 1408, 'ffn2_dw_gu', stages=[st])
    (s_down2,) = pair_sums(['ffn2_w_down'], g4, lands)
    st, g4 = pair_stage(['ffn2_w_gu'], [g_gu2])
    (dh2, gs['ffn2_pre_g']), ((l_down2,), lands) = norm_bwd([dgu2], w_gu2, h2, sw['ffn2_pre_g'], dy, 'ffn2_bwd_b',
                                                            stages=[ChipStage([(s_down2, 0, s_down2.shape[1])]), st])
    (s_gu2,) = pair_sums(['ffn2_w_gu'], g4, lands)

    (dm, dpl, dpa, dgl, dga, dya, dyl, gs['mix_post_g']), ((l_gu2a,),) = merge_bwd(
        dh2, m, sw['mix_post_g'], projs, g_lru, g_attn, p_l, p_a, 'merge_bwd', stages=[ChipStage(halves(s_gu2)[:1])])
    g_projs = [mm_tn([merged if n == 'w_out' else (y_lru if n == 'w_proj_lru' else y_attn)],
                     dm if n == 'w_out' else (dpl if n == 'w_proj_lru' else dpa), D_MODEL, 'd' + n)[0] for n in proj_names]
    st, g4 = pair_stage(proj_names, g_projs)
    (dq, dkv, dsk), ((l_gu2b,), lands, (full['ffn2_w_down'],)) = attn_bwd(
        qr, kr, v, dya, cos, sin_signed, sinks, 'attn_bwd', stages=[ChipStage(halves(s_gu2)[1:]), st, SwapStage([l_down2])])
    full['ffn2_w_down'] = [full['ffn2_w_down']]
    gs['attn_sinks'] = dsk[0:1, 0:N_Q_HEADS]
    s_projs = pair_sums(proj_names, g4, lands)
    (dgate, dxbr, vecs, dwa, dwx), (l_projs, full['ffn2_w_gu']) = lru_bwd(
        gate, xbr, h_lru, dyl, sw['conv_w'], sw['conv_b'], wa_bd, sw['lru_b_a'], wx_bd, sw['lru_b_x'], sw['lru_lambda'],
        'lru_bwd', stages=[ChipStage([(s, 0, s.shape[1]) for s in s_projs]), SwapStage([l_gu2a, l_gu2b])])
    gs['conv_w'] = vecs[0:4]
    gs['conv_b'], gs['lru_b_a'], gs['lru_b_x'], gs['lru_lambda'] = vecs[4:5], vecs[5:6], vecs[6:7], vecs[7:8]
    gs['lru_w_a'] = _diag_blocks(dwa)
    gs['lru_w_x'] = _diag_blocks(dwx)
    dz = [dgate, dxbr, dq, dkv, dgl, dga]
    g_in, ((lru_all,),) = mm_tn(dz, um, 512, 'dw_in', stages=[SmallGatherStage(_pack_lru(gs))])
    st, g4 = pair_stage(['w_in'], [g_in])
    (dh1, gs['mix_pre_g']), (lands, f_projs) = norm_bwd(dz, [w_in_t], h1, sw['mix_pre_g'], dh2, 'mix_bwd_in',
                                                        stages=[st, SwapStage(l_projs)])
    for n, f in zip(proj_names, f_projs):
        full[n] = [f]
    (s_in,) = pair_sums(['w_in'], g4, lands)

    (df1, dgu1, gs['ffn1_post_g']), ((l_in_a,),) = ffn_bwd_a(dh1, f1, sw['ffn1_post_g'], w_down1, g1, u1, 'ffn1_bwd_a',
                                                             stages=[ChipStage(halves(s_in)[:1])])
    g_down1, _ = mm_tn([a1], df1, 1408, 'ffn1_dw_down')
    st, g4 = pair_stage(['ffn1_w_down'], [g_down1])
    g_gu1, ((l_in_b,), lands) = mm_tn([dgu1], n1, 1408, 'ffn1_dw_gu', stages=[ChipStage(halves(s_in)[1:]), st])
    (s_down1,) = pair_sums(['ffn1_w_down'], g4, lands)
    st, g4 = pair_stage(['ffn1_w_gu'], [g_gu1])
    (dx, gs['ffn1_pre_g']), ((l_down1,), lands, full['w_in']) = norm_bwd(
        [dgu1], [w_gu1], xs, sw['ffn1_pre_g'], dh1, 'ffn1_bwd_b',
        stages=[ChipStage([(s_down1, 0, s_down1.shape[1])]), st, SwapStage([l_in_a, l_in_b])])
    (s_gu1,) = pair_sums(['ffn1_w_gu'], g4, lands)
    loss_row = jnp.pad(loss_blk[0:1], ((0, 0), (0, D_MODEL - loss_blk.shape[1])))
    vec_blk = _pack_vecs(gs, jnp.concatenate([gs['conv_w'], loss_row], axis=0))
    (l_gu1,), (vec_all,), (f_down1,) = comm_call(
        'reduce_last', [ChipStage([(s_gu1, 0, s_gu1.shape[1])]), SmallGatherStage(vec_blk), SwapStage([l_down1])])
    full['ffn1_w_down'] = [f_down1]
    full['ffn1_w_gu'] = comm_call('swap_last', [SwapStage([l_gu1])])[0]

    out_g, out_d, out_m, out_v = {}, {}, {}, {}
    for n, _, t in PACK:
        fn = adam_cols if t else adam_rows
        g_, d_, m_, v_ = fn(full[n], n, w[n][0], mom[n][0], var[n][0])
        out_g[n], out_d[n], out_m[n], out_v[n] = g_[None], d_[None], m_[None], v_[None]

    tot = small_sum(vec_all, lru_all)
    loss = tot[ROW_WA - 1, 0]
    conv_g = lax.dynamic_slice(tot[ROW_CONV:ROW_CONV + 4], (0, me_q * (LRU_W // N_CHIPS)), (4, LRU_W // N_CHIPS))
    small_g = _unpack_small(tot, shapes)
    small_g['conv_w'] = conv_g.reshape(shapes['conv_w'])
    g_pack = jnp.concatenate([tot[:ROW_CONV], conv_g.reshape(1, D_MODEL), jnp.zeros((ROW_WA - ROW_CONV - 1, D_MODEL), F32),
                              tot[ROW_WA:]], axis=0)
    packs = [_pack_small({n: d[n] for n in SMALL}, d['conv_w'].reshape(1, D_MODEL)) for d in (w, mom, var)]
    d_p, m_p, v_p = adam_small(g_pack, *packs)
    for n in SMALL:
        out_g[n] = small_g[n]
    for dst, p in ((out_d, d_p), (out_m, m_p), (out_v, v_p)):
        dst.update(_unpack_small(p, shapes))

    return (loss, dx[None], *[out_g[n] for n in WEIGHTS], *[out_d[n] for n in WEIGHTS],
            *[out_m[n] for n in WEIGHTS], *[out_v[n] for n in WEIGHTS])
```

```python
import jax
import jax.numpy as jnp
from jax import lax
from jax.experimental import pallas as pl
from jax.experimental.pallas import tpu as pltpu

F32 = jnp.float32
BF16 = jnp.bfloat16

SEQ = 2048
D_MODEL = 1024
D_FF = 2816
LRU_W = 1024
LRU_BLOCK_W = 64
HEAD_DIM = 64
N_Q_HEADS = 16
N_KV_HEADS = 4
KV_W = N_KV_HEADS * HEAD_DIM
ATTN_BLOCK = 128
N_ATTN_BLOCKS = SEQ // ATTN_BLOCK
IN_SEGS = (1024, 1024, 1024, 256, 256, 1024, 1024)
IN_W = sum(IN_SEGS)
NORM_EPS = 1e-6
MASK_VALUE = -1e30
ROPE_THETA = 10000.0
LRU_C = 8.0
MACARON = 0.5
ADAM_LR = 0.001
ADAM_B1 = 0.9
ADAM_B2 = 0.999
ADAM_EPS = 1e-08
ADAM_WD = 0.01
ADAM_STEP = 10

N_CHIPS = 4
N_DEV = 8
VMEM_LIMIT = 56 * 1024 * 1024
MM_ROWS = 256
MESH = pl.DeviceIdType.MESH
ANY = pl.BlockSpec(memory_space=pl.ANY)

PACK = (('ffn1_w_gu', 1408, True), ('w_in', 1408, True), ('ffn2_w_gu', 1408, True),
        ('ffn1_w_down', 704, False), ('ffn2_w_down', 704, False),
        ('w_proj_lru', 256, False), ('w_proj_attn', 256, False), ('w_out', 256, False))
PACK_ROWS_OF = {n: r for n, r, _ in PACK}
PACK_OFF = {}
_o = 0
for _n, _r, _t in PACK:
    PACK_OFF[_n] = _o
    _o += _r

SMALL_VECS = ('ffn1_pre_g', 'ffn1_post_g', 'mix_pre_g', 'conv_b', 'lru_b_a', 'lru_b_x', 'lru_lambda',
              'mix_post_g', 'ffn2_pre_g', 'ffn2_post_g')
SMALL_ROWS = 144
ROW_SINKS, ROW_CONV, ROW_WA, ROW_WX = 10, 11, 16, 80


def _dot(a, b):
    return jnp.dot(a, b, preferred_element_type=F32)


def _dot_nt(a, b):
    return lax.dot_general(a, b, (((1,), (1,)), ((), ())), preferred_element_type=F32)


def _dot_tn(a, b):
    return lax.dot_general(a, b, (((0,), (0,)), ((), ())), preferred_element_type=F32)


def _params(n_grid):
    return pltpu.CompilerParams(dimension_semantics=("arbitrary",) * n_grid, vmem_limit_bytes=VMEM_LIMIT)


def _sigmoid(x):
    return 1.0 / (1.0 + jnp.exp(-x))


def _rsqrt_mean_sq(x):
    return lax.rsqrt(jnp.mean(x * x, axis=-1, keepdims=True) + NORM_EPS)


def _expm1(x):
    poly = x * (1.0 + x * (0.5 + x * (1.0 / 6.0 + x * (1.0 / 24.0 + x * (1.0 / 120.0)))))
    return jnp.where(jnp.abs(x) < 0.1, poly, jnp.exp(x) - 1.0)


_GELU_K = 0.7978845608028654
_GELU_C = 0.044715


def _gelu(x):
    t = jnp.tanh(_GELU_K * (x + _GELU_C * x * x * x))
    return 0.5 * x * (1.0 + t), t


def _gelu_grad(x, t):
    return 0.5 * (1.0 + t) + 0.5 * x * (1.0 - t * t) * _GELU_K * (1.0 + 3.0 * _GELU_C * x * x)


def _load_weight(w_refs, dst_ref, sem):
    w_refs = list(w_refs) if isinstance(w_refs, (list, tuple)) else [w_refs]
    rows = dst_ref.shape[0] // N_CHIPS
    rp = rows // len(w_refs)
    cps = [pltpu.make_async_copy(w_ref.at[q], dst_ref.at[pl.ds(q * rows + p * rp, rp)], sem.at[p * N_CHIPS + q])
           for p, w_ref in enumerate(w_refs) for q in range(N_CHIPS)]
    for cp in cps:
        cp.start()
    for cp in cps:
        cp.wait()


def _weight_scratch(rows_total, parts=1):
    return [pltpu.VMEM((rows_total, D_MODEL), BF16), pltpu.SemaphoreType.DMA((N_CHIPS * parts,))]


_ROW = lambda tm: pl.BlockSpec((tm, D_MODEL), lambda i: (i, 0))
_VEC = pl.BlockSpec((1, D_MODEL), lambda i: (0, 0))


def _call(body, *, name, grid, in_specs, out_specs, out_shape, args, scratch_shapes=(), stages=()):
    in_specs, out_specs, out_shape, scratch_shapes = list(in_specs), list(out_specs), list(out_shape), list(scratch_shapes)
    n_in, n_out, n_sc = len(in_specs), len(out_specs), len(scratch_shapes)
    k_in = [len(s.inputs) for s in stages]
    k_out = [len(s.out_shape) for s in stages]
    k_sc = [len(s.scratch) for s in stages]
    last = grid[0] - 1

    def split(refs, counts):
        parts, pos = [], 0
        for k in counts:
            parts.append(refs[pos:pos + k])
            pos += k
        return parts

    def full(*refs):
        ins, s_ins, outs, s_outs, scr, s_scr = split(refs, [n_in, sum(k_in), n_out, sum(k_out), n_sc, sum(k_sc)])
        per_stage = list(zip(stages, split(s_ins, k_in), split(s_outs, k_out), split(s_scr, k_sc)))
        i = pl.program_id(0)
        if stages:
            @pl.when(i == 0)
            def _():
                for s, a, b, c in per_stage:
                    s.start(a, b, c)

        body(*ins, *outs, *scr)
        if stages:
            @pl.when(i == max(last - 1, 0))
            def _():
                for s, a, b, c in per_stage:
                    s.mid(a, b, c)

            @pl.when(i == last)
            def _():
                for s, a, b, c in per_stage:
                    s.end(a, b, c)

    res = pl.pallas_call(
        full, name=name, grid=grid,
        in_specs=in_specs + [ANY] * sum(k_in),
        out_specs=out_specs + [ANY] * sum(k_out),
        out_shape=out_shape + [o for s in stages for o in s.out_shape],
        scratch_shapes=scratch_shapes + [x for s in stages for x in s.scratch],
        compiler_params=_params(1),
    )(*args, *[a for s in stages for a in s.inputs])
    return list(res[:n_out]), split(list(res[n_out:]), k_out)


def ffn_fwd_a(x, g_pre, w_gu_t, name, stages=()):
    tm, tn = MM_ROWS, 256
    n_w = len(w_gu_t)

    def body(x_ref, gp_ref, *refs):
        w_refs = refs[:n_w]
        n_ref, g_ref, u_ref, a_ref, wt_ref, sem = refs[n_w:]

        @pl.when(pl.program_id(0) == 0)
        def _():
            _load_weight(w_refs, wt_ref, sem)

        xv = x_ref[...]
        n = (xv * _rsqrt_mean_sq(xv) * gp_ref[...]).astype(BF16)
        n_ref[...] = n
        for j in range(D_FF // tn):
            g = _dot_nt(n, wt_ref[j * tn:(j + 1) * tn, :])
            u = _dot_nt(n, wt_ref[D_FF + j * tn:D_FF + (j + 1) * tn, :])
            g_ref[:, j * tn:(j + 1) * tn] = g.astype(BF16)
            u_ref[:, j * tn:(j + 1) * tn] = u.astype(BF16)
            a_ref[:, j * tn:(j + 1) * tn] = (g * _sigmoid(g) * u).astype(BF16)

    wide = pl.BlockSpec((tm, D_FF), lambda i: (i, 0))
    return _call(
        body, name=name, grid=(SEQ // tm,),
        in_specs=[_ROW(tm), _VEC] + [ANY] * n_w,
        out_specs=[_ROW(tm), wide, wide, wide],
        out_shape=[jax.ShapeDtypeStruct((SEQ, D_MODEL), BF16)] + [jax.ShapeDtypeStruct((SEQ, D_FF), BF16)] * 3,
        scratch_shapes=_weight_scratch(2 * D_FF, n_w),
        args=[x, g_pre, *w_gu_t], stages=stages)


def ffn_fwd_b(a, w_down, g_post, h_in, name, target=None, stages=()):
    tm = MM_ROWS
    final = target is not None

    def body(*refs):
        if final:
            a_ref, wf_ref, gp_ref, h_ref, t_ref, f_ref, o_ref, loss_ref, wd_ref, sem = refs
        else:
            a_ref, wf_ref, gp_ref, h_ref, f_ref, o_ref, wd_ref, sem = refs

        @pl.when(pl.program_id(0) == 0)
        def _():
            _load_weight(wf_ref, wd_ref, sem)
            if final:
                loss_ref[...] = jnp.zeros_like(loss_ref)

        f = _dot(a_ref[...], wd_ref[...])
        f_ref[...] = f
        y = h_ref[...] + MACARON * (f * _rsqrt_mean_sq(f) * gp_ref[...])
        if final:
            err = y - t_ref[...]
            o_ref[...] = err * (1.0 / D_MODEL)
            loss_ref[...] += 0.5 * jnp.sum(err * err) * (1.0 / D_MODEL)
        else:
            o_ref[...] = y

    row = _ROW(tm)
    in_specs = [pl.BlockSpec((tm, D_FF), lambda i: (i, 0)), ANY, _VEC, row]
    out_specs = [row, row]
    out_shape = [jax.ShapeDtypeStruct((SEQ, D_MODEL), F32)] * 2
    args = [a, w_down, g_post, h_in]
    if final:
        in_specs.append(row)
        args.append(target)
        out_specs.append(pl.BlockSpec((8, 128), lambda i: (0, 0)))
        out_shape.append(jax.ShapeDtypeStruct((8, 128), F32))
    return _call(body, name=name, grid=(SEQ // tm,), in_specs=in_specs, out_specs=out_specs,
                 out_shape=out_shape, scratch_shapes=_weight_scratch(D_FF), args=args, stages=stages)


def ffn_bwd_a(d_out, f, g_post, w_down, g, u, name, stages=()):
    tm = MM_ROWS
    tc = 256

    def body(do_ref, f_ref, gp_ref, wf_ref, g_ref, u_ref, df_ref, dgu_ref, dgp_ref, wd_ref, sem):
        @pl.when(pl.program_id(0) == 0)
        def _():
            _load_weight(wf_ref, wd_ref, sem)
            dgp_ref[...] = jnp.zeros_like(dgp_ref)

        fv = f_ref[...]
        rf = _rsqrt_mean_sq(fv)
        fh = fv * rf
        dn = MACARON * do_ref[...]
        dgp_ref[...] += jnp.sum(dn * fh, axis=0, keepdims=True)
        t = dn * gp_ref[...]
        df = (rf * (t - fh * jnp.mean(t * fh, axis=-1, keepdims=True))).astype(BF16)
        df_ref[...] = df
        for c0 in range(0, D_FF, tc):
            da = _dot_nt(df, wd_ref[c0:c0 + tc, :])
            gv = g_ref[:, c0:c0 + tc].astype(F32)
            uv = u_ref[:, c0:c0 + tc].astype(F32)
            s = _sigmoid(gv)
            dgu_ref[:, c0:c0 + tc] = (da * uv * s * (1.0 + gv * (1.0 - s))).astype(BF16)
            dgu_ref[:, D_FF + c0:D_FF + c0 + tc] = (da * gv * s).astype(BF16)

    row = _ROW(tm)
    wide = pl.BlockSpec((tm, D_FF), lambda i: (i, 0))
    return _call(
        body, name=name, grid=(SEQ // tm,),
        in_specs=[row, row, _VEC, ANY, wide, wide],
        out_specs=[row, pl.BlockSpec((tm, 2 * D_FF), lambda i: (i, 0)), _VEC],
        out_shape=[jax.ShapeDtypeStruct((SEQ, D_MODEL), BF16), jax.ShapeDtypeStruct((SEQ, 2 * D_FF), BF16),
                   jax.ShapeDtypeStruct((1, D_MODEL), F32)],
        scratch_shapes=_weight_scratch(D_FF),
        args=[d_out, f, g_post, w_down, g, u], stages=stages)


def norm_bwd(pieces, w_t, x, g_pre, d_res, name, stages=()):
    tm = MM_ROWS
    widths = [p.shape[1] for p in pieces]
    offs = [sum(widths[:k]) for k in range(len(widths))]
    n_p = len(pieces)
    n_w = len(w_t)

    def body(*refs):
        p_refs = refs[:n_p]
        w_refs = refs[n_p:n_p + n_w]
        x_ref, g_ref, r_ref, dx_ref, dg_ref, wt_ref, sem = refs[n_p + n_w:]

        @pl.when(pl.program_id(0) == 0)
        def _():
            _load_weight(w_refs, wt_ref, sem)
            dg_ref[...] = jnp.zeros_like(dg_ref)

        dn = None
        for p_ref, lo, wd in zip(p_refs, offs, widths):
            part = _dot(p_ref[...], wt_ref[lo:lo + wd, :])
            dn = part if dn is None else dn + part
        xv = x_ref[...]
        r = _rsqrt_mean_sq(xv)
        xh = xv * r
        dg_ref[...] += jnp.sum(dn * xh, axis=0, keepdims=True)
        t = dn * g_ref[...]
        dx_ref[...] = r_ref[...] + r * (t - xh * jnp.mean(t * xh, axis=-1, keepdims=True))

    row = _ROW(tm)
    return _call(
        body, name=name, grid=(SEQ // tm,),
        in_specs=[pl.BlockSpec((tm, wd), lambda i: (i, 0)) for wd in widths] + [ANY] * n_w + [row, _VEC, row],
        out_specs=[row, _VEC],
        out_shape=[jax.ShapeDtypeStruct((SEQ, D_MODEL), F32), jax.ShapeDtypeStruct((1, D_MODEL), F32)],
        scratch_shapes=_weight_scratch(sum(widths), n_w),
        args=[*pieces, *w_t, x, g_pre, d_res], stages=stages)


def mm_tn(pieces, b, tm, name, stages=()):
    widths = [p.shape[1] for p in pieces]
    m_total = sum(widths)
    n_p = len(pieces)
    starts = [sum(widths[:k]) // tm for k in range(n_p)]
    counts = [wd // tm for wd in widths]

    def body(*refs):
        p_refs = refs[:n_p]
        b_ref, o_ref = refs[n_p:]
        i = pl.program_id(0)
        for p_ref, st, ct in zip(p_refs, starts, counts):
            @pl.when((i >= st) & (i < st + ct))
            def _(p_ref=p_ref):
                o_ref[...] = _dot_tn(p_ref[...], b_ref[...]).astype(BF16)

    def piece_spec(st, ct):
        return pl.BlockSpec((SEQ, tm), lambda i: (0, jnp.clip(i - st, 0, ct - 1)))

    (out,), stage_out = _call(
        body, name=name, grid=(m_total // tm,),
        in_specs=[piece_spec(st, ct) for st, ct in zip(starts, counts)] + [pl.BlockSpec((SEQ, D_MODEL), lambda i: (0, 0))],
        out_specs=[pl.BlockSpec((tm, D_MODEL), lambda i: (i, 0))],
        out_shape=[jax.ShapeDtypeStruct((m_total, D_MODEL), BF16)],
        args=[*pieces, b], stages=stages)
    return out, stage_out


def mix_in(h, g_pre, w_in_t, name, stages=()):
    tm = MM_ROWS
    offs = [sum(IN_SEGS[:k]) for k in range(len(IN_SEGS))]
    dts = [F32, F32, F32, F32, BF16, F32, F32]
    n_o = len(IN_SEGS)

    def body(*refs):
        h_ref, g_ref, wf_ref, um_ref = refs[:4]
        o_refs = refs[4:4 + n_o]
        wt_ref, sem = refs[4 + n_o:]

        @pl.when(pl.program_id(0) == 0)
        def _():
            _load_weight(wf_ref, wt_ref, sem)

        hv = h_ref[...]
        um = (hv * _rsqrt_mean_sq(hv) * g_ref[...]).astype(BF16)
        um_ref[...] = um
        for o_ref, lo, wd in zip(o_refs, offs, IN_SEGS):
            for c0 in range(0, wd, 256):
                o_ref[:, c0:c0 + 256] = _dot_nt(um, wt_ref[lo + c0:lo + c0 + 256, :]).astype(o_ref.dtype)

    return _call(
        body, name=name, grid=(SEQ // tm,),
        in_specs=[_ROW(tm), _VEC, ANY],
        out_specs=[_ROW(tm)] + [pl.BlockSpec((tm, wd), lambda i: (i, 0)) for wd in IN_SEGS],
        out_shape=[jax.ShapeDtypeStruct((SEQ, D_MODEL), BF16)]
        + [jax.ShapeDtypeStruct((SEQ, wd), dt) for wd, dt in zip(IN_SEGS, dts)],
        scratch_shapes=_weight_scratch(IN_W),
        args=[h, g_pre, w_in_t], stages=stages)


LRU_TC = 256


def _conv_fwd(xb, cw, cb, tt):
    xc = xb * cw[3:4, :] + cb
    shifted = []
    for s in (1, 2, 3):
        sh = jnp.where(tt >= s, pltpu.roll(xb, s, 0), 0.0)
        shifted.append(sh)
        xc = xc + sh * cw[3 - s:4 - s, :]
    return xc, shifted


def _lru_gates(xc, wa, ba, wx, bx, lam):
    xcb = xc.astype(BF16)
    r = _sigmoid(_dot(xcb, wa) + ba)
    i = _sigmoid(_dot(xcb, wx) + bx)
    nl = -lam
    sp = jnp.maximum(nl, 0.0) + jnp.log1p(jnp.exp(-jnp.abs(nl)))
    la = (-LRU_C * r) * sp
    a = jnp.exp(la)
    mult = jnp.sqrt(jnp.maximum(-_expm1(2.0 * la), 0.0))
    return xcb, r, i, sp, a, mult


def _scan(a, b, tt, reverse):
    n = a.shape[0]
    s = 1
    while s < n:
        more = 2 * s < n
        if s < 8:
            if reverse:
                keep = tt < n - s
                shift = n - s
            else:
                keep = tt >= s
                shift = s
            b = a * jnp.where(keep, pltpu.roll(b, shift, 0), 0.0) + b
            if more:
                a = a * jnp.where(keep, pltpu.roll(a, shift, 0), 1.0)
        elif reverse:
            b = jnp.concatenate([a[:n - s] * b[s:] + b[:n - s], b[n - s:]], axis=0)
            if more:
                a = jnp.concatenate([a[:n - s] * a[s:], a[n - s:]], axis=0)
        else:
            b = jnp.concatenate([b[:s], a[s:] * b[:n - s] + b[s:]], axis=0)
            if more:
                a = jnp.concatenate([a[:s], a[s:] * a[:n - s]], axis=0)
        s *= 2
    return b


def _lru_specs():
    col = pl.BlockSpec((SEQ, LRU_TC), lambda j: (0, j))
    vec = pl.BlockSpec((1, LRU_TC), lambda j: (0, j))
    bd = pl.BlockSpec((1, LRU_TC, LRU_TC), lambda j: (j, 0, 0))
    cw = pl.BlockSpec((4, LRU_TC), lambda j: (0, j))
    return col, vec, bd, cw


def lru_fwd(gate, xbr, conv_w, conv_b, wa_bd, b_a, wx_bd, b_x, lam, name, stages=()):
    col, vec, bd, cw = _lru_specs()

    def body(gate_ref, xbr_ref, cw_ref, cb_ref, wa_ref, ba_ref, wx_ref, bx_ref, lam_ref, y_ref, h_ref):
        tt = lax.broadcasted_iota(jnp.int32, (SEQ, LRU_TC), 0)
        xc, _ = _conv_fwd(xbr_ref[...], cw_ref[...], cb_ref[...], tt)
        _, r, i, sp, a, mult = _lru_gates(xc, wa_ref[0], ba_ref[...], wx_ref[0], bx_ref[...], lam_ref[...])
        h = _scan(a, mult * (i * xc), tt, reverse=False)
        h_ref[...] = h
        gl, _ = _gelu(gate_ref[...])
        y_ref[...] = (h * gl).astype(BF16)

    return _call(
        body, name=name, grid=(LRU_W // LRU_TC,),
        in_specs=[col, col, cw, vec, bd, vec, bd, vec, vec],
        out_specs=[col, col],
        out_shape=[jax.ShapeDtypeStruct((SEQ, LRU_W), BF16), jax.ShapeDtypeStruct((SEQ, LRU_W), F32)],
        args=[gate, xbr, conv_w, conv_b, wa_bd, b_a, wx_bd, b_x, lam], stages=stages)


def lru_bwd(gate, xbr, h, dy, conv_w, conv_b, wa_bd, b_a, wx_bd, b_x, lam, name, stages=()):
    col, vec, bd, cw = _lru_specs()

    def body(gate_ref, xbr_ref, h_ref, dy_ref, cw_ref, cb_ref, wa_ref, ba_ref, wx_ref, bx_ref, lam_ref,
             dgate_ref, dxbr_ref, vecs_ref, dwa_ref, dwx_ref):
        tt = lax.broadcasted_iota(jnp.int32, (SEQ, LRU_TC), 0)
        cwv = cw_ref[...]
        lam = lam_ref[...]
        xb = xbr_ref[...]
        xc, shifted = _conv_fwd(xb, cwv, cb_ref[...], tt)
        wa = wa_ref[0]
        wx = wx_ref[0]
        xcb, r, i, sp, a, mult = _lru_gates(xc, wa, ba_ref[...], wx, bx_ref[...], lam)
        hv = h_ref[...]
        dyv = dy_ref[...]
        gv = gate_ref[...]
        gl, th = _gelu(gv)
        dgate_ref[...] = (dyv * hv * _gelu_grad(gv, th)).astype(BF16)
        a_next = jnp.where(tt < SEQ - 1, pltpu.roll(a, SEQ - 1, 0), 0.0)
        gsum = _scan(a_next, dyv * gl, tt, reverse=True)
        h_prev = jnp.where(tt >= 1, pltpu.roll(hv, 1, 0), 0.0)
        d_mult = gsum * i * xc
        d_i = gsum * mult * xc
        d_xc = gsum * mult * i
        d_la = gsum * h_prev * a - d_mult * (a * a) / mult
        d_pr = (d_la * (-LRU_C * sp)) * r * (1.0 - r)
        d_pi = d_i * i * (1.0 - i)
        d_lam = jnp.sum(d_la * r, axis=0, keepdims=True) * (LRU_C * _sigmoid(-lam))
        d_prb = d_pr.astype(BF16)
        d_pib = d_pi.astype(BF16)
        d_xc = d_xc + _dot_nt(d_prb, wa) + _dot_nt(d_pib, wx)
        dwa_ref[0] = _dot_tn(xcb, d_prb)
        dwx_ref[0] = _dot_tn(xcb, d_pib)
        rows = [jnp.sum(d_xc * shifted[2], axis=0, keepdims=True),
                jnp.sum(d_xc * shifted[1], axis=0, keepdims=True),
                jnp.sum(d_xc * shifted[0], axis=0, keepdims=True),
                jnp.sum(d_xc * xb, axis=0, keepdims=True),
                jnp.sum(d_xc, axis=0, keepdims=True),
                jnp.sum(d_pr, axis=0, keepdims=True),
                jnp.sum(d_pi, axis=0, keepdims=True),
                d_lam]
        ri = lax.broadcasted_iota(jnp.int32, (8, LRU_TC), 0)
        acc = jnp.zeros((8, LRU_TC), F32)
        for k, rv in enumerate(rows):
            acc = jnp.where(ri == k, rv, acc)
        vecs_ref[...] = acc
        d_xb = d_xc * cwv[3:4, :]
        for s in (1, 2, 3):
            d_xb = d_xb + jnp.where(tt < SEQ - s, pltpu.roll(d_xc, SEQ - s, 0), 0.0) * cwv[3 - s:4 - s, :]
        dxbr_ref[...] = d_xb.astype(BF16)

    return _call(
        body, name=name, grid=(LRU_W // LRU_TC,),
        in_specs=[col, col, col, col, cw, vec, bd, vec, bd, vec, vec],
        out_specs=[col, col, pl.BlockSpec((8, LRU_TC), lambda j: (0, j)), bd, bd],
        out_shape=[jax.ShapeDtypeStruct((SEQ, LRU_W), BF16), jax.ShapeDtypeStruct((SEQ, LRU_W), BF16),
                   jax.ShapeDtypeStruct((8, LRU_W), F32),
                   jax.ShapeDtypeStruct((LRU_W // LRU_TC, LRU_TC, LRU_TC), F32),
                   jax.ShapeDtypeStruct((LRU_W // LRU_TC, LRU_TC, LRU_TC), F32)],
        args=[gate, xbr, h, dy, conv_w, conv_b, wa_bd, b_a, wx_bd, b_x, lam], stages=stages)


def _rope(x, cos, sin_signed):
    w = x.shape[1]
    reps = w // 128
    if reps > 1:
        cos = jnp.tile(cos, (1, reps))
        sin_signed = jnp.tile(sin_signed, (1, reps))
    lane = lax.broadcasted_iota(jnp.int32, x.shape, 1)
    first = (lane & 63) < 32
    partner = jnp.where(first, pltpu.roll(x, w - 32, 1), pltpu.roll(x, 32, 1))
    return x * cos + partner * sin_signed


def _both_halves(t, odd):
    lo = lax.broadcasted_iota(jnp.int32, t.shape, 1) < 64
    rolled = pltpu.roll(t, 64, 1)
    return jnp.where(lo, rolled, t) if odd else jnp.where(lo, t, rolled)


def _stack_heads(ta, tb):
    lo = lax.broadcasted_iota(jnp.int32, ta.shape, 1) < 64
    return jnp.concatenate([jnp.where(lo, ta, 0.0), jnp.where(lo, 0.0, ta),
                            jnp.where(lo, tb, 0.0), jnp.where(lo, 0.0, tb)], axis=0)


def _unstack_heads(o):
    lo = lax.broadcasted_iota(jnp.int32, (ATTN_BLOCK, 128), 1) < 64
    return (jnp.where(lo, o[0:128], o[128:256]), jnp.where(lo, o[256:384], o[384:512]))


def _attn_probs(qs, kd, sinks_ref, hk, first_block):
    s = _dot_nt(qs, kd) * (HEAD_DIM ** -0.5)
    row = lax.broadcasted_iota(jnp.int32, s.shape, 0)
    si = lax.broadcasted_iota(jnp.int32, s.shape, 1)
    diff = ATTN_BLOCK + (row & (ATTN_BLOCK - 1)) - si
    valid = (diff >= 0) & (diff < ATTN_BLOCK) & ((si >= ATTN_BLOCK) | jnp.logical_not(first_block))
    s = jnp.where(valid, s, MASK_VALUE)
    rg = lax.broadcasted_iota(jnp.int32, (4 * ATTN_BLOCK, 1), 0) >> 7
    sink = jnp.where(rg == 0, sinks_ref[4 * hk],
                     jnp.where(rg == 1, sinks_ref[4 * hk + 1],
                               jnp.where(rg == 2, sinks_ref[4 * hk + 2], sinks_ref[4 * hk + 3])))
    m = jnp.maximum(jnp.max(s, axis=1, keepdims=True), sink)
    e = jnp.exp(s - m)
    es = jnp.exp(sink - m)
    inv = 1.0 / (jnp.sum(e, axis=1, keepdims=True) + es)
    return e * inv, es * inv


def _prev(i):
    return jnp.maximum(i - 1, 0)


def attn_fwd(q, k, v, cos, sin_signed, sinks, name, stages=()):
    nb = ATTN_BLOCK

    def body(q_ref, kc_ref, kp_ref, vc_ref, vp_ref, cc_ref, sc_ref, cp_ref, sp_ref, sinks_ref,
             qr_ref, kr_ref, y_ref):
        first_block = pl.program_id(0) == 0
        qr = _rope(q_ref[...], cc_ref[...], sc_ref[...])
        kc = _rope(kc_ref[...], cc_ref[...], sc_ref[...])
        kp = _rope(kp_ref[...], cp_ref[...], sp_ref[...])
        qr_ref[...] = qr.astype(BF16)
        kr_ref[...] = kc.astype(BF16)
        k2 = jnp.concatenate([kp, kc], axis=0)
        v2 = jnp.concatenate([vp_ref[...].astype(F32), vc_ref[...].astype(F32)], axis=0)
        for hk in range(N_KV_HEADS):
            kt = hk // 2
            kd = _both_halves(k2[:, kt * 128:(kt + 1) * 128], hk % 2).astype(BF16)
            vd = _both_halves(v2[:, kt * 128:(kt + 1) * 128], hk % 2).astype(BF16)
            qs = _stack_heads(qr[:, (2 * hk) * 128:(2 * hk + 1) * 128],
                              qr[:, (2 * hk + 1) * 128:(2 * hk + 2) * 128]).astype(BF16)
            p, _ = _attn_probs(qs, kd, sinks_ref, hk, first_block)
            ta, tb = _unstack_heads(_dot(p.astype(BF16), vd))
            y_ref[:, (2 * hk) * 128:(2 * hk + 1) * 128] = ta.astype(BF16)
            y_ref[:, (2 * hk + 1) * 128:(2 * hk + 2) * 128] = tb.astype(BF16)

    cur = lambda w: pl.BlockSpec((nb, w), lambda i: (i, 0))
    prv = lambda w: pl.BlockSpec((nb, w), lambda i: (_prev(i), 0))
    return _call(
        body, name=name, grid=(N_ATTN_BLOCKS,),
        in_specs=[cur(D_MODEL), cur(KV_W), prv(KV_W), cur(KV_W), prv(KV_W), cur(128), cur(128), prv(128), prv(128),
                  pl.BlockSpec(memory_space=pltpu.SMEM)],
        out_specs=[cur(D_MODEL), cur(KV_W), cur(D_MODEL)],
        out_shape=[jax.ShapeDtypeStruct((SEQ, D_MODEL), BF16), jax.ShapeDtypeStruct((SEQ, KV_W), BF16),
                   jax.ShapeDtypeStruct((SEQ, D_MODEL), BF16)],
        args=[q, k, k, v, v, cos, sin_signed, cos, sin_signed, sinks], stages=stages)


def attn_bwd(qr, kr, v, dy, cos, sin_signed, sinks, name, stages=()):
    nb = ATTN_BLOCK
    n_steps = N_ATTN_BLOCKS + 1
    scale = HEAD_DIM ** -0.5

    def body(q_ref, kc_ref, kp_ref, vc_ref, vp_ref, dy_ref, cc_ref, sc_ref, cp_ref, sp_ref, sinks_ref,
             dq_ref, dkv_ref, dsk_ref, ck_ref, cv_ref):
        dk_ref = dkv_ref.at[:, pl.ds(0, KV_W)]
        dv_ref = dkv_ref.at[:, pl.ds(KV_W, KV_W)]
        i = pl.program_id(0)

        @pl.when(i == 0)
        def _():
            dsk_ref[...] = jnp.zeros_like(dsk_ref)
            ck_ref[...] = jnp.zeros_like(ck_ref)
            cv_ref[...] = jnp.zeros_like(cv_ref)

        @pl.when(i < N_ATTN_BLOCKS)
        def _():
            qv = q_ref[...].astype(F32)
            dov = dy_ref[...].astype(F32)
            k2 = jnp.concatenate([kp_ref[...].astype(F32), kc_ref[...].astype(F32)], axis=0)
            v2 = jnp.concatenate([vp_ref[...].astype(F32), vc_ref[...].astype(F32)], axis=0)
            lane = lax.broadcasted_iota(jnp.int32, (8, 128), 1)
            lo = lax.broadcasted_iota(jnp.int32, (2 * nb, 128), 1) < 64
            dsk = jnp.zeros((8, 128), F32)
            dk_tiles = []
            dv_tiles = []
            for hk in range(N_KV_HEADS):
                kt = hk // 2
                kd = _both_halves(k2[:, kt * 128:(kt + 1) * 128], hk % 2).astype(BF16)
                vd = _both_halves(v2[:, kt * 128:(kt + 1) * 128], hk % 2).astype(BF16)
                qs = _stack_heads(qv[:, (2 * hk) * 128:(2 * hk + 1) * 128],
                                  qv[:, (2 * hk + 1) * 128:(2 * hk + 2) * 128]).astype(BF16)
                dos = _stack_heads(dov[:, (2 * hk) * 128:(2 * hk + 1) * 128],
                                   dov[:, (2 * hk + 1) * 128:(2 * hk + 2) * 128]).astype(BF16)
                p, ps = _attn_probs(qs, kd, sinks_ref, hk, i == 0)
                dp = _dot_nt(dos, vd)
                delta = jnp.sum(p * dp, axis=1, keepdims=True)
                ds = (p * (dp - delta)).astype(BF16)
                dsink = -ps * delta
                for g in range(4):
                    dsk = dsk + jnp.where(lane == 4 * hk + g, jnp.sum(dsink[g * nb:(g + 1) * nb]), 0.0)
                ta, tb = _unstack_heads(_dot(ds, kd) * scale)
                dq_a = (2 * hk) * 128
                dq_ref[:, dq_a:dq_a + 128] = _rope(ta, cc_ref[...], -sc_ref[...]).astype(BF16)
                dq_ref[:, dq_a + 128:dq_a + 256] = _rope(tb, cc_ref[...], -sc_ref[...]).astype(BF16)
                rk = _dot_tn(ds, qs) * scale
                rv = _dot_tn(p.astype(BF16), dos)
                dk_tiles.append(rk + pltpu.roll(rk, 64, 1))
                dv_tiles.append(rv + pltpu.roll(rv, 64, 1))
            dsk_ref[...] += dsk
            dk_full = jnp.concatenate([jnp.where(lo, dk_tiles[0], dk_tiles[1]),
                                       jnp.where(lo, dk_tiles[2], dk_tiles[3])], axis=1)
            dv_full = jnp.concatenate([jnp.where(lo, dv_tiles[0], dv_tiles[1]),
                                       jnp.where(lo, dv_tiles[2], dv_tiles[3])], axis=1)
            dk_ref[...] = _rope(ck_ref[...] + dk_full[0:nb], cp_ref[...], -sp_ref[...]).astype(BF16)
            dv_ref[...] = (cv_ref[...] + dv_full[0:nb]).astype(BF16)
            ck_ref[...] = dk_full[nb:2 * nb]
            cv_ref[...] = dv_full[nb:2 * nb]

        @pl.when(i == N_ATTN_BLOCKS)
        def _():
            dk_ref[...] = _rope(ck_ref[...], cp_ref[...], -sp_ref[...]).astype(BF16)
            dv_ref[...] = cv_ref[...].astype(BF16)

    qi = lambda i: jnp.minimum(i, N_ATTN_BLOCKS - 1)
    cur = lambda w: pl.BlockSpec((nb, w), lambda i: (qi(i), 0))
    prv = lambda w: pl.BlockSpec((nb, w), lambda i: (_prev(qi(i)), 0))
    out_prev = lambda w: pl.BlockSpec((nb, w), lambda i: (_prev(i), 0))
    return _call(
        body, name=name, grid=(n_steps,),
        in_specs=[cur(D_MODEL), cur(KV_W), prv(KV_W), cur(KV_W), prv(KV_W), cur(D_MODEL),
                  cur(128), cur(128), out_prev(128), out_prev(128), pl.BlockSpec(memory_space=pltpu.SMEM)],
        out_specs=[cur(D_MODEL), out_prev(2 * KV_W), pl.BlockSpec((8, 128), lambda i: (0, 0))],
        out_shape=[jax.ShapeDtypeStruct((SEQ, D_MODEL), BF16), jax.ShapeDtypeStruct((SEQ, 2 * KV_W), BF16),
                   jax.ShapeDtypeStruct((8, 128), F32)],
        scratch_shapes=[pltpu.VMEM((nb, KV_W), F32), pltpu.VMEM((nb, KV_W), F32)],
        args=[qr, kr, kr, v, v, dy, cos, sin_signed, cos, sin_signed, sinks], stages=stages)


def _proj_scratch():
    return [pltpu.VMEM((D_MODEL, D_MODEL), BF16)] * 3 + [pltpu.SemaphoreType.DMA((3 * N_CHIPS,))]


def _load_projs(w_refs, wl_ref, wa_ref, wo_ref, sem):
    for k, (w_ref, dst) in enumerate(zip(w_refs, (wl_ref, wa_ref, wo_ref))):
        _load_weight(w_ref, dst, sem.at[pl.ds(k * N_CHIPS, N_CHIPS)])


def merge_fwd(y_lru, y_attn, g_lru, g_attn, projs, g_post, h_in, name, stages=()):
    tm = MM_ROWS

    def body(yl_ref, ya_ref, gl_ref, ga_ref, w1_ref, w2_ref, w3_ref, gp_ref, h_ref,
             pl_ref, pa_ref, mg_ref, m_ref, o_ref, wl_ref, wa_ref, wo_ref, sem):
        @pl.when(pl.program_id(0) == 0)
        def _():
            _load_projs((w1_ref, w2_ref, w3_ref), wl_ref, wa_ref, wo_ref, sem)

        p_l = _dot(yl_ref[...], wl_ref[...])
        p_a = _dot(ya_ref[...], wa_ref[...])
        pl_ref[...] = p_l.astype(BF16)
        pa_ref[...] = p_a.astype(BF16)
        merged = (_sigmoid(gl_ref[...]) * p_l + _sigmoid(ga_ref[...]) * p_a).astype(BF16)
        mg_ref[...] = merged
        m = _dot(merged, wo_ref[...])
        m_ref[...] = m
        o_ref[...] = h_ref[...] + m * _rsqrt_mean_sq(m) * gp_ref[...]

    row = _ROW(tm)
    return _call(
        body, name=name, grid=(SEQ // tm,),
        in_specs=[row, row, row, row, ANY, ANY, ANY, _VEC, row],
        out_specs=[row] * 5,
        out_shape=[jax.ShapeDtypeStruct((SEQ, D_MODEL), BF16)] * 3 + [jax.ShapeDtypeStruct((SEQ, D_MODEL), F32)] * 2,
        scratch_shapes=_proj_scratch(),
        args=[y_lru, y_attn, g_lru, g_attn, *projs, g_post, h_in], stages=stages)


def merge_bwd(d_out, m, g_post, projs, g_lru, g_attn, p_l, p_a, name, stages=()):
    tm = 256

    def body(do_ref, m_ref, gp_ref, w1_ref, w2_ref, w3_ref, gl_ref, ga_ref, pl_ref, pa_ref,
             dm_ref, dpl_ref, dpa_ref, dgl_ref, dga_ref, dya_ref, dyl_ref, dgp_ref, wl_ref, wa_ref, wo_ref, sem):
        @pl.when(pl.program_id(0) == 0)
        def _():
            _load_projs((w1_ref, w2_ref, w3_ref), wl_ref, wa_ref, wo_ref, sem)
            dgp_ref[...] = jnp.zeros_like(dgp_ref)

        mv = m_ref[...]
        rm = _rsqrt_mean_sq(mv)
        mh = mv * rm
        dn = do_ref[...]
        dgp_ref[...] += jnp.sum(dn * mh, axis=0, keepdims=True)
        t = dn * gp_ref[...]
        dm = (rm * (t - mh * jnp.mean(t * mh, axis=-1, keepdims=True))).astype(BF16)
        dm_ref[...] = dm
        dmg = _dot_nt(dm, wo_ref[...])
        sl = _sigmoid(gl_ref[...])
        sa = _sigmoid(ga_ref[...])
        dpl = (dmg * sl).astype(BF16)
        dpa = (dmg * sa).astype(BF16)
        dpl_ref[...] = dpl
        dpa_ref[...] = dpa
        dgl_ref[...] = (dmg * pl_ref[...].astype(F32) * sl * (1.0 - sl)).astype(BF16)
        dga_ref[...] = (dmg * pa_ref[...].astype(F32) * sa * (1.0 - sa)).astype(BF16)
        dyl_ref[...] = _dot_nt(dpl, wl_ref[...])
        dya_ref[...] = _dot_nt(dpa, wa_ref[...]).astype(BF16)

    row = _ROW(tm)
    return _call(
        body, name=name, grid=(SEQ // tm,),
        in_specs=[row, row, _VEC, ANY, ANY, ANY, row, row, row, row],
        out_specs=[row] * 7 + [_VEC],
        out_shape=[jax.ShapeDtypeStruct((SEQ, D_MODEL), BF16)] * 6 + [jax.ShapeDtypeStruct((SEQ, D_MODEL), F32),
                                                                       jax.ShapeDtypeStruct((1, D_MODEL), F32)],
        scratch_shapes=_proj_scratch(),
        args=[d_out, m, g_post, *projs, g_lru, g_attn, p_l, p_a], stages=stages)


def _rope_tables():
    half = HEAD_DIM // 2
    inv_freq = ROPE_THETA ** (-jnp.arange(half, dtype=F32) / half)
    ang = jnp.arange(SEQ, dtype=F32)[:, None] * inv_freq[None, :]
    cos, sin = jnp.cos(ang), jnp.sin(ang)
    return jnp.tile(jnp.concatenate([cos, cos], axis=1), (1, 2)), jnp.tile(jnp.concatenate([-sin, sin], axis=1), (1, 2))


def _block_diag(w):
    per = LRU_TC // LRU_BLOCK_W
    w4 = w.reshape(LRU_W // LRU_TC, per, LRU_BLOCK_W, LRU_BLOCK_W)
    eye = jnp.eye(per, dtype=w.dtype)
    return jnp.einsum('jacd,ab->jacbd', w4, eye).reshape(LRU_W // LRU_TC, LRU_TC, LRU_TC).astype(BF16)


def _diag_blocks(p):
    per = LRU_TC // LRU_BLOCK_W
    p5 = p.reshape(LRU_W // LRU_TC, per, LRU_BLOCK_W, per, LRU_BLOCK_W)
    return jnp.stack([p5[:, a, :, a, :] for a in range(per)], axis=1).reshape(LRU_W // LRU_BLOCK_W, LRU_BLOCK_W, LRU_BLOCK_W)


def _place():
    x, y, c = lax.axis_index('x'), lax.axis_index('y'), lax.axis_index('c')
    chips = [(1 - x, y), (x, 1 - y), (1 - x, 1 - y)]
    return x, y, c, chips


def _rcopy(src, dst, send_sem, recv_sem, to):
    return pltpu.make_async_remote_copy(src_ref=src, dst_ref=dst, send_sem=send_sem, recv_sem=recv_sem,
                                        device_id=to, device_id_type=MESH)


class _Stage:
    inputs, out_shape, scratch = (), (), ()

    def start(self, ins, outs, scr):
        plan = self._plan(ins, outs, scr)
        for ld in plan['loads']:
            ld.start()
        for cp in plan['sends']:
            cp.start()

    def mid(self, ins, outs, scr):
        plan = self._plan(ins, outs, scr)
        for ld, st in zip(plan['loads'], plan['stores']):
            ld.wait()
            st.start()
        for arrived, onward in zip(plan['arrivals'], plan['forwards']):
            arrived.wait_recv()
            onward.start()

    def end(self, ins, outs, scr):
        plan = self._plan(ins, outs, scr)
        for st in plan['stores']:
            st.wait()
        for arrived in (plan['final_arrivals'] if plan['forwards'] else plan['arrivals']):
            arrived.wait_recv()
        for cp in plan['sends'] + plan['forwards']:
            cp.wait_send()


def _empty_plan():
    return dict(loads=[], stores=[], sends=[], arrivals=[], forwards=[], final_arrivals=[])


class GatherStage(_Stage):
    SUB = 2

    def __init__(self, items):
        self.ranges = [(off, rows) for _, off, rows in items]
        self.inputs = [src for src, _, _ in items]
        self.out_shape = [jax.ShapeDtypeStruct((N_CHIPS, rows, D_MODEL), BF16) for _, rows in self.ranges]
        self.n_ici = 3 * self.SUB * len(items)
        self.scratch = [pltpu.VMEM((sum(r for _, r in self.ranges), D_MODEL), BF16), pltpu.SemaphoreType.DMA((2 * self.n_ici,)),
                        pltpu.SemaphoreType.DMA((2 * self.n_ici,)), pltpu.SemaphoreType.DMA((2 * len(items),))]

    def _plan(self, ins, outs, scr):
        buf, send, recv, lsem = scr
        x, y, c, chips = _place()
        me_q = 2 * x + y
        sib = (x, y, 1 - c)
        plan = _empty_plan()
        boff = 0
        for w, ((off, rows), p_ref, o_ref) in enumerate(zip(self.ranges, ins, outs)):
            hr = rows // 2
            ch = hr // self.SUB
            plan['loads'].append(pltpu.make_async_copy(p_ref.at[pl.ds(off, rows)], buf.at[pl.ds(boff, rows)], lsem.at[2 * w]))
            plan['stores'].append(pltpu.make_async_copy(buf.at[pl.ds(boff, rows)], o_ref.at[me_q], lsem.at[2 * w + 1]))
            boff += rows
            for k in range(self.SUB):
                mine = pl.ds(pl.multiple_of(c * hr + k * ch, 16), ch)
                theirs = pl.ds(pl.multiple_of((1 - c) * hr + k * ch, 16), ch)
                src = p_ref.at[pl.ds(pl.multiple_of(off + c * hr + k * ch, 16), ch)]
                for j, (cx, cy) in enumerate(chips):
                    i = (w * self.SUB + k) * 3 + j
                    got = o_ref.at[2 * cx + cy, mine]
                    got_sib = o_ref.at[2 * cx + cy, theirs]
                    plan['sends'].append(_rcopy(src, o_ref.at[me_q, mine], send.at[i], recv.at[i], (cx, cy, c)))
                    plan['arrivals'].append(_rcopy(got, got, send.at[i], recv.at[i], (cx, cy, c)))
                    plan['forwards'].append(_rcopy(got, got, send.at[self.n_ici + i], recv.at[self.n_ici + i], sib))
                    plan['final_arrivals'].append(
                        _rcopy(got_sib, got_sib, send.at[self.n_ici + i], recv.at[self.n_ici + i], sib))
        return plan


class PairStage(_Stage):
    def __init__(self, grads):
        self.inputs = list(grads)
        self.out_shape = [jax.ShapeDtypeStruct((N_CHIPS, 1) + g.shape[2:], BF16) for g in grads]
        n_cp = N_CHIPS * len(grads)
        self.scratch = [pltpu.SemaphoreType.DMA((n_cp,)), pltpu.SemaphoreType.DMA((n_cp,))]

    def _plan(self, ins, outs, scr):
        send, recv = scr
        x, y, c, _ = _place()
        plan = _empty_plan()
        for w, (g_ref, l_ref) in enumerate(zip(ins, outs)):
            for q in range(N_CHIPS):
                i = w * N_CHIPS + q
                plan['sends'].append(_rcopy(g_ref.at[q, pl.ds(1 - c, 1)], l_ref.at[q], send.at[i], recv.at[i], (x, y, 1 - c)))
        plan['arrivals'] = plan['sends']
        return plan


class ChipStage(_Stage):
    def __init__(self, items):
        self.ranges = [(off, n) for _, off, n in items]
        self.inputs = [s for s, _, _ in items]
        self.out_shape = [jax.ShapeDtypeStruct((N_CHIPS, n, D_MODEL), BF16) for _, n in self.ranges]
        n_cp = 3 * len(items)
        self.scratch = [pltpu.VMEM((sum(n for _, n in self.ranges), D_MODEL), BF16), pltpu.SemaphoreType.DMA((n_cp,)),
                        pltpu.SemaphoreType.DMA((n_cp,)), pltpu.SemaphoreType.DMA((2 * len(items),))]

    def _plan(self, ins, outs, scr):
        buf, send, recv, lsem = scr
        x, y, c, chips = _place()
        me_q = 2 * x + y
        plan = _empty_plan()
        boff = 0
        for w, ((off, n), s_ref, l_ref) in enumerate(zip(self.ranges, ins, outs)):
            rows = pl.ds(off, n)
            plan['loads'].append(pltpu.make_async_copy(s_ref.at[me_q, rows], buf.at[pl.ds(boff, n)], lsem.at[2 * w]))
            plan['stores'].append(pltpu.make_async_copy(buf.at[pl.ds(boff, n)], l_ref.at[me_q], lsem.at[2 * w + 1]))
            boff += n
            for j, (cx, cy) in enumerate(chips):
                i = w * 3 + j
                got = l_ref.at[2 * cx + cy]
                plan['sends'].append(_rcopy(s_ref.at[2 * cx + cy, rows], l_ref.at[me_q], send.at[i], recv.at[i], (cx, cy, c)))
                plan['arrivals'].append(_rcopy(got, got, send.at[i], recv.at[i], (cx, cy, c)))
        return plan


class SwapStage(_Stage):
    def __init__(self, items):
        n = len(items)
        self.inputs = list(items)
        self.out_shape = [jax.ShapeDtypeStruct((2,) + a.shape, a.dtype) for a in items]
        self.scratch = [pltpu.VMEM(a.shape, a.dtype) for a in items] + [
            pltpu.SemaphoreType.DMA((n,)), pltpu.SemaphoreType.DMA((n,)), pltpu.SemaphoreType.DMA((2 * n,))]

    def _plan(self, ins, outs, scr):
        bufs, (send, recv, lsem) = scr[:len(ins)], scr[len(ins):]
        x, y, c, _ = _place()
        plan = _empty_plan()
        for w, (h_ref, o_ref, buf) in enumerate(zip(ins, outs, bufs)):
            plan['loads'].append(pltpu.make_async_copy(h_ref, buf, lsem.at[2 * w]))
            plan['stores'].append(pltpu.make_async_copy(buf, o_ref.at[c], lsem.at[2 * w + 1]))
            got = o_ref.at[1 - c]
            plan['sends'].append(_rcopy(h_ref, o_ref.at[c], send.at[w], recv.at[w], (x, y, 1 - c)))
            plan['arrivals'].append(_rcopy(got, got, send.at[w], recv.at[w], (x, y, 1 - c)))
        return plan


class SmallGatherStage(_Stage):
    def __init__(self, blk):
        self.inputs = [blk]
        self.out_shape = [jax.ShapeDtypeStruct((N_DEV,) + blk.shape, blk.dtype)]
        self.scratch = [pltpu.VMEM(blk.shape, blk.dtype), pltpu.SemaphoreType.DMA((7,)), pltpu.SemaphoreType.DMA((7,)),
                        pltpu.SemaphoreType.DMA((2,))]

    def _plan(self, ins, outs, scr):
        (x_ref,), (o_ref,), (buf, send, recv, lsem) = ins, outs, scr
        x, y, c, chips = _place()
        sib = (x, y, 1 - c)

        def slot(px, py, pc):
            return o_ref.at[4 * px + 2 * py + pc]

        plan = _empty_plan()
        plan['loads'].append(pltpu.make_async_copy(x_ref, buf, lsem.at[0]))
        plan['stores'].append(pltpu.make_async_copy(buf, slot(x, y, c), lsem.at[1]))
        from_sib = slot(x, y, 1 - c)
        plan['sends'].append(_rcopy(x_ref, slot(x, y, c), send.at[0], recv.at[0], sib))
        plan['final_arrivals'].append(_rcopy(from_sib, from_sib, send.at[0], recv.at[0], sib))
        for j, (cx, cy) in enumerate(chips):
            got, got_sib = slot(cx, cy, c), slot(cx, cy, 1 - c)
            plan['sends'].append(_rcopy(x_ref, slot(x, y, c), send.at[1 + j], recv.at[1 + j], (cx, cy, c)))
            plan['arrivals'].append(_rcopy(got, got, send.at[1 + j], recv.at[1 + j], (cx, cy, c)))
            plan['forwards'].append(_rcopy(got, got, send.at[4 + j], recv.at[4 + j], sib))
            plan['final_arrivals'].append(_rcopy(got_sib, got_sib, send.at[4 + j], recv.at[4 + j], sib))
        return plan


_HBM = pl.BlockSpec(memory_space=pltpu.HBM)
_SEM = pl.BlockSpec(memory_space=pltpu.SEMAPHORE)
_DATAFLOW = pltpu.CompilerParams(has_side_effects=pltpu.SideEffectType.DATAFLOW_SIDE_EFFECTING)


def chip_exchange_start(s):
    def body(s_ref, land_ref, send, recv, s_thru, land_thru, token):
        x, y, c, chips = _place()
        for j, (cx, cy) in enumerate(chips):
            _rcopy(s_ref.at[2 * cx + cy], land_ref.at[2 * x + y], send.at[j], recv.at[j], (cx, cy, c)).start()
        token[...] = jnp.zeros_like(token)

    return pl.pallas_call(
        body, name='chip_exchange_start',
        out_shape=(pltpu.SemaphoreType.DMA((3,)), pltpu.SemaphoreType.DMA((3,)), pltpu.HBM(s.shape, s.dtype),
                   pltpu.HBM(s.shape, s.dtype), jax.ShapeDtypeStruct((8, 128), F32)),
        in_specs=(_HBM, _HBM), out_specs=(_SEM, _SEM, _HBM, _HBM, pl.BlockSpec(memory_space=pltpu.VMEM)),
        input_output_aliases={0: 2, 1: 3}, compiler_params=_DATAFLOW,
    )(pltpu.with_memory_space_constraint(s, pltpu.HBM),
      pltpu.with_memory_space_constraint(lax.empty(s.shape, s.dtype), pltpu.HBM))


def chip_exchange_wait(send, recv, s_thru, land_thru, after):
    def body(s_ref, land_ref, send_sem, recv_sem, after_ref, s_out, land_out):
        x, y, c, chips = _place()
        for j, (cx, cy) in enumerate(chips):
            cp = _rcopy(s_ref.at[2 * cx + cy], land_ref.at[2 * cx + cy], send_sem.at[j], recv_sem.at[j], (cx, cy, c))
            cp.wait_send()
            cp.wait_recv()

    return pl.pallas_call(
        body, name='chip_exchange_wait',
        out_shape=(pltpu.HBM(s_thru.shape, s_thru.dtype), pltpu.HBM(land_thru.shape, land_thru.dtype)),
        in_specs=(_HBM, _HBM, _SEM, _SEM, ANY), out_specs=(_HBM, _HBM),
        input_output_aliases={0: 0, 1: 1}, compiler_params=_DATAFLOW,
    )(s_thru, land_thru, send, recv, after)


def comm_call(name, stages):
    def body():
        pass

    return _call(body, name=name, grid=(1,), in_specs=[], out_specs=[], out_shape=[], args=[], stages=stages)[1]


def pair_sum(g4, land, c_arr, name):
    hr = g4.shape[2]

    def body(c_ref, g_ref, l_ref, o_ref):
        o_ref[0] = (g_ref[0, 0].astype(F32) + l_ref[0, 0].astype(F32)).astype(BF16)

    return pl.pallas_call(
        body, name=name,
        grid_spec=pltpu.PrefetchScalarGridSpec(
            num_scalar_prefetch=1, grid=(N_CHIPS,),
            in_specs=[pl.BlockSpec((1, 1, hr, D_MODEL), lambda q, c: (q, c[0], 0, 0)),
                      pl.BlockSpec((1, 1, hr, D_MODEL), lambda q, c: (q, 0, 0, 0))],
            out_specs=pl.BlockSpec((1, hr, D_MODEL), lambda q, c: (q, 0, 0))),
        out_shape=jax.ShapeDtypeStruct((N_CHIPS, hr, D_MODEL), BF16),
        compiler_params=_params(1),
    )(c_arr, g4, land)


def small_sum(vec_parts, lru_parts):
    def body(v_ref, l_ref, o_ref):
        for p_ref, lo, n in ((v_ref, 0, ROW_WA), (l_ref, ROW_WA, SMALL_ROWS - ROW_WA)):
            acc = p_ref[0]
            for s in range(1, N_DEV):
                acc = acc + p_ref[s]
            o_ref[lo:lo + n, :] = acc

    return pl.pallas_call(
        body, name='small_sum', grid=(1,),
        in_specs=[pl.BlockSpec(vec_parts.shape, lambda i: (0, 0, 0)), pl.BlockSpec(lru_parts.shape, lambda i: (0, 0, 0))],
        out_specs=pl.BlockSpec((SMALL_ROWS, D_MODEL), lambda i: (0, 0)),
        out_shape=jax.ShapeDtypeStruct((SMALL_ROWS, D_MODEL), F32),
        compiler_params=_params(1),
    )(vec_parts, lru_parts)


def _adam_math(w, g, m, v):
    m2 = ADAM_B1 * m + (1.0 - ADAM_B1) * g
    v2 = ADAM_B2 * v + (1.0 - ADAM_B2) * (g * g)
    m_hat = m2 / (1.0 - ADAM_B1 ** ADAM_STEP)
    v_hat = v2 / (1.0 - ADAM_B2 ** ADAM_STEP)
    delta = -ADAM_LR * (m_hat / (jnp.sqrt(v_hat) + ADAM_EPS) + ADAM_WD * w)
    return delta, m2, v2


def _adam_body(n_parts, transposed, n_after):
    def body(*refs):
        refs = refs[n_after:]
        g_refs = refs[:n_parts]
        w_ref, m_ref, v_ref, go_ref, d_ref, mo_ref, vo_ref = refs[n_parts:]
        def chips_added(blk):
            acc = blk[0].astype(F32)
            for s in range(1, N_CHIPS):
                acc = acc + blk[s].astype(F32)
            return acc

        if transposed:
            g = jnp.concatenate([chips_added(g_ref[h]) for h in range(2) for g_ref in g_refs], axis=0).T
        else:
            rows = [chips_added(g_ref[0]) for g_ref in g_refs]
            g = jnp.concatenate(rows, axis=0) if n_parts > 1 else rows[0]
        go_ref[...] = g
        d_ref[...], mo_ref[...], vo_ref[...] = _adam_math(w_ref[...], g, m_ref[...], v_ref[...])
    return body


def adam_rows(fulls, name, w, m, v, after=()):
    hr = w.shape[0] // 2
    blk = pl.BlockSpec((hr, D_MODEL), lambda h: (h, 0))
    return pl.pallas_call(
        _adam_body(len(fulls), False, len(after)), name='adam_' + name, grid=(2,),
        in_specs=[ANY] * len(after)
        + [pl.BlockSpec((1, N_CHIPS, f.shape[2], D_MODEL), lambda h: (h, 0, 0, 0)) for f in fulls] + [blk, blk, blk],
        out_specs=[blk] * 4,
        out_shape=[jax.ShapeDtypeStruct(w.shape, F32)] * 4,
        compiler_params=_params(1),
    )(*after, *fulls, w, m, v)


def adam_cols(fulls, name, w, m, v, after=()):
    cols = w.shape[1]
    tr = 128
    blk = pl.BlockSpec((tr, cols), lambda i: (i, 0))
    return pl.pallas_call(
        _adam_body(len(fulls), True, len(after)), name='adam_' + name, grid=(D_MODEL // tr,),
        in_specs=[ANY] * len(after)
        + [pl.BlockSpec((2, N_CHIPS, f.shape[2], tr), lambda i: (0, 0, 0, i)) for f in fulls] + [blk, blk, blk],
        out_specs=[blk] * 4,
        out_shape=[jax.ShapeDtypeStruct(w.shape, F32)] * 4,
        compiler_params=_params(1),
    )(*after, *fulls, w, m, v)


def adam_small(g, w, m, v):
    def body(g_ref, w_ref, m_ref, v_ref, d_ref, mo_ref, vo_ref):
        d_ref[...], mo_ref[...], vo_ref[...] = _adam_math(w_ref[...], g_ref[...], m_ref[...], v_ref[...])

    blk = pl.BlockSpec(w.shape, lambda i: (0, 0))
    return pl.pallas_call(
        body, name='adam_small', grid=(1,), in_specs=[blk] * 4, out_specs=[blk] * 3,
        out_shape=[jax.ShapeDtypeStruct(w.shape, F32)] * 3, compiler_params=_params(1),
    )(g, w, m, v)


WEIGHTS = ('ffn1_pre_g', 'ffn1_w_gu', 'ffn1_w_down', 'ffn1_post_g', 'mix_pre_g', 'w_in', 'conv_w', 'conv_b',
           'lru_w_a', 'lru_b_a', 'lru_w_x', 'lru_b_x', 'lru_lambda', 'attn_sinks', 'w_proj_lru', 'w_proj_attn',
           'w_out', 'mix_post_g', 'ffn2_pre_g', 'ffn2_w_gu', 'ffn2_w_down', 'ffn2_post_g')
SMALL = tuple(n for n in WEIGHTS if n not in PACK_OFF)


def _pack_vecs(d, conv_rows):
    sinks = jnp.pad(d['attn_sinks'].reshape(1, N_Q_HEADS), ((0, 0), (0, D_MODEL - N_Q_HEADS)))
    conv = jnp.pad(conv_rows, ((0, ROW_WA - ROW_CONV - conv_rows.shape[0]), (0, 0)))
    return jnp.concatenate([d[n].reshape(1, D_MODEL) for n in SMALL_VECS] + [sinks, conv], axis=0)


def _pack_lru(d):
    return jnp.concatenate([d['lru_w_a'].reshape(64, D_MODEL), d['lru_w_x'].reshape(64, D_MODEL)], axis=0)


def _pack_small(d, conv_rows):
    return jnp.concatenate([_pack_vecs(d, conv_rows), _pack_lru(d)], axis=0)


def _unpack_small(p, shapes):
    out = {n: p[k:k + 1].reshape(shapes[n]) for k, n in enumerate(SMALL_VECS)}
    out['attn_sinks'] = p[ROW_SINKS:ROW_SINKS + 1, :N_Q_HEADS].reshape(shapes['attn_sinks'])
    out['conv_w'] = p[ROW_CONV:ROW_CONV + 1].reshape(shapes['conv_w'])
    out['lru_w_a'] = p[ROW_WA:ROW_WA + 64].reshape(shapes['lru_w_a'])
    out['lru_w_x'] = p[ROW_WX:ROW_WX + 64].reshape(shapes['lru_w_x'])
    return out


def kernel(x, ffn1_pre_g, ffn1_w_gu, ffn1_w_down, ffn1_post_g, mix_pre_g, w_in, conv_w, conv_b, lru_w_a, lru_b_a, lru_w_x, lru_b_x, lru_lambda, attn_sinks, w_proj_lru, w_proj_attn, w_out, mix_post_g, ffn2_pre_g, ffn2_w_gu, ffn2_w_down, ffn2_post_g, loss_target, m_ffn1_pre_g, m_ffn1_w_gu, m_ffn1_w_down, m_ffn1_post_g, m_mix_pre_g, m_w_in, m_conv_w, m_conv_b, m_lru_w_a, m_lru_b_a, m_lru_w_x, m_lru_b_x, m_lru_lambda, m_attn_sinks, m_w_proj_lru, m_w_proj_attn, m_w_out, m_mix_post_g, m_ffn2_pre_g, m_ffn2_w_gu, m_ffn2_w_down, m_ffn2_post_g, v_ffn1_pre_g, v_ffn1_w_gu, v_ffn1_w_down, v_ffn1_post_g, v_mix_pre_g, v_w_in, v_conv_w, v_conv_b, v_lru_w_a, v_lru_b_a, v_lru_w_x, v_lru_b_x, v_lru_lambda, v_attn_sinks, v_w_proj_lru, v_w_proj_attn, v_w_out, v_mix_post_g, v_ffn2_pre_g, v_ffn2_w_gu, v_ffn2_w_down, v_ffn2_post_g):
    given = dict(locals())
    w = {n: given[n] for n in WEIGHTS}
    mom = {n: given['m_' + n] for n in WEIGHTS}
    var = {n: given['v_' + n] for n in WEIGHTS}
    shapes = {n: w[n].shape for n in WEIGHTS}
    xq = lax.axis_index('x')
    yq = lax.axis_index('y')
    cq = lax.axis_index('c')
    me_q = 2 * xq + yq

    c_arr = cq.reshape(1).astype(jnp.int32)
    xs, target = x[0], loss_target[0]
    sw = {n: (w[n][0] if w[n].ndim > 2 else w[n]) for n in SMALL}
    cos, sin_signed = _rope_tables()
    wa_bd = _block_diag(sw['lru_w_a'])
    wx_bd = _block_diag(sw['lru_w_x'])
    sinks = sw['attn_sinks'].reshape(N_Q_HEADS)

    shard = {n: (w[n][0].T if t else w[n][0]).astype(BF16) for n, _, t in PACK}
    conv_pad = jnp.pad(w['conv_w'][0], ((0, 4), (0, 0)))

    def whole(name):
        return (shard[name], 0, PACK_ROWS_OF[name])

    def part(name, p, n_parts=2):
        rows = PACK_ROWS_OF[name] // n_parts
        return (shard[name], p * rows, rows)

    (w_gu1,), (conv_all,) = comm_call('gather_first', [GatherStage([whole('ffn1_w_gu')]), SmallGatherStage(conv_pad)])
    sw['conv_w'] = jnp.transpose(conv_all[0::2, :4, :], (1, 0, 2)).reshape(4, LRU_W)
    proj_names = ['w_proj_lru', 'w_proj_attn', 'w_out']

    (n1, g1, u1, a1), ((w_down1,),) = ffn_fwd_a(xs, sw['ffn1_pre_g'], [w_gu1], 'ffn1_fwd_a',
                                                 stages=[GatherStage([whole('ffn1_w_down')])])
    (f1, h1), ((w_in_t,),) = ffn_fwd_b(a1, w_down1, sw['ffn1_post_g'], xs, 'ffn1_fwd_b', stages=[GatherStage([whole('w_in')])])
    (um, gate, xbr, q, k, v, g_lru, g_attn), ((w_gu2a,),) = mix_in(h1, sw['mix_pre_g'], w_in_t, 'mix_in',
                                                                   stages=[GatherStage([part('ffn2_w_gu', 0)])])
    (y_lru, h_lru), ((w_gu2b,),) = lru_fwd(gate, xbr, sw['conv_w'], sw['conv_b'], wa_bd, sw['lru_b_a'], wx_bd, sw['lru_b_x'],
                                           sw['lru_lambda'], 'lru_fwd', stages=[GatherStage([part('ffn2_w_gu', 1)])])
    (qr, kr, y_attn), (projs,) = attn_fwd(q, k, v, cos, sin_signed, sinks, 'attn_fwd',
                                          stages=[GatherStage([whole(n) for n in proj_names])])
    (p_l, p_a, merged, m, h2), ((w_down2,),) = merge_fwd(y_lru, y_attn, g_lru, g_attn, projs, sw['mix_post_g'], h1, 'merge_fwd',
                                                         stages=[GatherStage([whole('ffn2_w_down')])])
    w_gu2 = [w_gu2a, w_gu2b]
    (n2, g2, u2, a2), _ = ffn_fwd_a(h2, sw['ffn2_pre_g'], w_gu2, 'ffn2_fwd_a')
    (f2, dy, loss_blk), _ = ffn_fwd_b(a2, w_down2, sw['ffn2_post_g'], h2, 'ffn2_fwd_b', target=target)

    gs, full = {}, {}

    def pair_stage(names, grads):
        g4 = [g.reshape(N_CHIPS, 2, PACK_ROWS_OF[n] // 2, D_MODEL) for n, g in zip(names, grads)]
        return PairStage(g4), g4

    def pair_sums(names, g4, lands):
        return [pair_sum(g, l, c_arr, 'pair_sum_' + n) for n, g, l in zip(names, g4, lands)]

    def halves(s, n_parts=2):
        n = s.shape[1] // n_parts
        return [(s, p * n, n) for p in range(n_parts)]

    (df2, dgu2, gs['ffn2_post_g']), _ = ffn_bwd_a(dy, f2, sw['ffn2_post_g'], w_down2, g2, u2, 'ffn2_bwd_a')
    g_down2, _ = mm_tn([a2], df2, 1408, 'ffn2_dw_down')
    st, g4 = pair_stage(['ffn2_w_down'], [g_down2])
    g_gu2, (lands,) = mm_tn([dgu2], n2, 1408, 'ffn2_dw_gu', stages=[st])
    (s_down2,) = pair_sums(['ffn2_w_down'], g4, lands)
    st, g4 = pair_stage(['ffn2_w_gu'], [g_gu2])
    (dh2, gs['ffn2_pre_g']), ((l_down2,), lands) = norm_bwd([dgu2], w_gu2, h2, sw['ffn2_pre_g'], dy, 'ffn2_bwd_b',
                                                            stages=[ChipStage([(s_down2, 0, s_down2.shape[1])]), st])
    (s_gu2,) = pair_sums(['ffn2_w_gu'], g4, lands)

    (dm, dpl, dpa, dgl, dga, dya, dyl, gs['mix_post_g']), ((l_gu2a,),) = merge_bwd(
        dh2, m, sw['mix_post_g'], projs, g_lru, g_attn, p_l, p_a, 'merge_bwd', stages=[ChipStage(halves(s_gu2)[:1])])
    g_projs = [mm_tn([merged if n == 'w_out' else (y_lru if n == 'w_proj_lru' else y_attn)],
                     dm if n == 'w_out' else (dpl if n == 'w_proj_lru' else dpa), D_MODEL, 'd' + n)[0] for n in proj_names]
    st, g4 = pair_stage(proj_names, g_projs)
    (dq, dkv, dsk), ((l_gu2b,), lands, (full['ffn2_w_down'],)) = attn_bwd(
        qr, kr, v, dya, cos, sin_signed, sinks, 'attn_bwd', stages=[ChipStage(halves(s_gu2)[1:]), st, SwapStage([l_down2])])
    full['ffn2_w_down'] = [full['ffn2_w_down']]
    gs['attn_sinks'] = dsk[0:1, 0:N_Q_HEADS]
    s_projs = pair_sums(proj_names, g4, lands)
    (dgate, dxbr, vecs, dwa, dwx), (l_projs, full['ffn2_w_gu']) = lru_bwd(
        gate, xbr, h_lru, dyl, sw['conv_w'], sw['conv_b'], wa_bd, sw['lru_b_a'], wx_bd, sw['lru_b_x'], sw['lru_lambda'],
        'lru_bwd', stages=[ChipStage([(s, 0, s.shape[1]) for s in s_projs]), SwapStage([l_gu2a, l_gu2b])])
    gs['conv_w'] = vecs[0:4]
    gs['conv_b'], gs['lru_b_a'], gs['lru_b_x'], gs['lru_lambda'] = vecs[4:5], vecs[5:6], vecs[6:7], vecs[7:8]
    gs['lru_w_a'] = _diag_blocks(dwa)
    gs['lru_w_x'] = _diag_blocks(dwx)
    dz = [dgate, dxbr, dq, dkv, dgl, dga]
    g_in, ((lru_all,),) = mm_tn(dz, um, 512, 'dw_in', stages=[SmallGatherStage(_pack_lru(gs))])
    st, g4 = pair_stage(['w_in'], [g_in])
    (dh1, gs['mix_pre_g']), (lands, f_projs) = norm_bwd(dz, [w_in_t], h1, sw['mix_pre_g'], dh2, 'mix_bwd_in',
                                                        stages=[st, SwapStage(l_projs)])
    for n, f in zip(proj_names, f_projs):
        full[n] = [f]
    (s_in,) = pair_sums(['w_in'], g4, lands)

    (df1, dgu1, gs['ffn1_post_g']), ((l_in_a,),) = ffn_bwd_a(dh1, f1, sw['ffn1_post_g'], w_down1, g1, u1, 'ffn1_bwd_a',
                                                             stages=[ChipStage(halves(s_in)[:1])])
    g_down1, _ = mm_tn([a1], df1, 1408, 'ffn1_dw_down')
    st, g4 = pair_stage(['ffn1_w_down'], [g_down1])
    g_gu1, ((l_in_b,), lands) = mm_tn([dgu1], n1, 1408, 'ffn1_dw_gu', stages=[ChipStage(halves(s_in)[1:]), st])
    (s_down1,) = pair_sums(['ffn1_w_down'], g4, lands)
    st, g4 = pair_stage(['ffn1_w_gu'], [g_gu1])
    (dx, gs['ffn1_pre_g']), ((l_down1,), lands, full['w_in']) = norm_bwd(
        [dgu1], [w_gu1], xs, sw['ffn1_pre_g'], dh1, 'ffn1_bwd_b',
        stages=[ChipStage([(s_down1, 0, s_down1.shape[1])]), st, SwapStage([l_in_a, l_in_b])])
    (s_gu1,) = pair_sums(['ffn1_w_gu'], g4, lands)
    loss_row = jnp.pad(loss_blk[0:1], ((0, 0), (0, D_MODEL - loss_blk.shape[1])))
    vec_blk = _pack_vecs(gs, jnp.concatenate([gs['conv_w'], loss_row], axis=0))
    send, recv, s_thru, land_thru, token = chip_exchange_start(s_gu1)
    out_g, out_d, out_m, out_v = {}, {}, {}, {}

    def adam(n, after=()):
        fn = adam_cols if dict((k, t) for k, _, t in PACK)[n] else adam_rows
        g_, d_, m_, v_ = fn(full[n], n, w[n][0], mom[n][0], var[n][0], after=after)
        out_g[n], out_d[n], out_m[n], out_v[n] = g_[None], d_[None], m_[None], v_[None]

    behind = token
    for n in ['ffn2_w_gu', 'w_in', 'ffn2_w_down'] + proj_names:
        adam(n, after=(behind,))
        behind = out_v[n]
    s_back, l_gu1 = chip_exchange_wait(send, recv, s_thru, land_thru, after=behind)
    own = lax.dynamic_slice_in_dim(s_back, me_q, 1, axis=0)
    l_gu1 = lax.dynamic_update_slice_in_dim(l_gu1, own, me_q, axis=0)
    (vec_all,), (f_down1, f_gu1) = comm_call('swap_last', [SmallGatherStage(vec_blk), SwapStage([l_down1, l_gu1])])
    full['ffn1_w_down'] = [f_down1]
    full['ffn1_w_gu'] = [f_gu1]
    adam('ffn1_w_gu')
    adam('ffn1_w_down')

    tot = small_sum(vec_all, lru_all)
    loss = tot[ROW_WA - 1, 0]
    conv_g = lax.dynamic_slice(tot[ROW_CONV:ROW_CONV + 4], (0, me_q * (LRU_W // N_CHIPS)), (4, LRU_W // N_CHIPS))
    small_g = _unpack_small(tot, shapes)
    small_g['conv_w'] = conv_g.reshape(shapes['conv_w'])
    g_pack = jnp.concatenate([tot[:ROW_CONV], conv_g.reshape(1, D_MODEL), jnp.zeros((ROW_WA - ROW_CONV - 1, D_MODEL), F32),
                              tot[ROW_WA:]], axis=0)
    packs = [_pack_small({n: d[n] for n in SMALL}, d['conv_w'].reshape(1, D_MODEL)) for d in (w, mom, var)]
    d_p, m_p, v_p = adam_small(g_pack, *packs)
    for n in SMALL:
        out_g[n] = small_g[n]
    for dst, p in ((out_d, d_p), (out_m, m_p), (out_v, v_p)):
        dst.update(_unpack_small(p, shapes))

    return (loss, dx[None], *[out_g[n] for n in WEIGHTS], *[out_d[n] for n in WEIGHTS],
            *[out_m[n] for n in WEIGHTS], *[out_v[n] for n in WEIGHTS])
```

```python
import jax
import jax.numpy as jnp
import numpy as np
from jax import lax
from jax.experimental import pallas as pl
from jax.experimental.pallas import tpu as pltpu

F32 = jnp.float32
BF16 = jnp.bfloat16

SEQ = 2048
D_MODEL = 1024
D_FF = 2816
LRU_W = 1024
LRU_BLOCK_W = 64
HEAD_DIM = 64
N_Q_HEADS = 16
N_KV_HEADS = 4
KV_W = N_KV_HEADS * HEAD_DIM
ATTN_BLOCK = 128
N_ATTN_BLOCKS = SEQ // ATTN_BLOCK
IN_SEGS = (1024, 1024, 1024, 256, 256, 1024, 1024)
IN_W = sum(IN_SEGS)
NORM_EPS = 1e-6
MASK_VALUE = -1e30
ROPE_THETA = 10000.0
LRU_C = 8.0
MACARON = 0.5
ADAM_LR = 0.001
ADAM_B1 = 0.9
ADAM_B2 = 0.999
ADAM_EPS = 1e-08
ADAM_WD = 0.01
ADAM_STEP = 10

N_CHIPS = 4
N_DEV = 8
VMEM_LIMIT = 56 * 1024 * 1024
MM_ROWS = 256
MESH = pl.DeviceIdType.MESH
ANY = pl.BlockSpec(memory_space=pl.ANY)

PACK = (('ffn1_w_gu', 1408, True), ('w_in', 1408, True), ('ffn2_w_gu', 1408, True),
        ('ffn1_w_down', 704, False), ('ffn2_w_down', 704, False),
        ('w_proj_lru', 256, False), ('w_proj_attn', 256, False), ('w_out', 256, False))
PACK_ROWS_OF = {n: r for n, r, _ in PACK}
PACK_OFF = {}
_o = 0
for _n, _r, _t in PACK:
    PACK_OFF[_n] = _o
    _o += _r

SMALL_VECS = ('ffn1_pre_g', 'ffn1_post_g', 'mix_pre_g', 'conv_b', 'lru_b_a', 'lru_b_x', 'lru_lambda',
              'mix_post_g', 'ffn2_pre_g', 'ffn2_post_g')
SMALL_ROWS = 144
ROW_SINKS, ROW_CONV, ROW_WA, ROW_WX = 10, 11, 16, 80


def _dot(a, b):
    return jnp.dot(a, b, preferred_element_type=F32)


def _dot_nt(a, b):
    return lax.dot_general(a, b, (((1,), (1,)), ((), ())), preferred_element_type=F32)


def _dot_tn(a, b):
    return lax.dot_general(a, b, (((0,), (0,)), ((), ())), preferred_element_type=F32)


def _params(n_grid):
    return pltpu.CompilerParams(dimension_semantics=("arbitrary",) * n_grid, vmem_limit_bytes=VMEM_LIMIT)


def _sigmoid(x):
    return 1.0 / (1.0 + jnp.exp(-x))


def _rsqrt_mean_sq(x):
    return lax.rsqrt(jnp.mean(x * x, axis=-1, keepdims=True) + NORM_EPS)


def _expm1(x):
    poly = x * (1.0 + x * (0.5 + x * (1.0 / 6.0 + x * (1.0 / 24.0 + x * (1.0 / 120.0)))))
    return jnp.where(jnp.abs(x) < 0.1, poly, jnp.exp(x) - 1.0)


_GELU_K = 0.7978845608028654
_GELU_C = 0.044715


def _gelu(x):
    t = jnp.tanh(_GELU_K * (x + _GELU_C * x * x * x))
    return 0.5 * x * (1.0 + t), t


def _gelu_grad(x, t):
    return 0.5 * (1.0 + t) + 0.5 * x * (1.0 - t * t) * _GELU_K * (1.0 + 3.0 * _GELU_C * x * x)


def _load_weight(w_refs, dst_ref, sem):
    w_refs = list(w_refs) if isinstance(w_refs, (list, tuple)) else [w_refs]
    rows = dst_ref.shape[0] // N_CHIPS
    rp = rows // len(w_refs)
    cps = [pltpu.make_async_copy(w_ref.at[q], dst_ref.at[pl.ds(q * rows + p * rp, rp)], sem.at[p * N_CHIPS + q])
           for p, w_ref in enumerate(w_refs) for q in range(N_CHIPS)]
    for cp in cps:
        cp.start()
    for cp in cps:
        cp.wait()


def _weight_scratch(rows_total, parts=1):
    return [pltpu.VMEM((rows_total, D_MODEL), BF16), pltpu.SemaphoreType.DMA((N_CHIPS * parts,))]


_ROW = lambda tm: pl.BlockSpec((tm, D_MODEL), lambda i: (i, 0))
_VEC = pl.BlockSpec((1, D_MODEL), lambda i: (0, 0))


def _call(body, *, name, grid, in_specs, out_specs, out_shape, args, scratch_shapes=(), stages=()):
    in_specs, out_specs, out_shape, scratch_shapes = list(in_specs), list(out_specs), list(out_shape), list(scratch_shapes)
    n_in, n_out, n_sc = len(in_specs), len(out_specs), len(scratch_shapes)
    k_in = [len(s.inputs) for s in stages]
    k_out = [len(s.out_shape) for s in stages]
    k_sc = [len(s.scratch) for s in stages]
    last = grid[0] - 1

    def split(refs, counts):
        parts, pos = [], 0
        for k in counts:
            parts.append(refs[pos:pos + k])
            pos += k
        return parts

    kinds = tuple(sorted({k for s in stages for k in s.peers}))
    collective_id = {(): None, ('sib',): 0, ('chips',): 1, ('chips', 'sib'): 2}[kinds]

    def full(*refs):
        ins, s_ins, outs, s_outs, scr, s_scr = split(refs, [n_in, sum(k_in), n_out, sum(k_out), n_sc, sum(k_sc)])
        per_stage = list(zip(stages, split(s_ins, k_in), split(s_outs, k_out), split(s_scr, k_sc)))
        i = pl.program_id(0)
        if stages:
            @pl.when(i == 0)
            def _():
                x, y, c, chips = _place()
                peers = ([(x, y, 1 - c)] if 'sib' in kinds else []) + ([(cx, cy, c) for cx, cy in chips] if 'chips' in kinds else [])
                barrier = pltpu.get_barrier_semaphore()
                for peer in peers:
                    pl.semaphore_signal(barrier, inc=1, device_id=peer, device_id_type=MESH)
                pl.semaphore_wait(barrier, len(peers))
                for s, a, b, c_ in per_stage:
                    s.start(a, b, c_)

        body(*ins, *outs, *scr)
        if stages:
            @pl.when(i == max(last - 1, 0))
            def _():
                for s, a, b, c in per_stage:
                    s.mid(a, b, c)

            @pl.when(i == last)
            def _():
                for s, a, b, c in per_stage:
                    s.end(a, b, c)

    res = pl.pallas_call(
        full, name=name, grid=grid,
        in_specs=in_specs + [ANY] * sum(k_in),
        out_specs=out_specs + [ANY] * sum(k_out),
        out_shape=out_shape + [o for s in stages for o in s.out_shape],
        scratch_shapes=scratch_shapes + [x for s in stages for x in s.scratch],
        compiler_params=pltpu.CompilerParams(dimension_semantics=("arbitrary",), vmem_limit_bytes=VMEM_LIMIT,
                                             collective_id=collective_id),
    )(*args, *[a for s in stages for a in s.inputs])
    return list(res[:n_out]), split(list(res[n_out:]), k_out)


def ffn_fwd_a(x, g_pre, w_gu_t, name, stages=()):
    tm, tn = MM_ROWS, 256
    n_w = len(w_gu_t)

    def body(x_ref, gp_ref, *refs):
        w_refs = refs[:n_w]
        n_ref, g_ref, u_ref, a_ref, wt_ref, sem = refs[n_w:]

        @pl.when(pl.program_id(0) == 0)
        def _():
            _load_weight(w_refs, wt_ref, sem)

        xv = x_ref[...]
        n = (xv * _rsqrt_mean_sq(xv) * gp_ref[...]).astype(BF16)
        n_ref[...] = n
        for j in range(D_FF // tn):
            g = _dot_nt(n, wt_ref[j * tn:(j + 1) * tn, :])
            u = _dot_nt(n, wt_ref[D_FF + j * tn:D_FF + (j + 1) * tn, :])
            g_ref[:, j * tn:(j + 1) * tn] = g.astype(BF16)
            u_ref[:, j * tn:(j + 1) * tn] = u.astype(BF16)
            a_ref[:, j * tn:(j + 1) * tn] = (g * _sigmoid(g) * u).astype(BF16)

    wide = pl.BlockSpec((tm, D_FF), lambda i: (i, 0))
    return _call(
        body, name=name, grid=(SEQ // tm,),
        in_specs=[_ROW(tm), _VEC] + [ANY] * n_w,
        out_specs=[_ROW(tm), wide, wide, wide],
        out_shape=[jax.ShapeDtypeStruct((SEQ, D_MODEL), BF16)] + [jax.ShapeDtypeStruct((SEQ, D_FF), BF16)] * 3,
        scratch_shapes=_weight_scratch(2 * D_FF, n_w),
        args=[x, g_pre, *w_gu_t], stages=stages)


def ffn_fwd_b(a, w_down, g_post, h_in, name, target=None, stages=()):
    tm = MM_ROWS
    final = target is not None

    def body(*refs):
        if final:
            a_ref, wf_ref, gp_ref, h_ref, t_ref, f_ref, o_ref, loss_ref, wd_ref, sem = refs
        else:
            a_ref, wf_ref, gp_ref, h_ref, f_ref, o_ref, wd_ref, sem = refs

        @pl.when(pl.program_id(0) == 0)
        def _():
            _load_weight(wf_ref, wd_ref, sem)
            if final:
                loss_ref[...] = jnp.zeros_like(loss_ref)

        f = _dot(a_ref[...], wd_ref[...])
        f_ref[...] = f
        y = h_ref[...] + MACARON * (f * _rsqrt_mean_sq(f) * gp_ref[...])
        if final:
            err = y - t_ref[...]
            o_ref[...] = err * (1.0 / D_MODEL)
            loss_ref[...] += 0.5 * jnp.sum(err * err) * (1.0 / D_MODEL)
        else:
            o_ref[...] = y

    row = _ROW(tm)
    in_specs = [pl.BlockSpec((tm, D_FF), lambda i: (i, 0)), ANY, _VEC, row]
    out_specs = [row, row]
    out_shape = [jax.ShapeDtypeStruct((SEQ, D_MODEL), F32)] * 2
    args = [a, w_down, g_post, h_in]
    if final:
        in_specs.append(row)
        args.append(target)
        out_specs.append(pl.BlockSpec((8, 128), lambda i: (0, 0)))
        out_shape.append(jax.ShapeDtypeStruct((8, 128), F32))
    return _call(body, name=name, grid=(SEQ // tm,), in_specs=in_specs, out_specs=out_specs,
                 out_shape=out_shape, scratch_shapes=_weight_scratch(D_FF), args=args, stages=stages)


def ffn_bwd_a(d_out, f, g_post, w_down, g, u, name, stages=()):
    tm = MM_ROWS
    tc = 256

    def body(do_ref, f_ref, gp_ref, wf_ref, g_ref, u_ref, df_ref, dgu_ref, dgp_ref, wd_ref, sem):
        @pl.when(pl.program_id(0) == 0)
        def _():
            _load_weight(wf_ref, wd_ref, sem)
            dgp_ref[...] = jnp.zeros_like(dgp_ref)

        fv = f_ref[...]
        rf = _rsqrt_mean_sq(fv)
        fh = fv * rf
        dn = MACARON * do_ref[...]
        dgp_ref[...] += jnp.sum(dn * fh, axis=0, keepdims=True)
        t = dn * gp_ref[...]
        df = (rf * (t - fh * jnp.mean(t * fh, axis=-1, keepdims=True))).astype(BF16)
        df_ref[...] = df
        for c0 in range(0, D_FF, tc):
            da = _dot_nt(df, wd_ref[c0:c0 + tc, :])
            gv = g_ref[:, c0:c0 + tc].astype(F32)
            uv = u_ref[:, c0:c0 + tc].astype(F32)
            s = _sigmoid(gv)
            dgu_ref[:, c0:c0 + tc] = (da * uv * s * (1.0 + gv * (1.0 - s))).astype(BF16)
            dgu_ref[:, D_FF + c0:D_FF + c0 + tc] = (da * gv * s).astype(BF16)

    row = _ROW(tm)
    wide = pl.BlockSpec((tm, D_FF), lambda i: (i, 0))
    return _call(
        body, name=name, grid=(SEQ // tm,),
        in_specs=[row, row, _VEC, ANY, wide, wide],
        out_specs=[row, pl.BlockSpec((tm, 2 * D_FF), lambda i: (i, 0)), _VEC],
        out_shape=[jax.ShapeDtypeStruct((SEQ, D_MODEL), BF16), jax.ShapeDtypeStruct((SEQ, 2 * D_FF), BF16),
                   jax.ShapeDtypeStruct((1, D_MODEL), F32)],
        scratch_shapes=_weight_scratch(D_FF),
        args=[d_out, f, g_post, w_down, g, u], stages=stages)


def norm_bwd(pieces, w_t, x, g_pre, d_res, name, stages=()):
    tm = MM_ROWS
    widths = [p.shape[1] for p in pieces]
    offs = [sum(widths[:k]) for k in range(len(widths))]
    n_p = len(pieces)
    n_w = len(w_t)

    def body(*refs):
        p_refs = refs[:n_p]
        w_refs = refs[n_p:n_p + n_w]
        x_ref, g_ref, r_ref, dx_ref, dg_ref, wt_ref, sem = refs[n_p + n_w:]

        @pl.when(pl.program_id(0) == 0)
        def _():
            _load_weight(w_refs, wt_ref, sem)
            dg_ref[...] = jnp.zeros_like(dg_ref)

        dn = None
        for p_ref, lo, wd in zip(p_refs, offs, widths):
            part = _dot(p_ref[...], wt_ref[lo:lo + wd, :])
            dn = part if dn is None else dn + part
        xv = x_ref[...]
        r = _rsqrt_mean_sq(xv)
        xh = xv * r
        dg_ref[...] += jnp.sum(dn * xh, axis=0, keepdims=True)
        t = dn * g_ref[...]
        dx_ref[...] = r_ref[...] + r * (t - xh * jnp.mean(t * xh, axis=-1, keepdims=True))

    row = _ROW(tm)
    return _call(
        body, name=name, grid=(SEQ // tm,),
        in_specs=[pl.BlockSpec((tm, wd), lambda i: (i, 0)) for wd in widths] + [ANY] * n_w + [row, _VEC, row],
        out_specs=[row, _VEC],
        out_shape=[jax.ShapeDtypeStruct((SEQ, D_MODEL), F32), jax.ShapeDtypeStruct((1, D_MODEL), F32)],
        scratch_shapes=_weight_scratch(sum(widths), n_w),
        args=[*pieces, *w_t, x, g_pre, d_res], stages=stages)


def mm_tn(pieces, b, tm, name, stages=()):
    widths = [p.shape[1] for p in pieces]
    m_total = sum(widths)
    n_p = len(pieces)
    starts = [sum(widths[:k]) // tm for k in range(n_p)]
    counts = [wd // tm for wd in widths]

    def body(*refs):
        p_refs = refs[:n_p]
        b_ref, o_ref = refs[n_p:]
        i = pl.program_id(0)
        for p_ref, st, ct in zip(p_refs, starts, counts):
            @pl.when((i >= st) & (i < st + ct))
            def _(p_ref=p_ref):
                o_ref[...] = _dot_tn(p_ref[...], b_ref[...]).astype(BF16)

    def piece_spec(st, ct):
        return pl.BlockSpec((SEQ, tm), lambda i: (0, jnp.clip(i - st, 0, ct - 1)))

    (out,), stage_out = _call(
        body, name=name, grid=(m_total // tm,),
        in_specs=[piece_spec(st, ct) for st, ct in zip(starts, counts)] + [pl.BlockSpec((SEQ, D_MODEL), lambda i: (0, 0))],
        out_specs=[pl.BlockSpec((tm, D_MODEL), lambda i: (i, 0))],
        out_shape=[jax.ShapeDtypeStruct((m_total, D_MODEL), BF16)],
        args=[*pieces, b], stages=stages)
    return out, stage_out


def mix_in(h, g_pre, w_in_t, name, stages=()):
    tm = MM_ROWS
    offs = [sum(IN_SEGS[:k]) for k in range(len(IN_SEGS))]
    dts = [F32, F32, F32, F32, BF16, F32, F32]
    n_o = len(IN_SEGS)

    def body(*refs):
        h_ref, g_ref, wf_ref, um_ref = refs[:4]
        o_refs = refs[4:4 + n_o]
        wt_ref, sem = refs[4 + n_o:]

        @pl.when(pl.program_id(0) == 0)
        def _():
            _load_weight(wf_ref, wt_ref, sem)

        hv = h_ref[...]
        um = (hv * _rsqrt_mean_sq(hv) * g_ref[...]).astype(BF16)
        um_ref[...] = um
        for o_ref, lo, wd in zip(o_refs, offs, IN_SEGS):
            for c0 in range(0, wd, 256):
                o_ref[:, c0:c0 + 256] = _dot_nt(um, wt_ref[lo + c0:lo + c0 + 256, :]).astype(o_ref.dtype)

    return _call(
        body, name=name, grid=(SEQ // tm,),
        in_specs=[_ROW(tm), _VEC, ANY],
        out_specs=[_ROW(tm)] + [pl.BlockSpec((tm, wd), lambda i: (i, 0)) for wd in IN_SEGS],
        out_shape=[jax.ShapeDtypeStruct((SEQ, D_MODEL), BF16)]
        + [jax.ShapeDtypeStruct((SEQ, wd), dt) for wd, dt in zip(IN_SEGS, dts)],
        scratch_shapes=_weight_scratch(IN_W),
        args=[h, g_pre, w_in_t], stages=stages)


LRU_TC = 256


def _conv_fwd(xb, cw, cb, tt):
    xc = xb * cw[3:4, :] + cb
    shifted = []
    for s in (1, 2, 3):
        sh = jnp.where(tt >= s, pltpu.roll(xb, s, 0), 0.0)
        shifted.append(sh)
        xc = xc + sh * cw[3 - s:4 - s, :]
    return xc, shifted


def _lru_gates(xc, wa, ba, wx, bx, lam):
    xcb = xc.astype(BF16)
    r = _sigmoid(_dot(xcb, wa) + ba)
    i = _sigmoid(_dot(xcb, wx) + bx)
    nl = -lam
    sp = jnp.maximum(nl, 0.0) + jnp.log1p(jnp.exp(-jnp.abs(nl)))
    la = (-LRU_C * r) * sp
    a = jnp.exp(la)
    mult = jnp.sqrt(jnp.maximum(-_expm1(2.0 * la), 0.0))
    return xcb, r, i, sp, a, mult


def _scan(a, b, tt, reverse):
    n = a.shape[0]
    s = 1
    while s < n:
        more = 2 * s < n
        if s < 8:
            if reverse:
                keep = tt < n - s
                shift = n - s
            else:
                keep = tt >= s
                shift = s
            b = a * jnp.where(keep, pltpu.roll(b, shift, 0), 0.0) + b
            if more:
                a = a * jnp.where(keep, pltpu.roll(a, shift, 0), 1.0)
        elif reverse:
            b = jnp.concatenate([a[:n - s] * b[s:] + b[:n - s], b[n - s:]], axis=0)
            if more:
                a = jnp.concatenate([a[:n - s] * a[s:], a[n - s:]], axis=0)
        else:
            b = jnp.concatenate([b[:s], a[s:] * b[:n - s] + b[s:]], axis=0)
            if more:
                a = jnp.concatenate([a[:s], a[s:] * a[:n - s]], axis=0)
        s *= 2
    return b


def _lru_specs():
    col = pl.BlockSpec((SEQ, LRU_TC), lambda j: (0, j))
    vec = pl.BlockSpec((1, LRU_TC), lambda j: (0, j))
    bd = pl.BlockSpec((1, LRU_TC, LRU_TC), lambda j: (j, 0, 0))
    cw = pl.BlockSpec((4, LRU_TC), lambda j: (0, j))
    return col, vec, bd, cw


def lru_fwd(gate, xbr, conv_w, conv_b, wa_bd, b_a, wx_bd, b_x, lam, name, stages=()):
    col, vec, bd, cw = _lru_specs()

    def body(gate_ref, xbr_ref, cw_ref, cb_ref, wa_ref, ba_ref, wx_ref, bx_ref, lam_ref, y_ref, h_ref):
        tt = lax.broadcasted_iota(jnp.int32, (SEQ, LRU_TC), 0)
        xc, _ = _conv_fwd(xbr_ref[...], cw_ref[...], cb_ref[...], tt)
        _, r, i, sp, a, mult = _lru_gates(xc, wa_ref[0], ba_ref[...], wx_ref[0], bx_ref[...], lam_ref[...])
        h = _scan(a, mult * (i * xc), tt, reverse=False)
        h_ref[...] = h
        gl, _ = _gelu(gate_ref[...])
        y_ref[...] = (h * gl).astype(BF16)

    return _call(
        body, name=name, grid=(LRU_W // LRU_TC,),
        in_specs=[col, col, cw, vec, bd, vec, bd, vec, vec],
        out_specs=[col, col],
        out_shape=[jax.ShapeDtypeStruct((SEQ, LRU_W), BF16), jax.ShapeDtypeStruct((SEQ, LRU_W), F32)],
        args=[gate, xbr, conv_w, conv_b, wa_bd, b_a, wx_bd, b_x, lam], stages=stages)


def lru_bwd(gate, xbr, h, dy, conv_w, conv_b, wa_bd, b_a, wx_bd, b_x, lam, name, stages=()):
    col, vec, bd, cw = _lru_specs()

    def body(gate_ref, xbr_ref, h_ref, dy_ref, cw_ref, cb_ref, wa_ref, ba_ref, wx_ref, bx_ref, lam_ref,
             dgate_ref, dxbr_ref, vecs_ref, dwa_ref, dwx_ref):
        tt = lax.broadcasted_iota(jnp.int32, (SEQ, LRU_TC), 0)
        cwv = cw_ref[...]
        lam = lam_ref[...]
        xb = xbr_ref[...]
        xc, shifted = _conv_fwd(xb, cwv, cb_ref[...], tt)
        wa = wa_ref[0]
        wx = wx_ref[0]
        xcb, r, i, sp, a, mult = _lru_gates(xc, wa, ba_ref[...], wx, bx_ref[...], lam)
        hv = h_ref[...]
        dyv = dy_ref[...]
        gv = gate_ref[...]
        gl, th = _gelu(gv)
        dgate_ref[...] = (dyv * hv * _gelu_grad(gv, th)).astype(BF16)
        a_next = jnp.where(tt < SEQ - 1, pltpu.roll(a, SEQ - 1, 0), 0.0)
        gsum = _scan(a_next, dyv * gl, tt, reverse=True)
        h_prev = jnp.where(tt >= 1, pltpu.roll(hv, 1, 0), 0.0)
        d_mult = gsum * i * xc
        d_i = gsum * mult * xc
        d_xc = gsum * mult * i
        d_la = gsum * h_prev * a - d_mult * (a * a) / mult
        d_pr = (d_la * (-LRU_C * sp)) * r * (1.0 - r)
        d_pi = d_i * i * (1.0 - i)
        d_lam = jnp.sum(d_la * r, axis=0, keepdims=True) * (LRU_C * _sigmoid(-lam))
        d_prb = d_pr.astype(BF16)
        d_pib = d_pi.astype(BF16)
        d_xc = d_xc + _dot_nt(d_prb, wa) + _dot_nt(d_pib, wx)
        dwa_ref[0] = _dot_tn(xcb, d_prb)
        dwx_ref[0] = _dot_tn(xcb, d_pib)
        rows = [jnp.sum(d_xc * shifted[2], axis=0, keepdims=True),
                jnp.sum(d_xc * shifted[1], axis=0, keepdims=True),
                jnp.sum(d_xc * shifted[0], axis=0, keepdims=True),
                jnp.sum(d_xc * xb, axis=0, keepdims=True),
                jnp.sum(d_xc, axis=0, keepdims=True),
                jnp.sum(d_pr, axis=0, keepdims=True),
                jnp.sum(d_pi, axis=0, keepdims=True),
                d_lam]
        ri = lax.broadcasted_iota(jnp.int32, (8, LRU_TC), 0)
        acc = jnp.zeros((8, LRU_TC), F32)
        for k, rv in enumerate(rows):
            acc = jnp.where(ri == k, rv, acc)
        vecs_ref[...] = acc
        d_xb = d_xc * cwv[3:4, :]
        for s in (1, 2, 3):
            d_xb = d_xb + jnp.where(tt < SEQ - s, pltpu.roll(d_xc, SEQ - s, 0), 0.0) * cwv[3 - s:4 - s, :]
        dxbr_ref[...] = d_xb.astype(BF16)

    return _call(
        body, name=name, grid=(LRU_W // LRU_TC,),
        in_specs=[col, col, col, col, cw, vec, bd, vec, bd, vec, vec],
        out_specs=[col, col, pl.BlockSpec((8, LRU_TC), lambda j: (0, j)), bd, bd],
        out_shape=[jax.ShapeDtypeStruct((SEQ, LRU_W), BF16), jax.ShapeDtypeStruct((SEQ, LRU_W), BF16),
                   jax.ShapeDtypeStruct((8, LRU_W), F32),
                   jax.ShapeDtypeStruct((LRU_W // LRU_TC, LRU_TC, LRU_TC), F32),
                   jax.ShapeDtypeStruct((LRU_W // LRU_TC, LRU_TC, LRU_TC), F32)],
        args=[gate, xbr, h, dy, conv_w, conv_b, wa_bd, b_a, wx_bd, b_x, lam], stages=stages)


def _rope(x, cos, sin_signed):
    w = x.shape[1]
    reps = w // 128
    if reps > 1:
        cos = jnp.tile(cos, (1, reps))
        sin_signed = jnp.tile(sin_signed, (1, reps))
    lane = lax.broadcasted_iota(jnp.int32, x.shape, 1)
    first = (lane & 63) < 32
    partner = jnp.where(first, pltpu.roll(x, w - 32, 1), pltpu.roll(x, 32, 1))
    return x * cos + partner * sin_signed


def _both_halves(t, odd):
    lo = lax.broadcasted_iota(jnp.int32, t.shape, 1) < 64
    rolled = pltpu.roll(t, 64, 1)
    return jnp.where(lo, rolled, t) if odd else jnp.where(lo, t, rolled)


def _stack_heads(ta, tb):
    lo = lax.broadcasted_iota(jnp.int32, ta.shape, 1) < 64
    return jnp.concatenate([jnp.where(lo, ta, 0.0), jnp.where(lo, 0.0, ta),
                            jnp.where(lo, tb, 0.0), jnp.where(lo, 0.0, tb)], axis=0)


def _unstack_heads(o):
    lo = lax.broadcasted_iota(jnp.int32, (ATTN_BLOCK, 128), 1) < 64
    return (jnp.where(lo, o[0:128], o[128:256]), jnp.where(lo, o[256:384], o[384:512]))


def _attn_probs(qs, kd, sinks_ref, hk, first_block):
    s = _dot_nt(qs, kd) * (HEAD_DIM ** -0.5)
    row = lax.broadcasted_iota(jnp.int32, s.shape, 0)
    si = lax.broadcasted_iota(jnp.int32, s.shape, 1)
    diff = ATTN_BLOCK + (row & (ATTN_BLOCK - 1)) - si
    valid = (diff >= 0) & (diff < ATTN_BLOCK) & ((si >= ATTN_BLOCK) | jnp.logical_not(first_block))
    s = jnp.where(valid, s, MASK_VALUE)
    rg = lax.broadcasted_iota(jnp.int32, (4 * ATTN_BLOCK, 1), 0) >> 7
    sink = jnp.where(rg == 0, sinks_ref[4 * hk],
                     jnp.where(rg == 1, sinks_ref[4 * hk + 1],
                               jnp.where(rg == 2, sinks_ref[4 * hk + 2], sinks_ref[4 * hk + 3])))
    m = jnp.maximum(jnp.max(s, axis=1, keepdims=True), sink)
    e = jnp.exp(s - m)
    es = jnp.exp(sink - m)
    inv = 1.0 / (jnp.sum(e, axis=1, keepdims=True) + es)
    return e * inv, es * inv


def _prev(i):
    return jnp.maximum(i - 1, 0)


def attn_fwd(q, k, v, cos, sin_signed, sinks, name, stages=()):
    nb = ATTN_BLOCK

    def body(q_ref, kc_ref, kp_ref, vc_ref, vp_ref, cc_ref, sc_ref, cp_ref, sp_ref, sinks_ref,
             qr_ref, kr_ref, y_ref):
        first_block = pl.program_id(0) == 0
        qr = _rope(q_ref[...], cc_ref[...], sc_ref[...])
        kc = _rope(kc_ref[...], cc_ref[...], sc_ref[...])
        kp = _rope(kp_ref[...], cp_ref[...], sp_ref[...])
        qr_ref[...] = qr.astype(BF16)
        kr_ref[...] = kc.astype(BF16)
        k2 = jnp.concatenate([kp, kc], axis=0)
        v2 = jnp.concatenate([vp_ref[...].astype(F32), vc_ref[...].astype(F32)], axis=0)
        for hk in range(N_KV_HEADS):
            kt = hk // 2
            kd = _both_halves(k2[:, kt * 128:(kt + 1) * 128], hk % 2).astype(BF16)
            vd = _both_halves(v2[:, kt * 128:(kt + 1) * 128], hk % 2).astype(BF16)
            qs = _stack_heads(qr[:, (2 * hk) * 128:(2 * hk + 1) * 128],
                              qr[:, (2 * hk + 1) * 128:(2 * hk + 2) * 128]).astype(BF16)
            p, _ = _attn_probs(qs, kd, sinks_ref, hk, first_block)
            ta, tb = _unstack_heads(_dot(p.astype(BF16), vd))
            y_ref[:, (2 * hk) * 128:(2 * hk + 1) * 128] = ta.astype(BF16)
            y_ref[:, (2 * hk + 1) * 128:(2 * hk + 2) * 128] = tb.astype(BF16)

    cur = lambda w: pl.BlockSpec((nb, w), lambda i: (i, 0))
    prv = lambda w: pl.BlockSpec((nb, w), lambda i: (_prev(i), 0))
    return _call(
        body, name=name, grid=(N_ATTN_BLOCKS,),
        in_specs=[cur(D_MODEL), cur(KV_W), prv(KV_W), cur(KV_W), prv(KV_W), cur(128), cur(128), prv(128), prv(128),
                  pl.BlockSpec(memory_space=pltpu.SMEM)],
        out_specs=[cur(D_MODEL), cur(KV_W), cur(D_MODEL)],
        out_shape=[jax.ShapeDtypeStruct((SEQ, D_MODEL), BF16), jax.ShapeDtypeStruct((SEQ, KV_W), BF16),
                   jax.ShapeDtypeStruct((SEQ, D_MODEL), BF16)],
        args=[q, k, k, v, v, cos, sin_signed, cos, sin_signed, sinks], stages=stages)


def attn_bwd(qr, kr, v, dy, cos, sin_signed, sinks, name, stages=()):
    nb = ATTN_BLOCK
    n_steps = N_ATTN_BLOCKS + 1
    scale = HEAD_DIM ** -0.5

    def body(q_ref, kc_ref, kp_ref, vc_ref, vp_ref, dy_ref, cc_ref, sc_ref, cp_ref, sp_ref, sinks_ref,
             dq_ref, dkv_ref, dsk_ref, ck_ref, cv_ref):
        dk_ref = dkv_ref.at[:, pl.ds(0, KV_W)]
        dv_ref = dkv_ref.at[:, pl.ds(KV_W, KV_W)]
        i = pl.program_id(0)

        @pl.when(i == 0)
        def _():
            dsk_ref[...] = jnp.zeros_like(dsk_ref)
            ck_ref[...] = jnp.zeros_like(ck_ref)
            cv_ref[...] = jnp.zeros_like(cv_ref)

        @pl.when(i < N_ATTN_BLOCKS)
        def _():
            qv = q_ref[...].astype(F32)
            dov = dy_ref[...].astype(F32)
            k2 = jnp.concatenate([kp_ref[...].astype(F32), kc_ref[...].astype(F32)], axis=0)
            v2 = jnp.concatenate([vp_ref[...].astype(F32), vc_ref[...].astype(F32)], axis=0)
            lane = lax.broadcasted_iota(jnp.int32, (8, 128), 1)
            lo = lax.broadcasted_iota(jnp.int32, (2 * nb, 128), 1) < 64
            dsk = jnp.zeros((8, 128), F32)
            dk_tiles = []
            dv_tiles = []
            for hk in range(N_KV_HEADS):
                kt = hk // 2
                kd = _both_halves(k2[:, kt * 128:(kt + 1) * 128], hk % 2).astype(BF16)
                vd = _both_halves(v2[:, kt * 128:(kt + 1) * 128], hk % 2).astype(BF16)
                qs = _stack_heads(qv[:, (2 * hk) * 128:(2 * hk + 1) * 128],
                                  qv[:, (2 * hk + 1) * 128:(2 * hk + 2) * 128]).astype(BF16)
                dos = _stack_heads(dov[:, (2 * hk) * 128:(2 * hk + 1) * 128],
                                   dov[:, (2 * hk + 1) * 128:(2 * hk + 2) * 128]).astype(BF16)
                p, ps = _attn_probs(qs, kd, sinks_ref, hk, i == 0)
                dp = _dot_nt(dos, vd)
                delta = jnp.sum(p * dp, axis=1, keepdims=True)
                ds = (p * (dp - delta)).astype(BF16)
                dsink = -ps * delta
                for g in range(4):
                    dsk = dsk + jnp.where(lane == 4 * hk + g, jnp.sum(dsink[g * nb:(g + 1) * nb]), 0.0)
                ta, tb = _unstack_heads(_dot(ds, kd) * scale)
                dq_a = (2 * hk) * 128
                dq_ref[:, dq_a:dq_a + 128] = _rope(ta, cc_ref[...], -sc_ref[...]).astype(BF16)
                dq_ref[:, dq_a + 128:dq_a + 256] = _rope(tb, cc_ref[...], -sc_ref[...]).astype(BF16)
                rk = _dot_tn(ds, qs) * scale
                rv = _dot_tn(p.astype(BF16), dos)
                dk_tiles.append(rk + pltpu.roll(rk, 64, 1))
                dv_tiles.append(rv + pltpu.roll(rv, 64, 1))
            dsk_ref[...] += dsk
            dk_full = jnp.concatenate([jnp.where(lo, dk_tiles[0], dk_tiles[1]),
                                       jnp.where(lo, dk_tiles[2], dk_tiles[3])], axis=1)
            dv_full = jnp.concatenate([jnp.where(lo, dv_tiles[0], dv_tiles[1]),
                                       jnp.where(lo, dv_tiles[2], dv_tiles[3])], axis=1)
            dk_ref[...] = _rope(ck_ref[...] + dk_full[0:nb], cp_ref[...], -sp_ref[...]).astype(BF16)
            dv_ref[...] = (cv_ref[...] + dv_full[0:nb]).astype(BF16)
            ck_ref[...] = dk_full[nb:2 * nb]
            cv_ref[...] = dv_full[nb:2 * nb]

        @pl.when(i == N_ATTN_BLOCKS)
        def _():
            dk_ref[...] = _rope(ck_ref[...], cp_ref[...], -sp_ref[...]).astype(BF16)
            dv_ref[...] = cv_ref[...].astype(BF16)

    qi = lambda i: jnp.minimum(i, N_ATTN_BLOCKS - 1)
    cur = lambda w: pl.BlockSpec((nb, w), lambda i: (qi(i), 0))
    prv = lambda w: pl.BlockSpec((nb, w), lambda i: (_prev(qi(i)), 0))
    out_prev = lambda w: pl.BlockSpec((nb, w), lambda i: (_prev(i), 0))
    return _call(
        body, name=name, grid=(n_steps,),
        in_specs=[cur(D_MODEL), cur(KV_W), prv(KV_W), cur(KV_W), prv(KV_W), cur(D_MODEL),
                  cur(128), cur(128), out_prev(128), out_prev(128), pl.BlockSpec(memory_space=pltpu.SMEM)],
        out_specs=[cur(D_MODEL), out_prev(2 * KV_W), pl.BlockSpec((8, 128), lambda i: (0, 0))],
        out_shape=[jax.ShapeDtypeStruct((SEQ, D_MODEL), BF16), jax.ShapeDtypeStruct((SEQ, 2 * KV_W), BF16),
                   jax.ShapeDtypeStruct((8, 128), F32)],
        scratch_shapes=[pltpu.VMEM((nb, KV_W), F32), pltpu.VMEM((nb, KV_W), F32)],
        args=[qr, kr, kr, v, v, dy, cos, sin_signed, cos, sin_signed, sinks], stages=stages)


def _proj_scratch():
    return [pltpu.VMEM((D_MODEL, D_MODEL), BF16)] * 3 + [pltpu.SemaphoreType.DMA((3 * N_CHIPS,))]


def _load_projs(w_refs, wl_ref, wa_ref, wo_ref, sem):
    for k, (w_ref, dst) in enumerate(zip(w_refs, (wl_ref, wa_ref, wo_ref))):
        _load_weight(w_ref, dst, sem.at[pl.ds(k * N_CHIPS, N_CHIPS)])


def merge_fwd(y_lru, y_attn, g_lru, g_attn, projs, g_post, h_in, name, stages=()):
    tm = MM_ROWS

    def body(yl_ref, ya_ref, gl_ref, ga_ref, w1_ref, w2_ref, w3_ref, gp_ref, h_ref,
             pl_ref, pa_ref, mg_ref, m_ref, o_ref, wl_ref, wa_ref, wo_ref, sem):
        @pl.when(pl.program_id(0) == 0)
        def _():
            _load_projs((w1_ref, w2_ref, w3_ref), wl_ref, wa_ref, wo_ref, sem)

        p_l = _dot(yl_ref[...], wl_ref[...])
        p_a = _dot(ya_ref[...], wa_ref[...])
        pl_ref[...] = p_l.astype(BF16)
        pa_ref[...] = p_a.astype(BF16)
        merged = (_sigmoid(gl_ref[...]) * p_l + _sigmoid(ga_ref[...]) * p_a).astype(BF16)
        mg_ref[...] = merged
        m = _dot(merged, wo_ref[...])
        m_ref[...] = m
        o_ref[...] = h_ref[...] + m * _rsqrt_mean_sq(m) * gp_ref[...]

    row = _ROW(tm)
    return _call(
        body, name=name, grid=(SEQ // tm,),
        in_specs=[row, row, row, row, ANY, ANY, ANY, _VEC, row],
        out_specs=[row] * 5,
        out_shape=[jax.ShapeDtypeStruct((SEQ, D_MODEL), BF16)] * 3 + [jax.ShapeDtypeStruct((SEQ, D_MODEL), F32)] * 2,
        scratch_shapes=_proj_scratch(),
        args=[y_lru, y_attn, g_lru, g_attn, *projs, g_post, h_in], stages=stages)


def merge_bwd(d_out, m, g_post, projs, g_lru, g_attn, p_l, p_a, name, stages=()):
    tm = 256

    def body(do_ref, m_ref, gp_ref, w1_ref, w2_ref, w3_ref, gl_ref, ga_ref, pl_ref, pa_ref,
             dm_ref, dpl_ref, dpa_ref, dgl_ref, dga_ref, dya_ref, dyl_ref, dgp_ref, wl_ref, wa_ref, wo_ref, sem):
        @pl.when(pl.program_id(0) == 0)
        def _():
            _load_projs((w1_ref, w2_ref, w3_ref), wl_ref, wa_ref, wo_ref, sem)
            dgp_ref[...] = jnp.zeros_like(dgp_ref)

        mv = m_ref[...]
        rm = _rsqrt_mean_sq(mv)
        mh = mv * rm
        dn = do_ref[...]
        dgp_ref[...] += jnp.sum(dn * mh, axis=0, keepdims=True)
        t = dn * gp_ref[...]
        dm = (rm * (t - mh * jnp.mean(t * mh, axis=-1, keepdims=True))).astype(BF16)
        dm_ref[...] = dm
        dmg = _dot_nt(dm, wo_ref[...])
        sl = _sigmoid(gl_ref[...])
        sa = _sigmoid(ga_ref[...])
        dpl = (dmg * sl).astype(BF16)
        dpa = (dmg * sa).astype(BF16)
        dpl_ref[...] = dpl
        dpa_ref[...] = dpa
        dgl_ref[...] = (dmg * pl_ref[...].astype(F32) * sl * (1.0 - sl)).astype(BF16)
        dga_ref[...] = (dmg * pa_ref[...].astype(F32) * sa * (1.0 - sa)).astype(BF16)
        dyl_ref[...] = _dot_nt(dpl, wl_ref[...])
        dya_ref[...] = _dot_nt(dpa, wa_ref[...]).astype(BF16)

    row = _ROW(tm)
    return _call(
        body, name=name, grid=(SEQ // tm,),
        in_specs=[row, row, _VEC, ANY, ANY, ANY, row, row, row, row],
        out_specs=[row] * 7 + [_VEC],
        out_shape=[jax.ShapeDtypeStruct((SEQ, D_MODEL), BF16)] * 6 + [jax.ShapeDtypeStruct((SEQ, D_MODEL), F32),
                                                                       jax.ShapeDtypeStruct((1, D_MODEL), F32)],
        scratch_shapes=_proj_scratch(),
        args=[d_out, m, g_post, *projs, g_lru, g_attn, p_l, p_a], stages=stages)


def _rope_tables():
    half = HEAD_DIM // 2
    inv_freq = np.float32(ROPE_THETA) ** (-np.arange(half, dtype=np.float32) / np.float32(half))
    ang = np.arange(SEQ, dtype=np.float32)[:, None] * inv_freq[None, :]
    cos, sin = np.cos(ang), np.sin(ang)
    return (jnp.asarray(np.tile(np.concatenate([cos, cos], axis=1), (1, 2))),
            jnp.asarray(np.tile(np.concatenate([-sin, sin], axis=1), (1, 2))))


def _block_diag(w):
    per = LRU_TC // LRU_BLOCK_W
    w4 = w.reshape(LRU_W // LRU_TC, per, LRU_BLOCK_W, LRU_BLOCK_W)
    eye = jnp.eye(per, dtype=w.dtype)
    return jnp.einsum('jacd,ab->jacbd', w4, eye).reshape(LRU_W // LRU_TC, LRU_TC, LRU_TC).astype(BF16)


def _diag_blocks(p):
    per = LRU_TC // LRU_BLOCK_W
    p5 = p.reshape(LRU_W // LRU_TC, per, LRU_BLOCK_W, per, LRU_BLOCK_W)
    return jnp.stack([p5[:, a, :, a, :] for a in range(per)], axis=1).reshape(LRU_W // LRU_BLOCK_W, LRU_BLOCK_W, LRU_BLOCK_W)


def _place():
    x, y, c = lax.axis_index('x'), lax.axis_index('y'), lax.axis_index('c')
    chips = [(1 - x, y), (x, 1 - y), (1 - x, 1 - y)]
    return x, y, c, chips


def _rcopy(src, dst, send_sem, recv_sem, to):
    return pltpu.make_async_remote_copy(src_ref=src, dst_ref=dst, send_sem=send_sem, recv_sem=recv_sem,
                                        device_id=to, device_id_type=MESH)


class _Stage:
    inputs, out_shape, scratch, peers = (), (), (), ()

    def start(self, ins, outs, scr):
        plan = self._plan(ins, outs, scr)
        for ld in plan['loads']:
            ld.start()
        for cp in plan['sends']:
            cp.start()

    def mid(self, ins, outs, scr):
        plan = self._plan(ins, outs, scr)
        for ld, st in zip(plan['loads'], plan['stores']):
            ld.wait()
            st.start()
        for arrived, onward in zip(plan['arrivals'], plan['forwards']):
            arrived.wait_recv()
            onward.start()

    def end(self, ins, outs, scr):
        plan = self._plan(ins, outs, scr)
        for st in plan['stores']:
            st.wait()
        for arrived in (plan['final_arrivals'] if plan['forwards'] else plan['arrivals']):
            arrived.wait_recv()
        for cp in plan['sends'] + plan['forwards']:
            cp.wait_send()


def _empty_plan():
    return dict(loads=[], stores=[], sends=[], arrivals=[], forwards=[], final_arrivals=[])


class GatherStage(_Stage):
    peers = ('chips', 'sib')

    def __init__(self, items):
        self.ranges = [(off, rows) for _, off, rows in items]
        self.subs = [max(k for k in (4, 2, 1) if (rows // 2) % (16 * k) == 0) for _, rows in self.ranges]
        self.inputs = [src for src, _, _ in items]
        self.out_shape = [jax.ShapeDtypeStruct((N_CHIPS, rows, D_MODEL), BF16) for _, rows in self.ranges]
        self.n_ici = 3 * sum(self.subs)
        self.scratch = [pltpu.VMEM((sum(r for _, r in self.ranges), D_MODEL), BF16), pltpu.SemaphoreType.DMA((2 * self.n_ici,)),
                        pltpu.SemaphoreType.DMA((2 * self.n_ici,)), pltpu.SemaphoreType.DMA((2 * len(items),))]

    def _plan(self, ins, outs, scr):
        buf, send, recv, lsem = scr
        x, y, c, chips = _place()
        me_q = 2 * x + y
        sib = (x, y, 1 - c)
        plan = _empty_plan()
        boff = 0
        piece = 0
        for w, ((off, rows), sub, p_ref, o_ref) in enumerate(zip(self.ranges, self.subs, ins, outs)):
            hr = rows // 2
            ch = hr // sub
            plan['loads'].append(pltpu.make_async_copy(p_ref.at[pl.ds(off, rows)], buf.at[pl.ds(boff, rows)], lsem.at[2 * w]))
            plan['stores'].append(pltpu.make_async_copy(buf.at[pl.ds(boff, rows)], o_ref.at[me_q], lsem.at[2 * w + 1]))
            boff += rows
            for k in range(sub):
                mine = pl.ds(pl.multiple_of(c * hr + k * ch, 16), ch)
                theirs = pl.ds(pl.multiple_of((1 - c) * hr + k * ch, 16), ch)
                src = p_ref.at[pl.ds(pl.multiple_of(off + c * hr + k * ch, 16), ch)]
                piece += 1
                for j, (cx, cy) in enumerate(chips):
                    i = (piece - 1) * 3 + j
                    got = o_ref.at[2 * cx + cy, mine]
                    got_sib = o_ref.at[2 * cx + cy, theirs]
                    plan['sends'].append(_rcopy(src, o_ref.at[me_q, mine], send.at[i], recv.at[i], (cx, cy, c)))
                    plan['arrivals'].append(_rcopy(got, got, send.at[i], recv.at[i], (cx, cy, c)))
                    plan['forwards'].append(_rcopy(got, got, send.at[self.n_ici + i], recv.at[self.n_ici + i], sib))
                    plan['final_arrivals'].append(
                        _rcopy(got_sib, got_sib, send.at[self.n_ici + i], recv.at[self.n_ici + i], sib))
        return plan


class PairStage(_Stage):
    peers = ('sib',)

    def __init__(self, grads):
        self.inputs = list(grads)
        self.out_shape = [jax.ShapeDtypeStruct((N_CHIPS, 1) + g.shape[2:], BF16) for g in grads]
        n_cp = N_CHIPS * len(grads)
        self.scratch = [pltpu.SemaphoreType.DMA((n_cp,)), pltpu.SemaphoreType.DMA((n_cp,))]

    def _plan(self, ins, outs, scr):
        send, recv = scr
        x, y, c, _ = _place()
        plan = _empty_plan()
        for w, (g_ref, l_ref) in enumerate(zip(ins, outs)):
            for q in range(N_CHIPS):
                i = w * N_CHIPS + q
                plan['sends'].append(_rcopy(g_ref.at[q, pl.ds(1 - c, 1)], l_ref.at[q], send.at[i], recv.at[i], (x, y, 1 - c)))
        plan['arrivals'] = plan['sends']
        return plan


class ChipStage(_Stage):
    peers = ('chips',)

    def __init__(self, items):
        self.ranges = [(off, n) for _, off, n in items]
        self.inputs = [s for s, _, _ in items]
        self.out_shape = [jax.ShapeDtypeStruct((N_CHIPS, n, D_MODEL), BF16) for _, n in self.ranges]
        n_cp = 3 * len(items)
        self.scratch = [pltpu.VMEM((sum(n for _, n in self.ranges), D_MODEL), BF16), pltpu.SemaphoreType.DMA((n_cp,)),
                        pltpu.SemaphoreType.DMA((n_cp,)), pltpu.SemaphoreType.DMA((2 * len(items),))]

    def _plan(self, ins, outs, scr):
        buf, send, recv, lsem = scr
        x, y, c, chips = _place()
        me_q = 2 * x + y
        plan = _empty_plan()
        boff = 0
        for w, ((off, n), s_ref, l_ref) in enumerate(zip(self.ranges, ins, outs)):
            rows = pl.ds(off, n)
            plan['loads'].append(pltpu.make_async_copy(s_ref.at[me_q, rows], buf.at[pl.ds(boff, n)], lsem.at[2 * w]))
            plan['stores'].append(pltpu.make_async_copy(buf.at[pl.ds(boff, n)], l_ref.at[me_q], lsem.at[2 * w + 1]))
            boff += n
            for j, (cx, cy) in enumerate(chips):
                i = w * 3 + j
                got = l_ref.at[2 * cx + cy]
                plan['sends'].append(_rcopy(s_ref.at[2 * cx + cy, rows], l_ref.at[me_q], send.at[i], recv.at[i], (cx, cy, c)))
                plan['arrivals'].append(_rcopy(got, got, send.at[i], recv.at[i], (cx, cy, c)))
        return plan


class SwapStage(_Stage):
    peers = ('sib',)

    def __init__(self, items):
        n = len(items)
        self.inputs = list(items)
        self.out_shape = [jax.ShapeDtypeStruct((2,) + a.shape, a.dtype) for a in items]
        self.scratch = [pltpu.VMEM(a.shape, a.dtype) for a in items] + [
            pltpu.SemaphoreType.DMA((n,)), pltpu.SemaphoreType.DMA((n,)), pltpu.SemaphoreType.DMA((2 * n,))]

    def _plan(self, ins, outs, scr):
        bufs, (send, recv, lsem) = scr[:len(ins)], scr[len(ins):]
        x, y, c, _ = _place()
        plan = _empty_plan()
        for w, (h_ref, o_ref, buf) in enumerate(zip(ins, outs, bufs)):
            plan['loads'].append(pltpu.make_async_copy(h_ref, buf, lsem.at[2 * w]))
            plan['stores'].append(pltpu.make_async_copy(buf, o_ref.at[c], lsem.at[2 * w + 1]))
            got = o_ref.at[1 - c]
            plan['sends'].append(_rcopy(h_ref, o_ref.at[c], send.at[w], recv.at[w], (x, y, 1 - c)))
            plan['arrivals'].append(_rcopy(got, got, send.at[w], recv.at[w], (x, y, 1 - c)))
        return plan


class SmallGatherStage(_Stage):
    peers = ('chips', 'sib')

    def __init__(self, blk):
        self.inputs = [blk]
        self.out_shape = [jax.ShapeDtypeStruct((N_DEV,) + blk.shape, blk.dtype)]
        self.scratch = [pltpu.VMEM(blk.shape, blk.dtype), pltpu.SemaphoreType.DMA((7,)), pltpu.SemaphoreType.DMA((7,)),
                        pltpu.SemaphoreType.DMA((2,))]

    def _plan(self, ins, outs, scr):
        (x_ref,), (o_ref,), (buf, send, recv, lsem) = ins, outs, scr
        x, y, c, chips = _place()
        sib = (x, y, 1 - c)

        def slot(px, py, pc):
            return o_ref.at[4 * px + 2 * py + pc]

        plan = _empty_plan()
        plan['loads'].append(pltpu.make_async_copy(x_ref, buf, lsem.at[0]))
        plan['stores'].append(pltpu.make_async_copy(buf, slot(x, y, c), lsem.at[1]))
        from_sib = slot(x, y, 1 - c)
        plan['sends'].append(_rcopy(x_ref, slot(x, y, c), send.at[0], recv.at[0], sib))
        plan['final_arrivals'].append(_rcopy(from_sib, from_sib, send.at[0], recv.at[0], sib))
        for j, (cx, cy) in enumerate(chips):
            got, got_sib = slot(cx, cy, c), slot(cx, cy, 1 - c)
            plan['sends'].append(_rcopy(x_ref, slot(x, y, c), send.at[1 + j], recv.at[1 + j], (cx, cy, c)))
            plan['arrivals'].append(_rcopy(got, got, send.at[1 + j], recv.at[1 + j], (cx, cy, c)))
            plan['forwards'].append(_rcopy(got, got, send.at[4 + j], recv.at[4 + j], sib))
            plan['final_arrivals'].append(_rcopy(got_sib, got_sib, send.at[4 + j], recv.at[4 + j], sib))
        return plan


_HBM = pl.BlockSpec(memory_space=pltpu.HBM)
_SEM = pl.BlockSpec(memory_space=pltpu.SEMAPHORE)
_DATAFLOW = pltpu.CompilerParams(has_side_effects=pltpu.SideEffectType.DATAFLOW_SIDE_EFFECTING)


def chip_exchange_start(s):
    def body(s_ref, land_ref, send, recv, s_thru, land_thru, token):
        x, y, c, chips = _place()
        for j, (cx, cy) in enumerate(chips):
            _rcopy(s_ref.at[2 * cx + cy], land_ref.at[2 * x + y], send.at[j], recv.at[j], (cx, cy, c)).start()
        token[...] = jnp.zeros_like(token)

    return pl.pallas_call(
        body, name='chip_exchange_start',
        out_shape=(pltpu.SemaphoreType.DMA((3,)), pltpu.SemaphoreType.DMA((3,)), pltpu.HBM(s.shape, s.dtype),
                   pltpu.HBM(s.shape, s.dtype), jax.ShapeDtypeStruct((8, 128), F32)),
        in_specs=(_HBM, _HBM), out_specs=(_SEM, _SEM, _HBM, _HBM, pl.BlockSpec(memory_space=pltpu.VMEM)),
        input_output_aliases={0: 2, 1: 3}, compiler_params=_DATAFLOW,
    )(pltpu.with_memory_space_constraint(s, pltpu.HBM),
      pltpu.with_memory_space_constraint(lax.empty(s.shape, s.dtype), pltpu.HBM))


def chip_exchange_wait(send, recv, s_thru, land_thru, after):
    def body(s_ref, land_ref, send_sem, recv_sem, after_ref, s_out, land_out):
        x, y, c, chips = _place()
        for j, (cx, cy) in enumerate(chips):
            cp = _rcopy(s_ref.at[2 * cx + cy], land_ref.at[2 * cx + cy], send_sem.at[j], recv_sem.at[j], (cx, cy, c))
            cp.wait_send()
            cp.wait_recv()

    return pl.pallas_call(
        body, name='chip_exchange_wait',
        out_shape=(pltpu.HBM(s_thru.shape, s_thru.dtype), pltpu.HBM(land_thru.shape, land_thru.dtype)),
        in_specs=(_HBM, _HBM, _SEM, _SEM, ANY), out_specs=(_HBM, _HBM),
        input_output_aliases={0: 0, 1: 1}, compiler_params=_DATAFLOW,
    )(s_thru, land_thru, send, recv, after)


def comm_call(name, stages):
    def body():
        pass

    return _call(body, name=name, grid=(1,), in_specs=[], out_specs=[], out_shape=[], args=[], stages=stages)[1]


def pair_sum(g4, land, c_arr, name):
    hr = g4.shape[2]

    def body(c_ref, g_ref, l_ref, o_ref):
        o_ref[0] = (g_ref[0, 0].astype(F32) + l_ref[0, 0].astype(F32)).astype(BF16)

    return pl.pallas_call(
        body, name=name,
        grid_spec=pltpu.PrefetchScalarGridSpec(
            num_scalar_prefetch=1, grid=(N_CHIPS,),
            in_specs=[pl.BlockSpec((1, 1, hr, D_MODEL), lambda q, c: (q, c[0], 0, 0)),
                      pl.BlockSpec((1, 1, hr, D_MODEL), lambda q, c: (q, 0, 0, 0))],
            out_specs=pl.BlockSpec((1, hr, D_MODEL), lambda q, c: (q, 0, 0))),
        out_shape=jax.ShapeDtypeStruct((N_CHIPS, hr, D_MODEL), BF16),
        compiler_params=_params(1),
    )(c_arr, g4, land)


def small_sum(vec_parts, lru_parts):
    def body(v_ref, l_ref, o_ref):
        for p_ref, lo, n in ((v_ref, 0, ROW_WA), (l_ref, ROW_WA, SMALL_ROWS - ROW_WA)):
            acc = p_ref[0]
            for s in range(1, N_DEV):
                acc = acc + p_ref[s]
            o_ref[lo:lo + n, :] = acc

    return pl.pallas_call(
        body, name='small_sum', grid=(1,),
        in_specs=[pl.BlockSpec(vec_parts.shape, lambda i: (0, 0, 0)), pl.BlockSpec(lru_parts.shape, lambda i: (0, 0, 0))],
        out_specs=pl.BlockSpec((SMALL_ROWS, D_MODEL), lambda i: (0, 0)),
        out_shape=jax.ShapeDtypeStruct((SMALL_ROWS, D_MODEL), F32),
        compiler_params=_params(1),
    )(vec_parts, lru_parts)


def _adam_math(w, g, m, v):
    m2 = ADAM_B1 * m + (1.0 - ADAM_B1) * g
    v2 = ADAM_B2 * v + (1.0 - ADAM_B2) * (g * g)
    m_hat = m2 / (1.0 - ADAM_B1 ** ADAM_STEP)
    v_hat = v2 / (1.0 - ADAM_B2 ** ADAM_STEP)
    delta = -ADAM_LR * (m_hat / (jnp.sqrt(v_hat) + ADAM_EPS) + ADAM_WD * w)
    return delta, m2, v2


def _adam_body(n_parts, transposed, n_after):
    def body(*refs):
        refs = refs[n_after:]
        g_refs = refs[:n_parts]
        w_ref, m_ref, v_ref, go_ref, d_ref, mo_ref, vo_ref = refs[n_parts:]
        def chips_added(blk):
            acc = blk[0].astype(F32)
            for s in range(1, N_CHIPS):
                acc = acc + blk[s].astype(F32)
            return acc

        if transposed:
            g = jnp.concatenate([chips_added(g_ref[h]) for h in range(2) for g_ref in g_refs], axis=0).T
        else:
            rows = [chips_added(g_ref[0]) for g_ref in g_refs]
            g = jnp.concatenate(rows, axis=0) if n_parts > 1 else rows[0]
        go_ref[...] = g
        d_ref[...], mo_ref[...], vo_ref[...] = _adam_math(w_ref[...], g, m_ref[...], v_ref[...])
    return body


def adam_rows(fulls, name, w, m, v, after=()):
    hr = w.shape[0] // 2
    blk = pl.BlockSpec((hr, D_MODEL), lambda h: (h, 0))
    return pl.pallas_call(
        _adam_body(len(fulls), False, len(after)), name='adam_' + name, grid=(2,),
        in_specs=[ANY] * len(after)
        + [pl.BlockSpec((1, N_CHIPS, f.shape[2], D_MODEL), lambda h: (h, 0, 0, 0)) for f in fulls] + [blk, blk, blk],
        out_specs=[blk] * 4,
        out_shape=[jax.ShapeDtypeStruct(w.shape, F32)] * 4,
        compiler_params=_params(1),
    )(*after, *fulls, w, m, v)


def adam_cols(fulls, name, w, m, v, after=()):
    cols = w.shape[1]
    tr = 128
    blk = pl.BlockSpec((tr, cols), lambda i: (i, 0))
    return pl.pallas_call(
        _adam_body(len(fulls), True, len(after)), name='adam_' + name, grid=(D_MODEL // tr,),
        in_specs=[ANY] * len(after)
        + [pl.BlockSpec((2, N_CHIPS, f.shape[2], tr), lambda i: (0, 0, 0, i)) for f in fulls] + [blk, blk, blk],
        out_specs=[blk] * 4,
        out_shape=[jax.ShapeDtypeStruct(w.shape, F32)] * 4,
        compiler_params=_params(1),
    )(*after, *fulls, w, m, v)


def adam_small(g, w, m, v):
    def body(g_ref, w_ref, m_ref, v_ref, d_ref, mo_ref, vo_ref):
        d_ref[...], mo_ref[...], vo_ref[...] = _adam_math(w_ref[...], g_ref[...], m_ref[...], v_ref[...])

    blk = pl.BlockSpec(w.shape, lambda i: (0, 0))
    return pl.pallas_call(
        body, name='adam_small', grid=(1,), in_specs=[blk] * 4, out_specs=[blk] * 3,
        out_shape=[jax.ShapeDtypeStruct(w.shape, F32)] * 3, compiler_params=_params(1),
    )(g, w, m, v)


WEIGHTS = ('ffn1_pre_g', 'ffn1_w_gu', 'ffn1_w_down', 'ffn1_post_g', 'mix_pre_g', 'w_in', 'conv_w', 'conv_b',
           'lru_w_a', 'lru_b_a', 'lru_w_x', 'lru_b_x', 'lru_lambda', 'attn_sinks', 'w_proj_lru', 'w_proj_attn',
           'w_out', 'mix_post_g', 'ffn2_pre_g', 'ffn2_w_gu', 'ffn2_w_down', 'ffn2_post_g')
SMALL = tuple(n for n in WEIGHTS if n not in PACK_OFF)


def _pack_vecs(d, conv_rows):
    sinks = jnp.pad(d['attn_sinks'].reshape(1, N_Q_HEADS), ((0, 0), (0, D_MODEL - N_Q_HEADS)))
    conv = jnp.pad(conv_rows, ((0, ROW_WA - ROW_CONV - conv_rows.shape[0]), (0, 0)))
    return jnp.concatenate([d[n].reshape(1, D_MODEL) for n in SMALL_VECS] + [sinks, conv], axis=0)


def _pack_lru(d):
    return jnp.concatenate([d['lru_w_a'].reshape(64, D_MODEL), d['lru_w_x'].reshape(64, D_MODEL)], axis=0)


def _pack_small(d, conv_rows):
    return jnp.concatenate([_pack_vecs(d, conv_rows), _pack_lru(d)], axis=0)


def _unpack_small(p, shapes):
    out = {n: p[k:k + 1].reshape(shapes[n]) for k, n in enumerate(SMALL_VECS)}
    out['attn_sinks'] = p[ROW_SINKS:ROW_SINKS + 1, :N_Q_HEADS].reshape(shapes['attn_sinks'])
    out['conv_w'] = p[ROW_CONV:ROW_CONV + 1].reshape(shapes['conv_w'])
    out['lru_w_a'] = p[ROW_WA:ROW_WA + 64].reshape(shapes['lru_w_a'])
    out['lru_w_x'] = p[ROW_WX:ROW_WX + 64].reshape(shapes['lru_w_x'])
    return out


def kernel(x, ffn1_pre_g, ffn1_w_gu, ffn1_w_down, ffn1_post_g, mix_pre_g, w_in, conv_w, conv_b, lru_w_a, lru_b_a, lru_w_x, lru_b_x, lru_lambda, attn_sinks, w_proj_lru, w_proj_attn, w_out, mix_post_g, ffn2_pre_g, ffn2_w_gu, ffn2_w_down, ffn2_post_g, loss_target, m_ffn1_pre_g, m_ffn1_w_gu, m_ffn1_w_down, m_ffn1_post_g, m_mix_pre_g, m_w_in, m_conv_w, m_conv_b, m_lru_w_a, m_lru_b_a, m_lru_w_x, m_lru_b_x, m_lru_lambda, m_attn_sinks, m_w_proj_lru, m_w_proj_attn, m_w_out, m_mix_post_g, m_ffn2_pre_g, m_ffn2_w_gu, m_ffn2_w_down, m_ffn2_post_g, v_ffn1_pre_g, v_ffn1_w_gu, v_ffn1_w_down, v_ffn1_post_g, v_mix_pre_g, v_w_in, v_conv_w, v_conv_b, v_lru_w_a, v_lru_b_a, v_lru_w_x, v_lru_b_x, v_lru_lambda, v_attn_sinks, v_w_proj_lru, v_w_proj_attn, v_w_out, v_mix_post_g, v_ffn2_pre_g, v_ffn2_w_gu, v_ffn2_w_down, v_ffn2_post_g):
    given = dict(locals())
    w = {n: given[n] for n in WEIGHTS}
    mom = {n: given['m_' + n] for n in WEIGHTS}
    var = {n: given['v_' + n] for n in WEIGHTS}
    shapes = {n: w[n].shape for n in WEIGHTS}
    xq = lax.axis_index('x')
    yq = lax.axis_index('y')
    cq = lax.axis_index('c')
    me_q = 2 * xq + yq

    c_arr = cq.reshape(1).astype(jnp.int32)
    xs, target = x[0], loss_target[0]
    sw = {n: (w[n][0] if w[n].ndim > 2 else w[n]) for n in SMALL}
    cos, sin_signed = _rope_tables()
    wa_bd = _block_diag(sw['lru_w_a'])
    wx_bd = _block_diag(sw['lru_w_x'])
    sinks = sw['attn_sinks'].reshape(N_Q_HEADS)

    shard = {n: (w[n][0].T if t else w[n][0]).astype(BF16) for n, _, t in PACK}
    conv_pad = jnp.pad(w['conv_w'][0], ((0, 4), (0, 0)))

    def whole(name):
        return (shard[name], 0, PACK_ROWS_OF[name])

    def part(name, p, n_parts=2):
        rows = PACK_ROWS_OF[name] // n_parts
        return (shard[name], p * rows, rows)

    (w_gu1,), (conv_all,) = comm_call('gather_first', [GatherStage([whole('ffn1_w_gu')]), SmallGatherStage(conv_pad)])
    sw['conv_w'] = jnp.transpose(conv_all[0::2, :4, :], (1, 0, 2)).reshape(4, LRU_W)
    proj_names = ['w_proj_lru', 'w_proj_attn', 'w_out']

    (n1, g1, u1, a1), ((w_down1,),) = ffn_fwd_a(xs, sw['ffn1_pre_g'], [w_gu1], 'ffn1_fwd_a',
                                                 stages=[GatherStage([whole('ffn1_w_down')])])
    (f1, h1), ((w_in_t,),) = ffn_fwd_b(a1, w_down1, sw['ffn1_post_g'], xs, 'ffn1_fwd_b', stages=[GatherStage([whole('w_in')])])
    (um, gate, xbr, q, k, v, g_lru, g_attn), ((w_gu2a,),) = mix_in(h1, sw['mix_pre_g'], w_in_t, 'mix_in',
                                                                   stages=[GatherStage([part('ffn2_w_gu', 0)])])
    (y_lru, h_lru), ((w_gu2b,),) = lru_fwd(gate, xbr, sw['conv_w'], sw['conv_b'], wa_bd, sw['lru_b_a'], wx_bd, sw['lru_b_x'],
                                           sw['lru_lambda'], 'lru_fwd', stages=[GatherStage([part('ffn2_w_gu', 1)])])
    (qr, kr, y_attn), (projs,) = attn_fwd(q, k, v, cos, sin_signed, sinks, 'attn_fwd',
                                          stages=[GatherStage([whole(n) for n in proj_names])])
    (p_l, p_a, merged, m, h2), ((w_down2,),) = merge_fwd(y_lru, y_attn, g_lru, g_attn, projs, sw['mix_post_g'], h1, 'merge_fwd',
                                                         stages=[GatherStage([whole('ffn2_w_down')])])
    w_gu2 = [w_gu2a, w_gu2b]
    (n2, g2, u2, a2), _ = ffn_fwd_a(h2, sw['ffn2_pre_g'], w_gu2, 'ffn2_fwd_a')
    (f2, dy, loss_blk), _ = ffn_fwd_b(a2, w_down2, sw['ffn2_post_g'], h2, 'ffn2_fwd_b', target=target)

    gs, full = {}, {}

    def pair_stage(names, grads):
        g4 = [g.reshape(N_CHIPS, 2, PACK_ROWS_OF[n] // 2, D_MODEL) for n, g in zip(names, grads)]
        return PairStage(g4), g4

    def pair_sums(names, g4, lands):
        return [pair_sum(g, l, c_arr, 'pair_sum_' + n) for n, g, l in zip(names, g4, lands)]

    def halves(s, n_parts=2):
        n = s.shape[1] // n_parts
        return [(s, p * n, n) for p in range(n_parts)]

    (df2, dgu2, gs['ffn2_post_g']), _ = ffn_bwd_a(dy, f2, sw['ffn2_post_g'], w_down2, g2, u2, 'ffn2_bwd_a')
    g_down2, _ = mm_tn([a2], df2, 1408, 'ffn2_dw_down')
    st, g4 = pair_stage(['ffn2_w_down'], [g_down2])
    g_gu2, (lands,) = mm_tn([dgu2], n2, 1408, 'ffn2_dw_gu', stages=[st])
    (s_down2,) = pair_sums(['ffn2_w_down'], g4, lands)
    st, g4 = pair_stage(['ffn2_w_gu'], [g_gu2])
    (dh2, gs['ffn2_pre_g']), ((l_down2,), lands) = norm_bwd([dgu2], w_gu2, h2, sw['ffn2_pre_g'], dy, 'ffn2_bwd_b',
                                                            stages=[ChipStage([(s_down2, 0, s_down2.shape[1])]), st])
    (s_gu2,) = pair_sums(['ffn2_w_gu'], g4, lands)

    (dm, dpl, dpa, dgl, dga, dya, dyl, gs['mix_post_g']), ((l_gu2a,),) = merge_bwd(
        dh2, m, sw['mix_post_g'], projs, g_lru, g_attn, p_l, p_a, 'merge_bwd', stages=[ChipStage(halves(s_gu2)[:1])])
    g_projs = [mm_tn([merged if n == 'w_out' else (y_lru if n == 'w_proj_lru' else y_attn)],
                     dm if n == 'w_out' else (dpl if n == 'w_proj_lru' else dpa), D_MODEL, 'd' + n)[0] for n in proj_names]
    st, g4 = pair_stage(proj_names, g_projs)
    (dq, dkv, dsk), ((l_gu2b,), lands, (full['ffn2_w_down'],)) = attn_bwd(
        qr, kr, v, dya, cos, sin_signed, sinks, 'attn_bwd', stages=[ChipStage(halves(s_gu2)[1:]), st, SwapStage([l_down2])])
    full['ffn2_w_down'] = [full['ffn2_w_down']]
    gs['attn_sinks'] = dsk[0:1, 0:N_Q_HEADS]
    s_projs = pair_sums(proj_names, g4, lands)
    (dgate, dxbr, vecs, dwa, dwx), (l_projs, full['ffn2_w_gu']) = lru_bwd(
        gate, xbr, h_lru, dyl, sw['conv_w'], sw['conv_b'], wa_bd, sw['lru_b_a'], wx_bd, sw['lru_b_x'], sw['lru_lambda'],
        'lru_bwd', stages=[ChipStage([(s, 0, s.shape[1]) for s in s_projs]), SwapStage([l_gu2a, l_gu2b])])
    gs['conv_w'] = vecs[0:4]
    gs['conv_b'], gs['lru_b_a'], gs['lru_b_x'], gs['lru_lambda'] = vecs[4:5], vecs[5:6], vecs[6:7], vecs[7:8]
    gs['lru_w_a'] = _diag_blocks(dwa)
    gs['lru_w_x'] = _diag_blocks(dwx)
    dz = [dgate, dxbr, dq, dkv, dgl, dga]
    g_in, ((lru_all,),) = mm_tn(dz, um, 512, 'dw_in', stages=[SmallGatherStage(_pack_lru(gs))])
    st, g4 = pair_stage(['w_in'], [g_in])
    (dh1, gs['mix_pre_g']), (lands, f_projs) = norm_bwd(dz, [w_in_t], h1, sw['mix_pre_g'], dh2, 'mix_bwd_in',
                                                        stages=[st, SwapStage(l_projs)])
    for n, f in zip(proj_names, f_projs):
        full[n] = [f]
    (s_in,) = pair_sums(['w_in'], g4, lands)

    (df1, dgu1, gs['ffn1_post_g']), ((l_in_a,),) = ffn_bwd_a(dh1, f1, sw['ffn1_post_g'], w_down1, g1, u1, 'ffn1_bwd_a',
                                                             stages=[ChipStage(halves(s_in)[:1])])
    g_down1, _ = mm_tn([a1], df1, 1408, 'ffn1_dw_down')
    st, g4 = pair_stage(['ffn1_w_down'], [g_down1])
    g_gu1, ((l_in_b,), lands) = mm_tn([dgu1], n1, 1408, 'ffn1_dw_gu', stages=[ChipStage(halves(s_in)[1:]), st])
    (s_down1,) = pair_sums(['ffn1_w_down'], g4, lands)
    st, g4 = pair_stage(['ffn1_w_gu'], [g_gu1])
    (dx, gs['ffn1_pre_g']), ((l_down1,), lands, full['w_in']) = norm_bwd(
        [dgu1], [w_gu1], xs, sw['ffn1_pre_g'], dh1, 'ffn1_bwd_b',
        stages=[ChipStage([(s_down1, 0, s_down1.shape[1])]), st, SwapStage([l_in_a, l_in_b])])
    (s_gu1,) = pair_sums(['ffn1_w_gu'], g4, lands)
    loss_row = jnp.pad(loss_blk[0:1], ((0, 0), (0, D_MODEL - loss_blk.shape[1])))
    vec_blk = _pack_vecs(gs, jnp.concatenate([gs['conv_w'], loss_row], axis=0))
    send, recv, s_thru, land_thru, token = chip_exchange_start(s_gu1)
    out_g, out_d, out_m, out_v = {}, {}, {}, {}

    def adam(n, after=()):
        fn = adam_cols if dict((k, t) for k, _, t in PACK)[n] else adam_rows
        g_, d_, m_, v_ = fn(full[n], n, w[n][0], mom[n][0], var[n][0], after=after)
        out_g[n], out_d[n], out_m[n], out_v[n] = g_[None], d_[None], m_[None], v_[None]

    behind = token
    for n in ['ffn2_w_gu', 'w_in', 'ffn2_w_down'] + proj_names:
        adam(n, after=(behind,))
        behind = out_v[n]
    s_back, l_gu1 = chip_exchange_wait(send, recv, s_thru, land_thru, after=behind)
    own = lax.dynamic_slice_in_dim(s_back, me_q, 1, axis=0)
    l_gu1 = lax.dynamic_update_slice_in_dim(l_gu1, own, me_q, axis=0)
    (vec_all,), (f_down1, f_gu1) = comm_call('swap_last', [SmallGatherStage(vec_blk), SwapStage([l_down1, l_gu1])])
    full['ffn1_w_down'] = [f_down1]
    full['ffn1_w_gu'] = [f_gu1]
    adam('ffn1_w_gu')
    adam('ffn1_w_down')

    tot = small_sum(vec_all, lru_all)
    loss = tot[ROW_WA - 1, 0]
    conv_g = lax.dynamic_slice(tot[ROW_CONV:ROW_CONV + 4], (0, me_q * (LRU_W // N_CHIPS)), (4, LRU_W // N_CHIPS))
    small_g = _unpack_small(tot, shapes)
    small_g['conv_w'] = conv_g.reshape(shapes['conv_w'])
    g_pack = jnp.concatenate([tot[:ROW_CONV], conv_g.reshape(1, D_MODEL), jnp.zeros((ROW_WA - ROW_CONV - 1, D_MODEL), F32),
                              tot[ROW_WA:]], axis=0)
    packs = [_pack_small({n: d[n] for n in SMALL}, d['conv_w'].reshape(1, D_MODEL)) for d in (w, mom, var)]
    d_p, m_p, v_p = adam_small(g_pack, *packs)
    for n in SMALL:
        out_g[n] = small_g[n]
    for dst, p in ((out_d, d_p), (out_m, m_p), (out_v, v_p)):
        dst.update(_unpack_small(p, shapes))

    return (loss, dx[None], *[out_g[n] for n in WEIGHTS], *[out_d[n] for n in WEIGHTS],
            *[out_m[n] for n in WEIGHTS], *[out_v[n] for n in WEIGHTS])
```

```python
import jax
import jax.numpy as jnp
import numpy as np
from jax import lax
from jax.experimental import pallas as pl
from jax.experimental.pallas import tpu as pltpu

F32 = jnp.float32
BF16 = jnp.bfloat16

SEQ = 2048
D_MODEL = 1024
D_FF = 2816
LRU_W = 1024
LRU_BLOCK_W = 64
HEAD_DIM = 64
N_Q_HEADS = 16
N_KV_HEADS = 4
KV_W = N_KV_HEADS * HEAD_DIM
ATTN_BLOCK = 128
N_ATTN_BLOCKS = SEQ // ATTN_BLOCK
IN_SEGS = (1024, 1024, 1024, 256, 256, 1024, 1024)
IN_W = sum(IN_SEGS)
NORM_EPS = 1e-6
MASK_VALUE = -1e30
ROPE_THETA = 10000.0
LRU_C = 8.0
MACARON = 0.5
ADAM_LR = 0.001
ADAM_B1 = 0.9
ADAM_B2 = 0.999
ADAM_EPS = 1e-08
ADAM_WD = 0.01
ADAM_STEP = 10

N_CHIPS = 4
N_DEV = 8
VMEM_LIMIT = 56 * 1024 * 1024
MM_ROWS = 256
MESH = pl.DeviceIdType.MESH
ANY = pl.BlockSpec(memory_space=pl.ANY)

PACK = (('ffn1_w_gu', 1408, True), ('w_in', 1408, True), ('ffn2_w_gu', 1408, True),
        ('ffn1_w_down', 704, False), ('ffn2_w_down', 704, False),
        ('w_proj_lru', 256, False), ('w_proj_attn', 256, False), ('w_out', 256, False))
PACK_ROWS_OF = {n: r for n, r, _ in PACK}
PACK_OFF = {}
_o = 0
for _n, _r, _t in PACK:
    PACK_OFF[_n] = _o
    _o += _r

SMALL_VECS = ('ffn1_pre_g', 'ffn1_post_g', 'mix_pre_g', 'conv_b', 'lru_b_a', 'lru_b_x', 'lru_lambda',
              'mix_post_g', 'ffn2_pre_g', 'ffn2_post_g')
SMALL_ROWS = 144
ROW_SINKS, ROW_CONV, ROW_WA, ROW_WX = 10, 11, 16, 80


def _dot(a, b):
    return jnp.dot(a, b, preferred_element_type=F32)


def _dot_nt(a, b):
    return lax.dot_general(a, b, (((1,), (1,)), ((), ())), preferred_element_type=F32)


def _dot_tn(a, b):
    return lax.dot_general(a, b, (((0,), (0,)), ((), ())), preferred_element_type=F32)


def _params(n_grid):
    return pltpu.CompilerParams(dimension_semantics=("arbitrary",) * n_grid, vmem_limit_bytes=VMEM_LIMIT)


def _sigmoid(x):
    return 1.0 / (1.0 + jnp.exp(-x))


def _rsqrt_mean_sq(x):
    return lax.rsqrt(jnp.mean(x * x, axis=-1, keepdims=True) + NORM_EPS)


def _expm1(x):
    poly = x * (1.0 + x * (0.5 + x * (1.0 / 6.0 + x * (1.0 / 24.0 + x * (1.0 / 120.0)))))
    return jnp.where(jnp.abs(x) < 0.1, poly, jnp.exp(x) - 1.0)


_GELU_K = 0.7978845608028654
_GELU_C = 0.044715


def _gelu(x):
    t = jnp.tanh(_GELU_K * (x + _GELU_C * x * x * x))
    return 0.5 * x * (1.0 + t), t


def _gelu_grad(x, t):
    return 0.5 * (1.0 + t) + 0.5 * x * (1.0 - t * t) * _GELU_K * (1.0 + 3.0 * _GELU_C * x * x)


def _load_weight(w_refs, dst_ref, sem):
    w_refs = list(w_refs) if isinstance(w_refs, (list, tuple)) else [w_refs]
    rows = dst_ref.shape[0] // N_CHIPS
    rp = rows // len(w_refs)
    cps = [pltpu.make_async_copy(w_ref.at[q], dst_ref.at[pl.ds(q * rows + p * rp, rp)], sem.at[p * N_CHIPS + q])
           for p, w_ref in enumerate(w_refs) for q in range(N_CHIPS)]
    for cp in cps:
        cp.start()
    for cp in cps:
        cp.wait()


def _weight_scratch(rows_total, parts=1):
    return [pltpu.VMEM((rows_total, D_MODEL), BF16), pltpu.SemaphoreType.DMA((N_CHIPS * parts,))]


_ROW = lambda tm: pl.BlockSpec((tm, D_MODEL), lambda i: (i, 0))
_VEC = pl.BlockSpec((1, D_MODEL), lambda i: (0, 0))


def _call(body, *, name, grid, in_specs, out_specs, out_shape, args, scratch_shapes=(), stages=()):
    in_specs, out_specs, out_shape, scratch_shapes = list(in_specs), list(out_specs), list(out_shape), list(scratch_shapes)
    n_in, n_out, n_sc = len(in_specs), len(out_specs), len(scratch_shapes)
    k_in = [len(s.inputs) for s in stages]
    k_out = [len(s.out_shape) for s in stages]
    k_sc = [len(s.scratch) for s in stages]
    last = grid[0] - 1

    def split(refs, counts):
        parts, pos = [], 0
        for k in counts:
            parts.append(refs[pos:pos + k])
            pos += k
        return parts

    kinds = tuple(sorted({k for s in stages for k in s.peers}))
    collective_id = {(): None, ('sib',): 0, ('chips',): 1, ('chips', 'sib'): 2}[kinds]

    def full(*refs):
        ins, s_ins, outs, s_outs, scr, s_scr = split(refs, [n_in, sum(k_in), n_out, sum(k_out), n_sc, sum(k_sc)])
        per_stage = list(zip(stages, split(s_ins, k_in), split(s_outs, k_out), split(s_scr, k_sc)))
        i = pl.program_id(0)
        if stages:
            @pl.when(i == 0)
            def _():
                x, y, c, chips = _place()
                peers = ([(x, y, 1 - c)] if 'sib' in kinds else []) + ([(cx, cy, c) for cx, cy in chips] if 'chips' in kinds else [])
                barrier = pltpu.get_barrier_semaphore()
                for peer in peers:
                    pl.semaphore_signal(barrier, inc=1, device_id=peer, device_id_type=MESH)
                pl.semaphore_wait(barrier, len(peers))
                for s, a, b, c_ in per_stage:
                    s.start(a, b, c_)

        body(*ins, *outs, *scr)
        if stages:
            @pl.when(i == last // 2)
            def _():
                for s, a, b, c in per_stage:
                    s.relay(a, b, c)

            @pl.when(i == max(last - 1, 0))
            def _():
                for s, a, b, c in per_stage:
                    s.mid(a, b, c)

            @pl.when(i == last)
            def _():
                for s, a, b, c in per_stage:
                    s.end(a, b, c)

    res = pl.pallas_call(
        full, name=name, grid=grid,
        in_specs=in_specs + [ANY] * sum(k_in),
        out_specs=out_specs + [ANY] * sum(k_out),
        out_shape=out_shape + [o for s in stages for o in s.out_shape],
        scratch_shapes=scratch_shapes + [x for s in stages for x in s.scratch],
        compiler_params=pltpu.CompilerParams(dimension_semantics=("arbitrary",), vmem_limit_bytes=VMEM_LIMIT,
                                             collective_id=collective_id),
    )(*args, *[a for s in stages for a in s.inputs])
    return list(res[:n_out]), split(list(res[n_out:]), k_out)


def ffn_fwd_a(x, g_pre, w_gu_t, name, stages=()):
    tm, tn = MM_ROWS, 256
    n_w = len(w_gu_t)

    def body(x_ref, gp_ref, *refs):
        w_refs = refs[:n_w]
        n_ref, g_ref, u_ref, a_ref, wt_ref, sem = refs[n_w:]

        @pl.when(pl.program_id(0) == 0)
        def _():
            _load_weight(w_refs, wt_ref, sem)

        xv = x_ref[...]
        n = (xv * _rsqrt_mean_sq(xv) * gp_ref[...]).astype(BF16)
        n_ref[...] = n
        for j in range(D_FF // tn):
            g = _dot_nt(n, wt_ref[j * tn:(j + 1) * tn, :])
            u = _dot_nt(n, wt_ref[D_FF + j * tn:D_FF + (j + 1) * tn, :])
            g_ref[:, j * tn:(j + 1) * tn] = g.astype(BF16)
            u_ref[:, j * tn:(j + 1) * tn] = u.astype(BF16)
            a_ref[:, j * tn:(j + 1) * tn] = (g * _sigmoid(g) * u).astype(BF16)

    wide = pl.BlockSpec((tm, D_FF), lambda i: (i, 0))
    return _call(
        body, name=name, grid=(SEQ // tm,),
        in_specs=[_ROW(tm), _VEC] + [ANY] * n_w,
        out_specs=[_ROW(tm), wide, wide, wide],
        out_shape=[jax.ShapeDtypeStruct((SEQ, D_MODEL), BF16)] + [jax.ShapeDtypeStruct((SEQ, D_FF), BF16)] * 3,
        scratch_shapes=_weight_scratch(2 * D_FF, n_w),
        args=[x, g_pre, *w_gu_t], stages=stages)


def ffn_fwd_b(a, w_down, g_post, h_in, name, target=None, stages=()):
    tm = MM_ROWS
    final = target is not None

    def body(*refs):
        if final:
            a_ref, wf_ref, gp_ref, h_ref, t_ref, f_ref, o_ref, loss_ref, wd_ref, sem = refs
        else:
            a_ref, wf_ref, gp_ref, h_ref, f_ref, o_ref, wd_ref, sem = refs

        @pl.when(pl.program_id(0) == 0)
        def _():
            _load_weight(wf_ref, wd_ref, sem)
            if final:
                loss_ref[...] = jnp.zeros_like(loss_ref)

        f = _dot(a_ref[...], wd_ref[...])
        f_ref[...] = f
        y = h_ref[...] + MACARON * (f * _rsqrt_mean_sq(f) * gp_ref[...])
        if final:
            err = y - t_ref[...]
            o_ref[...] = err * (1.0 / D_MODEL)
            loss_ref[...] += 0.5 * jnp.sum(err * err) * (1.0 / D_MODEL)
        else:
            o_ref[...] = y

    row = _ROW(tm)
    in_specs = [pl.BlockSpec((tm, D_FF), lambda i: (i, 0)), ANY, _VEC, row]
    out_specs = [row, row]
    out_shape = [jax.ShapeDtypeStruct((SEQ, D_MODEL), F32)] * 2
    args = [a, w_down, g_post, h_in]
    if final:
        in_specs.append(row)
        args.append(target)
        out_specs.append(pl.BlockSpec((8, 128), lambda i: (0, 0)))
        out_shape.append(jax.ShapeDtypeStruct((8, 128), F32))
    return _call(body, name=name, grid=(SEQ // tm,), in_specs=in_specs, out_specs=out_specs,
                 out_shape=out_shape, scratch_shapes=_weight_scratch(D_FF), args=args, stages=stages)


def ffn_bwd_a(d_out, f, g_post, w_down, g, u, name, stages=()):
    tm = MM_ROWS
    tc = 256

    def body(do_ref, f_ref, gp_ref, wf_ref, g_ref, u_ref, df_ref, dgu_ref, dgp_ref, wd_ref, sem):
        @pl.when(pl.program_id(0) == 0)
        def _():
            _load_weight(wf_ref, wd_ref, sem)
            dgp_ref[...] = jnp.zeros_like(dgp_ref)

        fv = f_ref[...]
        rf = _rsqrt_mean_sq(fv)
        fh = fv * rf
        dn = MACARON * do_ref[...]
        dgp_ref[...] += jnp.sum(dn * fh, axis=0, keepdims=True)
        t = dn * gp_ref[...]
        df = (rf * (t - fh * jnp.mean(t * fh, axis=-1, keepdims=True))).astype(BF16)
        df_ref[...] = df
        for c0 in range(0, D_FF, tc):
            da = _dot_nt(df, wd_ref[c0:c0 + tc, :])
            gv = g_ref[:, c0:c0 + tc].astype(F32)
            uv = u_ref[:, c0:c0 + tc].astype(F32)
            s = _sigmoid(gv)
            dgu_ref[:, c0:c0 + tc] = (da * uv * s * (1.0 + gv * (1.0 - s))).astype(BF16)
            dgu_ref[:, D_FF + c0:D_FF + c0 + tc] = (da * gv * s).astype(BF16)

    row = _ROW(tm)
    wide = pl.BlockSpec((tm, D_FF), lambda i: (i, 0))
    return _call(
        body, name=name, grid=(SEQ // tm,),
        in_specs=[row, row, _VEC, ANY, wide, wide],
        out_specs=[row, pl.BlockSpec((tm, 2 * D_FF), lambda i: (i, 0)), _VEC],
        out_shape=[jax.ShapeDtypeStruct((SEQ, D_MODEL), BF16), jax.ShapeDtypeStruct((SEQ, 2 * D_FF), BF16),
                   jax.ShapeDtypeStruct((1, D_MODEL), F32)],
        scratch_shapes=_weight_scratch(D_FF),
        args=[d_out, f, g_post, w_down, g, u], stages=stages)


def norm_bwd(pieces, w_t, x, g_pre, d_res, name, stages=()):
    tm = MM_ROWS
    widths = [p.shape[1] for p in pieces]
    offs = [sum(widths[:k]) for k in range(len(widths))]
    n_p = len(pieces)
    n_w = len(w_t)

    def body(*refs):
        p_refs = refs[:n_p]
        w_refs = refs[n_p:n_p + n_w]
        x_ref, g_ref, r_ref, dx_ref, dg_ref, wt_ref, sem = refs[n_p + n_w:]

        @pl.when(pl.program_id(0) == 0)
        def _():
            _load_weight(w_refs, wt_ref, sem)
            dg_ref[...] = jnp.zeros_like(dg_ref)

        dn = None
        for p_ref, lo, wd in zip(p_refs, offs, widths):
            part = _dot(p_ref[...], wt_ref[lo:lo + wd, :])
            dn = part if dn is None else dn + part
        xv = x_ref[...]
        r = _rsqrt_mean_sq(xv)
        xh = xv * r
        dg_ref[...] += jnp.sum(dn * xh, axis=0, keepdims=True)
        t = dn * g_ref[...]
        dx_ref[...] = r_ref[...] + r * (t - xh * jnp.mean(t * xh, axis=-1, keepdims=True))

    row = _ROW(tm)
    return _call(
        body, name=name, grid=(SEQ // tm,),
        in_specs=[pl.BlockSpec((tm, wd), lambda i: (i, 0)) for wd in widths] + [ANY] * n_w + [row, _VEC, row],
        out_specs=[row, _VEC],
        out_shape=[jax.ShapeDtypeStruct((SEQ, D_MODEL), F32), jax.ShapeDtypeStruct((1, D_MODEL), F32)],
        scratch_shapes=_weight_scratch(sum(widths), n_w),
        args=[*pieces, *w_t, x, g_pre, d_res], stages=stages)


def mm_tn(pieces, b, tm, name, stages=()):
    widths = [p.shape[1] for p in pieces]
    m_total = sum(widths)
    n_p = len(pieces)
    starts = [sum(widths[:k]) // tm for k in range(n_p)]
    counts = [wd // tm for wd in widths]

    def body(*refs):
        p_refs = refs[:n_p]
        b_ref, o_ref = refs[n_p:]
        i = pl.program_id(0)
        for p_ref, st, ct in zip(p_refs, starts, counts):
            @pl.when((i >= st) & (i < st + ct))
            def _(p_ref=p_ref):
                o_ref[...] = _dot_tn(p_ref[...], b_ref[...]).astype(BF16)

    def piece_spec(st, ct):
        return pl.BlockSpec((SEQ, tm), lambda i: (0, jnp.clip(i - st, 0, ct - 1)))

    (out,), stage_out = _call(
        body, name=name, grid=(m_total // tm,),
        in_specs=[piece_spec(st, ct) for st, ct in zip(starts, counts)] + [pl.BlockSpec((SEQ, D_MODEL), lambda i: (0, 0))],
        out_specs=[pl.BlockSpec((tm, D_MODEL), lambda i: (i, 0))],
        out_shape=[jax.ShapeDtypeStruct((m_total, D_MODEL), BF16)],
        args=[*pieces, b], stages=stages)
    return out, stage_out


def mix_in(h, g_pre, w_in_t, name, stages=()):
    tm = MM_ROWS
    offs = [sum(IN_SEGS[:k]) for k in range(len(IN_SEGS))]
    dts = [F32, F32, F32, F32, BF16, F32, F32]
    n_o = len(IN_SEGS)

    def body(*refs):
        h_ref, g_ref, wf_ref, um_ref = refs[:4]
        o_refs = refs[4:4 + n_o]
        wt_ref, sem = refs[4 + n_o:]

        @pl.when(pl.program_id(0) == 0)
        def _():
            _load_weight(wf_ref, wt_ref, sem)

        hv = h_ref[...]
        um = (hv * _rsqrt_mean_sq(hv) * g_ref[...]).astype(BF16)
        um_ref[...] = um
        for o_ref, lo, wd in zip(o_refs, offs, IN_SEGS):
            for c0 in range(0, wd, 256):
                o_ref[:, c0:c0 + 256] = _dot_nt(um, wt_ref[lo + c0:lo + c0 + 256, :]).astype(o_ref.dtype)

    return _call(
        body, name=name, grid=(SEQ // tm,),
        in_specs=[_ROW(tm), _VEC, ANY],
        out_specs=[_ROW(tm)] + [pl.BlockSpec((tm, wd), lambda i: (i, 0)) for wd in IN_SEGS],
        out_shape=[jax.ShapeDtypeStruct((SEQ, D_MODEL), BF16)]
        + [jax.ShapeDtypeStruct((SEQ, wd), dt) for wd, dt in zip(IN_SEGS, dts)],
        scratch_shapes=_weight_scratch(IN_W),
        args=[h, g_pre, w_in_t], stages=stages)


LRU_TC = 256


def _conv_fwd(xb, cw, cb, tt):
    xc = xb * cw[3:4, :] + cb
    shifted = []
    for s in (1, 2, 3):
        sh = jnp.where(tt >= s, pltpu.roll(xb, s, 0), 0.0)
        shifted.append(sh)
        xc = xc + sh * cw[3 - s:4 - s, :]
    return xc, shifted


def _lru_gates(xc, wa, ba, wx, bx, lam):
    xcb = xc.astype(BF16)
    r = _sigmoid(_dot(xcb, wa) + ba)
    i = _sigmoid(_dot(xcb, wx) + bx)
    nl = -lam
    sp = jnp.maximum(nl, 0.0) + jnp.log1p(jnp.exp(-jnp.abs(nl)))
    la = (-LRU_C * r) * sp
    a = jnp.exp(la)
    mult = jnp.sqrt(jnp.maximum(-_expm1(2.0 * la), 0.0))
    return xcb, r, i, sp, a, mult


def _scan(a, b, tt, reverse):
    n = a.shape[0]
    s = 1
    while s < n:
        more = 2 * s < n
        if s < 8:
            if reverse:
                keep = tt < n - s
                shift = n - s
            else:
                keep = tt >= s
                shift = s
            b = a * jnp.where(keep, pltpu.roll(b, shift, 0), 0.0) + b
            if more:
                a = a * jnp.where(keep, pltpu.roll(a, shift, 0), 1.0)
        elif reverse:
            b = jnp.concatenate([a[:n - s] * b[s:] + b[:n - s], b[n - s:]], axis=0)
            if more:
                a = jnp.concatenate([a[:n - s] * a[s:], a[n - s:]], axis=0)
        else:
            b = jnp.concatenate([b[:s], a[s:] * b[:n - s] + b[s:]], axis=0)
            if more:
                a = jnp.concatenate([a[:s], a[s:] * a[:n - s]], axis=0)
        s *= 2
    return b


def _lru_specs():
    col = pl.BlockSpec((SEQ, LRU_TC), lambda j: (0, j))
    vec = pl.BlockSpec((1, LRU_TC), lambda j: (0, j))
    bd = pl.BlockSpec((1, LRU_TC, LRU_TC), lambda j: (j, 0, 0))
    cw = pl.BlockSpec((4, LRU_TC), lambda j: (0, j))
    return col, vec, bd, cw


def lru_fwd(gate, xbr, conv_w, conv_b, wa_bd, b_a, wx_bd, b_x, lam, name, stages=()):
    col, vec, bd, cw = _lru_specs()

    def body(gate_ref, xbr_ref, cw_ref, cb_ref, wa_ref, ba_ref, wx_ref, bx_ref, lam_ref, y_ref, h_ref):
        tt = lax.broadcasted_iota(jnp.int32, (SEQ, LRU_TC), 0)
        xc, _ = _conv_fwd(xbr_ref[...], cw_ref[...], cb_ref[...], tt)
        _, r, i, sp, a, mult = _lru_gates(xc, wa_ref[0], ba_ref[...], wx_ref[0], bx_ref[...], lam_ref[...])
        h = _scan(a, mult * (i * xc), tt, reverse=False)
        h_ref[...] = h
        gl, _ = _gelu(gate_ref[...])
        y_ref[...] = (h * gl).astype(BF16)

    return _call(
        body, name=name, grid=(LRU_W // LRU_TC,),
        in_specs=[col, col, cw, vec, bd, vec, bd, vec, vec],
        out_specs=[col, col],
        out_shape=[jax.ShapeDtypeStruct((SEQ, LRU_W), BF16), jax.ShapeDtypeStruct((SEQ, LRU_W), F32)],
        args=[gate, xbr, conv_w, conv_b, wa_bd, b_a, wx_bd, b_x, lam], stages=stages)


def lru_bwd(gate, xbr, h, dy, conv_w, conv_b, wa_bd, b_a, wx_bd, b_x, lam, name, stages=()):
    col, vec, bd, cw = _lru_specs()

    def body(gate_ref, xbr_ref, h_ref, dy_ref, cw_ref, cb_ref, wa_ref, ba_ref, wx_ref, bx_ref, lam_ref,
             dgate_ref, dxbr_ref, vecs_ref, dwa_ref, dwx_ref):
        tt = lax.broadcasted_iota(jnp.int32, (SEQ, LRU_TC), 0)
        cwv = cw_ref[...]
        lam = lam_ref[...]
        xb = xbr_ref[...]
        xc, shifted = _conv_fwd(xb, cwv, cb_ref[...], tt)
        wa = wa_ref[0]
        wx = wx_ref[0]
        xcb, r, i, sp, a, mult = _lru_gates(xc, wa, ba_ref[...], wx, bx_ref[...], lam)
        hv = h_ref[...]
        dyv = dy_ref[...]
        gv = gate_ref[...]
        gl, th = _gelu(gv)
        dgate_ref[...] = (dyv * hv * _gelu_grad(gv, th)).astype(BF16)
        a_next = jnp.where(tt < SEQ - 1, pltpu.roll(a, SEQ - 1, 0), 0.0)
        gsum = _scan(a_next, dyv * gl, tt, reverse=True)
        h_prev = jnp.where(tt >= 1, pltpu.roll(hv, 1, 0), 0.0)
        d_mult = gsum * i * xc
        d_i = gsum * mult * xc
        d_xc = gsum * mult * i
        d_la = gsum * h_prev * a - d_mult * (a * a) / mult
        d_pr = (d_la * (-LRU_C * sp)) * r * (1.0 - r)
        d_pi = d_i * i * (1.0 - i)
        d_lam = jnp.sum(d_la * r, axis=0, keepdims=True) * (LRU_C * _sigmoid(-lam))
        d_prb = d_pr.astype(BF16)
        d_pib = d_pi.astype(BF16)
        d_xc = d_xc + _dot_nt(d_prb, wa) + _dot_nt(d_pib, wx)
        dwa_ref[0] = _dot_tn(xcb, d_prb)
        dwx_ref[0] = _dot_tn(xcb, d_pib)
        rows = [jnp.sum(d_xc * shifted[2], axis=0, keepdims=True),
                jnp.sum(d_xc * shifted[1], axis=0, keepdims=True),
                jnp.sum(d_xc * shifted[0], axis=0, keepdims=True),
                jnp.sum(d_xc * xb, axis=0, keepdims=True),
                jnp.sum(d_xc, axis=0, keepdims=True),
                jnp.sum(d_pr, axis=0, keepdims=True),
                jnp.sum(d_pi, axis=0, keepdims=True),
                d_lam]
        ri = lax.broadcasted_iota(jnp.int32, (8, LRU_TC), 0)
        acc = jnp.zeros((8, LRU_TC), F32)
        for k, rv in enumerate(rows):
            acc = jnp.where(ri == k, rv, acc)
        vecs_ref[...] = acc
        d_xb = d_xc * cwv[3:4, :]
        for s in (1, 2, 3):
            d_xb = d_xb + jnp.where(tt < SEQ - s, pltpu.roll(d_xc, SEQ - s, 0), 0.0) * cwv[3 - s:4 - s, :]
        dxbr_ref[...] = d_xb.astype(BF16)

    return _call(
        body, name=name, grid=(LRU_W // LRU_TC,),
        in_specs=[col, col, col, col, cw, vec, bd, vec, bd, vec, vec],
        out_specs=[col, col, pl.BlockSpec((8, LRU_TC), lambda j: (0, j)), bd, bd],
        out_shape=[jax.ShapeDtypeStruct((SEQ, LRU_W), BF16), jax.ShapeDtypeStruct((SEQ, LRU_W), BF16),
                   jax.ShapeDtypeStruct((8, LRU_W), F32),
                   jax.ShapeDtypeStruct((LRU_W // LRU_TC, LRU_TC, LRU_TC), F32),
                   jax.ShapeDtypeStruct((LRU_W // LRU_TC, LRU_TC, LRU_TC), F32)],
        args=[gate, xbr, h, dy, conv_w, conv_b, wa_bd, b_a, wx_bd, b_x, lam], stages=stages)


def _rope(x, cos, sin_signed):
    w = x.shape[1]
    reps = w // 128
    if reps > 1:
        cos = jnp.tile(cos, (1, reps))
        sin_signed = jnp.tile(sin_signed, (1, reps))
    lane = lax.broadcasted_iota(jnp.int32, x.shape, 1)
    first = (lane & 63) < 32
    partner = jnp.where(first, pltpu.roll(x, w - 32, 1), pltpu.roll(x, 32, 1))
    return x * cos + partner * sin_signed


def _both_halves(t, odd):
    lo = lax.broadcasted_iota(jnp.int32, t.shape, 1) < 64
    rolled = pltpu.roll(t, 64, 1)
    return jnp.where(lo, rolled, t) if odd else jnp.where(lo, t, rolled)


def _stack_heads(ta, tb):
    lo = lax.broadcasted_iota(jnp.int32, ta.shape, 1) < 64
    return jnp.concatenate([jnp.where(lo, ta, 0.0), jnp.where(lo, 0.0, ta),
                            jnp.where(lo, tb, 0.0), jnp.where(lo, 0.0, tb)], axis=0)


def _unstack_heads(o):
    lo = lax.broadcasted_iota(jnp.int32, (ATTN_BLOCK, 128), 1) < 64
    return (jnp.where(lo, o[0:128], o[128:256]), jnp.where(lo, o[256:384], o[384:512]))


def _attn_probs(qs, kd, sinks_ref, hk, first_block):
    s = _dot_nt(qs, kd) * (HEAD_DIM ** -0.5)
    row = lax.broadcasted_iota(jnp.int32, s.shape, 0)
    si = lax.broadcasted_iota(jnp.int32, s.shape, 1)
    diff = ATTN_BLOCK + (row & (ATTN_BLOCK - 1)) - si
    valid = (diff >= 0) & (diff < ATTN_BLOCK) & ((si >= ATTN_BLOCK) | jnp.logical_not(first_block))
    s = jnp.where(valid, s, MASK_VALUE)
    rg = lax.broadcasted_iota(jnp.int32, (4 * ATTN_BLOCK, 1), 0) >> 7
    sink = jnp.where(rg == 0, sinks_ref[4 * hk],
                     jnp.where(rg == 1, sinks_ref[4 * hk + 1],
                               jnp.where(rg == 2, sinks_ref[4 * hk + 2], sinks_ref[4 * hk + 3])))
    m = jnp.maximum(jnp.max(s, axis=1, keepdims=True), sink)
    e = jnp.exp(s - m)
    es = jnp.exp(sink - m)
    inv = 1.0 / (jnp.sum(e, axis=1, keepdims=True) + es)
    return e * inv, es * inv


def _prev(i):
    return jnp.maximum(i - 1, 0)


def attn_fwd(q, k, v, cos, sin_signed, sinks, name, stages=()):
    nb = ATTN_BLOCK

    def body(q_ref, kc_ref, kp_ref, vc_ref, vp_ref, cc_ref, sc_ref, cp_ref, sp_ref, sinks_ref,
             qr_ref, kr_ref, y_ref):
        first_block = pl.program_id(0) == 0
        qr = _rope(q_ref[...], cc_ref[...], sc_ref[...])
        kc = _rope(kc_ref[...], cc_ref[...], sc_ref[...])
        kp = _rope(kp_ref[...], cp_ref[...], sp_ref[...])
        qr_ref[...] = qr.astype(BF16)
        kr_ref[...] = kc.astype(BF16)
        k2 = jnp.concatenate([kp, kc], axis=0)
        v2 = jnp.concatenate([vp_ref[...].astype(F32), vc_ref[...].astype(F32)], axis=0)
        for hk in range(N_KV_HEADS):
            kt = hk // 2
            kd = _both_halves(k2[:, kt * 128:(kt + 1) * 128], hk % 2).astype(BF16)
            vd = _both_halves(v2[:, kt * 128:(kt + 1) * 128], hk % 2).astype(BF16)
            qs = _stack_heads(qr[:, (2 * hk) * 128:(2 * hk + 1) * 128],
                              qr[:, (2 * hk + 1) * 128:(2 * hk + 2) * 128]).astype(BF16)
            p, _ = _attn_probs(qs, kd, sinks_ref, hk, first_block)
            ta, tb = _unstack_heads(_dot(p.astype(BF16), vd))
            y_ref[:, (2 * hk) * 128:(2 * hk + 1) * 128] = ta.astype(BF16)
            y_ref[:, (2 * hk + 1) * 128:(2 * hk + 2) * 128] = tb.astype(BF16)

    cur = lambda w: pl.BlockSpec((nb, w), lambda i: (i, 0))
    prv = lambda w: pl.BlockSpec((nb, w), lambda i: (_prev(i), 0))
    return _call(
        body, name=name, grid=(N_ATTN_BLOCKS,),
        in_specs=[cur(D_MODEL), cur(KV_W), prv(KV_W), cur(KV_W), prv(KV_W), cur(128), cur(128), prv(128), prv(128),
                  pl.BlockSpec(memory_space=pltpu.SMEM)],
        out_specs=[cur(D_MODEL), cur(KV_W), cur(D_MODEL)],
        out_shape=[jax.ShapeDtypeStruct((SEQ, D_MODEL), BF16), jax.ShapeDtypeStruct((SEQ, KV_W), BF16),
                   jax.ShapeDtypeStruct((SEQ, D_MODEL), BF16)],
        args=[q, k, k, v, v, cos, sin_signed, cos, sin_signed, sinks], stages=stages)


def attn_bwd(qr, kr, v, dy, cos, sin_signed, sinks, name, stages=()):
    nb = ATTN_BLOCK
    n_steps = N_ATTN_BLOCKS + 1
    scale = HEAD_DIM ** -0.5

    def body(q_ref, kc_ref, kp_ref, vc_ref, vp_ref, dy_ref, cc_ref, sc_ref, cp_ref, sp_ref, sinks_ref,
             dq_ref, dkv_ref, dsk_ref, ck_ref, cv_ref):
        dk_ref = dkv_ref.at[:, pl.ds(0, KV_W)]
        dv_ref = dkv_ref.at[:, pl.ds(KV_W, KV_W)]
        i = pl.program_id(0)

        @pl.when(i == 0)
        def _():
            dsk_ref[...] = jnp.zeros_like(dsk_ref)
            ck_ref[...] = jnp.zeros_like(ck_ref)
            cv_ref[...] = jnp.zeros_like(cv_ref)

        @pl.when(i < N_ATTN_BLOCKS)
        def _():
            qv = q_ref[...].astype(F32)
            dov = dy_ref[...].astype(F32)
            k2 = jnp.concatenate([kp_ref[...].astype(F32), kc_ref[...].astype(F32)], axis=0)
            v2 = jnp.concatenate([vp_ref[...].astype(F32), vc_ref[...].astype(F32)], axis=0)
            lane = lax.broadcasted_iota(jnp.int32, (8, 128), 1)
            lo = lax.broadcasted_iota(jnp.int32, (2 * nb, 128), 1) < 64
            dsk = jnp.zeros((8, 128), F32)
            dk_tiles = []
            dv_tiles = []
            for hk in range(N_KV_HEADS):
                kt = hk // 2
                kd = _both_halves(k2[:, kt * 128:(kt + 1) * 128], hk % 2).astype(BF16)
                vd = _both_halves(v2[:, kt * 128:(kt + 1) * 128], hk % 2).astype(BF16)
                qs = _stack_heads(qv[:, (2 * hk) * 128:(2 * hk + 1) * 128],
                                  qv[:, (2 * hk + 1) * 128:(2 * hk + 2) * 128]).astype(BF16)
                dos = _stack_heads(dov[:, (2 * hk) * 128:(2 * hk + 1) * 128],
                                   dov[:, (2 * hk + 1) * 128:(2 * hk + 2) * 128]).astype(BF16)
                p, ps = _attn_probs(qs, kd, sinks_ref, hk, i == 0)
                dp = _dot_nt(dos, vd)
                delta = jnp.sum(p * dp, axis=1, keepdims=True)
                ds = (p * (dp - delta)).astype(BF16)
                dsink = -ps * delta
                for g in range(4):
                    dsk = dsk + jnp.where(lane == 4 * hk + g, jnp.sum(dsink[g * nb:(g + 1) * nb]), 0.0)
                ta, tb = _unstack_heads(_dot(ds, kd) * scale)
                dq_a = (2 * hk) * 128
                dq_ref[:, dq_a:dq_a + 128] = _rope(ta, cc_ref[...], -sc_ref[...]).astype(BF16)
                dq_ref[:, dq_a + 128:dq_a + 256] = _rope(tb, cc_ref[...], -sc_ref[...]).astype(BF16)
                rk = _dot_tn(ds, qs) * scale
                rv = _dot_tn(p.astype(BF16), dos)
                dk_tiles.append(rk + pltpu.roll(rk, 64, 1))
                dv_tiles.append(rv + pltpu.roll(rv, 64, 1))
            dsk_ref[...] += dsk
            dk_full = jnp.concatenate([jnp.where(lo, dk_tiles[0], dk_tiles[1]),
                                       jnp.where(lo, dk_tiles[2], dk_tiles[3])], axis=1)
            dv_full = jnp.concatenate([jnp.where(lo, dv_tiles[0], dv_tiles[1]),
                                       jnp.where(lo, dv_tiles[2], dv_tiles[3])], axis=1)
            dk_ref[...] = _rope(ck_ref[...] + dk_full[0:nb], cp_ref[...], -sp_ref[...]).astype(BF16)
            dv_ref[...] = (cv_ref[...] + dv_full[0:nb]).astype(BF16)
            ck_ref[...] = dk_full[nb:2 * nb]
            cv_ref[...] = dv_full[nb:2 * nb]

        @pl.when(i == N_ATTN_BLOCKS)
        def _():
            dk_ref[...] = _rope(ck_ref[...], cp_ref[...], -sp_ref[...]).astype(BF16)
            dv_ref[...] = cv_ref[...].astype(BF16)

    qi = lambda i: jnp.minimum(i, N_ATTN_BLOCKS - 1)
    cur = lambda w: pl.BlockSpec((nb, w), lambda i: (qi(i), 0))
    prv = lambda w: pl.BlockSpec((nb, w), lambda i: (_prev(qi(i)), 0))
    out_prev = lambda w: pl.BlockSpec((nb, w), lambda i: (_prev(i), 0))
    return _call(
        body, name=name, grid=(n_steps,),
        in_specs=[cur(D_MODEL), cur(KV_W), prv(KV_W), cur(KV_W), prv(KV_W), cur(D_MODEL),
                  cur(128), cur(128), out_prev(128), out_prev(128), pl.BlockSpec(memory_space=pltpu.SMEM)],
        out_specs=[cur(D_MODEL), out_prev(2 * KV_W), pl.BlockSpec((8, 128), lambda i: (0, 0))],
        out_shape=[jax.ShapeDtypeStruct((SEQ, D_MODEL), BF16), jax.ShapeDtypeStruct((SEQ, 2 * KV_W), BF16),
                   jax.ShapeDtypeStruct((8, 128), F32)],
        scratch_shapes=[pltpu.VMEM((nb, KV_W), F32), pltpu.VMEM((nb, KV_W), F32)],
        args=[qr, kr, kr, v, v, dy, cos, sin_signed, cos, sin_signed, sinks], stages=stages)


def _proj_scratch():
    return [pltpu.VMEM((D_MODEL, D_MODEL), BF16)] * 3 + [pltpu.SemaphoreType.DMA((3 * N_CHIPS,))]


def _load_projs(w_refs, wl_ref, wa_ref, wo_ref, sem):
    for k, (w_ref, dst) in enumerate(zip(w_refs, (wl_ref, wa_ref, wo_ref))):
        _load_weight(w_ref, dst, sem.at[pl.ds(k * N_CHIPS, N_CHIPS)])


def merge_fwd(y_lru, y_attn, g_lru, g_attn, projs, g_post, h_in, name, stages=()):
    tm = MM_ROWS

    def body(yl_ref, ya_ref, gl_ref, ga_ref, w1_ref, w2_ref, w3_ref, gp_ref, h_ref,
             pl_ref, pa_ref, mg_ref, m_ref, o_ref, wl_ref, wa_ref, wo_ref, sem):
        @pl.when(pl.program_id(0) == 0)
        def _():
            _load_projs((w1_ref, w2_ref, w3_ref), wl_ref, wa_ref, wo_ref, sem)

        p_l = _dot(yl_ref[...], wl_ref[...])
        p_a = _dot(ya_ref[...], wa_ref[...])
        pl_ref[...] = p_l.astype(BF16)
        pa_ref[...] = p_a.astype(BF16)
        merged = (_sigmoid(gl_ref[...]) * p_l + _sigmoid(ga_ref[...]) * p_a).astype(BF16)
        mg_ref[...] = merged
        m = _dot(merged, wo_ref[...])
        m_ref[...] = m
        o_ref[...] = h_ref[...] + m * _rsqrt_mean_sq(m) * gp_ref[...]

    row = _ROW(tm)
    return _call(
        body, name=name, grid=(SEQ // tm,),
        in_specs=[row, row, row, row, ANY, ANY, ANY, _VEC, row],
        out_specs=[row] * 5,
        out_shape=[jax.ShapeDtypeStruct((SEQ, D_MODEL), BF16)] * 3 + [jax.ShapeDtypeStruct((SEQ, D_MODEL), F32)] * 2,
        scratch_shapes=_proj_scratch(),
        args=[y_lru, y_attn, g_lru, g_attn, *projs, g_post, h_in], stages=stages)


def merge_bwd(d_out, m, g_post, projs, g_lru, g_attn, p_l, p_a, name, stages=()):
    tm = 256

    def body(do_ref, m_ref, gp_ref, w1_ref, w2_ref, w3_ref, gl_ref, ga_ref, pl_ref, pa_ref,
             dm_ref, dpl_ref, dpa_ref, dgl_ref, dga_ref, dya_ref, dyl_ref, dgp_ref, wl_ref, wa_ref, wo_ref, sem):
        @pl.when(pl.program_id(0) == 0)
        def _():
            _load_projs((w1_ref, w2_ref, w3_ref), wl_ref, wa_ref, wo_ref, sem)
            dgp_ref[...] = jnp.zeros_like(dgp_ref)

        mv = m_ref[...]
        rm = _rsqrt_mean_sq(mv)
        mh = mv * rm
        dn = do_ref[...]
        dgp_ref[...] += jnp.sum(dn * mh, axis=0, keepdims=True)
        t = dn * gp_ref[...]
        dm = (rm * (t - mh * jnp.mean(t * mh, axis=-1, keepdims=True))).astype(BF16)
        dm_ref[...] = dm
        dmg = _dot_nt(dm, wo_ref[...])
        sl = _sigmoid(gl_ref[...])
        sa = _sigmoid(ga_ref[...])
        dpl = (dmg * sl).astype(BF16)
        dpa = (dmg * sa).astype(BF16)
        dpl_ref[...] = dpl
        dpa_ref[...] = dpa
        dgl_ref[...] = (dmg * pl_ref[...].astype(F32) * sl * (1.0 - sl)).astype(BF16)
        dga_ref[...] = (dmg * pa_ref[...].astype(F32) * sa * (1.0 - sa)).astype(BF16)
        dyl_ref[...] = _dot_nt(dpl, wl_ref[...])
        dya_ref[...] = _dot_nt(dpa, wa_ref[...]).astype(BF16)

    row = _ROW(tm)
    return _call(
        body, name=name, grid=(SEQ // tm,),
        in_specs=[row, row, _VEC, ANY, ANY, ANY, row, row, row, row],
        out_specs=[row] * 7 + [_VEC],
        out_shape=[jax.ShapeDtypeStruct((SEQ, D_MODEL), BF16)] * 6 + [jax.ShapeDtypeStruct((SEQ, D_MODEL), F32),
                                                                       jax.ShapeDtypeStruct((1, D_MODEL), F32)],
        scratch_shapes=_proj_scratch(),
        args=[d_out, m, g_post, *projs, g_lru, g_attn, p_l, p_a], stages=stages)


def _rope_tables():
    half = HEAD_DIM // 2
    inv_freq = np.float32(ROPE_THETA) ** (-np.arange(half, dtype=np.float32) / np.float32(half))
    ang = np.arange(SEQ, dtype=np.float32)[:, None] * inv_freq[None, :]
    cos, sin = np.cos(ang), np.sin(ang)
    return (jnp.asarray(np.tile(np.concatenate([cos, cos], axis=1), (1, 2))),
            jnp.asarray(np.tile(np.concatenate([-sin, sin], axis=1), (1, 2))))


def _block_diag(w):
    per = LRU_TC // LRU_BLOCK_W
    w4 = w.reshape(LRU_W // LRU_TC, per, LRU_BLOCK_W, LRU_BLOCK_W)
    eye = jnp.eye(per, dtype=w.dtype)
    return jnp.einsum('jacd,ab->jacbd', w4, eye).reshape(LRU_W // LRU_TC, LRU_TC, LRU_TC).astype(BF16)


def _diag_blocks(p):
    per = LRU_TC // LRU_BLOCK_W
    p5 = p.reshape(LRU_W // LRU_TC, per, LRU_BLOCK_W, per, LRU_BLOCK_W)
    return jnp.stack([p5[:, a, :, a, :] for a in range(per)], axis=1).reshape(LRU_W // LRU_BLOCK_W, LRU_BLOCK_W, LRU_BLOCK_W)


def _place():
    x, y, c = lax.axis_index('x'), lax.axis_index('y'), lax.axis_index('c')
    chips = [(1 - x, y), (x, 1 - y), (1 - x, 1 - y)]
    return x, y, c, chips


def _rcopy(src, dst, send_sem, recv_sem, to):
    return pltpu.make_async_remote_copy(src_ref=src, dst_ref=dst, send_sem=send_sem, recv_sem=recv_sem,
                                        device_id=to, device_id_type=MESH)


class _Stage:
    inputs, out_shape, scratch, peers = (), (), (), ()

    def start(self, ins, outs, scr):
        plan = self._plan(ins, outs, scr)
        for ld in plan['loads']:
            ld.start()
        for cp in plan['sends']:
            cp.start()

    def relay(self, ins, outs, scr):
        pass

    def mid(self, ins, outs, scr):
        plan = self._plan(ins, outs, scr)
        for ld, st in zip(plan['loads'], plan['stores']):
            ld.wait()
            st.start()
        for arrived, onward in zip(plan['arrivals'], plan['forwards']):
            arrived.wait_recv()
            onward.start()

    def end(self, ins, outs, scr):
        plan = self._plan(ins, outs, scr)
        for st in plan['stores']:
            st.wait()
        for arrived in (plan['final_arrivals'] if plan['forwards'] else plan['arrivals']):
            arrived.wait_recv()
        for cp in plan['sends'] + plan['forwards']:
            cp.wait_send()


def _empty_plan():
    return dict(loads=[], stores=[], sends=[], arrivals=[], forwards=[], final_arrivals=[])


class GatherStage(_Stage):
    peers = ('chips', 'sib')
    N_CP = 12

    def __init__(self, items):
        self.ranges = [(off, rows) for _, off, rows in items]
        self.inputs = [src for src, _, _ in items]
        self.out_shape = [jax.ShapeDtypeStruct((N_CHIPS, rows, D_MODEL), BF16) for _, rows in self.ranges]
        n = self.N_CP * len(items)
        self.scratch = [pltpu.VMEM((sum(r for _, r in self.ranges), D_MODEL), BF16), pltpu.SemaphoreType.DMA((n,)),
                        pltpu.SemaphoreType.DMA((n,)), pltpu.SemaphoreType.DMA((2 * len(items),))]

    def _plan(self, ins, outs, scr):
        buf, send, recv, lsem = scr
        x, y, c, _ = _place()
        me_q, q_x, q_y, q_d = 2 * x + y, 2 * (1 - x) + y, 2 * x + (1 - y), 2 * (1 - x) + (1 - y)
        to_x, to_y, sib = (1 - x, y, c), (x, 1 - y, c), (x, y, 1 - c)
        plan = dict(loads=[], stores=[], first=[], early=[], relays=[], late=[], hand_early=[], hand_late=[], final=[])
        boff = 0
        for w, ((off, rows), p_ref, o_ref) in enumerate(zip(self.ranges, ins, outs)):
            hr = rows // 2
            ch = hr // 2
            plan['loads'].append(pltpu.make_async_copy(p_ref.at[pl.ds(off, rows)], buf.at[pl.ds(boff, rows)], lsem.at[2 * w]))
            plan['stores'].append(pltpu.make_async_copy(buf.at[pl.ds(boff, rows)], o_ref.at[me_q], lsem.at[2 * w + 1]))
            boff += rows
            base = w * self.N_CP
            mine = [pl.ds(pl.multiple_of(c * hr + k * ch, 16), ch) for k in range(2)]
            theirs = [pl.ds(pl.multiple_of((1 - c) * hr + k * ch, 16), ch) for k in range(2)]
            src = [p_ref.at[pl.ds(pl.multiple_of(off + c * hr + k * ch, 16), ch)] for k in range(2)]

            def cp(k, s, d, to):
                return _rcopy(s, d, send.at[base + k], recv.at[base + k], to)

            def here(q, rows_):
                return o_ref.at[q, rows_]

            plan['first'] += [cp(0, src[0], here(me_q, mine[0]), to_x), cp(2, src[1], here(me_q, mine[1]), to_y),
                              cp(1, src[1], here(me_q, mine[1]), to_x), cp(3, src[0], here(me_q, mine[0]), to_y)]
            x_a, y_b = here(q_x, mine[0]), here(q_y, mine[1])
            plan['early'] += [cp(0, x_a, x_a, to_x), cp(2, y_b, y_b, to_y)]
            plan['relays'] += [cp(4, x_a, x_a, to_y), cp(5, y_b, y_b, to_x)]
            plan['hand_early'] += [cp(6, x_a, x_a, sib), cp(7, y_b, y_b, sib)]
            x_b, y_a, d_a, d_b = here(q_x, mine[1]), here(q_y, mine[0]), here(q_d, mine[0]), here(q_d, mine[1])
            plan['late'] += [cp(1, x_b, x_b, to_x), cp(3, y_a, y_a, to_y), cp(4, d_a, d_a, to_y), cp(5, d_b, d_b, to_x)]
            plan['hand_late'] += [cp(8, x_b, x_b, sib), cp(9, y_a, y_a, sib), cp(10, d_a, d_a, sib), cp(11, d_b, d_b, sib)]
            for k, (q, piece) in enumerate([(q_x, 0), (q_y, 1), (q_x, 1), (q_y, 0), (q_d, 0), (q_d, 1)]):
                got = here(q, theirs[piece])
                plan['final'].append(cp(6 + k, got, got, sib))
        return plan

    def start(self, ins, outs, scr):
        plan = self._plan(ins, outs, scr)
        for ld in plan['loads']:
            ld.start()
        for cp in plan['first']:
            cp.start()

    def relay(self, ins, outs, scr):
        plan = self._plan(ins, outs, scr)
        for arrived in plan['early']:
            arrived.wait_recv()
        for cp in plan['relays'] + plan['hand_early']:
            cp.start()

    def mid(self, ins, outs, scr):
        plan = self._plan(ins, outs, scr)
        for ld, st in zip(plan['loads'], plan['stores']):
            ld.wait()
            st.start()
        for arrived in plan['late']:
            arrived.wait_recv()
        for cp in plan['hand_late']:
            cp.start()

    def end(self, ins, outs, scr):
        plan = self._plan(ins, outs, scr)
        for st in plan['stores']:
            st.wait()
        for arrived in plan['final']:
            arrived.wait_recv()
        for cp in plan['first'] + plan['relays'] + plan['hand_early'] + plan['hand_late']:
            cp.wait_send()


class PairStage(_Stage):
    peers = ('sib',)

    def __init__(self, grads):
        self.inputs = list(grads)
        self.out_shape = [jax.ShapeDtypeStruct((N_CHIPS, 1) + g.shape[2:], BF16) for g in grads]
        n_cp = N_CHIPS * len(grads)
        self.scratch = [pltpu.SemaphoreType.DMA((n_cp,)), pltpu.SemaphoreType.DMA((n_cp,))]

    def _plan(self, ins, outs, scr):
        send, recv = scr
        x, y, c, _ = _place()
        plan = _empty_plan()
        for w, (g_ref, l_ref) in enumerate(zip(ins, outs)):
            for q in range(N_CHIPS):
                i = w * N_CHIPS + q
                plan['sends'].append(_rcopy(g_ref.at[q, pl.ds(1 - c, 1)], l_ref.at[q], send.at[i], recv.at[i], (x, y, 1 - c)))
        plan['arrivals'] = plan['sends']
        return plan


class ChipStage(_Stage):
    peers = ('chips',)

    def __init__(self, items):
        self.ranges = [(off, n) for _, off, n in items]
        self.inputs = [s for s, _, _ in items]
        self.out_shape = [jax.ShapeDtypeStruct((N_CHIPS, n, D_MODEL), BF16) for _, n in self.ranges]
        n_cp = 3 * len(items)
        self.scratch = [pltpu.VMEM((sum(n for _, n in self.ranges), D_MODEL), BF16), pltpu.SemaphoreType.DMA((n_cp,)),
                        pltpu.SemaphoreType.DMA((n_cp,)), pltpu.SemaphoreType.DMA((2 * len(items),))]

    def _plan(self, ins, outs, scr):
        buf, send, recv, lsem = scr
        x, y, c, chips = _place()
        me_q = 2 * x + y
        plan = _empty_plan()
        boff = 0
        for w, ((off, n), s_ref, l_ref) in enumerate(zip(self.ranges, ins, outs)):
            rows = pl.ds(off, n)
            plan['loads'].append(pltpu.make_async_copy(s_ref.at[me_q, rows], buf.at[pl.ds(boff, n)], lsem.at[2 * w]))
            plan['stores'].append(pltpu.make_async_copy(buf.at[pl.ds(boff, n)], l_ref.at[me_q], lsem.at[2 * w + 1]))
            boff += n
            for j, (cx, cy) in enumerate(chips):
                i = w * 3 + j
                got = l_ref.at[2 * cx + cy]
                plan['sends'].append(_rcopy(s_ref.at[2 * cx + cy, rows], l_ref.at[me_q], send.at[i], recv.at[i], (cx, cy, c)))
                plan['arrivals'].append(_rcopy(got, got, send.at[i], recv.at[i], (cx, cy, c)))
        return plan


class SwapStage(_Stage):
    peers = ('sib',)

    def __init__(self, items):
        n = len(items)
        self.inputs = list(items)
        self.out_shape = [jax.ShapeDtypeStruct((2,) + a.shape, a.dtype) for a in items]
        self.scratch = [pltpu.VMEM(a.shape, a.dtype) for a in items] + [
            pltpu.SemaphoreType.DMA((n,)), pltpu.SemaphoreType.DMA((n,)), pltpu.SemaphoreType.DMA((2 * n,))]

    def _plan(self, ins, outs, scr):
        bufs, (send, recv, lsem) = scr[:len(ins)], scr[len(ins):]
        x, y, c, _ = _place()
        plan = _empty_plan()
        for w, (h_ref, o_ref, buf) in enumerate(zip(ins, outs, bufs)):
            plan['loads'].append(pltpu.make_async_copy(h_ref, buf, lsem.at[2 * w]))
            plan['stores'].append(pltpu.make_async_copy(buf, o_ref.at[c], lsem.at[2 * w + 1]))
            got = o_ref.at[1 - c]
            plan['sends'].append(_rcopy(h_ref, o_ref.at[c], send.at[w], recv.at[w], (x, y, 1 - c)))
            plan['arrivals'].append(_rcopy(got, got, send.at[w], recv.at[w], (x, y, 1 - c)))
        return plan


class SmallGatherStage(_Stage):
    peers = ('chips', 'sib')

    def __init__(self, blk):
        self.inputs = [blk]
        self.out_shape = [jax.ShapeDtypeStruct((N_DEV,) + blk.shape, blk.dtype)]
        self.scratch = [pltpu.VMEM(blk.shape, blk.dtype), pltpu.SemaphoreType.DMA((7,)), pltpu.SemaphoreType.DMA((7,)),
                        pltpu.SemaphoreType.DMA((2,))]

    def _plan(self, ins, outs, scr):
        (x_ref,), (o_ref,), (buf, send, recv, lsem) = ins, outs, scr
        x, y, c, chips = _place()
        sib = (x, y, 1 - c)

        def slot(px, py, pc):
            return o_ref.at[4 * px + 2 * py + pc]

        plan = _empty_plan()
        plan['loads'].append(pltpu.make_async_copy(x_ref, buf, lsem.at[0]))
        plan['stores'].append(pltpu.make_async_copy(buf, slot(x, y, c), lsem.at[1]))
        from_sib = slot(x, y, 1 - c)
        plan['sends'].append(_rcopy(x_ref, slot(x, y, c), send.at[0], recv.at[0], sib))
        plan['final_arrivals'].append(_rcopy(from_sib, from_sib, send.at[0], recv.at[0], sib))
        for j, (cx, cy) in enumerate(chips):
            got, got_sib = slot(cx, cy, c), slot(cx, cy, 1 - c)
            plan['sends'].append(_rcopy(x_ref, slot(x, y, c), send.at[1 + j], recv.at[1 + j], (cx, cy, c)))
            plan['arrivals'].append(_rcopy(got, got, send.at[1 + j], recv.at[1 + j], (cx, cy, c)))
            plan['forwards'].append(_rcopy(got, got, send.at[4 + j], recv.at[4 + j], sib))
            plan['final_arrivals'].append(_rcopy(got_sib, got_sib, send.at[4 + j], recv.at[4 + j], sib))
        return plan


_HBM = pl.BlockSpec(memory_space=pltpu.HBM)
_SEM = pl.BlockSpec(memory_space=pltpu.SEMAPHORE)
_DATAFLOW = pltpu.CompilerParams(has_side_effects=pltpu.SideEffectType.DATAFLOW_SIDE_EFFECTING)


def chip_exchange_start(s):
    def body(s_ref, land_ref, send, recv, s_thru, land_thru, token):
        x, y, c, chips = _place()
        for j, (cx, cy) in enumerate(chips):
            _rcopy(s_ref.at[2 * cx + cy], land_ref.at[2 * x + y], send.at[j], recv.at[j], (cx, cy, c)).start()
        token[...] = jnp.zeros_like(token)

    return pl.pallas_call(
        body, name='chip_exchange_start',
        out_shape=(pltpu.SemaphoreType.DMA((3,)), pltpu.SemaphoreType.DMA((3,)), pltpu.HBM(s.shape, s.dtype),
                   pltpu.HBM(s.shape, s.dtype), jax.ShapeDtypeStruct((8, 128), F32)),
        in_specs=(_HBM, _HBM), out_specs=(_SEM, _SEM, _HBM, _HBM, pl.BlockSpec(memory_space=pltpu.VMEM)),
        input_output_aliases={0: 2, 1: 3}, compiler_params=_DATAFLOW,
    )(pltpu.with_memory_space_constraint(s, pltpu.HBM),
      pltpu.with_memory_space_constraint(lax.empty(s.shape, s.dtype), pltpu.HBM))


def chip_exchange_wait(send, recv, s_thru, land_thru, after):
    def body(s_ref, land_ref, send_sem, recv_sem, after_ref, s_out, land_out):
        x, y, c, chips = _place()
        for j, (cx, cy) in enumerate(chips):
            cp = _rcopy(s_ref.at[2 * cx + cy], land_ref.at[2 * cx + cy], send_sem.at[j], recv_sem.at[j], (cx, cy, c))
            cp.wait_send()
            cp.wait_recv()

    return pl.pallas_call(
        body, name='chip_exchange_wait',
        out_shape=(pltpu.HBM(s_thru.shape, s_thru.dtype), pltpu.HBM(land_thru.shape, land_thru.dtype)),
        in_specs=(_HBM, _HBM, _SEM, _SEM, ANY), out_specs=(_HBM, _HBM),
        input_output_aliases={0: 0, 1: 1}, compiler_params=_DATAFLOW,
    )(s_thru, land_thru, send, recv, after)


def comm_call(name, stages):
    def body():
        pass

    return _call(body, name=name, grid=(1,), in_specs=[], out_specs=[], out_shape=[], args=[], stages=stages)[1]


def pair_sum(g4, land, c_arr, name):
    hr = g4.shape[2]

    def body(c_ref, g_ref, l_ref, o_ref):
        o_ref[0] = (g_ref[0, 0].astype(F32) + l_ref[0, 0].astype(F32)).astype(BF16)

    return pl.pallas_call(
        body, name=name,
        grid_spec=pltpu.PrefetchScalarGridSpec(
            num_scalar_prefetch=1, grid=(N_CHIPS,),
            in_specs=[pl.BlockSpec((1, 1, hr, D_MODEL), lambda q, c: (q, c[0], 0, 0)),
                      pl.BlockSpec((1, 1, hr, D_MODEL), lambda q, c: (q, 0, 0, 0))],
            out_specs=pl.BlockSpec((1, hr, D_MODEL), lambda q, c: (q, 0, 0))),
        out_shape=jax.ShapeDtypeStruct((N_CHIPS, hr, D_MODEL), BF16),
        compiler_params=_params(1),
    )(c_arr, g4, land)


def small_sum(vec_parts, lru_parts):
    def body(v_ref, l_ref, o_ref):
        for p_ref, lo, n in ((v_ref, 0, ROW_WA), (l_ref, ROW_WA, SMALL_ROWS - ROW_WA)):
            acc = p_ref[0]
            for s in range(1, N_DEV):
                acc = acc + p_ref[s]
            o_ref[lo:lo + n, :] = acc

    return pl.pallas_call(
        body, name='small_sum', grid=(1,),
        in_specs=[pl.BlockSpec(vec_parts.shape, lambda i: (0, 0, 0)), pl.BlockSpec(lru_parts.shape, lambda i: (0, 0, 0))],
        out_specs=pl.BlockSpec((SMALL_ROWS, D_MODEL), lambda i: (0, 0)),
        out_shape=jax.ShapeDtypeStruct((SMALL_ROWS, D_MODEL), F32),
        compiler_params=_params(1),
    )(vec_parts, lru_parts)


def _adam_math(w, g, m, v):
    m2 = ADAM_B1 * m + (1.0 - ADAM_B1) * g
    v2 = ADAM_B2 * v + (1.0 - ADAM_B2) * (g * g)
    m_hat = m2 / (1.0 - ADAM_B1 ** ADAM_STEP)
    v_hat = v2 / (1.0 - ADAM_B2 ** ADAM_STEP)
    delta = -ADAM_LR * (m_hat / (jnp.sqrt(v_hat) + ADAM_EPS) + ADAM_WD * w)
    return delta, m2, v2


def _adam_body(n_parts, transposed, n_after):
    def body(*refs):
        refs = refs[n_after:]
        g_refs = refs[:n_parts]
        w_ref, m_ref, v_ref, go_ref, d_ref, mo_ref, vo_ref = refs[n_parts:]
        def chips_added(blk):
            acc = blk[0].astype(F32)
            for s in range(1, N_CHIPS):
                acc = acc + blk[s].astype(F32)
            return acc

        if transposed:
            g = jnp.concatenate([chips_added(g_ref[h]) for h in range(2) for g_ref in g_refs], axis=0).T
        else:
            rows = [chips_added(g_ref[0]) for g_ref in g_refs]
            g = jnp.concatenate(rows, axis=0) if n_parts > 1 else rows[0]
        go_ref[...] = g
        d_ref[...], mo_ref[...], vo_ref[...] = _adam_math(w_ref[...], g, m_ref[...], v_ref[...])
    return body


def adam_rows(fulls, name, w, m, v, after=()):
    hr = w.shape[0] // 2
    blk = pl.BlockSpec((hr, D_MODEL), lambda h: (h, 0))
    return pl.pallas_call(
        _adam_body(len(fulls), False, len(after)), name='adam_' + name, grid=(2,),
        in_specs=[ANY] * len(after)
        + [pl.BlockSpec((1, N_CHIPS, f.shape[2], D_MODEL), lambda h: (h, 0, 0, 0)) for f in fulls] + [blk, blk, blk],
        out_specs=[blk] * 4,
        out_shape=[jax.ShapeDtypeStruct(w.shape, F32)] * 4,
        compiler_params=_params(1),
    )(*after, *fulls, w, m, v)


def adam_cols(fulls, name, w, m, v, after=()):
    cols = w.shape[1]
    tr = 128
    blk = pl.BlockSpec((tr, cols), lambda i: (i, 0))
    return pl.pallas_call(
        _adam_body(len(fulls), True, len(after)), name='adam_' + name, grid=(D_MODEL // tr,),
        in_specs=[ANY] * len(after)
        + [pl.BlockSpec((2, N_CHIPS, f.shape[2], tr), lambda i: (0, 0, 0, i)) for f in fulls] + [blk, blk, blk],
        out_specs=[blk] * 4,
        out_shape=[jax.ShapeDtypeStruct(w.shape, F32)] * 4,
        compiler_params=_params(1),
    )(*after, *fulls, w, m, v)


def adam_small(g, w, m, v):
    def body(g_ref, w_ref, m_ref, v_ref, d_ref, mo_ref, vo_ref):
        d_ref[...], mo_ref[...], vo_ref[...] = _adam_math(w_ref[...], g_ref[...], m_ref[...], v_ref[...])

    blk = pl.BlockSpec(w.shape, lambda i: (0, 0))
    return pl.pallas_call(
        body, name='adam_small', grid=(1,), in_specs=[blk] * 4, out_specs=[blk] * 3,
        out_shape=[jax.ShapeDtypeStruct(w.shape, F32)] * 3, compiler_params=_params(1),
    )(g, w, m, v)


WEIGHTS = ('ffn1_pre_g', 'ffn1_w_gu', 'ffn1_w_down', 'ffn1_post_g', 'mix_pre_g', 'w_in', 'conv_w', 'conv_b',
           'lru_w_a', 'lru_b_a', 'lru_w_x', 'lru_b_x', 'lru_lambda', 'attn_sinks', 'w_proj_lru', 'w_proj_attn',
           'w_out', 'mix_post_g', 'ffn2_pre_g', 'ffn2_w_gu', 'ffn2_w_down', 'ffn2_post_g')
SMALL = tuple(n for n in WEIGHTS if n not in PACK_OFF)


def _pack_vecs(d, conv_rows):
    sinks = jnp.pad(d['attn_sinks'].reshape(1, N_Q_HEADS), ((0, 0), (0, D_MODEL - N_Q_HEADS)))
    conv = jnp.pad(conv_rows, ((0, ROW_WA - ROW_CONV - conv_rows.shape[0]), (0, 0)))
    return jnp.concatenate([d[n].reshape(1, D_MODEL) for n in SMALL_VECS] + [sinks, conv], axis=0)


def _pack_lru(d):
    return jnp.concatenate([d['lru_w_a'].reshape(64, D_MODEL), d['lru_w_x'].reshape(64, D_MODEL)], axis=0)


def _pack_small(d, conv_rows):
    return jnp.concatenate([_pack_vecs(d, conv_rows), _pack_lru(d)], axis=0)


def _unpack_small(p, shapes):
    out = {n: p[k:k + 1].reshape(shapes[n]) for k, n in enumerate(SMALL_VECS)}
    out['attn_sinks'] = p[ROW_SINKS:ROW_SINKS + 1, :N_Q_HEADS].reshape(shapes['attn_sinks'])
    out['conv_w'] = p[ROW_CONV:ROW_CONV + 1].reshape(shapes['conv_w'])
    out['lru_w_a'] = p[ROW_WA:ROW_WA + 64].reshape(shapes['lru_w_a'])
    out['lru_w_x'] = p[ROW_WX:ROW_WX + 64].reshape(shapes['lru_w_x'])
    return out


def kernel(x, ffn1_pre_g, ffn1_w_gu, ffn1_w_down, ffn1_post_g, mix_pre_g, w_in, conv_w, conv_b, lru_w_a, lru_b_a, lru_w_x, lru_b_x, lru_lambda, attn_sinks, w_proj_lru, w_proj_attn, w_out, mix_post_g, ffn2_pre_g, ffn2_w_gu, ffn2_w_down, ffn2_post_g, loss_target, m_ffn1_pre_g, m_ffn1_w_gu, m_ffn1_w_down, m_ffn1_post_g, m_mix_pre_g, m_w_in, m_conv_w, m_conv_b, m_lru_w_a, m_lru_b_a, m_lru_w_x, m_lru_b_x, m_lru_lambda, m_attn_sinks, m_w_proj_lru, m_w_proj_attn, m_w_out, m_mix_post_g, m_ffn2_pre_g, m_ffn2_w_gu, m_ffn2_w_down, m_ffn2_post_g, v_ffn1_pre_g, v_ffn1_w_gu, v_ffn1_w_down, v_ffn1_post_g, v_mix_pre_g, v_w_in, v_conv_w, v_conv_b, v_lru_w_a, v_lru_b_a, v_lru_w_x, v_lru_b_x, v_lru_lambda, v_attn_sinks, v_w_proj_lru, v_w_proj_attn, v_w_out, v_mix_post_g, v_ffn2_pre_g, v_ffn2_w_gu, v_ffn2_w_down, v_ffn2_post_g):
    given = dict(locals())
    w = {n: given[n] for n in WEIGHTS}
    mom = {n: given['m_' + n] for n in WEIGHTS}
    var = {n: given['v_' + n] for n in WEIGHTS}
    shapes = {n: w[n].shape for n in WEIGHTS}
    xq = lax.axis_index('x')
    yq = lax.axis_index('y')
    cq = lax.axis_index('c')
    me_q = 2 * xq + yq

    c_arr = cq.reshape(1).astype(jnp.int32)
    xs, target = x[0], loss_target[0]
    sw = {n: (w[n][0] if w[n].ndim > 2 else w[n]) for n in SMALL}
    cos, sin_signed = _rope_tables()
    wa_bd = _block_diag(sw['lru_w_a'])
    wx_bd = _block_diag(sw['lru_w_x'])
    sinks = sw['attn_sinks'].reshape(N_Q_HEADS)

    shard = {n: (w[n][0].T if t else w[n][0]).astype(BF16) for n, _, t in PACK}
    conv_pad = jnp.pad(w['conv_w'][0], ((0, 4), (0, 0)))

    def whole(name):
        return (shard[name], 0, PACK_ROWS_OF[name])

    def part(name, p, n_parts=2):
        rows = PACK_ROWS_OF[name] // n_parts
        return (shard[name], p * rows, rows)

    (w_gu1,), (conv_all,) = comm_call('gather_first', [GatherStage([whole('ffn1_w_gu')]), SmallGatherStage(conv_pad)])
    sw['conv_w'] = jnp.transpose(conv_all[0::2, :4, :], (1, 0, 2)).reshape(4, LRU_W)
    proj_names = ['w_proj_lru', 'w_proj_attn', 'w_out']

    (n1, g1, u1, a1), ((w_down1,),) = ffn_fwd_a(xs, sw['ffn1_pre_g'], [w_gu1], 'ffn1_fwd_a',
                                                 stages=[GatherStage([whole('ffn1_w_down')])])
    (f1, h1), ((w_in_t,),) = ffn_fwd_b(a1, w_down1, sw['ffn1_post_g'], xs, 'ffn1_fwd_b', stages=[GatherStage([whole('w_in')])])
    (um, gate, xbr, q, k, v, g_lru, g_attn), ((w_gu2a,),) = mix_in(h1, sw['mix_pre_g'], w_in_t, 'mix_in',
                                                                   stages=[GatherStage([part('ffn2_w_gu', 0)])])
    (y_lru, h_lru), ((w_gu2b,),) = lru_fwd(gate, xbr, sw['conv_w'], sw['conv_b'], wa_bd, sw['lru_b_a'], wx_bd, sw['lru_b_x'],
                                           sw['lru_lambda'], 'lru_fwd', stages=[GatherStage([part('ffn2_w_gu', 1)])])
    (qr, kr, y_attn), (projs,) = attn_fwd(q, k, v, cos, sin_signed, sinks, 'attn_fwd',
                                          stages=[GatherStage([whole(n) for n in proj_names])])
    (p_l, p_a, merged, m, h2), ((w_down2,),) = merge_fwd(y_lru, y_attn, g_lru, g_attn, projs, sw['mix_post_g'], h1, 'merge_fwd',
                                                         stages=[GatherStage([whole('ffn2_w_down')])])
    w_gu2 = [w_gu2a, w_gu2b]
    (n2, g2, u2, a2), _ = ffn_fwd_a(h2, sw['ffn2_pre_g'], w_gu2, 'ffn2_fwd_a')
    (f2, dy, loss_blk), _ = ffn_fwd_b(a2, w_down2, sw['ffn2_post_g'], h2, 'ffn2_fwd_b', target=target)

    gs, full = {}, {}

    def pair_stage(names, grads):
        g4 = [g.reshape(N_CHIPS, 2, PACK_ROWS_OF[n] // 2, D_MODEL) for n, g in zip(names, grads)]
        return PairStage(g4), g4

    def pair_sums(names, g4, lands):
        return [pair_sum(g, l, c_arr, 'pair_sum_' + n) for n, g, l in zip(names, g4, lands)]

    def halves(s, n_parts=2):
        n = s.shape[1] // n_parts
        return [(s, p * n, n) for p in range(n_parts)]

    (df2, dgu2, gs['ffn2_post_g']), _ = ffn_bwd_a(dy, f2, sw['ffn2_post_g'], w_down2, g2, u2, 'ffn2_bwd_a')
    g_down2, _ = mm_tn([a2], df2, 1408, 'ffn2_dw_down')
    st, g4 = pair_stage(['ffn2_w_down'], [g_down2])
    g_gu2, (lands,) = mm_tn([dgu2], n2, 1408, 'ffn2_dw_gu', stages=[st])
    (s_down2,) = pair_sums(['ffn2_w_down'], g4, lands)
    st, g4 = pair_stage(['ffn2_w_gu'], [g_gu2])
    (dh2, gs['ffn2_pre_g']), ((l_down2,), lands) = norm_bwd([dgu2], w_gu2, h2, sw['ffn2_pre_g'], dy, 'ffn2_bwd_b',
                                                            stages=[ChipStage([(s_down2, 0, s_down2.shape[1])]), st])
    (s_gu2,) = pair_sums(['ffn2_w_gu'], g4, lands)

    (dm, dpl, dpa, dgl, dga, dya, dyl, gs['mix_post_g']), ((l_gu2a,),) = merge_bwd(
        dh2, m, sw['mix_post_g'], projs, g_lru, g_attn, p_l, p_a, 'merge_bwd', stages=[ChipStage(halves(s_gu2)[:1])])
    g_projs = [mm_tn([merged if n == 'w_out' else (y_lru if n == 'w_proj_lru' else y_attn)],
                     dm if n == 'w_out' else (dpl if n == 'w_proj_lru' else dpa), D_MODEL, 'd' + n)[0] for n in proj_names]
    st, g4 = pair_stage(proj_names, g_projs)
    (dq, dkv, dsk), ((l_gu2b,), lands, (full['ffn2_w_down'],)) = attn_bwd(
        qr, kr, v, dya, cos, sin_signed, sinks, 'attn_bwd', stages=[ChipStage(halves(s_gu2)[1:]), st, SwapStage([l_down2])])
    full['ffn2_w_down'] = [full['ffn2_w_down']]
    gs['attn_sinks'] = dsk[0:1, 0:N_Q_HEADS]
    s_projs = pair_sums(proj_names, g4, lands)
    (dgate, dxbr, vecs, dwa, dwx), (l_projs, full['ffn2_w_gu']) = lru_bwd(
        gate, xbr, h_lru, dyl, sw['conv_w'], sw['conv_b'], wa_bd, sw['lru_b_a'], wx_bd, sw['lru_b_x'], sw['lru_lambda'],
        'lru_bwd', stages=[ChipStage([(s, 0, s.shape[1]) for s in s_projs]), SwapStage([l_gu2a, l_gu2b])])
    gs['conv_w'] = vecs[0:4]
    gs['conv_b'], gs['lru_b_a'], gs['lru_b_x'], gs['lru_lambda'] = vecs[4:5], vecs[5:6], vecs[6:7], vecs[7:8]
    gs['lru_w_a'] = _diag_blocks(dwa)
    gs['lru_w_x'] = _diag_blocks(dwx)
    dz = [dgate, dxbr, dq, dkv, dgl, dga]
    g_in, ((lru_all,),) = mm_tn(dz, um, 512, 'dw_in', stages=[SmallGatherStage(_pack_lru(gs))])
    st, g4 = pair_stage(['w_in'], [g_in])
    (dh1, gs['mix_pre_g']), (lands, f_projs) = norm_bwd(dz, [w_in_t], h1, sw['mix_pre_g'], dh2, 'mix_bwd_in',
                                                        stages=[st, SwapStage(l_projs)])
    for n, f in zip(proj_names, f_projs):
        full[n] = [f]
    (s_in,) = pair_sums(['w_in'], g4, lands)

    (df1, dgu1, gs['ffn1_post_g']), ((l_in_a,),) = ffn_bwd_a(dh1, f1, sw['ffn1_post_g'], w_down1, g1, u1, 'ffn1_bwd_a',
                                                             stages=[ChipStage(halves(s_in)[:1])])
    g_down1, _ = mm_tn([a1], df1, 1408, 'ffn1_dw_down')
    st, g4 = pair_stage(['ffn1_w_down'], [g_down1])
    g_gu1, ((l_in_b,), lands) = mm_tn([dgu1], n1, 1408, 'ffn1_dw_gu', stages=[ChipStage(halves(s_in)[1:]), st])
    (s_down1,) = pair_sums(['ffn1_w_down'], g4, lands)
    st, g4 = pair_stage(['ffn1_w_gu'], [g_gu1])
    (dx, gs['ffn1_pre_g']), ((l_down1,), lands, full['w_in']) = norm_bwd(
        [dgu1], [w_gu1], xs, sw['ffn1_pre_g'], dh1, 'ffn1_bwd_b',
        stages=[ChipStage([(s_down1, 0, s_down1.shape[1])]), st, SwapStage([l_in_a, l_in_b])])
    (s_gu1,) = pair_sums(['ffn1_w_gu'], g4, lands)
    loss_row = jnp.pad(loss_blk[0:1], ((0, 0), (0, D_MODEL - loss_blk.shape[1])))
    vec_blk = _pack_vecs(gs, jnp.concatenate([gs['conv_w'], loss_row], axis=0))
    send, recv, s_thru, land_thru, token = chip_exchange_start(s_gu1)
    out_g, out_d, out_m, out_v = {}, {}, {}, {}

    def adam(n, after=()):
        fn = adam_cols if dict((k, t) for k, _, t in PACK)[n] else adam_rows
        g_, d_, m_, v_ = fn(full[n], n, w[n][0], mom[n][0], var[n][0], after=after)
        out_g[n], out_d[n], out_m[n], out_v[n] = g_[None], d_[None], m_[None], v_[None]

    behind = token
    for n in ['ffn2_w_gu', 'w_in', 'ffn2_w_down'] + proj_names:
        adam(n, after=(behind,))
        behind = out_v[n]
    s_back, l_gu1 = chip_exchange_wait(send, recv, s_thru, land_thru, after=behind)
    own = lax.dynamic_slice_in_dim(s_back, me_q, 1, axis=0)
    l_gu1 = lax.dynamic_update_slice_in_dim(l_gu1, own, me_q, axis=0)
    (vec_all,), (f_down1, f_gu1) = comm_call('swap_last', [SmallGatherStage(vec_blk), SwapStage([l_down1, l_gu1])])
    full['ffn1_w_down'] = [f_down1]
    full['ffn1_w_gu'] = [f_gu1]
    adam('ffn1_w_gu')
    adam('ffn1_w_down')

    tot = small_sum(vec_all, lru_all)
    loss = tot[ROW_WA - 1, 0]
    conv_g = lax.dynamic_slice(tot[ROW_CONV:ROW_CONV + 4], (0, me_q * (LRU_W // N_CHIPS)), (4, LRU_W // N_CHIPS))
    small_g = _unpack_small(tot, shapes)
    small_g['conv_w'] = conv_g.reshape(shapes['conv_w'])
    g_pack = jnp.concatenate([tot[:ROW_CONV], conv_g.reshape(1, D_MODEL), jnp.zeros((ROW_WA - ROW_CONV - 1, D_MODEL), F32),
                              tot[ROW_WA:]], axis=0)
    packs = [_pack_small({n: d[n] for n in SMALL}, d['conv_w'].reshape(1, D_MODEL)) for d in (w, mom, var)]
    d_p, m_p, v_p = adam_small(g_pack, *packs)
    for n in SMALL:
        out_g[n] = small_g[n]
    for dst, p in ((out_d, d_p), (out_m, m_p), (out_v, v_p)):
        dst.update(_unpack_small(p, shapes))

    return (loss, dx[None], *[out_g[n] for n in WEIGHTS], *[out_d[n] for n in WEIGHTS],
            *[out_m[n] for n in WEIGHTS], *[out_v[n] for n in WEIGHTS])
```

```python
import jax
import jax.numpy as jnp
import numpy as np
from jax import lax
from jax.experimental import pallas as pl
from jax.experimental.pallas import tpu as pltpu

F32 = jnp.float32
BF16 = jnp.bfloat16

SEQ = 2048
D_MODEL = 1024
D_FF = 2816
LRU_W = 1024
LRU_BLOCK_W = 64
HEAD_DIM = 64
N_Q_HEADS = 16
N_KV_HEADS = 4
KV_W = N_KV_HEADS * HEAD_DIM
ATTN_BLOCK = 128
N_ATTN_BLOCKS = SEQ // ATTN_BLOCK
IN_SEGS = (1024, 1024, 1024, 256, 256, 1024, 1024)
IN_W = sum(IN_SEGS)
NORM_EPS = 1e-6
MASK_VALUE = -1e30
ROPE_THETA = 10000.0
LRU_C = 8.0
MACARON = 0.5
ADAM_LR = 0.001
ADAM_B1 = 0.9
ADAM_B2 = 0.999
ADAM_EPS = 1e-08
ADAM_WD = 0.01
ADAM_STEP = 10

N_CHIPS = 4
N_DEV = 8
VMEM_LIMIT = 56 * 1024 * 1024
MM_ROWS = 256
MESH = pl.DeviceIdType.MESH
ANY = pl.BlockSpec(memory_space=pl.ANY)

PACK = (('ffn1_w_gu', 1408, True), ('w_in', 1408, True), ('ffn2_w_gu', 1408, True),
        ('ffn1_w_down', 704, False), ('ffn2_w_down', 704, False),
        ('w_proj_lru', 256, False), ('w_proj_attn', 256, False), ('w_out', 256, False))
PACK_ROWS_OF = {n: r for n, r, _ in PACK}
PACK_OFF = {}
_o = 0
for _n, _r, _t in PACK:
    PACK_OFF[_n] = _o
    _o += _r

SMALL_VECS = ('ffn1_pre_g', 'ffn1_post_g', 'mix_pre_g', 'conv_b', 'lru_b_a', 'lru_b_x', 'lru_lambda',
              'mix_post_g', 'ffn2_pre_g', 'ffn2_post_g')
SMALL_ROWS = 144
ROW_SINKS, ROW_CONV, ROW_WA, ROW_WX = 10, 11, 16, 80


def _dot(a, b):
    return jnp.dot(a, b, preferred_element_type=F32)


def _dot_nt(a, b):
    return lax.dot_general(a, b, (((1,), (1,)), ((), ())), preferred_element_type=F32)


def _dot_tn(a, b):
    return lax.dot_general(a, b, (((0,), (0,)), ((), ())), preferred_element_type=F32)


def _params(n_grid):
    return pltpu.CompilerParams(dimension_semantics=("arbitrary",) * n_grid, vmem_limit_bytes=VMEM_LIMIT)


def _sigmoid(x):
    return 1.0 / (1.0 + jnp.exp(-x))


def _rsqrt_mean_sq(x):
    return lax.rsqrt(jnp.mean(x * x, axis=-1, keepdims=True) + NORM_EPS)


def _expm1(x):
    poly = x * (1.0 + x * (0.5 + x * (1.0 / 6.0)))
    return jnp.where(jnp.abs(x) < 0.02, poly, jnp.exp(x) - 1.0)


_GELU_K = 0.7978845608028654
_GELU_C = 0.044715


def _gelu(x):
    t = jnp.tanh(_GELU_K * (x + _GELU_C * x * x * x))
    return 0.5 * x * (1.0 + t), t


def _gelu_grad(x, t):
    return 0.5 * (1.0 + t) + 0.5 * x * (1.0 - t * t) * _GELU_K * (1.0 + 3.0 * _GELU_C * x * x)


def _load_weight(w_refs, dst_ref, sem):
    w_refs = list(w_refs) if isinstance(w_refs, (list, tuple)) else [w_refs]
    rows = dst_ref.shape[0] // N_CHIPS
    rp = rows // len(w_refs)
    cps = [pltpu.make_async_copy(w_ref.at[q], dst_ref.at[pl.ds(q * rows + p * rp, rp)], sem.at[p * N_CHIPS + q])
           for p, w_ref in enumerate(w_refs) for q in range(N_CHIPS)]
    for cp in cps:
        cp.start()
    for cp in cps:
        cp.wait()


def _weight_scratch(rows_total, parts=1):
    return [pltpu.VMEM((rows_total, D_MODEL), BF16), pltpu.SemaphoreType.DMA((N_CHIPS * parts,))]


_ROW = lambda tm: pl.BlockSpec((tm, D_MODEL), lambda i: (i, 0))
_VEC = pl.BlockSpec((1, D_MODEL), lambda i: (0, 0))


def _call(body, *, name, grid, in_specs, out_specs, out_shape, args, scratch_shapes=(), stages=()):
    in_specs, out_specs, out_shape, scratch_shapes = list(in_specs), list(out_specs), list(out_shape), list(scratch_shapes)
    n_in, n_out, n_sc = len(in_specs), len(out_specs), len(scratch_shapes)
    k_in = [len(s.inputs) for s in stages]
    k_out = [len(s.out_shape) for s in stages]
    k_sc = [len(s.scratch) for s in stages]
    last = grid[0] - 1

    def split(refs, counts):
        parts, pos = [], 0
        for k in counts:
            parts.append(refs[pos:pos + k])
            pos += k
        return parts

    kinds = tuple(sorted({k for s in stages for k in s.peers}))
    collective_id = {(): None, ('sib',): 0, ('chips',): 1, ('chips', 'sib'): 2}[kinds]

    def full(*refs):
        ins, s_ins, outs, s_outs, scr, s_scr = split(refs, [n_in, sum(k_in), n_out, sum(k_out), n_sc, sum(k_sc)])
        per_stage = list(zip(stages, split(s_ins, k_in), split(s_outs, k_out), split(s_scr, k_sc)))
        i = pl.program_id(0)
        if stages:
            @pl.when(i == 0)
            def _():
                x, y, c, chips = _place()
                peers = ([(x, y, 1 - c)] if 'sib' in kinds else []) + ([(cx, cy, c) for cx, cy in chips] if 'chips' in kinds else [])
                barrier = pltpu.get_barrier_semaphore()
                for peer in peers:
                    pl.semaphore_signal(barrier, inc=1, device_id=peer, device_id_type=MESH)
                pl.semaphore_wait(barrier, len(peers))
                for s, a, b, c_ in per_stage:
                    s.start(a, b, c_)

        body(*ins, *outs, *scr)
        if stages:
            @pl.when(i == last // 2)
            def _():
                for s, a, b, c in per_stage:
                    s.relay(a, b, c)

            @pl.when(i == max(last - 1, 0))
            def _():
                for s, a, b, c in per_stage:
                    s.mid(a, b, c)

            @pl.when(i == last)
            def _():
                for s, a, b, c in per_stage:
                    s.end(a, b, c)

    res = pl.pallas_call(
        full, name=name, grid=grid,
        in_specs=in_specs + [ANY] * sum(k_in),
        out_specs=out_specs + [ANY] * sum(k_out),
        out_shape=out_shape + [o for s in stages for o in s.out_shape],
        scratch_shapes=scratch_shapes + [x for s in stages for x in s.scratch],
        compiler_params=pltpu.CompilerParams(dimension_semantics=("arbitrary",), vmem_limit_bytes=VMEM_LIMIT,
                                             collective_id=collective_id),
    )(*args, *[a for s in stages for a in s.inputs])
    return list(res[:n_out]), split(list(res[n_out:]), k_out)


def ffn_fwd_a(x, g_pre, w_gu_t, name, stages=()):
    tm, tn = MM_ROWS, 256
    n_w = len(w_gu_t)

    def body(x_ref, gp_ref, *refs):
        w_refs = refs[:n_w]
        n_ref, g_ref, u_ref, a_ref, wt_ref, sem = refs[n_w:]

        @pl.when(pl.program_id(0) == 0)
        def _():
            _load_weight(w_refs, wt_ref, sem)

        xv = x_ref[...]
        n = (xv * _rsqrt_mean_sq(xv) * gp_ref[...]).astype(BF16)
        n_ref[...] = n
        for j in range(D_FF // tn):
            g = _dot_nt(n, wt_ref[j * tn:(j + 1) * tn, :])
            u = _dot_nt(n, wt_ref[D_FF + j * tn:D_FF + (j + 1) * tn, :])
            g_ref[:, j * tn:(j + 1) * tn] = g.astype(BF16)
            u_ref[:, j * tn:(j + 1) * tn] = u.astype(BF16)
            a_ref[:, j * tn:(j + 1) * tn] = (g * _sigmoid(g) * u).astype(BF16)

    wide = pl.BlockSpec((tm, D_FF), lambda i: (i, 0))
    return _call(
        body, name=name, grid=(SEQ // tm,),
        in_specs=[_ROW(tm), _VEC] + [ANY] * n_w,
        out_specs=[_ROW(tm), wide, wide, wide],
        out_shape=[jax.ShapeDtypeStruct((SEQ, D_MODEL), BF16)] + [jax.ShapeDtypeStruct((SEQ, D_FF), BF16)] * 3,
        scratch_shapes=_weight_scratch(2 * D_FF, n_w),
        args=[x, g_pre, *w_gu_t], stages=stages)


def ffn_fwd_b(a, w_down, g_post, h_in, name, target=None, stages=()):
    tm = MM_ROWS
    final = target is not None

    def body(*refs):
        if final:
            a_ref, wf_ref, gp_ref, h_ref, t_ref, f_ref, o_ref, loss_ref, wd_ref, sem = refs
        else:
            a_ref, wf_ref, gp_ref, h_ref, f_ref, o_ref, wd_ref, sem = refs

        @pl.when(pl.program_id(0) == 0)
        def _():
            _load_weight(wf_ref, wd_ref, sem)
            if final:
                loss_ref[...] = jnp.zeros_like(loss_ref)

        f = _dot(a_ref[...], wd_ref[...])
        f_ref[...] = f
        y = h_ref[...] + MACARON * (f * _rsqrt_mean_sq(f) * gp_ref[...])
        if final:
            err = y - t_ref[...]
            o_ref[...] = err * (1.0 / D_MODEL)
            loss_ref[...] += 0.5 * jnp.sum(err * err) * (1.0 / D_MODEL)
        else:
            o_ref[...] = y

    row = _ROW(tm)
    in_specs = [pl.BlockSpec((tm, D_FF), lambda i: (i, 0)), ANY, _VEC, row]
    out_specs = [row, row]
    out_shape = [jax.ShapeDtypeStruct((SEQ, D_MODEL), F32)] * 2
    args = [a, w_down, g_post, h_in]
    if final:
        in_specs.append(row)
        args.append(target)
        out_specs.append(pl.BlockSpec((8, 128), lambda i: (0, 0)))
        out_shape.append(jax.ShapeDtypeStruct((8, 128), F32))
    return _call(body, name=name, grid=(SEQ // tm,), in_specs=in_specs, out_specs=out_specs,
                 out_shape=out_shape, scratch_shapes=_weight_scratch(D_FF), args=args, stages=stages)


def ffn_bwd_a(d_out, f, g_post, w_down, g, u, name, stages=()):
    tm = MM_ROWS
    tc = 256

    def body(do_ref, f_ref, gp_ref, wf_ref, g_ref, u_ref, df_ref, dgu_ref, dgp_ref, wd_ref, sem):
        @pl.when(pl.program_id(0) == 0)
        def _():
            _load_weight(wf_ref, wd_ref, sem)
            dgp_ref[...] = jnp.zeros_like(dgp_ref)

        fv = f_ref[...]
        rf = _rsqrt_mean_sq(fv)
        fh = fv * rf
        dn = MACARON * do_ref[...]
        dgp_ref[...] += jnp.sum(dn * fh, axis=0, keepdims=True)
        t = dn * gp_ref[...]
        df = (rf * (t - fh * jnp.mean(t * fh, axis=-1, keepdims=True))).astype(BF16)
        df_ref[...] = df
        for c0 in range(0, D_FF, tc):
            da = _dot_nt(df, wd_ref[c0:c0 + tc, :])
            gv = g_ref[:, c0:c0 + tc].astype(F32)
            uv = u_ref[:, c0:c0 + tc].astype(F32)
            s = _sigmoid(gv)
            dgu_ref[:, c0:c0 + tc] = (da * uv * s * (1.0 + gv * (1.0 - s))).astype(BF16)
            dgu_ref[:, D_FF + c0:D_FF + c0 + tc] = (da * gv * s).astype(BF16)

    row = _ROW(tm)
    wide = pl.BlockSpec((tm, D_FF), lambda i: (i, 0))
    return _call(
        body, name=name, grid=(SEQ // tm,),
        in_specs=[row, row, _VEC, ANY, wide, wide],
        out_specs=[row, pl.BlockSpec((tm, 2 * D_FF), lambda i: (i, 0)), _VEC],
        out_shape=[jax.ShapeDtypeStruct((SEQ, D_MODEL), BF16), jax.ShapeDtypeStruct((SEQ, 2 * D_FF), BF16),
                   jax.ShapeDtypeStruct((1, D_MODEL), F32)],
        scratch_shapes=_weight_scratch(D_FF),
        args=[d_out, f, g_post, w_down, g, u], stages=stages)


def norm_bwd(pieces, w_t, x, g_pre, d_res, name, stages=()):
    tm = MM_ROWS
    widths = [p.shape[1] for p in pieces]
    offs = [sum(widths[:k]) for k in range(len(widths))]
    n_p = len(pieces)
    n_w = len(w_t)

    def body(*refs):
        p_refs = refs[:n_p]
        w_refs = refs[n_p:n_p + n_w]
        x_ref, g_ref, r_ref, dx_ref, dg_ref, wt_ref, sem = refs[n_p + n_w:]

        @pl.when(pl.program_id(0) == 0)
        def _():
            _load_weight(w_refs, wt_ref, sem)
            dg_ref[...] = jnp.zeros_like(dg_ref)

        dn = None
        for p_ref, lo, wd in zip(p_refs, offs, widths):
            part = _dot(p_ref[...], wt_ref[lo:lo + wd, :])
            dn = part if dn is None else dn + part
        xv = x_ref[...]
        r = _rsqrt_mean_sq(xv)
        xh = xv * r
        dg_ref[...] += jnp.sum(dn * xh, axis=0, keepdims=True)
        t = dn * g_ref[...]
        dx_ref[...] = r_ref[...] + r * (t - xh * jnp.mean(t * xh, axis=-1, keepdims=True))

    row = _ROW(tm)
    return _call(
        body, name=name, grid=(SEQ // tm,),
        in_specs=[pl.BlockSpec((tm, wd), lambda i: (i, 0)) for wd in widths] + [ANY] * n_w + [row, _VEC, row],
        out_specs=[row, _VEC],
        out_shape=[jax.ShapeDtypeStruct((SEQ, D_MODEL), F32), jax.ShapeDtypeStruct((1, D_MODEL), F32)],
        scratch_shapes=_weight_scratch(sum(widths), n_w),
        args=[*pieces, *w_t, x, g_pre, d_res], stages=stages)


def mm_tn(pieces, b, tm, name, stages=()):
    widths = [p.shape[1] for p in pieces]
    m_total = sum(widths)
    n_p = len(pieces)
    starts = [sum(widths[:k]) // tm for k in range(n_p)]
    counts = [wd // tm for wd in widths]

    def body(*refs):
        p_refs = refs[:n_p]
        b_ref, o_ref = refs[n_p:]
        i = pl.program_id(0)
        for p_ref, st, ct in zip(p_refs, starts, counts):
            @pl.when((i >= st) & (i < st + ct))
            def _(p_ref=p_ref):
                o_ref[...] = _dot_tn(p_ref[...], b_ref[...]).astype(BF16)

    def piece_spec(st, ct):
        return pl.BlockSpec((SEQ, tm), lambda i: (0, jnp.clip(i - st, 0, ct - 1)))

    (out,), stage_out = _call(
        body, name=name, grid=(m_total // tm,),
        in_specs=[piece_spec(st, ct) for st, ct in zip(starts, counts)] + [pl.BlockSpec((SEQ, D_MODEL), lambda i: (0, 0))],
        out_specs=[pl.BlockSpec((tm, D_MODEL), lambda i: (i, 0))],
        out_shape=[jax.ShapeDtypeStruct((m_total, D_MODEL), BF16)],
        args=[*pieces, b], stages=stages)
    return out, stage_out


def mix_in(h, g_pre, w_in_t, name, stages=()):
    tm = MM_ROWS
    offs = [sum(IN_SEGS[:k]) for k in range(len(IN_SEGS))]
    dts = [F32, F32, F32, F32, BF16, F32, F32]
    n_o = len(IN_SEGS)

    def body(*refs):
        h_ref, g_ref, wf_ref, um_ref = refs[:4]
        o_refs = refs[4:4 + n_o]
        wt_ref, sem = refs[4 + n_o:]

        @pl.when(pl.program_id(0) == 0)
        def _():
            _load_weight(wf_ref, wt_ref, sem)

        hv = h_ref[...]
        um = (hv * _rsqrt_mean_sq(hv) * g_ref[...]).astype(BF16)
        um_ref[...] = um
        for o_ref, lo, wd in zip(o_refs, offs, IN_SEGS):
            for c0 in range(0, wd, 256):
                o_ref[:, c0:c0 + 256] = _dot_nt(um, wt_ref[lo + c0:lo + c0 + 256, :]).astype(o_ref.dtype)

    return _call(
        body, name=name, grid=(SEQ // tm,),
        in_specs=[_ROW(tm), _VEC, ANY],
        out_specs=[_ROW(tm)] + [pl.BlockSpec((tm, wd), lambda i: (i, 0)) for wd in IN_SEGS],
        out_shape=[jax.ShapeDtypeStruct((SEQ, D_MODEL), BF16)]
        + [jax.ShapeDtypeStruct((SEQ, wd), dt) for wd, dt in zip(IN_SEGS, dts)],
        scratch_shapes=_weight_scratch(IN_W),
        args=[h, g_pre, w_in_t], stages=stages)


LRU_TC = 256


def _conv_fwd(xb, cw, cb, tt):
    xc = xb * cw[3:4, :] + cb
    shifted = []
    for s in (1, 2, 3):
        sh = jnp.where(tt >= s, pltpu.roll(xb, s, 0), 0.0)
        shifted.append(sh)
        xc = xc + sh * cw[3 - s:4 - s, :]
    return xc, shifted


def _lru_gates(xc, wa, ba, wx, bx, lam):
    xcb = xc.astype(BF16)
    r = _sigmoid(_dot(xcb, wa) + ba)
    i = _sigmoid(_dot(xcb, wx) + bx)
    nl = -lam
    sp = jnp.maximum(nl, 0.0) + jnp.log1p(jnp.exp(-jnp.abs(nl)))
    la = (-LRU_C * r) * sp
    a = jnp.exp(la)
    mult = jnp.sqrt(jnp.maximum(-_expm1(2.0 * la), 0.0))
    return xcb, r, i, sp, a, mult


def _scan(a, b, tt, reverse):
    n = a.shape[0]
    s = 1
    while s < n:
        more = 2 * s < n
        if s < 8:
            if reverse:
                keep = tt < n - s
                shift = n - s
            else:
                keep = tt >= s
                shift = s
            b = a * jnp.where(keep, pltpu.roll(b, shift, 0), 0.0) + b
            if more:
                a = a * jnp.where(keep, pltpu.roll(a, shift, 0), 1.0)
        elif reverse:
            b = jnp.concatenate([a[:n - s] * b[s:] + b[:n - s], b[n - s:]], axis=0)
            if more:
                a = jnp.concatenate([a[:n - s] * a[s:], a[n - s:]], axis=0)
        else:
            b = jnp.concatenate([b[:s], a[s:] * b[:n - s] + b[s:]], axis=0)
            if more:
                a = jnp.concatenate([a[:s], a[s:] * a[:n - s]], axis=0)
        s *= 2
    return b


def _lru_specs():
    col = pl.BlockSpec((SEQ, LRU_TC), lambda j: (0, j))
    vec = pl.BlockSpec((1, LRU_TC), lambda j: (0, j))
    bd = pl.BlockSpec((1, LRU_TC, LRU_TC), lambda j: (j, 0, 0))
    cw = pl.BlockSpec((4, LRU_TC), lambda j: (0, j))
    return col, vec, bd, cw


def lru_fwd(gate, xbr, conv_w, conv_b, wa_bd, b_a, wx_bd, b_x, lam, name, stages=()):
    col, vec, bd, cw = _lru_specs()

    def body(gate_ref, xbr_ref, cw_ref, cb_ref, wa_ref, ba_ref, wx_ref, bx_ref, lam_ref, y_ref, h_ref):
        tt = lax.broadcasted_iota(jnp.int32, (SEQ, LRU_TC), 0)
        xc, _ = _conv_fwd(xbr_ref[...], cw_ref[...], cb_ref[...], tt)
        _, r, i, sp, a, mult = _lru_gates(xc, wa_ref[0], ba_ref[...], wx_ref[0], bx_ref[...], lam_ref[...])
        h = _scan(a, mult * (i * xc), tt, reverse=False)
        h_ref[...] = h
        gl, _ = _gelu(gate_ref[...])
        y_ref[...] = (h * gl).astype(BF16)

    return _call(
        body, name=name, grid=(LRU_W // LRU_TC,),
        in_specs=[col, col, cw, vec, bd, vec, bd, vec, vec],
        out_specs=[col, col],
        out_shape=[jax.ShapeDtypeStruct((SEQ, LRU_W), BF16), jax.ShapeDtypeStruct((SEQ, LRU_W), F32)],
        args=[gate, xbr, conv_w, conv_b, wa_bd, b_a, wx_bd, b_x, lam], stages=stages)


def lru_bwd(gate, xbr, h, dy, conv_w, conv_b, wa_bd, b_a, wx_bd, b_x, lam, name, stages=()):
    col, vec, bd, cw = _lru_specs()

    def body(gate_ref, xbr_ref, h_ref, dy_ref, cw_ref, cb_ref, wa_ref, ba_ref, wx_ref, bx_ref, lam_ref,
             dgate_ref, dxbr_ref, vecs_ref, dwa_ref, dwx_ref):
        tt = lax.broadcasted_iota(jnp.int32, (SEQ, LRU_TC), 0)
        cwv = cw_ref[...]
        lam = lam_ref[...]
        xb = xbr_ref[...]
        xc, shifted = _conv_fwd(xb, cwv, cb_ref[...], tt)
        wa = wa_ref[0]
        wx = wx_ref[0]
        xcb, r, i, sp, a, mult = _lru_gates(xc, wa, ba_ref[...], wx, bx_ref[...], lam)
        hv = h_ref[...]
        dyv = dy_ref[...]
        gv = gate_ref[...]
        gl, th = _gelu(gv)
        dgate_ref[...] = (dyv * hv * _gelu_grad(gv, th)).astype(BF16)
        a_next = jnp.where(tt < SEQ - 1, pltpu.roll(a, SEQ - 1, 0), 0.0)
        gsum = _scan(a_next, dyv * gl, tt, reverse=True)
        h_prev = jnp.where(tt >= 1, pltpu.roll(hv, 1, 0), 0.0)
        d_mult = gsum * i * xc
        d_i = gsum * mult * xc
        d_xc = gsum * mult * i
        d_la = gsum * h_prev * a - d_mult * (a * a) / mult
        d_pr = (d_la * (-LRU_C * sp)) * r * (1.0 - r)
        d_pi = d_i * i * (1.0 - i)
        d_lam = jnp.sum(d_la * r, axis=0, keepdims=True) * (LRU_C * _sigmoid(-lam))
        d_prb = d_pr.astype(BF16)
        d_pib = d_pi.astype(BF16)
        d_xc = d_xc + _dot_nt(d_prb, wa) + _dot_nt(d_pib, wx)
        dwa_ref[0] = _dot_tn(xcb, d_prb)
        dwx_ref[0] = _dot_tn(xcb, d_pib)
        rows = [jnp.sum(d_xc * shifted[2], axis=0, keepdims=True),
                jnp.sum(d_xc * shifted[1], axis=0, keepdims=True),
                jnp.sum(d_xc * shifted[0], axis=0, keepdims=True),
                jnp.sum(d_xc * xb, axis=0, keepdims=True),
                jnp.sum(d_xc, axis=0, keepdims=True),
                jnp.sum(d_pr, axis=0, keepdims=True),
                jnp.sum(d_pi, axis=0, keepdims=True),
                d_lam]
        ri = lax.broadcasted_iota(jnp.int32, (8, LRU_TC), 0)
        acc = jnp.zeros((8, LRU_TC), F32)
        for k, rv in enumerate(rows):
            acc = jnp.where(ri == k, rv, acc)
        vecs_ref[...] = acc
        d_xb = d_xc * cwv[3:4, :]
        for s in (1, 2, 3):
            d_xb = d_xb + jnp.where(tt < SEQ - s, pltpu.roll(d_xc, SEQ - s, 0), 0.0) * cwv[3 - s:4 - s, :]
        dxbr_ref[...] = d_xb.astype(BF16)

    return _call(
        body, name=name, grid=(LRU_W // LRU_TC,),
        in_specs=[col, col, col, col, cw, vec, bd, vec, bd, vec, vec],
        out_specs=[col, col, pl.BlockSpec((8, LRU_TC), lambda j: (0, j)), bd, bd],
        out_shape=[jax.ShapeDtypeStruct((SEQ, LRU_W), BF16), jax.ShapeDtypeStruct((SEQ, LRU_W), BF16),
                   jax.ShapeDtypeStruct((8, LRU_W), F32),
                   jax.ShapeDtypeStruct((LRU_W // LRU_TC, LRU_TC, LRU_TC), F32),
                   jax.ShapeDtypeStruct((LRU_W // LRU_TC, LRU_TC, LRU_TC), F32)],
        args=[gate, xbr, h, dy, conv_w, conv_b, wa_bd, b_a, wx_bd, b_x, lam], stages=stages)


def _rope(x, cos, sin_signed):
    w = x.shape[1]
    reps = w // 128
    if reps > 1:
        cos = jnp.tile(cos, (1, reps))
        sin_signed = jnp.tile(sin_signed, (1, reps))
    lane = lax.broadcasted_iota(jnp.int32, x.shape, 1)
    first = (lane & 63) < 32
    partner = jnp.where(first, pltpu.roll(x, w - 32, 1), pltpu.roll(x, 32, 1))
    return x * cos + partner * sin_signed


def _both_halves(t, odd):
    lo = lax.broadcasted_iota(jnp.int32, t.shape, 1) < 64
    rolled = pltpu.roll(t, 64, 1)
    return jnp.where(lo, rolled, t) if odd else jnp.where(lo, t, rolled)


def _stack_heads(ta, tb):
    lo = lax.broadcasted_iota(jnp.int32, ta.shape, 1) < 64
    return jnp.concatenate([jnp.where(lo, ta, 0.0), jnp.where(lo, 0.0, ta),
                            jnp.where(lo, tb, 0.0), jnp.where(lo, 0.0, tb)], axis=0)


def _unstack_heads(o):
    lo = lax.broadcasted_iota(jnp.int32, (ATTN_BLOCK, 128), 1) < 64
    return (jnp.where(lo, o[0:128], o[128:256]), jnp.where(lo, o[256:384], o[384:512]))


def _window_upper():
    shape = (4 * ATTN_BLOCK, ATTN_BLOCK)
    return lax.broadcasted_iota(jnp.int32, shape, 1) > (lax.broadcasted_iota(jnp.int32, shape, 0) & (ATTN_BLOCK - 1))


def _fold(t, upper):
    return jnp.where(upper, t[:, :ATTN_BLOCK], t[:, ATTN_BLOCK:])


def _unfold(t, upper):
    zero = jnp.zeros_like(t)
    return jnp.concatenate([jnp.where(upper, t, zero), jnp.where(upper, zero, t)], axis=1)


def _attn_probs(qs, kd, sinks_ref, hk, first_block, upper):
    s = _fold(_dot_nt(qs, kd), upper) * (HEAD_DIM ** -0.5)
    s = jnp.where(jnp.logical_and(upper, first_block), MASK_VALUE, s)
    rg = lax.broadcasted_iota(jnp.int32, (4 * ATTN_BLOCK, 1), 0) >> 7
    sink = jnp.where(rg == 0, sinks_ref[4 * hk],
                     jnp.where(rg == 1, sinks_ref[4 * hk + 1],
                               jnp.where(rg == 2, sinks_ref[4 * hk + 2], sinks_ref[4 * hk + 3])))
    m = jnp.maximum(jnp.max(s, axis=1, keepdims=True), sink)
    e = jnp.exp(s - m)
    es = jnp.exp(sink - m)
    inv = 1.0 / (jnp.sum(e, axis=1, keepdims=True) + es)
    return e * inv, es * inv


def _prev(i):
    return jnp.maximum(i - 1, 0)


def attn_fwd(q, k, v, cos, sin_signed, sinks, name, stages=()):
    nb = ATTN_BLOCK

    def body(q_ref, kc_ref, kp_ref, vc_ref, vp_ref, cc_ref, sc_ref, cp_ref, sp_ref, sinks_ref,
             qr_ref, kr_ref, y_ref):
        first_block = pl.program_id(0) == 0
        qr = _rope(q_ref[...], cc_ref[...], sc_ref[...])
        kc = _rope(kc_ref[...], cc_ref[...], sc_ref[...])
        kp = _rope(kp_ref[...], cp_ref[...], sp_ref[...])
        qr_ref[...] = qr.astype(BF16)
        kr_ref[...] = kc.astype(BF16)
        k2 = jnp.concatenate([kp, kc], axis=0)
        v2 = jnp.concatenate([vp_ref[...].astype(F32), vc_ref[...].astype(F32)], axis=0)
        upper = _window_upper()
        for hk in range(N_KV_HEADS):
            kt = hk // 2
            kd = _both_halves(k2[:, kt * 128:(kt + 1) * 128], hk % 2).astype(BF16)
            vd = _both_halves(v2[:, kt * 128:(kt + 1) * 128], hk % 2).astype(BF16)
            qs = _stack_heads(qr[:, (2 * hk) * 128:(2 * hk + 1) * 128],
                              qr[:, (2 * hk + 1) * 128:(2 * hk + 2) * 128]).astype(BF16)
            p, _ = _attn_probs(qs, kd, sinks_ref, hk, first_block, upper)
            ta, tb = _unstack_heads(_dot(_unfold(p.astype(BF16), upper), vd))
            y_ref[:, (2 * hk) * 128:(2 * hk + 1) * 128] = ta.astype(BF16)
            y_ref[:, (2 * hk + 1) * 128:(2 * hk + 2) * 128] = tb.astype(BF16)

    cur = lambda w: pl.BlockSpec((nb, w), lambda i: (i, 0))
    prv = lambda w: pl.BlockSpec((nb, w), lambda i: (_prev(i), 0))
    return _call(
        body, name=name, grid=(N_ATTN_BLOCKS,),
        in_specs=[cur(D_MODEL), cur(KV_W), prv(KV_W), cur(KV_W), prv(KV_W), cur(128), cur(128), prv(128), prv(128),
                  pl.BlockSpec(memory_space=pltpu.SMEM)],
        out_specs=[cur(D_MODEL), cur(KV_W), cur(D_MODEL)],
        out_shape=[jax.ShapeDtypeStruct((SEQ, D_MODEL), BF16), jax.ShapeDtypeStruct((SEQ, KV_W), BF16),
                   jax.ShapeDtypeStruct((SEQ, D_MODEL), BF16)],
        args=[q, k, k, v, v, cos, sin_signed, cos, sin_signed, sinks], stages=stages)


def attn_bwd(qr, kr, v, dy, cos, sin_signed, sinks, name, stages=()):
    nb = ATTN_BLOCK
    n_steps = N_ATTN_BLOCKS + 1
    scale = HEAD_DIM ** -0.5

    def body(q_ref, kc_ref, kp_ref, vc_ref, vp_ref, dy_ref, cc_ref, sc_ref, cp_ref, sp_ref, sinks_ref,
             dq_ref, dkv_ref, dsk_ref, ck_ref, cv_ref):
        dk_ref = dkv_ref.at[:, pl.ds(0, KV_W)]
        dv_ref = dkv_ref.at[:, pl.ds(KV_W, KV_W)]
        i = pl.program_id(0)

        @pl.when(i == 0)
        def _():
            dsk_ref[...] = jnp.zeros_like(dsk_ref)
            ck_ref[...] = jnp.zeros_like(ck_ref)
            cv_ref[...] = jnp.zeros_like(cv_ref)

        @pl.when(i < N_ATTN_BLOCKS)
        def _():
            qv = q_ref[...].astype(F32)
            dov = dy_ref[...].astype(F32)
            k2 = jnp.concatenate([kp_ref[...].astype(F32), kc_ref[...].astype(F32)], axis=0)
            v2 = jnp.concatenate([vp_ref[...].astype(F32), vc_ref[...].astype(F32)], axis=0)
            lane = lax.broadcasted_iota(jnp.int32, (8, 128), 1)
            lo = lax.broadcasted_iota(jnp.int32, (2 * nb, 128), 1) < 64
            dsk = jnp.zeros((8, 128), F32)
            dk_tiles = []
            dv_tiles = []
            upper = _window_upper()
            for hk in range(N_KV_HEADS):
                kt = hk // 2
                kd = _both_halves(k2[:, kt * 128:(kt + 1) * 128], hk % 2).astype(BF16)
                vd = _both_halves(v2[:, kt * 128:(kt + 1) * 128], hk % 2).astype(BF16)
                qs = _stack_heads(qv[:, (2 * hk) * 128:(2 * hk + 1) * 128],
                                  qv[:, (2 * hk + 1) * 128:(2 * hk + 2) * 128]).astype(BF16)
                dos = _stack_heads(dov[:, (2 * hk) * 128:(2 * hk + 1) * 128],
                                   dov[:, (2 * hk + 1) * 128:(2 * hk + 2) * 128]).astype(BF16)
                p, ps = _attn_probs(qs, kd, sinks_ref, hk, i == 0, upper)
                dp = _fold(_dot_nt(dos, vd), upper)
                delta = jnp.sum(p * dp, axis=1, keepdims=True)
                ds = _unfold((p * (dp - delta)).astype(BF16), upper)
                dsink = -ps * delta
                for g in range(4):
                    dsk = dsk + jnp.where(lane == 4 * hk + g, jnp.sum(dsink[g * nb:(g + 1) * nb]), 0.0)
                ta, tb = _unstack_heads(_dot(ds, kd) * scale)
                dq_a = (2 * hk) * 128
                dq_ref[:, dq_a:dq_a + 128] = _rope(ta, cc_ref[...], -sc_ref[...]).astype(BF16)
                dq_ref[:, dq_a + 128:dq_a + 256] = _rope(tb, cc_ref[...], -sc_ref[...]).astype(BF16)
                rk = _dot_tn(ds, qs) * scale
                rv = _dot_tn(_unfold(p.astype(BF16), upper), dos)
                dk_tiles.append(rk + pltpu.roll(rk, 64, 1))
                dv_tiles.append(rv + pltpu.roll(rv, 64, 1))
            dsk_ref[...] += dsk
            dk_full = jnp.concatenate([jnp.where(lo, dk_tiles[0], dk_tiles[1]),
                                       jnp.where(lo, dk_tiles[2], dk_tiles[3])], axis=1)
            dv_full = jnp.concatenate([jnp.where(lo, dv_tiles[0], dv_tiles[1]),
                                       jnp.where(lo, dv_tiles[2], dv_tiles[3])], axis=1)
            dk_ref[...] = _rope(ck_ref[...] + dk_full[0:nb], cp_ref[...], -sp_ref[...]).astype(BF16)
            dv_ref[...] = (cv_ref[...] + dv_full[0:nb]).astype(BF16)
            ck_ref[...] = dk_full[nb:2 * nb]
            cv_ref[...] = dv_full[nb:2 * nb]

        @pl.when(i == N_ATTN_BLOCKS)
        def _():
            dk_ref[...] = _rope(ck_ref[...], cp_ref[...], -sp_ref[...]).astype(BF16)
            dv_ref[...] = cv_ref[...].astype(BF16)

    qi = lambda i: jnp.minimum(i, N_ATTN_BLOCKS - 1)
    cur = lambda w: pl.BlockSpec((nb, w), lambda i: (qi(i), 0))
    prv = lambda w: pl.BlockSpec((nb, w), lambda i: (_prev(qi(i)), 0))
    out_prev = lambda w: pl.BlockSpec((nb, w), lambda i: (_prev(i), 0))
    return _call(
        body, name=name, grid=(n_steps,),
        in_specs=[cur(D_MODEL), cur(KV_W), prv(KV_W), cur(KV_W), prv(KV_W), cur(D_MODEL),
                  cur(128), cur(128), out_prev(128), out_prev(128), pl.BlockSpec(memory_space=pltpu.SMEM)],
        out_specs=[cur(D_MODEL), out_prev(2 * KV_W), pl.BlockSpec((8, 128), lambda i: (0, 0))],
        out_shape=[jax.ShapeDtypeStruct((SEQ, D_MODEL), BF16), jax.ShapeDtypeStruct((SEQ, 2 * KV_W), BF16),
                   jax.ShapeDtypeStruct((8, 128), F32)],
        scratch_shapes=[pltpu.VMEM((nb, KV_W), F32), pltpu.VMEM((nb, KV_W), F32)],
        args=[qr, kr, kr, v, v, dy, cos, sin_signed, cos, sin_signed, sinks], stages=stages)


def _proj_scratch():
    return [pltpu.VMEM((D_MODEL, D_MODEL), BF16)] * 3 + [pltpu.SemaphoreType.DMA((3 * N_CHIPS,))]


def _load_projs(w_refs, wl_ref, wa_ref, wo_ref, sem):
    for k, (w_ref, dst) in enumerate(zip(w_refs, (wl_ref, wa_ref, wo_ref))):
        _load_weight(w_ref, dst, sem.at[pl.ds(k * N_CHIPS, N_CHIPS)])


def merge_fwd(y_lru, y_attn, g_lru, g_attn, projs, g_post, h_in, name, stages=()):
    tm = MM_ROWS

    def body(yl_ref, ya_ref, gl_ref, ga_ref, w1_ref, w2_ref, w3_ref, gp_ref, h_ref,
             pl_ref, pa_ref, mg_ref, m_ref, o_ref, wl_ref, wa_ref, wo_ref, sem):
        @pl.when(pl.program_id(0) == 0)
        def _():
            _load_projs((w1_ref, w2_ref, w3_ref), wl_ref, wa_ref, wo_ref, sem)

        p_l = _dot(yl_ref[...], wl_ref[...])
        p_a = _dot(ya_ref[...], wa_ref[...])
        pl_ref[...] = p_l.astype(BF16)
        pa_ref[...] = p_a.astype(BF16)
        merged = (_sigmoid(gl_ref[...]) * p_l + _sigmoid(ga_ref[...]) * p_a).astype(BF16)
        mg_ref[...] = merged
        m = _dot(merged, wo_ref[...])
        m_ref[...] = m
        o_ref[...] = h_ref[...] + m * _rsqrt_mean_sq(m) * gp_ref[...]

    row = _ROW(tm)
    return _call(
        body, name=name, grid=(SEQ // tm,),
        in_specs=[row, row, row, row, ANY, ANY, ANY, _VEC, row],
        out_specs=[row] * 5,
        out_shape=[jax.ShapeDtypeStruct((SEQ, D_MODEL), BF16)] * 3 + [jax.ShapeDtypeStruct((SEQ, D_MODEL), F32)] * 2,
        scratch_shapes=_proj_scratch(),
        args=[y_lru, y_attn, g_lru, g_attn, *projs, g_post, h_in], stages=stages)


def merge_bwd(d_out, m, g_post, projs, g_lru, g_attn, p_l, p_a, name, stages=()):
    tm = 256

    def body(do_ref, m_ref, gp_ref, w1_ref, w2_ref, w3_ref, gl_ref, ga_ref, pl_ref, pa_ref,
             dm_ref, dpl_ref, dpa_ref, dgl_ref, dga_ref, dya_ref, dyl_ref, dgp_ref, wl_ref, wa_ref, wo_ref, sem):
        @pl.when(pl.program_id(0) == 0)
        def _():
            _load_projs((w1_ref, w2_ref, w3_ref), wl_ref, wa_ref, wo_ref, sem)
            dgp_ref[...] = jnp.zeros_like(dgp_ref)

        mv = m_ref[...]
        rm = _rsqrt_mean_sq(mv)
        mh = mv * rm
        dn = do_ref[...]
        dgp_ref[...] += jnp.sum(dn * mh, axis=0, keepdims=True)
        t = dn * gp_ref[...]
        dm = (rm * (t - mh * jnp.mean(t * mh, axis=-1, keepdims=True))).astype(BF16)
        dm_ref[...] = dm
        dmg = _dot_nt(dm, wo_ref[...])
        sl = _sigmoid(gl_ref[...])
        sa = _sigmoid(ga_ref[...])
        dpl = (dmg * sl).astype(BF16)
        dpa = (dmg * sa).astype(BF16)
        dpl_ref[...] = dpl
        dpa_ref[...] = dpa
        dgl_ref[...] = (dmg * pl_ref[...].astype(F32) * sl * (1.0 - sl)).astype(BF16)
        dga_ref[...] = (dmg * pa_ref[...].astype(F32) * sa * (1.0 - sa)).astype(BF16)
        dyl_ref[...] = _dot_nt(dpl, wl_ref[...])
        dya_ref[...] = _dot_nt(dpa, wa_ref[...]).astype(BF16)

    row = _ROW(tm)
    return _call(
        body, name=name, grid=(SEQ // tm,),
        in_specs=[row, row, _VEC, ANY, ANY, ANY, row, row, row, row],
        out_specs=[row] * 7 + [_VEC],
        out_shape=[jax.ShapeDtypeStruct((SEQ, D_MODEL), BF16)] * 6 + [jax.ShapeDtypeStruct((SEQ, D_MODEL), F32),
                                                                       jax.ShapeDtypeStruct((1, D_MODEL), F32)],
        scratch_shapes=_proj_scratch(),
        args=[d_out, m, g_post, *projs, g_lru, g_attn, p_l, p_a], stages=stages)


def _rope_tables():
    half = HEAD_DIM // 2
    inv_freq = np.float32(ROPE_THETA) ** (-np.arange(half, dtype=np.float32) / np.float32(half))
    ang = np.arange(SEQ, dtype=np.float32)[:, None] * inv_freq[None, :]
    cos, sin = np.cos(ang), np.sin(ang)
    return (jnp.asarray(np.tile(np.concatenate([cos, cos], axis=1), (1, 2))),
            jnp.asarray(np.tile(np.concatenate([-sin, sin], axis=1), (1, 2))))


def _block_diag(w):
    per = LRU_TC // LRU_BLOCK_W
    w4 = w.reshape(LRU_W // LRU_TC, per, LRU_BLOCK_W, LRU_BLOCK_W)
    eye = jnp.eye(per, dtype=w.dtype)
    return jnp.einsum('jacd,ab->jacbd', w4, eye).reshape(LRU_W // LRU_TC, LRU_TC, LRU_TC).astype(BF16)


def _diag_blocks(p):
    per = LRU_TC // LRU_BLOCK_W
    p5 = p.reshape(LRU_W // LRU_TC, per, LRU_BLOCK_W, per, LRU_BLOCK_W)
    return jnp.stack([p5[:, a, :, a, :] for a in range(per)], axis=1).reshape(LRU_W // LRU_BLOCK_W, LRU_BLOCK_W, LRU_BLOCK_W)


def _place():
    x, y, c = lax.axis_index('x'), lax.axis_index('y'), lax.axis_index('c')
    chips = [(1 - x, y), (x, 1 - y), (1 - x, 1 - y)]
    return x, y, c, chips


def _rcopy(src, dst, send_sem, recv_sem, to):
    return pltpu.make_async_remote_copy(src_ref=src, dst_ref=dst, send_sem=send_sem, recv_sem=recv_sem,
                                        device_id=to, device_id_type=MESH)


class _Stage:
    inputs, out_shape, scratch, peers = (), (), (), ()

    def start(self, ins, outs, scr):
        plan = self._plan(ins, outs, scr)
        for ld in plan['loads']:
            ld.start()
        for cp in plan['sends']:
            cp.start()

    def relay(self, ins, outs, scr):
        pass

    def mid(self, ins, outs, scr):
        plan = self._plan(ins, outs, scr)
        for ld, st in zip(plan['loads'], plan['stores']):
            ld.wait()
            st.start()
        for arrived, onward in zip(plan['arrivals'], plan['forwards']):
            arrived.wait_recv()
            onward.start()

    def end(self, ins, outs, scr):
        plan = self._plan(ins, outs, scr)
        for st in plan['stores']:
            st.wait()
        for arrived in (plan['final_arrivals'] if plan['forwards'] else plan['arrivals']):
            arrived.wait_recv()
        for cp in plan['sends'] + plan['forwards']:
            cp.wait_send()


def _empty_plan():
    return dict(loads=[], stores=[], sends=[], arrivals=[], forwards=[], final_arrivals=[])


class GatherStage(_Stage):
    peers = ('chips', 'sib')
    N_CP = 12

    def __init__(self, items):
        self.ranges = [(off, rows) for _, off, rows in items]
        self.inputs = [src for src, _, _ in items]
        self.out_shape = [jax.ShapeDtypeStruct((N_CHIPS, rows, D_MODEL), BF16) for _, rows in self.ranges]
        n = self.N_CP * len(items)
        self.scratch = [pltpu.VMEM((sum(r for _, r in self.ranges), D_MODEL), BF16), pltpu.SemaphoreType.DMA((n,)),
                        pltpu.SemaphoreType.DMA((n,)), pltpu.SemaphoreType.DMA((2 * len(items),))]

    def _plan(self, ins, outs, scr):
        buf, send, recv, lsem = scr
        x, y, c, _ = _place()
        me_q, q_x, q_y, q_d = 2 * x + y, 2 * (1 - x) + y, 2 * x + (1 - y), 2 * (1 - x) + (1 - y)
        to_x, to_y, sib = (1 - x, y, c), (x, 1 - y, c), (x, y, 1 - c)
        plan = dict(loads=[], stores=[], first=[], early=[], relays=[], late=[], hand_early=[], hand_late=[], final=[])
        boff = 0
        for w, ((off, rows), p_ref, o_ref) in enumerate(zip(self.ranges, ins, outs)):
            hr = rows // 2
            ch = hr // 2
            plan['loads'].append(pltpu.make_async_copy(p_ref.at[pl.ds(off, rows)], buf.at[pl.ds(boff, rows)], lsem.at[2 * w]))
            plan['stores'].append(pltpu.make_async_copy(buf.at[pl.ds(boff, rows)], o_ref.at[me_q], lsem.at[2 * w + 1]))
            boff += rows
            base = w * self.N_CP
            mine = [pl.ds(pl.multiple_of(c * hr + k * ch, 16), ch) for k in range(2)]
            theirs = [pl.ds(pl.multiple_of((1 - c) * hr + k * ch, 16), ch) for k in range(2)]
            src = [p_ref.at[pl.ds(pl.multiple_of(off + c * hr + k * ch, 16), ch)] for k in range(2)]

            def cp(k, s, d, to):
                return _rcopy(s, d, send.at[base + k], recv.at[base + k], to)

            def here(q, rows_):
                return o_ref.at[q, rows_]

            plan['first'] += [cp(0, src[0], here(me_q, mine[0]), to_x), cp(2, src[1], here(me_q, mine[1]), to_y),
                              cp(1, src[1], here(me_q, mine[1]), to_x), cp(3, src[0], here(me_q, mine[0]), to_y)]
            x_a, y_b = here(q_x, mine[0]), here(q_y, mine[1])
            plan['early'] += [cp(0, x_a, x_a, to_x), cp(2, y_b, y_b, to_y)]
            plan['relays'] += [cp(4, x_a, x_a, to_y), cp(5, y_b, y_b, to_x)]
            plan['hand_early'] += [cp(6, x_a, x_a, sib), cp(7, y_b, y_b, sib)]
            x_b, y_a, d_a, d_b = here(q_x, mine[1]), here(q_y, mine[0]), here(q_d, mine[0]), here(q_d, mine[1])
            plan['late'] += [cp(1, x_b, x_b, to_x), cp(3, y_a, y_a, to_y), cp(4, d_a, d_a, to_y), cp(5, d_b, d_b, to_x)]
            plan['hand_late'] += [cp(8, x_b, x_b, sib), cp(9, y_a, y_a, sib), cp(10, d_a, d_a, sib), cp(11, d_b, d_b, sib)]
            for k, (q, piece) in enumerate([(q_x, 0), (q_y, 1), (q_x, 1), (q_y, 0), (q_d, 0), (q_d, 1)]):
                got = here(q, theirs[piece])
                plan['final'].append(cp(6 + k, got, got, sib))
        return plan

    def start(self, ins, outs, scr):
        plan = self._plan(ins, outs, scr)
        for ld in plan['loads']:
            ld.start()
        for cp in plan['first']:
            cp.start()

    def relay(self, ins, outs, scr):
        plan = self._plan(ins, outs, scr)
        for arrived in plan['early']:
            arrived.wait_recv()
        for cp in plan['relays'] + plan['hand_early']:
            cp.start()

    def mid(self, ins, outs, scr):
        plan = self._plan(ins, outs, scr)
        for ld, st in zip(plan['loads'], plan['stores']):
            ld.wait()
            st.start()
        for arrived in plan['late']:
            arrived.wait_recv()
        for cp in plan['hand_late']:
            cp.start()

    def end(self, ins, outs, scr):
        plan = self._plan(ins, outs, scr)
        for st in plan['stores']:
            st.wait()
        for arrived in plan['final']:
            arrived.wait_recv()
        for cp in plan['first'] + plan['relays'] + plan['hand_early'] + plan['hand_late']:
            cp.wait_send()


class PairStage(_Stage):
    peers = ('sib',)

    def __init__(self, grads):
        self.inputs = list(grads)
        self.out_shape = [jax.ShapeDtypeStruct((N_CHIPS, 1) + g.shape[2:], BF16) for g in grads]
        n_cp = N_CHIPS * len(grads)
        self.scratch = [pltpu.SemaphoreType.DMA((n_cp,)), pltpu.SemaphoreType.DMA((n_cp,))]

    def _plan(self, ins, outs, scr):
        send, recv = scr
        x, y, c, _ = _place()
        plan = _empty_plan()
        for w, (g_ref, l_ref) in enumerate(zip(ins, outs)):
            for q in range(N_CHIPS):
                i = w * N_CHIPS + q
                plan['sends'].append(_rcopy(g_ref.at[q, pl.ds(1 - c, 1)], l_ref.at[q], send.at[i], recv.at[i], (x, y, 1 - c)))
        plan['arrivals'] = plan['sends']
        return plan


class ChipStage(_Stage):
    peers = ('chips',)

    def __init__(self, items):
        self.ranges = [(off, n) for _, off, n in items]
        self.inputs = [s for s, _, _ in items]
        self.out_shape = [jax.ShapeDtypeStruct((N_CHIPS, n, D_MODEL), BF16) for _, n in self.ranges]
        n_cp = 3 * len(items)
        self.scratch = [pltpu.VMEM((sum(n for _, n in self.ranges), D_MODEL), BF16), pltpu.SemaphoreType.DMA((n_cp,)),
                        pltpu.SemaphoreType.DMA((n_cp,)), pltpu.SemaphoreType.DMA((2 * len(items),))]

    def _plan(self, ins, outs, scr):
        buf, send, recv, lsem = scr
        x, y, c, chips = _place()
        me_q = 2 * x + y
        plan = _empty_plan()
        boff = 0
        for w, ((off, n), s_ref, l_ref) in enumerate(zip(self.ranges, ins, outs)):
            rows = pl.ds(off, n)
            plan['loads'].append(pltpu.make_async_copy(s_ref.at[me_q, rows], buf.at[pl.ds(boff, n)], lsem.at[2 * w]))
            plan['stores'].append(pltpu.make_async_copy(buf.at[pl.ds(boff, n)], l_ref.at[me_q], lsem.at[2 * w + 1]))
            boff += n
            for j, (cx, cy) in enumerate(chips):
                i = w * 3 + j
                got = l_ref.at[2 * cx + cy]
                plan['sends'].append(_rcopy(s_ref.at[2 * cx + cy, rows], l_ref.at[me_q], send.at[i], recv.at[i], (cx, cy, c)))
                plan['arrivals'].append(_rcopy(got, got, send.at[i], recv.at[i], (cx, cy, c)))
        return plan


class SwapStage(_Stage):
    peers = ('sib',)

    def __init__(self, items):
        n = len(items)
        self.inputs = list(items)
        self.out_shape = [jax.ShapeDtypeStruct((2,) + a.shape, a.dtype) for a in items]
        self.scratch = [pltpu.VMEM(a.shape, a.dtype) for a in items] + [
            pltpu.SemaphoreType.DMA((n,)), pltpu.SemaphoreType.DMA((n,)), pltpu.SemaphoreType.DMA((2 * n,))]

    def _plan(self, ins, outs, scr):
        bufs, (send, recv, lsem) = scr[:len(ins)], scr[len(ins):]
        x, y, c, _ = _place()
        plan = _empty_plan()
        for w, (h_ref, o_ref, buf) in enumerate(zip(ins, outs, bufs)):
            plan['loads'].append(pltpu.make_async_copy(h_ref, buf, lsem.at[2 * w]))
            plan['stores'].append(pltpu.make_async_copy(buf, o_ref.at[c], lsem.at[2 * w + 1]))
            got = o_ref.at[1 - c]
            plan['sends'].append(_rcopy(h_ref, o_ref.at[c], send.at[w], recv.at[w], (x, y, 1 - c)))
            plan['arrivals'].append(_rcopy(got, got, send.at[w], recv.at[w], (x, y, 1 - c)))
        return plan


class SmallGatherStage(_Stage):
    peers = ('chips', 'sib')

    def __init__(self, blk):
        self.inputs = [blk]
        self.out_shape = [jax.ShapeDtypeStruct((N_DEV,) + blk.shape, blk.dtype)]
        self.scratch = [pltpu.VMEM(blk.shape, blk.dtype), pltpu.SemaphoreType.DMA((7,)), pltpu.SemaphoreType.DMA((7,)),
                        pltpu.SemaphoreType.DMA((2,))]

    def _plan(self, ins, outs, scr):
        (x_ref,), (o_ref,), (buf, send, recv, lsem) = ins, outs, scr
        x, y, c, chips = _place()
        sib = (x, y, 1 - c)

        def slot(px, py, pc):
            return o_ref.at[4 * px + 2 * py + pc]

        plan = _empty_plan()
        plan['loads'].append(pltpu.make_async_copy(x_ref, buf, lsem.at[0]))
        plan['stores'].append(pltpu.make_async_copy(buf, slot(x, y, c), lsem.at[1]))
        from_sib = slot(x, y, 1 - c)
        plan['sends'].append(_rcopy(x_ref, slot(x, y, c), send.at[0], recv.at[0], sib))
        plan['final_arrivals'].append(_rcopy(from_sib, from_sib, send.at[0], recv.at[0], sib))
        for j, (cx, cy) in enumerate(chips):
            got, got_sib = slot(cx, cy, c), slot(cx, cy, 1 - c)
            plan['sends'].append(_rcopy(x_ref, slot(x, y, c), send.at[1 + j], recv.at[1 + j], (cx, cy, c)))
            plan['arrivals'].append(_rcopy(got, got, send.at[1 + j], recv.at[1 + j], (cx, cy, c)))
            plan['forwards'].append(_rcopy(got, got, send.at[4 + j], recv.at[4 + j], sib))
            plan['final_arrivals'].append(_rcopy(got_sib, got_sib, send.at[4 + j], recv.at[4 + j], sib))
        return plan


_HBM = pl.BlockSpec(memory_space=pltpu.HBM)
_SEM = pl.BlockSpec(memory_space=pltpu.SEMAPHORE)
_DATAFLOW = pltpu.CompilerParams(has_side_effects=pltpu.SideEffectType.DATAFLOW_SIDE_EFFECTING)


def chip_exchange_start(s):
    def body(s_ref, land_ref, send, recv, s_thru, land_thru, token):
        x, y, c, chips = _place()
        for j, (cx, cy) in enumerate(chips):
            _rcopy(s_ref.at[2 * cx + cy], land_ref.at[2 * x + y], send.at[j], recv.at[j], (cx, cy, c)).start()
        token[...] = jnp.zeros_like(token)

    return pl.pallas_call(
        body, name='chip_exchange_start',
        out_shape=(pltpu.SemaphoreType.DMA((3,)), pltpu.SemaphoreType.DMA((3,)), pltpu.HBM(s.shape, s.dtype),
                   pltpu.HBM(s.shape, s.dtype), jax.ShapeDtypeStruct((8, 128), F32)),
        in_specs=(_HBM, _HBM), out_specs=(_SEM, _SEM, _HBM, _HBM, pl.BlockSpec(memory_space=pltpu.VMEM)),
        input_output_aliases={0: 2, 1: 3}, compiler_params=_DATAFLOW,
    )(pltpu.with_memory_space_constraint(s, pltpu.HBM),
      pltpu.with_memory_space_constraint(lax.empty(s.shape, s.dtype), pltpu.HBM))


def chip_exchange_wait(send, recv, s_thru, land_thru, after):
    def body(s_ref, land_ref, send_sem, recv_sem, after_ref, s_out, land_out):
        x, y, c, chips = _place()
        for j, (cx, cy) in enumerate(chips):
            cp = _rcopy(s_ref.at[2 * cx + cy], land_ref.at[2 * cx + cy], send_sem.at[j], recv_sem.at[j], (cx, cy, c))
            cp.wait_send()
            cp.wait_recv()

    return pl.pallas_call(
        body, name='chip_exchange_wait',
        out_shape=(pltpu.HBM(s_thru.shape, s_thru.dtype), pltpu.HBM(land_thru.shape, land_thru.dtype)),
        in_specs=(_HBM, _HBM, _SEM, _SEM, ANY), out_specs=(_HBM, _HBM),
        input_output_aliases={0: 0, 1: 1}, compiler_params=_DATAFLOW,
    )(s_thru, land_thru, send, recv, after)


def comm_call(name, stages):
    def body():
        pass

    return _call(body, name=name, grid=(1,), in_specs=[], out_specs=[], out_shape=[], args=[], stages=stages)[1]


def pair_sum(g4, land, c_arr, name):
    hr = g4.shape[2]

    def body(c_ref, g_ref, l_ref, o_ref):
        o_ref[0] = (g_ref[0, 0].astype(F32) + l_ref[0, 0].astype(F32)).astype(BF16)

    return pl.pallas_call(
        body, name=name,
        grid_spec=pltpu.PrefetchScalarGridSpec(
            num_scalar_prefetch=1, grid=(N_CHIPS,),
            in_specs=[pl.BlockSpec((1, 1, hr, D_MODEL), lambda q, c: (q, c[0], 0, 0)),
                      pl.BlockSpec((1, 1, hr, D_MODEL), lambda q, c: (q, 0, 0, 0))],
            out_specs=pl.BlockSpec((1, hr, D_MODEL), lambda q, c: (q, 0, 0))),
        out_shape=jax.ShapeDtypeStruct((N_CHIPS, hr, D_MODEL), BF16),
        compiler_params=_params(1),
    )(c_arr, g4, land)


def small_sum(vec_parts, lru_parts):
    def body(v_ref, l_ref, o_ref):
        for p_ref, lo, n in ((v_ref, 0, ROW_WA), (l_ref, ROW_WA, SMALL_ROWS - ROW_WA)):
            acc = p_ref[0]
            for s in range(1, N_DEV):
                acc = acc + p_ref[s]
            o_ref[lo:lo + n, :] = acc

    return pl.pallas_call(
        body, name='small_sum', grid=(1,),
        in_specs=[pl.BlockSpec(vec_parts.shape, lambda i: (0, 0, 0)), pl.BlockSpec(lru_parts.shape, lambda i: (0, 0, 0))],
        out_specs=pl.BlockSpec((SMALL_ROWS, D_MODEL), lambda i: (0, 0)),
        out_shape=jax.ShapeDtypeStruct((SMALL_ROWS, D_MODEL), F32),
        compiler_params=_params(1),
    )(vec_parts, lru_parts)


def _adam_math(w, g, m, v):
    m2 = ADAM_B1 * m + (1.0 - ADAM_B1) * g
    v2 = ADAM_B2 * v + (1.0 - ADAM_B2) * (g * g)
    m_hat = m2 / (1.0 - ADAM_B1 ** ADAM_STEP)
    v_hat = v2 / (1.0 - ADAM_B2 ** ADAM_STEP)
    delta = -ADAM_LR * (m_hat / (jnp.sqrt(v_hat) + ADAM_EPS) + ADAM_WD * w)
    return delta, m2, v2


def _adam_body(n_parts, transposed, n_after):
    def body(*refs):
        refs = refs[n_after:]
        g_refs = refs[:n_parts]
        w_ref, m_ref, v_ref, go_ref, d_ref, mo_ref, vo_ref = refs[n_parts:]
        def chips_added(blk):
            acc = blk[0].astype(F32)
            for s in range(1, N_CHIPS):
                acc = acc + blk[s].astype(F32)
            return acc

        if transposed:
            g = jnp.concatenate([chips_added(g_ref[h]) for h in range(2) for g_ref in g_refs], axis=0).T
        else:
            rows = [chips_added(g_ref[0]) for g_ref in g_refs]
            g = jnp.concatenate(rows, axis=0) if n_parts > 1 else rows[0]
        go_ref[...] = g
        d_ref[...], mo_ref[...], vo_ref[...] = _adam_math(w_ref[...], g, m_ref[...], v_ref[...])
    return body


def adam_rows(fulls, name, w, m, v, after=()):
    hr = w.shape[0] // 2
    blk = pl.BlockSpec((hr, D_MODEL), lambda h: (h, 0))
    return pl.pallas_call(
        _adam_body(len(fulls), False, len(after)), name='adam_' + name, grid=(2,),
        in_specs=[ANY] * len(after)
        + [pl.BlockSpec((1, N_CHIPS, f.shape[2], D_MODEL), lambda h: (h, 0, 0, 0)) for f in fulls] + [blk, blk, blk],
        out_specs=[blk] * 4,
        out_shape=[jax.ShapeDtypeStruct(w.shape, F32)] * 4,
        compiler_params=_params(1),
    )(*after, *fulls, w, m, v)


def adam_cols(fulls, name, w, m, v, after=()):
    cols = w.shape[1]
    tr = 128
    blk = pl.BlockSpec((tr, cols), lambda i: (i, 0))
    return pl.pallas_call(
        _adam_body(len(fulls), True, len(after)), name='adam_' + name, grid=(D_MODEL // tr,),
        in_specs=[ANY] * len(after)
        + [pl.BlockSpec((2, N_CHIPS, f.shape[2], tr), lambda i: (0, 0, 0, i)) for f in fulls] + [blk, blk, blk],
        out_specs=[blk] * 4,
        out_shape=[jax.ShapeDtypeStruct(w.shape, F32)] * 4,
        compiler_params=_params(1),
    )(*after, *fulls, w, m, v)


def adam_small(g, w, m, v):
    def body(g_ref, w_ref, m_ref, v_ref, d_ref, mo_ref, vo_ref):
        d_ref[...], mo_ref[...], vo_ref[...] = _adam_math(w_ref[...], g_ref[...], m_ref[...], v_ref[...])

    blk = pl.BlockSpec(w.shape, lambda i: (0, 0))
    return pl.pallas_call(
        body, name='adam_small', grid=(1,), in_specs=[blk] * 4, out_specs=[blk] * 3,
        out_shape=[jax.ShapeDtypeStruct(w.shape, F32)] * 3, compiler_params=_params(1),
    )(g, w, m, v)


WEIGHTS = ('ffn1_pre_g', 'ffn1_w_gu', 'ffn1_w_down', 'ffn1_post_g', 'mix_pre_g', 'w_in', 'conv_w', 'conv_b',
           'lru_w_a', 'lru_b_a', 'lru_w_x', 'lru_b_x', 'lru_lambda', 'attn_sinks', 'w_proj_lru', 'w_proj_attn',
           'w_out', 'mix_post_g', 'ffn2_pre_g', 'ffn2_w_gu', 'ffn2_w_down', 'ffn2_post_g')
SMALL = tuple(n for n in WEIGHTS if n not in PACK_OFF)


def _pack_vecs(d, conv_rows):
    sinks = jnp.pad(d['attn_sinks'].reshape(1, N_Q_HEADS), ((0, 0), (0, D_MODEL - N_Q_HEADS)))
    conv = jnp.pad(conv_rows, ((0, ROW_WA - ROW_CONV - conv_rows.shape[0]), (0, 0)))
    return jnp.concatenate([d[n].reshape(1, D_MODEL) for n in SMALL_VECS] + [sinks, conv], axis=0)


def _pack_lru(d):
    return jnp.concatenate([d['lru_w_a'].reshape(64, D_MODEL), d['lru_w_x'].reshape(64, D_MODEL)], axis=0)


def _pack_small(d, conv_rows):
    return jnp.concatenate([_pack_vecs(d, conv_rows), _pack_lru(d)], axis=0)


def _unpack_small(p, shapes):
    out = {n: p[k:k + 1].reshape(shapes[n]) for k, n in enumerate(SMALL_VECS)}
    out['attn_sinks'] = p[ROW_SINKS:ROW_SINKS + 1, :N_Q_HEADS].reshape(shapes['attn_sinks'])
    out['conv_w'] = p[ROW_CONV:ROW_CONV + 1].reshape(shapes['conv_w'])
    out['lru_w_a'] = p[ROW_WA:ROW_WA + 64].reshape(shapes['lru_w_a'])
    out['lru_w_x'] = p[ROW_WX:ROW_WX + 64].reshape(shapes['lru_w_x'])
    return out


def kernel(x, ffn1_pre_g, ffn1_w_gu, ffn1_w_down, ffn1_post_g, mix_pre_g, w_in, conv_w, conv_b, lru_w_a, lru_b_a, lru_w_x, lru_b_x, lru_lambda, attn_sinks, w_proj_lru, w_proj_attn, w_out, mix_post_g, ffn2_pre_g, ffn2_w_gu, ffn2_w_down, ffn2_post_g, loss_target, m_ffn1_pre_g, m_ffn1_w_gu, m_ffn1_w_down, m_ffn1_post_g, m_mix_pre_g, m_w_in, m_conv_w, m_conv_b, m_lru_w_a, m_lru_b_a, m_lru_w_x, m_lru_b_x, m_lru_lambda, m_attn_sinks, m_w_proj_lru, m_w_proj_attn, m_w_out, m_mix_post_g, m_ffn2_pre_g, m_ffn2_w_gu, m_ffn2_w_down, m_ffn2_post_g, v_ffn1_pre_g, v_ffn1_w_gu, v_ffn1_w_down, v_ffn1_post_g, v_mix_pre_g, v_w_in, v_conv_w, v_conv_b, v_lru_w_a, v_lru_b_a, v_lru_w_x, v_lru_b_x, v_lru_lambda, v_attn_sinks, v_w_proj_lru, v_w_proj_attn, v_w_out, v_mix_post_g, v_ffn2_pre_g, v_ffn2_w_gu, v_ffn2_w_down, v_ffn2_post_g):
    given = dict(locals())
    w = {n: given[n] for n in WEIGHTS}
    mom = {n: given['m_' + n] for n in WEIGHTS}
    var = {n: given['v_' + n] for n in WEIGHTS}
    shapes = {n: w[n].shape for n in WEIGHTS}
    xq = lax.axis_index('x')
    yq = lax.axis_index('y')
    cq = lax.axis_index('c')
    me_q = 2 * xq + yq

    c_arr = cq.reshape(1).astype(jnp.int32)
    xs, target = x[0], loss_target[0]
    sw = {n: (w[n][0] if w[n].ndim > 2 else w[n]) for n in SMALL}
    cos, sin_signed = _rope_tables()
    wa_bd = _block_diag(sw['lru_w_a'])
    wx_bd = _block_diag(sw['lru_w_x'])
    sinks = sw['attn_sinks'].reshape(N_Q_HEADS)

    shard = {n: (w[n][0].T if t else w[n][0]).astype(BF16) for n, _, t in PACK}
    conv_pad = jnp.pad(w['conv_w'][0], ((0, 4), (0, 0)))

    def whole(name):
        return (shard[name], 0, PACK_ROWS_OF[name])

    def part(name, p, n_parts=2):
        rows = PACK_ROWS_OF[name] // n_parts
        return (shard[name], p * rows, rows)

    (w_gu1,), (conv_all,) = comm_call('gather_first', [GatherStage([whole('ffn1_w_gu')]), SmallGatherStage(conv_pad)])
    sw['conv_w'] = jnp.transpose(conv_all[0::2, :4, :], (1, 0, 2)).reshape(4, LRU_W)
    proj_names = ['w_proj_lru', 'w_proj_attn', 'w_out']

    (n1, g1, u1, a1), ((w_down1,),) = ffn_fwd_a(xs, sw['ffn1_pre_g'], [w_gu1], 'ffn1_fwd_a',
                                                 stages=[GatherStage([whole('ffn1_w_down')])])
    (f1, h1), ((w_in_t,),) = ffn_fwd_b(a1, w_down1, sw['ffn1_post_g'], xs, 'ffn1_fwd_b', stages=[GatherStage([whole('w_in')])])
    (um, gate, xbr, q, k, v, g_lru, g_attn), ((w_gu2a,),) = mix_in(h1, sw['mix_pre_g'], w_in_t, 'mix_in',
                                                                   stages=[GatherStage([part('ffn2_w_gu', 0)])])
    (y_lru, h_lru), ((w_gu2b,),) = lru_fwd(gate, xbr, sw['conv_w'], sw['conv_b'], wa_bd, sw['lru_b_a'], wx_bd, sw['lru_b_x'],
                                           sw['lru_lambda'], 'lru_fwd', stages=[GatherStage([part('ffn2_w_gu', 1)])])
    (qr, kr, y_attn), (projs,) = attn_fwd(q, k, v, cos, sin_signed, sinks, 'attn_fwd',
                                          stages=[GatherStage([whole(n) for n in proj_names])])
    (p_l, p_a, merged, m, h2), ((w_down2,),) = merge_fwd(y_lru, y_attn, g_lru, g_attn, projs, sw['mix_post_g'], h1, 'merge_fwd',
                                                         stages=[GatherStage([whole('ffn2_w_down')])])
    w_gu2 = [w_gu2a, w_gu2b]
    (n2, g2, u2, a2), _ = ffn_fwd_a(h2, sw['ffn2_pre_g'], w_gu2, 'ffn2_fwd_a')
    (f2, dy, loss_blk), _ = ffn_fwd_b(a2, w_down2, sw['ffn2_post_g'], h2, 'ffn2_fwd_b', target=target)

    gs, full = {}, {}

    def pair_stage(names, grads):
        g4 = [g.reshape(N_CHIPS, 2, PACK_ROWS_OF[n] // 2, D_MODEL) for n, g in zip(names, grads)]
        return PairStage(g4), g4

    def pair_sums(names, g4, lands):
        return [pair_sum(g, l, c_arr, 'pair_sum_' + n) for n, g, l in zip(names, g4, lands)]

    def halves(s, n_parts=2):
        n = s.shape[1] // n_parts
        return [(s, p * n, n) for p in range(n_parts)]

    (df2, dgu2, gs['ffn2_post_g']), _ = ffn_bwd_a(dy, f2, sw['ffn2_post_g'], w_down2, g2, u2, 'ffn2_bwd_a')
    g_down2, _ = mm_tn([a2], df2, 1408, 'ffn2_dw_down')
    st, g4 = pair_stage(['ffn2_w_down'], [g_down2])
    g_gu2, (lands,) = mm_tn([dgu2], n2, 1408, 'ffn2_dw_gu', stages=[st])
    (s_down2,) = pair_sums(['ffn2_w_down'], g4, lands)
    st, g4 = pair_stage(['ffn2_w_gu'], [g_gu2])
    (dh2, gs['ffn2_pre_g']), ((l_down2,), lands) = norm_bwd([dgu2], w_gu2, h2, sw['ffn2_pre_g'], dy, 'ffn2_bwd_b',
                                                            stages=[ChipStage([(s_down2, 0, s_down2.shape[1])]), st])
    (s_gu2,) = pair_sums(['ffn2_w_gu'], g4, lands)

    (dm, dpl, dpa, dgl, dga, dya, dyl, gs['mix_post_g']), ((l_gu2a,),) = merge_bwd(
        dh2, m, sw['mix_post_g'], projs, g_lru, g_attn, p_l, p_a, 'merge_bwd', stages=[ChipStage(halves(s_gu2)[:1])])
    g_projs = [mm_tn([merged if n == 'w_out' else (y_lru if n == 'w_proj_lru' else y_attn)],
                     dm if n == 'w_out' else (dpl if n == 'w_proj_lru' else dpa), D_MODEL, 'd' + n)[0] for n in proj_names]
    st, g4 = pair_stage(proj_names, g_projs)
    (dq, dkv, dsk), ((l_gu2b,), lands, (full['ffn2_w_down'],)) = attn_bwd(
        qr, kr, v, dya, cos, sin_signed, sinks, 'attn_bwd', stages=[ChipStage(halves(s_gu2)[1:]), st, SwapStage([l_down2])])
    full['ffn2_w_down'] = [full['ffn2_w_down']]
    gs['attn_sinks'] = dsk[0:1, 0:N_Q_HEADS]
    s_projs = pair_sums(proj_names, g4, lands)
    (dgate, dxbr, vecs, dwa, dwx), (l_projs, full['ffn2_w_gu']) = lru_bwd(
        gate, xbr, h_lru, dyl, sw['conv_w'], sw['conv_b'], wa_bd, sw['lru_b_a'], wx_bd, sw['lru_b_x'], sw['lru_lambda'],
        'lru_bwd', stages=[ChipStage([(s, 0, s.shape[1]) for s in s_projs]), SwapStage([l_gu2a, l_gu2b])])
    gs['conv_w'] = vecs[0:4]
    gs['conv_b'], gs['lru_b_a'], gs['lru_b_x'], gs['lru_lambda'] = vecs[4:5], vecs[5:6], vecs[6:7], vecs[7:8]
    gs['lru_w_a'] = _diag_blocks(dwa)
    gs['lru_w_x'] = _diag_blocks(dwx)
    dz = [dgate, dxbr, dq, dkv, dgl, dga]
    g_in, ((lru_all,),) = mm_tn(dz, um, 512, 'dw_in', stages=[SmallGatherStage(_pack_lru(gs))])
    st, g4 = pair_stage(['w_in'], [g_in])
    (dh1, gs['mix_pre_g']), (lands, f_projs) = norm_bwd(dz, [w_in_t], h1, sw['mix_pre_g'], dh2, 'mix_bwd_in',
                                                        stages=[st, SwapStage(l_projs)])
    for n, f in zip(proj_names, f_projs):
        full[n] = [f]
    (s_in,) = pair_sums(['w_in'], g4, lands)

    (df1, dgu1, gs['ffn1_post_g']), ((l_in_a,),) = ffn_bwd_a(dh1, f1, sw['ffn1_post_g'], w_down1, g1, u1, 'ffn1_bwd_a',
                                                             stages=[ChipStage(halves(s_in)[:1])])
    g_down1, _ = mm_tn([a1], df1, 1408, 'ffn1_dw_down')
    st, g4 = pair_stage(['ffn1_w_down'], [g_down1])
    g_gu1, ((l_in_b,), lands) = mm_tn([dgu1], n1, 1408, 'ffn1_dw_gu', stages=[ChipStage(halves(s_in)[1:]), st])
    (s_down1,) = pair_sums(['ffn1_w_down'], g4, lands)
    st, g4 = pair_stage(['ffn1_w_gu'], [g_gu1])
    (dx, gs['ffn1_pre_g']), ((l_down1,), lands, full['w_in']) = norm_bwd(
        [dgu1], [w_gu1], xs, sw['ffn1_pre_g'], dh1, 'ffn1_bwd_b',
        stages=[ChipStage([(s_down1, 0, s_down1.shape[1])]), st, SwapStage([l_in_a, l_in_b])])
    (s_gu1,) = pair_sums(['ffn1_w_gu'], g4, lands)
    loss_row = jnp.pad(loss_blk[0:1], ((0, 0), (0, D_MODEL - loss_blk.shape[1])))
    vec_blk = _pack_vecs(gs, jnp.concatenate([gs['conv_w'], loss_row], axis=0))
    send, recv, s_thru, land_thru, token = chip_exchange_start(s_gu1)
    out_g, out_d, out_m, out_v = {}, {}, {}, {}

    def adam(n, after=()):
        fn = adam_cols if dict((k, t) for k, _, t in PACK)[n] else adam_rows
        g_, d_, m_, v_ = fn(full[n], n, w[n][0], mom[n][0], var[n][0], after=after)
        out_g[n], out_d[n], out_m[n], out_v[n] = g_[None], d_[None], m_[None], v_[None]

    behind = token
    for n in ['ffn2_w_gu', 'w_in', 'ffn2_w_down'] + proj_names:
        adam(n, after=(behind,))
        behind = out_v[n]
    s_back, l_gu1 = chip_exchange_wait(send, recv, s_thru, land_thru, after=behind)
    own = lax.dynamic_slice_in_dim(s_back, me_q, 1, axis=0)
    l_gu1 = lax.dynamic_update_slice_in_dim(l_gu1, own, me_q, axis=0)
    (vec_all,), (f_down1, f_gu1) = comm_call('swap_last', [SmallGatherStage(vec_blk), SwapStage([l_down1, l_gu1])])
    full['ffn1_w_down'] = [f_down1]
    full['ffn1_w_gu'] = [f_gu1]
    adam('ffn1_w_gu')
    adam('ffn1_w_down')

    tot = small_sum(vec_all, lru_all)
    loss = tot[ROW_WA - 1, 0]
    conv_g = lax.dynamic_slice(tot[ROW_CONV:ROW_CONV + 4], (0, me_q * (LRU_W // N_CHIPS)), (4, LRU_W // N_CHIPS))
    small_g = _unpack_small(tot, shapes)
    small_g['conv_w'] = conv_g.reshape(shapes['conv_w'])
    g_pack = jnp.concatenate([tot[:ROW_CONV], conv_g.reshape(1, D_MODEL), jnp.zeros((ROW_WA - ROW_CONV - 1, D_MODEL), F32),
                              tot[ROW_WA:]], axis=0)
    packs = [_pack_small({n: d[n] for n in SMALL}, d['conv_w'].reshape(1, D_MODEL)) for d in (w, mom, var)]
    d_p, m_p, v_p = adam_small(g_pack, *packs)
    for n in SMALL:
        out_g[n] = small_g[n]
    for dst, p in ((out_d, d_p), (out_m, m_p), (out_v, v_p)):
        dst.update(_unpack_small(p, shapes))

    return (loss, dx[None], *[out_g[n] for n in WEIGHTS], *[out_d[n] for n in WEIGHTS],
            *[out_m[n] for n in WEIGHTS], *[out_v[n] for n in WEIGHTS])
```

```python
import jax
import jax.numpy as jnp
import numpy as np
from jax import lax
from jax.experimental import pallas as pl
from jax.experimental.pallas import tpu as pltpu

F32 = jnp.float32
BF16 = jnp.bfloat16

SEQ = 2048
D_MODEL = 1024
D_FF = 2816
LRU_W = 1024
LRU_BLOCK_W = 64
HEAD_DIM = 64
N_Q_HEADS = 16
N_KV_HEADS = 4
KV_W = N_KV_HEADS * HEAD_DIM
ATTN_BLOCK = 128
N_ATTN_BLOCKS = SEQ // ATTN_BLOCK
IN_SEGS = (1024, 1024, 1024, 256, 256, 1024, 1024)
IN_W = sum(IN_SEGS)
NORM_EPS = 1e-6
MASK_VALUE = -1e30
ROPE_THETA = 10000.0
LRU_C = 8.0
MACARON = 0.5
ADAM_LR = 0.001
ADAM_B1 = 0.9
ADAM_B2 = 0.999
ADAM_EPS = 1e-08
ADAM_WD = 0.01
ADAM_STEP = 10

N_CHIPS = 4
N_DEV = 8
VMEM_LIMIT = 56 * 1024 * 1024
MM_ROWS = 256
MESH = pl.DeviceIdType.MESH
ANY = pl.BlockSpec(memory_space=pl.ANY)

PACK = (('ffn1_w_gu', 1408, True), ('w_in', 1408, True), ('ffn2_w_gu', 1408, True),
        ('ffn1_w_down', 704, False), ('ffn2_w_down', 704, False),
        ('w_proj_lru', 256, False), ('w_proj_attn', 256, False), ('w_out', 256, False))
PACK_ROWS_OF = {n: r for n, r, _ in PACK}
PACK_OFF = {}
_o = 0
for _n, _r, _t in PACK:
    PACK_OFF[_n] = _o
    _o += _r

SMALL_VECS = ('ffn1_pre_g', 'ffn1_post_g', 'mix_pre_g', 'conv_b', 'lru_b_a', 'lru_b_x', 'lru_lambda',
              'mix_post_g', 'ffn2_pre_g', 'ffn2_post_g')
SMALL_ROWS = 144
ROW_SINKS, ROW_CONV, ROW_WA, ROW_WX = 10, 11, 16, 80


def _dot(a, b):
    return jnp.dot(a, b, preferred_element_type=F32)


def _dot_nt(a, b):
    return lax.dot_general(a, b, (((1,), (1,)), ((), ())), preferred_element_type=F32)


def _dot_tn(a, b):
    return lax.dot_general(a, b, (((0,), (0,)), ((), ())), preferred_element_type=F32)


def _params(n_grid):
    return pltpu.CompilerParams(dimension_semantics=("arbitrary",) * n_grid, vmem_limit_bytes=VMEM_LIMIT)


def _sigmoid(x):
    return 1.0 / (1.0 + jnp.exp(-x))


def _rsqrt_mean_sq(x):
    return lax.rsqrt(jnp.mean(x * x, axis=-1, keepdims=True) + NORM_EPS)


def _expm1(x):
    poly = x * (1.0 + x * (0.5 + x * (1.0 / 6.0)))
    return jnp.where(jnp.abs(x) < 0.02, poly, jnp.exp(x) - 1.0)


_GELU_K = 0.7978845608028654
_GELU_C = 0.044715


def _gelu(x):
    t = jnp.tanh(_GELU_K * (x + _GELU_C * x * x * x))
    return 0.5 * x * (1.0 + t), t


def _gelu_grad(x, t):
    return 0.5 * (1.0 + t) + 0.5 * x * (1.0 - t * t) * _GELU_K * (1.0 + 3.0 * _GELU_C * x * x)


def _load_weight(w_refs, dst_ref, sem):
    w_refs = list(w_refs) if isinstance(w_refs, (list, tuple)) else [w_refs]
    rows = dst_ref.shape[0] // N_CHIPS
    rp = rows // len(w_refs)
    cps = [pltpu.make_async_copy(w_ref.at[q], dst_ref.at[pl.ds(q * rows + p * rp, rp)], sem.at[p * N_CHIPS + q])
           for p, w_ref in enumerate(w_refs) for q in range(N_CHIPS)]
    for cp in cps:
        cp.start()
    for cp in cps:
        cp.wait()


def _weight_scratch(rows_total, parts=1):
    return [pltpu.VMEM((rows_total, D_MODEL), BF16), pltpu.SemaphoreType.DMA((N_CHIPS * parts,))]


_ROW = lambda tm: pl.BlockSpec((tm, D_MODEL), lambda i: (i, 0))
_VEC = pl.BlockSpec((1, D_MODEL), lambda i: (0, 0))


def _call(body, *, name, grid, in_specs, out_specs, out_shape, args, scratch_shapes=(), stages=()):
    in_specs, out_specs, out_shape, scratch_shapes = list(in_specs), list(out_specs), list(out_shape), list(scratch_shapes)
    n_in, n_out, n_sc = len(in_specs), len(out_specs), len(scratch_shapes)
    k_in = [len(s.inputs) for s in stages]
    k_out = [len(s.out_shape) for s in stages]
    k_sc = [len(s.scratch) for s in stages]
    last = grid[0] - 1

    def split(refs, counts):
        parts, pos = [], 0
        for k in counts:
            parts.append(refs[pos:pos + k])
            pos += k
        return parts

    kinds = tuple(sorted({k for s in stages for k in s.peers}))
    collective_id = {(): None, ('sib',): 0, ('chips',): 1, ('chips', 'sib'): 2}[kinds]

    def full(*refs):
        ins, s_ins, outs, s_outs, scr, s_scr = split(refs, [n_in, sum(k_in), n_out, sum(k_out), n_sc, sum(k_sc)])
        per_stage = list(zip(stages, split(s_ins, k_in), split(s_outs, k_out), split(s_scr, k_sc)))
        i = pl.program_id(0)
        if stages:
            @pl.when(i == 0)
            def _():
                x, y, c, chips = _place()
                peers = ([(x, y, 1 - c)] if 'sib' in kinds else []) + ([(cx, cy, c) for cx, cy in chips] if 'chips' in kinds else [])
                barrier = pltpu.get_barrier_semaphore()
                for peer in peers:
                    pl.semaphore_signal(barrier, inc=1, device_id=peer, device_id_type=MESH)
                pl.semaphore_wait(barrier, len(peers))
                for s, a, b, c_ in per_stage:
                    s.start(a, b, c_)

        body(*ins, *outs, *scr)
        if stages:
            @pl.when(i == last // 2)
            def _():
                for s, a, b, c in per_stage:
                    s.relay(a, b, c)

            @pl.when(i == max(last - 1, 0))
            def _():
                for s, a, b, c in per_stage:
                    s.mid(a, b, c)

            @pl.when(i == last)
            def _():
                for s, a, b, c in per_stage:
                    s.end(a, b, c)

    res = pl.pallas_call(
        full, name=name, grid=grid,
        in_specs=in_specs + [ANY] * sum(k_in),
        out_specs=out_specs + [ANY] * sum(k_out),
        out_shape=out_shape + [o for s in stages for o in s.out_shape],
        scratch_shapes=scratch_shapes + [x for s in stages for x in s.scratch],
        compiler_params=pltpu.CompilerParams(dimension_semantics=("arbitrary",), vmem_limit_bytes=VMEM_LIMIT,
                                             collective_id=collective_id),
    )(*args, *[a for s in stages for a in s.inputs])
    return list(res[:n_out]), split(list(res[n_out:]), k_out)


def ffn_fwd_a(x, g_pre, w_gu_t, name, stages=()):
    tm, tn = MM_ROWS, 256
    n_w = len(w_gu_t)

    def body(x_ref, gp_ref, *refs):
        w_refs = refs[:n_w]
        n_ref, g_ref, u_ref, a_ref, wt_ref, sem = refs[n_w:]

        @pl.when(pl.program_id(0) == 0)
        def _():
            _load_weight(w_refs, wt_ref, sem)

        xv = x_ref[...]
        n = (xv * _rsqrt_mean_sq(xv) * gp_ref[...]).astype(BF16)
        n_ref[...] = n
        for j in range(D_FF // tn):
            g = _dot_nt(n, wt_ref[j * tn:(j + 1) * tn, :])
            u = _dot_nt(n, wt_ref[D_FF + j * tn:D_FF + (j + 1) * tn, :])
            g_ref[:, j * tn:(j + 1) * tn] = g.astype(BF16)
            u_ref[:, j * tn:(j + 1) * tn] = u.astype(BF16)
            a_ref[:, j * tn:(j + 1) * tn] = (g * _sigmoid(g) * u).astype(BF16)

    wide = pl.BlockSpec((tm, D_FF), lambda i: (i, 0))
    return _call(
        body, name=name, grid=(SEQ // tm,),
        in_specs=[_ROW(tm), _VEC] + [ANY] * n_w,
        out_specs=[_ROW(tm), wide, wide, wide],
        out_shape=[jax.ShapeDtypeStruct((SEQ, D_MODEL), BF16)] + [jax.ShapeDtypeStruct((SEQ, D_FF), BF16)] * 3,
        scratch_shapes=_weight_scratch(2 * D_FF, n_w),
        args=[x, g_pre, *w_gu_t], stages=stages)


def ffn_fwd_b(a, w_down, g_post, h_in, name, target=None, stages=()):
    tm = MM_ROWS
    final = target is not None

    def body(*refs):
        if final:
            a_ref, wf_ref, gp_ref, h_ref, t_ref, f_ref, o_ref, loss_ref, wd_ref, sem = refs
        else:
            a_ref, wf_ref, gp_ref, h_ref, f_ref, o_ref, wd_ref, sem = refs

        @pl.when(pl.program_id(0) == 0)
        def _():
            _load_weight(wf_ref, wd_ref, sem)
            if final:
                loss_ref[...] = jnp.zeros_like(loss_ref)

        f = _dot(a_ref[...], wd_ref[...])
        f_ref[...] = f
        y = h_ref[...] + MACARON * (f * _rsqrt_mean_sq(f) * gp_ref[...])
        if final:
            err = y - t_ref[...]
            o_ref[...] = err * (1.0 / D_MODEL)
            loss_ref[...] += 0.5 * jnp.sum(err * err) * (1.0 / D_MODEL)
        else:
            o_ref[...] = y

    row = _ROW(tm)
    in_specs = [pl.BlockSpec((tm, D_FF), lambda i: (i, 0)), ANY, _VEC, row]
    out_specs = [row, row]
    out_shape = [jax.ShapeDtypeStruct((SEQ, D_MODEL), F32)] * 2
    args = [a, w_down, g_post, h_in]
    if final:
        in_specs.append(row)
        args.append(target)
        out_specs.append(pl.BlockSpec((8, 128), lambda i: (0, 0)))
        out_shape.append(jax.ShapeDtypeStruct((8, 128), F32))
    return _call(body, name=name, grid=(SEQ // tm,), in_specs=in_specs, out_specs=out_specs,
                 out_shape=out_shape, scratch_shapes=_weight_scratch(D_FF), args=args, stages=stages)


def ffn_bwd_a(d_out, f, g_post, w_down, g, u, name, stages=()):
    tm = MM_ROWS
    tc = 256

    def body(do_ref, f_ref, gp_ref, wf_ref, g_ref, u_ref, df_ref, dgu_ref, dgp_ref, wd_ref, sem):
        @pl.when(pl.program_id(0) == 0)
        def _():
            _load_weight(wf_ref, wd_ref, sem)
            dgp_ref[...] = jnp.zeros_like(dgp_ref)

        fv = f_ref[...]
        rf = _rsqrt_mean_sq(fv)
        fh = fv * rf
        dn = MACARON * do_ref[...]
        dgp_ref[...] += jnp.sum(dn * fh, axis=0, keepdims=True)
        t = dn * gp_ref[...]
        df = (rf * (t - fh * jnp.mean(t * fh, axis=-1, keepdims=True))).astype(BF16)
        df_ref[...] = df
        for c0 in range(0, D_FF, tc):
            da = _dot_nt(df, wd_ref[c0:c0 + tc, :])
            gv = g_ref[:, c0:c0 + tc].astype(F32)
            uv = u_ref[:, c0:c0 + tc].astype(F32)
            s = _sigmoid(gv)
            dgu_ref[:, c0:c0 + tc] = (da * uv * s * (1.0 + gv * (1.0 - s))).astype(BF16)
            dgu_ref[:, D_FF + c0:D_FF + c0 + tc] = (da * gv * s).astype(BF16)

    row = _ROW(tm)
    wide = pl.BlockSpec((tm, D_FF), lambda i: (i, 0))
    return _call(
        body, name=name, grid=(SEQ // tm,),
        in_specs=[row, row, _VEC, ANY, wide, wide],
        out_specs=[row, pl.BlockSpec((tm, 2 * D_FF), lambda i: (i, 0)), _VEC],
        out_shape=[jax.ShapeDtypeStruct((SEQ, D_MODEL), BF16), jax.ShapeDtypeStruct((SEQ, 2 * D_FF), BF16),
                   jax.ShapeDtypeStruct((1, D_MODEL), F32)],
        scratch_shapes=_weight_scratch(D_FF),
        args=[d_out, f, g_post, w_down, g, u], stages=stages)


def norm_bwd(pieces, w_t, x, g_pre, d_res, name, stages=()):
    tm = MM_ROWS
    widths = [p.shape[1] for p in pieces]
    offs = [sum(widths[:k]) for k in range(len(widths))]
    n_p = len(pieces)
    n_w = len(w_t)

    def body(*refs):
        p_refs = refs[:n_p]
        w_refs = refs[n_p:n_p + n_w]
        x_ref, g_ref, r_ref, dx_ref, dg_ref, wt_ref, sem = refs[n_p + n_w:]

        @pl.when(pl.program_id(0) == 0)
        def _():
            _load_weight(w_refs, wt_ref, sem)
            dg_ref[...] = jnp.zeros_like(dg_ref)

        dn = None
        for p_ref, lo, wd in zip(p_refs, offs, widths):
            part = _dot(p_ref[...], wt_ref[lo:lo + wd, :])
            dn = part if dn is None else dn + part
        xv = x_ref[...]
        r = _rsqrt_mean_sq(xv)
        xh = xv * r
        dg_ref[...] += jnp.sum(dn * xh, axis=0, keepdims=True)
        t = dn * g_ref[...]
        dx_ref[...] = r_ref[...] + r * (t - xh * jnp.mean(t * xh, axis=-1, keepdims=True))

    row = _ROW(tm)
    return _call(
        body, name=name, grid=(SEQ // tm,),
        in_specs=[pl.BlockSpec((tm, wd), lambda i: (i, 0)) for wd in widths] + [ANY] * n_w + [row, _VEC, row],
        out_specs=[row, _VEC],
        out_shape=[jax.ShapeDtypeStruct((SEQ, D_MODEL), F32), jax.ShapeDtypeStruct((1, D_MODEL), F32)],
        scratch_shapes=_weight_scratch(sum(widths), n_w),
        args=[*pieces, *w_t, x, g_pre, d_res], stages=stages)


def mm_tn(pieces, b, tm, name, stages=()):
    widths = [p.shape[1] for p in pieces]
    m_total = sum(widths)
    n_p = len(pieces)
    starts = [sum(widths[:k]) // tm for k in range(n_p)]
    counts = [wd // tm for wd in widths]

    def body(*refs):
        p_refs = refs[:n_p]
        b_ref, o_ref = refs[n_p:]
        i = pl.program_id(0)
        for p_ref, st, ct in zip(p_refs, starts, counts):
            @pl.when((i >= st) & (i < st + ct))
            def _(p_ref=p_ref):
                o_ref[...] = _dot_tn(p_ref[...], b_ref[...]).astype(BF16)

    def piece_spec(st, ct):
        return pl.BlockSpec((SEQ, tm), lambda i: (0, jnp.clip(i - st, 0, ct - 1)))

    (out,), stage_out = _call(
        body, name=name, grid=(m_total // tm,),
        in_specs=[piece_spec(st, ct) for st, ct in zip(starts, counts)] + [pl.BlockSpec((SEQ, D_MODEL), lambda i: (0, 0))],
        out_specs=[pl.BlockSpec((tm, D_MODEL), lambda i: (i, 0))],
        out_shape=[jax.ShapeDtypeStruct((m_total, D_MODEL), BF16)],
        args=[*pieces, b], stages=stages)
    return out, stage_out


def mix_in(h, g_pre, w_in_t, name, stages=()):
    tm = MM_ROWS
    offs = [sum(IN_SEGS[:k]) for k in range(len(IN_SEGS))]
    dts = [F32, F32, F32, F32, BF16, F32, F32]
    n_o = len(IN_SEGS)

    def body(*refs):
        h_ref, g_ref, wf_ref, um_ref = refs[:4]
        o_refs = refs[4:4 + n_o]
        wt_ref, sem = refs[4 + n_o:]

        @pl.when(pl.program_id(0) == 0)
        def _():
            _load_weight(wf_ref, wt_ref, sem)

        hv = h_ref[...]
        um = (hv * _rsqrt_mean_sq(hv) * g_ref[...]).astype(BF16)
        um_ref[...] = um
        for o_ref, lo, wd in zip(o_refs, offs, IN_SEGS):
            for c0 in range(0, wd, 256):
                o_ref[:, c0:c0 + 256] = _dot_nt(um, wt_ref[lo + c0:lo + c0 + 256, :]).astype(o_ref.dtype)

    return _call(
        body, name=name, grid=(SEQ // tm,),
        in_specs=[_ROW(tm), _VEC, ANY],
        out_specs=[_ROW(tm)] + [pl.BlockSpec((tm, wd), lambda i: (i, 0)) for wd in IN_SEGS],
        out_shape=[jax.ShapeDtypeStruct((SEQ, D_MODEL), BF16)]
        + [jax.ShapeDtypeStruct((SEQ, wd), dt) for wd, dt in zip(IN_SEGS, dts)],
        scratch_shapes=_weight_scratch(IN_W),
        args=[h, g_pre, w_in_t], stages=stages)


LRU_TC = 256


def _conv_fwd(xb, cw, cb, tt):
    xc = xb * cw[3:4, :] + cb
    shifted = []
    for s in (1, 2, 3):
        sh = jnp.where(tt >= s, pltpu.roll(xb, s, 0), 0.0)
        shifted.append(sh)
        xc = xc + sh * cw[3 - s:4 - s, :]
    return xc, shifted


def _lru_gates(xc, wa, ba, wx, bx, lam):
    xcb = xc.astype(BF16)
    r = _sigmoid(_dot(xcb, wa) + ba)
    i = _sigmoid(_dot(xcb, wx) + bx)
    nl = -lam
    sp = jnp.maximum(nl, 0.0) + jnp.log1p(jnp.exp(-jnp.abs(nl)))
    la = (-LRU_C * r) * sp
    a = jnp.exp(la)
    mult = jnp.sqrt(jnp.maximum(-_expm1(2.0 * la), 0.0))
    return xcb, r, i, sp, a, mult


def _scan(a, b, tt, reverse):
    n = a.shape[0]
    s = 1
    while s < n:
        more = 2 * s < n
        if s < 8:
            if reverse:
                keep = tt < n - s
                shift = n - s
            else:
                keep = tt >= s
                shift = s
            b = a * jnp.where(keep, pltpu.roll(b, shift, 0), 0.0) + b
            if more:
                a = a * jnp.where(keep, pltpu.roll(a, shift, 0), 1.0)
        elif reverse:
            b = jnp.concatenate([a[:n - s] * b[s:] + b[:n - s], b[n - s:]], axis=0)
            if more:
                a = jnp.concatenate([a[:n - s] * a[s:], a[n - s:]], axis=0)
        else:
            b = jnp.concatenate([b[:s], a[s:] * b[:n - s] + b[s:]], axis=0)
            if more:
                a = jnp.concatenate([a[:s], a[s:] * a[:n - s]], axis=0)
        s *= 2
    return b


def _lru_specs():
    col = pl.BlockSpec((SEQ, LRU_TC), lambda j: (0, j))
    vec = pl.BlockSpec((1, LRU_TC), lambda j: (0, j))
    bd = pl.BlockSpec((1, LRU_TC, LRU_TC), lambda j: (j, 0, 0))
    cw = pl.BlockSpec((4, LRU_TC), lambda j: (0, j))
    return col, vec, bd, cw


def lru_fwd(gate, xbr, conv_w, conv_b, wa_bd, b_a, wx_bd, b_x, lam, name, stages=()):
    col, vec, bd, cw = _lru_specs()

    def body(gate_ref, xbr_ref, cw_ref, cb_ref, wa_ref, ba_ref, wx_ref, bx_ref, lam_ref, y_ref, h_ref):
        tt = lax.broadcasted_iota(jnp.int32, (SEQ, LRU_TC), 0)
        xc, _ = _conv_fwd(xbr_ref[...], cw_ref[...], cb_ref[...], tt)
        _, r, i, sp, a, mult = _lru_gates(xc, wa_ref[0], ba_ref[...], wx_ref[0], bx_ref[...], lam_ref[...])
        h = _scan(a, mult * (i * xc), tt, reverse=False)
        h_ref[...] = h
        gl, _ = _gelu(gate_ref[...])
        y_ref[...] = (h * gl).astype(BF16)

    return _call(
        body, name=name, grid=(LRU_W // LRU_TC,),
        in_specs=[col, col, cw, vec, bd, vec, bd, vec, vec],
        out_specs=[col, col],
        out_shape=[jax.ShapeDtypeStruct((SEQ, LRU_W), BF16), jax.ShapeDtypeStruct((SEQ, LRU_W), F32)],
        args=[gate, xbr, conv_w, conv_b, wa_bd, b_a, wx_bd, b_x, lam], stages=stages)


def lru_bwd(gate, xbr, h, dy, conv_w, conv_b, wa_bd, b_a, wx_bd, b_x, lam, name, stages=()):
    col, vec, bd, cw = _lru_specs()

    def body(gate_ref, xbr_ref, h_ref, dy_ref, cw_ref, cb_ref, wa_ref, ba_ref, wx_ref, bx_ref, lam_ref,
             dgate_ref, dxbr_ref, vecs_ref, dwa_ref, dwx_ref):
        tt = lax.broadcasted_iota(jnp.int32, (SEQ, LRU_TC), 0)
        cwv = cw_ref[...]
        lam = lam_ref[...]
        xb = xbr_ref[...]
        xc, shifted = _conv_fwd(xb, cwv, cb_ref[...], tt)
        wa = wa_ref[0]
        wx = wx_ref[0]
        xcb, r, i, sp, a, mult = _lru_gates(xc, wa, ba_ref[...], wx, bx_ref[...], lam)
        hv = h_ref[...]
        dyv = dy_ref[...]
        gv = gate_ref[...]
        gl, th = _gelu(gv)
        dgate_ref[...] = (dyv * hv * _gelu_grad(gv, th)).astype(BF16)
        a_next = jnp.where(tt < SEQ - 1, pltpu.roll(a, SEQ - 1, 0), 0.0)
        gsum = _scan(a_next, dyv * gl, tt, reverse=True)
        h_prev = jnp.where(tt >= 1, pltpu.roll(hv, 1, 0), 0.0)
        d_mult = gsum * i * xc
        d_i = gsum * mult * xc
        d_xc = gsum * mult * i
        d_la = gsum * h_prev * a - d_mult * (a * a) / mult
        d_pr = (d_la * (-LRU_C * sp)) * r * (1.0 - r)
        d_pi = d_i * i * (1.0 - i)
        d_lam = jnp.sum(d_la * r, axis=0, keepdims=True) * (LRU_C * _sigmoid(-lam))
        d_prb = d_pr.astype(BF16)
        d_pib = d_pi.astype(BF16)
        d_xc = d_xc + _dot_nt(d_prb, wa) + _dot_nt(d_pib, wx)
        dwa_ref[0] = _dot_tn(xcb, d_prb)
        dwx_ref[0] = _dot_tn(xcb, d_pib)
        rows = [jnp.sum(d_xc * shifted[2], axis=0, keepdims=True),
                jnp.sum(d_xc * shifted[1], axis=0, keepdims=True),
                jnp.sum(d_xc * shifted[0], axis=0, keepdims=True),
                jnp.sum(d_xc * xb, axis=0, keepdims=True),
                jnp.sum(d_xc, axis=0, keepdims=True),
                jnp.sum(d_pr, axis=0, keepdims=True),
                jnp.sum(d_pi, axis=0, keepdims=True),
                d_lam]
        ri = lax.broadcasted_iota(jnp.int32, (8, LRU_TC), 0)
        acc = jnp.zeros((8, LRU_TC), F32)
        for k, rv in enumerate(rows):
            acc = jnp.where(ri == k, rv, acc)
        vecs_ref[...] = acc
        d_xb = d_xc * cwv[3:4, :]
        for s in (1, 2, 3):
            d_xb = d_xb + jnp.where(tt < SEQ - s, pltpu.roll(d_xc, SEQ - s, 0), 0.0) * cwv[3 - s:4 - s, :]
        dxbr_ref[...] = d_xb.astype(BF16)

    return _call(
        body, name=name, grid=(LRU_W // LRU_TC,),
        in_specs=[col, col, col, col, cw, vec, bd, vec, bd, vec, vec],
        out_specs=[col, col, pl.BlockSpec((8, LRU_TC), lambda j: (0, j)), bd, bd],
        out_shape=[jax.ShapeDtypeStruct((SEQ, LRU_W), BF16), jax.ShapeDtypeStruct((SEQ, LRU_W), BF16),
                   jax.ShapeDtypeStruct((8, LRU_W), F32),
                   jax.ShapeDtypeStruct((LRU_W // LRU_TC, LRU_TC, LRU_TC), F32),
                   jax.ShapeDtypeStruct((LRU_W // LRU_TC, LRU_TC, LRU_TC), F32)],
        args=[gate, xbr, h, dy, conv_w, conv_b, wa_bd, b_a, wx_bd, b_x, lam], stages=stages)


def _rope(x, cos, sin_signed):
    w = x.shape[1]
    reps = w // 128
    if reps > 1:
        cos = jnp.tile(cos, (1, reps))
        sin_signed = jnp.tile(sin_signed, (1, reps))
    lane = lax.broadcasted_iota(jnp.int32, x.shape, 1)
    first = (lane & 63) < 32
    partner = jnp.where(first, pltpu.roll(x, w - 32, 1), pltpu.roll(x, 32, 1))
    return x * cos + partner * sin_signed


def _both_halves(t, odd):
    lo = lax.broadcasted_iota(jnp.int32, t.shape, 1) < 64
    rolled = pltpu.roll(t, 64, 1)
    return jnp.where(lo, rolled, t) if odd else jnp.where(lo, t, rolled)


def _stack_heads(ta, tb):
    lo = lax.broadcasted_iota(jnp.int32, ta.shape, 1) < 64
    return jnp.concatenate([jnp.where(lo, ta, 0.0), jnp.where(lo, 0.0, ta),
                            jnp.where(lo, tb, 0.0), jnp.where(lo, 0.0, tb)], axis=0)


def _unstack_heads(o):
    lo = lax.broadcasted_iota(jnp.int32, (ATTN_BLOCK, 128), 1) < 64
    return (jnp.where(lo, o[0:128], o[128:256]), jnp.where(lo, o[256:384], o[384:512]))


def _window_upper_t():
    shape = (ATTN_BLOCK, 4 * ATTN_BLOCK)
    return lax.broadcasted_iota(jnp.int32, shape, 0) > (lax.broadcasted_iota(jnp.int32, shape, 1) & (ATTN_BLOCK - 1))


def _fold_t(t, upper_t):
    return jnp.where(upper_t, t[:ATTN_BLOCK], t[ATTN_BLOCK:])


def _unfold_t(t, upper_t):
    zero = jnp.zeros_like(t)
    return jnp.concatenate([jnp.where(upper_t, t, zero), jnp.where(upper_t, zero, t)], axis=0)


def _attn_probs_t(kd, qs, sinks_ref, hk, first_block, upper_t):
    s = _fold_t(_dot_nt(kd, qs), upper_t) * (HEAD_DIM ** -0.5)
    s = jnp.where(jnp.logical_and(upper_t, first_block), MASK_VALUE, s)
    rg = lax.broadcasted_iota(jnp.int32, (1, 4 * ATTN_BLOCK), 1) >> 7
    sink = jnp.where(rg == 0, sinks_ref[4 * hk],
                     jnp.where(rg == 1, sinks_ref[4 * hk + 1],
                               jnp.where(rg == 2, sinks_ref[4 * hk + 2], sinks_ref[4 * hk + 3])))
    m = jnp.maximum(jnp.max(s, axis=0, keepdims=True), sink)
    e = jnp.exp(s - m)
    es = jnp.exp(sink - m)
    inv = 1.0 / (jnp.sum(e, axis=0, keepdims=True) + es)
    return e * inv, es * inv


def _prev(i):
    return jnp.maximum(i - 1, 0)


def attn_fwd(q, k, v, cos, sin_signed, sinks, name, stages=()):
    nb = ATTN_BLOCK

    def body(q_ref, kc_ref, kp_ref, vc_ref, vp_ref, cc_ref, sc_ref, cp_ref, sp_ref, sinks_ref,
             qr_ref, kr_ref, y_ref):
        first_block = pl.program_id(0) == 0
        qr = _rope(q_ref[...], cc_ref[...], sc_ref[...])
        kc = _rope(kc_ref[...], cc_ref[...], sc_ref[...])
        kp = _rope(kp_ref[...], cp_ref[...], sp_ref[...])
        qr_ref[...] = qr.astype(BF16)
        kr_ref[...] = kc.astype(BF16)
        k2 = jnp.concatenate([kp, kc], axis=0)
        v2 = jnp.concatenate([vp_ref[...].astype(F32), vc_ref[...].astype(F32)], axis=0)
        upper_t = _window_upper_t()
        for hk in range(N_KV_HEADS):
            kt = hk // 2
            kd = _both_halves(k2[:, kt * 128:(kt + 1) * 128], hk % 2).astype(BF16)
            vd = _both_halves(v2[:, kt * 128:(kt + 1) * 128], hk % 2).astype(BF16)
            qs = _stack_heads(qr[:, (2 * hk) * 128:(2 * hk + 1) * 128],
                              qr[:, (2 * hk + 1) * 128:(2 * hk + 2) * 128]).astype(BF16)
            p, _ = _attn_probs_t(kd, qs, sinks_ref, hk, first_block, upper_t)
            ta, tb = _unstack_heads(_dot_tn(_unfold_t(p.astype(BF16), upper_t), vd))
            y_ref[:, (2 * hk) * 128:(2 * hk + 1) * 128] = ta.astype(BF16)
            y_ref[:, (2 * hk + 1) * 128:(2 * hk + 2) * 128] = tb.astype(BF16)

    cur = lambda w: pl.BlockSpec((nb, w), lambda i: (i, 0))
    prv = lambda w: pl.BlockSpec((nb, w), lambda i: (_prev(i), 0))
    return _call(
        body, name=name, grid=(N_ATTN_BLOCKS,),
        in_specs=[cur(D_MODEL), cur(KV_W), prv(KV_W), cur(KV_W), prv(KV_W), cur(128), cur(128), prv(128), prv(128),
                  pl.BlockSpec(memory_space=pltpu.SMEM)],
        out_specs=[cur(D_MODEL), cur(KV_W), cur(D_MODEL)],
        out_shape=[jax.ShapeDtypeStruct((SEQ, D_MODEL), BF16), jax.ShapeDtypeStruct((SEQ, KV_W), BF16),
                   jax.ShapeDtypeStruct((SEQ, D_MODEL), BF16)],
        args=[q, k, k, v, v, cos, sin_signed, cos, sin_signed, sinks], stages=stages)


def attn_bwd(qr, kr, v, dy, cos, sin_signed, sinks, name, stages=()):
    nb = ATTN_BLOCK
    n_steps = N_ATTN_BLOCKS + 1
    scale = HEAD_DIM ** -0.5

    def body(q_ref, kc_ref, kp_ref, vc_ref, vp_ref, dy_ref, cc_ref, sc_ref, cp_ref, sp_ref, sinks_ref,
             dq_ref, dkv_ref, dsk_ref, ck_ref, cv_ref):
        dk_ref = dkv_ref.at[:, pl.ds(0, KV_W)]
        dv_ref = dkv_ref.at[:, pl.ds(KV_W, KV_W)]
        i = pl.program_id(0)

        @pl.when(i == 0)
        def _():
            dsk_ref[...] = jnp.zeros_like(dsk_ref)
            ck_ref[...] = jnp.zeros_like(ck_ref)
            cv_ref[...] = jnp.zeros_like(cv_ref)

        @pl.when(i < N_ATTN_BLOCKS)
        def _():
            qv = q_ref[...].astype(F32)
            dov = dy_ref[...].astype(F32)
            k2 = jnp.concatenate([kp_ref[...].astype(F32), kc_ref[...].astype(F32)], axis=0)
            v2 = jnp.concatenate([vp_ref[...].astype(F32), vc_ref[...].astype(F32)], axis=0)
            lane = lax.broadcasted_iota(jnp.int32, (8, 128), 1)
            lo = lax.broadcasted_iota(jnp.int32, (2 * nb, 128), 1) < 64
            dsk = jnp.zeros((8, 128), F32)
            dk_tiles = []
            dv_tiles = []
            upper_t = _window_upper_t()
            for hk in range(N_KV_HEADS):
                kt = hk // 2
                kd = _both_halves(k2[:, kt * 128:(kt + 1) * 128], hk % 2).astype(BF16)
                vd = _both_halves(v2[:, kt * 128:(kt + 1) * 128], hk % 2).astype(BF16)
                qs = _stack_heads(qv[:, (2 * hk) * 128:(2 * hk + 1) * 128],
                                  qv[:, (2 * hk + 1) * 128:(2 * hk + 2) * 128]).astype(BF16)
                dos = _stack_heads(dov[:, (2 * hk) * 128:(2 * hk + 1) * 128],
                                   dov[:, (2 * hk + 1) * 128:(2 * hk + 2) * 128]).astype(BF16)
                p, ps = _attn_probs_t(kd, qs, sinks_ref, hk, i == 0, upper_t)
                dp = _fold_t(_dot_nt(vd, dos), upper_t)
                delta = jnp.sum(p * dp, axis=0, keepdims=True)
                ds = _unfold_t((p * (dp - delta)).astype(BF16), upper_t)
                dsink = -ps * delta
                for g in range(4):
                    dsk = dsk + jnp.where(lane == 4 * hk + g, jnp.sum(dsink[:, g * nb:(g + 1) * nb]), 0.0)
                ta, tb = _unstack_heads(_dot_tn(ds, kd) * scale)
                dq_a = (2 * hk) * 128
                dq_ref[:, dq_a:dq_a + 128] = _rope(ta, cc_ref[...], -sc_ref[...]).astype(BF16)
                dq_ref[:, dq_a + 128:dq_a + 256] = _rope(tb, cc_ref[...], -sc_ref[...]).astype(BF16)
                rk = _dot(ds, qs) * scale
                rv = _dot(_unfold_t(p.astype(BF16), upper_t), dos)
                dk_tiles.append(rk + pltpu.roll(rk, 64, 1))
                dv_tiles.append(rv + pltpu.roll(rv, 64, 1))
            dsk_ref[...] += dsk
            dk_full = jnp.concatenate([jnp.where(lo, dk_tiles[0], dk_tiles[1]),
                                       jnp.where(lo, dk_tiles[2], dk_tiles[3])], axis=1)
            dv_full = jnp.concatenate([jnp.where(lo, dv_tiles[0], dv_tiles[1]),
                                       jnp.where(lo, dv_tiles[2], dv_tiles[3])], axis=1)
            dk_ref[...] = _rope(ck_ref[...] + dk_full[0:nb], cp_ref[...], -sp_ref[...]).astype(BF16)
            dv_ref[...] = (cv_ref[...] + dv_full[0:nb]).astype(BF16)
            ck_ref[...] = dk_full[nb:2 * nb]
            cv_ref[...] = dv_full[nb:2 * nb]

        @pl.when(i == N_ATTN_BLOCKS)
        def _():
            dk_ref[...] = _rope(ck_ref[...], cp_ref[...], -sp_ref[...]).astype(BF16)
            dv_ref[...] = cv_ref[...].astype(BF16)

    qi = lambda i: jnp.minimum(i, N_ATTN_BLOCKS - 1)
    cur = lambda w: pl.BlockSpec((nb, w), lambda i: (qi(i), 0))
    prv = lambda w: pl.BlockSpec((nb, w), lambda i: (_prev(qi(i)), 0))
    out_prev = lambda w: pl.BlockSpec((nb, w), lambda i: (_prev(i), 0))
    return _call(
        body, name=name, grid=(n_steps,),
        in_specs=[cur(D_MODEL), cur(KV_W), prv(KV_W), cur(KV_W), prv(KV_W), cur(D_MODEL),
                  cur(128), cur(128), out_prev(128), out_prev(128), pl.BlockSpec(memory_space=pltpu.SMEM)],
        out_specs=[cur(D_MODEL), out_prev(2 * KV_W), pl.BlockSpec((8, 128), lambda i: (0, 0))],
        out_shape=[jax.ShapeDtypeStruct((SEQ, D_MODEL), BF16), jax.ShapeDtypeStruct((SEQ, 2 * KV_W), BF16),
                   jax.ShapeDtypeStruct((8, 128), F32)],
        scratch_shapes=[pltpu.VMEM((nb, KV_W), F32), pltpu.VMEM((nb, KV_W), F32)],
        args=[qr, kr, kr, v, v, dy, cos, sin_signed, cos, sin_signed, sinks], stages=stages)


def _proj_scratch():
    return [pltpu.VMEM((D_MODEL, D_MODEL), BF16)] * 3 + [pltpu.SemaphoreType.DMA((3 * N_CHIPS,))]


def _load_projs(w_refs, wl_ref, wa_ref, wo_ref, sem):
    for k, (w_ref, dst) in enumerate(zip(w_refs, (wl_ref, wa_ref, wo_ref))):
        _load_weight(w_ref, dst, sem.at[pl.ds(k * N_CHIPS, N_CHIPS)])


def merge_fwd(y_lru, y_attn, g_lru, g_attn, projs, g_post, h_in, name, stages=()):
    tm = MM_ROWS

    def body(yl_ref, ya_ref, gl_ref, ga_ref, w1_ref, w2_ref, w3_ref, gp_ref, h_ref,
             pl_ref, pa_ref, mg_ref, m_ref, o_ref, wl_ref, wa_ref, wo_ref, sem):
        @pl.when(pl.program_id(0) == 0)
        def _():
            _load_projs((w1_ref, w2_ref, w3_ref), wl_ref, wa_ref, wo_ref, sem)

        p_l = _dot(yl_ref[...], wl_ref[...])
        p_a = _dot(ya_ref[...], wa_ref[...])
        pl_ref[...] = p_l.astype(BF16)
        pa_ref[...] = p_a.astype(BF16)
        merged = (_sigmoid(gl_ref[...]) * p_l + _sigmoid(ga_ref[...]) * p_a).astype(BF16)
        mg_ref[...] = merged
        m = _dot(merged, wo_ref[...])
        m_ref[...] = m
        o_ref[...] = h_ref[...] + m * _rsqrt_mean_sq(m) * gp_ref[...]

    row = _ROW(tm)
    return _call(
        body, name=name, grid=(SEQ // tm,),
        in_specs=[row, row, row, row, ANY, ANY, ANY, _VEC, row],
        out_specs=[row] * 5,
        out_shape=[jax.ShapeDtypeStruct((SEQ, D_MODEL), BF16)] * 3 + [jax.ShapeDtypeStruct((SEQ, D_MODEL), F32)] * 2,
        scratch_shapes=_proj_scratch(),
        args=[y_lru, y_attn, g_lru, g_attn, *projs, g_post, h_in], stages=stages)


def merge_bwd(d_out, m, g_post, projs, g_lru, g_attn, p_l, p_a, name, stages=()):
    tm = 256

    def body(do_ref, m_ref, gp_ref, w1_ref, w2_ref, w3_ref, gl_ref, ga_ref, pl_ref, pa_ref,
             dm_ref, dpl_ref, dpa_ref, dgl_ref, dga_ref, dya_ref, dyl_ref, dgp_ref, wl_ref, wa_ref, wo_ref, sem):
        @pl.when(pl.program_id(0) == 0)
        def _():
            _load_projs((w1_ref, w2_ref, w3_ref), wl_ref, wa_ref, wo_ref, sem)
            dgp_ref[...] = jnp.zeros_like(dgp_ref)

        mv = m_ref[...]
        rm = _rsqrt_mean_sq(mv)
        mh = mv * rm
        dn = do_ref[...]
        dgp_ref[...] += jnp.sum(dn * mh, axis=0, keepdims=True)
        t = dn * gp_ref[...]
        dm = (rm * (t - mh * jnp.mean(t * mh, axis=-1, keepdims=True))).astype(BF16)
        dm_ref[...] = dm
        dmg = _dot_nt(dm, wo_ref[...])
        sl = _sigmoid(gl_ref[...])
        sa = _sigmoid(ga_ref[...])
        dpl = (dmg * sl).astype(BF16)
        dpa = (dmg * sa).astype(BF16)
        dpl_ref[...] = dpl
        dpa_ref[...] = dpa
        dgl_ref[...] = (dmg * pl_ref[...].astype(F32) * sl * (1.0 - sl)).astype(BF16)
        dga_ref[...] = (dmg * pa_ref[...].astype(F32) * sa * (1.0 - sa)).astype(BF16)
        dyl_ref[...] = _dot_nt(dpl, wl_ref[...])
        dya_ref[...] = _dot_nt(dpa, wa_ref[...]).astype(BF16)

    row = _ROW(tm)
    return _call(
        body, name=name, grid=(SEQ // tm,),
        in_specs=[row, row, _VEC, ANY, ANY, ANY, row, row, row, row],
        out_specs=[row] * 7 + [_VEC],
        out_shape=[jax.ShapeDtypeStruct((SEQ, D_MODEL), BF16)] * 6 + [jax.ShapeDtypeStruct((SEQ, D_MODEL), F32),
                                                                       jax.ShapeDtypeStruct((1, D_MODEL), F32)],
        scratch_shapes=_proj_scratch(),
        args=[d_out, m, g_post, *projs, g_lru, g_attn, p_l, p_a], stages=stages)


def _rope_tables():
    half = HEAD_DIM // 2
    inv_freq = np.float32(ROPE_THETA) ** (-np.arange(half, dtype=np.float32) / np.float32(half))
    ang = np.arange(SEQ, dtype=np.float32)[:, None] * inv_freq[None, :]
    cos, sin = np.cos(ang), np.sin(ang)
    return (jnp.asarray(np.tile(np.concatenate([cos, cos], axis=1), (1, 2))),
            jnp.asarray(np.tile(np.concatenate([-sin, sin], axis=1), (1, 2))))


def _block_diag(w):
    per = LRU_TC // LRU_BLOCK_W
    w4 = w.reshape(LRU_W // LRU_TC, per, LRU_BLOCK_W, LRU_BLOCK_W)
    eye = jnp.eye(per, dtype=w.dtype)
    return jnp.einsum('jacd,ab->jacbd', w4, eye).reshape(LRU_W // LRU_TC, LRU_TC, LRU_TC).astype(BF16)


def _diag_blocks(p):
    per = LRU_TC // LRU_BLOCK_W
    p5 = p.reshape(LRU_W // LRU_TC, per, LRU_BLOCK_W, per, LRU_BLOCK_W)
    return jnp.stack([p5[:, a, :, a, :] for a in range(per)], axis=1).reshape(LRU_W // LRU_BLOCK_W, LRU_BLOCK_W, LRU_BLOCK_W)


def _place():
    x, y, c = lax.axis_index('x'), lax.axis_index('y'), lax.axis_index('c')
    chips = [(1 - x, y), (x, 1 - y), (1 - x, 1 - y)]
    return x, y, c, chips


def _rcopy(src, dst, send_sem, recv_sem, to):
    return pltpu.make_async_remote_copy(src_ref=src, dst_ref=dst, send_sem=send_sem, recv_sem=recv_sem,
                                        device_id=to, device_id_type=MESH)


class _Stage:
    inputs, out_shape, scratch, peers = (), (), (), ()

    def start(self, ins, outs, scr):
        plan = self._plan(ins, outs, scr)
        for ld in plan['loads']:
            ld.start()
        for cp in plan['sends']:
            cp.start()

    def relay(self, ins, outs, scr):
        pass

    def mid(self, ins, outs, scr):
        plan = self._plan(ins, outs, scr)
        for ld, st in zip(plan['loads'], plan['stores']):
            ld.wait()
            st.start()
        for arrived, onward in zip(plan['arrivals'], plan['forwards']):
            arrived.wait_recv()
            onward.start()

    def end(self, ins, outs, scr):
        plan = self._plan(ins, outs, scr)
        for st in plan['stores']:
            st.wait()
        for arrived in (plan['final_arrivals'] if plan['forwards'] else plan['arrivals']):
            arrived.wait_recv()
        for cp in plan['sends'] + plan['forwards']:
            cp.wait_send()


def _empty_plan():
    return dict(loads=[], stores=[], sends=[], arrivals=[], forwards=[], final_arrivals=[])


class GatherStage(_Stage):
    peers = ('chips', 'sib')
    N_CP = 12

    def __init__(self, items):
        self.ranges = [(off, rows) for _, off, rows in items]
        self.inputs = [src for src, _, _ in items]
        self.out_shape = [jax.ShapeDtypeStruct((N_CHIPS, rows, D_MODEL), BF16) for _, rows in self.ranges]
        n = self.N_CP * len(items)
        self.scratch = [pltpu.VMEM((sum(r for _, r in self.ranges), D_MODEL), BF16), pltpu.SemaphoreType.DMA((n,)),
                        pltpu.SemaphoreType.DMA((n,)), pltpu.SemaphoreType.DMA((2 * len(items),))]

    def _plan(self, ins, outs, scr):
        buf, send, recv, lsem = scr
        x, y, c, _ = _place()
        me_q, q_x, q_y, q_d = 2 * x + y, 2 * (1 - x) + y, 2 * x + (1 - y), 2 * (1 - x) + (1 - y)
        to_x, to_y, sib = (1 - x, y, c), (x, 1 - y, c), (x, y, 1 - c)
        plan = dict(loads=[], stores=[], first=[], early=[], relays=[], late=[], hand_early=[], hand_late=[], final=[])
        boff = 0
        for w, ((off, rows), p_ref, o_ref) in enumerate(zip(self.ranges, ins, outs)):
            hr = rows // 2
            ch = hr // 2
            plan['loads'].append(pltpu.make_async_copy(p_ref.at[pl.ds(off, rows)], buf.at[pl.ds(boff, rows)], lsem.at[2 * w]))
            plan['stores'].append(pltpu.make_async_copy(buf.at[pl.ds(boff, rows)], o_ref.at[me_q], lsem.at[2 * w + 1]))
            boff += rows
            base = w * self.N_CP
            mine = [pl.ds(pl.multiple_of(c * hr + k * ch, 16), ch) for k in range(2)]
            theirs = [pl.ds(pl.multiple_of((1 - c) * hr + k * ch, 16), ch) for k in range(2)]
            src = [p_ref.at[pl.ds(pl.multiple_of(off + c * hr + k * ch, 16), ch)] for k in range(2)]

            def cp(k, s, d, to):
                return _rcopy(s, d, send.at[base + k], recv.at[base + k], to)

            def here(q, rows_):
                return o_ref.at[q, rows_]

            plan['first'] += [cp(0, src[0], here(me_q, mine[0]), to_x), cp(2, src[1], here(me_q, mine[1]), to_y),
                              cp(1, src[1], here(me_q, mine[1]), to_x), cp(3, src[0], here(me_q, mine[0]), to_y)]
            x_a, y_b = here(q_x, mine[0]), here(q_y, mine[1])
            plan['early'] += [cp(0, x_a, x_a, to_x), cp(2, y_b, y_b, to_y)]
            plan['relays'] += [cp(4, x_a, x_a, to_y), cp(5, y_b, y_b, to_x)]
            plan['hand_early'] += [cp(6, x_a, x_a, sib), cp(7, y_b, y_b, sib)]
            x_b, y_a, d_a, d_b = here(q_x, mine[1]), here(q_y, mine[0]), here(q_d, mine[0]), here(q_d, mine[1])
            plan['late'] += [cp(1, x_b, x_b, to_x), cp(3, y_a, y_a, to_y), cp(4, d_a, d_a, to_y), cp(5, d_b, d_b, to_x)]
            plan['hand_late'] += [cp(8, x_b, x_b, sib), cp(9, y_a, y_a, sib), cp(10, d_a, d_a, sib), cp(11, d_b, d_b, sib)]
            for k, (q, piece) in enumerate([(q_x, 0), (q_y, 1), (q_x, 1), (q_y, 0), (q_d, 0), (q_d, 1)]):
                got = here(q, theirs[piece])
                plan['final'].append(cp(6 + k, got, got, sib))
        return plan

    def start(self, ins, outs, scr):
        plan = self._plan(ins, outs, scr)
        for ld in plan['loads']:
            ld.start()
        for cp in plan['first']:
            cp.start()

    def relay(self, ins, outs, scr):
        plan = self._plan(ins, outs, scr)
        for arrived in plan['early']:
            arrived.wait_recv()
        for cp in plan['relays'] + plan['hand_early']:
            cp.start()

    def mid(self, ins, outs, scr):
        plan = self._plan(ins, outs, scr)
        for ld, st in zip(plan['loads'], plan['stores']):
            ld.wait()
            st.start()
        for arrived in plan['late']:
            arrived.wait_recv()
        for cp in plan['hand_late']:
            cp.start()

    def end(self, ins, outs, scr):
        plan = self._plan(ins, outs, scr)
        for st in plan['stores']:
            st.wait()
        for arrived in plan['final']:
            arrived.wait_recv()
        for cp in plan['first'] + plan['relays'] + plan['hand_early'] + plan['hand_late']:
            cp.wait_send()


class PairStage(_Stage):
    peers = ('sib',)

    def __init__(self, grads):
        self.inputs = list(grads)
        self.out_shape = [jax.ShapeDtypeStruct((N_CHIPS, 1) + g.shape[2:], BF16) for g in grads]
        n_cp = N_CHIPS * len(grads)
        self.scratch = [pltpu.SemaphoreType.DMA((n_cp,)), pltpu.SemaphoreType.DMA((n_cp,))]

    def _plan(self, ins, outs, scr):
        send, recv = scr
        x, y, c, _ = _place()
        plan = _empty_plan()
        for w, (g_ref, l_ref) in enumerate(zip(ins, outs)):
            for q in range(N_CHIPS):
                i = w * N_CHIPS + q
                plan['sends'].append(_rcopy(g_ref.at[q, pl.ds(1 - c, 1)], l_ref.at[q], send.at[i], recv.at[i], (x, y, 1 - c)))
        plan['arrivals'] = plan['sends']
        return plan


class ChipStage(_Stage):
    peers = ('chips',)

    def __init__(self, items):
        self.ranges = [(off, n) for _, off, n in items]
        self.inputs = [s for s, _, _ in items]
        self.out_shape = [jax.ShapeDtypeStruct((N_CHIPS, n, D_MODEL), BF16) for _, n in self.ranges]
        n_cp = 3 * len(items)
        self.scratch = [pltpu.VMEM((sum(n for _, n in self.ranges), D_MODEL), BF16), pltpu.SemaphoreType.DMA((n_cp,)),
                        pltpu.SemaphoreType.DMA((n_cp,)), pltpu.SemaphoreType.DMA((2 * len(items),))]

    def _plan(self, ins, outs, scr):
        buf, send, recv, lsem = scr
        x, y, c, chips = _place()
        me_q = 2 * x + y
        plan = _empty_plan()
        boff = 0
        for w, ((off, n), s_ref, l_ref) in enumerate(zip(self.ranges, ins, outs)):
            rows = pl.ds(off, n)
            plan['loads'].append(pltpu.make_async_copy(s_ref.at[me_q, rows], buf.at[pl.ds(boff, n)], lsem.at[2 * w]))
            plan['stores'].append(pltpu.make_async_copy(buf.at[pl.ds(boff, n)], l_ref.at[me_q], lsem.at[2 * w + 1]))
            boff += n
            for j, (cx, cy) in enumerate(chips):
                i = w * 3 + j
                got = l_ref.at[2 * cx + cy]
                plan['sends'].append(_rcopy(s_ref.at[2 * cx + cy, rows], l_ref.at[me_q], send.at[i], recv.at[i], (cx, cy, c)))
                plan['arrivals'].append(_rcopy(got, got, send.at[i], recv.at[i], (cx, cy, c)))
        return plan


class SwapStage(_Stage):
    peers = ('sib',)

    def __init__(self, items):
        n = len(items)
        self.inputs = list(items)
        self.out_shape = [jax.ShapeDtypeStruct((2,) + a.shape, a.dtype) for a in items]
        self.scratch = [pltpu.VMEM(a.shape, a.dtype) for a in items] + [
            pltpu.SemaphoreType.DMA((n,)), pltpu.SemaphoreType.DMA((n,)), pltpu.SemaphoreType.DMA((2 * n,))]

    def _plan(self, ins, outs, scr):
        bufs, (send, recv, lsem) = scr[:len(ins)], scr[len(ins):]
        x, y, c, _ = _place()
        plan = _empty_plan()
        for w, (h_ref, o_ref, buf) in enumerate(zip(ins, outs, bufs)):
            plan['loads'].append(pltpu.make_async_copy(h_ref, buf, lsem.at[2 * w]))
            plan['stores'].append(pltpu.make_async_copy(buf, o_ref.at[c], lsem.at[2 * w + 1]))
            got = o_ref.at[1 - c]
            plan['sends'].append(_rcopy(h_ref, o_ref.at[c], send.at[w], recv.at[w], (x, y, 1 - c)))
            plan['arrivals'].append(_rcopy(got, got, send.at[w], recv.at[w], (x, y, 1 - c)))
        return plan


class SmallGatherStage(_Stage):
    peers = ('chips', 'sib')

    def __init__(self, blk):
        self.inputs = [blk]
        self.out_shape = [jax.ShapeDtypeStruct((N_DEV,) + blk.shape, blk.dtype)]
        self.scratch = [pltpu.VMEM(blk.shape, blk.dtype), pltpu.SemaphoreType.DMA((7,)), pltpu.SemaphoreType.DMA((7,)),
                        pltpu.SemaphoreType.DMA((2,))]

    def _plan(self, ins, outs, scr):
        (x_ref,), (o_ref,), (buf, send, recv, lsem) = ins, outs, scr
        x, y, c, chips = _place()
        sib = (x, y, 1 - c)

        def slot(px, py, pc):
            return o_ref.at[4 * px + 2 * py + pc]

        plan = _empty_plan()
        plan['loads'].append(pltpu.make_async_copy(x_ref, buf, lsem.at[0]))
        plan['stores'].append(pltpu.make_async_copy(buf, slot(x, y, c), lsem.at[1]))
        from_sib = slot(x, y, 1 - c)
        plan['sends'].append(_rcopy(x_ref, slot(x, y, c), send.at[0], recv.at[0], sib))
        plan['final_arrivals'].append(_rcopy(from_sib, from_sib, send.at[0], recv.at[0], sib))
        for j, (cx, cy) in enumerate(chips):
            got, got_sib = slot(cx, cy, c), slot(cx, cy, 1 - c)
            plan['sends'].append(_rcopy(x_ref, slot(x, y, c), send.at[1 + j], recv.at[1 + j], (cx, cy, c)))
            plan['arrivals'].append(_rcopy(got, got, send.at[1 + j], recv.at[1 + j], (cx, cy, c)))
            plan['forwards'].append(_rcopy(got, got, send.at[4 + j], recv.at[4 + j], sib))
            plan['final_arrivals'].append(_rcopy(got_sib, got_sib, send.at[4 + j], recv.at[4 + j], sib))
        return plan


_HBM = pl.BlockSpec(memory_space=pltpu.HBM)
_SEM = pl.BlockSpec(memory_space=pltpu.SEMAPHORE)
_DATAFLOW = pltpu.CompilerParams(has_side_effects=pltpu.SideEffectType.DATAFLOW_SIDE_EFFECTING)


def chip_exchange_start(s):
    def body(s_ref, land_ref, send, recv, s_thru, land_thru, token):
        x, y, c, chips = _place()
        for j, (cx, cy) in enumerate(chips):
            _rcopy(s_ref.at[2 * cx + cy], land_ref.at[2 * x + y], send.at[j], recv.at[j], (cx, cy, c)).start()
        token[...] = jnp.zeros_like(token)

    return pl.pallas_call(
        body, name='chip_exchange_start',
        out_shape=(pltpu.SemaphoreType.DMA((3,)), pltpu.SemaphoreType.DMA((3,)), pltpu.HBM(s.shape, s.dtype),
                   pltpu.HBM(s.shape, s.dtype), jax.ShapeDtypeStruct((8, 128), F32)),
        in_specs=(_HBM, _HBM), out_specs=(_SEM, _SEM, _HBM, _HBM, pl.BlockSpec(memory_space=pltpu.VMEM)),
        input_output_aliases={0: 2, 1: 3}, compiler_params=_DATAFLOW,
    )(pltpu.with_memory_space_constraint(s, pltpu.HBM),
      pltpu.with_memory_space_constraint(lax.empty(s.shape, s.dtype), pltpu.HBM))


def chip_exchange_wait(send, recv, s_thru, land_thru, after):
    def body(s_ref, land_ref, send_sem, recv_sem, after_ref, s_out, land_out):
        x, y, c, chips = _place()
        for j, (cx, cy) in enumerate(chips):
            cp = _rcopy(s_ref.at[2 * cx + cy], land_ref.at[2 * cx + cy], send_sem.at[j], recv_sem.at[j], (cx, cy, c))
            cp.wait_send()
            cp.wait_recv()

    return pl.pallas_call(
        body, name='chip_exchange_wait',
        out_shape=(pltpu.HBM(s_thru.shape, s_thru.dtype), pltpu.HBM(land_thru.shape, land_thru.dtype)),
        in_specs=(_HBM, _HBM, _SEM, _SEM, ANY), out_specs=(_HBM, _HBM),
        input_output_aliases={0: 0, 1: 1}, compiler_params=_DATAFLOW,
    )(s_thru, land_thru, send, recv, after)


def comm_call(name, stages):
    def body():
        pass

    return _call(body, name=name, grid=(1,), in_specs=[], out_specs=[], out_shape=[], args=[], stages=stages)[1]


def pair_sum(g4, land, c_arr, name):
    hr = g4.shape[2]

    def body(c_ref, g_ref, l_ref, o_ref):
        o_ref[0] = (g_ref[0, 0].astype(F32) + l_ref[0, 0].astype(F32)).astype(BF16)

    return pl.pallas_call(
        body, name=name,
        grid_spec=pltpu.PrefetchScalarGridSpec(
            num_scalar_prefetch=1, grid=(N_CHIPS,),
            in_specs=[pl.BlockSpec((1, 1, hr, D_MODEL), lambda q, c: (q, c[0], 0, 0)),
                      pl.BlockSpec((1, 1, hr, D_MODEL), lambda q, c: (q, 0, 0, 0))],
            out_specs=pl.BlockSpec((1, hr, D_MODEL), lambda q, c: (q, 0, 0))),
        out_shape=jax.ShapeDtypeStruct((N_CHIPS, hr, D_MODEL), BF16),
        compiler_params=_params(1),
    )(c_arr, g4, land)


def small_sum(vec_parts, lru_parts):
    def body(v_ref, l_ref, o_ref):
        for p_ref, lo, n in ((v_ref, 0, ROW_WA), (l_ref, ROW_WA, SMALL_ROWS - ROW_WA)):
            acc = p_ref[0]
            for s in range(1, N_DEV):
                acc = acc + p_ref[s]
            o_ref[lo:lo + n, :] = acc

    return pl.pallas_call(
        body, name='small_sum', grid=(1,),
        in_specs=[pl.BlockSpec(vec_parts.shape, lambda i: (0, 0, 0)), pl.BlockSpec(lru_parts.shape, lambda i: (0, 0, 0))],
        out_specs=pl.BlockSpec((SMALL_ROWS, D_MODEL), lambda i: (0, 0)),
        out_shape=jax.ShapeDtypeStruct((SMALL_ROWS, D_MODEL), F32),
        compiler_params=_params(1),
    )(vec_parts, lru_parts)


def _adam_math(w, g, m, v):
    m2 = ADAM_B1 * m + (1.0 - ADAM_B1) * g
    v2 = ADAM_B2 * v + (1.0 - ADAM_B2) * (g * g)
    m_hat = m2 / (1.0 - ADAM_B1 ** ADAM_STEP)
    v_hat = v2 / (1.0 - ADAM_B2 ** ADAM_STEP)
    delta = -ADAM_LR * (m_hat / (jnp.sqrt(v_hat) + ADAM_EPS) + ADAM_WD * w)
    return delta, m2, v2


def _adam_body(n_parts, transposed, n_after):
    def body(*refs):
        refs = refs[n_after:]
        g_refs = refs[:n_parts]
        w_ref, m_ref, v_ref, go_ref, d_ref, mo_ref, vo_ref = refs[n_parts:]
        def chips_added(blk):
            acc = blk[0].astype(F32)
            for s in range(1, N_CHIPS):
                acc = acc + blk[s].astype(F32)
            return acc

        if transposed:
            g = jnp.concatenate([chips_added(g_ref[h]) for h in range(2) for g_ref in g_refs], axis=0).T
        else:
            rows = [chips_added(g_ref[0]) for g_ref in g_refs]
            g = jnp.concatenate(rows, axis=0) if n_parts > 1 else rows[0]
        go_ref[...] = g
        d_ref[...], mo_ref[...], vo_ref[...] = _adam_math(w_ref[...], g, m_ref[...], v_ref[...])
    return body


def adam_rows(fulls, name, w, m, v, after=()):
    hr = w.shape[0] // 2
    blk = pl.BlockSpec((hr, D_MODEL), lambda h: (h, 0))
    return pl.pallas_call(
        _adam_body(len(fulls), False, len(after)), name='adam_' + name, grid=(2,),
        in_specs=[ANY] * len(after)
        + [pl.BlockSpec((1, N_CHIPS, f.shape[2], D_MODEL), lambda h: (h, 0, 0, 0)) for f in fulls] + [blk, blk, blk],
        out_specs=[blk] * 4,
        out_shape=[jax.ShapeDtypeStruct(w.shape, F32)] * 4,
        compiler_params=_params(1),
    )(*after, *fulls, w, m, v)


def adam_cols(fulls, name, w, m, v, after=()):
    cols = w.shape[1]
    tr = 128
    blk = pl.BlockSpec((tr, cols), lambda i: (i, 0))
    return pl.pallas_call(
        _adam_body(len(fulls), True, len(after)), name='adam_' + name, grid=(D_MODEL // tr,),
        in_specs=[ANY] * len(after)
        + [pl.BlockSpec((2, N_CHIPS, f.shape[2], tr), lambda i: (0, 0, 0, i)) for f in fulls] + [blk, blk, blk],
        out_specs=[blk] * 4,
        out_shape=[jax.ShapeDtypeStruct(w.shape, F32)] * 4,
        compiler_params=_params(1),
    )(*after, *fulls, w, m, v)


def adam_small(g, w, m, v):
    def body(g_ref, w_ref, m_ref, v_ref, d_ref, mo_ref, vo_ref):
        d_ref[...], mo_ref[...], vo_ref[...] = _adam_math(w_ref[...], g_ref[...], m_ref[...], v_ref[...])

    blk = pl.BlockSpec(w.shape, lambda i: (0, 0))
    return pl.pallas_call(
        body, name='adam_small', grid=(1,), in_specs=[blk] * 4, out_specs=[blk] * 3,
        out_shape=[jax.ShapeDtypeStruct(w.shape, F32)] * 3, compiler_params=_params(1),
    )(g, w, m, v)


WEIGHTS = ('ffn1_pre_g', 'ffn1_w_gu', 'ffn1_w_down', 'ffn1_post_g', 'mix_pre_g', 'w_in', 'conv_w', 'conv_b',
           'lru_w_a', 'lru_b_a', 'lru_w_x', 'lru_b_x', 'lru_lambda', 'attn_sinks', 'w_proj_lru', 'w_proj_attn',
           'w_out', 'mix_post_g', 'ffn2_pre_g', 'ffn2_w_gu', 'ffn2_w_down', 'ffn2_post_g')
SMALL = tuple(n for n in WEIGHTS if n not in PACK_OFF)


def _pack_vecs(d, conv_rows):
    sinks = jnp.pad(d['attn_sinks'].reshape(1, N_Q_HEADS), ((0, 0), (0, D_MODEL - N_Q_HEADS)))
    conv = jnp.pad(conv_rows, ((0, ROW_WA - ROW_CONV - conv_rows.shape[0]), (0, 0)))
    return jnp.concatenate([d[n].reshape(1, D_MODEL) for n in SMALL_VECS] + [sinks, conv], axis=0)


def _pack_lru(d):
    return jnp.concatenate([d['lru_w_a'].reshape(64, D_MODEL), d['lru_w_x'].reshape(64, D_MODEL)], axis=0)


def _pack_small(d, conv_rows):
    return jnp.concatenate([_pack_vecs(d, conv_rows), _pack_lru(d)], axis=0)


def _unpack_small(p, shapes):
    out = {n: p[k:k + 1].reshape(shapes[n]) for k, n in enumerate(SMALL_VECS)}
    out['attn_sinks'] = p[ROW_SINKS:ROW_SINKS + 1, :N_Q_HEADS].reshape(shapes['attn_sinks'])
    out['conv_w'] = p[ROW_CONV:ROW_CONV + 1].reshape(shapes['conv_w'])
    out['lru_w_a'] = p[ROW_WA:ROW_WA + 64].reshape(shapes['lru_w_a'])
    out['lru_w_x'] = p[ROW_WX:ROW_WX + 64].reshape(shapes['lru_w_x'])
    return out


def kernel(x, ffn1_pre_g, ffn1_w_gu, ffn1_w_down, ffn1_post_g, mix_pre_g, w_in, conv_w, conv_b, lru_w_a, lru_b_a, lru_w_x, lru_b_x, lru_lambda, attn_sinks, w_proj_lru, w_proj_attn, w_out, mix_post_g, ffn2_pre_g, ffn2_w_gu, ffn2_w_down, ffn2_post_g, loss_target, m_ffn1_pre_g, m_ffn1_w_gu, m_ffn1_w_down, m_ffn1_post_g, m_mix_pre_g, m_w_in, m_conv_w, m_conv_b, m_lru_w_a, m_lru_b_a, m_lru_w_x, m_lru_b_x, m_lru_lambda, m_attn_sinks, m_w_proj_lru, m_w_proj_attn, m_w_out, m_mix_post_g, m_ffn2_pre_g, m_ffn2_w_gu, m_ffn2_w_down, m_ffn2_post_g, v_ffn1_pre_g, v_ffn1_w_gu, v_ffn1_w_down, v_ffn1_post_g, v_mix_pre_g, v_w_in, v_conv_w, v_conv_b, v_lru_w_a, v_lru_b_a, v_lru_w_x, v_lru_b_x, v_lru_lambda, v_attn_sinks, v_w_proj_lru, v_w_proj_attn, v_w_out, v_mix_post_g, v_ffn2_pre_g, v_ffn2_w_gu, v_ffn2_w_down, v_ffn2_post_g):
    given = dict(locals())
    w = {n: given[n] for n in WEIGHTS}
    mom = {n: given['m_' + n] for n in WEIGHTS}
    var = {n: given['v_' + n] for n in WEIGHTS}
    shapes = {n: w[n].shape for n in WEIGHTS}
    xq = lax.axis_index('x')
    yq = lax.axis_index('y')
    cq = lax.axis_index('c')
    me_q = 2 * xq + yq

    c_arr = cq.reshape(1).astype(jnp.int32)
    xs, target = x[0], loss_target[0]
    sw = {n: (w[n][0] if w[n].ndim > 2 else w[n]) for n in SMALL}
    cos, sin_signed = _rope_tables()
    wa_bd = _block_diag(sw['lru_w_a'])
    wx_bd = _block_diag(sw['lru_w_x'])
    sinks = sw['attn_sinks'].reshape(N_Q_HEADS)

    shard = {n: (w[n][0].T if t else w[n][0]).astype(BF16) for n, _, t in PACK}
    conv_pad = jnp.pad(w['conv_w'][0], ((0, 4), (0, 0)))

    def whole(name):
        return (shard[name], 0, PACK_ROWS_OF[name])

    def part(name, p, n_parts=2):
        rows = PACK_ROWS_OF[name] // n_parts
        return (shard[name], p * rows, rows)

    (w_gu1,), (conv_all,) = comm_call('gather_first', [GatherStage([whole('ffn1_w_gu')]), SmallGatherStage(conv_pad)])
    sw['conv_w'] = jnp.transpose(conv_all[0::2, :4, :], (1, 0, 2)).reshape(4, LRU_W)
    proj_names = ['w_proj_lru', 'w_proj_attn', 'w_out']

    (n1, g1, u1, a1), ((w_down1,),) = ffn_fwd_a(xs, sw['ffn1_pre_g'], [w_gu1], 'ffn1_fwd_a',
                                                 stages=[GatherStage([whole('ffn1_w_down')])])
    (f1, h1), ((w_in_t,),) = ffn_fwd_b(a1, w_down1, sw['ffn1_post_g'], xs, 'ffn1_fwd_b', stages=[GatherStage([whole('w_in')])])
    (um, gate, xbr, q, k, v, g_lru, g_attn), ((w_gu2a,),) = mix_in(h1, sw['mix_pre_g'], w_in_t, 'mix_in',
                                                                   stages=[GatherStage([part('ffn2_w_gu', 0)])])
    (y_lru, h_lru), ((w_gu2b,),) = lru_fwd(gate, xbr, sw['conv_w'], sw['conv_b'], wa_bd, sw['lru_b_a'], wx_bd, sw['lru_b_x'],
                                           sw['lru_lambda'], 'lru_fwd', stages=[GatherStage([part('ffn2_w_gu', 1)])])
    (qr, kr, y_attn), (projs,) = attn_fwd(q, k, v, cos, sin_signed, sinks, 'attn_fwd',
                                          stages=[GatherStage([whole(n) for n in proj_names])])
    (p_l, p_a, merged, m, h2), ((w_down2,),) = merge_fwd(y_lru, y_attn, g_lru, g_attn, projs, sw['mix_post_g'], h1, 'merge_fwd',
                                                         stages=[GatherStage([whole('ffn2_w_down')])])
    w_gu2 = [w_gu2a, w_gu2b]
    (n2, g2, u2, a2), _ = ffn_fwd_a(h2, sw['ffn2_pre_g'], w_gu2, 'ffn2_fwd_a')
    (f2, dy, loss_blk), _ = ffn_fwd_b(a2, w_down2, sw['ffn2_post_g'], h2, 'ffn2_fwd_b', target=target)

    gs, full = {}, {}

    def pair_stage(names, grads):
        g4 = [g.reshape(N_CHIPS, 2, PACK_ROWS_OF[n] // 2, D_MODEL) for n, g in zip(names, grads)]
        return PairStage(g4), g4

    def pair_sums(names, g4, lands):
        return [pair_sum(g, l, c_arr, 'pair_sum_' + n) for n, g, l in zip(names, g4, lands)]

    def halves(s, n_parts=2):
        n = s.shape[1] // n_parts
        return [(s, p * n, n) for p in range(n_parts)]

    (df2, dgu2, gs['ffn2_post_g']), _ = ffn_bwd_a(dy, f2, sw['ffn2_post_g'], w_down2, g2, u2, 'ffn2_bwd_a')
    g_down2, _ = mm_tn([a2], df2, 1408, 'ffn2_dw_down')
    st, g4 = pair_stage(['ffn2_w_down'], [g_down2])
    g_gu2, (lands,) = mm_tn([dgu2], n2, 1408, 'ffn2_dw_gu', stages=[st])
    (s_down2,) = pair_sums(['ffn2_w_down'], g4, lands)
    st, g4 = pair_stage(['ffn2_w_gu'], [g_gu2])
    (dh2, gs['ffn2_pre_g']), ((l_down2,), lands) = norm_bwd([dgu2], w_gu2, h2, sw['ffn2_pre_g'], dy, 'ffn2_bwd_b',
                                                            stages=[ChipStage([(s_down2, 0, s_down2.shape[1])]), st])
    (s_gu2,) = pair_sums(['ffn2_w_gu'], g4, lands)

    (dm, dpl, dpa, dgl, dga, dya, dyl, gs['mix_post_g']), ((l_gu2a,),) = merge_bwd(
        dh2, m, sw['mix_post_g'], projs, g_lru, g_attn, p_l, p_a, 'merge_bwd', stages=[ChipStage(halves(s_gu2)[:1])])
    g_projs = [mm_tn([merged if n == 'w_out' else (y_lru if n == 'w_proj_lru' else y_attn)],
                     dm if n == 'w_out' else (dpl if n == 'w_proj_lru' else dpa), D_MODEL, 'd' + n)[0] for n in proj_names]
    st, g4 = pair_stage(proj_names, g_projs)
    (dq, dkv, dsk), ((l_gu2b,), lands, (full['ffn2_w_down'],)) = attn_bwd(
        qr, kr, v, dya, cos, sin_signed, sinks, 'attn_bwd', stages=[ChipStage(halves(s_gu2)[1:]), st, SwapStage([l_down2])])
    full['ffn2_w_down'] = [full['ffn2_w_down']]
    gs['attn_sinks'] = dsk[0:1, 0:N_Q_HEADS]
    s_projs = pair_sums(proj_names, g4, lands)
    (dgate, dxbr, vecs, dwa, dwx), (l_projs, full['ffn2_w_gu']) = lru_bwd(
        gate, xbr, h_lru, dyl, sw['conv_w'], sw['conv_b'], wa_bd, sw['lru_b_a'], wx_bd, sw['lru_b_x'], sw['lru_lambda'],
        'lru_bwd', stages=[ChipStage([(s, 0, s.shape[1]) for s in s_projs]), SwapStage([l_gu2a, l_gu2b])])
    gs['conv_w'] = vecs[0:4]
    gs['conv_b'], gs['lru_b_a'], gs['lru_b_x'], gs['lru_lambda'] = vecs[4:5], vecs[5:6], vecs[6:7], vecs[7:8]
    gs['lru_w_a'] = _diag_blocks(dwa)
    gs['lru_w_x'] = _diag_blocks(dwx)
    dz = [dgate, dxbr, dq, dkv, dgl, dga]
    g_in, ((lru_all,),) = mm_tn(dz, um, 512, 'dw_in', stages=[SmallGatherStage(_pack_lru(gs))])
    st, g4 = pair_stage(['w_in'], [g_in])
    (dh1, gs['mix_pre_g']), (lands, f_projs) = norm_bwd(dz, [w_in_t], h1, sw['mix_pre_g'], dh2, 'mix_bwd_in',
                                                        stages=[st, SwapStage(l_projs)])
    for n, f in zip(proj_names, f_projs):
        full[n] = [f]
    (s_in,) = pair_sums(['w_in'], g4, lands)

    (df1, dgu1, gs['ffn1_post_g']), ((l_in_a,),) = ffn_bwd_a(dh1, f1, sw['ffn1_post_g'], w_down1, g1, u1, 'ffn1_bwd_a',
                                                             stages=[ChipStage(halves(s_in)[:1])])
    g_down1, _ = mm_tn([a1], df1, 1408, 'ffn1_dw_down')
    st, g4 = pair_stage(['ffn1_w_down'], [g_down1])
    g_gu1, ((l_in_b,), lands) = mm_tn([dgu1], n1, 1408, 'ffn1_dw_gu', stages=[ChipStage(halves(s_in)[1:]), st])
    (s_down1,) = pair_sums(['ffn1_w_down'], g4, lands)
    st, g4 = pair_stage(['ffn1_w_gu'], [g_gu1])
    (dx, gs['ffn1_pre_g']), ((l_down1,), lands, full['w_in']) = norm_bwd(
        [dgu1], [w_gu1], xs, sw['ffn1_pre_g'], dh1, 'ffn1_bwd_b',
        stages=[ChipStage([(s_down1, 0, s_down1.shape[1])]), st, SwapStage([l_in_a, l_in_b])])
    (s_gu1,) = pair_sums(['ffn1_w_gu'], g4, lands)
    loss_row = jnp.pad(loss_blk[0:1], ((0, 0), (0, D_MODEL - loss_blk.shape[1])))
    vec_blk = _pack_vecs(gs, jnp.concatenate([gs['conv_w'], loss_row], axis=0))
    send, recv, s_thru, land_thru, token = chip_exchange_start(s_gu1)
    out_g, out_d, out_m, out_v = {}, {}, {}, {}

    def adam(n, after=()):
        fn = adam_cols if dict((k, t) for k, _, t in PACK)[n] else adam_rows
        g_, d_, m_, v_ = fn(full[n], n, w[n][0], mom[n][0], var[n][0], after=after)
        out_g[n], out_d[n], out_m[n], out_v[n] = g_[None], d_[None], m_[None], v_[None]

    behind = token
    for n in ['ffn2_w_gu', 'w_in', 'ffn2_w_down'] + proj_names:
        adam(n, after=(behind,))
        behind = out_v[n]
    s_back, l_gu1 = chip_exchange_wait(send, recv, s_thru, land_thru, after=behind)
    own = lax.dynamic_slice_in_dim(s_back, me_q, 1, axis=0)
    l_gu1 = lax.dynamic_update_slice_in_dim(l_gu1, own, me_q, axis=0)
    (vec_all,), (f_down1, f_gu1) = comm_call('swap_last', [SmallGatherStage(vec_blk), SwapStage([l_down1, l_gu1])])
    full['ffn1_w_down'] = [f_down1]
    full['ffn1_w_gu'] = [f_gu1]
    adam('ffn1_w_gu')
    adam('ffn1_w_down')

    tot = small_sum(vec_all, lru_all)
    loss = tot[ROW_WA - 1, 0]
    conv_g = lax.dynamic_slice(tot[ROW_CONV:ROW_CONV + 4], (0, me_q * (LRU_W // N_CHIPS)), (4, LRU_W // N_CHIPS))
    small_g = _unpack_small(tot, shapes)
    small_g['conv_w'] = conv_g.reshape(shapes['conv_w'])
    g_pack = jnp.concatenate([tot[:ROW_CONV], conv_g.reshape(1, D_MODEL), jnp.zeros((ROW_WA - ROW_CONV - 1, D_MODEL), F32),
                              tot[ROW_WA:]], axis=0)
    packs = [_pack_small({n: d[n] for n in SMALL}, d['conv_w'].reshape(1, D_MODEL)) for d in (w, mom, var)]
    d_p, m_p, v_p = adam_small(g_pack, *packs)
    for n in SMALL:
        out_g[n] = small_g[n]
    for dst, p in ((out_d, d_p), (out_m, m_p), (out_v, v_p)):
        dst.update(_unpack_small(p, shapes))

    return (loss, dx[None], *[out_g[n] for n in WEIGHTS], *[out_d[n] for n in WEIGHTS],
            *[out_m[n] for n in WEIGHTS], *[out_v[n] for n in WEIGHTS])
```

```python
import jax
import jax.numpy as jnp
import numpy as np
from jax import lax
from jax.experimental import pallas as pl
from jax.experimental.pallas import tpu as pltpu

F32 = jnp.float32
BF16 = jnp.bfloat16

SEQ = 2048
D_MODEL = 1024
D_FF = 2816
LRU_W = 1024
LRU_BLOCK_W = 64
HEAD_DIM = 64
N_Q_HEADS = 16
N_KV_HEADS = 4
KV_W = N_KV_HEADS * HEAD_DIM
ATTN_BLOCK = 128
N_ATTN_BLOCKS = SEQ // ATTN_BLOCK
IN_SEGS = (1024, 1024, 1024, 256, 256, 1024, 1024)
IN_W = sum(IN_SEGS)
NORM_EPS = 1e-6
MASK_VALUE = -1e30
ROPE_THETA = 10000.0
LRU_C = 8.0
MACARON = 0.5
ADAM_LR = 0.001
ADAM_B1 = 0.9
ADAM_B2 = 0.999
ADAM_EPS = 1e-08
ADAM_WD = 0.01
ADAM_STEP = 10

N_CHIPS = 4
N_DEV = 8
VMEM_LIMIT = 56 * 1024 * 1024
MM_ROWS = 256
MESH = pl.DeviceIdType.MESH
ANY = pl.BlockSpec(memory_space=pl.ANY)

PACK = (('ffn1_w_gu', 1408, True), ('w_in', 1408, True), ('ffn2_w_gu', 1408, True),
        ('ffn1_w_down', 704, False), ('ffn2_w_down', 704, False),
        ('w_proj_lru', 256, False), ('w_proj_attn', 256, False), ('w_out', 256, False))
PACK_ROWS_OF = {n: r for n, r, _ in PACK}
PACK_OFF = {}
_o = 0
for _n, _r, _t in PACK:
    PACK_OFF[_n] = _o
    _o += _r

SMALL_VECS = ('ffn1_pre_g', 'ffn1_post_g', 'mix_pre_g', 'conv_b', 'lru_b_a', 'lru_b_x', 'lru_lambda',
              'mix_post_g', 'ffn2_pre_g', 'ffn2_post_g')
SMALL_ROWS = 144
ROW_SINKS, ROW_CONV, ROW_WA, ROW_WX = 10, 11, 16, 80


def _dot(a, b):
    return jnp.dot(a, b, preferred_element_type=F32)


def _dot_nt(a, b):
    return lax.dot_general(a, b, (((1,), (1,)), ((), ())), preferred_element_type=F32)


def _dot_tn(a, b):
    return lax.dot_general(a, b, (((0,), (0,)), ((), ())), preferred_element_type=F32)


def _params(n_grid):
    return pltpu.CompilerParams(dimension_semantics=("arbitrary",) * n_grid, vmem_limit_bytes=VMEM_LIMIT)


def _sigmoid(x):
    return 1.0 / (1.0 + jnp.exp(-x))


def _rsqrt_mean_sq(x):
    return lax.rsqrt(jnp.mean(x * x, axis=-1, keepdims=True) + NORM_EPS)


def _expm1(x):
    poly = x * (1.0 + x * (0.5 + x * (1.0 / 6.0)))
    return jnp.where(jnp.abs(x) < 0.02, poly, jnp.exp(x) - 1.0)


_GELU_K = 0.7978845608028654
_GELU_C = 0.044715


def _gelu(x):
    t = jnp.tanh(_GELU_K * (x + _GELU_C * x * x * x))
    return 0.5 * x * (1.0 + t), t


def _gelu_grad(x, t):
    return 0.5 * (1.0 + t) + 0.5 * x * (1.0 - t * t) * _GELU_K * (1.0 + 3.0 * _GELU_C * x * x)


def _load_weight(w_refs, dst_ref, sem):
    w_refs = list(w_refs) if isinstance(w_refs, (list, tuple)) else [w_refs]
    rows = dst_ref.shape[0] // N_CHIPS
    rp = rows // len(w_refs)
    cps = [pltpu.make_async_copy(w_ref.at[q], dst_ref.at[pl.ds(q * rows + p * rp, rp)], sem.at[p * N_CHIPS + q])
           for p, w_ref in enumerate(w_refs) for q in range(N_CHIPS)]
    for cp in cps:
        cp.start()
    for cp in cps:
        cp.wait()


def _weight_scratch(rows_total, parts=1):
    return [pltpu.VMEM((rows_total, D_MODEL), BF16), pltpu.SemaphoreType.DMA((N_CHIPS * parts,))]


_ROW = lambda tm: pl.BlockSpec((tm, D_MODEL), lambda i: (i, 0))
_VEC = pl.BlockSpec((1, D_MODEL), lambda i: (0, 0))


def _call(body, *, name, grid, in_specs, out_specs, out_shape, args, scratch_shapes=(), stages=()):
    in_specs, out_specs, out_shape, scratch_shapes = list(in_specs), list(out_specs), list(out_shape), list(scratch_shapes)
    n_in, n_out, n_sc = len(in_specs), len(out_specs), len(scratch_shapes)
    k_in = [len(s.inputs) for s in stages]
    k_out = [len(s.out_shape) for s in stages]
    k_sc = [len(s.scratch) for s in stages]
    last = grid[0] - 1

    def split(refs, counts):
        parts, pos = [], 0
        for k in counts:
            parts.append(refs[pos:pos + k])
            pos += k
        return parts

    kinds = tuple(sorted({k for s in stages for k in s.peers}))
    collective_id = {(): None, ('sib',): 0, ('chips',): 1, ('chips', 'sib'): 2}[kinds]

    def full(*refs):
        ins, s_ins, outs, s_outs, scr, s_scr = split(refs, [n_in, sum(k_in), n_out, sum(k_out), n_sc, sum(k_sc)])
        per_stage = list(zip(stages, split(s_ins, k_in), split(s_outs, k_out), split(s_scr, k_sc)))
        i = pl.program_id(0)
        if stages:
            @pl.when(i == 0)
            def _():
                x, y, c, chips = _place()
                peers = ([(x, y, 1 - c)] if 'sib' in kinds else []) + ([(cx, cy, c) for cx, cy in chips] if 'chips' in kinds else [])
                barrier = pltpu.get_barrier_semaphore()
                for peer in peers:
                    pl.semaphore_signal(barrier, inc=1, device_id=peer, device_id_type=MESH)
                pl.semaphore_wait(barrier, len(peers))
                for s, a, b, c_ in per_stage:
                    s.start(a, b, c_)

        body(*ins, *outs, *scr)
        if stages:
            @pl.when(i == last // 2)
            def _():
                for s, a, b, c in per_stage:
                    s.relay(a, b, c)

            @pl.when(i == max(last - 1, 0))
            def _():
                for s, a, b, c in per_stage:
                    s.mid(a, b, c)

            @pl.when(i == last)
            def _():
                for s, a, b, c in per_stage:
                    s.end(a, b, c)

    res = pl.pallas_call(
        full, name=name, grid=grid,
        in_specs=in_specs + [ANY] * sum(k_in),
        out_specs=out_specs + [ANY] * sum(k_out),
        out_shape=out_shape + [o for s in stages for o in s.out_shape],
        scratch_shapes=scratch_shapes + [x for s in stages for x in s.scratch],
        compiler_params=pltpu.CompilerParams(dimension_semantics=("arbitrary",), vmem_limit_bytes=VMEM_LIMIT,
                                             collective_id=collective_id),
    )(*args, *[a for s in stages for a in s.inputs])
    return list(res[:n_out]), split(list(res[n_out:]), k_out)


def ffn_fwd_a(x, g_pre, w_gu_t, name, stages=()):
    tm, tn = MM_ROWS, 256
    n_w = len(w_gu_t)

    def body(x_ref, gp_ref, *refs):
        w_refs = refs[:n_w]
        n_ref, g_ref, u_ref, a_ref, wt_ref, sem = refs[n_w:]

        @pl.when(pl.program_id(0) == 0)
        def _():
            _load_weight(w_refs, wt_ref, sem)

        xv = x_ref[...]
        n = (xv * _rsqrt_mean_sq(xv) * gp_ref[...]).astype(BF16)
        n_ref[...] = n
        for j in range(D_FF // tn):
            g = _dot_nt(n, wt_ref[j * tn:(j + 1) * tn, :])
            u = _dot_nt(n, wt_ref[D_FF + j * tn:D_FF + (j + 1) * tn, :])
            g_ref[:, j * tn:(j + 1) * tn] = g.astype(BF16)
            u_ref[:, j * tn:(j + 1) * tn] = u.astype(BF16)
            a_ref[:, j * tn:(j + 1) * tn] = (g * _sigmoid(g) * u).astype(BF16)

    wide = pl.BlockSpec((tm, D_FF), lambda i: (i, 0))
    return _call(
        body, name=name, grid=(SEQ // tm,),
        in_specs=[_ROW(tm), _VEC] + [ANY] * n_w,
        out_specs=[_ROW(tm), wide, wide, wide],
        out_shape=[jax.ShapeDtypeStruct((SEQ, D_MODEL), BF16)] + [jax.ShapeDtypeStruct((SEQ, D_FF), BF16)] * 3,
        scratch_shapes=_weight_scratch(2 * D_FF, n_w),
        args=[x, g_pre, *w_gu_t], stages=stages)


def ffn_fwd_b(a, w_down, g_post, h_in, name, target=None, stages=()):
    tm = MM_ROWS
    final = target is not None

    def body(*refs):
        if final:
            a_ref, wf_ref, gp_ref, h_ref, t_ref, f_ref, o_ref, loss_ref, wd_ref, sem = refs
        else:
            a_ref, wf_ref, gp_ref, h_ref, f_ref, o_ref, wd_ref, sem = refs

        @pl.when(pl.program_id(0) == 0)
        def _():
            _load_weight(wf_ref, wd_ref, sem)
            if final:
                loss_ref[...] = jnp.zeros_like(loss_ref)

        f = _dot(a_ref[...], wd_ref[...])
        f_ref[...] = f
        y = h_ref[...] + MACARON * (f * _rsqrt_mean_sq(f) * gp_ref[...])
        if final:
            err = y - t_ref[...]
            o_ref[...] = err * (1.0 / D_MODEL)
            loss_ref[...] += 0.5 * jnp.sum(err * err) * (1.0 / D_MODEL)
        else:
            o_ref[...] = y

    row = _ROW(tm)
    in_specs = [pl.BlockSpec((tm, D_FF), lambda i: (i, 0)), ANY, _VEC, row]
    out_specs = [row, row]
    out_shape = [jax.ShapeDtypeStruct((SEQ, D_MODEL), F32)] * 2
    args = [a, w_down, g_post, h_in]
    if final:
        in_specs.append(row)
        args.append(target)
        out_specs.append(pl.BlockSpec((8, 128), lambda i: (0, 0)))
        out_shape.append(jax.ShapeDtypeStruct((8, 128), F32))
    return _call(body, name=name, grid=(SEQ // tm,), in_specs=in_specs, out_specs=out_specs,
                 out_shape=out_shape, scratch_shapes=_weight_scratch(D_FF), args=args, stages=stages)


def ffn_bwd_a(d_out, f, g_post, w_down, g, u, name, stages=()):
    tm = MM_ROWS
    tc = 256

    def body(do_ref, f_ref, gp_ref, wf_ref, g_ref, u_ref, df_ref, dgu_ref, dgp_ref, wd_ref, sem):
        @pl.when(pl.program_id(0) == 0)
        def _():
            _load_weight(wf_ref, wd_ref, sem)
            dgp_ref[...] = jnp.zeros_like(dgp_ref)

        fv = f_ref[...]
        rf = _rsqrt_mean_sq(fv)
        fh = fv * rf
        dn = MACARON * do_ref[...]
        dgp_ref[...] += jnp.sum(dn * fh, axis=0, keepdims=True)
        t = dn * gp_ref[...]
        df = (rf * (t - fh * jnp.mean(t * fh, axis=-1, keepdims=True))).astype(BF16)
        df_ref[...] = df
        for c0 in range(0, D_FF, tc):
            da = _dot_nt(df, wd_ref[c0:c0 + tc, :])
            gv = g_ref[:, c0:c0 + tc].astype(F32)
            uv = u_ref[:, c0:c0 + tc].astype(F32)
            s = _sigmoid(gv)
            dgu_ref[:, c0:c0 + tc] = (da * uv * s * (1.0 + gv * (1.0 - s))).astype(BF16)
            dgu_ref[:, D_FF + c0:D_FF + c0 + tc] = (da * gv * s).astype(BF16)

    row = _ROW(tm)
    wide = pl.BlockSpec((tm, D_FF), lambda i: (i, 0))
    return _call(
        body, name=name, grid=(SEQ // tm,),
        in_specs=[row, row, _VEC, ANY, wide, wide],
        out_specs=[row, pl.BlockSpec((tm, 2 * D_FF), lambda i: (i, 0)), _VEC],
        out_shape=[jax.ShapeDtypeStruct((SEQ, D_MODEL), BF16), jax.ShapeDtypeStruct((SEQ, 2 * D_FF), BF16),
                   jax.ShapeDtypeStruct((1, D_MODEL), F32)],
        scratch_shapes=_weight_scratch(D_FF),
        args=[d_out, f, g_post, w_down, g, u], stages=stages)


def norm_bwd(pieces, w_t, x, g_pre, d_res, name, stages=()):
    tm = MM_ROWS
    widths = [p.shape[1] for p in pieces]
    offs = [sum(widths[:k]) for k in range(len(widths))]
    n_p = len(pieces)
    n_w = len(w_t)

    def body(*refs):
        p_refs = refs[:n_p]
        w_refs = refs[n_p:n_p + n_w]
        x_ref, g_ref, r_ref, dx_ref, dg_ref, wt_ref, sem = refs[n_p + n_w:]

        @pl.when(pl.program_id(0) == 0)
        def _():
            _load_weight(w_refs, wt_ref, sem)
            dg_ref[...] = jnp.zeros_like(dg_ref)

        dn = None
        for p_ref, lo, wd in zip(p_refs, offs, widths):
            part = _dot(p_ref[...], wt_ref[lo:lo + wd, :])
            dn = part if dn is None else dn + part
        xv = x_ref[...]
        r = _rsqrt_mean_sq(xv)
        xh = xv * r
        dg_ref[...] += jnp.sum(dn * xh, axis=0, keepdims=True)
        t = dn * g_ref[...]
        dx_ref[...] = r_ref[...] + r * (t - xh * jnp.mean(t * xh, axis=-1, keepdims=True))

    row = _ROW(tm)
    return _call(
        body, name=name, grid=(SEQ // tm,),
        in_specs=[pl.BlockSpec((tm, wd), lambda i: (i, 0)) for wd in widths] + [ANY] * n_w + [row, _VEC, row],
        out_specs=[row, _VEC],
        out_shape=[jax.ShapeDtypeStruct((SEQ, D_MODEL), F32), jax.ShapeDtypeStruct((1, D_MODEL), F32)],
        scratch_shapes=_weight_scratch(sum(widths), n_w),
        args=[*pieces, *w_t, x, g_pre, d_res], stages=stages)


def mm_tn(pieces, b, tm, name, stages=()):
    widths = [p.shape[1] for p in pieces]
    m_total = sum(widths)
    n_p = len(pieces)
    starts = [sum(widths[:k]) // tm for k in range(n_p)]
    counts = [wd // tm for wd in widths]

    def body(*refs):
        p_refs = refs[:n_p]
        b_ref, o_ref = refs[n_p:]
        i = pl.program_id(0)
        for p_ref, st, ct in zip(p_refs, starts, counts):
            @pl.when((i >= st) & (i < st + ct))
            def _(p_ref=p_ref):
                o_ref[...] = _dot_tn(p_ref[...], b_ref[...]).astype(BF16)

    def piece_spec(st, ct):
        return pl.BlockSpec((SEQ, tm), lambda i: (0, jnp.clip(i - st, 0, ct - 1)))

    (out,), stage_out = _call(
        body, name=name, grid=(m_total // tm,),
        in_specs=[piece_spec(st, ct) for st, ct in zip(starts, counts)] + [pl.BlockSpec((SEQ, D_MODEL), lambda i: (0, 0))],
        out_specs=[pl.BlockSpec((tm, D_MODEL), lambda i: (i, 0))],
        out_shape=[jax.ShapeDtypeStruct((m_total, D_MODEL), BF16)],
        args=[*pieces, b], stages=stages)
    return out, stage_out


def mix_in(h, g_pre, w_in_t, name, stages=()):
    tm = MM_ROWS
    offs = [sum(IN_SEGS[:k]) for k in range(len(IN_SEGS))]
    dts = [F32, F32, F32, F32, BF16, F32, F32]
    n_o = len(IN_SEGS)

    def body(*refs):
        h_ref, g_ref, wf_ref, um_ref = refs[:4]
        o_refs = refs[4:4 + n_o]
        wt_ref, sem = refs[4 + n_o:]

        @pl.when(pl.program_id(0) == 0)
        def _():
            _load_weight(wf_ref, wt_ref, sem)

        hv = h_ref[...]
        um = (hv * _rsqrt_mean_sq(hv) * g_ref[...]).astype(BF16)
        um_ref[...] = um
        for o_ref, lo, wd in zip(o_refs, offs, IN_SEGS):
            for c0 in range(0, wd, 256):
                o_ref[:, c0:c0 + 256] = _dot_nt(um, wt_ref[lo + c0:lo + c0 + 256, :]).astype(o_ref.dtype)

    return _call(
        body, name=name, grid=(SEQ // tm,),
        in_specs=[_ROW(tm), _VEC, ANY],
        out_specs=[_ROW(tm)] + [pl.BlockSpec((tm, wd), lambda i: (i, 0)) for wd in IN_SEGS],
        out_shape=[jax.ShapeDtypeStruct((SEQ, D_MODEL), BF16)]
        + [jax.ShapeDtypeStruct((SEQ, wd), dt) for wd, dt in zip(IN_SEGS, dts)],
        scratch_shapes=_weight_scratch(IN_W),
        args=[h, g_pre, w_in_t], stages=stages)


LRU_TC = 256


def _conv_fwd(xb, cw, cb, tt):
    xc = xb * cw[3:4, :] + cb
    shifted = []
    for s in (1, 2, 3):
        sh = jnp.where(tt >= s, pltpu.roll(xb, s, 0), 0.0)
        shifted.append(sh)
        xc = xc + sh * cw[3 - s:4 - s, :]
    return xc, shifted


def _lru_gates(xc, wa, ba, wx, bx, lam):
    xcb = xc.astype(BF16)
    r = _sigmoid(_dot(xcb, wa) + ba)
    i = _sigmoid(_dot(xcb, wx) + bx)
    nl = -lam
    sp = jnp.maximum(nl, 0.0) + jnp.log1p(jnp.exp(-jnp.abs(nl)))
    la = (-LRU_C * r) * sp
    a = jnp.exp(la)
    mult = jnp.sqrt(jnp.maximum(-_expm1(2.0 * la), 0.0))
    return xcb, r, i, sp, a, mult


def _scan(a, b, tt, reverse, a_s, b_s):
    n = a.shape[0]
    tg = tt & 7
    for s in (1, 2, 4):
        keep = (tg < 8 - s) if reverse else (tg >= s)
        shift = n - s if reverse else s
        b = a * jnp.where(keep, pltpu.roll(b, shift, 0), 0.0) + b
        a = a * jnp.where(keep, pltpu.roll(a, shift, 0), 1.0)
    a_s[...] = a
    b_s[...] = b
    groups = n // 8

    def step(g, carry):
        gi = (groups - 1 - g) if reverse else g
        rows = pl.ds(pl.multiple_of(gi * 8, 8), 8)
        hg = a_s[rows, :] * carry + b_s[rows, :]
        b_s[rows, :] = hg
        return hg[0:1, :] if reverse else hg[7:8, :]

    lax.fori_loop(0, groups, step, jnp.zeros((1, a.shape[1]), F32), unroll=8)
    return b_s[...]


def _lru_specs():
    col = pl.BlockSpec((SEQ, LRU_TC), lambda j: (0, j))
    vec = pl.BlockSpec((1, LRU_TC), lambda j: (0, j))
    bd = pl.BlockSpec((1, LRU_TC, LRU_TC), lambda j: (j, 0, 0))
    cw = pl.BlockSpec((4, LRU_TC), lambda j: (0, j))
    return col, vec, bd, cw


def lru_fwd(gate, xbr, conv_w, conv_b, wa_bd, b_a, wx_bd, b_x, lam, name, stages=()):
    col, vec, bd, cw = _lru_specs()

    def body(gate_ref, xbr_ref, cw_ref, cb_ref, wa_ref, ba_ref, wx_ref, bx_ref, lam_ref, y_ref, h_ref, a_s, b_s):
        tt = lax.broadcasted_iota(jnp.int32, (SEQ, LRU_TC), 0)
        xc, _ = _conv_fwd(xbr_ref[...], cw_ref[...], cb_ref[...], tt)
        _, r, i, sp, a, mult = _lru_gates(xc, wa_ref[0], ba_ref[...], wx_ref[0], bx_ref[...], lam_ref[...])
        h = _scan(a, mult * (i * xc), tt, False, a_s, b_s)
        h_ref[...] = h
        gl, _ = _gelu(gate_ref[...])
        y_ref[...] = (h * gl).astype(BF16)

    return _call(
        body, name=name, grid=(LRU_W // LRU_TC,),
        in_specs=[col, col, cw, vec, bd, vec, bd, vec, vec],
        out_specs=[col, col],
        out_shape=[jax.ShapeDtypeStruct((SEQ, LRU_W), BF16), jax.ShapeDtypeStruct((SEQ, LRU_W), F32)],
        scratch_shapes=[pltpu.VMEM((SEQ, LRU_TC), F32)] * 2,
        args=[gate, xbr, conv_w, conv_b, wa_bd, b_a, wx_bd, b_x, lam], stages=stages)


def lru_bwd(gate, xbr, h, dy, conv_w, conv_b, wa_bd, b_a, wx_bd, b_x, lam, name, stages=()):
    col, vec, bd, cw = _lru_specs()

    def body(gate_ref, xbr_ref, h_ref, dy_ref, cw_ref, cb_ref, wa_ref, ba_ref, wx_ref, bx_ref, lam_ref,
             dgate_ref, dxbr_ref, vecs_ref, dwa_ref, dwx_ref, a_s, b_s):
        tt = lax.broadcasted_iota(jnp.int32, (SEQ, LRU_TC), 0)
        cwv = cw_ref[...]
        lam = lam_ref[...]
        xb = xbr_ref[...]
        xc, shifted = _conv_fwd(xb, cwv, cb_ref[...], tt)
        wa = wa_ref[0]
        wx = wx_ref[0]
        xcb, r, i, sp, a, mult = _lru_gates(xc, wa, ba_ref[...], wx, bx_ref[...], lam)
        hv = h_ref[...]
        dyv = dy_ref[...]
        gv = gate_ref[...]
        gl, th = _gelu(gv)
        dgate_ref[...] = (dyv * hv * _gelu_grad(gv, th)).astype(BF16)
        a_next = jnp.where(tt < SEQ - 1, pltpu.roll(a, SEQ - 1, 0), 0.0)
        gsum = _scan(a_next, dyv * gl, tt, True, a_s, b_s)
        h_prev = jnp.where(tt >= 1, pltpu.roll(hv, 1, 0), 0.0)
        d_mult = gsum * i * xc
        d_i = gsum * mult * xc
        d_xc = gsum * mult * i
        d_la = gsum * h_prev * a - d_mult * (a * a) / mult
        d_pr = (d_la * (-LRU_C * sp)) * r * (1.0 - r)
        d_pi = d_i * i * (1.0 - i)
        d_lam = jnp.sum(d_la * r, axis=0, keepdims=True) * (LRU_C * _sigmoid(-lam))
        d_prb = d_pr.astype(BF16)
        d_pib = d_pi.astype(BF16)
        d_xc = d_xc + _dot_nt(d_prb, wa) + _dot_nt(d_pib, wx)
        dwa_ref[0] = _dot_tn(xcb, d_prb)
        dwx_ref[0] = _dot_tn(xcb, d_pib)
        rows = [jnp.sum(d_xc * shifted[2], axis=0, keepdims=True),
                jnp.sum(d_xc * shifted[1], axis=0, keepdims=True),
                jnp.sum(d_xc * shifted[0], axis=0, keepdims=True),
                jnp.sum(d_xc * xb, axis=0, keepdims=True),
                jnp.sum(d_xc, axis=0, keepdims=True),
                jnp.sum(d_pr, axis=0, keepdims=True),
                jnp.sum(d_pi, axis=0, keepdims=True),
                d_lam]
        ri = lax.broadcasted_iota(jnp.int32, (8, LRU_TC), 0)
        acc = jnp.zeros((8, LRU_TC), F32)
        for k, rv in enumerate(rows):
            acc = jnp.where(ri == k, rv, acc)
        vecs_ref[...] = acc
        d_xb = d_xc * cwv[3:4, :]
        for s in (1, 2, 3):
            d_xb = d_xb + jnp.where(tt < SEQ - s, pltpu.roll(d_xc, SEQ - s, 0), 0.0) * cwv[3 - s:4 - s, :]
        dxbr_ref[...] = d_xb.astype(BF16)

    return _call(
        body, name=name, grid=(LRU_W // LRU_TC,),
        in_specs=[col, col, col, col, cw, vec, bd, vec, bd, vec, vec],
        out_specs=[col, col, pl.BlockSpec((8, LRU_TC), lambda j: (0, j)), bd, bd],
        out_shape=[jax.ShapeDtypeStruct((SEQ, LRU_W), BF16), jax.ShapeDtypeStruct((SEQ, LRU_W), BF16),
                   jax.ShapeDtypeStruct((8, LRU_W), F32),
                   jax.ShapeDtypeStruct((LRU_W // LRU_TC, LRU_TC, LRU_TC), F32),
                   jax.ShapeDtypeStruct((LRU_W // LRU_TC, LRU_TC, LRU_TC), F32)],
        scratch_shapes=[pltpu.VMEM((SEQ, LRU_TC), F32)] * 2,
        args=[gate, xbr, h, dy, conv_w, conv_b, wa_bd, b_a, wx_bd, b_x, lam], stages=stages)


def _rope(x, cos, sin_signed):
    w = x.shape[1]
    reps = w // 128
    if reps > 1:
        cos = jnp.tile(cos, (1, reps))
        sin_signed = jnp.tile(sin_signed, (1, reps))
    lane = lax.broadcasted_iota(jnp.int32, x.shape, 1)
    first = (lane & 63) < 32
    partner = jnp.where(first, pltpu.roll(x, w - 32, 1), pltpu.roll(x, 32, 1))
    return x * cos + partner * sin_signed


def _both_halves(t, odd):
    lo = lax.broadcasted_iota(jnp.int32, t.shape, 1) < 64
    rolled = pltpu.roll(t, 64, 1)
    return jnp.where(lo, rolled, t) if odd else jnp.where(lo, t, rolled)


def _stack_heads(ta, tb):
    lo = lax.broadcasted_iota(jnp.int32, ta.shape, 1) < 64
    return jnp.concatenate([jnp.where(lo, ta, 0.0), jnp.where(lo, 0.0, ta),
                            jnp.where(lo, tb, 0.0), jnp.where(lo, 0.0, tb)], axis=0)


def _unstack_heads(o):
    lo = lax.broadcasted_iota(jnp.int32, (ATTN_BLOCK, 128), 1) < 64
    return (jnp.where(lo, o[0:128], o[128:256]), jnp.where(lo, o[256:384], o[384:512]))


def _window_upper_t():
    shape = (ATTN_BLOCK, 4 * ATTN_BLOCK)
    return lax.broadcasted_iota(jnp.int32, shape, 0) > (lax.broadcasted_iota(jnp.int32, shape, 1) & (ATTN_BLOCK - 1))


def _fold_t(t, upper_t):
    return jnp.where(upper_t, t[:ATTN_BLOCK], t[ATTN_BLOCK:])


def _unfold_t(t, upper_t):
    zero = jnp.zeros_like(t)
    return jnp.concatenate([jnp.where(upper_t, t, zero), jnp.where(upper_t, zero, t)], axis=0)


def _attn_probs_t(kd, qs, sinks_ref, hk, first_block, upper_t):
    s = _fold_t(_dot_nt(kd, qs), upper_t) * (HEAD_DIM ** -0.5)
    s = jnp.where(jnp.logical_and(upper_t, first_block), MASK_VALUE, s)
    rg = lax.broadcasted_iota(jnp.int32, (1, 4 * ATTN_BLOCK), 1) >> 7
    sink = jnp.where(rg == 0, sinks_ref[4 * hk],
                     jnp.where(rg == 1, sinks_ref[4 * hk + 1],
                               jnp.where(rg == 2, sinks_ref[4 * hk + 2], sinks_ref[4 * hk + 3])))
    m = jnp.maximum(jnp.max(s, axis=0, keepdims=True), sink)
    e = jnp.exp(s - m)
    es = jnp.exp(sink - m)
    inv = 1.0 / (jnp.sum(e, axis=0, keepdims=True) + es)
    return e * inv, es * inv


def _prev(i):
    return jnp.maximum(i - 1, 0)


def attn_fwd(q, k, v, cos, sin_signed, sinks, name, stages=()):
    nb = ATTN_BLOCK

    def body(q_ref, kc_ref, kp_ref, vc_ref, vp_ref, cc_ref, sc_ref, cp_ref, sp_ref, sinks_ref,
             qr_ref, kr_ref, y_ref):
        first_block = pl.program_id(0) == 0
        qr = _rope(q_ref[...], cc_ref[...], sc_ref[...])
        kc = _rope(kc_ref[...], cc_ref[...], sc_ref[...])
        kp = _rope(kp_ref[...], cp_ref[...], sp_ref[...])
        qr_ref[...] = qr.astype(BF16)
        kr_ref[...] = kc.astype(BF16)
        k2 = jnp.concatenate([kp, kc], axis=0)
        v2 = jnp.concatenate([vp_ref[...].astype(F32), vc_ref[...].astype(F32)], axis=0)
        upper_t = _window_upper_t()
        for hk in range(N_KV_HEADS):
            kt = hk // 2
            kd = _both_halves(k2[:, kt * 128:(kt + 1) * 128], hk % 2).astype(BF16)
            vd = _both_halves(v2[:, kt * 128:(kt + 1) * 128], hk % 2).astype(BF16)
            qs = _stack_heads(qr[:, (2 * hk) * 128:(2 * hk + 1) * 128],
                              qr[:, (2 * hk + 1) * 128:(2 * hk + 2) * 128]).astype(BF16)
            p, _ = _attn_probs_t(kd, qs, sinks_ref, hk, first_block, upper_t)
            ta, tb = _unstack_heads(_dot_tn(_unfold_t(p.astype(BF16), upper_t), vd))
            y_ref[:, (2 * hk) * 128:(2 * hk + 1) * 128] = ta.astype(BF16)
            y_ref[:, (2 * hk + 1) * 128:(2 * hk + 2) * 128] = tb.astype(BF16)

    cur = lambda w: pl.BlockSpec((nb, w), lambda i: (i, 0))
    prv = lambda w: pl.BlockSpec((nb, w), lambda i: (_prev(i), 0))
    return _call(
        body, name=name, grid=(N_ATTN_BLOCKS,),
        in_specs=[cur(D_MODEL), cur(KV_W), prv(KV_W), cur(KV_W), prv(KV_W), cur(128), cur(128), prv(128), prv(128),
                  pl.BlockSpec(memory_space=pltpu.SMEM)],
        out_specs=[cur(D_MODEL), cur(KV_W), cur(D_MODEL)],
        out_shape=[jax.ShapeDtypeStruct((SEQ, D_MODEL), BF16), jax.ShapeDtypeStruct((SEQ, KV_W), BF16),
                   jax.ShapeDtypeStruct((SEQ, D_MODEL), BF16)],
        args=[q, k, k, v, v, cos, sin_signed, cos, sin_signed, sinks], stages=stages)


def attn_bwd(qr, kr, v, dy, cos, sin_signed, sinks, name, stages=()):
    nb = ATTN_BLOCK
    n_steps = N_ATTN_BLOCKS + 1
    scale = HEAD_DIM ** -0.5

    def body(q_ref, kc_ref, kp_ref, vc_ref, vp_ref, dy_ref, cc_ref, sc_ref, cp_ref, sp_ref, sinks_ref,
             dq_ref, dkv_ref, dsk_ref, ck_ref, cv_ref):
        dk_ref = dkv_ref.at[:, pl.ds(0, KV_W)]
        dv_ref = dkv_ref.at[:, pl.ds(KV_W, KV_W)]
        i = pl.program_id(0)

        @pl.when(i == 0)
        def _():
            dsk_ref[...] = jnp.zeros_like(dsk_ref)
            ck_ref[...] = jnp.zeros_like(ck_ref)
            cv_ref[...] = jnp.zeros_like(cv_ref)

        @pl.when(i < N_ATTN_BLOCKS)
        def _():
            qv = q_ref[...].astype(F32)
            dov = dy_ref[...].astype(F32)
            k2 = jnp.concatenate([kp_ref[...].astype(F32), kc_ref[...].astype(F32)], axis=0)
            v2 = jnp.concatenate([vp_ref[...].astype(F32), vc_ref[...].astype(F32)], axis=0)
            lane = lax.broadcasted_iota(jnp.int32, (8, 128), 1)
            lo = lax.broadcasted_iota(jnp.int32, (2 * nb, 128), 1) < 64
            dsk = jnp.zeros((8, 128), F32)
            dk_tiles = []
            dv_tiles = []
            upper_t = _window_upper_t()
            for hk in range(N_KV_HEADS):
                kt = hk // 2
                kd = _both_halves(k2[:, kt * 128:(kt + 1) * 128], hk % 2).astype(BF16)
                vd = _both_halves(v2[:, kt * 128:(kt + 1) * 128], hk % 2).astype(BF16)
                qs = _stack_heads(qv[:, (2 * hk) * 128:(2 * hk + 1) * 128],
                                  qv[:, (2 * hk + 1) * 128:(2 * hk + 2) * 128]).astype(BF16)
                dos = _stack_heads(dov[:, (2 * hk) * 128:(2 * hk + 1) * 128],
                                   dov[:, (2 * hk + 1) * 128:(2 * hk + 2) * 128]).astype(BF16)
                p, ps = _attn_probs_t(kd, qs, sinks_ref, hk, i == 0, upper_t)
                dp = _fold_t(_dot_nt(vd, dos), upper_t)
                delta = jnp.sum(p * dp, axis=0, keepdims=True)
                ds = _unfold_t((p * (dp - delta)).astype(BF16), upper_t)
                dsink = -ps * delta
                for g in range(4):
                    dsk = dsk + jnp.where(lane == 4 * hk + g, jnp.sum(dsink[:, g * nb:(g + 1) * nb]), 0.0)
                ta, tb = _unstack_heads(_dot_tn(ds, kd) * scale)
                dq_a = (2 * hk) * 128
                dq_ref[:, dq_a:dq_a + 128] = _rope(ta, cc_ref[...], -sc_ref[...]).astype(BF16)
                dq_ref[:, dq_a + 128:dq_a + 256] = _rope(tb, cc_ref[...], -sc_ref[...]).astype(BF16)
                rk = _dot(ds, qs) * scale
                rv = _dot(_unfold_t(p.astype(BF16), upper_t), dos)
                dk_tiles.append(rk + pltpu.roll(rk, 64, 1))
                dv_tiles.append(rv + pltpu.roll(rv, 64, 1))
            dsk_ref[...] += dsk
            dk_full = jnp.concatenate([jnp.where(lo, dk_tiles[0], dk_tiles[1]),
                                       jnp.where(lo, dk_tiles[2], dk_tiles[3])], axis=1)
            dv_full = jnp.concatenate([jnp.where(lo, dv_tiles[0], dv_tiles[1]),
                                       jnp.where(lo, dv_tiles[2], dv_tiles[3])], axis=1)
            dk_ref[...] = _rope(ck_ref[...] + dk_full[0:nb], cp_ref[...], -sp_ref[...]).astype(BF16)
            dv_ref[...] = (cv_ref[...] + dv_full[0:nb]).astype(BF16)
            ck_ref[...] = dk_full[nb:2 * nb]
            cv_ref[...] = dv_full[nb:2 * nb]

        @pl.when(i == N_ATTN_BLOCKS)
        def _():
            dk_ref[...] = _rope(ck_ref[...], cp_ref[...], -sp_ref[...]).astype(BF16)
            dv_ref[...] = cv_ref[...].astype(BF16)

    qi = lambda i: jnp.minimum(i, N_ATTN_BLOCKS - 1)
    cur = lambda w: pl.BlockSpec((nb, w), lambda i: (qi(i), 0))
    prv = lambda w: pl.BlockSpec((nb, w), lambda i: (_prev(qi(i)), 0))
    out_prev = lambda w: pl.BlockSpec((nb, w), lambda i: (_prev(i), 0))
    return _call(
        body, name=name, grid=(n_steps,),
        in_specs=[cur(D_MODEL), cur(KV_W), prv(KV_W), cur(KV_W), prv(KV_W), cur(D_MODEL),
                  cur(128), cur(128), out_prev(128), out_prev(128), pl.BlockSpec(memory_space=pltpu.SMEM)],
        out_specs=[cur(D_MODEL), out_prev(2 * KV_W), pl.BlockSpec((8, 128), lambda i: (0, 0))],
        out_shape=[jax.ShapeDtypeStruct((SEQ, D_MODEL), BF16), jax.ShapeDtypeStruct((SEQ, 2 * KV_W), BF16),
                   jax.ShapeDtypeStruct((8, 128), F32)],
        scratch_shapes=[pltpu.VMEM((nb, KV_W), F32), pltpu.VMEM((nb, KV_W), F32)],
        args=[qr, kr, kr, v, v, dy, cos, sin_signed, cos, sin_signed, sinks], stages=stages)


def _proj_scratch():
    return [pltpu.VMEM((D_MODEL, D_MODEL), BF16)] * 3 + [pltpu.SemaphoreType.DMA((3 * N_CHIPS,))]


def _load_projs(w_refs, wl_ref, wa_ref, wo_ref, sem):
    for k, (w_ref, dst) in enumerate(zip(w_refs, (wl_ref, wa_ref, wo_ref))):
        _load_weight(w_ref, dst, sem.at[pl.ds(k * N_CHIPS, N_CHIPS)])


def merge_fwd(y_lru, y_attn, g_lru, g_attn, projs, g_post, h_in, name, stages=()):
    tm = MM_ROWS

    def body(yl_ref, ya_ref, gl_ref, ga_ref, w1_ref, w2_ref, w3_ref, gp_ref, h_ref,
             pl_ref, pa_ref, mg_ref, m_ref, o_ref, wl_ref, wa_ref, wo_ref, sem):
        @pl.when(pl.program_id(0) == 0)
        def _():
            _load_projs((w1_ref, w2_ref, w3_ref), wl_ref, wa_ref, wo_ref, sem)

        p_l = _dot(yl_ref[...], wl_ref[...])
        p_a = _dot(ya_ref[...], wa_ref[...])
        pl_ref[...] = p_l.astype(BF16)
        pa_ref[...] = p_a.astype(BF16)
        merged = (_sigmoid(gl_ref[...]) * p_l + _sigmoid(ga_ref[...]) * p_a).astype(BF16)
        mg_ref[...] = merged
        m = _dot(merged, wo_ref[...])
        m_ref[...] = m
        o_ref[...] = h_ref[...] + m * _rsqrt_mean_sq(m) * gp_ref[...]

    row = _ROW(tm)
    return _call(
        body, name=name, grid=(SEQ // tm,),
        in_specs=[row, row, row, row, ANY, ANY, ANY, _VEC, row],
        out_specs=[row] * 5,
        out_shape=[jax.ShapeDtypeStruct((SEQ, D_MODEL), BF16)] * 3 + [jax.ShapeDtypeStruct((SEQ, D_MODEL), F32)] * 2,
        scratch_shapes=_proj_scratch(),
        args=[y_lru, y_attn, g_lru, g_attn, *projs, g_post, h_in], stages=stages)


def merge_bwd(d_out, m, g_post, projs, g_lru, g_attn, p_l, p_a, name, stages=()):
    tm = 256

    def body(do_ref, m_ref, gp_ref, w1_ref, w2_ref, w3_ref, gl_ref, ga_ref, pl_ref, pa_ref,
             dm_ref, dpl_ref, dpa_ref, dgl_ref, dga_ref, dya_ref, dyl_ref, dgp_ref, wl_ref, wa_ref, wo_ref, sem):
        @pl.when(pl.program_id(0) == 0)
        def _():
            _load_projs((w1_ref, w2_ref, w3_ref), wl_ref, wa_ref, wo_ref, sem)
            dgp_ref[...] = jnp.zeros_like(dgp_ref)

        mv = m_ref[...]
        rm = _rsqrt_mean_sq(mv)
        mh = mv * rm
        dn = do_ref[...]
        dgp_ref[...] += jnp.sum(dn * mh, axis=0, keepdims=True)
        t = dn * gp_ref[...]
        dm = (rm * (t - mh * jnp.mean(t * mh, axis=-1, keepdims=True))).astype(BF16)
        dm_ref[...] = dm
        dmg = _dot_nt(dm, wo_ref[...])
        sl = _sigmoid(gl_ref[...])
        sa = _sigmoid(ga_ref[...])
        dpl = (dmg * sl).astype(BF16)
        dpa = (dmg * sa).astype(BF16)
        dpl_ref[...] = dpl
        dpa_ref[...] = dpa
        dgl_ref[...] = (dmg * pl_ref[...].astype(F32) * sl * (1.0 - sl)).astype(BF16)
        dga_ref[...] = (dmg * pa_ref[...].astype(F32) * sa * (1.0 - sa)).astype(BF16)
        dyl_ref[...] = _dot_nt(dpl, wl_ref[...])
        dya_ref[...] = _dot_nt(dpa, wa_ref[...]).astype(BF16)

    row = _ROW(tm)
    return _call(
        body, name=name, grid=(SEQ // tm,),
        in_specs=[row, row, _VEC, ANY, ANY, ANY, row, row, row, row],
        out_specs=[row] * 7 + [_VEC],
        out_shape=[jax.ShapeDtypeStruct((SEQ, D_MODEL), BF16)] * 6 + [jax.ShapeDtypeStruct((SEQ, D_MODEL), F32),
                                                                       jax.ShapeDtypeStruct((1, D_MODEL), F32)],
        scratch_shapes=_proj_scratch(),
        args=[d_out, m, g_post, *projs, g_lru, g_attn, p_l, p_a], stages=stages)


def _rope_tables():
    half = HEAD_DIM // 2
    inv_freq = np.float32(ROPE_THETA) ** (-np.arange(half, dtype=np.float32) / np.float32(half))
    ang = np.arange(SEQ, dtype=np.float32)[:, None] * inv_freq[None, :]
    cos, sin = np.cos(ang), np.sin(ang)
    return (jnp.asarray(np.tile(np.concatenate([cos, cos], axis=1), (1, 2))),
            jnp.asarray(np.tile(np.concatenate([-sin, sin], axis=1), (1, 2))))


def _block_diag(w):
    per = LRU_TC // LRU_BLOCK_W
    w4 = w.reshape(LRU_W // LRU_TC, per, LRU_BLOCK_W, LRU_BLOCK_W)
    eye = jnp.eye(per, dtype=w.dtype)
    return jnp.einsum('jacd,ab->jacbd', w4, eye).reshape(LRU_W // LRU_TC, LRU_TC, LRU_TC).astype(BF16)


def _diag_blocks(p):
    per = LRU_TC // LRU_BLOCK_W
    p5 = p.reshape(LRU_W // LRU_TC, per, LRU_BLOCK_W, per, LRU_BLOCK_W)
    return jnp.stack([p5[:, a, :, a, :] for a in range(per)], axis=1).reshape(LRU_W // LRU_BLOCK_W, LRU_BLOCK_W, LRU_BLOCK_W)


def _place():
    x, y, c = lax.axis_index('x'), lax.axis_index('y'), lax.axis_index('c')
    chips = [(1 - x, y), (x, 1 - y), (1 - x, 1 - y)]
    return x, y, c, chips


def _rcopy(src, dst, send_sem, recv_sem, to):
    return pltpu.make_async_remote_copy(src_ref=src, dst_ref=dst, send_sem=send_sem, recv_sem=recv_sem,
                                        device_id=to, device_id_type=MESH)


class _Stage:
    inputs, out_shape, scratch, peers = (), (), (), ()

    def start(self, ins, outs, scr):
        plan = self._plan(ins, outs, scr)
        for ld in plan['loads']:
            ld.start()
        for cp in plan['sends']:
            cp.start()

    def relay(self, ins, outs, scr):
        pass

    def mid(self, ins, outs, scr):
        plan = self._plan(ins, outs, scr)
        for ld, st in zip(plan['loads'], plan['stores']):
            ld.wait()
            st.start()
        for arrived, onward in zip(plan['arrivals'], plan['forwards']):
            arrived.wait_recv()
            onward.start()

    def end(self, ins, outs, scr):
        plan = self._plan(ins, outs, scr)
        for st in plan['stores']:
            st.wait()
        for arrived in (plan['final_arrivals'] if plan['forwards'] else plan['arrivals']):
            arrived.wait_recv()
        for cp in plan['sends'] + plan['forwards']:
            cp.wait_send()


def _empty_plan():
    return dict(loads=[], stores=[], sends=[], arrivals=[], forwards=[], final_arrivals=[])


class GatherStage(_Stage):
    peers = ('chips', 'sib')
    N_CP = 12

    def __init__(self, items):
        self.ranges = [(off, rows) for _, off, rows in items]
        self.inputs = [src for src, _, _ in items]
        self.out_shape = [jax.ShapeDtypeStruct((N_CHIPS, rows, D_MODEL), BF16) for _, rows in self.ranges]
        n = self.N_CP * len(items)
        self.scratch = [pltpu.VMEM((sum(r for _, r in self.ranges), D_MODEL), BF16), pltpu.SemaphoreType.DMA((n,)),
                        pltpu.SemaphoreType.DMA((n,)), pltpu.SemaphoreType.DMA((2 * len(items),))]

    def _plan(self, ins, outs, scr):
        buf, send, recv, lsem = scr
        x, y, c, _ = _place()
        me_q, q_x, q_y, q_d = 2 * x + y, 2 * (1 - x) + y, 2 * x + (1 - y), 2 * (1 - x) + (1 - y)
        to_x, to_y, sib = (1 - x, y, c), (x, 1 - y, c), (x, y, 1 - c)
        plan = dict(loads=[], stores=[], first=[], early=[], relays=[], late=[], hand_early=[], hand_late=[], final=[])
        boff = 0
        for w, ((off, rows), p_ref, o_ref) in enumerate(zip(self.ranges, ins, outs)):
            hr = rows // 2
            ch = hr // 2
            plan['loads'].append(pltpu.make_async_copy(p_ref.at[pl.ds(off, rows)], buf.at[pl.ds(boff, rows)], lsem.at[2 * w]))
            plan['stores'].append(pltpu.make_async_copy(buf.at[pl.ds(boff, rows)], o_ref.at[me_q], lsem.at[2 * w + 1]))
            boff += rows
            base = w * self.N_CP
            mine = [pl.ds(pl.multiple_of(c * hr + k * ch, 16), ch) for k in range(2)]
            theirs = [pl.ds(pl.multiple_of((1 - c) * hr + k * ch, 16), ch) for k in range(2)]
            src = [p_ref.at[pl.ds(pl.multiple_of(off + c * hr + k * ch, 16), ch)] for k in range(2)]

            def cp(k, s, d, to):
                return _rcopy(s, d, send.at[base + k], recv.at[base + k], to)

            def here(q, rows_):
                return o_ref.at[q, rows_]

            plan['first'] += [cp(0, src[0], here(me_q, mine[0]), to_x), cp(2, src[1], here(me_q, mine[1]), to_y),
                              cp(1, src[1], here(me_q, mine[1]), to_x), cp(3, src[0], here(me_q, mine[0]), to_y)]
            x_a, y_b = here(q_x, mine[0]), here(q_y, mine[1])
            plan['early'] += [cp(0, x_a, x_a, to_x), cp(2, y_b, y_b, to_y)]
            plan['relays'] += [cp(4, x_a, x_a, to_y), cp(5, y_b, y_b, to_x)]
            plan['hand_early'] += [cp(6, x_a, x_a, sib), cp(7, y_b, y_b, sib)]
            x_b, y_a, d_a, d_b = here(q_x, mine[1]), here(q_y, mine[0]), here(q_d, mine[0]), here(q_d, mine[1])
            plan['late'] += [cp(1, x_b, x_b, to_x), cp(3, y_a, y_a, to_y), cp(4, d_a, d_a, to_y), cp(5, d_b, d_b, to_x)]
            plan['hand_late'] += [cp(8, x_b, x_b, sib), cp(9, y_a, y_a, sib), cp(10, d_a, d_a, sib), cp(11, d_b, d_b, sib)]
            for k, (q, piece) in enumerate([(q_x, 0), (q_y, 1), (q_x, 1), (q_y, 0), (q_d, 0), (q_d, 1)]):
                got = here(q, theirs[piece])
                plan['final'].append(cp(6 + k, got, got, sib))
        return plan

    def start(self, ins, outs, scr):
        plan = self._plan(ins, outs, scr)
        for ld in plan['loads']:
            ld.start()
        for cp in plan['first']:
            cp.start()

    def relay(self, ins, outs, scr):
        plan = self._plan(ins, outs, scr)
        for arrived in plan['early']:
            arrived.wait_recv()
        for cp in plan['relays'] + plan['hand_early']:
            cp.start()

    def mid(self, ins, outs, scr):
        plan = self._plan(ins, outs, scr)
        for ld, st in zip(plan['loads'], plan['stores']):
            ld.wait()
            st.start()
        for arrived in plan['late']:
            arrived.wait_recv()
        for cp in plan['hand_late']:
            cp.start()

    def end(self, ins, outs, scr):
        plan = self._plan(ins, outs, scr)
        for st in plan['stores']:
            st.wait()
        for arrived in plan['final']:
            arrived.wait_recv()
        for cp in plan['first'] + plan['relays'] + plan['hand_early'] + plan['hand_late']:
            cp.wait_send()


class PairStage(_Stage):
    peers = ('sib',)

    def __init__(self, grads):
        self.inputs = list(grads)
        self.out_shape = [jax.ShapeDtypeStruct((N_CHIPS, 1) + g.shape[2:], BF16) for g in grads]
        n_cp = N_CHIPS * len(grads)
        self.scratch = [pltpu.SemaphoreType.DMA((n_cp,)), pltpu.SemaphoreType.DMA((n_cp,))]

    def _plan(self, ins, outs, scr):
        send, recv = scr
        x, y, c, _ = _place()
        plan = _empty_plan()
        for w, (g_ref, l_ref) in enumerate(zip(ins, outs)):
            for q in range(N_CHIPS):
                i = w * N_CHIPS + q
                plan['sends'].append(_rcopy(g_ref.at[q, pl.ds(1 - c, 1)], l_ref.at[q], send.at[i], recv.at[i], (x, y, 1 - c)))
        plan['arrivals'] = plan['sends']
        return plan


class ChipStage(_Stage):
    peers = ('chips',)

    def __init__(self, items):
        self.ranges = [(off, n) for _, off, n in items]
        self.inputs = [s for s, _, _ in items]
        self.out_shape = [jax.ShapeDtypeStruct((N_CHIPS, n, D_MODEL), BF16) for _, n in self.ranges]
        n_cp = 3 * len(items)
        self.scratch = [pltpu.VMEM((sum(n for _, n in self.ranges), D_MODEL), BF16), pltpu.SemaphoreType.DMA((n_cp,)),
                        pltpu.SemaphoreType.DMA((n_cp,)), pltpu.SemaphoreType.DMA((2 * len(items),))]

    def _plan(self, ins, outs, scr):
        buf, send, recv, lsem = scr
        x, y, c, chips = _place()
        me_q = 2 * x + y
        plan = _empty_plan()
        boff = 0
        for w, ((off, n), s_ref, l_ref) in enumerate(zip(self.ranges, ins, outs)):
            rows = pl.ds(off, n)
            plan['loads'].append(pltpu.make_async_copy(s_ref.at[me_q, rows], buf.at[pl.ds(boff, n)], lsem.at[2 * w]))
            plan['stores'].append(pltpu.make_async_copy(buf.at[pl.ds(boff, n)], l_ref.at[me_q], lsem.at[2 * w + 1]))
            boff += n
            for j, (cx, cy) in enumerate(chips):
                i = w * 3 + j
                got = l_ref.at[2 * cx + cy]
                plan['sends'].append(_rcopy(s_ref.at[2 * cx + cy, rows], l_ref.at[me_q], send.at[i], recv.at[i], (cx, cy, c)))
                plan['arrivals'].append(_rcopy(got, got, send.at[i], recv.at[i], (cx, cy, c)))
        return plan


class SwapStage(_Stage):
    peers = ('sib',)

    def __init__(self, items):
        n = len(items)
        self.inputs = list(items)
        self.out_shape = [jax.ShapeDtypeStruct((2,) + a.shape, a.dtype) for a in items]
        self.scratch = [pltpu.VMEM(a.shape, a.dtype) for a in items] + [
            pltpu.SemaphoreType.DMA((n,)), pltpu.SemaphoreType.DMA((n,)), pltpu.SemaphoreType.DMA((2 * n,))]

    def _plan(self, ins, outs, scr):
        bufs, (send, recv, lsem) = scr[:len(ins)], scr[len(ins):]
        x, y, c, _ = _place()
        plan = _empty_plan()
        for w, (h_ref, o_ref, buf) in enumerate(zip(ins, outs, bufs)):
            plan['loads'].append(pltpu.make_async_copy(h_ref, buf, lsem.at[2 * w]))
            plan['stores'].append(pltpu.make_async_copy(buf, o_ref.at[c], lsem.at[2 * w + 1]))
            got = o_ref.at[1 - c]
            plan['sends'].append(_rcopy(h_ref, o_ref.at[c], send.at[w], recv.at[w], (x, y, 1 - c)))
            plan['arrivals'].append(_rcopy(got, got, send.at[w], recv.at[w], (x, y, 1 - c)))
        return plan


class SmallGatherStage(_Stage):
    peers = ('chips', 'sib')

    def __init__(self, blk):
        self.inputs = [blk]
        self.out_shape = [jax.ShapeDtypeStruct((N_DEV,) + blk.shape, blk.dtype)]
        self.scratch = [pltpu.VMEM(blk.shape, blk.dtype), pltpu.SemaphoreType.DMA((7,)), pltpu.SemaphoreType.DMA((7,)),
                        pltpu.SemaphoreType.DMA((2,))]

    def _plan(self, ins, outs, scr):
        (x_ref,), (o_ref,), (buf, send, recv, lsem) = ins, outs, scr
        x, y, c, chips = _place()
        sib = (x, y, 1 - c)

        def slot(px, py, pc):
            return o_ref.at[4 * px + 2 * py + pc]

        plan = _empty_plan()
        plan['loads'].append(pltpu.make_async_copy(x_ref, buf, lsem.at[0]))
        plan['stores'].append(pltpu.make_async_copy(buf, slot(x, y, c), lsem.at[1]))
        from_sib = slot(x, y, 1 - c)
        plan['sends'].append(_rcopy(x_ref, slot(x, y, c), send.at[0], recv.at[0], sib))
        plan['final_arrivals'].append(_rcopy(from_sib, from_sib, send.at[0], recv.at[0], sib))
        for j, (cx, cy) in enumerate(chips):
            got, got_sib = slot(cx, cy, c), slot(cx, cy, 1 - c)
            plan['sends'].append(_rcopy(x_ref, slot(x, y, c), send.at[1 + j], recv.at[1 + j], (cx, cy, c)))
            plan['arrivals'].append(_rcopy(got, got, send.at[1 + j], recv.at[1 + j], (cx, cy, c)))
            plan['forwards'].append(_rcopy(got, got, send.at[4 + j], recv.at[4 + j], sib))
            plan['final_arrivals'].append(_rcopy(got_sib, got_sib, send.at[4 + j], recv.at[4 + j], sib))
        return plan


_HBM = pl.BlockSpec(memory_space=pltpu.HBM)
_SEM = pl.BlockSpec(memory_space=pltpu.SEMAPHORE)
_DATAFLOW = pltpu.CompilerParams(has_side_effects=pltpu.SideEffectType.DATAFLOW_SIDE_EFFECTING)


def chip_exchange_start(s):
    def body(s_ref, land_ref, send, recv, s_thru, land_thru, token):
        x, y, c, chips = _place()
        for j, (cx, cy) in enumerate(chips):
            _rcopy(s_ref.at[2 * cx + cy], land_ref.at[2 * x + y], send.at[j], recv.at[j], (cx, cy, c)).start()
        token[...] = jnp.zeros_like(token)

    return pl.pallas_call(
        body, name='chip_exchange_start',
        out_shape=(pltpu.SemaphoreType.DMA((3,)), pltpu.SemaphoreType.DMA((3,)), pltpu.HBM(s.shape, s.dtype),
                   pltpu.HBM(s.shape, s.dtype), jax.ShapeDtypeStruct((8, 128), F32)),
        in_specs=(_HBM, _HBM), out_specs=(_SEM, _SEM, _HBM, _HBM, pl.BlockSpec(memory_space=pltpu.VMEM)),
        input_output_aliases={0: 2, 1: 3}, compiler_params=_DATAFLOW,
    )(pltpu.with_memory_space_constraint(s, pltpu.HBM),
      pltpu.with_memory_space_constraint(lax.empty(s.shape, s.dtype), pltpu.HBM))


def chip_exchange_wait(send, recv, s_thru, land_thru, after):
    def body(s_ref, land_ref, send_sem, recv_sem, after_ref, s_out, land_out):
        x, y, c, chips = _place()
        for j, (cx, cy) in enumerate(chips):
            cp = _rcopy(s_ref.at[2 * cx + cy], land_ref.at[2 * cx + cy], send_sem.at[j], recv_sem.at[j], (cx, cy, c))
            cp.wait_send()
            cp.wait_recv()

    return pl.pallas_call(
        body, name='chip_exchange_wait',
        out_shape=(pltpu.HBM(s_thru.shape, s_thru.dtype), pltpu.HBM(land_thru.shape, land_thru.dtype)),
        in_specs=(_HBM, _HBM, _SEM, _SEM, ANY), out_specs=(_HBM, _HBM),
        input_output_aliases={0: 0, 1: 1}, compiler_params=_DATAFLOW,
    )(s_thru, land_thru, send, recv, after)


def comm_call(name, stages):
    def body():
        pass

    return _call(body, name=name, grid=(1,), in_specs=[], out_specs=[], out_shape=[], args=[], stages=stages)[1]


def pair_sum(g4, land, c_arr, name):
    hr = g4.shape[2]

    def body(c_ref, g_ref, l_ref, o_ref):
        o_ref[0] = (g_ref[0, 0].astype(F32) + l_ref[0, 0].astype(F32)).astype(BF16)

    return pl.pallas_call(
        body, name=name,
        grid_spec=pltpu.PrefetchScalarGridSpec(
            num_scalar_prefetch=1, grid=(N_CHIPS,),
            in_specs=[pl.BlockSpec((1, 1, hr, D_MODEL), lambda q, c: (q, c[0], 0, 0)),
                      pl.BlockSpec((1, 1, hr, D_MODEL), lambda q, c: (q, 0, 0, 0))],
            out_specs=pl.BlockSpec((1, hr, D_MODEL), lambda q, c: (q, 0, 0))),
        out_shape=jax.ShapeDtypeStruct((N_CHIPS, hr, D_MODEL), BF16),
        compiler_params=_params(1),
    )(c_arr, g4, land)


def small_sum(vec_parts, lru_parts):
    def body(v_ref, l_ref, o_ref):
        for p_ref, lo, n in ((v_ref, 0, ROW_WA), (l_ref, ROW_WA, SMALL_ROWS - ROW_WA)):
            acc = p_ref[0]
            for s in range(1, N_DEV):
                acc = acc + p_ref[s]
            o_ref[lo:lo + n, :] = acc

    return pl.pallas_call(
        body, name='small_sum', grid=(1,),
        in_specs=[pl.BlockSpec(vec_parts.shape, lambda i: (0, 0, 0)), pl.BlockSpec(lru_parts.shape, lambda i: (0, 0, 0))],
        out_specs=pl.BlockSpec((SMALL_ROWS, D_MODEL), lambda i: (0, 0)),
        out_shape=jax.ShapeDtypeStruct((SMALL_ROWS, D_MODEL), F32),
        compiler_params=_params(1),
    )(vec_parts, lru_parts)


def _adam_math(w, g, m, v):
    m2 = ADAM_B1 * m + (1.0 - ADAM_B1) * g
    v2 = ADAM_B2 * v + (1.0 - ADAM_B2) * (g * g)
    m_hat = m2 / (1.0 - ADAM_B1 ** ADAM_STEP)
    v_hat = v2 / (1.0 - ADAM_B2 ** ADAM_STEP)
    delta = -ADAM_LR * (m_hat / (jnp.sqrt(v_hat) + ADAM_EPS) + ADAM_WD * w)
    return delta, m2, v2


def _adam_body(n_parts, transposed, n_after):
    def body(*refs):
        refs = refs[n_after:]
        g_refs = refs[:n_parts]
        w_ref, m_ref, v_ref, go_ref, d_ref, mo_ref, vo_ref = refs[n_parts:]
        def chips_added(blk):
            acc = blk[0].astype(F32)
            for s in range(1, N_CHIPS):
                acc = acc + blk[s].astype(F32)
            return acc

        if transposed:
            g = jnp.concatenate([chips_added(g_ref[h]) for h in range(2) for g_ref in g_refs], axis=0).T
        else:
            rows = [chips_added(g_ref[0]) for g_ref in g_refs]
            g = jnp.concatenate(rows, axis=0) if n_parts > 1 else rows[0]
        go_ref[...] = g
        d_ref[...], mo_ref[...], vo_ref[...] = _adam_math(w_ref[...], g, m_ref[...], v_ref[...])
    return body


def adam_rows(fulls, name, w, m, v, after=()):
    hr = w.shape[0] // 2
    blk = pl.BlockSpec((hr, D_MODEL), lambda h: (h, 0))
    return pl.pallas_call(
        _adam_body(len(fulls), False, len(after)), name='adam_' + name, grid=(2,),
        in_specs=[ANY] * len(after)
        + [pl.BlockSpec((1, N_CHIPS, f.shape[2], D_MODEL), lambda h: (h, 0, 0, 0)) for f in fulls] + [blk, blk, blk],
        out_specs=[blk] * 4,
        out_shape=[jax.ShapeDtypeStruct(w.shape, F32)] * 4,
        compiler_params=_params(1),
    )(*after, *fulls, w, m, v)


def adam_cols(fulls, name, w, m, v, after=()):
    cols = w.shape[1]
    tr = 128
    blk = pl.BlockSpec((tr, cols), lambda i: (i, 0))
    return pl.pallas_call(
        _adam_body(len(fulls), True, len(after)), name='adam_' + name, grid=(D_MODEL // tr,),
        in_specs=[ANY] * len(after)
        + [pl.BlockSpec((2, N_CHIPS, f.shape[2], tr), lambda i: (0, 0, 0, i)) for f in fulls] + [blk, blk, blk],
        out_specs=[blk] * 4,
        out_shape=[jax.ShapeDtypeStruct(w.shape, F32)] * 4,
        compiler_params=_params(1),
    )(*after, *fulls, w, m, v)


def adam_small(g, w, m, v):
    def body(g_ref, w_ref, m_ref, v_ref, d_ref, mo_ref, vo_ref):
        d_ref[...], mo_ref[...], vo_ref[...] = _adam_math(w_ref[...], g_ref[...], m_ref[...], v_ref[...])

    blk = pl.BlockSpec(w.shape, lambda i: (0, 0))
    return pl.pallas_call(
        body, name='adam_small', grid=(1,), in_specs=[blk] * 4, out_specs=[blk] * 3,
        out_shape=[jax.ShapeDtypeStruct(w.shape, F32)] * 3, compiler_params=_params(1),
    )(g, w, m, v)


WEIGHTS = ('ffn1_pre_g', 'ffn1_w_gu', 'ffn1_w_down', 'ffn1_post_g', 'mix_pre_g', 'w_in', 'conv_w', 'conv_b',
           'lru_w_a', 'lru_b_a', 'lru_w_x', 'lru_b_x', 'lru_lambda', 'attn_sinks', 'w_proj_lru', 'w_proj_attn',
           'w_out', 'mix_post_g', 'ffn2_pre_g', 'ffn2_w_gu', 'ffn2_w_down', 'ffn2_post_g')
SMALL = tuple(n for n in WEIGHTS if n not in PACK_OFF)


def _pack_vecs(d, conv_rows):
    sinks = jnp.pad(d['attn_sinks'].reshape(1, N_Q_HEADS), ((0, 0), (0, D_MODEL - N_Q_HEADS)))
    conv = jnp.pad(conv_rows, ((0, ROW_WA - ROW_CONV - conv_rows.shape[0]), (0, 0)))
    return jnp.concatenate([d[n].reshape(1, D_MODEL) for n in SMALL_VECS] + [sinks, conv], axis=0)


def _pack_lru(d):
    return jnp.concatenate([d['lru_w_a'].reshape(64, D_MODEL), d['lru_w_x'].reshape(64, D_MODEL)], axis=0)


def _pack_small(d, conv_rows):
    return jnp.concatenate([_pack_vecs(d, conv_rows), _pack_lru(d)], axis=0)


def _unpack_small(p, shapes):
    out = {n: p[k:k + 1].reshape(shapes[n]) for k, n in enumerate(SMALL_VECS)}
    out['attn_sinks'] = p[ROW_SINKS:ROW_SINKS + 1, :N_Q_HEADS].reshape(shapes['attn_sinks'])
    out['conv_w'] = p[ROW_CONV:ROW_CONV + 1].reshape(shapes['conv_w'])
    out['lru_w_a'] = p[ROW_WA:ROW_WA + 64].reshape(shapes['lru_w_a'])
    out['lru_w_x'] = p[ROW_WX:ROW_WX + 64].reshape(shapes['lru_w_x'])
    return out


def kernel(x, ffn1_pre_g, ffn1_w_gu, ffn1_w_down, ffn1_post_g, mix_pre_g, w_in, conv_w, conv_b, lru_w_a, lru_b_a, lru_w_x, lru_b_x, lru_lambda, attn_sinks, w_proj_lru, w_proj_attn, w_out, mix_post_g, ffn2_pre_g, ffn2_w_gu, ffn2_w_down, ffn2_post_g, loss_target, m_ffn1_pre_g, m_ffn1_w_gu, m_ffn1_w_down, m_ffn1_post_g, m_mix_pre_g, m_w_in, m_conv_w, m_conv_b, m_lru_w_a, m_lru_b_a, m_lru_w_x, m_lru_b_x, m_lru_lambda, m_attn_sinks, m_w_proj_lru, m_w_proj_attn, m_w_out, m_mix_post_g, m_ffn2_pre_g, m_ffn2_w_gu, m_ffn2_w_down, m_ffn2_post_g, v_ffn1_pre_g, v_ffn1_w_gu, v_ffn1_w_down, v_ffn1_post_g, v_mix_pre_g, v_w_in, v_conv_w, v_conv_b, v_lru_w_a, v_lru_b_a, v_lru_w_x, v_lru_b_x, v_lru_lambda, v_attn_sinks, v_w_proj_lru, v_w_proj_attn, v_w_out, v_mix_post_g, v_ffn2_pre_g, v_ffn2_w_gu, v_ffn2_w_down, v_ffn2_post_g):
    given = dict(locals())
    w = {n: given[n] for n in WEIGHTS}
    mom = {n: given['m_' + n] for n in WEIGHTS}
    var = {n: given['v_' + n] for n in WEIGHTS}
    shapes = {n: w[n].shape for n in WEIGHTS}
    xq = lax.axis_index('x')
    yq = lax.axis_index('y')
    cq = lax.axis_index('c')
    me_q = 2 * xq + yq

    c_arr = cq.reshape(1).astype(jnp.int32)
    xs, target = x[0], loss_target[0]
    sw = {n: (w[n][0] if w[n].ndim > 2 else w[n]) for n in SMALL}
    cos, sin_signed = _rope_tables()
    wa_bd = _block_diag(sw['lru_w_a'])
    wx_bd = _block_diag(sw['lru_w_x'])
    sinks = sw['attn_sinks'].reshape(N_Q_HEADS)

    shard = {n: (w[n][0].T if t else w[n][0]).astype(BF16) for n, _, t in PACK}
    conv_pad = jnp.pad(w['conv_w'][0], ((0, 4), (0, 0)))

    def whole(name):
        return (shard[name], 0, PACK_ROWS_OF[name])

    def part(name, p, n_parts=2):
        rows = PACK_ROWS_OF[name] // n_parts
        return (shard[name], p * rows, rows)

    (w_gu1,), (conv_all,) = comm_call('gather_first', [GatherStage([whole('ffn1_w_gu')]), SmallGatherStage(conv_pad)])
    sw['conv_w'] = jnp.transpose(conv_all[0::2, :4, :], (1, 0, 2)).reshape(4, LRU_W)
    proj_names = ['w_proj_lru', 'w_proj_attn', 'w_out']

    (n1, g1, u1, a1), ((w_down1,),) = ffn_fwd_a(xs, sw['ffn1_pre_g'], [w_gu1], 'ffn1_fwd_a',
                                                 stages=[GatherStage([whole('ffn1_w_down')])])
    (f1, h1), ((w_in_t,),) = ffn_fwd_b(a1, w_down1, sw['ffn1_post_g'], xs, 'ffn1_fwd_b', stages=[GatherStage([whole('w_in')])])
    (um, gate, xbr, q, k, v, g_lru, g_attn), ((w_gu2a,),) = mix_in(h1, sw['mix_pre_g'], w_in_t, 'mix_in',
                                                                   stages=[GatherStage([part('ffn2_w_gu', 0)])])
    (y_lru, h_lru), ((w_gu2b,),) = lru_fwd(gate, xbr, sw['conv_w'], sw['conv_b'], wa_bd, sw['lru_b_a'], wx_bd, sw['lru_b_x'],
                                           sw['lru_lambda'], 'lru_fwd', stages=[GatherStage([part('ffn2_w_gu', 1)])])
    (qr, kr, y_attn), (projs,) = attn_fwd(q, k, v, cos, sin_signed, sinks, 'attn_fwd',
                                          stages=[GatherStage([whole(n) for n in proj_names])])
    (p_l, p_a, merged, m, h2), ((w_down2,),) = merge_fwd(y_lru, y_attn, g_lru, g_attn, projs, sw['mix_post_g'], h1, 'merge_fwd',
                                                         stages=[GatherStage([whole('ffn2_w_down')])])
    w_gu2 = [w_gu2a, w_gu2b]
    (n2, g2, u2, a2), _ = ffn_fwd_a(h2, sw['ffn2_pre_g'], w_gu2, 'ffn2_fwd_a')
    (f2, dy, loss_blk), _ = ffn_fwd_b(a2, w_down2, sw['ffn2_post_g'], h2, 'ffn2_fwd_b', target=target)

    gs, full = {}, {}

    def pair_stage(names, grads):
        g4 = [g.reshape(N_CHIPS, 2, PACK_ROWS_OF[n] // 2, D_MODEL) for n, g in zip(names, grads)]
        return PairStage(g4), g4

    def pair_sums(names, g4, lands):
        return [pair_sum(g, l, c_arr, 'pair_sum_' + n) for n, g, l in zip(names, g4, lands)]

    def halves(s, n_parts=2):
        n = s.shape[1] // n_parts
        return [(s, p * n, n) for p in range(n_parts)]

    (df2, dgu2, gs['ffn2_post_g']), _ = ffn_bwd_a(dy, f2, sw['ffn2_post_g'], w_down2, g2, u2, 'ffn2_bwd_a')
    g_down2, _ = mm_tn([a2], df2, 1408, 'ffn2_dw_down')
    st, g4 = pair_stage(['ffn2_w_down'], [g_down2])
    g_gu2, (lands,) = mm_tn([dgu2], n2, 1408, 'ffn2_dw_gu', stages=[st])
    (s_down2,) = pair_sums(['ffn2_w_down'], g4, lands)
    st, g4 = pair_stage(['ffn2_w_gu'], [g_gu2])
    (dh2, gs['ffn2_pre_g']), ((l_down2,), lands) = norm_bwd([dgu2], w_gu2, h2, sw['ffn2_pre_g'], dy, 'ffn2_bwd_b',
                                                            stages=[ChipStage([(s_down2, 0, s_down2.shape[1])]), st])
    (s_gu2,) = pair_sums(['ffn2_w_gu'], g4, lands)

    (dm, dpl, dpa, dgl, dga, dya, dyl, gs['mix_post_g']), ((l_gu2a,),) = merge_bwd(
        dh2, m, sw['mix_post_g'], projs, g_lru, g_attn, p_l, p_a, 'merge_bwd', stages=[ChipStage(halves(s_gu2)[:1])])
    g_projs = [mm_tn([merged if n == 'w_out' else (y_lru if n == 'w_proj_lru' else y_attn)],
                     dm if n == 'w_out' else (dpl if n == 'w_proj_lru' else dpa), D_MODEL, 'd' + n)[0] for n in proj_names]
    st, g4 = pair_stage(proj_names, g_projs)
    (dq, dkv, dsk), ((l_gu2b,), lands, (full['ffn2_w_down'],)) = attn_bwd(
        qr, kr, v, dya, cos, sin_signed, sinks, 'attn_bwd', stages=[ChipStage(halves(s_gu2)[1:]), st, SwapStage([l_down2])])
    full['ffn2_w_down'] = [full['ffn2_w_down']]
    gs['attn_sinks'] = dsk[0:1, 0:N_Q_HEADS]
    s_projs = pair_sums(proj_names, g4, lands)
    (dgate, dxbr, vecs, dwa, dwx), (l_projs, full['ffn2_w_gu']) = lru_bwd(
        gate, xbr, h_lru, dyl, sw['conv_w'], sw['conv_b'], wa_bd, sw['lru_b_a'], wx_bd, sw['lru_b_x'], sw['lru_lambda'],
        'lru_bwd', stages=[ChipStage([(s, 0, s.shape[1]) for s in s_projs]), SwapStage([l_gu2a, l_gu2b])])
    gs['conv_w'] = vecs[0:4]
    gs['conv_b'], gs['lru_b_a'], gs['lru_b_x'], gs['lru_lambda'] = vecs[4:5], vecs[5:6], vecs[6:7], vecs[7:8]
    gs['lru_w_a'] = _diag_blocks(dwa)
    gs['lru_w_x'] = _diag_blocks(dwx)
    dz = [dgate, dxbr, dq, dkv, dgl, dga]
    g_in, ((lru_all,),) = mm_tn(dz, um, 512, 'dw_in', stages=[SmallGatherStage(_pack_lru(gs))])
    st, g4 = pair_stage(['w_in'], [g_in])
    (dh1, gs['mix_pre_g']), (lands, f_projs) = norm_bwd(dz, [w_in_t], h1, sw['mix_pre_g'], dh2, 'mix_bwd_in',
                                                        stages=[st, SwapStage(l_projs)])
    for n, f in zip(proj_names, f_projs):
        full[n] = [f]
    (s_in,) = pair_sums(['w_in'], g4, lands)

    (df1, dgu1, gs['ffn1_post_g']), ((l_in_a,),) = ffn_bwd_a(dh1, f1, sw['ffn1_post_g'], w_down1, g1, u1, 'ffn1_bwd_a',
                                                             stages=[ChipStage(halves(s_in)[:1])])
    g_down1, _ = mm_tn([a1], df1, 1408, 'ffn1_dw_down')
    st, g4 = pair_stage(['ffn1_w_down'], [g_down1])
    g_gu1, ((l_in_b,), lands) = mm_tn([dgu1], n1, 1408, 'ffn1_dw_gu', stages=[ChipStage(halves(s_in)[1:]), st])
    (s_down1,) = pair_sums(['ffn1_w_down'], g4, lands)
    st, g4 = pair_stage(['ffn1_w_gu'], [g_gu1])
    (dx, gs['ffn1_pre_g']), ((l_down1,), lands, full['w_in']) = norm_bwd(
        [dgu1], [w_gu1], xs, sw['ffn1_pre_g'], dh1, 'ffn1_bwd_b',
        stages=[ChipStage([(s_down1, 0, s_down1.shape[1])]), st, SwapStage([l_in_a, l_in_b])])
    (s_gu1,) = pair_sums(['ffn1_w_gu'], g4, lands)
    loss_row = jnp.pad(loss_blk[0:1], ((0, 0), (0, D_MODEL - loss_blk.shape[1])))
    vec_blk = _pack_vecs(gs, jnp.concatenate([gs['conv_w'], loss_row], axis=0))
    send, recv, s_thru, land_thru, token = chip_exchange_start(s_gu1)
    out_g, out_d, out_m, out_v = {}, {}, {}, {}

    def adam(n, after=()):
        fn = adam_cols if dict((k, t) for k, _, t in PACK)[n] else adam_rows
        g_, d_, m_, v_ = fn(full[n], n, w[n][0], mom[n][0], var[n][0], after=after)
        out_g[n], out_d[n], out_m[n], out_v[n] = g_[None], d_[None], m_[None], v_[None]

    behind = token
    for n in ['ffn2_w_gu', 'w_in', 'ffn2_w_down'] + proj_names:
        adam(n, after=(behind,))
        behind = out_v[n]
    s_back, l_gu1 = chip_exchange_wait(send, recv, s_thru, land_thru, after=behind)
    own = lax.dynamic_slice_in_dim(s_back, me_q, 1, axis=0)
    l_gu1 = lax.dynamic_update_slice_in_dim(l_gu1, own, me_q, axis=0)
    (vec_all,), (f_down1, f_gu1) = comm_call('swap_last', [SmallGatherStage(vec_blk), SwapStage([l_down1, l_gu1])])
    full['ffn1_w_down'] = [f_down1]
    full['ffn1_w_gu'] = [f_gu1]
    adam('ffn1_w_gu')
    adam('ffn1_w_down')

    tot = small_sum(vec_all, lru_all)
    loss = tot[ROW_WA - 1, 0]
    conv_g = lax.dynamic_slice(tot[ROW_CONV:ROW_CONV + 4], (0, me_q * (LRU_W // N_CHIPS)), (4, LRU_W // N_CHIPS))
    small_g = _unpack_small(tot, shapes)
    small_g['conv_w'] = conv_g.reshape(shapes['conv_w'])
    g_pack = jnp.concatenate([tot[:ROW_CONV], conv_g.reshape(1, D_MODEL), jnp.zeros((ROW_WA - ROW_CONV - 1, D_MODEL), F32),
                              tot[ROW_WA:]], axis=0)
    packs = [_pack_small({n: d[n] for n in SMALL}, d['conv_w'].reshape(1, D_MODEL)) for d in (w, mom, var)]
    d_p, m_p, v_p = adam_small(g_pack, *packs)
    for n in SMALL:
        out_g[n] = small_g[n]
    for dst, p in ((out_d, d_p), (out_m, m_p), (out_v, v_p)):
        dst.update(_unpack_small(p, shapes))

    return (loss, dx[None], *[out_g[n] for n in WEIGHTS], *[out_d[n] for n in WEIGHTS],
            *[out_m[n] for n in WEIGHTS], *[out_v[n] for n in WEIGHTS])
```

```python
import jax
import jax.numpy as jnp
import numpy as np
from jax import lax
from jax.experimental import pallas as pl
from jax.experimental.pallas import tpu as pltpu

F32 = jnp.float32
BF16 = jnp.bfloat16

SEQ = 2048
D_MODEL = 1024
D_FF = 2816
LRU_W = 1024
LRU_BLOCK_W = 64
HEAD_DIM = 64
N_Q_HEADS = 16
N_KV_HEADS = 4
KV_W = N_KV_HEADS * HEAD_DIM
ATTN_BLOCK = 128
N_ATTN_BLOCKS = SEQ // ATTN_BLOCK
IN_SEGS = (1024, 1024, 1024, 256, 256, 1024, 1024)
IN_W = sum(IN_SEGS)
NORM_EPS = 1e-6
MASK_VALUE = -1e30
ROPE_THETA = 10000.0
LRU_C = 8.0
MACARON = 0.5
ADAM_LR = 0.001
ADAM_B1 = 0.9
ADAM_B2 = 0.999
ADAM_EPS = 1e-08
ADAM_WD = 0.01
ADAM_STEP = 10

N_CHIPS = 4
N_DEV = 8
VMEM_LIMIT = 56 * 1024 * 1024
MM_ROWS = 256
MESH = pl.DeviceIdType.MESH
ANY = pl.BlockSpec(memory_space=pl.ANY)

PACK = (('ffn1_w_gu', 1408, True), ('w_in', 1408, True), ('ffn2_w_gu', 1408, True),
        ('ffn1_w_down', 704, False), ('ffn2_w_down', 704, False),
        ('w_proj_lru', 256, False), ('w_proj_attn', 256, False), ('w_out', 256, False))
PACK_ROWS_OF = {n: r for n, r, _ in PACK}
PACK_OFF = {}
_o = 0
for _n, _r, _t in PACK:
    PACK_OFF[_n] = _o
    _o += _r

SMALL_VECS = ('ffn1_pre_g', 'ffn1_post_g', 'mix_pre_g', 'conv_b', 'lru_b_a', 'lru_b_x', 'lru_lambda',
              'mix_post_g', 'ffn2_pre_g', 'ffn2_post_g')
SMALL_ROWS = 144
ROW_SINKS, ROW_CONV, ROW_WA, ROW_WX = 10, 11, 16, 80


def _dot(a, b):
    return jnp.dot(a, b, preferred_element_type=F32)


def _dot_nt(a, b):
    return lax.dot_general(a, b, (((1,), (1,)), ((), ())), preferred_element_type=F32)


def _dot_tn(a, b):
    return lax.dot_general(a, b, (((0,), (0,)), ((), ())), preferred_element_type=F32)


def _params(n_grid):
    return pltpu.CompilerParams(dimension_semantics=("arbitrary",) * n_grid, vmem_limit_bytes=VMEM_LIMIT)


def _sigmoid(x):
    return 1.0 / (1.0 + jnp.exp(-x))


def _rsqrt_mean_sq(x):
    return lax.rsqrt(jnp.mean(x * x, axis=-1, keepdims=True) + NORM_EPS)


def _expm1(x):
    poly = x * (1.0 + x * (0.5 + x * (1.0 / 6.0)))
    return jnp.where(jnp.abs(x) < 0.02, poly, jnp.exp(x) - 1.0)


_GELU_K = 0.7978845608028654
_GELU_C = 0.044715


def _gelu(x):
    t = jnp.tanh(_GELU_K * (x + _GELU_C * x * x * x))
    return 0.5 * x * (1.0 + t), t


def _gelu_grad(x, t):
    return 0.5 * (1.0 + t) + 0.5 * x * (1.0 - t * t) * _GELU_K * (1.0 + 3.0 * _GELU_C * x * x)


def _load_weight(w_refs, dst_ref, sem):
    w_refs = list(w_refs) if isinstance(w_refs, (list, tuple)) else [w_refs]
    rows = dst_ref.shape[0] // N_CHIPS
    rp = rows // len(w_refs)
    cps = [pltpu.make_async_copy(w_ref.at[q], dst_ref.at[pl.ds(q * rows + p * rp, rp)], sem.at[p * N_CHIPS + q])
           for p, w_ref in enumerate(w_refs) for q in range(N_CHIPS)]
    for cp in cps:
        cp.start()
    for cp in cps:
        cp.wait()


def _weight_scratch(rows_total, parts=1):
    return [pltpu.VMEM((rows_total, D_MODEL), BF16), pltpu.SemaphoreType.DMA((N_CHIPS * parts,))]


_ROW = lambda tm: pl.BlockSpec((tm, D_MODEL), lambda i: (i, 0))
_VEC = pl.BlockSpec((1, D_MODEL), lambda i: (0, 0))


def _call(body, *, name, grid, in_specs, out_specs, out_shape, args, scratch_shapes=(), stages=()):
    in_specs, out_specs, out_shape, scratch_shapes = list(in_specs), list(out_specs), list(out_shape), list(scratch_shapes)
    n_in, n_out, n_sc = len(in_specs), len(out_specs), len(scratch_shapes)
    k_in = [len(s.inputs) for s in stages]
    k_out = [len(s.out_shape) for s in stages]
    k_sc = [len(s.scratch) for s in stages]
    last = grid[0] - 1

    def split(refs, counts):
        parts, pos = [], 0
        for k in counts:
            parts.append(refs[pos:pos + k])
            pos += k
        return parts

    kinds = tuple(sorted({k for s in stages for k in s.peers}))
    collective_id = {(): None, ('sib',): 0, ('chips',): 1, ('chips', 'sib'): 2}[kinds]

    def full(*refs):
        ins, s_ins, outs, s_outs, scr, s_scr = split(refs, [n_in, sum(k_in), n_out, sum(k_out), n_sc, sum(k_sc)])
        per_stage = list(zip(stages, split(s_ins, k_in), split(s_outs, k_out), split(s_scr, k_sc)))
        i = pl.program_id(0)
        if stages:
            @pl.when(i == 0)
            def _():
                x, y, c, chips = _place()
                peers = ([(x, y, 1 - c)] if 'sib' in kinds else []) + ([(cx, cy, c) for cx, cy in chips] if 'chips' in kinds else [])
                barrier = pltpu.get_barrier_semaphore()
                for peer in peers:
                    pl.semaphore_signal(barrier, inc=1, device_id=peer, device_id_type=MESH)
                pl.semaphore_wait(barrier, len(peers))
                for s, a, b, c_ in per_stage:
                    s.start(a, b, c_)

        body(*ins, *outs, *scr)
        if stages:
            @pl.when(i == last // 2)
            def _():
                for s, a, b, c in per_stage:
                    s.relay(a, b, c)

            @pl.when(i == max(last - 1, 0))
            def _():
                for s, a, b, c in per_stage:
                    s.mid(a, b, c)

            @pl.when(i == last)
            def _():
                for s, a, b, c in per_stage:
                    s.end(a, b, c)

    res = pl.pallas_call(
        full, name=name, grid=grid,
        in_specs=in_specs + [ANY] * sum(k_in),
        out_specs=out_specs + [ANY] * sum(k_out),
        out_shape=out_shape + [o for s in stages for o in s.out_shape],
        scratch_shapes=scratch_shapes + [x for s in stages for x in s.scratch],
        compiler_params=pltpu.CompilerParams(dimension_semantics=("arbitrary",), vmem_limit_bytes=VMEM_LIMIT,
                                             collective_id=collective_id),
    )(*args, *[a for s in stages for a in s.inputs])
    return list(res[:n_out]), split(list(res[n_out:]), k_out)


def ffn_fwd_a(x, g_pre, w_gu_t, name, stages=()):
    tm, tn = MM_ROWS, 256
    n_w = len(w_gu_t)

    def body(x_ref, gp_ref, *refs):
        w_refs = refs[:n_w]
        n_ref, g_ref, u_ref, a_ref, wt_ref, sem = refs[n_w:]

        @pl.when(pl.program_id(0) == 0)
        def _():
            _load_weight(w_refs, wt_ref, sem)

        xv = x_ref[...]
        n = (xv * _rsqrt_mean_sq(xv) * gp_ref[...]).astype(BF16)
        n_ref[...] = n
        for j in range(D_FF // tn):
            g = _dot_nt(n, wt_ref[j * tn:(j + 1) * tn, :])
            u = _dot_nt(n, wt_ref[D_FF + j * tn:D_FF + (j + 1) * tn, :])
            g_ref[:, j * tn:(j + 1) * tn] = g.astype(BF16)
            u_ref[:, j * tn:(j + 1) * tn] = u.astype(BF16)
            a_ref[:, j * tn:(j + 1) * tn] = (g * _sigmoid(g) * u).astype(BF16)

    wide = pl.BlockSpec((tm, D_FF), lambda i: (i, 0))
    return _call(
        body, name=name, grid=(SEQ // tm,),
        in_specs=[_ROW(tm), _VEC] + [ANY] * n_w,
        out_specs=[_ROW(tm), wide, wide, wide],
        out_shape=[jax.ShapeDtypeStruct((SEQ, D_MODEL), BF16)] + [jax.ShapeDtypeStruct((SEQ, D_FF), BF16)] * 3,
        scratch_shapes=_weight_scratch(2 * D_FF, n_w),
        args=[x, g_pre, *w_gu_t], stages=stages)


def ffn_fwd_b(a, w_down, g_post, h_in, name, target=None, stages=()):
    tm = MM_ROWS
    final = target is not None

    def body(*refs):
        if final:
            a_ref, wf_ref, gp_ref, h_ref, t_ref, f_ref, o_ref, loss_ref, wd_ref, sem = refs
        else:
            a_ref, wf_ref, gp_ref, h_ref, f_ref, o_ref, wd_ref, sem = refs

        @pl.when(pl.program_id(0) == 0)
        def _():
            _load_weight(wf_ref, wd_ref, sem)
            if final:
                loss_ref[...] = jnp.zeros_like(loss_ref)

        f = _dot(a_ref[...], wd_ref[...])
        f_ref[...] = f
        y = h_ref[...] + MACARON * (f * _rsqrt_mean_sq(f) * gp_ref[...])
        if final:
            err = y - t_ref[...]
            o_ref[...] = err * (1.0 / D_MODEL)
            loss_ref[...] += 0.5 * jnp.sum(err * err) * (1.0 / D_MODEL)
        else:
            o_ref[...] = y

    row = _ROW(tm)
    in_specs = [pl.BlockSpec((tm, D_FF), lambda i: (i, 0)), ANY, _VEC, row]
    out_specs = [row, row]
    out_shape = [jax.ShapeDtypeStruct((SEQ, D_MODEL), F32)] * 2
    args = [a, w_down, g_post, h_in]
    if final:
        in_specs.append(row)
        args.append(target)
        out_specs.append(pl.BlockSpec((8, 128), lambda i: (0, 0)))
        out_shape.append(jax.ShapeDtypeStruct((8, 128), F32))
    return _call(body, name=name, grid=(SEQ // tm,), in_specs=in_specs, out_specs=out_specs,
                 out_shape=out_shape, scratch_shapes=_weight_scratch(D_FF), args=args, stages=stages)


def ffn_bwd_a(d_out, f, g_post, w_down, g, u, name, stages=()):
    tm = MM_ROWS
    tc = 256

    def body(do_ref, f_ref, gp_ref, wf_ref, g_ref, u_ref, df_ref, dgu_ref, dgp_ref, wd_ref, sem):
        @pl.when(pl.program_id(0) == 0)
        def _():
            _load_weight(wf_ref, wd_ref, sem)
            dgp_ref[...] = jnp.zeros_like(dgp_ref)

        fv = f_ref[...]
        rf = _rsqrt_mean_sq(fv)
        fh = fv * rf
        dn = MACARON * do_ref[...]
        dgp_ref[...] += jnp.sum(dn * fh, axis=0, keepdims=True)
        t = dn * gp_ref[...]
        df = (rf * (t - fh * jnp.mean(t * fh, axis=-1, keepdims=True))).astype(BF16)
        df_ref[...] = df
        for c0 in range(0, D_FF, tc):
            da = _dot_nt(df, wd_ref[c0:c0 + tc, :])
            gv = g_ref[:, c0:c0 + tc].astype(F32)
            uv = u_ref[:, c0:c0 + tc].astype(F32)
            s = _sigmoid(gv)
            dgu_ref[:, c0:c0 + tc] = (da * uv * s * (1.0 + gv * (1.0 - s))).astype(BF16)
            dgu_ref[:, D_FF + c0:D_FF + c0 + tc] = (da * gv * s).astype(BF16)

    row = _ROW(tm)
    wide = pl.BlockSpec((tm, D_FF), lambda i: (i, 0))
    return _call(
        body, name=name, grid=(SEQ // tm,),
        in_specs=[row, row, _VEC, ANY, wide, wide],
        out_specs=[row, pl.BlockSpec((tm, 2 * D_FF), lambda i: (i, 0)), _VEC],
        out_shape=[jax.ShapeDtypeStruct((SEQ, D_MODEL), BF16), jax.ShapeDtypeStruct((SEQ, 2 * D_FF), BF16),
                   jax.ShapeDtypeStruct((1, D_MODEL), F32)],
        scratch_shapes=_weight_scratch(D_FF),
        args=[d_out, f, g_post, w_down, g, u], stages=stages)


def norm_bwd(pieces, w_t, x, g_pre, d_res, name, stages=()):
    tm = MM_ROWS
    widths = [p.shape[1] for p in pieces]
    offs = [sum(widths[:k]) for k in range(len(widths))]
    n_p = len(pieces)
    n_w = len(w_t)

    def body(*refs):
        p_refs = refs[:n_p]
        w_refs = refs[n_p:n_p + n_w]
        x_ref, g_ref, r_ref, dx_ref, dg_ref, wt_ref, sem = refs[n_p + n_w:]

        @pl.when(pl.program_id(0) == 0)
        def _():
            _load_weight(w_refs, wt_ref, sem)
            dg_ref[...] = jnp.zeros_like(dg_ref)

        dn = None
        for p_ref, lo, wd in zip(p_refs, offs, widths):
            part = _dot(p_ref[...], wt_ref[lo:lo + wd, :])
            dn = part if dn is None else dn + part
        xv = x_ref[...]
        r = _rsqrt_mean_sq(xv)
        xh = xv * r
        dg_ref[...] += jnp.sum(dn * xh, axis=0, keepdims=True)
        t = dn * g_ref[...]
        dx_ref[...] = r_ref[...] + r * (t - xh * jnp.mean(t * xh, axis=-1, keepdims=True))

    row = _ROW(tm)
    return _call(
        body, name=name, grid=(SEQ // tm,),
        in_specs=[pl.BlockSpec((tm, wd), lambda i: (i, 0)) for wd in widths] + [ANY] * n_w + [row, _VEC, row],
        out_specs=[row, _VEC],
        out_shape=[jax.ShapeDtypeStruct((SEQ, D_MODEL), F32), jax.ShapeDtypeStruct((1, D_MODEL), F32)],
        scratch_shapes=_weight_scratch(sum(widths), n_w),
        args=[*pieces, *w_t, x, g_pre, d_res], stages=stages)


def mm_tn(pieces, b, tm, name, stages=()):
    widths = [p.shape[1] for p in pieces]
    m_total = sum(widths)
    n_p = len(pieces)
    starts = [sum(widths[:k]) // tm for k in range(n_p)]
    counts = [wd // tm for wd in widths]

    def body(*refs):
        p_refs = refs[:n_p]
        b_ref, o_ref = refs[n_p:]
        i = pl.program_id(0)
        for p_ref, st, ct in zip(p_refs, starts, counts):
            @pl.when((i >= st) & (i < st + ct))
            def _(p_ref=p_ref):
                o_ref[...] = _dot_tn(p_ref[...], b_ref[...]).astype(BF16)

    def piece_spec(st, ct):
        return pl.BlockSpec((SEQ, tm), lambda i: (0, jnp.clip(i - st, 0, ct - 1)))

    (out,), stage_out = _call(
        body, name=name, grid=(m_total // tm,),
        in_specs=[piece_spec(st, ct) for st, ct in zip(starts, counts)] + [pl.BlockSpec((SEQ, D_MODEL), lambda i: (0, 0))],
        out_specs=[pl.BlockSpec((tm, D_MODEL), lambda i: (i, 0))],
        out_shape=[jax.ShapeDtypeStruct((m_total, D_MODEL), BF16)],
        args=[*pieces, b], stages=stages)
    return out, stage_out


def mix_in(h, g_pre, w_in_t, name, stages=()):
    tm = MM_ROWS
    offs = [sum(IN_SEGS[:k]) for k in range(len(IN_SEGS))]
    dts = [F32, F32, F32, F32, BF16, F32, F32]
    n_o = len(IN_SEGS)
    n_w = len(w_in_t)

    def body(*refs):
        h_ref, g_ref = refs[:2]
        w_refs = refs[2:2 + n_w]
        um_ref = refs[2 + n_w]
        o_refs = refs[3 + n_w:3 + n_w + n_o]
        wt_ref, sem = refs[3 + n_w + n_o:]

        @pl.when(pl.program_id(0) == 0)
        def _():
            _load_weight(w_refs, wt_ref, sem)

        hv = h_ref[...]
        um = (hv * _rsqrt_mean_sq(hv) * g_ref[...]).astype(BF16)
        um_ref[...] = um
        for o_ref, lo, wd in zip(o_refs, offs, IN_SEGS):
            for c0 in range(0, wd, 256):
                o_ref[:, c0:c0 + 256] = _dot_nt(um, wt_ref[lo + c0:lo + c0 + 256, :]).astype(o_ref.dtype)

    return _call(
        body, name=name, grid=(SEQ // tm,),
        in_specs=[_ROW(tm), _VEC] + [ANY] * n_w,
        out_specs=[_ROW(tm)] + [pl.BlockSpec((tm, wd), lambda i: (i, 0)) for wd in IN_SEGS],
        out_shape=[jax.ShapeDtypeStruct((SEQ, D_MODEL), BF16)]
        + [jax.ShapeDtypeStruct((SEQ, wd), dt) for wd, dt in zip(IN_SEGS, dts)],
        scratch_shapes=_weight_scratch(IN_W, n_w),
        args=[h, g_pre, *w_in_t], stages=stages)


LRU_TC = 256


def _conv_fwd(xb, cw, cb, tt):
    xc = xb * cw[3:4, :] + cb
    shifted = []
    for s in (1, 2, 3):
        sh = jnp.where(tt >= s, pltpu.roll(xb, s, 0), 0.0)
        shifted.append(sh)
        xc = xc + sh * cw[3 - s:4 - s, :]
    return xc, shifted


def _lru_gates(xc, wa, ba, wx, bx, lam):
    xcb = xc.astype(BF16)
    r = _sigmoid(_dot(xcb, wa) + ba)
    i = _sigmoid(_dot(xcb, wx) + bx)
    nl = -lam
    sp = jnp.maximum(nl, 0.0) + jnp.log1p(jnp.exp(-jnp.abs(nl)))
    la = (-LRU_C * r) * sp
    a = jnp.exp(la)
    mult = jnp.sqrt(jnp.maximum(-_expm1(2.0 * la), 0.0))
    return xcb, r, i, sp, a, mult


def _scan(a, b, tt, reverse, a_s, b_s):
    n = a.shape[0]
    tg = tt & 7
    for s in (1, 2, 4):
        keep = (tg < 8 - s) if reverse else (tg >= s)
        shift = n - s if reverse else s
        b = a * jnp.where(keep, pltpu.roll(b, shift, 0), 0.0) + b
        a = a * jnp.where(keep, pltpu.roll(a, shift, 0), 1.0)
    a_s[...] = a
    b_s[...] = b
    groups = n // 8

    def step(g, carry):
        gi = (groups - 1 - g) if reverse else g
        rows = pl.ds(pl.multiple_of(gi * 8, 8), 8)
        hg = a_s[rows, :] * carry + b_s[rows, :]
        b_s[rows, :] = hg
        return hg[0:1, :] if reverse else hg[7:8, :]

    lax.fori_loop(0, groups, step, jnp.zeros((1, a.shape[1]), F32), unroll=8)
    return b_s[...]


def _lru_specs():
    col = pl.BlockSpec((SEQ, LRU_TC), lambda j: (0, j))
    vec = pl.BlockSpec((1, LRU_TC), lambda j: (0, j))
    bd = pl.BlockSpec((1, LRU_TC, LRU_TC), lambda j: (j, 0, 0))
    cw = pl.BlockSpec((4, LRU_TC), lambda j: (0, j))
    return col, vec, bd, cw


def lru_fwd(gate, xbr, conv_w, conv_b, wa_bd, b_a, wx_bd, b_x, lam, name, stages=()):
    col, vec, bd, cw = _lru_specs()

    def body(gate_ref, xbr_ref, cw_ref, cb_ref, wa_ref, ba_ref, wx_ref, bx_ref, lam_ref, y_ref, h_ref, a_s, b_s):
        tt = lax.broadcasted_iota(jnp.int32, (SEQ, LRU_TC), 0)
        xc, _ = _conv_fwd(xbr_ref[...], cw_ref[...], cb_ref[...], tt)
        _, r, i, sp, a, mult = _lru_gates(xc, wa_ref[0], ba_ref[...], wx_ref[0], bx_ref[...], lam_ref[...])
        h = _scan(a, mult * (i * xc), tt, False, a_s, b_s)
        h_ref[...] = h
        gl, _ = _gelu(gate_ref[...])
        y_ref[...] = (h * gl).astype(BF16)

    return _call(
        body, name=name, grid=(LRU_W // LRU_TC,),
        in_specs=[col, col, cw, vec, bd, vec, bd, vec, vec],
        out_specs=[col, col],
        out_shape=[jax.ShapeDtypeStruct((SEQ, LRU_W), BF16), jax.ShapeDtypeStruct((SEQ, LRU_W), F32)],
        scratch_shapes=[pltpu.VMEM((SEQ, LRU_TC), F32)] * 2,
        args=[gate, xbr, conv_w, conv_b, wa_bd, b_a, wx_bd, b_x, lam], stages=stages)


def lru_bwd(gate, xbr, h, dy, conv_w, conv_b, wa_bd, b_a, wx_bd, b_x, lam, name, stages=()):
    col, vec, bd, cw = _lru_specs()

    def body(gate_ref, xbr_ref, h_ref, dy_ref, cw_ref, cb_ref, wa_ref, ba_ref, wx_ref, bx_ref, lam_ref,
             dgate_ref, dxbr_ref, vecs_ref, dwa_ref, dwx_ref, a_s, b_s):
        tt = lax.broadcasted_iota(jnp.int32, (SEQ, LRU_TC), 0)
        cwv = cw_ref[...]
        lam = lam_ref[...]
        xb = xbr_ref[...]
        xc, shifted = _conv_fwd(xb, cwv, cb_ref[...], tt)
        wa = wa_ref[0]
        wx = wx_ref[0]
        xcb, r, i, sp, a, mult = _lru_gates(xc, wa, ba_ref[...], wx, bx_ref[...], lam)
        hv = h_ref[...]
        dyv = dy_ref[...]
        gv = gate_ref[...]
        gl, th = _gelu(gv)
        dgate_ref[...] = (dyv * hv * _gelu_grad(gv, th)).astype(BF16)
        a_next = jnp.where(tt < SEQ - 1, pltpu.roll(a, SEQ - 1, 0), 0.0)
        gsum = _scan(a_next, dyv * gl, tt, True, a_s, b_s)
        h_prev = jnp.where(tt >= 1, pltpu.roll(hv, 1, 0), 0.0)
        d_mult = gsum * i * xc
        d_i = gsum * mult * xc
        d_xc = gsum * mult * i
        d_la = gsum * h_prev * a - d_mult * (a * a) / mult
        d_pr = (d_la * (-LRU_C * sp)) * r * (1.0 - r)
        d_pi = d_i * i * (1.0 - i)
        d_lam = jnp.sum(d_la * r, axis=0, keepdims=True) * (LRU_C * _sigmoid(-lam))
        d_prb = d_pr.astype(BF16)
        d_pib = d_pi.astype(BF16)
        d_xc = d_xc + _dot_nt(d_prb, wa) + _dot_nt(d_pib, wx)
        dwa_ref[0] = _dot_tn(xcb, d_prb)
        dwx_ref[0] = _dot_tn(xcb, d_pib)
        rows = [jnp.sum(d_xc * shifted[2], axis=0, keepdims=True),
                jnp.sum(d_xc * shifted[1], axis=0, keepdims=True),
                jnp.sum(d_xc * shifted[0], axis=0, keepdims=True),
                jnp.sum(d_xc * xb, axis=0, keepdims=True),
                jnp.sum(d_xc, axis=0, keepdims=True),
                jnp.sum(d_pr, axis=0, keepdims=True),
                jnp.sum(d_pi, axis=0, keepdims=True),
                d_lam]
        ri = lax.broadcasted_iota(jnp.int32, (8, LRU_TC), 0)
        acc = jnp.zeros((8, LRU_TC), F32)
        for k, rv in enumerate(rows):
            acc = jnp.where(ri == k, rv, acc)
        vecs_ref[...] = acc
        d_xb = d_xc * cwv[3:4, :]
        for s in (1, 2, 3):
            d_xb = d_xb + jnp.where(tt < SEQ - s, pltpu.roll(d_xc, SEQ - s, 0), 0.0) * cwv[3 - s:4 - s, :]
        dxbr_ref[...] = d_xb.astype(BF16)

    return _call(
        body, name=name, grid=(LRU_W // LRU_TC,),
        in_specs=[col, col, col, col, cw, vec, bd, vec, bd, vec, vec],
        out_specs=[col, col, pl.BlockSpec((8, LRU_TC), lambda j: (0, j)), bd, bd],
        out_shape=[jax.ShapeDtypeStruct((SEQ, LRU_W), BF16), jax.ShapeDtypeStruct((SEQ, LRU_W), BF16),
                   jax.ShapeDtypeStruct((8, LRU_W), F32),
                   jax.ShapeDtypeStruct((LRU_W // LRU_TC, LRU_TC, LRU_TC), F32),
                   jax.ShapeDtypeStruct((LRU_W // LRU_TC, LRU_TC, LRU_TC), F32)],
        scratch_shapes=[pltpu.VMEM((SEQ, LRU_TC), F32)] * 2,
        args=[gate, xbr, h, dy, conv_w, conv_b, wa_bd, b_a, wx_bd, b_x, lam], stages=stages)


def _rope(x, cos, sin_signed):
    w = x.shape[1]
    reps = w // 128
    if reps > 1:
        cos = jnp.tile(cos, (1, reps))
        sin_signed = jnp.tile(sin_signed, (1, reps))
    lane = lax.broadcasted_iota(jnp.int32, x.shape, 1)
    first = (lane & 63) < 32
    partner = jnp.where(first, pltpu.roll(x, w - 32, 1), pltpu.roll(x, 32, 1))
    return x * cos + partner * sin_signed


def _both_halves(t, odd):
    lo = lax.broadcasted_iota(jnp.int32, t.shape, 1) < 64
    rolled = pltpu.roll(t, 64, 1)
    return jnp.where(lo, rolled, t) if odd else jnp.where(lo, t, rolled)


def _stack_heads(ta, tb):
    lo = lax.broadcasted_iota(jnp.int32, ta.shape, 1) < 64
    return jnp.concatenate([jnp.where(lo, ta, 0.0), jnp.where(lo, 0.0, ta),
                            jnp.where(lo, tb, 0.0), jnp.where(lo, 0.0, tb)], axis=0)


def _unstack_heads(o):
    lo = lax.broadcasted_iota(jnp.int32, (ATTN_BLOCK, 128), 1) < 64
    return (jnp.where(lo, o[0:128], o[128:256]), jnp.where(lo, o[256:384], o[384:512]))


def _window_upper_t():
    shape = (ATTN_BLOCK, 4 * ATTN_BLOCK)
    return lax.broadcasted_iota(jnp.int32, shape, 0) > (lax.broadcasted_iota(jnp.int32, shape, 1) & (ATTN_BLOCK - 1))


def _fold_t(t, upper_t):
    return jnp.where(upper_t, t[:ATTN_BLOCK], t[ATTN_BLOCK:])


def _unfold_t(t, upper_t):
    zero = jnp.zeros_like(t)
    return jnp.concatenate([jnp.where(upper_t, t, zero), jnp.where(upper_t, zero, t)], axis=0)


def _attn_probs_t(kd, qs, sinks_ref, hk, first_block, upper_t):
    s = _fold_t(_dot_nt(kd, qs), upper_t) * (HEAD_DIM ** -0.5)
    s = jnp.where(jnp.logical_and(upper_t, first_block), MASK_VALUE, s)
    rg = lax.broadcasted_iota(jnp.int32, (1, 4 * ATTN_BLOCK), 1) >> 7
    sink = jnp.where(rg == 0, sinks_ref[4 * hk],
                     jnp.where(rg == 1, sinks_ref[4 * hk + 1],
                               jnp.where(rg == 2, sinks_ref[4 * hk + 2], sinks_ref[4 * hk + 3])))
    m = jnp.maximum(jnp.max(s, axis=0, keepdims=True), sink)
    e = jnp.exp(s - m)
    es = jnp.exp(sink - m)
    inv = 1.0 / (jnp.sum(e, axis=0, keepdims=True) + es)
    return e * inv, es * inv


def _prev(i):
    return jnp.maximum(i - 1, 0)


def attn_fwd(q, k, v, cos, sin_signed, sinks, name, stages=()):
    nb = ATTN_BLOCK

    def body(q_ref, kc_ref, kp_ref, vc_ref, vp_ref, cc_ref, sc_ref, cp_ref, sp_ref, sinks_ref,
             qr_ref, kr_ref, y_ref):
        first_block = pl.program_id(0) == 0
        qr = _rope(q_ref[...], cc_ref[...], sc_ref[...])
        kc = _rope(kc_ref[...], cc_ref[...], sc_ref[...])
        kp = _rope(kp_ref[...], cp_ref[...], sp_ref[...])
        qr_ref[...] = qr.astype(BF16)
        kr_ref[...] = kc.astype(BF16)
        k2 = jnp.concatenate([kp, kc], axis=0)
        v2 = jnp.concatenate([vp_ref[...].astype(F32), vc_ref[...].astype(F32)], axis=0)
        upper_t = _window_upper_t()
        for hk in range(N_KV_HEADS):
            kt = hk // 2
            kd = _both_halves(k2[:, kt * 128:(kt + 1) * 128], hk % 2).astype(BF16)
            vd = _both_halves(v2[:, kt * 128:(kt + 1) * 128], hk % 2).astype(BF16)
            qs = _stack_heads(qr[:, (2 * hk) * 128:(2 * hk + 1) * 128],
                              qr[:, (2 * hk + 1) * 128:(2 * hk + 2) * 128]).astype(BF16)
            p, _ = _attn_probs_t(kd, qs, sinks_ref, hk, first_block, upper_t)
            ta, tb = _unstack_heads(_dot_tn(_unfold_t(p.astype(BF16), upper_t), vd))
            y_ref[:, (2 * hk) * 128:(2 * hk + 1) * 128] = ta.astype(BF16)
            y_ref[:, (2 * hk + 1) * 128:(2 * hk + 2) * 128] = tb.astype(BF16)

    cur = lambda w: pl.BlockSpec((nb, w), lambda i: (i, 0))
    prv = lambda w: pl.BlockSpec((nb, w), lambda i: (_prev(i), 0))
    return _call(
        body, name=name, grid=(N_ATTN_BLOCKS,),
        in_specs=[cur(D_MODEL), cur(KV_W), prv(KV_W), cur(KV_W), prv(KV_W), cur(128), cur(128), prv(128), prv(128),
                  pl.BlockSpec(memory_space=pltpu.SMEM)],
        out_specs=[cur(D_MODEL), cur(KV_W), cur(D_MODEL)],
        out_shape=[jax.ShapeDtypeStruct((SEQ, D_MODEL), BF16), jax.ShapeDtypeStruct((SEQ, KV_W), BF16),
                   jax.ShapeDtypeStruct((SEQ, D_MODEL), BF16)],
        args=[q, k, k, v, v, cos, sin_signed, cos, sin_signed, sinks], stages=stages)


def attn_bwd(qr, kr, v, dy, cos, sin_signed, sinks, name, stages=()):
    nb = ATTN_BLOCK
    n_steps = N_ATTN_BLOCKS + 1
    scale = HEAD_DIM ** -0.5

    def body(q_ref, kc_ref, kp_ref, vc_ref, vp_ref, dy_ref, cc_ref, sc_ref, cp_ref, sp_ref, sinks_ref,
             dq_ref, dkv_ref, dsk_ref, ck_ref, cv_ref):
        dk_ref = dkv_ref.at[:, pl.ds(0, KV_W)]
        dv_ref = dkv_ref.at[:, pl.ds(KV_W, KV_W)]
        i = pl.program_id(0)

        @pl.when(i == 0)
        def _():
            dsk_ref[...] = jnp.zeros_like(dsk_ref)
            ck_ref[...] = jnp.zeros_like(ck_ref)
            cv_ref[...] = jnp.zeros_like(cv_ref)

        @pl.when(i < N_ATTN_BLOCKS)
        def _():
            qv = q_ref[...].astype(F32)
            dov = dy_ref[...].astype(F32)
            k2 = jnp.concatenate([kp_ref[...].astype(F32), kc_ref[...].astype(F32)], axis=0)
            v2 = jnp.concatenate([vp_ref[...].astype(F32), vc_ref[...].astype(F32)], axis=0)
            lane = lax.broadcasted_iota(jnp.int32, (8, 128), 1)
            lo = lax.broadcasted_iota(jnp.int32, (2 * nb, 128), 1) < 64
            dsk = jnp.zeros((8, 128), F32)
            dk_tiles = []
            dv_tiles = []
            upper_t = _window_upper_t()
            for hk in range(N_KV_HEADS):
                kt = hk // 2
                kd = _both_halves(k2[:, kt * 128:(kt + 1) * 128], hk % 2).astype(BF16)
                vd = _both_halves(v2[:, kt * 128:(kt + 1) * 128], hk % 2).astype(BF16)
                qs = _stack_heads(qv[:, (2 * hk) * 128:(2 * hk + 1) * 128],
                                  qv[:, (2 * hk + 1) * 128:(2 * hk + 2) * 128]).astype(BF16)
                dos = _stack_heads(dov[:, (2 * hk) * 128:(2 * hk + 1) * 128],
                                   dov[:, (2 * hk + 1) * 128:(2 * hk + 2) * 128]).astype(BF16)
                p, ps = _attn_probs_t(kd, qs, sinks_ref, hk, i == 0, upper_t)
                dp = _fold_t(_dot_nt(vd, dos), upper_t)
                delta = jnp.sum(p * dp, axis=0, keepdims=True)
                ds = _unfold_t((p * (dp - delta)).astype(BF16), upper_t)
                dsink = -ps * delta
                for g in range(4):
                    dsk = dsk + jnp.where(lane == 4 * hk + g, jnp.sum(dsink[:, g * nb:(g + 1) * nb]), 0.0)
                ta, tb = _unstack_heads(_dot_tn(ds, kd) * scale)
                dq_a = (2 * hk) * 128
                dq_ref[:, dq_a:dq_a + 128] = _rope(ta, cc_ref[...], -sc_ref[...]).astype(BF16)
                dq_ref[:, dq_a + 128:dq_a + 256] = _rope(tb, cc_ref[...], -sc_ref[...]).astype(BF16)
                rk = _dot(ds, qs) * scale
                rv = _dot(_unfold_t(p.astype(BF16), upper_t), dos)
                dk_tiles.append(rk + pltpu.roll(rk, 64, 1))
                dv_tiles.append(rv + pltpu.roll(rv, 64, 1))
            dsk_ref[...] += dsk
            dk_full = jnp.concatenate([jnp.where(lo, dk_tiles[0], dk_tiles[1]),
                                       jnp.where(lo, dk_tiles[2], dk_tiles[3])], axis=1)
            dv_full = jnp.concatenate([jnp.where(lo, dv_tiles[0], dv_tiles[1]),
                                       jnp.where(lo, dv_tiles[2], dv_tiles[3])], axis=1)
            dk_ref[...] = _rope(ck_ref[...] + dk_full[0:nb], cp_ref[...], -sp_ref[...]).astype(BF16)
            dv_ref[...] = (cv_ref[...] + dv_full[0:nb]).astype(BF16)
            ck_ref[...] = dk_full[nb:2 * nb]
            cv_ref[...] = dv_full[nb:2 * nb]

        @pl.when(i == N_ATTN_BLOCKS)
        def _():
            dk_ref[...] = _rope(ck_ref[...], cp_ref[...], -sp_ref[...]).astype(BF16)
            dv_ref[...] = cv_ref[...].astype(BF16)

    qi = lambda i: jnp.minimum(i, N_ATTN_BLOCKS - 1)
    cur = lambda w: pl.BlockSpec((nb, w), lambda i: (qi(i), 0))
    prv = lambda w: pl.BlockSpec((nb, w), lambda i: (_prev(qi(i)), 0))
    out_prev = lambda w: pl.BlockSpec((nb, w), lambda i: (_prev(i), 0))
    return _call(
        body, name=name, grid=(n_steps,),
        in_specs=[cur(D_MODEL), cur(KV_W), prv(KV_W), cur(KV_W), prv(KV_W), cur(D_MODEL),
                  cur(128), cur(128), out_prev(128), out_prev(128), pl.BlockSpec(memory_space=pltpu.SMEM)],
        out_specs=[cur(D_MODEL), out_prev(2 * KV_W), pl.BlockSpec((8, 128), lambda i: (0, 0))],
        out_shape=[jax.ShapeDtypeStruct((SEQ, D_MODEL), BF16), jax.ShapeDtypeStruct((SEQ, 2 * KV_W), BF16),
                   jax.ShapeDtypeStruct((8, 128), F32)],
        scratch_shapes=[pltpu.VMEM((nb, KV_W), F32), pltpu.VMEM((nb, KV_W), F32)],
        args=[qr, kr, kr, v, v, dy, cos, sin_signed, cos, sin_signed, sinks], stages=stages)


def _proj_scratch():
    return [pltpu.VMEM((D_MODEL, D_MODEL), BF16)] * 3 + [pltpu.SemaphoreType.DMA((3 * N_CHIPS,))]


def _load_projs(w_refs, wl_ref, wa_ref, wo_ref, sem):
    for k, (w_ref, dst) in enumerate(zip(w_refs, (wl_ref, wa_ref, wo_ref))):
        _load_weight(w_ref, dst, sem.at[pl.ds(k * N_CHIPS, N_CHIPS)])


def merge_fwd(y_lru, y_attn, g_lru, g_attn, projs, g_post, h_in, name, stages=()):
    tm = MM_ROWS

    def body(yl_ref, ya_ref, gl_ref, ga_ref, w1_ref, w2_ref, w3_ref, gp_ref, h_ref,
             pl_ref, pa_ref, mg_ref, m_ref, o_ref, wl_ref, wa_ref, wo_ref, sem):
        @pl.when(pl.program_id(0) == 0)
        def _():
            _load_projs((w1_ref, w2_ref, w3_ref), wl_ref, wa_ref, wo_ref, sem)

        p_l = _dot(yl_ref[...], wl_ref[...])
        p_a = _dot(ya_ref[...], wa_ref[...])
        pl_ref[...] = p_l.astype(BF16)
        pa_ref[...] = p_a.astype(BF16)
        merged = (_sigmoid(gl_ref[...]) * p_l + _sigmoid(ga_ref[...]) * p_a).astype(BF16)
        mg_ref[...] = merged
        m = _dot(merged, wo_ref[...])
        m_ref[...] = m
        o_ref[...] = h_ref[...] + m * _rsqrt_mean_sq(m) * gp_ref[...]

    row = _ROW(tm)
    return _call(
        body, name=name, grid=(SEQ // tm,),
        in_specs=[row, row, row, row, ANY, ANY, ANY, _VEC, row],
        out_specs=[row] * 5,
        out_shape=[jax.ShapeDtypeStruct((SEQ, D_MODEL), BF16)] * 3 + [jax.ShapeDtypeStruct((SEQ, D_MODEL), F32)] * 2,
        scratch_shapes=_proj_scratch(),
        args=[y_lru, y_attn, g_lru, g_attn, *projs, g_post, h_in], stages=stages)


def merge_bwd(d_out, m, g_post, projs, g_lru, g_attn, p_l, p_a, name, stages=()):
    tm = 256

    def body(do_ref, m_ref, gp_ref, w1_ref, w2_ref, w3_ref, gl_ref, ga_ref, pl_ref, pa_ref,
             dm_ref, dpl_ref, dpa_ref, dgl_ref, dga_ref, dya_ref, dyl_ref, dgp_ref, wl_ref, wa_ref, wo_ref, sem):
        @pl.when(pl.program_id(0) == 0)
        def _():
            _load_projs((w1_ref, w2_ref, w3_ref), wl_ref, wa_ref, wo_ref, sem)
            dgp_ref[...] = jnp.zeros_like(dgp_ref)

        mv = m_ref[...]
        rm = _rsqrt_mean_sq(mv)
        mh = mv * rm
        dn = do_ref[...]
        dgp_ref[...] += jnp.sum(dn * mh, axis=0, keepdims=True)
        t = dn * gp_ref[...]
        dm = (rm * (t - mh * jnp.mean(t * mh, axis=-1, keepdims=True))).astype(BF16)
        dm_ref[...] = dm
        dmg = _dot_nt(dm, wo_ref[...])
        sl = _sigmoid(gl_ref[...])
        sa = _sigmoid(ga_ref[...])
        dpl = (dmg * sl).astype(BF16)
        dpa = (dmg * sa).astype(BF16)
        dpl_ref[...] = dpl
        dpa_ref[...] = dpa
        dgl_ref[...] = (dmg * pl_ref[...].astype(F32) * sl * (1.0 - sl)).astype(BF16)
        dga_ref[...] = (dmg * pa_ref[...].astype(F32) * sa * (1.0 - sa)).astype(BF16)
        dyl_ref[...] = _dot_nt(dpl, wl_ref[...])
        dya_ref[...] = _dot_nt(dpa, wa_ref[...]).astype(BF16)

    row = _ROW(tm)
    return _call(
        body, name=name, grid=(SEQ // tm,),
        in_specs=[row, row, _VEC, ANY, ANY, ANY, row, row, row, row],
        out_specs=[row] * 7 + [_VEC],
        out_shape=[jax.ShapeDtypeStruct((SEQ, D_MODEL), BF16)] * 6 + [jax.ShapeDtypeStruct((SEQ, D_MODEL), F32),
                                                                       jax.ShapeDtypeStruct((1, D_MODEL), F32)],
        scratch_shapes=_proj_scratch(),
        args=[d_out, m, g_post, *projs, g_lru, g_attn, p_l, p_a], stages=stages)


def _rope_tables():
    half = HEAD_DIM // 2
    inv_freq = np.float32(ROPE_THETA) ** (-np.arange(half, dtype=np.float32) / np.float32(half))
    ang = np.arange(SEQ, dtype=np.float32)[:, None] * inv_freq[None, :]
    cos, sin = np.cos(ang), np.sin(ang)
    return (jnp.asarray(np.tile(np.concatenate([cos, cos], axis=1), (1, 2))),
            jnp.asarray(np.tile(np.concatenate([-sin, sin], axis=1), (1, 2))))


def _block_diag(w):
    per = LRU_TC // LRU_BLOCK_W
    w4 = w.reshape(LRU_W // LRU_TC, per, LRU_BLOCK_W, LRU_BLOCK_W)
    eye = jnp.eye(per, dtype=w.dtype)
    return jnp.einsum('jacd,ab->jacbd', w4, eye).reshape(LRU_W // LRU_TC, LRU_TC, LRU_TC).astype(BF16)


def _diag_blocks(p):
    per = LRU_TC // LRU_BLOCK_W
    p5 = p.reshape(LRU_W // LRU_TC, per, LRU_BLOCK_W, per, LRU_BLOCK_W)
    return jnp.stack([p5[:, a, :, a, :] for a in range(per)], axis=1).reshape(LRU_W // LRU_BLOCK_W, LRU_BLOCK_W, LRU_BLOCK_W)


def _place():
    x, y, c = lax.axis_index('x'), lax.axis_index('y'), lax.axis_index('c')
    chips = [(1 - x, y), (x, 1 - y), (1 - x, 1 - y)]
    return x, y, c, chips


def _rcopy(src, dst, send_sem, recv_sem, to):
    return pltpu.make_async_remote_copy(src_ref=src, dst_ref=dst, send_sem=send_sem, recv_sem=recv_sem,
                                        device_id=to, device_id_type=MESH)


class _Stage:
    inputs, out_shape, scratch, peers = (), (), (), ()

    def start(self, ins, outs, scr):
        plan = self._plan(ins, outs, scr)
        for ld in plan['loads']:
            ld.start()
        for cp in plan['sends']:
            cp.start()

    def relay(self, ins, outs, scr):
        pass

    def mid(self, ins, outs, scr):
        plan = self._plan(ins, outs, scr)
        for ld, st in zip(plan['loads'], plan['stores']):
            ld.wait()
            st.start()
        for arrived, onward in zip(plan['arrivals'], plan['forwards']):
            arrived.wait_recv()
            onward.start()

    def end(self, ins, outs, scr):
        plan = self._plan(ins, outs, scr)
        for st in plan['stores']:
            st.wait()
        for arrived in (plan['final_arrivals'] if plan['forwards'] else plan['arrivals']):
            arrived.wait_recv()
        for cp in plan['sends'] + plan['forwards']:
            cp.wait_send()


def _empty_plan():
    return dict(loads=[], stores=[], sends=[], arrivals=[], forwards=[], final_arrivals=[])


class GatherStage(_Stage):
    peers = ('chips', 'sib')
    N_CP = 12

    def __init__(self, items):
        self.ranges = [(off, rows) for _, off, rows in items]
        self.inputs = [src for src, _, _ in items]
        self.out_shape = [jax.ShapeDtypeStruct((N_CHIPS, rows, D_MODEL), BF16) for _, rows in self.ranges]
        n = self.N_CP * len(items)
        self.scratch = [pltpu.VMEM((sum(r for _, r in self.ranges), D_MODEL), BF16), pltpu.SemaphoreType.DMA((n,)),
                        pltpu.SemaphoreType.DMA((n,)), pltpu.SemaphoreType.DMA((2 * len(items),))]

    def _plan(self, ins, outs, scr):
        buf, send, recv, lsem = scr
        x, y, c, _ = _place()
        me_q, q_x, q_y, q_d = 2 * x + y, 2 * (1 - x) + y, 2 * x + (1 - y), 2 * (1 - x) + (1 - y)
        to_x, to_y, sib = (1 - x, y, c), (x, 1 - y, c), (x, y, 1 - c)
        plan = dict(loads=[], stores=[], first=[], early=[], relays=[], late=[], hand_early=[], hand_late=[], final=[])
        boff = 0
        for w, ((off, rows), p_ref, o_ref) in enumerate(zip(self.ranges, ins, outs)):
            hr = rows // 2
            ch = hr // 2
            plan['loads'].append(pltpu.make_async_copy(p_ref.at[pl.ds(off, rows)], buf.at[pl.ds(boff, rows)], lsem.at[2 * w]))
            plan['stores'].append(pltpu.make_async_copy(buf.at[pl.ds(boff, rows)], o_ref.at[me_q], lsem.at[2 * w + 1]))
            boff += rows
            base = w * self.N_CP
            mine = [pl.ds(pl.multiple_of(c * hr + k * ch, 16), ch) for k in range(2)]
            theirs = [pl.ds(pl.multiple_of((1 - c) * hr + k * ch, 16), ch) for k in range(2)]
            src = [p_ref.at[pl.ds(pl.multiple_of(off + c * hr + k * ch, 16), ch)] for k in range(2)]

            def cp(k, s, d, to):
                return _rcopy(s, d, send.at[base + k], recv.at[base + k], to)

            def here(q, rows_):
                return o_ref.at[q, rows_]

            plan['first'] += [cp(0, src[0], here(me_q, mine[0]), to_x), cp(2, src[1], here(me_q, mine[1]), to_y),
                              cp(1, src[1], here(me_q, mine[1]), to_x), cp(3, src[0], here(me_q, mine[0]), to_y)]
            x_a, y_b = here(q_x, mine[0]), here(q_y, mine[1])
            plan['early'] += [cp(0, x_a, x_a, to_x), cp(2, y_b, y_b, to_y)]
            plan['relays'] += [cp(4, x_a, x_a, to_y), cp(5, y_b, y_b, to_x)]
            plan['hand_early'] += [cp(6, x_a, x_a, sib), cp(7, y_b, y_b, sib)]
            x_b, y_a, d_a, d_b = here(q_x, mine[1]), here(q_y, mine[0]), here(q_d, mine[0]), here(q_d, mine[1])
            plan['late'] += [cp(1, x_b, x_b, to_x), cp(3, y_a, y_a, to_y), cp(4, d_a, d_a, to_y), cp(5, d_b, d_b, to_x)]
            plan['hand_late'] += [cp(8, x_b, x_b, sib), cp(9, y_a, y_a, sib), cp(10, d_a, d_a, sib), cp(11, d_b, d_b, sib)]
            for k, (q, piece) in enumerate([(q_x, 0), (q_y, 1), (q_x, 1), (q_y, 0), (q_d, 0), (q_d, 1)]):
                got = here(q, theirs[piece])
                plan['final'].append(cp(6 + k, got, got, sib))
        return plan

    def start(self, ins, outs, scr):
        plan = self._plan(ins, outs, scr)
        for ld in plan['loads']:
            ld.start()
        for cp in plan['first']:
            cp.start()

    def relay(self, ins, outs, scr):
        plan = self._plan(ins, outs, scr)
        for arrived in plan['early']:
            arrived.wait_recv()
        for cp in plan['relays'] + plan['hand_early']:
            cp.start()

    def mid(self, ins, outs, scr):
        plan = self._plan(ins, outs, scr)
        for ld, st in zip(plan['loads'], plan['stores']):
            ld.wait()
            st.start()
        for arrived in plan['late']:
            arrived.wait_recv()
        for cp in plan['hand_late']:
            cp.start()

    def end(self, ins, outs, scr):
        plan = self._plan(ins, outs, scr)
        for st in plan['stores']:
            st.wait()
        for arrived in plan['final']:
            arrived.wait_recv()
        for cp in plan['first'] + plan['relays'] + plan['hand_early'] + plan['hand_late']:
            cp.wait_send()


class PairStage(_Stage):
    peers = ('sib',)

    def __init__(self, grads):
        self.inputs = list(grads)
        self.out_shape = [jax.ShapeDtypeStruct((N_CHIPS, 1) + g.shape[2:], BF16) for g in grads]
        n_cp = N_CHIPS * len(grads)
        self.scratch = [pltpu.SemaphoreType.DMA((n_cp,)), pltpu.SemaphoreType.DMA((n_cp,))]

    def _plan(self, ins, outs, scr):
        send, recv = scr
        x, y, c, _ = _place()
        plan = _empty_plan()
        for w, (g_ref, l_ref) in enumerate(zip(ins, outs)):
            for q in range(N_CHIPS):
                i = w * N_CHIPS + q
                plan['sends'].append(_rcopy(g_ref.at[q, pl.ds(1 - c, 1)], l_ref.at[q], send.at[i], recv.at[i], (x, y, 1 - c)))
        plan['arrivals'] = plan['sends']
        return plan


class ChipStage(_Stage):
    peers = ('chips',)

    def __init__(self, items):
        self.ranges = [(off, n) for _, off, n in items]
        self.inputs = [s for s, _, _ in items]
        self.out_shape = [jax.ShapeDtypeStruct((N_CHIPS, n, D_MODEL), BF16) for _, n in self.ranges]
        n_cp = 3 * len(items)
        self.scratch = [pltpu.VMEM((sum(n for _, n in self.ranges), D_MODEL), BF16), pltpu.SemaphoreType.DMA((n_cp,)),
                        pltpu.SemaphoreType.DMA((n_cp,)), pltpu.SemaphoreType.DMA((2 * len(items),))]

    def _plan(self, ins, outs, scr):
        buf, send, recv, lsem = scr
        x, y, c, chips = _place()
        me_q = 2 * x + y
        plan = _empty_plan()
        boff = 0
        for w, ((off, n), s_ref, l_ref) in enumerate(zip(self.ranges, ins, outs)):
            rows = pl.ds(off, n)
            plan['loads'].append(pltpu.make_async_copy(s_ref.at[me_q, rows], buf.at[pl.ds(boff, n)], lsem.at[2 * w]))
            plan['stores'].append(pltpu.make_async_copy(buf.at[pl.ds(boff, n)], l_ref.at[me_q], lsem.at[2 * w + 1]))
            boff += n
            for j, (cx, cy) in enumerate(chips):
                i = w * 3 + j
                got = l_ref.at[2 * cx + cy]
                plan['sends'].append(_rcopy(s_ref.at[2 * cx + cy, rows], l_ref.at[me_q], send.at[i], recv.at[i], (cx, cy, c)))
                plan['arrivals'].append(_rcopy(got, got, send.at[i], recv.at[i], (cx, cy, c)))
        return plan


class SwapStage(_Stage):
    peers = ('sib',)

    def __init__(self, items):
        n = len(items)
        self.inputs = list(items)
        self.out_shape = [jax.ShapeDtypeStruct((2,) + a.shape, a.dtype) for a in items]
        self.scratch = [pltpu.VMEM(a.shape, a.dtype) for a in items] + [
            pltpu.SemaphoreType.DMA((n,)), pltpu.SemaphoreType.DMA((n,)), pltpu.SemaphoreType.DMA((2 * n,))]

    def _plan(self, ins, outs, scr):
        bufs, (send, recv, lsem) = scr[:len(ins)], scr[len(ins):]
        x, y, c, _ = _place()
        plan = _empty_plan()
        for w, (h_ref, o_ref, buf) in enumerate(zip(ins, outs, bufs)):
            plan['loads'].append(pltpu.make_async_copy(h_ref, buf, lsem.at[2 * w]))
            plan['stores'].append(pltpu.make_async_copy(buf, o_ref.at[c], lsem.at[2 * w + 1]))
            got = o_ref.at[1 - c]
            plan['sends'].append(_rcopy(h_ref, o_ref.at[c], send.at[w], recv.at[w], (x, y, 1 - c)))
            plan['arrivals'].append(_rcopy(got, got, send.at[w], recv.at[w], (x, y, 1 - c)))
        return plan


class SmallGatherStage(_Stage):
    peers = ('chips', 'sib')

    def __init__(self, blk):
        self.inputs = [blk]
        self.out_shape = [jax.ShapeDtypeStruct((N_DEV,) + blk.shape, blk.dtype)]
        self.scratch = [pltpu.VMEM(blk.shape, blk.dtype), pltpu.SemaphoreType.DMA((7,)), pltpu.SemaphoreType.DMA((7,)),
                        pltpu.SemaphoreType.DMA((2,))]

    def _plan(self, ins, outs, scr):
        (x_ref,), (o_ref,), (buf, send, recv, lsem) = ins, outs, scr
        x, y, c, chips = _place()
        sib = (x, y, 1 - c)

        def slot(px, py, pc):
            return o_ref.at[4 * px + 2 * py + pc]

        plan = _empty_plan()
        plan['loads'].append(pltpu.make_async_copy(x_ref, buf, lsem.at[0]))
        plan['stores'].append(pltpu.make_async_copy(buf, slot(x, y, c), lsem.at[1]))
        from_sib = slot(x, y, 1 - c)
        plan['sends'].append(_rcopy(x_ref, slot(x, y, c), send.at[0], recv.at[0], sib))
        plan['final_arrivals'].append(_rcopy(from_sib, from_sib, send.at[0], recv.at[0], sib))
        for j, (cx, cy) in enumerate(chips):
            got, got_sib = slot(cx, cy, c), slot(cx, cy, 1 - c)
            plan['sends'].append(_rcopy(x_ref, slot(x, y, c), send.at[1 + j], recv.at[1 + j], (cx, cy, c)))
            plan['arrivals'].append(_rcopy(got, got, send.at[1 + j], recv.at[1 + j], (cx, cy, c)))
            plan['forwards'].append(_rcopy(got, got, send.at[4 + j], recv.at[4 + j], sib))
            plan['final_arrivals'].append(_rcopy(got_sib, got_sib, send.at[4 + j], recv.at[4 + j], sib))
        return plan


_HBM = pl.BlockSpec(memory_space=pltpu.HBM)
_SEM = pl.BlockSpec(memory_space=pltpu.SEMAPHORE)
_DATAFLOW = pltpu.CompilerParams(has_side_effects=pltpu.SideEffectType.DATAFLOW_SIDE_EFFECTING)


def chip_exchange_start(s):
    def body(s_ref, land_ref, send, recv, s_thru, land_thru, token):
        x, y, c, chips = _place()
        for j, (cx, cy) in enumerate(chips):
            _rcopy(s_ref.at[2 * cx + cy], land_ref.at[2 * x + y], send.at[j], recv.at[j], (cx, cy, c)).start()
        token[...] = jnp.zeros_like(token)

    return pl.pallas_call(
        body, name='chip_exchange_start',
        out_shape=(pltpu.SemaphoreType.DMA((3,)), pltpu.SemaphoreType.DMA((3,)), pltpu.HBM(s.shape, s.dtype),
                   pltpu.HBM(s.shape, s.dtype), jax.ShapeDtypeStruct((8, 128), F32)),
        in_specs=(_HBM, _HBM), out_specs=(_SEM, _SEM, _HBM, _HBM, pl.BlockSpec(memory_space=pltpu.VMEM)),
        input_output_aliases={0: 2, 1: 3}, compiler_params=_DATAFLOW,
    )(pltpu.with_memory_space_constraint(s, pltpu.HBM),
      pltpu.with_memory_space_constraint(lax.empty(s.shape, s.dtype), pltpu.HBM))


def chip_exchange_wait(send, recv, s_thru, land_thru, after):
    def body(s_ref, land_ref, send_sem, recv_sem, after_ref, s_out, land_out):
        x, y, c, chips = _place()
        for j, (cx, cy) in enumerate(chips):
            cp = _rcopy(s_ref.at[2 * cx + cy], land_ref.at[2 * cx + cy], send_sem.at[j], recv_sem.at[j], (cx, cy, c))
            cp.wait_send()
            cp.wait_recv()

    return pl.pallas_call(
        body, name='chip_exchange_wait',
        out_shape=(pltpu.HBM(s_thru.shape, s_thru.dtype), pltpu.HBM(land_thru.shape, land_thru.dtype)),
        in_specs=(_HBM, _HBM, _SEM, _SEM, ANY), out_specs=(_HBM, _HBM),
        input_output_aliases={0: 0, 1: 1}, compiler_params=_DATAFLOW,
    )(s_thru, land_thru, send, recv, after)


def comm_call(name, stages):
    def body():
        pass

    return _call(body, name=name, grid=(1,), in_specs=[], out_specs=[], out_shape=[], args=[], stages=stages)[1]


def pair_sum(g4, land, c_arr, name):
    hr = g4.shape[2]

    def body(c_ref, g_ref, l_ref, o_ref):
        o_ref[0] = (g_ref[0, 0].astype(F32) + l_ref[0, 0].astype(F32)).astype(BF16)

    return pl.pallas_call(
        body, name=name,
        grid_spec=pltpu.PrefetchScalarGridSpec(
            num_scalar_prefetch=1, grid=(N_CHIPS,),
            in_specs=[pl.BlockSpec((1, 1, hr, D_MODEL), lambda q, c: (q, c[0], 0, 0)),
                      pl.BlockSpec((1, 1, hr, D_MODEL), lambda q, c: (q, 0, 0, 0))],
            out_specs=pl.BlockSpec((1, hr, D_MODEL), lambda q, c: (q, 0, 0))),
        out_shape=jax.ShapeDtypeStruct((N_CHIPS, hr, D_MODEL), BF16),
        compiler_params=_params(1),
    )(c_arr, g4, land)


def small_sum(vec_parts, lru_parts):
    def body(v_ref, l_ref, o_ref):
        for p_ref, lo, n in ((v_ref, 0, ROW_WA), (l_ref, ROW_WA, SMALL_ROWS - ROW_WA)):
            acc = p_ref[0]
            for s in range(1, N_DEV):
                acc = acc + p_ref[s]
            o_ref[lo:lo + n, :] = acc

    return pl.pallas_call(
        body, name='small_sum', grid=(1,),
        in_specs=[pl.BlockSpec(vec_parts.shape, lambda i: (0, 0, 0)), pl.BlockSpec(lru_parts.shape, lambda i: (0, 0, 0))],
        out_specs=pl.BlockSpec((SMALL_ROWS, D_MODEL), lambda i: (0, 0)),
        out_shape=jax.ShapeDtypeStruct((SMALL_ROWS, D_MODEL), F32),
        compiler_params=_params(1),
    )(vec_parts, lru_parts)


def _adam_math(w, g, m, v):
    m2 = ADAM_B1 * m + (1.0 - ADAM_B1) * g
    v2 = ADAM_B2 * v + (1.0 - ADAM_B2) * (g * g)
    m_hat = m2 / (1.0 - ADAM_B1 ** ADAM_STEP)
    v_hat = v2 / (1.0 - ADAM_B2 ** ADAM_STEP)
    delta = -ADAM_LR * (m_hat / (jnp.sqrt(v_hat) + ADAM_EPS) + ADAM_WD * w)
    return delta, m2, v2


def _adam_body(n_parts, transposed, n_after):
    def body(*refs):
        refs = refs[n_after:]
        g_refs = refs[:n_parts]
        w_ref, m_ref, v_ref, go_ref, d_ref, mo_ref, vo_ref = refs[n_parts:]
        def chips_added(blk):
            acc = blk[0].astype(F32)
            for s in range(1, N_CHIPS):
                acc = acc + blk[s].astype(F32)
            return acc

        if transposed:
            g = jnp.concatenate([chips_added(g_ref[h]) for h in range(2) for g_ref in g_refs], axis=0).T
        else:
            rows = [chips_added(g_ref[0]) for g_ref in g_refs]
            g = jnp.concatenate(rows, axis=0) if n_parts > 1 else rows[0]
        go_ref[...] = g
        d_ref[...], mo_ref[...], vo_ref[...] = _adam_math(w_ref[...], g, m_ref[...], v_ref[...])
    return body


def adam_rows(fulls, name, w, m, v, after=()):
    hr = w.shape[0] // 2
    blk = pl.BlockSpec((hr, D_MODEL), lambda h: (h, 0))
    return pl.pallas_call(
        _adam_body(len(fulls), False, len(after)), name='adam_' + name, grid=(2,),
        in_specs=[ANY] * len(after)
        + [pl.BlockSpec((1, N_CHIPS, f.shape[2], D_MODEL), lambda h: (h, 0, 0, 0)) for f in fulls] + [blk, blk, blk],
        out_specs=[blk] * 4,
        out_shape=[jax.ShapeDtypeStruct(w.shape, F32)] * 4,
        compiler_params=_params(1),
    )(*after, *fulls, w, m, v)


def adam_cols(fulls, name, w, m, v, after=()):
    cols = w.shape[1]
    tr = 128
    blk = pl.BlockSpec((tr, cols), lambda i: (i, 0))
    return pl.pallas_call(
        _adam_body(len(fulls), True, len(after)), name='adam_' + name, grid=(D_MODEL // tr,),
        in_specs=[ANY] * len(after)
        + [pl.BlockSpec((2, N_CHIPS, f.shape[2], tr), lambda i: (0, 0, 0, i)) for f in fulls] + [blk, blk, blk],
        out_specs=[blk] * 4,
        out_shape=[jax.ShapeDtypeStruct(w.shape, F32)] * 4,
        compiler_params=_params(1),
    )(*after, *fulls, w, m, v)


def adam_small(g, w, m, v):
    def body(g_ref, w_ref, m_ref, v_ref, d_ref, mo_ref, vo_ref):
        d_ref[...], mo_ref[...], vo_ref[...] = _adam_math(w_ref[...], g_ref[...], m_ref[...], v_ref[...])

    blk = pl.BlockSpec(w.shape, lambda i: (0, 0))
    return pl.pallas_call(
        body, name='adam_small', grid=(1,), in_specs=[blk] * 4, out_specs=[blk] * 3,
        out_shape=[jax.ShapeDtypeStruct(w.shape, F32)] * 3, compiler_params=_params(1),
    )(g, w, m, v)


WEIGHTS = ('ffn1_pre_g', 'ffn1_w_gu', 'ffn1_w_down', 'ffn1_post_g', 'mix_pre_g', 'w_in', 'conv_w', 'conv_b',
           'lru_w_a', 'lru_b_a', 'lru_w_x', 'lru_b_x', 'lru_lambda', 'attn_sinks', 'w_proj_lru', 'w_proj_attn',
           'w_out', 'mix_post_g', 'ffn2_pre_g', 'ffn2_w_gu', 'ffn2_w_down', 'ffn2_post_g')
SMALL = tuple(n for n in WEIGHTS if n not in PACK_OFF)


def cast_t(w, name):
    cols = w.shape[1]
    tc = 128

    def body(w_ref, o_ref):
        o_ref[...] = w_ref[...].T.astype(BF16)

    return pl.pallas_call(
        body, name=name, grid=(cols // tc,),
        in_specs=[pl.BlockSpec((D_MODEL, tc), lambda j: (0, j))],
        out_specs=pl.BlockSpec((tc, D_MODEL), lambda j: (j, 0)),
        out_shape=jax.ShapeDtypeStruct((cols, D_MODEL), BF16),
        compiler_params=_params(1),
    )(w)


def _pack_vecs(d, conv_rows):
    sinks = jnp.pad(d['attn_sinks'].reshape(1, N_Q_HEADS), ((0, 0), (0, D_MODEL - N_Q_HEADS)))
    conv = jnp.pad(conv_rows, ((0, ROW_WA - ROW_CONV - conv_rows.shape[0]), (0, 0)))
    return jnp.concatenate([d[n].reshape(1, D_MODEL) for n in SMALL_VECS] + [sinks, conv], axis=0)


def _pack_lru(d):
    return jnp.concatenate([d['lru_w_a'].reshape(64, D_MODEL), d['lru_w_x'].reshape(64, D_MODEL)], axis=0)


def _pack_small(d, conv_rows):
    return jnp.concatenate([_pack_vecs(d, conv_rows), _pack_lru(d)], axis=0)


def _unpack_small(p, shapes):
    out = {n: p[k:k + 1].reshape(shapes[n]) for k, n in enumerate(SMALL_VECS)}
    out['attn_sinks'] = p[ROW_SINKS:ROW_SINKS + 1, :N_Q_HEADS].reshape(shapes['attn_sinks'])
    out['conv_w'] = p[ROW_CONV:ROW_CONV + 1].reshape(shapes['conv_w'])
    out['lru_w_a'] = p[ROW_WA:ROW_WA + 64].reshape(shapes['lru_w_a'])
    out['lru_w_x'] = p[ROW_WX:ROW_WX + 64].reshape(shapes['lru_w_x'])
    return out


def kernel(x, ffn1_pre_g, ffn1_w_gu, ffn1_w_down, ffn1_post_g, mix_pre_g, w_in, conv_w, conv_b, lru_w_a, lru_b_a, lru_w_x, lru_b_x, lru_lambda, attn_sinks, w_proj_lru, w_proj_attn, w_out, mix_post_g, ffn2_pre_g, ffn2_w_gu, ffn2_w_down, ffn2_post_g, loss_target, m_ffn1_pre_g, m_ffn1_w_gu, m_ffn1_w_down, m_ffn1_post_g, m_mix_pre_g, m_w_in, m_conv_w, m_conv_b, m_lru_w_a, m_lru_b_a, m_lru_w_x, m_lru_b_x, m_lru_lambda, m_attn_sinks, m_w_proj_lru, m_w_proj_attn, m_w_out, m_mix_post_g, m_ffn2_pre_g, m_ffn2_w_gu, m_ffn2_w_down, m_ffn2_post_g, v_ffn1_pre_g, v_ffn1_w_gu, v_ffn1_w_down, v_ffn1_post_g, v_mix_pre_g, v_w_in, v_conv_w, v_conv_b, v_lru_w_a, v_lru_b_a, v_lru_w_x, v_lru_b_x, v_lru_lambda, v_attn_sinks, v_w_proj_lru, v_w_proj_attn, v_w_out, v_mix_post_g, v_ffn2_pre_g, v_ffn2_w_gu, v_ffn2_w_down, v_ffn2_post_g):
    given = dict(locals())
    w = {n: given[n] for n in WEIGHTS}
    mom = {n: given['m_' + n] for n in WEIGHTS}
    var = {n: given['v_' + n] for n in WEIGHTS}
    shapes = {n: w[n].shape for n in WEIGHTS}
    xq = lax.axis_index('x')
    yq = lax.axis_index('y')
    cq = lax.axis_index('c')
    me_q = 2 * xq + yq

    c_arr = cq.reshape(1).astype(jnp.int32)
    xs, target = x[0], loss_target[0]
    sw = {n: (w[n][0] if w[n].ndim > 2 else w[n]) for n in SMALL}
    cos, sin_signed = _rope_tables()
    wa_bd = _block_diag(sw['lru_w_a'])
    wx_bd = _block_diag(sw['lru_w_x'])
    sinks = sw['attn_sinks'].reshape(N_Q_HEADS)

    shard = {n: (cast_t(w[n][0], 'cast_' + n) if t else w[n][0].astype(BF16)) for n, _, t in PACK}
    conv_pad = jnp.pad(w['conv_w'][0], ((0, 4), (0, 0)))

    def whole(name):
        return (shard[name], 0, PACK_ROWS_OF[name])

    def part(name, p, n_parts=2):
        rows = PACK_ROWS_OF[name] // n_parts
        return (shard[name], p * rows, rows)

    (w_gu1,), (conv_all,) = comm_call('gather_first', [GatherStage([whole('ffn1_w_gu')]), SmallGatherStage(conv_pad)])
    sw['conv_w'] = jnp.transpose(conv_all[0::2, :4, :], (1, 0, 2)).reshape(4, LRU_W)
    proj_names = ['w_proj_lru', 'w_proj_attn', 'w_out']

    (n1, g1, u1, a1), ((w_down1, w_in_a),) = ffn_fwd_a(xs, sw['ffn1_pre_g'], [w_gu1], 'ffn1_fwd_a',
                                                        stages=[GatherStage([whole('ffn1_w_down'), part('w_in', 0)])])
    (f1, h1), ((w_in_b,),) = ffn_fwd_b(a1, w_down1, sw['ffn1_post_g'], xs, 'ffn1_fwd_b',
                                       stages=[GatherStage([part('w_in', 1)])])
    w_in_t = [w_in_a, w_in_b]
    (um, gate, xbr, q, k, v, g_lru, g_attn), ((w_gu2a,),) = mix_in(h1, sw['mix_pre_g'], w_in_t, 'mix_in',
                                                                   stages=[GatherStage([part('ffn2_w_gu', 0)])])
    (y_lru, h_lru), ((w_gu2b,),) = lru_fwd(gate, xbr, sw['conv_w'], sw['conv_b'], wa_bd, sw['lru_b_a'], wx_bd, sw['lru_b_x'],
                                           sw['lru_lambda'], 'lru_fwd', stages=[GatherStage([part('ffn2_w_gu', 1)])])
    (qr, kr, y_attn), (projs,) = attn_fwd(q, k, v, cos, sin_signed, sinks, 'attn_fwd',
                                          stages=[GatherStage([whole(n) for n in proj_names])])
    (p_l, p_a, merged, m, h2), ((w_down2,),) = merge_fwd(y_lru, y_attn, g_lru, g_attn, projs, sw['mix_post_g'], h1, 'merge_fwd',
                                                         stages=[GatherStage([whole('ffn2_w_down')])])
    w_gu2 = [w_gu2a, w_gu2b]
    (n2, g2, u2, a2), _ = ffn_fwd_a(h2, sw['ffn2_pre_g'], w_gu2, 'ffn2_fwd_a')
    (f2, dy, loss_blk), _ = ffn_fwd_b(a2, w_down2, sw['ffn2_post_g'], h2, 'ffn2_fwd_b', target=target)

    gs, full = {}, {}

    def pair_stage(names, grads):
        g4 = [g.reshape(N_CHIPS, 2, PACK_ROWS_OF[n] // 2, D_MODEL) for n, g in zip(names, grads)]
        return PairStage(g4), g4

    def pair_sums(names, g4, lands):
        return [pair_sum(g, l, c_arr, 'pair_sum_' + n) for n, g, l in zip(names, g4, lands)]

    def halves(s, n_parts=2):
        n = s.shape[1] // n_parts
        return [(s, p * n, n) for p in range(n_parts)]

    (df2, dgu2, gs['ffn2_post_g']), _ = ffn_bwd_a(dy, f2, sw['ffn2_post_g'], w_down2, g2, u2, 'ffn2_bwd_a')
    g_down2, _ = mm_tn([a2], df2, 1408, 'ffn2_dw_down')
    st, g4 = pair_stage(['ffn2_w_down'], [g_down2])
    g_gu2, (lands,) = mm_tn([dgu2], n2, 1408, 'ffn2_dw_gu', stages=[st])
    (s_down2,) = pair_sums(['ffn2_w_down'], g4, lands)
    st, g4 = pair_stage(['ffn2_w_gu'], [g_gu2])
    (dh2, gs['ffn2_pre_g']), ((l_down2,), lands) = norm_bwd([dgu2], w_gu2, h2, sw['ffn2_pre_g'], dy, 'ffn2_bwd_b',
                                                            stages=[ChipStage([(s_down2, 0, s_down2.shape[1])]), st])
    (s_gu2,) = pair_sums(['ffn2_w_gu'], g4, lands)

    (dm, dpl, dpa, dgl, dga, dya, dyl, gs['mix_post_g']), ((l_gu2a,),) = merge_bwd(
        dh2, m, sw['mix_post_g'], projs, g_lru, g_attn, p_l, p_a, 'merge_bwd', stages=[ChipStage(halves(s_gu2)[:1])])
    g_projs = [mm_tn([merged if n == 'w_out' else (y_lru if n == 'w_proj_lru' else y_attn)],
                     dm if n == 'w_out' else (dpl if n == 'w_proj_lru' else dpa), D_MODEL, 'd' + n)[0] for n in proj_names]
    st, g4 = pair_stage(proj_names, g_projs)
    (dq, dkv, dsk), ((l_gu2b,), lands, (full['ffn2_w_down'],)) = attn_bwd(
        qr, kr, v, dya, cos, sin_signed, sinks, 'attn_bwd', stages=[ChipStage(halves(s_gu2)[1:]), st, SwapStage([l_down2])])
    full['ffn2_w_down'] = [full['ffn2_w_down']]
    gs['attn_sinks'] = dsk[0:1, 0:N_Q_HEADS]
    s_projs = pair_sums(proj_names, g4, lands)
    (dgate, dxbr, vecs, dwa, dwx), (l_projs, full['ffn2_w_gu']) = lru_bwd(
        gate, xbr, h_lru, dyl, sw['conv_w'], sw['conv_b'], wa_bd, sw['lru_b_a'], wx_bd, sw['lru_b_x'], sw['lru_lambda'],
        'lru_bwd', stages=[ChipStage([(s, 0, s.shape[1]) for s in s_projs]), SwapStage([l_gu2a, l_gu2b])])
    gs['conv_w'] = vecs[0:4]
    gs['conv_b'], gs['lru_b_a'], gs['lru_b_x'], gs['lru_lambda'] = vecs[4:5], vecs[5:6], vecs[6:7], vecs[7:8]
    gs['lru_w_a'] = _diag_blocks(dwa)
    gs['lru_w_x'] = _diag_blocks(dwx)
    dz = [dgate, dxbr, dq, dkv, dgl, dga]
    g_in, ((lru_all,),) = mm_tn(dz, um, 512, 'dw_in', stages=[SmallGatherStage(_pack_lru(gs))])
    st, g4 = pair_stage(['w_in'], [g_in])
    (dh1, gs['mix_pre_g']), (lands, f_projs) = norm_bwd(dz, w_in_t, h1, sw['mix_pre_g'], dh2, 'mix_bwd_in',
                                                        stages=[st, SwapStage(l_projs)])
    for n, f in zip(proj_names, f_projs):
        full[n] = [f]
    (s_in,) = pair_sums(['w_in'], g4, lands)

    (df1, dgu1, gs['ffn1_post_g']), ((l_in_a,),) = ffn_bwd_a(dh1, f1, sw['ffn1_post_g'], w_down1, g1, u1, 'ffn1_bwd_a',
                                                             stages=[ChipStage(halves(s_in)[:1])])
    g_down1, _ = mm_tn([a1], df1, 1408, 'ffn1_dw_down')
    st, g4 = pair_stage(['ffn1_w_down'], [g_down1])
    g_gu1, ((l_in_b,), lands) = mm_tn([dgu1], n1, 1408, 'ffn1_dw_gu', stages=[ChipStage(halves(s_in)[1:]), st])
    (s_down1,) = pair_sums(['ffn1_w_down'], g4, lands)
    st, g4 = pair_stage(['ffn1_w_gu'], [g_gu1])
    (dx, gs['ffn1_pre_g']), ((l_down1,), lands, full['w_in']) = norm_bwd(
        [dgu1], [w_gu1], xs, sw['ffn1_pre_g'], dh1, 'ffn1_bwd_b',
        stages=[ChipStage([(s_down1, 0, s_down1.shape[1])]), st, SwapStage([l_in_a, l_in_b])])
    (s_gu1,) = pair_sums(['ffn1_w_gu'], g4, lands)
    loss_row = jnp.pad(loss_blk[0:1], ((0, 0), (0, D_MODEL - loss_blk.shape[1])))
    vec_blk = _pack_vecs(gs, jnp.concatenate([gs['conv_w'], loss_row], axis=0))
    send, recv, s_thru, land_thru, token = chip_exchange_start(s_gu1)
    out_g, out_d, out_m, out_v = {}, {}, {}, {}

    def adam(n, after=()):
        fn = adam_cols if dict((k, t) for k, _, t in PACK)[n] else adam_rows
        g_, d_, m_, v_ = fn(full[n], n, w[n][0], mom[n][0], var[n][0], after=after)
        out_g[n], out_d[n], out_m[n], out_v[n] = g_[None], d_[None], m_[None], v_[None]

    behind = token
    for n in ['ffn2_w_gu', 'w_in', 'ffn2_w_down'] + proj_names:
        adam(n, after=(behind,))
        behind = out_v[n]
    s_back, l_gu1 = chip_exchange_wait(send, recv, s_thru, land_thru, after=behind)
    own = lax.dynamic_slice_in_dim(s_back, me_q, 1, axis=0)
    l_gu1 = lax.dynamic_update_slice_in_dim(l_gu1, own, me_q, axis=0)
    (vec_all,), (f_down1, f_gu1) = comm_call('swap_last', [SmallGatherStage(vec_blk), SwapStage([l_down1, l_gu1])])
    full['ffn1_w_down'] = [f_down1]
    full['ffn1_w_gu'] = [f_gu1]
    adam('ffn1_w_gu')
    adam('ffn1_w_down')

    tot = small_sum(vec_all, lru_all)
    loss = tot[ROW_WA - 1, 0]
    conv_g = lax.dynamic_slice(tot[ROW_CONV:ROW_CONV + 4], (0, me_q * (LRU_W // N_CHIPS)), (4, LRU_W // N_CHIPS))
    small_g = _unpack_small(tot, shapes)
    small_g['conv_w'] = conv_g.reshape(shapes['conv_w'])
    g_pack = jnp.concatenate([tot[:ROW_CONV], conv_g.reshape(1, D_MODEL), jnp.zeros((ROW_WA - ROW_CONV - 1, D_MODEL), F32),
                              tot[ROW_WA:]], axis=0)
    packs = [_pack_small({n: d[n] for n in SMALL}, d['conv_w'].reshape(1, D_MODEL)) for d in (w, mom, var)]
    d_p, m_p, v_p = adam_small(g_pack, *packs)
    for n in SMALL:
        out_g[n] = small_g[n]
    for dst, p in ((out_d, d_p), (out_m, m_p), (out_v, v_p)):
        dst.update(_unpack_small(p, shapes))

    return (loss, dx[None], *[out_g[n] for n in WEIGHTS], *[out_d[n] for n in WEIGHTS],
            *[out_m[n] for n in WEIGHTS], *[out_v[n] for n in WEIGHTS])
```

```python
import jax
import jax.numpy as jnp
import numpy as np
from jax import lax
from jax.experimental import pallas as pl
from jax.experimental.pallas import tpu as pltpu

F32 = jnp.float32
BF16 = jnp.bfloat16

SEQ = 2048
D_MODEL = 1024
D_FF = 2816
LRU_W = 1024
LRU_BLOCK_W = 64
HEAD_DIM = 64
N_Q_HEADS = 16
N_KV_HEADS = 4
KV_W = N_KV_HEADS * HEAD_DIM
ATTN_BLOCK = 128
N_ATTN_BLOCKS = SEQ // ATTN_BLOCK
IN_SEGS = (1024, 1024, 1024, 256, 256, 1024, 1024)
IN_W = sum(IN_SEGS)
NORM_EPS = 1e-6
MASK_VALUE = -1e30
ROPE_THETA = 10000.0
LRU_C = 8.0
MACARON = 0.5
ADAM_LR = 0.001
ADAM_B1 = 0.9
ADAM_B2 = 0.999
ADAM_EPS = 1e-08
ADAM_WD = 0.01
ADAM_STEP = 10

N_CHIPS = 4
N_DEV = 8
VMEM_LIMIT = 56 * 1024 * 1024
MM_ROWS = 256
MESH = pl.DeviceIdType.MESH
ANY = pl.BlockSpec(memory_space=pl.ANY)

PACK = (('ffn1_w_gu', 1408, True), ('w_in', 1408, True), ('ffn2_w_gu', 1408, True),
        ('ffn1_w_down', 704, False), ('ffn2_w_down', 704, False),
        ('w_proj_lru', 256, False), ('w_proj_attn', 256, False), ('w_out', 256, False))
PACK_ROWS_OF = {n: r for n, r, _ in PACK}
PACK_OFF = {}
_o = 0
for _n, _r, _t in PACK:
    PACK_OFF[_n] = _o
    _o += _r

SMALL_VECS = ('ffn1_pre_g', 'ffn1_post_g', 'mix_pre_g', 'conv_b', 'lru_b_a', 'lru_b_x', 'lru_lambda',
              'mix_post_g', 'ffn2_pre_g', 'ffn2_post_g')
SMALL_ROWS = 144
ROW_SINKS, ROW_CONV, ROW_WA, ROW_WX = 10, 11, 16, 80


def _dot(a, b):
    return jnp.dot(a, b, preferred_element_type=F32)


def _dot_nt(a, b):
    return lax.dot_general(a, b, (((1,), (1,)), ((), ())), preferred_element_type=F32)


def _dot_tn(a, b):
    return lax.dot_general(a, b, (((0,), (0,)), ((), ())), preferred_element_type=F32)


def _params(n_grid):
    return pltpu.CompilerParams(dimension_semantics=("arbitrary",) * n_grid, vmem_limit_bytes=VMEM_LIMIT)


def _sigmoid(x):
    return 1.0 / (1.0 + jnp.exp(-x))


def _rsqrt_mean_sq(x):
    return lax.rsqrt(jnp.mean(x * x, axis=-1, keepdims=True) + NORM_EPS)


def _expm1(x):
    poly = x * (1.0 + x * (0.5 + x * (1.0 / 6.0)))
    return jnp.where(jnp.abs(x) < 0.02, poly, jnp.exp(x) - 1.0)


_GELU_K = 0.7978845608028654
_GELU_C = 0.044715


def _gelu(x):
    t = jnp.tanh(_GELU_K * (x + _GELU_C * x * x * x))
    return 0.5 * x * (1.0 + t), t


def _gelu_grad(x, t):
    return 0.5 * (1.0 + t) + 0.5 * x * (1.0 - t * t) * _GELU_K * (1.0 + 3.0 * _GELU_C * x * x)


def _load_weight(w_refs, dst_ref, sem):
    w_refs = list(w_refs) if isinstance(w_refs, (list, tuple)) else [w_refs]
    rows = dst_ref.shape[0] // N_CHIPS
    assert sum(w_ref.shape[1] for w_ref in w_refs) == rows
    cps, off = [], 0
    for p, w_ref in enumerate(w_refs):
        rp = w_ref.shape[1]
        cps += [pltpu.make_async_copy(w_ref.at[q], dst_ref.at[pl.ds(q * rows + off, rp)], sem.at[p * N_CHIPS + q])
                for q in range(N_CHIPS)]
        off += rp
    for cp in cps:
        cp.start()
    for cp in cps:
        cp.wait()


def _weight_scratch(rows_total, parts=1):
    return [pltpu.VMEM((rows_total, D_MODEL), BF16), pltpu.SemaphoreType.DMA((N_CHIPS * parts,))]


_ROW = lambda tm: pl.BlockSpec((tm, D_MODEL), lambda i: (i, 0))
_VEC = pl.BlockSpec((1, D_MODEL), lambda i: (0, 0))


def _call(body, *, name, grid, in_specs, out_specs, out_shape, args, scratch_shapes=(), stages=()):
    in_specs, out_specs, out_shape, scratch_shapes = list(in_specs), list(out_specs), list(out_shape), list(scratch_shapes)
    n_in, n_out, n_sc = len(in_specs), len(out_specs), len(scratch_shapes)
    k_in = [len(s.inputs) for s in stages]
    k_out = [len(s.out_shape) for s in stages]
    k_sc = [len(s.scratch) for s in stages]
    last = grid[0] - 1

    def split(refs, counts):
        parts, pos = [], 0
        for k in counts:
            parts.append(refs[pos:pos + k])
            pos += k
        return parts

    kinds = tuple(sorted({k for s in stages for k in s.peers}))
    collective_id = {(): None, ('sib',): 0, ('chips',): 1, ('chips', 'sib'): 2}[kinds]

    def full(*refs):
        ins, s_ins, outs, s_outs, scr, s_scr = split(refs, [n_in, sum(k_in), n_out, sum(k_out), n_sc, sum(k_sc)])
        per_stage = list(zip(stages, split(s_ins, k_in), split(s_outs, k_out), split(s_scr, k_sc)))
        i = pl.program_id(0)
        if stages:
            @pl.when(i == 0)
            def _():
                x, y, c, chips = _place()
                peers = ([(x, y, 1 - c)] if 'sib' in kinds else []) + ([(cx, cy, c) for cx, cy in chips] if 'chips' in kinds else [])
                barrier = pltpu.get_barrier_semaphore()
                for peer in peers:
                    pl.semaphore_signal(barrier, inc=1, device_id=peer, device_id_type=MESH)
                pl.semaphore_wait(barrier, len(peers))
                for s, a, b, c_ in per_stage:
                    s.start(a, b, c_)

        body(*ins, *outs, *scr)
        if stages:
            @pl.when(i == last // 2)
            def _():
                for s, a, b, c in per_stage:
                    s.relay(a, b, c)

            @pl.when(i == max(last - 1, 0))
            def _():
                for s, a, b, c in per_stage:
                    s.mid(a, b, c)

            @pl.when(i == last)
            def _():
                for s, a, b, c in per_stage:
                    s.end(a, b, c)

    res = pl.pallas_call(
        full, name=name, grid=grid,
        in_specs=in_specs + [ANY] * sum(k_in),
        out_specs=out_specs + [ANY] * sum(k_out),
        out_shape=out_shape + [o for s in stages for o in s.out_shape],
        scratch_shapes=scratch_shapes + [x for s in stages for x in s.scratch],
        compiler_params=pltpu.CompilerParams(dimension_semantics=("arbitrary",), vmem_limit_bytes=VMEM_LIMIT,
                                             collective_id=collective_id),
    )(*args, *[a for s in stages for a in s.inputs])
    return list(res[:n_out]), split(list(res[n_out:]), k_out)


def ffn_fwd_a(x, g_pre, w_gu_t, name, stages=()):
    tm, tn = MM_ROWS, 256
    n_w = len(w_gu_t)

    def body(x_ref, gp_ref, *refs):
        w_refs = refs[:n_w]
        n_ref, g_ref, u_ref, a_ref, wt_ref, sem = refs[n_w:]

        @pl.when(pl.program_id(0) == 0)
        def _():
            _load_weight(w_refs, wt_ref, sem)

        xv = x_ref[...]
        n = (xv * _rsqrt_mean_sq(xv) * gp_ref[...]).astype(BF16)
        n_ref[...] = n
        for j in range(D_FF // tn):
            g = _dot_nt(n, wt_ref[j * tn:(j + 1) * tn, :])
            u = _dot_nt(n, wt_ref[D_FF + j * tn:D_FF + (j + 1) * tn, :])
            g_ref[:, j * tn:(j + 1) * tn] = g.astype(BF16)
            u_ref[:, j * tn:(j + 1) * tn] = u.astype(BF16)
            a_ref[:, j * tn:(j + 1) * tn] = (g * _sigmoid(g) * u).astype(BF16)

    wide = pl.BlockSpec((tm, D_FF), lambda i: (i, 0))
    return _call(
        body, name=name, grid=(SEQ // tm,),
        in_specs=[_ROW(tm), _VEC] + [ANY] * n_w,
        out_specs=[_ROW(tm), wide, wide, wide],
        out_shape=[jax.ShapeDtypeStruct((SEQ, D_MODEL), BF16)] + [jax.ShapeDtypeStruct((SEQ, D_FF), BF16)] * 3,
        scratch_shapes=_weight_scratch(2 * D_FF, n_w),
        args=[x, g_pre, *w_gu_t], stages=stages)


def ffn_fwd_b(a, w_down, g_post, h_in, name, target=None, stages=()):
    tm = MM_ROWS
    final = target is not None
    n_w = len(w_down)

    def body(*refs):
        w_refs = refs[:n_w]
        if final:
            a_ref, gp_ref, h_ref, t_ref, f_ref, o_ref, loss_ref, wd_ref, sem = refs[n_w:]
        else:
            a_ref, gp_ref, h_ref, f_ref, o_ref, wd_ref, sem = refs[n_w:]

        @pl.when(pl.program_id(0) == 0)
        def _():
            _load_weight(w_refs, wd_ref, sem)
            if final:
                loss_ref[...] = jnp.zeros_like(loss_ref)

        f = _dot(a_ref[...], wd_ref[...])
        f_ref[...] = f
        y = h_ref[...] + MACARON * (f * _rsqrt_mean_sq(f) * gp_ref[...])
        if final:
            err = y - t_ref[...]
            o_ref[...] = err * (1.0 / D_MODEL)
            loss_ref[...] += 0.5 * jnp.sum(err * err) * (1.0 / D_MODEL)
        else:
            o_ref[...] = y

    row = _ROW(tm)
    in_specs = [ANY] * n_w + [pl.BlockSpec((tm, D_FF), lambda i: (i, 0)), _VEC, row]
    out_specs = [row, row]
    out_shape = [jax.ShapeDtypeStruct((SEQ, D_MODEL), F32)] * 2
    args = [*w_down, a, g_post, h_in]
    if final:
        in_specs.append(row)
        args.append(target)
        out_specs.append(pl.BlockSpec((8, 128), lambda i: (0, 0)))
        out_shape.append(jax.ShapeDtypeStruct((8, 128), F32))
    return _call(body, name=name, grid=(SEQ // tm,), in_specs=in_specs, out_specs=out_specs,
                 out_shape=out_shape, scratch_shapes=_weight_scratch(D_FF, n_w), args=args, stages=stages)


def ffn_bwd_a(d_out, f, g_post, w_down, g, u, name, stages=()):
    tm = MM_ROWS
    tc = 256
    n_w = len(w_down)

    def body(*refs):
        w_refs = refs[:n_w]
        do_ref, f_ref, gp_ref, g_ref, u_ref, df_ref, dgu_ref, dgp_ref, wd_ref, sem = refs[n_w:]

        @pl.when(pl.program_id(0) == 0)
        def _():
            _load_weight(w_refs, wd_ref, sem)
            dgp_ref[...] = jnp.zeros_like(dgp_ref)

        fv = f_ref[...]
        rf = _rsqrt_mean_sq(fv)
        fh = fv * rf
        dn = MACARON * do_ref[...]
        dgp_ref[...] += jnp.sum(dn * fh, axis=0, keepdims=True)
        t = dn * gp_ref[...]
        df = (rf * (t - fh * jnp.mean(t * fh, axis=-1, keepdims=True))).astype(BF16)
        df_ref[...] = df
        for c0 in range(0, D_FF, tc):
            da = _dot_nt(df, wd_ref[c0:c0 + tc, :])
            gv = g_ref[:, c0:c0 + tc].astype(F32)
            uv = u_ref[:, c0:c0 + tc].astype(F32)
            s = _sigmoid(gv)
            dgu_ref[:, c0:c0 + tc] = (da * uv * s * (1.0 + gv * (1.0 - s))).astype(BF16)
            dgu_ref[:, D_FF + c0:D_FF + c0 + tc] = (da * gv * s).astype(BF16)

    row = _ROW(tm)
    wide = pl.BlockSpec((tm, D_FF), lambda i: (i, 0))
    return _call(
        body, name=name, grid=(SEQ // tm,),
        in_specs=[ANY] * n_w + [row, row, _VEC, wide, wide],
        out_specs=[row, pl.BlockSpec((tm, 2 * D_FF), lambda i: (i, 0)), _VEC],
        out_shape=[jax.ShapeDtypeStruct((SEQ, D_MODEL), BF16), jax.ShapeDtypeStruct((SEQ, 2 * D_FF), BF16),
                   jax.ShapeDtypeStruct((1, D_MODEL), F32)],
        scratch_shapes=_weight_scratch(D_FF, n_w),
        args=[*w_down, d_out, f, g_post, g, u], stages=stages)


def norm_bwd(pieces, w_t, x, g_pre, d_res, name, stages=()):
    tm = MM_ROWS
    widths = [p.shape[1] for p in pieces]
    offs = [sum(widths[:k]) for k in range(len(widths))]
    n_p = len(pieces)
    n_w = len(w_t)

    def body(*refs):
        p_refs = refs[:n_p]
        w_refs = refs[n_p:n_p + n_w]
        x_ref, g_ref, r_ref, dx_ref, dg_ref, wt_ref, sem = refs[n_p + n_w:]

        @pl.when(pl.program_id(0) == 0)
        def _():
            _load_weight(w_refs, wt_ref, sem)
            dg_ref[...] = jnp.zeros_like(dg_ref)

        dn = None
        for p_ref, lo, wd in zip(p_refs, offs, widths):
            part = _dot(p_ref[...], wt_ref[lo:lo + wd, :])
            dn = part if dn is None else dn + part
        xv = x_ref[...]
        r = _rsqrt_mean_sq(xv)
        xh = xv * r
        dg_ref[...] += jnp.sum(dn * xh, axis=0, keepdims=True)
        t = dn * g_ref[...]
        dx_ref[...] = r_ref[...] + r * (t - xh * jnp.mean(t * xh, axis=-1, keepdims=True))

    row = _ROW(tm)
    return _call(
        body, name=name, grid=(SEQ // tm,),
        in_specs=[pl.BlockSpec((tm, wd), lambda i: (i, 0)) for wd in widths] + [ANY] * n_w + [row, _VEC, row],
        out_specs=[row, _VEC],
        out_shape=[jax.ShapeDtypeStruct((SEQ, D_MODEL), F32), jax.ShapeDtypeStruct((1, D_MODEL), F32)],
        scratch_shapes=_weight_scratch(sum(widths), n_w),
        args=[*pieces, *w_t, x, g_pre, d_res], stages=stages)


def mm_tn(pieces, b, tm, name, stages=()):
    widths = [p.shape[1] for p in pieces]
    m_total = sum(widths)
    n_p = len(pieces)
    starts = [sum(widths[:k]) // tm for k in range(n_p)]
    counts = [wd // tm for wd in widths]

    def body(*refs):
        p_refs = refs[:n_p]
        b_ref, o_ref = refs[n_p:]
        i = pl.program_id(0)
        for p_ref, st, ct in zip(p_refs, starts, counts):
            @pl.when((i >= st) & (i < st + ct))
            def _(p_ref=p_ref):
                o_ref[...] = _dot_tn(p_ref[...], b_ref[...]).astype(BF16)

    def piece_spec(st, ct):
        return pl.BlockSpec((SEQ, tm), lambda i: (0, jnp.clip(i - st, 0, ct - 1)))

    (out,), stage_out = _call(
        body, name=name, grid=(m_total // tm,),
        in_specs=[piece_spec(st, ct) for st, ct in zip(starts, counts)] + [pl.BlockSpec((SEQ, D_MODEL), lambda i: (0, 0))],
        out_specs=[pl.BlockSpec((tm, D_MODEL), lambda i: (i, 0))],
        out_shape=[jax.ShapeDtypeStruct((m_total, D_MODEL), BF16)],
        args=[*pieces, b], stages=stages)
    return out, stage_out


def mix_in(h, g_pre, w_in_t, name, stages=()):
    tm = MM_ROWS
    offs = [sum(IN_SEGS[:k]) for k in range(len(IN_SEGS))]
    dts = [F32, F32, F32, F32, BF16, F32, F32]
    n_o = len(IN_SEGS)
    n_w = len(w_in_t)

    def body(*refs):
        h_ref, g_ref = refs[:2]
        w_refs = refs[2:2 + n_w]
        um_ref = refs[2 + n_w]
        o_refs = refs[3 + n_w:3 + n_w + n_o]
        wt_ref, sem = refs[3 + n_w + n_o:]

        @pl.when(pl.program_id(0) == 0)
        def _():
            _load_weight(w_refs, wt_ref, sem)

        hv = h_ref[...]
        um = (hv * _rsqrt_mean_sq(hv) * g_ref[...]).astype(BF16)
        um_ref[...] = um
        for o_ref, lo, wd in zip(o_refs, offs, IN_SEGS):
            for c0 in range(0, wd, 256):
                o_ref[:, c0:c0 + 256] = _dot_nt(um, wt_ref[lo + c0:lo + c0 + 256, :]).astype(o_ref.dtype)

    return _call(
        body, name=name, grid=(SEQ // tm,),
        in_specs=[_ROW(tm), _VEC] + [ANY] * n_w,
        out_specs=[_ROW(tm)] + [pl.BlockSpec((tm, wd), lambda i: (i, 0)) for wd in IN_SEGS],
        out_shape=[jax.ShapeDtypeStruct((SEQ, D_MODEL), BF16)]
        + [jax.ShapeDtypeStruct((SEQ, wd), dt) for wd, dt in zip(IN_SEGS, dts)],
        scratch_shapes=_weight_scratch(IN_W, n_w),
        args=[h, g_pre, *w_in_t], stages=stages)


LRU_TC = 256


def _conv_fwd(xb, cw, cb, tt):
    xc = xb * cw[3:4, :] + cb
    shifted = []
    for s in (1, 2, 3):
        sh = jnp.where(tt >= s, pltpu.roll(xb, s, 0), 0.0)
        shifted.append(sh)
        xc = xc + sh * cw[3 - s:4 - s, :]
    return xc, shifted


def _lru_gates(xc, wa, ba, wx, bx, lam):
    xcb = xc.astype(BF16)
    r = _sigmoid(_dot(xcb, wa) + ba)
    i = _sigmoid(_dot(xcb, wx) + bx)
    nl = -lam
    sp = jnp.maximum(nl, 0.0) + jnp.log1p(jnp.exp(-jnp.abs(nl)))
    la = (-LRU_C * r) * sp
    a = jnp.exp(la)
    mult = jnp.sqrt(jnp.maximum(-_expm1(2.0 * la), 0.0))
    return xcb, r, i, sp, a, mult


def _scan(a, b, tt, reverse, a_s, b_s):
    n = a.shape[0]
    tg = tt & 7
    for s in (1, 2, 4):
        keep = (tg < 8 - s) if reverse else (tg >= s)
        shift = n - s if reverse else s
        b = a * jnp.where(keep, pltpu.roll(b, shift, 0), 0.0) + b
        a = a * jnp.where(keep, pltpu.roll(a, shift, 0), 1.0)
    a_s[...] = a
    b_s[...] = b
    groups = n // 8

    def step(g, carry):
        gi = (groups - 1 - g) if reverse else g
        rows = pl.ds(pl.multiple_of(gi * 8, 8), 8)
        hg = a_s[rows, :] * carry + b_s[rows, :]
        b_s[rows, :] = hg
        return hg[0:1, :] if reverse else hg[7:8, :]

    lax.fori_loop(0, groups, step, jnp.zeros((1, a.shape[1]), F32), unroll=8)
    return b_s[...]


def _lru_specs():
    col = pl.BlockSpec((SEQ, LRU_TC), lambda j: (0, j))
    vec = pl.BlockSpec((1, LRU_TC), lambda j: (0, j))
    bd = pl.BlockSpec((1, LRU_TC, LRU_TC), lambda j: (j, 0, 0))
    cw = pl.BlockSpec((4, LRU_TC), lambda j: (0, j))
    return col, vec, bd, cw


def lru_fwd(gate, xbr, conv_w, conv_b, wa_bd, b_a, wx_bd, b_x, lam, name, stages=()):
    col, vec, bd, cw = _lru_specs()

    def body(gate_ref, xbr_ref, cw_ref, cb_ref, wa_ref, ba_ref, wx_ref, bx_ref, lam_ref, y_ref, h_ref, a_s, b_s):
        tt = lax.broadcasted_iota(jnp.int32, (SEQ, LRU_TC), 0)
        xc, _ = _conv_fwd(xbr_ref[...], cw_ref[...], cb_ref[...], tt)
        _, r, i, sp, a, mult = _lru_gates(xc, wa_ref[0], ba_ref[...], wx_ref[0], bx_ref[...], lam_ref[...])
        h = _scan(a, mult * (i * xc), tt, False, a_s, b_s)
        h_ref[...] = h
        gl, _ = _gelu(gate_ref[...])
        y_ref[...] = (h * gl).astype(BF16)

    return _call(
        body, name=name, grid=(LRU_W // LRU_TC,),
        in_specs=[col, col, cw, vec, bd, vec, bd, vec, vec],
        out_specs=[col, col],
        out_shape=[jax.ShapeDtypeStruct((SEQ, LRU_W), BF16), jax.ShapeDtypeStruct((SEQ, LRU_W), F32)],
        scratch_shapes=[pltpu.VMEM((SEQ, LRU_TC), F32)] * 2,
        args=[gate, xbr, conv_w, conv_b, wa_bd, b_a, wx_bd, b_x, lam], stages=stages)


def lru_bwd(gate, xbr, h, dy, conv_w, conv_b, wa_bd, b_a, wx_bd, b_x, lam, name, stages=()):
    col, vec, bd, cw = _lru_specs()

    def body(gate_ref, xbr_ref, h_ref, dy_ref, cw_ref, cb_ref, wa_ref, ba_ref, wx_ref, bx_ref, lam_ref,
             dgate_ref, dxbr_ref, vecs_ref, dwa_ref, dwx_ref, a_s, b_s):
        tt = lax.broadcasted_iota(jnp.int32, (SEQ, LRU_TC), 0)
        cwv = cw_ref[...]
        lam = lam_ref[...]
        xb = xbr_ref[...]
        xc, shifted = _conv_fwd(xb, cwv, cb_ref[...], tt)
        wa = wa_ref[0]
        wx = wx_ref[0]
        xcb, r, i, sp, a, mult = _lru_gates(xc, wa, ba_ref[...], wx, bx_ref[...], lam)
        hv = h_ref[...]
        dyv = dy_ref[...]
        gv = gate_ref[...]
        gl, th = _gelu(gv)
        dgate_ref[...] = (dyv * hv * _gelu_grad(gv, th)).astype(BF16)
        a_next = jnp.where(tt < SEQ - 1, pltpu.roll(a, SEQ - 1, 0), 0.0)
        gsum = _scan(a_next, dyv * gl, tt, True, a_s, b_s)
        h_prev = jnp.where(tt >= 1, pltpu.roll(hv, 1, 0), 0.0)
        d_mult = gsum * i * xc
        d_i = gsum * mult * xc
        d_xc = gsum * mult * i
        d_la = gsum * h_prev * a - d_mult * (a * a) / mult
        d_pr = (d_la * (-LRU_C * sp)) * r * (1.0 - r)
        d_pi = d_i * i * (1.0 - i)
        d_lam = jnp.sum(d_la * r, axis=0, keepdims=True) * (LRU_C * _sigmoid(-lam))
        d_prb = d_pr.astype(BF16)
        d_pib = d_pi.astype(BF16)
        d_xc = d_xc + _dot_nt(d_prb, wa) + _dot_nt(d_pib, wx)
        dwa_ref[0] = _dot_tn(xcb, d_prb)
        dwx_ref[0] = _dot_tn(xcb, d_pib)
        rows = [jnp.sum(d_xc * shifted[2], axis=0, keepdims=True),
                jnp.sum(d_xc * shifted[1], axis=0, keepdims=True),
                jnp.sum(d_xc * shifted[0], axis=0, keepdims=True),
                jnp.sum(d_xc * xb, axis=0, keepdims=True),
                jnp.sum(d_xc, axis=0, keepdims=True),
                jnp.sum(d_pr, axis=0, keepdims=True),
                jnp.sum(d_pi, axis=0, keepdims=True),
                d_lam]
        ri = lax.broadcasted_iota(jnp.int32, (8, LRU_TC), 0)
        acc = jnp.zeros((8, LRU_TC), F32)
        for k, rv in enumerate(rows):
            acc = jnp.where(ri == k, rv, acc)
        vecs_ref[...] = acc
        d_xb = d_xc * cwv[3:4, :]
        for s in (1, 2, 3):
            d_xb = d_xb + jnp.where(tt < SEQ - s, pltpu.roll(d_xc, SEQ - s, 0), 0.0) * cwv[3 - s:4 - s, :]
        dxbr_ref[...] = d_xb.astype(BF16)

    return _call(
        body, name=name, grid=(LRU_W // LRU_TC,),
        in_specs=[col, col, col, col, cw, vec, bd, vec, bd, vec, vec],
        out_specs=[col, col, pl.BlockSpec((8, LRU_TC), lambda j: (0, j)), bd, bd],
        out_shape=[jax.ShapeDtypeStruct((SEQ, LRU_W), BF16), jax.ShapeDtypeStruct((SEQ, LRU_W), BF16),
                   jax.ShapeDtypeStruct((8, LRU_W), F32),
                   jax.ShapeDtypeStruct((LRU_W // LRU_TC, LRU_TC, LRU_TC), F32),
                   jax.ShapeDtypeStruct((LRU_W // LRU_TC, LRU_TC, LRU_TC), F32)],
        scratch_shapes=[pltpu.VMEM((SEQ, LRU_TC), F32)] * 2,
        args=[gate, xbr, h, dy, conv_w, conv_b, wa_bd, b_a, wx_bd, b_x, lam], stages=stages)


def _rope(x, cos, sin_signed):
    w = x.shape[1]
    reps = w // 128
    if reps > 1:
        cos = jnp.tile(cos, (1, reps))
        sin_signed = jnp.tile(sin_signed, (1, reps))
    lane = lax.broadcasted_iota(jnp.int32, x.shape, 1)
    first = (lane & 63) < 32
    partner = jnp.where(first, pltpu.roll(x, w - 32, 1), pltpu.roll(x, 32, 1))
    return x * cos + partner * sin_signed


def _both_halves(t, odd):
    lo = lax.broadcasted_iota(jnp.int32, t.shape, 1) < 64
    rolled = pltpu.roll(t, 64, 1)
    return jnp.where(lo, rolled, t) if odd else jnp.where(lo, t, rolled)


def _stack_heads(ta, tb):
    lo = lax.broadcasted_iota(jnp.int32, ta.shape, 1) < 64
    return jnp.concatenate([jnp.where(lo, ta, 0.0), jnp.where(lo, 0.0, ta),
                            jnp.where(lo, tb, 0.0), jnp.where(lo, 0.0, tb)], axis=0)


def _unstack_heads(o):
    lo = lax.broadcasted_iota(jnp.int32, (ATTN_BLOCK, 128), 1) < 64
    return (jnp.where(lo, o[0:128], o[128:256]), jnp.where(lo, o[256:384], o[384:512]))


def _window_upper_t():
    shape = (ATTN_BLOCK, 4 * ATTN_BLOCK)
    return lax.broadcasted_iota(jnp.int32, shape, 0) > (lax.broadcasted_iota(jnp.int32, shape, 1) & (ATTN_BLOCK - 1))


def _fold_t(t, upper_t):
    return jnp.where(upper_t, t[:ATTN_BLOCK], t[ATTN_BLOCK:])


def _unfold_t(t, upper_t):
    zero = jnp.zeros_like(t)
    return jnp.concatenate([jnp.where(upper_t, t, zero), jnp.where(upper_t, zero, t)], axis=0)


def _attn_probs_t(kd, qs, sinks_ref, hk, first_block, upper_t):
    s = _fold_t(_dot_nt(kd, qs), upper_t) * (HEAD_DIM ** -0.5)
    s = jnp.where(jnp.logical_and(upper_t, first_block), MASK_VALUE, s)
    rg = lax.broadcasted_iota(jnp.int32, (1, 4 * ATTN_BLOCK), 1) >> 7
    sink = jnp.where(rg == 0, sinks_ref[4 * hk],
                     jnp.where(rg == 1, sinks_ref[4 * hk + 1],
                               jnp.where(rg == 2, sinks_ref[4 * hk + 2], sinks_ref[4 * hk + 3])))
    m = jnp.maximum(jnp.max(s, axis=0, keepdims=True), sink)
    e = jnp.exp(s - m)
    es = jnp.exp(sink - m)
    inv = 1.0 / (jnp.sum(e, axis=0, keepdims=True) + es)
    return e * inv, es * inv


def _prev(i):
    return jnp.maximum(i - 1, 0)


def attn_fwd(q, k, v, cos, sin_signed, sinks, name, stages=()):
    nb = ATTN_BLOCK

    def body(q_ref, kc_ref, kp_ref, vc_ref, vp_ref, cc_ref, sc_ref, cp_ref, sp_ref, sinks_ref,
             qr_ref, kr_ref, y_ref):
        first_block = pl.program_id(0) == 0
        qr = _rope(q_ref[...], cc_ref[...], sc_ref[...])
        kc = _rope(kc_ref[...], cc_ref[...], sc_ref[...])
        kp = _rope(kp_ref[...], cp_ref[...], sp_ref[...])
        qr_ref[...] = qr.astype(BF16)
        kr_ref[...] = kc.astype(BF16)
        k2 = jnp.concatenate([kp, kc], axis=0)
        v2 = jnp.concatenate([vp_ref[...].astype(F32), vc_ref[...].astype(F32)], axis=0)
        upper_t = _window_upper_t()
        for hk in range(N_KV_HEADS):
            kt = hk // 2
            kd = _both_halves(k2[:, kt * 128:(kt + 1) * 128], hk % 2).astype(BF16)
            vd = _both_halves(v2[:, kt * 128:(kt + 1) * 128], hk % 2).astype(BF16)
            qs = _stack_heads(qr[:, (2 * hk) * 128:(2 * hk + 1) * 128],
                              qr[:, (2 * hk + 1) * 128:(2 * hk + 2) * 128]).astype(BF16)
            p, _ = _attn_probs_t(kd, qs, sinks_ref, hk, first_block, upper_t)
            ta, tb = _unstack_heads(_dot_tn(_unfold_t(p.astype(BF16), upper_t), vd))
            y_ref[:, (2 * hk) * 128:(2 * hk + 1) * 128] = ta.astype(BF16)
            y_ref[:, (2 * hk + 1) * 128:(2 * hk + 2) * 128] = tb.astype(BF16)

    cur = lambda w: pl.BlockSpec((nb, w), lambda i: (i, 0))
    prv = lambda w: pl.BlockSpec((nb, w), lambda i: (_prev(i), 0))
    return _call(
        body, name=name, grid=(N_ATTN_BLOCKS,),
        in_specs=[cur(D_MODEL), cur(KV_W), prv(KV_W), cur(KV_W), prv(KV_W), cur(128), cur(128), prv(128), prv(128),
                  pl.BlockSpec(memory_space=pltpu.SMEM)],
        out_specs=[cur(D_MODEL), cur(KV_W), cur(D_MODEL)],
        out_shape=[jax.ShapeDtypeStruct((SEQ, D_MODEL), BF16), jax.ShapeDtypeStruct((SEQ, KV_W), BF16),
                   jax.ShapeDtypeStruct((SEQ, D_MODEL), BF16)],
        args=[q, k, k, v, v, cos, sin_signed, cos, sin_signed, sinks], stages=stages)


def attn_bwd(qr, kr, v, dy, cos, sin_signed, sinks, name, stages=()):
    nb = ATTN_BLOCK
    n_steps = N_ATTN_BLOCKS + 1
    scale = HEAD_DIM ** -0.5

    def body(q_ref, kc_ref, kp_ref, vc_ref, vp_ref, dy_ref, cc_ref, sc_ref, cp_ref, sp_ref, sinks_ref,
             dq_ref, dkv_ref, dsk_ref, ck_ref, cv_ref):
        dk_ref = dkv_ref.at[:, pl.ds(0, KV_W)]
        dv_ref = dkv_ref.at[:, pl.ds(KV_W, KV_W)]
        i = pl.program_id(0)

        @pl.when(i == 0)
        def _():
            dsk_ref[...] = jnp.zeros_like(dsk_ref)
            ck_ref[...] = jnp.zeros_like(ck_ref)
            cv_ref[...] = jnp.zeros_like(cv_ref)

        @pl.when(i < N_ATTN_BLOCKS)
        def _():
            qv = q_ref[...].astype(F32)
            dov = dy_ref[...].astype(F32)
            k2 = jnp.concatenate([kp_ref[...].astype(F32), kc_ref[...].astype(F32)], axis=0)
            v2 = jnp.concatenate([vp_ref[...].astype(F32), vc_ref[...].astype(F32)], axis=0)
            lane = lax.broadcasted_iota(jnp.int32, (8, 128), 1)
            lo = lax.broadcasted_iota(jnp.int32, (2 * nb, 128), 1) < 64
            dsk = jnp.zeros((8, 128), F32)
            dk_tiles = []
            dv_tiles = []
            upper_t = _window_upper_t()
            for hk in range(N_KV_HEADS):
                kt = hk // 2
                kd = _both_halves(k2[:, kt * 128:(kt + 1) * 128], hk % 2).astype(BF16)
                vd = _both_halves(v2[:, kt * 128:(kt + 1) * 128], hk % 2).astype(BF16)
                qs = _stack_heads(qv[:, (2 * hk) * 128:(2 * hk + 1) * 128],
                                  qv[:, (2 * hk + 1) * 128:(2 * hk + 2) * 128]).astype(BF16)
                dos = _stack_heads(dov[:, (2 * hk) * 128:(2 * hk + 1) * 128],
                                   dov[:, (2 * hk + 1) * 128:(2 * hk + 2) * 128]).astype(BF16)
                p, ps = _attn_probs_t(kd, qs, sinks_ref, hk, i == 0, upper_t)
                dp = _fold_t(_dot_nt(vd, dos), upper_t)
                delta = jnp.sum(p * dp, axis=0, keepdims=True)
                ds = _unfold_t((p * (dp - delta)).astype(BF16), upper_t)
                dsink = -ps * delta
                for g in range(4):
                    dsk = dsk + jnp.where(lane == 4 * hk + g, jnp.sum(dsink[:, g * nb:(g + 1) * nb]), 0.0)
                ta, tb = _unstack_heads(_dot_tn(ds, kd) * scale)
                dq_a = (2 * hk) * 128
                dq_ref[:, dq_a:dq_a + 128] = _rope(ta, cc_ref[...], -sc_ref[...]).astype(BF16)
                dq_ref[:, dq_a + 128:dq_a + 256] = _rope(tb, cc_ref[...], -sc_ref[...]).astype(BF16)
                rk = _dot(ds, qs) * scale
                rv = _dot(_unfold_t(p.astype(BF16), upper_t), dos)
                dk_tiles.append(rk + pltpu.roll(rk, 64, 1))
                dv_tiles.append(rv + pltpu.roll(rv, 64, 1))
            dsk_ref[...] += dsk
            dk_full = jnp.concatenate([jnp.where(lo, dk_tiles[0], dk_tiles[1]),
                                       jnp.where(lo, dk_tiles[2], dk_tiles[3])], axis=1)
            dv_full = jnp.concatenate([jnp.where(lo, dv_tiles[0], dv_tiles[1]),
                                       jnp.where(lo, dv_tiles[2], dv_tiles[3])], axis=1)
            dk_ref[...] = _rope(ck_ref[...] + dk_full[0:nb], cp_ref[...], -sp_ref[...]).astype(BF16)
            dv_ref[...] = (cv_ref[...] + dv_full[0:nb]).astype(BF16)
            ck_ref[...] = dk_full[nb:2 * nb]
            cv_ref[...] = dv_full[nb:2 * nb]

        @pl.when(i == N_ATTN_BLOCKS)
        def _():
            dk_ref[...] = _rope(ck_ref[...], cp_ref[...], -sp_ref[...]).astype(BF16)
            dv_ref[...] = cv_ref[...].astype(BF16)

    qi = lambda i: jnp.minimum(i, N_ATTN_BLOCKS - 1)
    cur = lambda w: pl.BlockSpec((nb, w), lambda i: (qi(i), 0))
    prv = lambda w: pl.BlockSpec((nb, w), lambda i: (_prev(qi(i)), 0))
    out_prev = lambda w: pl.BlockSpec((nb, w), lambda i: (_prev(i), 0))
    return _call(
        body, name=name, grid=(n_steps,),
        in_specs=[cur(D_MODEL), cur(KV_W), prv(KV_W), cur(KV_W), prv(KV_W), cur(D_MODEL),
                  cur(128), cur(128), out_prev(128), out_prev(128), pl.BlockSpec(memory_space=pltpu.SMEM)],
        out_specs=[cur(D_MODEL), out_prev(2 * KV_W), pl.BlockSpec((8, 128), lambda i: (0, 0))],
        out_shape=[jax.ShapeDtypeStruct((SEQ, D_MODEL), BF16), jax.ShapeDtypeStruct((SEQ, 2 * KV_W), BF16),
                   jax.ShapeDtypeStruct((8, 128), F32)],
        scratch_shapes=[pltpu.VMEM((nb, KV_W), F32), pltpu.VMEM((nb, KV_W), F32)],
        args=[qr, kr, kr, v, v, dy, cos, sin_signed, cos, sin_signed, sinks], stages=stages)


def _proj_scratch():
    return [pltpu.VMEM((D_MODEL, D_MODEL), BF16)] * 3 + [pltpu.SemaphoreType.DMA((3 * N_CHIPS,))]


def _load_projs(w_refs, wl_ref, wa_ref, wo_ref, sem):
    for k, (w_ref, dst) in enumerate(zip(w_refs, (wl_ref, wa_ref, wo_ref))):
        _load_weight(w_ref, dst, sem.at[pl.ds(k * N_CHIPS, N_CHIPS)])


def merge_fwd(y_lru, y_attn, g_lru, g_attn, projs, g_post, h_in, name, stages=()):
    tm = MM_ROWS

    def body(yl_ref, ya_ref, gl_ref, ga_ref, w1_ref, w2_ref, w3_ref, gp_ref, h_ref,
             pl_ref, pa_ref, mg_ref, m_ref, o_ref, wl_ref, wa_ref, wo_ref, sem):
        @pl.when(pl.program_id(0) == 0)
        def _():
            _load_projs((w1_ref, w2_ref, w3_ref), wl_ref, wa_ref, wo_ref, sem)

        p_l = _dot(yl_ref[...], wl_ref[...])
        p_a = _dot(ya_ref[...], wa_ref[...])
        pl_ref[...] = p_l.astype(BF16)
        pa_ref[...] = p_a.astype(BF16)
        merged = (_sigmoid(gl_ref[...]) * p_l + _sigmoid(ga_ref[...]) * p_a).astype(BF16)
        mg_ref[...] = merged
        m = _dot(merged, wo_ref[...])
        m_ref[...] = m
        o_ref[...] = h_ref[...] + m * _rsqrt_mean_sq(m) * gp_ref[...]

    row = _ROW(tm)
    return _call(
        body, name=name, grid=(SEQ // tm,),
        in_specs=[row, row, row, row, ANY, ANY, ANY, _VEC, row],
        out_specs=[row] * 5,
        out_shape=[jax.ShapeDtypeStruct((SEQ, D_MODEL), BF16)] * 3 + [jax.ShapeDtypeStruct((SEQ, D_MODEL), F32)] * 2,
        scratch_shapes=_proj_scratch(),
        args=[y_lru, y_attn, g_lru, g_attn, *projs, g_post, h_in], stages=stages)


def merge_bwd(d_out, m, g_post, projs, g_lru, g_attn, p_l, p_a, name, stages=()):
    tm = 256

    def body(do_ref, m_ref, gp_ref, w1_ref, w2_ref, w3_ref, gl_ref, ga_ref, pl_ref, pa_ref,
             dm_ref, dpl_ref, dpa_ref, dgl_ref, dga_ref, dya_ref, dyl_ref, dgp_ref, wl_ref, wa_ref, wo_ref, sem):
        @pl.when(pl.program_id(0) == 0)
        def _():
            _load_projs((w1_ref, w2_ref, w3_ref), wl_ref, wa_ref, wo_ref, sem)
            dgp_ref[...] = jnp.zeros_like(dgp_ref)

        mv = m_ref[...]
        rm = _rsqrt_mean_sq(mv)
        mh = mv * rm
        dn = do_ref[...]
        dgp_ref[...] += jnp.sum(dn * mh, axis=0, keepdims=True)
        t = dn * gp_ref[...]
        dm = (rm * (t - mh * jnp.mean(t * mh, axis=-1, keepdims=True))).astype(BF16)
        dm_ref[...] = dm
        dmg = _dot_nt(dm, wo_ref[...])
        sl = _sigmoid(gl_ref[...])
        sa = _sigmoid(ga_ref[...])
        dpl = (dmg * sl).astype(BF16)
        dpa = (dmg * sa).astype(BF16)
        dpl_ref[...] = dpl
        dpa_ref[...] = dpa
        dgl_ref[...] = (dmg * pl_ref[...].astype(F32) * sl * (1.0 - sl)).astype(BF16)
        dga_ref[...] = (dmg * pa_ref[...].astype(F32) * sa * (1.0 - sa)).astype(BF16)
        dyl_ref[...] = _dot_nt(dpl, wl_ref[...])
        dya_ref[...] = _dot_nt(dpa, wa_ref[...]).astype(BF16)

    row = _ROW(tm)
    return _call(
        body, name=name, grid=(SEQ // tm,),
        in_specs=[row, row, _VEC, ANY, ANY, ANY, row, row, row, row],
        out_specs=[row] * 7 + [_VEC],
        out_shape=[jax.ShapeDtypeStruct((SEQ, D_MODEL), BF16)] * 6 + [jax.ShapeDtypeStruct((SEQ, D_MODEL), F32),
                                                                       jax.ShapeDtypeStruct((1, D_MODEL), F32)],
        scratch_shapes=_proj_scratch(),
        args=[d_out, m, g_post, *projs, g_lru, g_attn, p_l, p_a], stages=stages)


def _rope_tables():
    half = HEAD_DIM // 2
    inv_freq = np.float32(ROPE_THETA) ** (-np.arange(half, dtype=np.float32) / np.float32(half))
    ang = np.arange(SEQ, dtype=np.float32)[:, None] * inv_freq[None, :]
    cos, sin = np.cos(ang), np.sin(ang)
    return (jnp.asarray(np.tile(np.concatenate([cos, cos], axis=1), (1, 2))),
            jnp.asarray(np.tile(np.concatenate([-sin, sin], axis=1), (1, 2))))


def _block_diag(w):
    per = LRU_TC // LRU_BLOCK_W
    w4 = w.reshape(LRU_W // LRU_TC, per, LRU_BLOCK_W, LRU_BLOCK_W)
    eye = jnp.eye(per, dtype=w.dtype)
    return jnp.einsum('jacd,ab->jacbd', w4, eye).reshape(LRU_W // LRU_TC, LRU_TC, LRU_TC).astype(BF16)


def _diag_blocks(p):
    per = LRU_TC // LRU_BLOCK_W
    p5 = p.reshape(LRU_W // LRU_TC, per, LRU_BLOCK_W, per, LRU_BLOCK_W)
    return jnp.stack([p5[:, a, :, a, :] for a in range(per)], axis=1).reshape(LRU_W // LRU_BLOCK_W, LRU_BLOCK_W, LRU_BLOCK_W)


def _place():
    x, y, c = lax.axis_index('x'), lax.axis_index('y'), lax.axis_index('c')
    chips = [(1 - x, y), (x, 1 - y), (1 - x, 1 - y)]
    return x, y, c, chips


def _rcopy(src, dst, send_sem, recv_sem, to):
    return pltpu.make_async_remote_copy(src_ref=src, dst_ref=dst, send_sem=send_sem, recv_sem=recv_sem,
                                        device_id=to, device_id_type=MESH)


class _Stage:
    inputs, out_shape, scratch, peers = (), (), (), ()

    def start(self, ins, outs, scr):
        plan = self._plan(ins, outs, scr)
        for ld in plan['loads']:
            ld.start()
        for cp in plan['sends']:
            cp.start()

    def relay(self, ins, outs, scr):
        pass

    def mid(self, ins, outs, scr):
        plan = self._plan(ins, outs, scr)
        for ld, st in zip(plan['loads'], plan['stores']):
            ld.wait()
            st.start()
        for arrived, onward in zip(plan['arrivals'], plan['forwards']):
            arrived.wait_recv()
            onward.start()

    def end(self, ins, outs, scr):
        plan = self._plan(ins, outs, scr)
        for st in plan['stores']:
            st.wait()
        for arrived in (plan['final_arrivals'] if plan['forwards'] else plan['arrivals']):
            arrived.wait_recv()
        for cp in plan['sends'] + plan['forwards']:
            cp.wait_send()


def _empty_plan():
    return dict(loads=[], stores=[], sends=[], arrivals=[], forwards=[], final_arrivals=[])


class GatherStage(_Stage):
    peers = ('chips', 'sib')
    N_CP = 12

    def __init__(self, items):
        self.ranges = [(off, rows) for _, off, rows in items]
        self.inputs = [src for src, _, _ in items]
        self.out_shape = [jax.ShapeDtypeStruct((N_CHIPS, rows, D_MODEL), BF16) for _, rows in self.ranges]
        n = self.N_CP * len(items)
        self.scratch = [pltpu.VMEM((sum(r for _, r in self.ranges), D_MODEL), BF16), pltpu.SemaphoreType.DMA((n,)),
                        pltpu.SemaphoreType.DMA((n,)), pltpu.SemaphoreType.DMA((2 * len(items),))]

    def _plan(self, ins, outs, scr):
        buf, send, recv, lsem = scr
        x, y, c, _ = _place()
        me_q, q_x, q_y, q_d = 2 * x + y, 2 * (1 - x) + y, 2 * x + (1 - y), 2 * (1 - x) + (1 - y)
        to_x, to_y, sib = (1 - x, y, c), (x, 1 - y, c), (x, y, 1 - c)
        plan = dict(loads=[], stores=[], first=[], early=[], relays=[], late=[], hand_early=[], hand_late=[], final=[])
        boff = 0
        for w, ((off, rows), p_ref, o_ref) in enumerate(zip(self.ranges, ins, outs)):
            hr = rows // 2
            ch = hr // 2
            plan['loads'].append(pltpu.make_async_copy(p_ref.at[pl.ds(off, rows)], buf.at[pl.ds(boff, rows)], lsem.at[2 * w]))
            plan['stores'].append(pltpu.make_async_copy(buf.at[pl.ds(boff, rows)], o_ref.at[me_q], lsem.at[2 * w + 1]))
            boff += rows
            base = w * self.N_CP
            mine = [pl.ds(pl.multiple_of(c * hr + k * ch, 16), ch) for k in range(2)]
            theirs = [pl.ds(pl.multiple_of((1 - c) * hr + k * ch, 16), ch) for k in range(2)]
            src = [p_ref.at[pl.ds(pl.multiple_of(off + c * hr + k * ch, 16), ch)] for k in range(2)]

            def cp(k, s, d, to):
                return _rcopy(s, d, send.at[base + k], recv.at[base + k], to)

            def here(q, rows_):
                return o_ref.at[q, rows_]

            plan['first'] += [cp(0, src[0], here(me_q, mine[0]), to_x), cp(2, src[1], here(me_q, mine[1]), to_y),
                              cp(1, src[1], here(me_q, mine[1]), to_x), cp(3, src[0], here(me_q, mine[0]), to_y)]
            x_a, y_b = here(q_x, mine[0]), here(q_y, mine[1])
            plan['early'] += [cp(0, x_a, x_a, to_x), cp(2, y_b, y_b, to_y)]
            plan['relays'] += [cp(4, x_a, x_a, to_y), cp(5, y_b, y_b, to_x)]
            plan['hand_early'] += [cp(6, x_a, x_a, sib), cp(7, y_b, y_b, sib)]
            x_b, y_a, d_a, d_b = here(q_x, mine[1]), here(q_y, mine[0]), here(q_d, mine[0]), here(q_d, mine[1])
            plan['late'] += [cp(1, x_b, x_b, to_x), cp(3, y_a, y_a, to_y), cp(4, d_a, d_a, to_y), cp(5, d_b, d_b, to_x)]
            plan['hand_late'] += [cp(8, x_b, x_b, sib), cp(9, y_a, y_a, sib), cp(10, d_a, d_a, sib), cp(11, d_b, d_b, sib)]
            for k, (q, piece) in enumerate([(q_x, 0), (q_y, 1), (q_x, 1), (q_y, 0), (q_d, 0), (q_d, 1)]):
                got = here(q, theirs[piece])
                plan['final'].append(cp(6 + k, got, got, sib))
        return plan

    def start(self, ins, outs, scr):
        plan = self._plan(ins, outs, scr)
        for ld in plan['loads']:
            ld.start()
        for cp in plan['first']:
            cp.start()

    def relay(self, ins, outs, scr):
        plan = self._plan(ins, outs, scr)
        for arrived in plan['early']:
            arrived.wait_recv()
        for cp in plan['relays'] + plan['hand_early']:
            cp.start()

    def mid(self, ins, outs, scr):
        plan = self._plan(ins, outs, scr)
        for ld, st in zip(plan['loads'], plan['stores']):
            ld.wait()
            st.start()
        for arrived in plan['late']:
            arrived.wait_recv()
        for cp in plan['hand_late']:
            cp.start()

    def end(self, ins, outs, scr):
        plan = self._plan(ins, outs, scr)
        for st in plan['stores']:
            st.wait()
        for arrived in plan['final']:
            arrived.wait_recv()
        for cp in plan['first'] + plan['relays'] + plan['hand_early'] + plan['hand_late']:
            cp.wait_send()


class PairStage(_Stage):
    peers = ('sib',)

    def __init__(self, grads):
        self.inputs = list(grads)
        self.out_shape = [jax.ShapeDtypeStruct((N_CHIPS, 1) + g.shape[2:], BF16) for g in grads]
        n_cp = N_CHIPS * len(grads)
        self.scratch = [pltpu.SemaphoreType.DMA((n_cp,)), pltpu.SemaphoreType.DMA((n_cp,))]

    def _plan(self, ins, outs, scr):
        send, recv = scr
        x, y, c, _ = _place()
        plan = _empty_plan()
        for w, (g_ref, l_ref) in enumerate(zip(ins, outs)):
            for q in range(N_CHIPS):
                i = w * N_CHIPS + q
                plan['sends'].append(_rcopy(g_ref.at[q, pl.ds(1 - c, 1)], l_ref.at[q], send.at[i], recv.at[i], (x, y, 1 - c)))
        plan['arrivals'] = plan['sends']
        return plan


class ChipStage(_Stage):
    peers = ('chips',)

    def __init__(self, items):
        self.ranges = [(off, n) for _, off, n in items]
        self.inputs = [s for s, _, _ in items]
        self.out_shape = [jax.ShapeDtypeStruct((N_CHIPS, n, D_MODEL), BF16) for _, n in self.ranges]
        n_cp = 3 * len(items)
        self.scratch = [pltpu.VMEM((sum(n for _, n in self.ranges), D_MODEL), BF16), pltpu.SemaphoreType.DMA((n_cp,)),
                        pltpu.SemaphoreType.DMA((n_cp,)), pltpu.SemaphoreType.DMA((2 * len(items),))]

    def _plan(self, ins, outs, scr):
        buf, send, recv, lsem = scr
        x, y, c, chips = _place()
        me_q = 2 * x + y
        plan = _empty_plan()
        boff = 0
        for w, ((off, n), s_ref, l_ref) in enumerate(zip(self.ranges, ins, outs)):
            rows = pl.ds(off, n)
            plan['loads'].append(pltpu.make_async_copy(s_ref.at[me_q, rows], buf.at[pl.ds(boff, n)], lsem.at[2 * w]))
            plan['stores'].append(pltpu.make_async_copy(buf.at[pl.ds(boff, n)], l_ref.at[me_q], lsem.at[2 * w + 1]))
            boff += n
            for j, (cx, cy) in enumerate(chips):
                i = w * 3 + j
                got = l_ref.at[2 * cx + cy]
                plan['sends'].append(_rcopy(s_ref.at[2 * cx + cy, rows], l_ref.at[me_q], send.at[i], recv.at[i], (cx, cy, c)))
                plan['arrivals'].append(_rcopy(got, got, send.at[i], recv.at[i], (cx, cy, c)))
        return plan


class SwapStage(_Stage):
    peers = ('sib',)

    def __init__(self, items):
        n = len(items)
        self.inputs = list(items)
        self.out_shape = [jax.ShapeDtypeStruct((2,) + a.shape, a.dtype) for a in items]
        self.scratch = [pltpu.VMEM(a.shape, a.dtype) for a in items] + [
            pltpu.SemaphoreType.DMA((n,)), pltpu.SemaphoreType.DMA((n,)), pltpu.SemaphoreType.DMA((2 * n,))]

    def _plan(self, ins, outs, scr):
        bufs, (send, recv, lsem) = scr[:len(ins)], scr[len(ins):]
        x, y, c, _ = _place()
        plan = _empty_plan()
        for w, (h_ref, o_ref, buf) in enumerate(zip(ins, outs, bufs)):
            plan['loads'].append(pltpu.make_async_copy(h_ref, buf, lsem.at[2 * w]))
            plan['stores'].append(pltpu.make_async_copy(buf, o_ref.at[c], lsem.at[2 * w + 1]))
            got = o_ref.at[1 - c]
            plan['sends'].append(_rcopy(h_ref, o_ref.at[c], send.at[w], recv.at[w], (x, y, 1 - c)))
            plan['arrivals'].append(_rcopy(got, got, send.at[w], recv.at[w], (x, y, 1 - c)))
        return plan


class SmallGatherStage(_Stage):
    peers = ('chips', 'sib')

    def __init__(self, blk):
        self.inputs = [blk]
        self.out_shape = [jax.ShapeDtypeStruct((N_DEV,) + blk.shape, blk.dtype)]
        self.scratch = [pltpu.VMEM(blk.shape, blk.dtype), pltpu.SemaphoreType.DMA((7,)), pltpu.SemaphoreType.DMA((7,)),
                        pltpu.SemaphoreType.DMA((2,))]

    def _plan(self, ins, outs, scr):
        (x_ref,), (o_ref,), (buf, send, recv, lsem) = ins, outs, scr
        x, y, c, chips = _place()
        sib = (x, y, 1 - c)

        def slot(px, py, pc):
            return o_ref.at[4 * px + 2 * py + pc]

        plan = _empty_plan()
        plan['loads'].append(pltpu.make_async_copy(x_ref, buf, lsem.at[0]))
        plan['stores'].append(pltpu.make_async_copy(buf, slot(x, y, c), lsem.at[1]))
        from_sib = slot(x, y, 1 - c)
        plan['sends'].append(_rcopy(x_ref, slot(x, y, c), send.at[0], recv.at[0], sib))
        plan['final_arrivals'].append(_rcopy(from_sib, from_sib, send.at[0], recv.at[0], sib))
        for j, (cx, cy) in enumerate(chips):
            got, got_sib = slot(cx, cy, c), slot(cx, cy, 1 - c)
            plan['sends'].append(_rcopy(x_ref, slot(x, y, c), send.at[1 + j], recv.at[1 + j], (cx, cy, c)))
            plan['arrivals'].append(_rcopy(got, got, send.at[1 + j], recv.at[1 + j], (cx, cy, c)))
            plan['forwards'].append(_rcopy(got, got, send.at[4 + j], recv.at[4 + j], sib))
            plan['final_arrivals'].append(_rcopy(got_sib, got_sib, send.at[4 + j], recv.at[4 + j], sib))
        return plan


_HBM = pl.BlockSpec(memory_space=pltpu.HBM)
_SEM = pl.BlockSpec(memory_space=pltpu.SEMAPHORE)
_DATAFLOW = pltpu.CompilerParams(has_side_effects=pltpu.SideEffectType.DATAFLOW_SIDE_EFFECTING)


def chip_exchange_start(s):
    def body(s_ref, land_ref, send, recv, s_thru, land_thru, token):
        x, y, c, chips = _place()
        for j, (cx, cy) in enumerate(chips):
            _rcopy(s_ref.at[2 * cx + cy], land_ref.at[2 * x + y], send.at[j], recv.at[j], (cx, cy, c)).start()
        token[...] = jnp.zeros_like(token)

    return pl.pallas_call(
        body, name='chip_exchange_start',
        out_shape=(pltpu.SemaphoreType.DMA((3,)), pltpu.SemaphoreType.DMA((3,)), pltpu.HBM(s.shape, s.dtype),
                   pltpu.HBM(s.shape, s.dtype), jax.ShapeDtypeStruct((8, 128), F32)),
        in_specs=(_HBM, _HBM), out_specs=(_SEM, _SEM, _HBM, _HBM, pl.BlockSpec(memory_space=pltpu.VMEM)),
        input_output_aliases={0: 2, 1: 3}, compiler_params=_DATAFLOW,
    )(pltpu.with_memory_space_constraint(s, pltpu.HBM),
      pltpu.with_memory_space_constraint(lax.empty(s.shape, s.dtype), pltpu.HBM))


def chip_exchange_wait(send, recv, s_thru, land_thru, after):
    def body(s_ref, land_ref, send_sem, recv_sem, after_ref, s_out, land_out):
        x, y, c, chips = _place()
        for j, (cx, cy) in enumerate(chips):
            cp = _rcopy(s_ref.at[2 * cx + cy], land_ref.at[2 * cx + cy], send_sem.at[j], recv_sem.at[j], (cx, cy, c))
            cp.wait_send()
            cp.wait_recv()

    return pl.pallas_call(
        body, name='chip_exchange_wait',
        out_shape=(pltpu.HBM(s_thru.shape, s_thru.dtype), pltpu.HBM(land_thru.shape, land_thru.dtype)),
        in_specs=(_HBM, _HBM, _SEM, _SEM, ANY), out_specs=(_HBM, _HBM),
        input_output_aliases={0: 0, 1: 1}, compiler_params=_DATAFLOW,
    )(s_thru, land_thru, send, recv, after)


def comm_call(name, stages):
    def body():
        pass

    return _call(body, name=name, grid=(1,), in_specs=[], out_specs=[], out_shape=[], args=[], stages=stages)[1]


def pair_sum(g4, land, c_arr, name):
    hr = g4.shape[2]

    def body(c_ref, g_ref, l_ref, o_ref):
        o_ref[0] = (g_ref[0, 0].astype(F32) + l_ref[0, 0].astype(F32)).astype(BF16)

    return pl.pallas_call(
        body, name=name,
        grid_spec=pltpu.PrefetchScalarGridSpec(
            num_scalar_prefetch=1, grid=(N_CHIPS,),
            in_specs=[pl.BlockSpec((1, 1, hr, D_MODEL), lambda q, c: (q, c[0], 0, 0)),
                      pl.BlockSpec((1, 1, hr, D_MODEL), lambda q, c: (q, 0, 0, 0))],
            out_specs=pl.BlockSpec((1, hr, D_MODEL), lambda q, c: (q, 0, 0))),
        out_shape=jax.ShapeDtypeStruct((N_CHIPS, hr, D_MODEL), BF16),
        compiler_params=_params(1),
    )(c_arr, g4, land)


def small_sum(vec_parts, lru_parts):
    def body(v_ref, l_ref, o_ref):
        for p_ref, lo, n in ((v_ref, 0, ROW_WA), (l_ref, ROW_WA, SMALL_ROWS - ROW_WA)):
            acc = p_ref[0]
            for s in range(1, N_DEV):
                acc = acc + p_ref[s]
            o_ref[lo:lo + n, :] = acc

    return pl.pallas_call(
        body, name='small_sum', grid=(1,),
        in_specs=[pl.BlockSpec(vec_parts.shape, lambda i: (0, 0, 0)), pl.BlockSpec(lru_parts.shape, lambda i: (0, 0, 0))],
        out_specs=pl.BlockSpec((SMALL_ROWS, D_MODEL), lambda i: (0, 0)),
        out_shape=jax.ShapeDtypeStruct((SMALL_ROWS, D_MODEL), F32),
        compiler_params=_params(1),
    )(vec_parts, lru_parts)


def _adam_math(w, g, m, v):
    m2 = ADAM_B1 * m + (1.0 - ADAM_B1) * g
    v2 = ADAM_B2 * v + (1.0 - ADAM_B2) * (g * g)
    m_hat = m2 / (1.0 - ADAM_B1 ** ADAM_STEP)
    v_hat = v2 / (1.0 - ADAM_B2 ** ADAM_STEP)
    delta = -ADAM_LR * (m_hat / (jnp.sqrt(v_hat) + ADAM_EPS) + ADAM_WD * w)
    return delta, m2, v2


def _adam_body(n_parts, transposed, n_after):
    def body(*refs):
        refs = refs[n_after:]
        g_refs = refs[:n_parts]
        w_ref, m_ref, v_ref, go_ref, d_ref, mo_ref, vo_ref = refs[n_parts:]
        def chips_added(blk):
            acc = blk[0].astype(F32)
            for s in range(1, N_CHIPS):
                acc = acc + blk[s].astype(F32)
            return acc

        if transposed:
            g = jnp.concatenate([chips_added(g_ref[h]) for h in range(2) for g_ref in g_refs], axis=0).T
        else:
            rows = [chips_added(g_ref[0]) for g_ref in g_refs]
            g = jnp.concatenate(rows, axis=0) if n_parts > 1 else rows[0]
        go_ref[...] = g
        d_ref[...], mo_ref[...], vo_ref[...] = _adam_math(w_ref[...], g, m_ref[...], v_ref[...])
    return body


def adam_rows(fulls, name, w, m, v, after=()):
    hr = w.shape[0] // 2
    blk = pl.BlockSpec((hr, D_MODEL), lambda h: (h, 0))
    return pl.pallas_call(
        _adam_body(len(fulls), False, len(after)), name='adam_' + name, grid=(2,),
        in_specs=[ANY] * len(after)
        + [pl.BlockSpec((1, N_CHIPS, f.shape[2], D_MODEL), lambda h: (h, 0, 0, 0)) for f in fulls] + [blk, blk, blk],
        out_specs=[blk] * 4,
        out_shape=[jax.ShapeDtypeStruct(w.shape, F32)] * 4,
        compiler_params=_params(1),
    )(*after, *fulls, w, m, v)


def adam_cols(fulls, name, w, m, v, after=()):
    cols = w.shape[1]
    tr = 128
    blk = pl.BlockSpec((tr, cols), lambda i: (i, 0))
    return pl.pallas_call(
        _adam_body(len(fulls), True, len(after)), name='adam_' + name, grid=(D_MODEL // tr,),
        in_specs=[ANY] * len(after)
        + [pl.BlockSpec((2, N_CHIPS, f.shape[2], tr), lambda i: (0, 0, 0, i)) for f in fulls] + [blk, blk, blk],
        out_specs=[blk] * 4,
        out_shape=[jax.ShapeDtypeStruct(w.shape, F32)] * 4,
        compiler_params=_params(1),
    )(*after, *fulls, w, m, v)


def adam_small(g, w, m, v):
    def body(g_ref, w_ref, m_ref, v_ref, d_ref, mo_ref, vo_ref):
        d_ref[...], mo_ref[...], vo_ref[...] = _adam_math(w_ref[...], g_ref[...], m_ref[...], v_ref[...])

    blk = pl.BlockSpec(w.shape, lambda i: (0, 0))
    return pl.pallas_call(
        body, name='adam_small', grid=(1,), in_specs=[blk] * 4, out_specs=[blk] * 3,
        out_shape=[jax.ShapeDtypeStruct(w.shape, F32)] * 3, compiler_params=_params(1),
    )(g, w, m, v)


WEIGHTS = ('ffn1_pre_g', 'ffn1_w_gu', 'ffn1_w_down', 'ffn1_post_g', 'mix_pre_g', 'w_in', 'conv_w', 'conv_b',
           'lru_w_a', 'lru_b_a', 'lru_w_x', 'lru_b_x', 'lru_lambda', 'attn_sinks', 'w_proj_lru', 'w_proj_attn',
           'w_out', 'mix_post_g', 'ffn2_pre_g', 'ffn2_w_gu', 'ffn2_w_down', 'ffn2_post_g')
SMALL = tuple(n for n in WEIGHTS if n not in PACK_OFF)


def cast_t(w, name):
    cols = w.shape[1]
    tc = 128

    def body(w_ref, o_ref):
        o_ref[...] = w_ref[...].T.astype(BF16)

    return pl.pallas_call(
        body, name=name, grid=(cols // tc,),
        in_specs=[pl.BlockSpec((D_MODEL, tc), lambda j: (0, j))],
        out_specs=pl.BlockSpec((tc, D_MODEL), lambda j: (j, 0)),
        out_shape=jax.ShapeDtypeStruct((cols, D_MODEL), BF16),
        compiler_params=_params(1),
    )(w)


def _pack_vecs(d, conv_rows):
    sinks = jnp.pad(d['attn_sinks'].reshape(1, N_Q_HEADS), ((0, 0), (0, D_MODEL - N_Q_HEADS)))
    conv = jnp.pad(conv_rows, ((0, ROW_WA - ROW_CONV - conv_rows.shape[0]), (0, 0)))
    return jnp.concatenate([d[n].reshape(1, D_MODEL) for n in SMALL_VECS] + [sinks, conv], axis=0)


def _pack_lru(d):
    return jnp.concatenate([d['lru_w_a'].reshape(64, D_MODEL), d['lru_w_x'].reshape(64, D_MODEL)], axis=0)


def _pack_small(d, conv_rows):
    return jnp.concatenate([_pack_vecs(d, conv_rows), _pack_lru(d)], axis=0)


def _unpack_small(p, shapes):
    out = {n: p[k:k + 1].reshape(shapes[n]) for k, n in enumerate(SMALL_VECS)}
    out['attn_sinks'] = p[ROW_SINKS:ROW_SINKS + 1, :N_Q_HEADS].reshape(shapes['attn_sinks'])
    out['conv_w'] = p[ROW_CONV:ROW_CONV + 1].reshape(shapes['conv_w'])
    out['lru_w_a'] = p[ROW_WA:ROW_WA + 64].reshape(shapes['lru_w_a'])
    out['lru_w_x'] = p[ROW_WX:ROW_WX + 64].reshape(shapes['lru_w_x'])
    return out


def kernel(x, ffn1_pre_g, ffn1_w_gu, ffn1_w_down, ffn1_post_g, mix_pre_g, w_in, conv_w, conv_b, lru_w_a, lru_b_a, lru_w_x, lru_b_x, lru_lambda, attn_sinks, w_proj_lru, w_proj_attn, w_out, mix_post_g, ffn2_pre_g, ffn2_w_gu, ffn2_w_down, ffn2_post_g, loss_target, m_ffn1_pre_g, m_ffn1_w_gu, m_ffn1_w_down, m_ffn1_post_g, m_mix_pre_g, m_w_in, m_conv_w, m_conv_b, m_lru_w_a, m_lru_b_a, m_lru_w_x, m_lru_b_x, m_lru_lambda, m_attn_sinks, m_w_proj_lru, m_w_proj_attn, m_w_out, m_mix_post_g, m_ffn2_pre_g, m_ffn2_w_gu, m_ffn2_w_down, m_ffn2_post_g, v_ffn1_pre_g, v_ffn1_w_gu, v_ffn1_w_down, v_ffn1_post_g, v_mix_pre_g, v_w_in, v_conv_w, v_conv_b, v_lru_w_a, v_lru_b_a, v_lru_w_x, v_lru_b_x, v_lru_lambda, v_attn_sinks, v_w_proj_lru, v_w_proj_attn, v_w_out, v_mix_post_g, v_ffn2_pre_g, v_ffn2_w_gu, v_ffn2_w_down, v_ffn2_post_g):
    given = dict(locals())
    w = {n: given[n] for n in WEIGHTS}
    mom = {n: given['m_' + n] for n in WEIGHTS}
    var = {n: given['v_' + n] for n in WEIGHTS}
    shapes = {n: w[n].shape for n in WEIGHTS}
    xq = lax.axis_index('x')
    yq = lax.axis_index('y')
    cq = lax.axis_index('c')
    me_q = 2 * xq + yq

    c_arr = cq.reshape(1).astype(jnp.int32)
    xs, target = x[0], loss_target[0]
    sw = {n: (w[n][0] if w[n].ndim > 2 else w[n]) for n in SMALL}
    cos, sin_signed = _rope_tables()
    wa_bd = _block_diag(sw['lru_w_a'])
    wx_bd = _block_diag(sw['lru_w_x'])
    sinks = sw['attn_sinks'].reshape(N_Q_HEADS)

    shard = {n: (cast_t(w[n][0], 'cast_' + n) if t else w[n][0].astype(BF16)) for n, _, t in PACK}
    conv_pad = jnp.pad(w['conv_w'][0], ((0, 4), (0, 0)))

    def whole(name):
        return (shard[name], 0, PACK_ROWS_OF[name])

    def part(name, p, n_parts=2):
        rows = PACK_ROWS_OF[name] // n_parts
        return (shard[name], p * rows, rows)

    def split_at(name, cut):
        return (shard[name], 0, cut), (shard[name], cut, PACK_ROWS_OF[name] - cut)

    (w_gu1,), (conv_all,) = comm_call('gather_first', [GatherStage([whole('ffn1_w_gu')]), SmallGatherStage(conv_pad)])
    sw['conv_w'] = jnp.transpose(conv_all[0::2, :4, :], (1, 0, 2)).reshape(4, LRU_W)
    proj_names = ['w_proj_lru', 'w_proj_attn', 'w_out']

    w_in_first, w_in_rest = split_at('w_in', 384)
    (n1, g1, u1, a1), ((w_down1, w_in_a),) = ffn_fwd_a(xs, sw['ffn1_pre_g'], [w_gu1], 'ffn1_fwd_a',
                                                        stages=[GatherStage([whole('ffn1_w_down'), w_in_first])])
    w_down1 = [w_down1]
    (f1, h1), ((w_in_b,),) = ffn_fwd_b(a1, w_down1, sw['ffn1_post_g'], xs, 'ffn1_fwd_b', stages=[GatherStage([w_in_rest])])
    w_in_t = [w_in_a, w_in_b]
    (um, gate, xbr, q, k, v, g_lru, g_attn), ((w_gu2a,),) = mix_in(h1, sw['mix_pre_g'], w_in_t, 'mix_in',
                                                                   stages=[GatherStage([part('ffn2_w_gu', 0)])])
    (y_lru, h_lru), ((w_gu2b,),) = lru_fwd(gate, xbr, sw['conv_w'], sw['conv_b'], wa_bd, sw['lru_b_a'], wx_bd, sw['lru_b_x'],
                                           sw['lru_lambda'], 'lru_fwd', stages=[GatherStage([part('ffn2_w_gu', 1)])])
    w_down2_first, w_down2_rest = split_at('ffn2_w_down', 320)
    (qr, kr, y_attn), ((*projs, w_down2a),) = attn_fwd(q, k, v, cos, sin_signed, sinks, 'attn_fwd',
                                                       stages=[GatherStage([whole(n) for n in proj_names] + [w_down2_first])])
    (p_l, p_a, merged, m, h2), ((w_down2b,),) = merge_fwd(y_lru, y_attn, g_lru, g_attn, projs, sw['mix_post_g'], h1, 'merge_fwd',
                                                          stages=[GatherStage([w_down2_rest])])
    w_down2 = [w_down2a, w_down2b]
    w_gu2 = [w_gu2a, w_gu2b]
    (n2, g2, u2, a2), _ = ffn_fwd_a(h2, sw['ffn2_pre_g'], w_gu2, 'ffn2_fwd_a')
    (f2, dy, loss_blk), _ = ffn_fwd_b(a2, w_down2, sw['ffn2_post_g'], h2, 'ffn2_fwd_b', target=target)

    gs, full = {}, {}

    def pair_stage(names, grads):
        g4 = [g.reshape(N_CHIPS, 2, PACK_ROWS_OF[n] // 2, D_MODEL) for n, g in zip(names, grads)]
        return PairStage(g4), g4

    def pair_sums(names, g4, lands):
        return [pair_sum(g, l, c_arr, 'pair_sum_' + n) for n, g, l in zip(names, g4, lands)]

    def halves(s, n_parts=2):
        n = s.shape[1] // n_parts
        return [(s, p * n, n) for p in range(n_parts)]

    (df2, dgu2, gs['ffn2_post_g']), _ = ffn_bwd_a(dy, f2, sw['ffn2_post_g'], w_down2, g2, u2, 'ffn2_bwd_a')
    g_down2, _ = mm_tn([a2], df2, 1408, 'ffn2_dw_down')
    st, g4 = pair_stage(['ffn2_w_down'], [g_down2])
    g_gu2, (lands,) = mm_tn([dgu2], n2, 1408, 'ffn2_dw_gu', stages=[st])
    (s_down2,) = pair_sums(['ffn2_w_down'], g4, lands)
    st, g4 = pair_stage(['ffn2_w_gu'], [g_gu2])
    (dh2, gs['ffn2_pre_g']), ((l_down2,), lands) = norm_bwd([dgu2], w_gu2, h2, sw['ffn2_pre_g'], dy, 'ffn2_bwd_b',
                                                            stages=[ChipStage([(s_down2, 0, s_down2.shape[1])]), st])
    (s_gu2,) = pair_sums(['ffn2_w_gu'], g4, lands)

    (dm, dpl, dpa, dgl, dga, dya, dyl, gs['mix_post_g']), ((l_gu2a,),) = merge_bwd(
        dh2, m, sw['mix_post_g'], projs, g_lru, g_attn, p_l, p_a, 'merge_bwd', stages=[ChipStage(halves(s_gu2)[:1])])
    g_projs = [mm_tn([merged if n == 'w_out' else (y_lru if n == 'w_proj_lru' else y_attn)],
                     dm if n == 'w_out' else (dpl if n == 'w_proj_lru' else dpa), D_MODEL, 'd' + n)[0] for n in proj_names]
    st, g4 = pair_stage(proj_names, g_projs)
    (dq, dkv, dsk), ((l_gu2b,), lands, (full['ffn2_w_down'],)) = attn_bwd(
        qr, kr, v, dya, cos, sin_signed, sinks, 'attn_bwd', stages=[ChipStage(halves(s_gu2)[1:]), st, SwapStage([l_down2])])
    full['ffn2_w_down'] = [full['ffn2_w_down']]
    gs['attn_sinks'] = dsk[0:1, 0:N_Q_HEADS]
    s_projs = pair_sums(proj_names, g4, lands)
    (dgate, dxbr, vecs, dwa, dwx), (l_projs, full['ffn2_w_gu']) = lru_bwd(
        gate, xbr, h_lru, dyl, sw['conv_w'], sw['conv_b'], wa_bd, sw['lru_b_a'], wx_bd, sw['lru_b_x'], sw['lru_lambda'],
        'lru_bwd', stages=[ChipStage([(s, 0, s.shape[1]) for s in s_projs]), SwapStage([l_gu2a, l_gu2b])])
    gs['conv_w'] = vecs[0:4]
    gs['conv_b'], gs['lru_b_a'], gs['lru_b_x'], gs['lru_lambda'] = vecs[4:5], vecs[5:6], vecs[6:7], vecs[7:8]
    gs['lru_w_a'] = _diag_blocks(dwa)
    gs['lru_w_x'] = _diag_blocks(dwx)
    dz = [dgate, dxbr, dq, dkv, dgl, dga]
    g_in, ((lru_all,),) = mm_tn(dz, um, 512, 'dw_in', stages=[SmallGatherStage(_pack_lru(gs))])
    st, g4 = pair_stage(['w_in'], [g_in])
    (dh1, gs['mix_pre_g']), (lands, f_projs) = norm_bwd(dz, w_in_t, h1, sw['mix_pre_g'], dh2, 'mix_bwd_in',
                                                        stages=[st, SwapStage(l_projs)])
    for n, f in zip(proj_names, f_projs):
        full[n] = [f]
    (s_in,) = pair_sums(['w_in'], g4, lands)

    (df1, dgu1, gs['ffn1_post_g']), ((l_in_a,),) = ffn_bwd_a(dh1, f1, sw['ffn1_post_g'], w_down1, g1, u1, 'ffn1_bwd_a',
                                                             stages=[ChipStage(halves(s_in)[:1])])
    g_down1, _ = mm_tn([a1], df1, 1408, 'ffn1_dw_down')
    st, g4 = pair_stage(['ffn1_w_down'], [g_down1])
    g_gu1, ((l_in_b,), lands) = mm_tn([dgu1], n1, 1408, 'ffn1_dw_gu', stages=[ChipStage(halves(s_in)[1:]), st])
    (s_down1,) = pair_sums(['ffn1_w_down'], g4, lands)
    st, g4 = pair_stage(['ffn1_w_gu'], [g_gu1])
    (dx, gs['ffn1_pre_g']), ((l_down1,), lands, full['w_in']) = norm_bwd(
        [dgu1], [w_gu1], xs, sw['ffn1_pre_g'], dh1, 'ffn1_bwd_b',
        stages=[ChipStage([(s_down1, 0, s_down1.shape[1])]), st, SwapStage([l_in_a, l_in_b])])
    (s_gu1,) = pair_sums(['ffn1_w_gu'], g4, lands)
    loss_row = jnp.pad(loss_blk[0:1], ((0, 0), (0, D_MODEL - loss_blk.shape[1])))
    vec_blk = _pack_vecs(gs, jnp.concatenate([gs['conv_w'], loss_row], axis=0))
    send, recv, s_thru, land_thru, token = chip_exchange_start(s_gu1)
    out_g, out_d, out_m, out_v = {}, {}, {}, {}

    def adam(n, after=()):
        fn = adam_cols if dict((k, t) for k, _, t in PACK)[n] else adam_rows
        g_, d_, m_, v_ = fn(full[n], n, w[n][0], mom[n][0], var[n][0], after=after)
        out_g[n], out_d[n], out_m[n], out_v[n] = g_[None], d_[None], m_[None], v_[None]

    behind = token
    for n in ['ffn2_w_gu', 'w_in', 'ffn2_w_down'] + proj_names:
        adam(n, after=(behind,))
        behind = out_v[n]
    s_back, l_gu1 = chip_exchange_wait(send, recv, s_thru, land_thru, after=behind)
    own = lax.dynamic_slice_in_dim(s_back, me_q, 1, axis=0)
    l_gu1 = lax.dynamic_update_slice_in_dim(l_gu1, own, me_q, axis=0)
    (vec_all,), (f_down1, f_gu1) = comm_call('swap_last', [SmallGatherStage(vec_blk), SwapStage([l_down1, l_gu1])])
    full['ffn1_w_down'] = [f_down1]
    full['ffn1_w_gu'] = [f_gu1]
    adam('ffn1_w_gu')
    adam('ffn1_w_down')

    tot = small_sum(vec_all, lru_all)
    loss = tot[ROW_WA - 1, 0]
    conv_g = lax.dynamic_slice(tot[ROW_CONV:ROW_CONV + 4], (0, me_q * (LRU_W // N_CHIPS)), (4, LRU_W // N_CHIPS))
    small_g = _unpack_small(tot, shapes)
    small_g['conv_w'] = conv_g.reshape(shapes['conv_w'])
    g_pack = jnp.concatenate([tot[:ROW_CONV], conv_g.reshape(1, D_MODEL), jnp.zeros((ROW_WA - ROW_CONV - 1, D_MODEL), F32),
                              tot[ROW_WA:]], axis=0)
    packs = [_pack_small({n: d[n] for n in SMALL}, d['conv_w'].reshape(1, D_MODEL)) for d in (w, mom, var)]
    d_p, m_p, v_p = adam_small(g_pack, *packs)
    for n in SMALL:
        out_g[n] = small_g[n]
    for dst, p in ((out_d, d_p), (out_m, m_p), (out_v, v_p)):
        dst.update(_unpack_small(p, shapes))

    return (loss, dx[None], *[out_g[n] for n in WEIGHTS], *[out_d[n] for n in WEIGHTS],
            *[out_m[n] for n in WEIGHTS], *[out_v[n] for n in WEIGHTS])
```

```python
import jax
import jax.numpy as jnp
import numpy as np
from jax import lax
from jax.experimental import pallas as pl
from jax.experimental.pallas import tpu as pltpu

F32 = jnp.float32
BF16 = jnp.bfloat16

SEQ = 2048
D_MODEL = 1024
D_FF = 2816
LRU_W = 1024
LRU_BLOCK_W = 64
HEAD_DIM = 64
N_Q_HEADS = 16
N_KV_HEADS = 4
KV_W = N_KV_HEADS * HEAD_DIM
ATTN_BLOCK = 128
N_ATTN_BLOCKS = SEQ // ATTN_BLOCK
IN_SEGS = (1024, 1024, 1024, 256, 256, 1024, 1024)
IN_W = sum(IN_SEGS)
NORM_EPS = 1e-6
MASK_VALUE = -1e30
ROPE_THETA = 10000.0
LRU_C = 8.0
MACARON = 0.5
ADAM_LR = 0.001
ADAM_B1 = 0.9
ADAM_B2 = 0.999
ADAM_EPS = 1e-08
ADAM_WD = 0.01
ADAM_STEP = 10

N_CHIPS = 4
N_DEV = 8
VMEM_LIMIT = 56 * 1024 * 1024
MM_ROWS = 256
MESH = pl.DeviceIdType.MESH
ANY = pl.BlockSpec(memory_space=pl.ANY)

PACK = (('ffn1_w_gu', 1408, True), ('w_in', 1408, True), ('ffn2_w_gu', 1408, True),
        ('ffn1_w_down', 704, False), ('ffn2_w_down', 704, False),
        ('w_proj_lru', 256, False), ('w_proj_attn', 256, False), ('w_out', 256, False))
PACK_ROWS_OF = {n: r for n, r, _ in PACK}
PACK_OFF = {}
_o = 0
for _n, _r, _t in PACK:
    PACK_OFF[_n] = _o
    _o += _r

SMALL_VECS = ('ffn1_pre_g', 'ffn1_post_g', 'mix_pre_g', 'conv_b', 'lru_b_a', 'lru_b_x', 'lru_lambda',
              'mix_post_g', 'ffn2_pre_g', 'ffn2_post_g')
SMALL_ROWS = 144
ROW_SINKS, ROW_CONV, ROW_WA, ROW_WX = 10, 11, 16, 80


def _dot(a, b):
    return jnp.dot(a, b, preferred_element_type=F32)


def _dot_nt(a, b):
    return lax.dot_general(a, b, (((1,), (1,)), ((), ())), preferred_element_type=F32)


def _dot_tn(a, b):
    return lax.dot_general(a, b, (((0,), (0,)), ((), ())), preferred_element_type=F32)


def _params(n_grid):
    return pltpu.CompilerParams(dimension_semantics=("arbitrary",) * n_grid, vmem_limit_bytes=VMEM_LIMIT)


def _sigmoid(x):
    return 1.0 / (1.0 + jnp.exp(-x))


def _rsqrt_mean_sq(x):
    return lax.rsqrt(jnp.mean(x * x, axis=-1, keepdims=True) + NORM_EPS)


def _expm1(x):
    poly = x * (1.0 + x * (0.5 + x * (1.0 / 6.0)))
    return jnp.where(jnp.abs(x) < 0.02, poly, jnp.exp(x) - 1.0)


_GELU_K = 0.7978845608028654
_GELU_C = 0.044715


def _gelu(x):
    t = jnp.tanh(_GELU_K * (x + _GELU_C * x * x * x))
    return 0.5 * x * (1.0 + t), t


def _gelu_grad(x, t):
    return 0.5 * (1.0 + t) + 0.5 * x * (1.0 - t * t) * _GELU_K * (1.0 + 3.0 * _GELU_C * x * x)


def _load_weight(w_refs, dst_ref, sem):
    w_refs = list(w_refs) if isinstance(w_refs, (list, tuple)) else [w_refs]
    rows = dst_ref.shape[0] // N_CHIPS
    rp = rows // len(w_refs)
    cps = [pltpu.make_async_copy(w_ref.at[q], dst_ref.at[pl.ds(q * rows + p * rp, rp)], sem.at[p * N_CHIPS + q])
           for p, w_ref in enumerate(w_refs) for q in range(N_CHIPS)]
    for cp in cps:
        cp.start()
    for cp in cps:
        cp.wait()


def _weight_scratch(rows_total, parts=1):
    return [pltpu.VMEM((rows_total, D_MODEL), BF16), pltpu.SemaphoreType.DMA((N_CHIPS * parts,))]


_ROW = lambda tm: pl.BlockSpec((tm, D_MODEL), lambda i: (i, 0))
_VEC = pl.BlockSpec((1, D_MODEL), lambda i: (0, 0))


def _call(body, *, name, grid, in_specs, out_specs, out_shape, args, scratch_shapes=(), stages=()):
    in_specs, out_specs, out_shape, scratch_shapes = list(in_specs), list(out_specs), list(out_shape), list(scratch_shapes)
    n_in, n_out, n_sc = len(in_specs), len(out_specs), len(scratch_shapes)
    k_in = [len(s.inputs) for s in stages]
    k_out = [len(s.out_shape) for s in stages]
    k_sc = [len(s.scratch) for s in stages]
    last = grid[0] - 1

    def split(refs, counts):
        parts, pos = [], 0
        for k in counts:
            parts.append(refs[pos:pos + k])
            pos += k
        return parts

    kinds = tuple(sorted({k for s in stages for k in s.peers}))
    collective_id = {(): None, ('sib',): 0, ('chips',): 1, ('chips', 'sib'): 2}[kinds]

    def full(*refs):
        ins, s_ins, outs, s_outs, scr, s_scr = split(refs, [n_in, sum(k_in), n_out, sum(k_out), n_sc, sum(k_sc)])
        per_stage = list(zip(stages, split(s_ins, k_in), split(s_outs, k_out), split(s_scr, k_sc)))
        i = pl.program_id(0)
        if stages:
            @pl.when(i == 0)
            def _():
                x, y, c, chips = _place()
                peers = ([(x, y, 1 - c)] if 'sib' in kinds else []) + ([(cx, cy, c) for cx, cy in chips] if 'chips' in kinds else [])
                barrier = pltpu.get_barrier_semaphore()
                for peer in peers:
                    pl.semaphore_signal(barrier, inc=1, device_id=peer, device_id_type=MESH)
                pl.semaphore_wait(barrier, len(peers))
                for s, a, b, c_ in per_stage:
                    s.start(a, b, c_)

        body(*ins, *outs, *scr)
        if stages:
            @pl.when(i == last // 2)
            def _():
                for s, a, b, c in per_stage:
                    s.relay(a, b, c)

            @pl.when(i == max(last - 1, 0))
            def _():
                for s, a, b, c in per_stage:
                    s.mid(a, b, c)

            @pl.when(i == last)
            def _():
                for s, a, b, c in per_stage:
                    s.end(a, b, c)

    res = pl.pallas_call(
        full, name=name, grid=grid,
        in_specs=in_specs + [ANY] * sum(k_in),
        out_specs=out_specs + [ANY] * sum(k_out),
        out_shape=out_shape + [o for s in stages for o in s.out_shape],
        scratch_shapes=scratch_shapes + [x for s in stages for x in s.scratch],
        compiler_params=pltpu.CompilerParams(dimension_semantics=("arbitrary",), vmem_limit_bytes=VMEM_LIMIT,
                                             collective_id=collective_id),
    )(*args, *[a for s in stages for a in s.inputs])
    return list(res[:n_out]), split(list(res[n_out:]), k_out)


def ffn_fwd_a(x, g_pre, w_gu_t, name, stages=()):
    tm, tn = MM_ROWS, 256
    n_w = len(w_gu_t)

    def body(x_ref, gp_ref, *refs):
        w_refs = refs[:n_w]
        n_ref, g_ref, u_ref, a_ref, wt_ref, sem = refs[n_w:]

        @pl.when(pl.program_id(0) == 0)
        def _():
            _load_weight(w_refs, wt_ref, sem)

        xv = x_ref[...]
        n = (xv * _rsqrt_mean_sq(xv) * gp_ref[...]).astype(BF16)
        n_ref[...] = n
        for j in range(D_FF // tn):
            g = _dot_nt(n, wt_ref[j * tn:(j + 1) * tn, :])
            u = _dot_nt(n, wt_ref[D_FF + j * tn:D_FF + (j + 1) * tn, :])
            g_ref[:, j * tn:(j + 1) * tn] = g.astype(BF16)
            u_ref[:, j * tn:(j + 1) * tn] = u.astype(BF16)
            a_ref[:, j * tn:(j + 1) * tn] = (g * _sigmoid(g) * u).astype(BF16)

    wide = pl.BlockSpec((tm, D_FF), lambda i: (i, 0))
    return _call(
        body, name=name, grid=(SEQ // tm,),
        in_specs=[_ROW(tm), _VEC] + [ANY] * n_w,
        out_specs=[_ROW(tm), wide, wide, wide],
        out_shape=[jax.ShapeDtypeStruct((SEQ, D_MODEL), BF16)] + [jax.ShapeDtypeStruct((SEQ, D_FF), BF16)] * 3,
        scratch_shapes=_weight_scratch(2 * D_FF, n_w),
        args=[x, g_pre, *w_gu_t], stages=stages)


def ffn_fwd_b(a, w_down, g_post, h_in, name, target=None, stages=()):
    tm = MM_ROWS
    final = target is not None

    def body(*refs):
        if final:
            a_ref, wf_ref, gp_ref, h_ref, t_ref, f_ref, o_ref, loss_ref, wd_ref, sem = refs
        else:
            a_ref, wf_ref, gp_ref, h_ref, f_ref, o_ref, wd_ref, sem = refs

        @pl.when(pl.program_id(0) == 0)
        def _():
            _load_weight(wf_ref, wd_ref, sem)
            if final:
                loss_ref[...] = jnp.zeros_like(loss_ref)

        f = _dot(a_ref[...], wd_ref[...])
        f_ref[...] = f
        y = h_ref[...] + MACARON * (f * _rsqrt_mean_sq(f) * gp_ref[...])
        if final:
            err = y - t_ref[...]
            o_ref[...] = err * (1.0 / D_MODEL)
            loss_ref[...] += 0.5 * jnp.sum(err * err) * (1.0 / D_MODEL)
        else:
            o_ref[...] = y

    row = _ROW(tm)
    in_specs = [pl.BlockSpec((tm, D_FF), lambda i: (i, 0)), ANY, _VEC, row]
    out_specs = [row, row]
    out_shape = [jax.ShapeDtypeStruct((SEQ, D_MODEL), F32)] * 2
    args = [a, w_down, g_post, h_in]
    if final:
        in_specs.append(row)
        args.append(target)
        out_specs.append(pl.BlockSpec((8, 128), lambda i: (0, 0)))
        out_shape.append(jax.ShapeDtypeStruct((8, 128), F32))
    return _call(body, name=name, grid=(SEQ // tm,), in_specs=in_specs, out_specs=out_specs,
                 out_shape=out_shape, scratch_shapes=_weight_scratch(D_FF), args=args, stages=stages)


def ffn_bwd_a(d_out, f, g_post, w_down, g, u, name, stages=()):
    tm = MM_ROWS
    tc = 256

    def body(do_ref, f_ref, gp_ref, wf_ref, g_ref, u_ref, df_ref, dgu_ref, dgp_ref, wd_ref, sem):
        @pl.when(pl.program_id(0) == 0)
        def _():
            _load_weight(wf_ref, wd_ref, sem)
            dgp_ref[...] = jnp.zeros_like(dgp_ref)

        fv = f_ref[...]
        rf = _rsqrt_mean_sq(fv)
        fh = fv * rf
        dn = MACARON * do_ref[...]
        dgp_ref[...] += jnp.sum(dn * fh, axis=0, keepdims=True)
        t = dn * gp_ref[...]
        df = (rf * (t - fh * jnp.mean(t * fh, axis=-1, keepdims=True))).astype(BF16)
        df_ref[...] = df
        for c0 in range(0, D_FF, tc):
            da = _dot_nt(df, wd_ref[c0:c0 + tc, :])
            gv = g_ref[:, c0:c0 + tc].astype(F32)
            uv = u_ref[:, c0:c0 + tc].astype(F32)
            s = _sigmoid(gv)
            dgu_ref[:, c0:c0 + tc] = (da * uv * s * (1.0 + gv * (1.0 - s))).astype(BF16)
            dgu_ref[:, D_FF + c0:D_FF + c0 + tc] = (da * gv * s).astype(BF16)

    row = _ROW(tm)
    wide = pl.BlockSpec((tm, D_FF), lambda i: (i, 0))
    return _call(
        body, name=name, grid=(SEQ // tm,),
        in_specs=[row, row, _VEC, ANY, wide, wide],
        out_specs=[row, pl.BlockSpec((tm, 2 * D_FF), lambda i: (i, 0)), _VEC],
        out_shape=[jax.ShapeDtypeStruct((SEQ, D_MODEL), BF16), jax.ShapeDtypeStruct((SEQ, 2 * D_FF), BF16),
                   jax.ShapeDtypeStruct((1, D_MODEL), F32)],
        scratch_shapes=_weight_scratch(D_FF),
        args=[d_out, f, g_post, w_down, g, u], stages=stages)


def norm_bwd(pieces, w_t, x, g_pre, d_res, name, stages=()):
    tm = MM_ROWS
    widths = [p.shape[1] for p in pieces]
    offs = [sum(widths[:k]) for k in range(len(widths))]
    n_p = len(pieces)
    n_w = len(w_t)

    def body(*refs):
        p_refs = refs[:n_p]
        w_refs = refs[n_p:n_p + n_w]
        x_ref, g_ref, r_ref, dx_ref, dg_ref, wt_ref, sem = refs[n_p + n_w:]

        @pl.when(pl.program_id(0) == 0)
        def _():
            _load_weight(w_refs, wt_ref, sem)
            dg_ref[...] = jnp.zeros_like(dg_ref)

        dn = None
        for p_ref, lo, wd in zip(p_refs, offs, widths):
            part = _dot(p_ref[...], wt_ref[lo:lo + wd, :])
            dn = part if dn is None else dn + part
        xv = x_ref[...]
        r = _rsqrt_mean_sq(xv)
        xh = xv * r
        dg_ref[...] += jnp.sum(dn * xh, axis=0, keepdims=True)
        t = dn * g_ref[...]
        dx_ref[...] = r_ref[...] + r * (t - xh * jnp.mean(t * xh, axis=-1, keepdims=True))

    row = _ROW(tm)
    return _call(
        body, name=name, grid=(SEQ // tm,),
        in_specs=[pl.BlockSpec((tm, wd), lambda i: (i, 0)) for wd in widths] + [ANY] * n_w + [row, _VEC, row],
        out_specs=[row, _VEC],
        out_shape=[jax.ShapeDtypeStruct((SEQ, D_MODEL), F32), jax.ShapeDtypeStruct((1, D_MODEL), F32)],
        scratch_shapes=_weight_scratch(sum(widths), n_w),
        args=[*pieces, *w_t, x, g_pre, d_res], stages=stages)


def mm_tn(pieces, b, tm, name, stages=()):
    widths = [p.shape[1] for p in pieces]
    m_total = sum(widths)
    n_p = len(pieces)
    starts = [sum(widths[:k]) // tm for k in range(n_p)]
    counts = [wd // tm for wd in widths]

    def body(*refs):
        p_refs = refs[:n_p]
        b_ref, o_ref = refs[n_p:]
        i = pl.program_id(0)
        for p_ref, st, ct in zip(p_refs, starts, counts):
            @pl.when((i >= st) & (i < st + ct))
            def _(p_ref=p_ref):
                o_ref[...] = _dot_tn(p_ref[...], b_ref[...]).astype(BF16)

    def piece_spec(st, ct):
        return pl.BlockSpec((SEQ, tm), lambda i: (0, jnp.clip(i - st, 0, ct - 1)))

    (out,), stage_out = _call(
        body, name=name, grid=(m_total // tm,),
        in_specs=[piece_spec(st, ct) for st, ct in zip(starts, counts)] + [pl.BlockSpec((SEQ, D_MODEL), lambda i: (0, 0))],
        out_specs=[pl.BlockSpec((tm, D_MODEL), lambda i: (i, 0))],
        out_shape=[jax.ShapeDtypeStruct((m_total, D_MODEL), BF16)],
        args=[*pieces, b], stages=stages)
    return out, stage_out


def mix_in(h, g_pre, w_in_t, name, stages=()):
    tm = MM_ROWS
    offs = [sum(IN_SEGS[:k]) for k in range(len(IN_SEGS))]
    dts = [F32, F32, F32, F32, BF16, F32, F32]
    n_o = len(IN_SEGS)
    n_w = len(w_in_t)

    def body(*refs):
        h_ref, g_ref = refs[:2]
        w_refs = refs[2:2 + n_w]
        um_ref = refs[2 + n_w]
        o_refs = refs[3 + n_w:3 + n_w + n_o]
        wt_ref, sem = refs[3 + n_w + n_o:]

        @pl.when(pl.program_id(0) == 0)
        def _():
            _load_weight(w_refs, wt_ref, sem)

        hv = h_ref[...]
        um = (hv * _rsqrt_mean_sq(hv) * g_ref[...]).astype(BF16)
        um_ref[...] = um
        for o_ref, lo, wd in zip(o_refs, offs, IN_SEGS):
            for c0 in range(0, wd, 256):
                o_ref[:, c0:c0 + 256] = _dot_nt(um, wt_ref[lo + c0:lo + c0 + 256, :]).astype(o_ref.dtype)

    return _call(
        body, name=name, grid=(SEQ // tm,),
        in_specs=[_ROW(tm), _VEC] + [ANY] * n_w,
        out_specs=[_ROW(tm)] + [pl.BlockSpec((tm, wd), lambda i: (i, 0)) for wd in IN_SEGS],
        out_shape=[jax.ShapeDtypeStruct((SEQ, D_MODEL), BF16)]
        + [jax.ShapeDtypeStruct((SEQ, wd), dt) for wd, dt in zip(IN_SEGS, dts)],
        scratch_shapes=_weight_scratch(IN_W, n_w),
        args=[h, g_pre, *w_in_t], stages=stages)


LRU_TC = 256


def _conv_fwd(xb, cw, cb, tt):
    xc = xb * cw[3:4, :] + cb
    shifted = []
    for s in (1, 2, 3):
        sh = jnp.where(tt >= s, pltpu.roll(xb, s, 0), 0.0)
        shifted.append(sh)
        xc = xc + sh * cw[3 - s:4 - s, :]
    return xc, shifted


def _lru_gates(xc, wa, ba, wx, bx, lam):
    xcb = xc.astype(BF16)
    r = _sigmoid(_dot(xcb, wa) + ba)
    i = _sigmoid(_dot(xcb, wx) + bx)
    nl = -lam
    sp = jnp.maximum(nl, 0.0) + jnp.log1p(jnp.exp(-jnp.abs(nl)))
    la = (-LRU_C * r) * sp
    a = jnp.exp(la)
    mult = jnp.sqrt(jnp.maximum(-_expm1(2.0 * la), 0.0))
    return xcb, r, i, sp, a, mult


def _scan(a, b, tt, reverse, a_s, b_s):
    n = a.shape[0]
    tg = tt & 7
    for s in (1, 2, 4):
        keep = (tg < 8 - s) if reverse else (tg >= s)
        shift = n - s if reverse else s
        b = a * jnp.where(keep, pltpu.roll(b, shift, 0), 0.0) + b
        a = a * jnp.where(keep, pltpu.roll(a, shift, 0), 1.0)
    a_s[...] = a
    b_s[...] = b
    groups = n // 8

    def step(g, carry):
        gi = (groups - 1 - g) if reverse else g
        rows = pl.ds(pl.multiple_of(gi * 8, 8), 8)
        hg = a_s[rows, :] * carry + b_s[rows, :]
        b_s[rows, :] = hg
        return hg[0:1, :] if reverse else hg[7:8, :]

    lax.fori_loop(0, groups, step, jnp.zeros((1, a.shape[1]), F32), unroll=8)
    return b_s[...]


def _lru_specs():
    col = pl.BlockSpec((SEQ, LRU_TC), lambda j: (0, j))
    vec = pl.BlockSpec((1, LRU_TC), lambda j: (0, j))
    bd = pl.BlockSpec((1, LRU_TC, LRU_TC), lambda j: (j, 0, 0))
    cw = pl.BlockSpec((4, LRU_TC), lambda j: (0, j))
    return col, vec, bd, cw


def lru_fwd(gate, xbr, conv_w, conv_b, wa_bd, b_a, wx_bd, b_x, lam, name, stages=()):
    col, vec, bd, cw = _lru_specs()

    def body(gate_ref, xbr_ref, cw_ref, cb_ref, wa_ref, ba_ref, wx_ref, bx_ref, lam_ref, y_ref, h_ref, a_s, b_s):
        tt = lax.broadcasted_iota(jnp.int32, (SEQ, LRU_TC), 0)
        xc, _ = _conv_fwd(xbr_ref[...], cw_ref[...], cb_ref[...], tt)
        _, r, i, sp, a, mult = _lru_gates(xc, wa_ref[0], ba_ref[...], wx_ref[0], bx_ref[...], lam_ref[...])
        h = _scan(a, mult * (i * xc), tt, False, a_s, b_s)
        h_ref[...] = h
        gl, _ = _gelu(gate_ref[...])
        y_ref[...] = (h * gl).astype(BF16)

    return _call(
        body, name=name, grid=(LRU_W // LRU_TC,),
        in_specs=[col, col, cw, vec, bd, vec, bd, vec, vec],
        out_specs=[col, col],
        out_shape=[jax.ShapeDtypeStruct((SEQ, LRU_W), BF16), jax.ShapeDtypeStruct((SEQ, LRU_W), F32)],
        scratch_shapes=[pltpu.VMEM((SEQ, LRU_TC), F32)] * 2,
        args=[gate, xbr, conv_w, conv_b, wa_bd, b_a, wx_bd, b_x, lam], stages=stages)


def lru_bwd(gate, xbr, h, dy, conv_w, conv_b, wa_bd, b_a, wx_bd, b_x, lam, name, stages=()):
    col, vec, bd, cw = _lru_specs()

    def body(gate_ref, xbr_ref, h_ref, dy_ref, cw_ref, cb_ref, wa_ref, ba_ref, wx_ref, bx_ref, lam_ref,
             dgate_ref, dxbr_ref, vecs_ref, dwa_ref, dwx_ref, a_s, b_s):
        tt = lax.broadcasted_iota(jnp.int32, (SEQ, LRU_TC), 0)
        cwv = cw_ref[...]
        lam = lam_ref[...]
        xb = xbr_ref[...]
        xc, shifted = _conv_fwd(xb, cwv, cb_ref[...], tt)
        wa = wa_ref[0]
        wx = wx_ref[0]
        xcb, r, i, sp, a, mult = _lru_gates(xc, wa, ba_ref[...], wx, bx_ref[...], lam)
        hv = h_ref[...]
        dyv = dy_ref[...]
        gv = gate_ref[...]
        gl, th = _gelu(gv)
        dgate_ref[...] = (dyv * hv * _gelu_grad(gv, th)).astype(BF16)
        a_next = jnp.where(tt < SEQ - 1, pltpu.roll(a, SEQ - 1, 0), 0.0)
        gsum = _scan(a_next, dyv * gl, tt, True, a_s, b_s)
        h_prev = jnp.where(tt >= 1, pltpu.roll(hv, 1, 0), 0.0)
        d_mult = gsum * i * xc
        d_i = gsum * mult * xc
        d_xc = gsum * mult * i
        d_la = gsum * h_prev * a - d_mult * (a * a) / mult
        d_pr = (d_la * (-LRU_C * sp)) * r * (1.0 - r)
        d_pi = d_i * i * (1.0 - i)
        d_lam = jnp.sum(d_la * r, axis=0, keepdims=True) * (LRU_C * _sigmoid(-lam))
        d_prb = d_pr.astype(BF16)
        d_pib = d_pi.astype(BF16)
        d_xc = d_xc + _dot_nt(d_prb, wa) + _dot_nt(d_pib, wx)
        dwa_ref[0] = _dot_tn(xcb, d_prb)
        dwx_ref[0] = _dot_tn(xcb, d_pib)
        rows = [jnp.sum(d_xc * shifted[2], axis=0, keepdims=True),
                jnp.sum(d_xc * shifted[1], axis=0, keepdims=True),
                jnp.sum(d_xc * shifted[0], axis=0, keepdims=True),
                jnp.sum(d_xc * xb, axis=0, keepdims=True),
                jnp.sum(d_xc, axis=0, keepdims=True),
                jnp.sum(d_pr, axis=0, keepdims=True),
                jnp.sum(d_pi, axis=0, keepdims=True),
                d_lam]
        ri = lax.broadcasted_iota(jnp.int32, (8, LRU_TC), 0)
        acc = jnp.zeros((8, LRU_TC), F32)
        for k, rv in enumerate(rows):
            acc = jnp.where(ri == k, rv, acc)
        vecs_ref[...] = acc
        d_xb = d_xc * cwv[3:4, :]
        for s in (1, 2, 3):
            d_xb = d_xb + jnp.where(tt < SEQ - s, pltpu.roll(d_xc, SEQ - s, 0), 0.0) * cwv[3 - s:4 - s, :]
        dxbr_ref[...] = d_xb.astype(BF16)

    return _call(
        body, name=name, grid=(LRU_W // LRU_TC,),
        in_specs=[col, col, col, col, cw, vec, bd, vec, bd, vec, vec],
        out_specs=[col, col, pl.BlockSpec((8, LRU_TC), lambda j: (0, j)), bd, bd],
        out_shape=[jax.ShapeDtypeStruct((SEQ, LRU_W), BF16), jax.ShapeDtypeStruct((SEQ, LRU_W), BF16),
                   jax.ShapeDtypeStruct((8, LRU_W), F32),
                   jax.ShapeDtypeStruct((LRU_W // LRU_TC, LRU_TC, LRU_TC), F32),
                   jax.ShapeDtypeStruct((LRU_W // LRU_TC, LRU_TC, LRU_TC), F32)],
        scratch_shapes=[pltpu.VMEM((SEQ, LRU_TC), F32)] * 2,
        args=[gate, xbr, h, dy, conv_w, conv_b, wa_bd, b_a, wx_bd, b_x, lam], stages=stages)


def _rope(x, cos, sin_signed):
    w = x.shape[1]
    reps = w // 128
    if reps > 1:
        cos = jnp.tile(cos, (1, reps))
        sin_signed = jnp.tile(sin_signed, (1, reps))
    lane = lax.broadcasted_iota(jnp.int32, x.shape, 1)
    first = (lane & 63) < 32
    partner = jnp.where(first, pltpu.roll(x, w - 32, 1), pltpu.roll(x, 32, 1))
    return x * cos + partner * sin_signed


def _both_halves(t, odd):
    lo = lax.broadcasted_iota(jnp.int32, t.shape, 1) < 64
    rolled = pltpu.roll(t, 64, 1)
    return jnp.where(lo, rolled, t) if odd else jnp.where(lo, t, rolled)


def _stack_heads(ta, tb):
    lo = lax.broadcasted_iota(jnp.int32, ta.shape, 1) < 64
    return jnp.concatenate([jnp.where(lo, ta, 0.0), jnp.where(lo, 0.0, ta),
                            jnp.where(lo, tb, 0.0), jnp.where(lo, 0.0, tb)], axis=0)


def _unstack_heads(o):
    lo = lax.broadcasted_iota(jnp.int32, (ATTN_BLOCK, 128), 1) < 64
    return (jnp.where(lo, o[0:128], o[128:256]), jnp.where(lo, o[256:384], o[384:512]))


def _window_upper_t():
    shape = (ATTN_BLOCK, 4 * ATTN_BLOCK)
    return lax.broadcasted_iota(jnp.int32, shape, 0) > (lax.broadcasted_iota(jnp.int32, shape, 1) & (ATTN_BLOCK - 1))


def _fold_t(t, upper_t):
    return jnp.where(upper_t, t[:ATTN_BLOCK], t[ATTN_BLOCK:])


def _unfold_t(t, upper_t):
    zero = jnp.zeros_like(t)
    return jnp.concatenate([jnp.where(upper_t, t, zero), jnp.where(upper_t, zero, t)], axis=0)


def _attn_probs_t(kd, qs, sinks_ref, hk, first_block, upper_t):
    s = _fold_t(_dot_nt(kd, qs), upper_t) * (HEAD_DIM ** -0.5)
    s = jnp.where(jnp.logical_and(upper_t, first_block), MASK_VALUE, s)
    rg = lax.broadcasted_iota(jnp.int32, (1, 4 * ATTN_BLOCK), 1) >> 7
    sink = jnp.where(rg == 0, sinks_ref[4 * hk],
                     jnp.where(rg == 1, sinks_ref[4 * hk + 1],
                               jnp.where(rg == 2, sinks_ref[4 * hk + 2], sinks_ref[4 * hk + 3])))
    m = jnp.maximum(jnp.max(s, axis=0, keepdims=True), sink)
    e = jnp.exp(s - m)
    es = jnp.exp(sink - m)
    inv = 1.0 / (jnp.sum(e, axis=0, keepdims=True) + es)
    return e * inv, es * inv


def _prev(i):
    return jnp.maximum(i - 1, 0)


def attn_fwd(q, k, v, cos, sin_signed, sinks, name, stages=()):
    nb = ATTN_BLOCK

    def body(q_ref, kc_ref, kp_ref, vc_ref, vp_ref, cc_ref, sc_ref, cp_ref, sp_ref, sinks_ref,
             qr_ref, kr_ref, y_ref):
        first_block = pl.program_id(0) == 0
        qr = _rope(q_ref[...], cc_ref[...], sc_ref[...])
        kc = _rope(kc_ref[...], cc_ref[...], sc_ref[...])
        kp = _rope(kp_ref[...], cp_ref[...], sp_ref[...])
        qr_ref[...] = qr.astype(BF16)
        kr_ref[...] = kc.astype(BF16)
        k2 = jnp.concatenate([kp, kc], axis=0)
        v2 = jnp.concatenate([vp_ref[...].astype(F32), vc_ref[...].astype(F32)], axis=0)
        upper_t = _window_upper_t()
        for hk in range(N_KV_HEADS):
            kt = hk // 2
            kd = _both_halves(k2[:, kt * 128:(kt + 1) * 128], hk % 2).astype(BF16)
            vd = _both_halves(v2[:, kt * 128:(kt + 1) * 128], hk % 2).astype(BF16)
            qs = _stack_heads(qr[:, (2 * hk) * 128:(2 * hk + 1) * 128],
                              qr[:, (2 * hk + 1) * 128:(2 * hk + 2) * 128]).astype(BF16)
            p, _ = _attn_probs_t(kd, qs, sinks_ref, hk, first_block, upper_t)
            ta, tb = _unstack_heads(_dot_tn(_unfold_t(p.astype(BF16), upper_t), vd))
            y_ref[:, (2 * hk) * 128:(2 * hk + 1) * 128] = ta.astype(BF16)
            y_ref[:, (2 * hk + 1) * 128:(2 * hk + 2) * 128] = tb.astype(BF16)

    cur = lambda w: pl.BlockSpec((nb, w), lambda i: (i, 0))
    prv = lambda w: pl.BlockSpec((nb, w), lambda i: (_prev(i), 0))
    return _call(
        body, name=name, grid=(N_ATTN_BLOCKS,),
        in_specs=[cur(D_MODEL), cur(KV_W), prv(KV_W), cur(KV_W), prv(KV_W), cur(128), cur(128), prv(128), prv(128),
                  pl.BlockSpec(memory_space=pltpu.SMEM)],
        out_specs=[cur(D_MODEL), cur(KV_W), cur(D_MODEL)],
        out_shape=[jax.ShapeDtypeStruct((SEQ, D_MODEL), BF16), jax.ShapeDtypeStruct((SEQ, KV_W), BF16),
                   jax.ShapeDtypeStruct((SEQ, D_MODEL), BF16)],
        args=[q, k, k, v, v, cos, sin_signed, cos, sin_signed, sinks], stages=stages)


def attn_bwd(qr, kr, v, dy, cos, sin_signed, sinks, name, stages=()):
    nb = ATTN_BLOCK
    n_steps = N_ATTN_BLOCKS + 1
    scale = HEAD_DIM ** -0.5

    def body(q_ref, kc_ref, kp_ref, vc_ref, vp_ref, dy_ref, cc_ref, sc_ref, cp_ref, sp_ref, sinks_ref,
             dq_ref, dkv_ref, dsk_ref, ck_ref, cv_ref):
        dk_ref = dkv_ref.at[:, pl.ds(0, KV_W)]
        dv_ref = dkv_ref.at[:, pl.ds(KV_W, KV_W)]
        i = pl.program_id(0)

        @pl.when(i == 0)
        def _():
            dsk_ref[...] = jnp.zeros_like(dsk_ref)
            ck_ref[...] = jnp.zeros_like(ck_ref)
            cv_ref[...] = jnp.zeros_like(cv_ref)

        @pl.when(i < N_ATTN_BLOCKS)
        def _():
            qv = q_ref[...].astype(F32)
            dov = dy_ref[...].astype(F32)
            k2 = jnp.concatenate([kp_ref[...].astype(F32), kc_ref[...].astype(F32)], axis=0)
            v2 = jnp.concatenate([vp_ref[...].astype(F32), vc_ref[...].astype(F32)], axis=0)
            lane = lax.broadcasted_iota(jnp.int32, (8, 128), 1)
            lo = lax.broadcasted_iota(jnp.int32, (2 * nb, 128), 1) < 64
            dsk = jnp.zeros((8, 128), F32)
            dk_tiles = []
            dv_tiles = []
            upper_t = _window_upper_t()
            for hk in range(N_KV_HEADS):
                kt = hk // 2
                kd = _both_halves(k2[:, kt * 128:(kt + 1) * 128], hk % 2).astype(BF16)
                vd = _both_halves(v2[:, kt * 128:(kt + 1) * 128], hk % 2).astype(BF16)
                qs = _stack_heads(qv[:, (2 * hk) * 128:(2 * hk + 1) * 128],
                                  qv[:, (2 * hk + 1) * 128:(2 * hk + 2) * 128]).astype(BF16)
                dos = _stack_heads(dov[:, (2 * hk) * 128:(2 * hk + 1) * 128],
                                   dov[:, (2 * hk + 1) * 128:(2 * hk + 2) * 128]).astype(BF16)
                p, ps = _attn_probs_t(kd, qs, sinks_ref, hk, i == 0, upper_t)
                dp = _fold_t(_dot_nt(vd, dos), upper_t)
                delta = jnp.sum(p * dp, axis=0, keepdims=True)
                ds = _unfold_t((p * (dp - delta)).astype(BF16), upper_t)
                dsink = -ps * delta
                for g in range(4):
                    dsk = dsk + jnp.where(lane == 4 * hk + g, jnp.sum(dsink[:, g * nb:(g + 1) * nb]), 0.0)
                ta, tb = _unstack_heads(_dot_tn(ds, kd) * scale)
                dq_a = (2 * hk) * 128
                dq_ref[:, dq_a:dq_a + 128] = _rope(ta, cc_ref[...], -sc_ref[...]).astype(BF16)
                dq_ref[:, dq_a + 128:dq_a + 256] = _rope(tb, cc_ref[...], -sc_ref[...]).astype(BF16)
                rk = _dot(ds, qs) * scale
                rv = _dot(_unfold_t(p.astype(BF16), upper_t), dos)
                dk_tiles.append(rk + pltpu.roll(rk, 64, 1))
                dv_tiles.append(rv + pltpu.roll(rv, 64, 1))
            dsk_ref[...] += dsk
            dk_full = jnp.concatenate([jnp.where(lo, dk_tiles[0], dk_tiles[1]),
                                       jnp.where(lo, dk_tiles[2], dk_tiles[3])], axis=1)
            dv_full = jnp.concatenate([jnp.where(lo, dv_tiles[0], dv_tiles[1]),
                                       jnp.where(lo, dv_tiles[2], dv_tiles[3])], axis=1)
            dk_ref[...] = _rope(ck_ref[...] + dk_full[0:nb], cp_ref[...], -sp_ref[...]).astype(BF16)
            dv_ref[...] = (cv_ref[...] + dv_full[0:nb]).astype(BF16)
            ck_ref[...] = dk_full[nb:2 * nb]
            cv_ref[...] = dv_full[nb:2 * nb]

        @pl.when(i == N_ATTN_BLOCKS)
        def _():
            dk_ref[...] = _rope(ck_ref[...], cp_ref[...], -sp_ref[...]).astype(BF16)
            dv_ref[...] = cv_ref[...].astype(BF16)

    qi = lambda i: jnp.minimum(i, N_ATTN_BLOCKS - 1)
    cur = lambda w: pl.BlockSpec((nb, w), lambda i: (qi(i), 0))
    prv = lambda w: pl.BlockSpec((nb, w), lambda i: (_prev(qi(i)), 0))
    out_prev = lambda w: pl.BlockSpec((nb, w), lambda i: (_prev(i), 0))
    return _call(
        body, name=name, grid=(n_steps,),
        in_specs=[cur(D_MODEL), cur(KV_W), prv(KV_W), cur(KV_W), prv(KV_W), cur(D_MODEL),
                  cur(128), cur(128), out_prev(128), out_prev(128), pl.BlockSpec(memory_space=pltpu.SMEM)],
        out_specs=[cur(D_MODEL), out_prev(2 * KV_W), pl.BlockSpec((8, 128), lambda i: (0, 0))],
        out_shape=[jax.ShapeDtypeStruct((SEQ, D_MODEL), BF16), jax.ShapeDtypeStruct((SEQ, 2 * KV_W), BF16),
                   jax.ShapeDtypeStruct((8, 128), F32)],
        scratch_shapes=[pltpu.VMEM((nb, KV_W), F32), pltpu.VMEM((nb, KV_W), F32)],
        args=[qr, kr, kr, v, v, dy, cos, sin_signed, cos, sin_signed, sinks], stages=stages)


def _proj_scratch():
    return [pltpu.VMEM((D_MODEL, D_MODEL), BF16)] * 3 + [pltpu.SemaphoreType.DMA((3 * N_CHIPS,))]


def _load_projs(w_refs, wl_ref, wa_ref, wo_ref, sem):
    for k, (w_ref, dst) in enumerate(zip(w_refs, (wl_ref, wa_ref, wo_ref))):
        _load_weight(w_ref, dst, sem.at[pl.ds(k * N_CHIPS, N_CHIPS)])


def merge_fwd(y_lru, y_attn, g_lru, g_attn, projs, g_post, h_in, name, stages=()):
    tm = MM_ROWS

    def body(yl_ref, ya_ref, gl_ref, ga_ref, w1_ref, w2_ref, w3_ref, gp_ref, h_ref,
             pl_ref, pa_ref, mg_ref, m_ref, o_ref, wl_ref, wa_ref, wo_ref, sem):
        @pl.when(pl.program_id(0) == 0)
        def _():
            _load_projs((w1_ref, w2_ref, w3_ref), wl_ref, wa_ref, wo_ref, sem)

        p_l = _dot(yl_ref[...], wl_ref[...])
        p_a = _dot(ya_ref[...], wa_ref[...])
        pl_ref[...] = p_l.astype(BF16)
        pa_ref[...] = p_a.astype(BF16)
        merged = (_sigmoid(gl_ref[...]) * p_l + _sigmoid(ga_ref[...]) * p_a).astype(BF16)
        mg_ref[...] = merged
        m = _dot(merged, wo_ref[...])
        m_ref[...] = m
        o_ref[...] = h_ref[...] + m * _rsqrt_mean_sq(m) * gp_ref[...]

    row = _ROW(tm)
    return _call(
        body, name=name, grid=(SEQ // tm,),
        in_specs=[row, row, row, row, ANY, ANY, ANY, _VEC, row],
        out_specs=[row] * 5,
        out_shape=[jax.ShapeDtypeStruct((SEQ, D_MODEL), BF16)] * 3 + [jax.ShapeDtypeStruct((SEQ, D_MODEL), F32)] * 2,
        scratch_shapes=_proj_scratch(),
        args=[y_lru, y_attn, g_lru, g_attn, *projs, g_post, h_in], stages=stages)


def merge_bwd(d_out, m, g_post, projs, g_lru, g_attn, p_l, p_a, name, stages=()):
    tm = 256

    def body(do_ref, m_ref, gp_ref, w1_ref, w2_ref, w3_ref, gl_ref, ga_ref, pl_ref, pa_ref,
             dm_ref, dpl_ref, dpa_ref, dgl_ref, dga_ref, dya_ref, dyl_ref, dgp_ref, wl_ref, wa_ref, wo_ref, sem):
        @pl.when(pl.program_id(0) == 0)
        def _():
            _load_projs((w1_ref, w2_ref, w3_ref), wl_ref, wa_ref, wo_ref, sem)
            dgp_ref[...] = jnp.zeros_like(dgp_ref)

        mv = m_ref[...]
        rm = _rsqrt_mean_sq(mv)
        mh = mv * rm
        dn = do_ref[...]
        dgp_ref[...] += jnp.sum(dn * mh, axis=0, keepdims=True)
        t = dn * gp_ref[...]
        dm = (rm * (t - mh * jnp.mean(t * mh, axis=-1, keepdims=True))).astype(BF16)
        dm_ref[...] = dm
        dmg = _dot_nt(dm, wo_ref[...])
        sl = _sigmoid(gl_ref[...])
        sa = _sigmoid(ga_ref[...])
        dpl = (dmg * sl).astype(BF16)
        dpa = (dmg * sa).astype(BF16)
        dpl_ref[...] = dpl
        dpa_ref[...] = dpa
        dgl_ref[...] = (dmg * pl_ref[...].astype(F32) * sl * (1.0 - sl)).astype(BF16)
        dga_ref[...] = (dmg * pa_ref[...].astype(F32) * sa * (1.0 - sa)).astype(BF16)
        dyl_ref[...] = _dot_nt(dpl, wl_ref[...])
        dya_ref[...] = _dot_nt(dpa, wa_ref[...]).astype(BF16)

    row = _ROW(tm)
    return _call(
        body, name=name, grid=(SEQ // tm,),
        in_specs=[row, row, _VEC, ANY, ANY, ANY, row, row, row, row],
        out_specs=[row] * 7 + [_VEC],
        out_shape=[jax.ShapeDtypeStruct((SEQ, D_MODEL), BF16)] * 6 + [jax.ShapeDtypeStruct((SEQ, D_MODEL), F32),
                                                                       jax.ShapeDtypeStruct((1, D_MODEL), F32)],
        scratch_shapes=_proj_scratch(),
        args=[d_out, m, g_post, *projs, g_lru, g_attn, p_l, p_a], stages=stages)


def _rope_tables():
    half = HEAD_DIM // 2
    inv_freq = np.float32(ROPE_THETA) ** (-np.arange(half, dtype=np.float32) / np.float32(half))
    ang = np.arange(SEQ, dtype=np.float32)[:, None] * inv_freq[None, :]
    cos, sin = np.cos(ang), np.sin(ang)
    return (jnp.asarray(np.tile(np.concatenate([cos, cos], axis=1), (1, 2))),
            jnp.asarray(np.tile(np.concatenate([-sin, sin], axis=1), (1, 2))))


def _block_diag(w):
    per = LRU_TC // LRU_BLOCK_W
    w4 = w.reshape(LRU_W // LRU_TC, per, LRU_BLOCK_W, LRU_BLOCK_W)
    eye = jnp.eye(per, dtype=w.dtype)
    return jnp.einsum('jacd,ab->jacbd', w4, eye).reshape(LRU_W // LRU_TC, LRU_TC, LRU_TC).astype(BF16)


def _diag_blocks(p):
    per = LRU_TC // LRU_BLOCK_W
    p5 = p.reshape(LRU_W // LRU_TC, per, LRU_BLOCK_W, per, LRU_BLOCK_W)
    return jnp.stack([p5[:, a, :, a, :] for a in range(per)], axis=1).reshape(LRU_W // LRU_BLOCK_W, LRU_BLOCK_W, LRU_BLOCK_W)


def _place():
    x, y, c = lax.axis_index('x'), lax.axis_index('y'), lax.axis_index('c')
    chips = [(1 - x, y), (x, 1 - y), (1 - x, 1 - y)]
    return x, y, c, chips


def _rcopy(src, dst, send_sem, recv_sem, to):
    return pltpu.make_async_remote_copy(src_ref=src, dst_ref=dst, send_sem=send_sem, recv_sem=recv_sem,
                                        device_id=to, device_id_type=MESH)


class _Stage:
    inputs, out_shape, scratch, peers = (), (), (), ()

    def start(self, ins, outs, scr):
        plan = self._plan(ins, outs, scr)
        for ld in plan['loads']:
            ld.start()
        for cp in plan['sends']:
            cp.start()

    def relay(self, ins, outs, scr):
        pass

    def mid(self, ins, outs, scr):
        plan = self._plan(ins, outs, scr)
        for ld, st in zip(plan['loads'], plan['stores']):
            ld.wait()
            st.start()
        for arrived, onward in zip(plan['arrivals'], plan['forwards']):
            arrived.wait_recv()
            onward.start()

    def end(self, ins, outs, scr):
        plan = self._plan(ins, outs, scr)
        for st in plan['stores']:
            st.wait()
        for arrived in (plan['final_arrivals'] if plan['forwards'] else plan['arrivals']):
            arrived.wait_recv()
        for cp in plan['sends'] + plan['forwards']:
            cp.wait_send()


def _empty_plan():
    return dict(loads=[], stores=[], sends=[], arrivals=[], forwards=[], final_arrivals=[])


class GatherStage(_Stage):
    peers = ('chips', 'sib')
    N_CP = 12

    def __init__(self, items):
        self.ranges = [(off, rows) for _, off, rows in items]
        self.inputs = [src for src, _, _ in items]
        self.out_shape = [jax.ShapeDtypeStruct((N_CHIPS, rows, D_MODEL), BF16) for _, rows in self.ranges]
        n = self.N_CP * len(items)
        self.scratch = [pltpu.VMEM((sum(r for _, r in self.ranges), D_MODEL), BF16), pltpu.SemaphoreType.DMA((n,)),
                        pltpu.SemaphoreType.DMA((n,)), pltpu.SemaphoreType.DMA((2 * len(items),))]

    def _plan(self, ins, outs, scr):
        buf, send, recv, lsem = scr
        x, y, c, _ = _place()
        me_q, q_x, q_y, q_d = 2 * x + y, 2 * (1 - x) + y, 2 * x + (1 - y), 2 * (1 - x) + (1 - y)
        to_x, to_y, sib = (1 - x, y, c), (x, 1 - y, c), (x, y, 1 - c)
        plan = dict(loads=[], stores=[], first=[], early=[], relays=[], late=[], hand_early=[], hand_late=[], final=[])
        boff = 0
        for w, ((off, rows), p_ref, o_ref) in enumerate(zip(self.ranges, ins, outs)):
            hr = rows // 2
            ch = hr // 2
            plan['loads'].append(pltpu.make_async_copy(p_ref.at[pl.ds(off, rows)], buf.at[pl.ds(boff, rows)], lsem.at[2 * w]))
            plan['stores'].append(pltpu.make_async_copy(buf.at[pl.ds(boff, rows)], o_ref.at[me_q], lsem.at[2 * w + 1]))
            boff += rows
            base = w * self.N_CP
            mine = [pl.ds(pl.multiple_of(c * hr + k * ch, 16), ch) for k in range(2)]
            theirs = [pl.ds(pl.multiple_of((1 - c) * hr + k * ch, 16), ch) for k in range(2)]
            src = [p_ref.at[pl.ds(pl.multiple_of(off + c * hr + k * ch, 16), ch)] for k in range(2)]

            def cp(k, s, d, to):
                return _rcopy(s, d, send.at[base + k], recv.at[base + k], to)

            def here(q, rows_):
                return o_ref.at[q, rows_]

            plan['first'] += [cp(0, src[0], here(me_q, mine[0]), to_x), cp(2, src[1], here(me_q, mine[1]), to_y),
                              cp(1, src[1], here(me_q, mine[1]), to_x), cp(3, src[0], here(me_q, mine[0]), to_y)]
            x_a, y_b = here(q_x, mine[0]), here(q_y, mine[1])
            plan['early'] += [cp(0, x_a, x_a, to_x), cp(2, y_b, y_b, to_y)]
            plan['relays'] += [cp(4, x_a, x_a, to_y), cp(5, y_b, y_b, to_x)]
            plan['hand_early'] += [cp(6, x_a, x_a, sib), cp(7, y_b, y_b, sib)]
            x_b, y_a, d_a, d_b = here(q_x, mine[1]), here(q_y, mine[0]), here(q_d, mine[0]), here(q_d, mine[1])
            plan['late'] += [cp(1, x_b, x_b, to_x), cp(3, y_a, y_a, to_y), cp(4, d_a, d_a, to_y), cp(5, d_b, d_b, to_x)]
            plan['hand_late'] += [cp(8, x_b, x_b, sib), cp(9, y_a, y_a, sib), cp(10, d_a, d_a, sib), cp(11, d_b, d_b, sib)]
            for k, (q, piece) in enumerate([(q_x, 0), (q_y, 1), (q_x, 1), (q_y, 0), (q_d, 0), (q_d, 1)]):
                got = here(q, theirs[piece])
                plan['final'].append(cp(6 + k, got, got, sib))
        return plan

    def start(self, ins, outs, scr):
        plan = self._plan(ins, outs, scr)
        for ld in plan['loads']:
            ld.start()
        for cp in plan['first']:
            cp.start()

    def relay(self, ins, outs, scr):
        plan = self._plan(ins, outs, scr)
        for arrived in plan['early']:
            arrived.wait_recv()
        for cp in plan['relays'] + plan['hand_early']:
            cp.start()

    def mid(self, ins, outs, scr):
        plan = self._plan(ins, outs, scr)
        for ld, st in zip(plan['loads'], plan['stores']):
            ld.wait()
            st.start()
        for arrived in plan['late']:
            arrived.wait_recv()
        for cp in plan['hand_late']:
            cp.start()

    def end(self, ins, outs, scr):
        plan = self._plan(ins, outs, scr)
        for st in plan['stores']:
            st.wait()
        for arrived in plan['final']:
            arrived.wait_recv()
        for cp in plan['first'] + plan['relays'] + plan['hand_early'] + plan['hand_late']:
            cp.wait_send()


class PairStage(_Stage):
    peers = ('sib',)

    def __init__(self, grads):
        self.inputs = list(grads)
        self.out_shape = [jax.ShapeDtypeStruct((N_CHIPS, 1) + g.shape[2:], BF16) for g in grads]
        n_cp = N_CHIPS * len(grads)
        self.scratch = [pltpu.SemaphoreType.DMA((n_cp,)), pltpu.SemaphoreType.DMA((n_cp,))]

    def _plan(self, ins, outs, scr):
        send, recv = scr
        x, y, c, _ = _place()
        plan = _empty_plan()
        for w, (g_ref, l_ref) in enumerate(zip(ins, outs)):
            for q in range(N_CHIPS):
                i = w * N_CHIPS + q
                plan['sends'].append(_rcopy(g_ref.at[q, pl.ds(1 - c, 1)], l_ref.at[q], send.at[i], recv.at[i], (x, y, 1 - c)))
        plan['arrivals'] = plan['sends']
        return plan


class ChipStage(_Stage):
    peers = ('chips',)

    def __init__(self, items):
        self.ranges = [(off, n) for _, off, n in items]
        self.inputs = [s for s, _, _ in items]
        self.out_shape = [jax.ShapeDtypeStruct((N_CHIPS, n, D_MODEL), BF16) for _, n in self.ranges]
        n_cp = 3 * len(items)
        self.scratch = [pltpu.VMEM((sum(n for _, n in self.ranges), D_MODEL), BF16), pltpu.SemaphoreType.DMA((n_cp,)),
                        pltpu.SemaphoreType.DMA((n_cp,)), pltpu.SemaphoreType.DMA((2 * len(items),))]

    def _plan(self, ins, outs, scr):
        buf, send, recv, lsem = scr
        x, y, c, chips = _place()
        me_q = 2 * x + y
        plan = _empty_plan()
        boff = 0
        for w, ((off, n), s_ref, l_ref) in enumerate(zip(self.ranges, ins, outs)):
            rows = pl.ds(off, n)
            plan['loads'].append(pltpu.make_async_copy(s_ref.at[me_q, rows], buf.at[pl.ds(boff, n)], lsem.at[2 * w]))
            plan['stores'].append(pltpu.make_async_copy(buf.at[pl.ds(boff, n)], l_ref.at[me_q], lsem.at[2 * w + 1]))
            boff += n
            for j, (cx, cy) in enumerate(chips):
                i = w * 3 + j
                got = l_ref.at[2 * cx + cy]
                plan['sends'].append(_rcopy(s_ref.at[2 * cx + cy, rows], l_ref.at[me_q], send.at[i], recv.at[i], (cx, cy, c)))
                plan['arrivals'].append(_rcopy(got, got, send.at[i], recv.at[i], (cx, cy, c)))
        return plan


class SwapStage(_Stage):
    peers = ('sib',)

    def __init__(self, items):
        n = len(items)
        self.inputs = list(items)
        self.out_shape = [jax.ShapeDtypeStruct((2,) + a.shape, a.dtype) for a in items]
        self.scratch = [pltpu.VMEM(a.shape, a.dtype) for a in items] + [
            pltpu.SemaphoreType.DMA((n,)), pltpu.SemaphoreType.DMA((n,)), pltpu.SemaphoreType.DMA((2 * n,))]

    def _plan(self, ins, outs, scr):
        bufs, (send, recv, lsem) = scr[:len(ins)], scr[len(ins):]
        x, y, c, _ = _place()
        plan = _empty_plan()
        for w, (h_ref, o_ref, buf) in enumerate(zip(ins, outs, bufs)):
            plan['loads'].append(pltpu.make_async_copy(h_ref, buf, lsem.at[2 * w]))
            plan['stores'].append(pltpu.make_async_copy(buf, o_ref.at[c], lsem.at[2 * w + 1]))
            got = o_ref.at[1 - c]
            plan['sends'].append(_rcopy(h_ref, o_ref.at[c], send.at[w], recv.at[w], (x, y, 1 - c)))
            plan['arrivals'].append(_rcopy(got, got, send.at[w], recv.at[w], (x, y, 1 - c)))
        return plan


class SmallGatherStage(_Stage):
    peers = ('chips', 'sib')

    def __init__(self, blk):
        self.inputs = [blk]
        self.out_shape = [jax.ShapeDtypeStruct((N_DEV,) + blk.shape, blk.dtype)]
        self.scratch = [pltpu.VMEM(blk.shape, blk.dtype), pltpu.SemaphoreType.DMA((7,)), pltpu.SemaphoreType.DMA((7,)),
                        pltpu.SemaphoreType.DMA((2,))]

    def _plan(self, ins, outs, scr):
        (x_ref,), (o_ref,), (buf, send, recv, lsem) = ins, outs, scr
        x, y, c, chips = _place()
        sib = (x, y, 1 - c)

        def slot(px, py, pc):
            return o_ref.at[4 * px + 2 * py + pc]

        plan = _empty_plan()
        plan['loads'].append(pltpu.make_async_copy(x_ref, buf, lsem.at[0]))
        plan['stores'].append(pltpu.make_async_copy(buf, slot(x, y, c), lsem.at[1]))
        from_sib = slot(x, y, 1 - c)
        plan['sends'].append(_rcopy(x_ref, slot(x, y, c), send.at[0], recv.at[0], sib))
        plan['final_arrivals'].append(_rcopy(from_sib, from_sib, send.at[0], recv.at[0], sib))
        for j, (cx, cy) in enumerate(chips):
            got, got_sib = slot(cx, cy, c), slot(cx, cy, 1 - c)
            plan['sends'].append(_rcopy(x_ref, slot(x, y, c), send.at[1 + j], recv.at[1 + j], (cx, cy, c)))
            plan['arrivals'].append(_rcopy(got, got, send.at[1 + j], recv.at[1 + j], (cx, cy, c)))
            plan['forwards'].append(_rcopy(got, got, send.at[4 + j], recv.at[4 + j], sib))
            plan['final_arrivals'].append(_rcopy(got_sib, got_sib, send.at[4 + j], recv.at[4 + j], sib))
        return plan


_HBM = pl.BlockSpec(memory_space=pltpu.HBM)
_SEM = pl.BlockSpec(memory_space=pltpu.SEMAPHORE)
_DATAFLOW = pltpu.CompilerParams(has_side_effects=pltpu.SideEffectType.DATAFLOW_SIDE_EFFECTING)


def chip_exchange_start(s):
    def body(s_ref, land_ref, send, recv, s_thru, land_thru, token):
        x, y, c, chips = _place()
        for j, (cx, cy) in enumerate(chips):
            _rcopy(s_ref.at[2 * cx + cy], land_ref.at[2 * x + y], send.at[j], recv.at[j], (cx, cy, c)).start()
        token[...] = jnp.zeros_like(token)

    return pl.pallas_call(
        body, name='chip_exchange_start',
        out_shape=(pltpu.SemaphoreType.DMA((3,)), pltpu.SemaphoreType.DMA((3,)), pltpu.HBM(s.shape, s.dtype),
                   pltpu.HBM(s.shape, s.dtype), jax.ShapeDtypeStruct((8, 128), F32)),
        in_specs=(_HBM, _HBM), out_specs=(_SEM, _SEM, _HBM, _HBM, pl.BlockSpec(memory_space=pltpu.VMEM)),
        input_output_aliases={0: 2, 1: 3}, compiler_params=_DATAFLOW,
    )(pltpu.with_memory_space_constraint(s, pltpu.HBM),
      pltpu.with_memory_space_constraint(lax.empty(s.shape, s.dtype), pltpu.HBM))


def chip_exchange_wait(send, recv, s_thru, land_thru, after):
    def body(s_ref, land_ref, send_sem, recv_sem, after_ref, s_out, land_out):
        x, y, c, chips = _place()
        for j, (cx, cy) in enumerate(chips):
            cp = _rcopy(s_ref.at[2 * cx + cy], land_ref.at[2 * cx + cy], send_sem.at[j], recv_sem.at[j], (cx, cy, c))
            cp.wait_send()
            cp.wait_recv()

    return pl.pallas_call(
        body, name='chip_exchange_wait',
        out_shape=(pltpu.HBM(s_thru.shape, s_thru.dtype), pltpu.HBM(land_thru.shape, land_thru.dtype)),
        in_specs=(_HBM, _HBM, _SEM, _SEM, ANY), out_specs=(_HBM, _HBM),
        input_output_aliases={0: 0, 1: 1}, compiler_params=_DATAFLOW,
    )(s_thru, land_thru, send, recv, after)


def comm_call(name, stages):
    def body():
        pass

    return _call(body, name=name, grid=(1,), in_specs=[], out_specs=[], out_shape=[], args=[], stages=stages)[1]


def pair_sum(g4, land, c_arr, name):
    hr = g4.shape[2]

    def body(c_ref, g_ref, l_ref, o_ref):
        o_ref[0] = (g_ref[0, 0].astype(F32) + l_ref[0, 0].astype(F32)).astype(BF16)

    return pl.pallas_call(
        body, name=name,
        grid_spec=pltpu.PrefetchScalarGridSpec(
            num_scalar_prefetch=1, grid=(N_CHIPS,),
            in_specs=[pl.BlockSpec((1, 1, hr, D_MODEL), lambda q, c: (q, c[0], 0, 0)),
                      pl.BlockSpec((1, 1, hr, D_MODEL), lambda q, c: (q, 0, 0, 0))],
            out_specs=pl.BlockSpec((1, hr, D_MODEL), lambda q, c: (q, 0, 0))),
        out_shape=jax.ShapeDtypeStruct((N_CHIPS, hr, D_MODEL), BF16),
        compiler_params=_params(1),
    )(c_arr, g4, land)


def small_sum(vec_parts, lru_parts):
    def body(v_ref, l_ref, o_ref):
        for p_ref, lo, n in ((v_ref, 0, ROW_WA), (l_ref, ROW_WA, SMALL_ROWS - ROW_WA)):
            acc = p_ref[0]
            for s in range(1, N_DEV):
                acc = acc + p_ref[s]
            o_ref[lo:lo + n, :] = acc

    return pl.pallas_call(
        body, name='small_sum', grid=(1,),
        in_specs=[pl.BlockSpec(vec_parts.shape, lambda i: (0, 0, 0)), pl.BlockSpec(lru_parts.shape, lambda i: (0, 0, 0))],
        out_specs=pl.BlockSpec((SMALL_ROWS, D_MODEL), lambda i: (0, 0)),
        out_shape=jax.ShapeDtypeStruct((SMALL_ROWS, D_MODEL), F32),
        compiler_params=_params(1),
    )(vec_parts, lru_parts)


def _adam_math(w, g, m, v):
    m2 = ADAM_B1 * m + (1.0 - ADAM_B1) * g
    v2 = ADAM_B2 * v + (1.0 - ADAM_B2) * (g * g)
    m_hat = m2 / (1.0 - ADAM_B1 ** ADAM_STEP)
    v_hat = v2 / (1.0 - ADAM_B2 ** ADAM_STEP)
    delta = -ADAM_LR * (m_hat / (jnp.sqrt(v_hat) + ADAM_EPS) + ADAM_WD * w)
    return delta, m2, v2


def _adam_body(n_parts, transposed, n_after):
    def body(*refs):
        refs = refs[n_after:]
        g_refs = refs[:n_parts]
        w_ref, m_ref, v_ref, go_ref, d_ref, mo_ref, vo_ref = refs[n_parts:]
        def chips_added(blk):
            acc = blk[0].astype(F32)
            for s in range(1, N_CHIPS):
                acc = acc + blk[s].astype(F32)
            return acc

        if transposed:
            g = jnp.concatenate([chips_added(g_ref[h]) for h in range(2) for g_ref in g_refs], axis=0).T
        else:
            rows = [chips_added(g_ref[0]) for g_ref in g_refs]
            g = jnp.concatenate(rows, axis=0) if n_parts > 1 else rows[0]
        go_ref[...] = g
        d_ref[...], mo_ref[...], vo_ref[...] = _adam_math(w_ref[...], g, m_ref[...], v_ref[...])
    return body


def adam_rows(fulls, name, w, m, v, after=()):
    hr = w.shape[0] // 2
    blk = pl.BlockSpec((hr, D_MODEL), lambda h: (h, 0))
    return pl.pallas_call(
        _adam_body(len(fulls), False, len(after)), name='adam_' + name, grid=(2,),
        in_specs=[ANY] * len(after)
        + [pl.BlockSpec((1, N_CHIPS, f.shape[2], D_MODEL), lambda h: (h, 0, 0, 0)) for f in fulls] + [blk, blk, blk],
        out_specs=[blk] * 4,
        out_shape=[jax.ShapeDtypeStruct(w.shape, F32)] * 4,
        compiler_params=_params(1),
    )(*after, *fulls, w, m, v)


def adam_cols(fulls, name, w, m, v, after=()):
    cols = w.shape[1]
    tr = 128
    blk = pl.BlockSpec((tr, cols), lambda i: (i, 0))
    return pl.pallas_call(
        _adam_body(len(fulls), True, len(after)), name='adam_' + name, grid=(D_MODEL // tr,),
        in_specs=[ANY] * len(after)
        + [pl.BlockSpec((2, N_CHIPS, f.shape[2], tr), lambda i: (0, 0, 0, i)) for f in fulls] + [blk, blk, blk],
        out_specs=[blk] * 4,
        out_shape=[jax.ShapeDtypeStruct(w.shape, F32)] * 4,
        compiler_params=_params(1),
    )(*after, *fulls, w, m, v)


def adam_small(g, w, m, v):
    def body(g_ref, w_ref, m_ref, v_ref, d_ref, mo_ref, vo_ref):
        d_ref[...], mo_ref[...], vo_ref[...] = _adam_math(w_ref[...], g_ref[...], m_ref[...], v_ref[...])

    blk = pl.BlockSpec(w.shape, lambda i: (0, 0))
    return pl.pallas_call(
        body, name='adam_small', grid=(1,), in_specs=[blk] * 4, out_specs=[blk] * 3,
        out_shape=[jax.ShapeDtypeStruct(w.shape, F32)] * 3, compiler_params=_params(1),
    )(g, w, m, v)


WEIGHTS = ('ffn1_pre_g', 'ffn1_w_gu', 'ffn1_w_down', 'ffn1_post_g', 'mix_pre_g', 'w_in', 'conv_w', 'conv_b',
           'lru_w_a', 'lru_b_a', 'lru_w_x', 'lru_b_x', 'lru_lambda', 'attn_sinks', 'w_proj_lru', 'w_proj_attn',
           'w_out', 'mix_post_g', 'ffn2_pre_g', 'ffn2_w_gu', 'ffn2_w_down', 'ffn2_post_g')
SMALL = tuple(n for n in WEIGHTS if n not in PACK_OFF)


def cast_t(w, name, stages=()):
    cols = w.shape[1]
    tc = 128

    def body(w_ref, o_ref):
        o_ref[...] = w_ref[...].T.astype(BF16)

    (out,), stage_out = _call(
        body, name=name, grid=(cols // tc,),
        in_specs=[pl.BlockSpec((D_MODEL, tc), lambda j: (0, j))],
        out_specs=[pl.BlockSpec((tc, D_MODEL), lambda j: (j, 0))],
        out_shape=[jax.ShapeDtypeStruct((cols, D_MODEL), BF16)],
        args=[w], stages=stages)
    return out, stage_out


def _pack_vecs(d, conv_rows):
    sinks = jnp.pad(d['attn_sinks'].reshape(1, N_Q_HEADS), ((0, 0), (0, D_MODEL - N_Q_HEADS)))
    conv = jnp.pad(conv_rows, ((0, ROW_WA - ROW_CONV - conv_rows.shape[0]), (0, 0)))
    return jnp.concatenate([d[n].reshape(1, D_MODEL) for n in SMALL_VECS] + [sinks, conv], axis=0)


def _pack_lru(d):
    return jnp.concatenate([d['lru_w_a'].reshape(64, D_MODEL), d['lru_w_x'].reshape(64, D_MODEL)], axis=0)


def _pack_small(d, conv_rows):
    return jnp.concatenate([_pack_vecs(d, conv_rows), _pack_lru(d)], axis=0)


def _unpack_small(p, shapes):
    out = {n: p[k:k + 1].reshape(shapes[n]) for k, n in enumerate(SMALL_VECS)}
    out['attn_sinks'] = p[ROW_SINKS:ROW_SINKS + 1, :N_Q_HEADS].reshape(shapes['attn_sinks'])
    out['conv_w'] = p[ROW_CONV:ROW_CONV + 1].reshape(shapes['conv_w'])
    out['lru_w_a'] = p[ROW_WA:ROW_WA + 64].reshape(shapes['lru_w_a'])
    out['lru_w_x'] = p[ROW_WX:ROW_WX + 64].reshape(shapes['lru_w_x'])
    return out


def kernel(x, ffn1_pre_g, ffn1_w_gu, ffn1_w_down, ffn1_post_g, mix_pre_g, w_in, conv_w, conv_b, lru_w_a, lru_b_a, lru_w_x, lru_b_x, lru_lambda, attn_sinks, w_proj_lru, w_proj_attn, w_out, mix_post_g, ffn2_pre_g, ffn2_w_gu, ffn2_w_down, ffn2_post_g, loss_target, m_ffn1_pre_g, m_ffn1_w_gu, m_ffn1_w_down, m_ffn1_post_g, m_mix_pre_g, m_w_in, m_conv_w, m_conv_b, m_lru_w_a, m_lru_b_a, m_lru_w_x, m_lru_b_x, m_lru_lambda, m_attn_sinks, m_w_proj_lru, m_w_proj_attn, m_w_out, m_mix_post_g, m_ffn2_pre_g, m_ffn2_w_gu, m_ffn2_w_down, m_ffn2_post_g, v_ffn1_pre_g, v_ffn1_w_gu, v_ffn1_w_down, v_ffn1_post_g, v_mix_pre_g, v_w_in, v_conv_w, v_conv_b, v_lru_w_a, v_lru_b_a, v_lru_w_x, v_lru_b_x, v_lru_lambda, v_attn_sinks, v_w_proj_lru, v_w_proj_attn, v_w_out, v_mix_post_g, v_ffn2_pre_g, v_ffn2_w_gu, v_ffn2_w_down, v_ffn2_post_g):
    given = dict(locals())
    w = {n: given[n] for n in WEIGHTS}
    mom = {n: given['m_' + n] for n in WEIGHTS}
    var = {n: given['v_' + n] for n in WEIGHTS}
    shapes = {n: w[n].shape for n in WEIGHTS}
    xq = lax.axis_index('x')
    yq = lax.axis_index('y')
    cq = lax.axis_index('c')
    me_q = 2 * xq + yq

    c_arr = cq.reshape(1).astype(jnp.int32)
    xs, target = x[0], loss_target[0]
    sw = {n: (w[n][0] if w[n].ndim > 2 else w[n]) for n in SMALL}
    cos, sin_signed = _rope_tables()
    wa_bd = _block_diag(sw['lru_w_a'])
    wx_bd = _block_diag(sw['lru_w_x'])
    sinks = sw['attn_sinks'].reshape(N_Q_HEADS)

    shard = {n: w[n][0].astype(BF16) for n, _, t in PACK if not t}
    conv_pad = jnp.pad(w['conv_w'][0], ((0, 4), (0, 0)))

    def whole(name):
        return (shard[name], 0, PACK_ROWS_OF[name])

    def part(name, p, n_parts=2):
        rows = PACK_ROWS_OF[name] // n_parts
        return (shard[name], p * rows, rows)

    shard['ffn1_w_gu'], _ = cast_t(w['ffn1_w_gu'][0], 'cast_ffn1_w_gu')
    shard['w_in'], ((w_gu1a,), (conv_all,)) = cast_t(
        w['w_in'][0], 'cast_w_in', stages=[GatherStage([part('ffn1_w_gu', 0)]), SmallGatherStage(conv_pad)])
    shard['ffn2_w_gu'], ((w_gu1b,),) = cast_t(w['ffn2_w_gu'][0], 'cast_ffn2_w_gu', stages=[GatherStage([part('ffn1_w_gu', 1)])])
    w_gu1 = [w_gu1a, w_gu1b]
    sw['conv_w'] = jnp.transpose(conv_all[0::2, :4, :], (1, 0, 2)).reshape(4, LRU_W)
    proj_names = ['w_proj_lru', 'w_proj_attn', 'w_out']

    (n1, g1, u1, a1), ((w_down1, w_in_a),) = ffn_fwd_a(xs, sw['ffn1_pre_g'], w_gu1, 'ffn1_fwd_a',
                                                        stages=[GatherStage([whole('ffn1_w_down'), part('w_in', 0)])])
    (f1, h1), ((w_in_b,),) = ffn_fwd_b(a1, w_down1, sw['ffn1_post_g'], xs, 'ffn1_fwd_b',
                                       stages=[GatherStage([part('w_in', 1)])])
    w_in_t = [w_in_a, w_in_b]
    (um, gate, xbr, q, k, v, g_lru, g_attn), ((w_gu2a,),) = mix_in(h1, sw['mix_pre_g'], w_in_t, 'mix_in',
                                                                   stages=[GatherStage([part('ffn2_w_gu', 0)])])
    (y_lru, h_lru), ((w_gu2b,),) = lru_fwd(gate, xbr, sw['conv_w'], sw['conv_b'], wa_bd, sw['lru_b_a'], wx_bd, sw['lru_b_x'],
                                           sw['lru_lambda'], 'lru_fwd', stages=[GatherStage([part('ffn2_w_gu', 1)])])
    (qr, kr, y_attn), (projs,) = attn_fwd(q, k, v, cos, sin_signed, sinks, 'attn_fwd',
                                          stages=[GatherStage([whole(n) for n in proj_names])])
    (p_l, p_a, merged, m, h2), ((w_down2,),) = merge_fwd(y_lru, y_attn, g_lru, g_attn, projs, sw['mix_post_g'], h1, 'merge_fwd',
                                                         stages=[GatherStage([whole('ffn2_w_down')])])
    w_gu2 = [w_gu2a, w_gu2b]
    (n2, g2, u2, a2), _ = ffn_fwd_a(h2, sw['ffn2_pre_g'], w_gu2, 'ffn2_fwd_a')
    (f2, dy, loss_blk), _ = ffn_fwd_b(a2, w_down2, sw['ffn2_post_g'], h2, 'ffn2_fwd_b', target=target)

    gs, full = {}, {}

    def pair_stage(names, grads):
        g4 = [g.reshape(N_CHIPS, 2, PACK_ROWS_OF[n] // 2, D_MODEL) for n, g in zip(names, grads)]
        return PairStage(g4), g4

    def pair_sums(names, g4, lands):
        return [pair_sum(g, l, c_arr, 'pair_sum_' + n) for n, g, l in zip(names, g4, lands)]

    def halves(s, n_parts=2):
        n = s.shape[1] // n_parts
        return [(s, p * n, n) for p in range(n_parts)]

    (df2, dgu2, gs['ffn2_post_g']), _ = ffn_bwd_a(dy, f2, sw['ffn2_post_g'], w_down2, g2, u2, 'ffn2_bwd_a')
    g_down2, _ = mm_tn([a2], df2, 1408, 'ffn2_dw_down')
    st, g4 = pair_stage(['ffn2_w_down'], [g_down2])
    g_gu2, (lands,) = mm_tn([dgu2], n2, 1408, 'ffn2_dw_gu', stages=[st])
    (s_down2,) = pair_sums(['ffn2_w_down'], g4, lands)
    st, g4 = pair_stage(['ffn2_w_gu'], [g_gu2])
    (dh2, gs['ffn2_pre_g']), ((l_down2,), lands) = norm_bwd([dgu2], w_gu2, h2, sw['ffn2_pre_g'], dy, 'ffn2_bwd_b',
                                                            stages=[ChipStage([(s_down2, 0, s_down2.shape[1])]), st])
    (s_gu2,) = pair_sums(['ffn2_w_gu'], g4, lands)

    (dm, dpl, dpa, dgl, dga, dya, dyl, gs['mix_post_g']), ((l_gu2a,),) = merge_bwd(
        dh2, m, sw['mix_post_g'], projs, g_lru, g_attn, p_l, p_a, 'merge_bwd', stages=[ChipStage(halves(s_gu2)[:1])])
    g_projs = [mm_tn([merged if n == 'w_out' else (y_lru if n == 'w_proj_lru' else y_attn)],
                     dm if n == 'w_out' else (dpl if n == 'w_proj_lru' else dpa), D_MODEL, 'd' + n)[0] for n in proj_names]
    st, g4 = pair_stage(proj_names, g_projs)
    (dq, dkv, dsk), ((l_gu2b,), lands, (full['ffn2_w_down'],)) = attn_bwd(
        qr, kr, v, dya, cos, sin_signed, sinks, 'attn_bwd', stages=[ChipStage(halves(s_gu2)[1:]), st, SwapStage([l_down2])])
    full['ffn2_w_down'] = [full['ffn2_w_down']]
    gs['attn_sinks'] = dsk[0:1, 0:N_Q_HEADS]
    s_projs = pair_sums(proj_names, g4, lands)
    (dgate, dxbr, vecs, dwa, dwx), (l_projs, full['ffn2_w_gu']) = lru_bwd(
        gate, xbr, h_lru, dyl, sw['conv_w'], sw['conv_b'], wa_bd, sw['lru_b_a'], wx_bd, sw['lru_b_x'], sw['lru_lambda'],
        'lru_bwd', stages=[ChipStage([(s, 0, s.shape[1]) for s in s_projs]), SwapStage([l_gu2a, l_gu2b])])
    gs['conv_w'] = vecs[0:4]
    gs['conv_b'], gs['lru_b_a'], gs['lru_b_x'], gs['lru_lambda'] = vecs[4:5], vecs[5:6], vecs[6:7], vecs[7:8]
    gs['lru_w_a'] = _diag_blocks(dwa)
    gs['lru_w_x'] = _diag_blocks(dwx)
    dz = [dgate, dxbr, dq, dkv, dgl, dga]
    g_in, ((lru_all,),) = mm_tn(dz, um, 512, 'dw_in', stages=[SmallGatherStage(_pack_lru(gs))])
    st, g4 = pair_stage(['w_in'], [g_in])
    (dh1, gs['mix_pre_g']), (lands, f_projs) = norm_bwd(dz, w_in_t, h1, sw['mix_pre_g'], dh2, 'mix_bwd_in',
                                                        stages=[st, SwapStage(l_projs)])
    for n, f in zip(proj_names, f_projs):
        full[n] = [f]
    (s_in,) = pair_sums(['w_in'], g4, lands)

    (df1, dgu1, gs['ffn1_post_g']), ((l_in_a,),) = ffn_bwd_a(dh1, f1, sw['ffn1_post_g'], w_down1, g1, u1, 'ffn1_bwd_a',
                                                             stages=[ChipStage(halves(s_in)[:1])])
    g_down1, _ = mm_tn([a1], df1, 1408, 'ffn1_dw_down')
    st, g4 = pair_stage(['ffn1_w_down'], [g_down1])
    g_gu1, ((l_in_b,), lands) = mm_tn([dgu1], n1, 1408, 'ffn1_dw_gu', stages=[ChipStage(halves(s_in)[1:]), st])
    (s_down1,) = pair_sums(['ffn1_w_down'], g4, lands)
    st, g4 = pair_stage(['ffn1_w_gu'], [g_gu1])
    (dx, gs['ffn1_pre_g']), ((l_down1,), lands, full['w_in']) = norm_bwd(
        [dgu1], w_gu1, xs, sw['ffn1_pre_g'], dh1, 'ffn1_bwd_b',
        stages=[ChipStage([(s_down1, 0, s_down1.shape[1])]), st, SwapStage([l_in_a, l_in_b])])
    (s_gu1,) = pair_sums(['ffn1_w_gu'], g4, lands)
    loss_row = jnp.pad(loss_blk[0:1], ((0, 0), (0, D_MODEL - loss_blk.shape[1])))
    vec_blk = _pack_vecs(gs, jnp.concatenate([gs['conv_w'], loss_row], axis=0))
    send, recv, s_thru, land_thru, token = chip_exchange_start(s_gu1)
    out_g, out_d, out_m, out_v = {}, {}, {}, {}

    def adam(n, after=()):
        fn = adam_cols if dict((k, t) for k, _, t in PACK)[n] else adam_rows
        g_, d_, m_, v_ = fn(full[n], n, w[n][0], mom[n][0], var[n][0], after=after)
        out_g[n], out_d[n], out_m[n], out_v[n] = g_[None], d_[None], m_[None], v_[None]

    behind = token
    for n in ['ffn2_w_gu', 'w_in', 'ffn2_w_down'] + proj_names:
        adam(n, after=(behind,))
        behind = out_v[n]
    s_back, l_gu1 = chip_exchange_wait(send, recv, s_thru, land_thru, after=behind)
    own = lax.dynamic_slice_in_dim(s_back, me_q, 1, axis=0)
    l_gu1 = lax.dynamic_update_slice_in_dim(l_gu1, own, me_q, axis=0)
    (vec_all,), (f_down1, f_gu1) = comm_call('swap_last', [SmallGatherStage(vec_blk), SwapStage([l_down1, l_gu1])])
    full['ffn1_w_down'] = [f_down1]
    full['ffn1_w_gu'] = [f_gu1]
    adam('ffn1_w_gu')
    adam('ffn1_w_down')

    tot = small_sum(vec_all, lru_all)
    loss = tot[ROW_WA - 1, 0]
    conv_g = lax.dynamic_slice(tot[ROW_CONV:ROW_CONV + 4], (0, me_q * (LRU_W // N_CHIPS)), (4, LRU_W // N_CHIPS))
    small_g = _unpack_small(tot, shapes)
    small_g['conv_w'] = conv_g.reshape(shapes['conv_w'])
    g_pack = jnp.concatenate([tot[:ROW_CONV], conv_g.reshape(1, D_MODEL), jnp.zeros((ROW_WA - ROW_CONV - 1, D_MODEL), F32),
                              tot[ROW_WA:]], axis=0)
    packs = [_pack_small({n: d[n] for n in SMALL}, d['conv_w'].reshape(1, D_MODEL)) for d in (w, mom, var)]
    d_p, m_p, v_p = adam_small(g_pack, *packs)
    for n in SMALL:
        out_g[n] = small_g[n]
    for dst, p in ((out_d, d_p), (out_m, m_p), (out_v, v_p)):
        dst.update(_unpack_small(p, shapes))

    return (loss, dx[None], *[out_g[n] for n in WEIGHTS], *[out_d[n] for n in WEIGHTS],
            *[out_m[n] for n in WEIGHTS], *[out_v[n] for n in WEIGHTS])
```

```python
import jax
import jax.numpy as jnp
import numpy as np
from jax import lax
from jax.experimental import pallas as pl
from jax.experimental.pallas import tpu as pltpu

F32 = jnp.float32
BF16 = jnp.bfloat16

SEQ = 2048
D_MODEL = 1024
D_FF = 2816
LRU_W = 1024
LRU_BLOCK_W = 64
HEAD_DIM = 64
N_Q_HEADS = 16
N_KV_HEADS = 4
KV_W = N_KV_HEADS * HEAD_DIM
ATTN_BLOCK = 128
N_ATTN_BLOCKS = SEQ // ATTN_BLOCK
IN_SEGS = (1024, 1024, 1024, 256, 256, 1024, 1024)
IN_W = sum(IN_SEGS)
NORM_EPS = 1e-6
MASK_VALUE = -1e30
ROPE_THETA = 10000.0
LRU_C = 8.0
MACARON = 0.5
ADAM_LR = 0.001
ADAM_B1 = 0.9
ADAM_B2 = 0.999
ADAM_EPS = 1e-08
ADAM_WD = 0.01
ADAM_STEP = 10

N_CHIPS = 4
N_DEV = 8
VMEM_LIMIT = 56 * 1024 * 1024
MM_ROWS = 256
MESH = pl.DeviceIdType.MESH
ANY = pl.BlockSpec(memory_space=pl.ANY)

PACK = (('ffn1_w_gu', 1408, True), ('w_in', 1408, True), ('ffn2_w_gu', 1408, True),
        ('ffn1_w_down', 704, False), ('ffn2_w_down', 704, False),
        ('w_proj_lru', 256, False), ('w_proj_attn', 256, False), ('w_out', 256, False))
PACK_ROWS_OF = {n: r for n, r, _ in PACK}
PACK_OFF = {}
_o = 0
for _n, _r, _t in PACK:
    PACK_OFF[_n] = _o
    _o += _r

SMALL_VECS = ('ffn1_pre_g', 'ffn1_post_g', 'mix_pre_g', 'conv_b', 'lru_b_a', 'lru_b_x', 'lru_lambda',
              'mix_post_g', 'ffn2_pre_g', 'ffn2_post_g')
SMALL_ROWS = 144
ROW_SINKS, ROW_CONV, ROW_WA, ROW_WX = 10, 11, 16, 80


def _dot(a, b):
    return jnp.dot(a, b, preferred_element_type=F32)


def _dot_nt(a, b):
    return lax.dot_general(a, b, (((1,), (1,)), ((), ())), preferred_element_type=F32)


def _dot_tn(a, b):
    return lax.dot_general(a, b, (((0,), (0,)), ((), ())), preferred_element_type=F32)


def _params(n_grid):
    return pltpu.CompilerParams(dimension_semantics=("arbitrary",) * n_grid, vmem_limit_bytes=VMEM_LIMIT)


def _sigmoid(x):
    return 1.0 / (1.0 + jnp.exp(-x))


def _rsqrt_mean_sq(x):
    return lax.rsqrt(jnp.mean(x * x, axis=-1, keepdims=True) + NORM_EPS)


def _expm1(x):
    poly = x * (1.0 + x * (0.5 + x * (1.0 / 6.0)))
    return jnp.where(jnp.abs(x) < 0.02, poly, jnp.exp(x) - 1.0)


_GELU_K = 0.7978845608028654
_GELU_C = 0.044715


def _gelu(x):
    t = jnp.tanh(_GELU_K * (x + _GELU_C * x * x * x))
    return 0.5 * x * (1.0 + t), t


def _gelu_grad(x, t):
    return 0.5 * (1.0 + t) + 0.5 * x * (1.0 - t * t) * _GELU_K * (1.0 + 3.0 * _GELU_C * x * x)


def _load_weight(w_refs, dst_ref, sem):
    w_refs = list(w_refs) if isinstance(w_refs, (list, tuple)) else [w_refs]
    rows = dst_ref.shape[0] // N_CHIPS
    rp = rows // len(w_refs)
    cps = [pltpu.make_async_copy(w_ref.at[q], dst_ref.at[pl.ds(q * rows + p * rp, rp)], sem.at[p * N_CHIPS + q])
           for p, w_ref in enumerate(w_refs) for q in range(N_CHIPS)]
    for cp in cps:
        cp.start()
    for cp in cps:
        cp.wait()


def _weight_scratch(rows_total, parts=1):
    return [pltpu.VMEM((rows_total, D_MODEL), BF16), pltpu.SemaphoreType.DMA((N_CHIPS * parts,))]


_ROW = lambda tm: pl.BlockSpec((tm, D_MODEL), lambda i: (i, 0))
_VEC = pl.BlockSpec((1, D_MODEL), lambda i: (0, 0))


def _call(body, *, name, grid, in_specs, out_specs, out_shape, args, scratch_shapes=(), stages=()):
    in_specs, out_specs, out_shape, scratch_shapes = list(in_specs), list(out_specs), list(out_shape), list(scratch_shapes)
    n_in, n_out, n_sc = len(in_specs), len(out_specs), len(scratch_shapes)
    k_in = [len(s.inputs) for s in stages]
    k_out = [len(s.out_shape) for s in stages]
    k_sc = [len(s.scratch) for s in stages]
    last = grid[0] - 1

    def split(refs, counts):
        parts, pos = [], 0
        for k in counts:
            parts.append(refs[pos:pos + k])
            pos += k
        return parts

    kinds = tuple(sorted({k for s in stages for k in s.peers}))
    collective_id = {(): None, ('sib',): 0, ('chips',): 1, ('chips', 'sib'): 2}[kinds]

    def full(*refs):
        ins, s_ins, outs, s_outs, scr, s_scr = split(refs, [n_in, sum(k_in), n_out, sum(k_out), n_sc, sum(k_sc)])
        per_stage = list(zip(stages, split(s_ins, k_in), split(s_outs, k_out), split(s_scr, k_sc)))
        i = pl.program_id(0)
        if stages:
            @pl.when(i == 0)
            def _():
                x, y, c, chips = _place()
                peers = ([(x, y, 1 - c)] if 'sib' in kinds else []) + ([(cx, cy, c) for cx, cy in chips] if 'chips' in kinds else [])
                barrier = pltpu.get_barrier_semaphore()
                for peer in peers:
                    pl.semaphore_signal(barrier, inc=1, device_id=peer, device_id_type=MESH)
                pl.semaphore_wait(barrier, len(peers))
                for s, a, b, c_ in per_stage:
                    s.start(a, b, c_)

        body(*ins, *outs, *scr)
        if stages:
            @pl.when(i == last // 4)
            def _():
                for s, a, b, c in per_stage:
                    s.relay(a, b, c)

            @pl.when(i == max(last - 1, 0))
            def _():
                for s, a, b, c in per_stage:
                    s.mid(a, b, c)

            @pl.when(i == last)
            def _():
                for s, a, b, c in per_stage:
                    s.end(a, b, c)

    res = pl.pallas_call(
        full, name=name, grid=grid,
        in_specs=in_specs + [ANY] * sum(k_in),
        out_specs=out_specs + [ANY] * sum(k_out),
        out_shape=out_shape + [o for s in stages for o in s.out_shape],
        scratch_shapes=scratch_shapes + [x for s in stages for x in s.scratch],
        compiler_params=pltpu.CompilerParams(dimension_semantics=("arbitrary",), vmem_limit_bytes=VMEM_LIMIT,
                                             collective_id=collective_id),
    )(*args, *[a for s in stages for a in s.inputs])
    return list(res[:n_out]), split(list(res[n_out:]), k_out)


def ffn_fwd_a(x, g_pre, w_gu_t, name, stages=()):
    tm, tn = MM_ROWS, 256
    n_w = len(w_gu_t)

    def body(x_ref, gp_ref, *refs):
        w_refs = refs[:n_w]
        n_ref, g_ref, u_ref, a_ref, wt_ref, sem = refs[n_w:]

        @pl.when(pl.program_id(0) == 0)
        def _():
            _load_weight(w_refs, wt_ref, sem)

        xv = x_ref[...]
        n = (xv * _rsqrt_mean_sq(xv) * gp_ref[...]).astype(BF16)
        n_ref[...] = n
        for j in range(D_FF // tn):
            g = _dot_nt(n, wt_ref[j * tn:(j + 1) * tn, :])
            u = _dot_nt(n, wt_ref[D_FF + j * tn:D_FF + (j + 1) * tn, :])
            g_ref[:, j * tn:(j + 1) * tn] = g.astype(BF16)
            u_ref[:, j * tn:(j + 1) * tn] = u.astype(BF16)
            a_ref[:, j * tn:(j + 1) * tn] = (g * _sigmoid(g) * u).astype(BF16)

    wide = pl.BlockSpec((tm, D_FF), lambda i: (i, 0))
    return _call(
        body, name=name, grid=(SEQ // tm,),
        in_specs=[_ROW(tm), _VEC] + [ANY] * n_w,
        out_specs=[_ROW(tm), wide, wide, wide],
        out_shape=[jax.ShapeDtypeStruct((SEQ, D_MODEL), BF16)] + [jax.ShapeDtypeStruct((SEQ, D_FF), BF16)] * 3,
        scratch_shapes=_weight_scratch(2 * D_FF, n_w),
        args=[x, g_pre, *w_gu_t], stages=stages)


def ffn_fwd_b(a, w_down, g_post, h_in, name, target=None, stages=()):
    tm = MM_ROWS
    final = target is not None

    def body(*refs):
        if final:
            a_ref, wf_ref, gp_ref, h_ref, t_ref, f_ref, o_ref, loss_ref, wd_ref, sem = refs
        else:
            a_ref, wf_ref, gp_ref, h_ref, f_ref, o_ref, wd_ref, sem = refs

        @pl.when(pl.program_id(0) == 0)
        def _():
            _load_weight(wf_ref, wd_ref, sem)
            if final:
                loss_ref[...] = jnp.zeros_like(loss_ref)

        f = _dot(a_ref[...], wd_ref[...])
        f_ref[...] = f
        y = h_ref[...] + MACARON * (f * _rsqrt_mean_sq(f) * gp_ref[...])
        if final:
            err = y - t_ref[...]
            o_ref[...] = err * (1.0 / D_MODEL)
            loss_ref[...] += 0.5 * jnp.sum(err * err) * (1.0 / D_MODEL)
        else:
            o_ref[...] = y

    row = _ROW(tm)
    in_specs = [pl.BlockSpec((tm, D_FF), lambda i: (i, 0)), ANY, _VEC, row]
    out_specs = [row, row]
    out_shape = [jax.ShapeDtypeStruct((SEQ, D_MODEL), F32)] * 2
    args = [a, w_down, g_post, h_in]
    if final:
        in_specs.append(row)
        args.append(target)
        out_specs.append(pl.BlockSpec((8, 128), lambda i: (0, 0)))
        out_shape.append(jax.ShapeDtypeStruct((8, 128), F32))
    return _call(body, name=name, grid=(SEQ // tm,), in_specs=in_specs, out_specs=out_specs,
                 out_shape=out_shape, scratch_shapes=_weight_scratch(D_FF), args=args, stages=stages)


def ffn_bwd_a(d_out, f, g_post, w_down, g, u, name, stages=()):
    tm = MM_ROWS
    tc = 256

    def body(do_ref, f_ref, gp_ref, wf_ref, g_ref, u_ref, df_ref, dgu_ref, dgp_ref, wd_ref, sem):
        @pl.when(pl.program_id(0) == 0)
        def _():
            _load_weight(wf_ref, wd_ref, sem)
            dgp_ref[...] = jnp.zeros_like(dgp_ref)

        fv = f_ref[...]
        rf = _rsqrt_mean_sq(fv)
        fh = fv * rf
        dn = MACARON * do_ref[...]
        dgp_ref[...] += jnp.sum(dn * fh, axis=0, keepdims=True)
        t = dn * gp_ref[...]
        df = (rf * (t - fh * jnp.mean(t * fh, axis=-1, keepdims=True))).astype(BF16)
        df_ref[...] = df
        for c0 in range(0, D_FF, tc):
            da = _dot_nt(df, wd_ref[c0:c0 + tc, :])
            gv = g_ref[:, c0:c0 + tc].astype(F32)
            uv = u_ref[:, c0:c0 + tc].astype(F32)
            s = _sigmoid(gv)
            dgu_ref[:, c0:c0 + tc] = (da * uv * s * (1.0 + gv * (1.0 - s))).astype(BF16)
            dgu_ref[:, D_FF + c0:D_FF + c0 + tc] = (da * gv * s).astype(BF16)

    row = _ROW(tm)
    wide = pl.BlockSpec((tm, D_FF), lambda i: (i, 0))
    return _call(
        body, name=name, grid=(SEQ // tm,),
        in_specs=[row, row, _VEC, ANY, wide, wide],
        out_specs=[row, pl.BlockSpec((tm, 2 * D_FF), lambda i: (i, 0)), _VEC],
        out_shape=[jax.ShapeDtypeStruct((SEQ, D_MODEL), BF16), jax.ShapeDtypeStruct((SEQ, 2 * D_FF), BF16),
                   jax.ShapeDtypeStruct((1, D_MODEL), F32)],
        scratch_shapes=_weight_scratch(D_FF),
        args=[d_out, f, g_post, w_down, g, u], stages=stages)


def norm_bwd(pieces, w_t, x, g_pre, d_res, name, stages=()):
    tm = MM_ROWS
    widths = [p.shape[1] for p in pieces]
    offs = [sum(widths[:k]) for k in range(len(widths))]
    n_p = len(pieces)
    n_w = len(w_t)

    def body(*refs):
        p_refs = refs[:n_p]
        w_refs = refs[n_p:n_p + n_w]
        x_ref, g_ref, r_ref, dx_ref, dg_ref, wt_ref, sem = refs[n_p + n_w:]

        @pl.when(pl.program_id(0) == 0)
        def _():
            _load_weight(w_refs, wt_ref, sem)
            dg_ref[...] = jnp.zeros_like(dg_ref)

        dn = None
        for p_ref, lo, wd in zip(p_refs, offs, widths):
            part = _dot(p_ref[...], wt_ref[lo:lo + wd, :])
            dn = part if dn is None else dn + part
        xv = x_ref[...]
        r = _rsqrt_mean_sq(xv)
        xh = xv * r
        dg_ref[...] += jnp.sum(dn * xh, axis=0, keepdims=True)
        t = dn * g_ref[...]
        dx_ref[...] = r_ref[...] + r * (t - xh * jnp.mean(t * xh, axis=-1, keepdims=True))

    row = _ROW(tm)
    return _call(
        body, name=name, grid=(SEQ // tm,),
        in_specs=[pl.BlockSpec((tm, wd), lambda i: (i, 0)) for wd in widths] + [ANY] * n_w + [row, _VEC, row],
        out_specs=[row, _VEC],
        out_shape=[jax.ShapeDtypeStruct((SEQ, D_MODEL), F32), jax.ShapeDtypeStruct((1, D_MODEL), F32)],
        scratch_shapes=_weight_scratch(sum(widths), n_w),
        args=[*pieces, *w_t, x, g_pre, d_res], stages=stages)


def mm_tn(pieces, b, tm, name, stages=()):
    widths = [p.shape[1] for p in pieces]
    m_total = sum(widths)
    n_p = len(pieces)
    starts = [sum(widths[:k]) // tm for k in range(n_p)]
    counts = [wd // tm for wd in widths]

    def body(*refs):
        p_refs = refs[:n_p]
        b_ref, o_ref = refs[n_p:]
        i = pl.program_id(0)
        for p_ref, st, ct in zip(p_refs, starts, counts):
            @pl.when((i >= st) & (i < st + ct))
            def _(p_ref=p_ref):
                o_ref[...] = _dot_tn(p_ref[...], b_ref[...]).astype(BF16)

    def piece_spec(st, ct):
        return pl.BlockSpec((SEQ, tm), lambda i: (0, jnp.clip(i - st, 0, ct - 1)))

    (out,), stage_out = _call(
        body, name=name, grid=(m_total // tm,),
        in_specs=[piece_spec(st, ct) for st, ct in zip(starts, counts)] + [pl.BlockSpec((SEQ, D_MODEL), lambda i: (0, 0))],
        out_specs=[pl.BlockSpec((tm, D_MODEL), lambda i: (i, 0))],
        out_shape=[jax.ShapeDtypeStruct((m_total, D_MODEL), BF16)],
        args=[*pieces, b], stages=stages)
    return out, stage_out


def mix_in(h, g_pre, w_in_t, name, stages=()):
    tm = MM_ROWS
    offs = [sum(IN_SEGS[:k]) for k in range(len(IN_SEGS))]
    dts = [F32, F32, F32, F32, BF16, F32, F32]
    n_o = len(IN_SEGS)
    n_w = len(w_in_t)

    def body(*refs):
        h_ref, g_ref = refs[:2]
        w_refs = refs[2:2 + n_w]
        um_ref = refs[2 + n_w]
        o_refs = refs[3 + n_w:3 + n_w + n_o]
        wt_ref, sem = refs[3 + n_w + n_o:]

        @pl.when(pl.program_id(0) == 0)
        def _():
            _load_weight(w_refs, wt_ref, sem)

        hv = h_ref[...]
        um = (hv * _rsqrt_mean_sq(hv) * g_ref[...]).astype(BF16)
        um_ref[...] = um
        for o_ref, lo, wd in zip(o_refs, offs, IN_SEGS):
            for c0 in range(0, wd, 256):
                o_ref[:, c0:c0 + 256] = _dot_nt(um, wt_ref[lo + c0:lo + c0 + 256, :]).astype(o_ref.dtype)

    return _call(
        body, name=name, grid=(SEQ // tm,),
        in_specs=[_ROW(tm), _VEC] + [ANY] * n_w,
        out_specs=[_ROW(tm)] + [pl.BlockSpec((tm, wd), lambda i: (i, 0)) for wd in IN_SEGS],
        out_shape=[jax.ShapeDtypeStruct((SEQ, D_MODEL), BF16)]
        + [jax.ShapeDtypeStruct((SEQ, wd), dt) for wd, dt in zip(IN_SEGS, dts)],
        scratch_shapes=_weight_scratch(IN_W, n_w),
        args=[h, g_pre, *w_in_t], stages=stages)


LRU_TC = 256


def _conv_fwd(xb, cw, cb, tt):
    xc = xb * cw[3:4, :] + cb
    shifted = []
    for s in (1, 2, 3):
        sh = jnp.where(tt >= s, pltpu.roll(xb, s, 0), 0.0)
        shifted.append(sh)
        xc = xc + sh * cw[3 - s:4 - s, :]
    return xc, shifted


def _lru_gates(xc, wa, ba, wx, bx, lam):
    xcb = xc.astype(BF16)
    r = _sigmoid(_dot(xcb, wa) + ba)
    i = _sigmoid(_dot(xcb, wx) + bx)
    nl = -lam
    sp = jnp.maximum(nl, 0.0) + jnp.log1p(jnp.exp(-jnp.abs(nl)))
    la = (-LRU_C * r) * sp
    a = jnp.exp(la)
    mult = jnp.sqrt(jnp.maximum(-_expm1(2.0 * la), 0.0))
    return xcb, r, i, sp, a, mult


def _scan(a, b, tt, reverse, a_s, b_s):
    n = a.shape[0]
    tg = tt & 7
    for s in (1, 2, 4):
        keep = (tg < 8 - s) if reverse else (tg >= s)
        shift = n - s if reverse else s
        b = a * jnp.where(keep, pltpu.roll(b, shift, 0), 0.0) + b
        a = a * jnp.where(keep, pltpu.roll(a, shift, 0), 1.0)
    a_s[...] = a
    b_s[...] = b
    groups = n // 8

    def step(g, carry):
        gi = (groups - 1 - g) if reverse else g
        rows = pl.ds(pl.multiple_of(gi * 8, 8), 8)
        hg = a_s[rows, :] * carry + b_s[rows, :]
        b_s[rows, :] = hg
        return hg[0:1, :] if reverse else hg[7:8, :]

    lax.fori_loop(0, groups, step, jnp.zeros((1, a.shape[1]), F32), unroll=8)
    return b_s[...]


def _lru_specs():
    col = pl.BlockSpec((SEQ, LRU_TC), lambda j: (0, j))
    vec = pl.BlockSpec((1, LRU_TC), lambda j: (0, j))
    bd = pl.BlockSpec((1, LRU_TC, LRU_TC), lambda j: (j, 0, 0))
    cw = pl.BlockSpec((4, LRU_TC), lambda j: (0, j))
    return col, vec, bd, cw


def lru_fwd(gate, xbr, conv_w, conv_b, wa_bd, b_a, wx_bd, b_x, lam, name, stages=()):
    col, vec, bd, cw = _lru_specs()

    def body(gate_ref, xbr_ref, cw_ref, cb_ref, wa_ref, ba_ref, wx_ref, bx_ref, lam_ref, y_ref, h_ref, a_s, b_s):
        tt = lax.broadcasted_iota(jnp.int32, (SEQ, LRU_TC), 0)
        xc, _ = _conv_fwd(xbr_ref[...], cw_ref[...], cb_ref[...], tt)
        _, r, i, sp, a, mult = _lru_gates(xc, wa_ref[0], ba_ref[...], wx_ref[0], bx_ref[...], lam_ref[...])
        h = _scan(a, mult * (i * xc), tt, False, a_s, b_s)
        h_ref[...] = h
        gl, _ = _gelu(gate_ref[...])
        y_ref[...] = (h * gl).astype(BF16)

    return _call(
        body, name=name, grid=(LRU_W // LRU_TC,),
        in_specs=[col, col, cw, vec, bd, vec, bd, vec, vec],
        out_specs=[col, col],
        out_shape=[jax.ShapeDtypeStruct((SEQ, LRU_W), BF16), jax.ShapeDtypeStruct((SEQ, LRU_W), F32)],
        scratch_shapes=[pltpu.VMEM((SEQ, LRU_TC), F32)] * 2,
        args=[gate, xbr, conv_w, conv_b, wa_bd, b_a, wx_bd, b_x, lam], stages=stages)


def lru_bwd(gate, xbr, h, dy, conv_w, conv_b, wa_bd, b_a, wx_bd, b_x, lam, name, stages=()):
    col, vec, bd, cw = _lru_specs()

    def body(gate_ref, xbr_ref, h_ref, dy_ref, cw_ref, cb_ref, wa_ref, ba_ref, wx_ref, bx_ref, lam_ref,
             dgate_ref, dxbr_ref, vecs_ref, dwa_ref, dwx_ref, a_s, b_s):
        tt = lax.broadcasted_iota(jnp.int32, (SEQ, LRU_TC), 0)
        cwv = cw_ref[...]
        lam = lam_ref[...]
        xb = xbr_ref[...]
        xc, shifted = _conv_fwd(xb, cwv, cb_ref[...], tt)
        wa = wa_ref[0]
        wx = wx_ref[0]
        xcb, r, i, sp, a, mult = _lru_gates(xc, wa, ba_ref[...], wx, bx_ref[...], lam)
        hv = h_ref[...]
        dyv = dy_ref[...]
        gv = gate_ref[...]
        gl, th = _gelu(gv)
        dgate_ref[...] = (dyv * hv * _gelu_grad(gv, th)).astype(BF16)
        a_next = jnp.where(tt < SEQ - 1, pltpu.roll(a, SEQ - 1, 0), 0.0)
        gsum = _scan(a_next, dyv * gl, tt, True, a_s, b_s)
        h_prev = jnp.where(tt >= 1, pltpu.roll(hv, 1, 0), 0.0)
        d_mult = gsum * i * xc
        d_i = gsum * mult * xc
        d_xc = gsum * mult * i
        d_la = gsum * h_prev * a - d_mult * (a * a) / mult
        d_pr = (d_la * (-LRU_C * sp)) * r * (1.0 - r)
        d_pi = d_i * i * (1.0 - i)
        d_lam = jnp.sum(d_la * r, axis=0, keepdims=True) * (LRU_C * _sigmoid(-lam))
        d_prb = d_pr.astype(BF16)
        d_pib = d_pi.astype(BF16)
        d_xc = d_xc + _dot_nt(d_prb, wa) + _dot_nt(d_pib, wx)
        dwa_ref[0] = _dot_tn(xcb, d_prb)
        dwx_ref[0] = _dot_tn(xcb, d_pib)
        rows = [jnp.sum(d_xc * shifted[2], axis=0, keepdims=True),
                jnp.sum(d_xc * shifted[1], axis=0, keepdims=True),
                jnp.sum(d_xc * shifted[0], axis=0, keepdims=True),
                jnp.sum(d_xc * xb, axis=0, keepdims=True),
                jnp.sum(d_xc, axis=0, keepdims=True),
                jnp.sum(d_pr, axis=0, keepdims=True),
                jnp.sum(d_pi, axis=0, keepdims=True),
                d_lam]
        ri = lax.broadcasted_iota(jnp.int32, (8, LRU_TC), 0)
        acc = jnp.zeros((8, LRU_TC), F32)
        for k, rv in enumerate(rows):
            acc = jnp.where(ri == k, rv, acc)
        vecs_ref[...] = acc
        d_xb = d_xc * cwv[3:4, :]
        for s in (1, 2, 3):
            d_xb = d_xb + jnp.where(tt < SEQ - s, pltpu.roll(d_xc, SEQ - s, 0), 0.0) * cwv[3 - s:4 - s, :]
        dxbr_ref[...] = d_xb.astype(BF16)

    return _call(
        body, name=name, grid=(LRU_W // LRU_TC,),
        in_specs=[col, col, col, col, cw, vec, bd, vec, bd, vec, vec],
        out_specs=[col, col, pl.BlockSpec((8, LRU_TC), lambda j: (0, j)), bd, bd],
        out_shape=[jax.ShapeDtypeStruct((SEQ, LRU_W), BF16), jax.ShapeDtypeStruct((SEQ, LRU_W), BF16),
                   jax.ShapeDtypeStruct((8, LRU_W), F32),
                   jax.ShapeDtypeStruct((LRU_W // LRU_TC, LRU_TC, LRU_TC), F32),
                   jax.ShapeDtypeStruct((LRU_W // LRU_TC, LRU_TC, LRU_TC), F32)],
        scratch_shapes=[pltpu.VMEM((SEQ, LRU_TC), F32)] * 2,
        args=[gate, xbr, h, dy, conv_w, conv_b, wa_bd, b_a, wx_bd, b_x, lam], stages=stages)


def _rope(x, cos, sin_signed):
    w = x.shape[1]
    reps = w // 128
    if reps > 1:
        cos = jnp.tile(cos, (1, reps))
        sin_signed = jnp.tile(sin_signed, (1, reps))
    lane = lax.broadcasted_iota(jnp.int32, x.shape, 1)
    first = (lane & 63) < 32
    partner = jnp.where(first, pltpu.roll(x, w - 32, 1), pltpu.roll(x, 32, 1))
    return x * cos + partner * sin_signed


def _both_halves(t, odd):
    lo = lax.broadcasted_iota(jnp.int32, t.shape, 1) < 64
    rolled = pltpu.roll(t, 64, 1)
    return jnp.where(lo, rolled, t) if odd else jnp.where(lo, t, rolled)


def _stack_heads(ta, tb):
    lo = lax.broadcasted_iota(jnp.int32, ta.shape, 1) < 64
    return jnp.concatenate([jnp.where(lo, ta, 0.0), jnp.where(lo, 0.0, ta),
                            jnp.where(lo, tb, 0.0), jnp.where(lo, 0.0, tb)], axis=0)


def _unstack_heads(o):
    lo = lax.broadcasted_iota(jnp.int32, (ATTN_BLOCK, 128), 1) < 64
    return (jnp.where(lo, o[0:128], o[128:256]), jnp.where(lo, o[256:384], o[384:512]))


def _window_upper_t():
    shape = (ATTN_BLOCK, 4 * ATTN_BLOCK)
    return lax.broadcasted_iota(jnp.int32, shape, 0) > (lax.broadcasted_iota(jnp.int32, shape, 1) & (ATTN_BLOCK - 1))


def _fold_t(t, upper_t):
    return jnp.where(upper_t, t[:ATTN_BLOCK], t[ATTN_BLOCK:])


def _unfold_t(t, upper_t):
    zero = jnp.zeros_like(t)
    return jnp.concatenate([jnp.where(upper_t, t, zero), jnp.where(upper_t, zero, t)], axis=0)


def _attn_probs_t(kd, qs, sinks_ref, hk, first_block, upper_t):
    s = _fold_t(_dot_nt(kd, qs), upper_t) * (HEAD_DIM ** -0.5)
    s = jnp.where(jnp.logical_and(upper_t, first_block), MASK_VALUE, s)
    rg = lax.broadcasted_iota(jnp.int32, (1, 4 * ATTN_BLOCK), 1) >> 7
    sink = jnp.where(rg == 0, sinks_ref[4 * hk],
                     jnp.where(rg == 1, sinks_ref[4 * hk + 1],
                               jnp.where(rg == 2, sinks_ref[4 * hk + 2], sinks_ref[4 * hk + 3])))
    m = jnp.maximum(jnp.max(s, axis=0, keepdims=True), sink)
    e = jnp.exp(s - m)
    es = jnp.exp(sink - m)
    inv = 1.0 / (jnp.sum(e, axis=0, keepdims=True) + es)
    return e * inv, es * inv


def _prev(i):
    return jnp.maximum(i - 1, 0)


def attn_fwd(q, k, v, cos, sin_signed, sinks, name, stages=()):
    nb = ATTN_BLOCK

    def body(q_ref, kc_ref, kp_ref, vc_ref, vp_ref, cc_ref, sc_ref, cp_ref, sp_ref, sinks_ref,
             qr_ref, kr_ref, y_ref):
        first_block = pl.program_id(0) == 0
        qr = _rope(q_ref[...], cc_ref[...], sc_ref[...])
        kc = _rope(kc_ref[...], cc_ref[...], sc_ref[...])
        kp = _rope(kp_ref[...], cp_ref[...], sp_ref[...])
        qr_ref[...] = qr.astype(BF16)
        kr_ref[...] = kc.astype(BF16)
        k2 = jnp.concatenate([kp, kc], axis=0)
        v2 = jnp.concatenate([vp_ref[...].astype(F32), vc_ref[...].astype(F32)], axis=0)
        upper_t = _window_upper_t()
        for hk in range(N_KV_HEADS):
            kt = hk // 2
            kd = _both_halves(k2[:, kt * 128:(kt + 1) * 128], hk % 2).astype(BF16)
            vd = _both_halves(v2[:, kt * 128:(kt + 1) * 128], hk % 2).astype(BF16)
            qs = _stack_heads(qr[:, (2 * hk) * 128:(2 * hk + 1) * 128],
                              qr[:, (2 * hk + 1) * 128:(2 * hk + 2) * 128]).astype(BF16)
            p, _ = _attn_probs_t(kd, qs, sinks_ref, hk, first_block, upper_t)
            ta, tb = _unstack_heads(_dot_tn(_unfold_t(p.astype(BF16), upper_t), vd))
            y_ref[:, (2 * hk) * 128:(2 * hk + 1) * 128] = ta.astype(BF16)
            y_ref[:, (2 * hk + 1) * 128:(2 * hk + 2) * 128] = tb.astype(BF16)

    cur = lambda w: pl.BlockSpec((nb, w), lambda i: (i, 0))
    prv = lambda w: pl.BlockSpec((nb, w), lambda i: (_prev(i), 0))
    return _call(
        body, name=name, grid=(N_ATTN_BLOCKS,),
        in_specs=[cur(D_MODEL), cur(KV_W), prv(KV_W), cur(KV_W), prv(KV_W), cur(128), cur(128), prv(128), prv(128),
                  pl.BlockSpec(memory_space=pltpu.SMEM)],
        out_specs=[cur(D_MODEL), cur(KV_W), cur(D_MODEL)],
        out_shape=[jax.ShapeDtypeStruct((SEQ, D_MODEL), BF16), jax.ShapeDtypeStruct((SEQ, KV_W), BF16),
                   jax.ShapeDtypeStruct((SEQ, D_MODEL), BF16)],
        args=[q, k, k, v, v, cos, sin_signed, cos, sin_signed, sinks], stages=stages)


def attn_bwd(qr, kr, v, dy, cos, sin_signed, sinks, name, stages=()):
    nb = ATTN_BLOCK
    n_steps = N_ATTN_BLOCKS + 1
    scale = HEAD_DIM ** -0.5

    def body(q_ref, kc_ref, kp_ref, vc_ref, vp_ref, dy_ref, cc_ref, sc_ref, cp_ref, sp_ref, sinks_ref,
             dq_ref, dkv_ref, dsk_ref, ck_ref, cv_ref):
        dk_ref = dkv_ref.at[:, pl.ds(0, KV_W)]
        dv_ref = dkv_ref.at[:, pl.ds(KV_W, KV_W)]
        i = pl.program_id(0)

        @pl.when(i == 0)
        def _():
            dsk_ref[...] = jnp.zeros_like(dsk_ref)
            ck_ref[...] = jnp.zeros_like(ck_ref)
            cv_ref[...] = jnp.zeros_like(cv_ref)

        @pl.when(i < N_ATTN_BLOCKS)
        def _():
            qv = q_ref[...].astype(F32)
            dov = dy_ref[...].astype(F32)
            k2 = jnp.concatenate([kp_ref[...].astype(F32), kc_ref[...].astype(F32)], axis=0)
            v2 = jnp.concatenate([vp_ref[...].astype(F32), vc_ref[...].astype(F32)], axis=0)
            lane = lax.broadcasted_iota(jnp.int32, (8, 128), 1)
            lo = lax.broadcasted_iota(jnp.int32, (2 * nb, 128), 1) < 64
            dsk = jnp.zeros((8, 128), F32)
            dk_tiles = []
            dv_tiles = []
            upper_t = _window_upper_t()
            for hk in range(N_KV_HEADS):
                kt = hk // 2
                kd = _both_halves(k2[:, kt * 128:(kt + 1) * 128], hk % 2).astype(BF16)
                vd = _both_halves(v2[:, kt * 128:(kt + 1) * 128], hk % 2).astype(BF16)
                qs = _stack_heads(qv[:, (2 * hk) * 128:(2 * hk + 1) * 128],
                                  qv[:, (2 * hk + 1) * 128:(2 * hk + 2) * 128]).astype(BF16)
                dos = _stack_heads(dov[:, (2 * hk) * 128:(2 * hk + 1) * 128],
                                   dov[:, (2 * hk + 1) * 128:(2 * hk + 2) * 128]).astype(BF16)
                p, ps = _attn_probs_t(kd, qs, sinks_ref, hk, i == 0, upper_t)
                dp = _fold_t(_dot_nt(vd, dos), upper_t)
                delta = jnp.sum(p * dp, axis=0, keepdims=True)
                ds = _unfold_t((p * (dp - delta)).astype(BF16), upper_t)
                dsink = -ps * delta
                for g in range(4):
                    dsk = dsk + jnp.where(lane == 4 * hk + g, jnp.sum(dsink[:, g * nb:(g + 1) * nb]), 0.0)
                ta, tb = _unstack_heads(_dot_tn(ds, kd) * scale)
                dq_a = (2 * hk) * 128
                dq_ref[:, dq_a:dq_a + 128] = _rope(ta, cc_ref[...], -sc_ref[...]).astype(BF16)
                dq_ref[:, dq_a + 128:dq_a + 256] = _rope(tb, cc_ref[...], -sc_ref[...]).astype(BF16)
                rk = _dot(ds, qs) * scale
                rv = _dot(_unfold_t(p.astype(BF16), upper_t), dos)
                dk_tiles.append(rk + pltpu.roll(rk, 64, 1))
                dv_tiles.append(rv + pltpu.roll(rv, 64, 1))
            dsk_ref[...] += dsk
            dk_full = jnp.concatenate([jnp.where(lo, dk_tiles[0], dk_tiles[1]),
                                       jnp.where(lo, dk_tiles[2], dk_tiles[3])], axis=1)
            dv_full = jnp.concatenate([jnp.where(lo, dv_tiles[0], dv_tiles[1]),
                                       jnp.where(lo, dv_tiles[2], dv_tiles[3])], axis=1)
            dk_ref[...] = _rope(ck_ref[...] + dk_full[0:nb], cp_ref[...], -sp_ref[...]).astype(BF16)
            dv_ref[...] = (cv_ref[...] + dv_full[0:nb]).astype(BF16)
            ck_ref[...] = dk_full[nb:2 * nb]
            cv_ref[...] = dv_full[nb:2 * nb]

        @pl.when(i == N_ATTN_BLOCKS)
        def _():
            dk_ref[...] = _rope(ck_ref[...], cp_ref[...], -sp_ref[...]).astype(BF16)
            dv_ref[...] = cv_ref[...].astype(BF16)

    qi = lambda i: jnp.minimum(i, N_ATTN_BLOCKS - 1)
    cur = lambda w: pl.BlockSpec((nb, w), lambda i: (qi(i), 0))
    prv = lambda w: pl.BlockSpec((nb, w), lambda i: (_prev(qi(i)), 0))
    out_prev = lambda w: pl.BlockSpec((nb, w), lambda i: (_prev(i), 0))
    return _call(
        body, name=name, grid=(n_steps,),
        in_specs=[cur(D_MODEL), cur(KV_W), prv(KV_W), cur(KV_W), prv(KV_W), cur(D_MODEL),
                  cur(128), cur(128), out_prev(128), out_prev(128), pl.BlockSpec(memory_space=pltpu.SMEM)],
        out_specs=[cur(D_MODEL), out_prev(2 * KV_W), pl.BlockSpec((8, 128), lambda i: (0, 0))],
        out_shape=[jax.ShapeDtypeStruct((SEQ, D_MODEL), BF16), jax.ShapeDtypeStruct((SEQ, 2 * KV_W), BF16),
                   jax.ShapeDtypeStruct((8, 128), F32)],
        scratch_shapes=[pltpu.VMEM((nb, KV_W), F32), pltpu.VMEM((nb, KV_W), F32)],
        args=[qr, kr, kr, v, v, dy, cos, sin_signed, cos, sin_signed, sinks], stages=stages)


def _proj_scratch():
    return [pltpu.VMEM((D_MODEL, D_MODEL), BF16)] * 3 + [pltpu.SemaphoreType.DMA((3 * N_CHIPS,))]


def _load_projs(w_refs, wl_ref, wa_ref, wo_ref, sem):
    for k, (w_ref, dst) in enumerate(zip(w_refs, (wl_ref, wa_ref, wo_ref))):
        _load_weight(w_ref, dst, sem.at[pl.ds(k * N_CHIPS, N_CHIPS)])


def merge_fwd(y_lru, y_attn, g_lru, g_attn, projs, g_post, h_in, name, stages=()):
    tm = MM_ROWS

    def body(yl_ref, ya_ref, gl_ref, ga_ref, w1_ref, w2_ref, w3_ref, gp_ref, h_ref,
             pl_ref, pa_ref, mg_ref, m_ref, o_ref, wl_ref, wa_ref, wo_ref, sem):
        @pl.when(pl.program_id(0) == 0)
        def _():
            _load_projs((w1_ref, w2_ref, w3_ref), wl_ref, wa_ref, wo_ref, sem)

        p_l = _dot(yl_ref[...], wl_ref[...])
        p_a = _dot(ya_ref[...], wa_ref[...])
        pl_ref[...] = p_l.astype(BF16)
        pa_ref[...] = p_a.astype(BF16)
        merged = (_sigmoid(gl_ref[...]) * p_l + _sigmoid(ga_ref[...]) * p_a).astype(BF16)
        mg_ref[...] = merged
        m = _dot(merged, wo_ref[...])
        m_ref[...] = m
        o_ref[...] = h_ref[...] + m * _rsqrt_mean_sq(m) * gp_ref[...]

    row = _ROW(tm)
    return _call(
        body, name=name, grid=(SEQ // tm,),
        in_specs=[row, row, row, row, ANY, ANY, ANY, _VEC, row],
        out_specs=[row] * 5,
        out_shape=[jax.ShapeDtypeStruct((SEQ, D_MODEL), BF16)] * 3 + [jax.ShapeDtypeStruct((SEQ, D_MODEL), F32)] * 2,
        scratch_shapes=_proj_scratch(),
        args=[y_lru, y_attn, g_lru, g_attn, *projs, g_post, h_in], stages=stages)


def merge_bwd(d_out, m, g_post, projs, g_lru, g_attn, p_l, p_a, name, stages=()):
    tm = 256

    def body(do_ref, m_ref, gp_ref, w1_ref, w2_ref, w3_ref, gl_ref, ga_ref, pl_ref, pa_ref,
             dm_ref, dpl_ref, dpa_ref, dgl_ref, dga_ref, dya_ref, dyl_ref, dgp_ref, wl_ref, wa_ref, wo_ref, sem):
        @pl.when(pl.program_id(0) == 0)
        def _():
            _load_projs((w1_ref, w2_ref, w3_ref), wl_ref, wa_ref, wo_ref, sem)
            dgp_ref[...] = jnp.zeros_like(dgp_ref)

        mv = m_ref[...]
        rm = _rsqrt_mean_sq(mv)
        mh = mv * rm
        dn = do_ref[...]
        dgp_ref[...] += jnp.sum(dn * mh, axis=0, keepdims=True)
        t = dn * gp_ref[...]
        dm = (rm * (t - mh * jnp.mean(t * mh, axis=-1, keepdims=True))).astype(BF16)
        dm_ref[...] = dm
        dmg = _dot_nt(dm, wo_ref[...])
        sl = _sigmoid(gl_ref[...])
        sa = _sigmoid(ga_ref[...])
        dpl = (dmg * sl).astype(BF16)
        dpa = (dmg * sa).astype(BF16)
        dpl_ref[...] = dpl
        dpa_ref[...] = dpa
        dgl_ref[...] = (dmg * pl_ref[...].astype(F32) * sl * (1.0 - sl)).astype(BF16)
        dga_ref[...] = (dmg * pa_ref[...].astype(F32) * sa * (1.0 - sa)).astype(BF16)
        dyl_ref[...] = _dot_nt(dpl, wl_ref[...])
        dya_ref[...] = _dot_nt(dpa, wa_ref[...]).astype(BF16)

    row = _ROW(tm)
    return _call(
        body, name=name, grid=(SEQ // tm,),
        in_specs=[row, row, _VEC, ANY, ANY, ANY, row, row, row, row],
        out_specs=[row] * 7 + [_VEC],
        out_shape=[jax.ShapeDtypeStruct((SEQ, D_MODEL), BF16)] * 6 + [jax.ShapeDtypeStruct((SEQ, D_MODEL), F32),
                                                                       jax.ShapeDtypeStruct((1, D_MODEL), F32)],
        scratch_shapes=_proj_scratch(),
        args=[d_out, m, g_post, *projs, g_lru, g_attn, p_l, p_a], stages=stages)


def _rope_tables():
    half = HEAD_DIM // 2
    inv_freq = np.float32(ROPE_THETA) ** (-np.arange(half, dtype=np.float32) / np.float32(half))
    ang = np.arange(SEQ, dtype=np.float32)[:, None] * inv_freq[None, :]
    cos, sin = np.cos(ang), np.sin(ang)
    return (jnp.asarray(np.tile(np.concatenate([cos, cos], axis=1), (1, 2))),
            jnp.asarray(np.tile(np.concatenate([-sin, sin], axis=1), (1, 2))))


def _block_diag(w):
    per = LRU_TC // LRU_BLOCK_W
    w4 = w.reshape(LRU_W // LRU_TC, per, LRU_BLOCK_W, LRU_BLOCK_W)
    eye = jnp.eye(per, dtype=w.dtype)
    return jnp.einsum('jacd,ab->jacbd', w4, eye).reshape(LRU_W // LRU_TC, LRU_TC, LRU_TC).astype(BF16)


def _diag_blocks(p):
    per = LRU_TC // LRU_BLOCK_W
    p5 = p.reshape(LRU_W // LRU_TC, per, LRU_BLOCK_W, per, LRU_BLOCK_W)
    return jnp.stack([p5[:, a, :, a, :] for a in range(per)], axis=1).reshape(LRU_W // LRU_BLOCK_W, LRU_BLOCK_W, LRU_BLOCK_W)


def _place():
    x, y, c = lax.axis_index('x'), lax.axis_index('y'), lax.axis_index('c')
    chips = [(1 - x, y), (x, 1 - y), (1 - x, 1 - y)]
    return x, y, c, chips


def _rcopy(src, dst, send_sem, recv_sem, to):
    return pltpu.make_async_remote_copy(src_ref=src, dst_ref=dst, send_sem=send_sem, recv_sem=recv_sem,
                                        device_id=to, device_id_type=MESH)


class _Stage:
    inputs, out_shape, scratch, peers = (), (), (), ()

    def start(self, ins, outs, scr):
        plan = self._plan(ins, outs, scr)
        for ld in plan['loads']:
            ld.start()
        for cp in plan['sends']:
            cp.start()

    def relay(self, ins, outs, scr):
        pass

    def mid(self, ins, outs, scr):
        plan = self._plan(ins, outs, scr)
        for ld, st in zip(plan['loads'], plan['stores']):
            ld.wait()
            st.start()
        for arrived, onward in zip(plan['arrivals'], plan['forwards']):
            arrived.wait_recv()
            onward.start()

    def end(self, ins, outs, scr):
        plan = self._plan(ins, outs, scr)
        for st in plan['stores']:
            st.wait()
        for arrived in (plan['final_arrivals'] if plan['forwards'] else plan['arrivals']):
            arrived.wait_recv()
        for cp in plan['sends'] + plan['forwards']:
            cp.wait_send()


def _empty_plan():
    return dict(loads=[], stores=[], sends=[], arrivals=[], forwards=[], final_arrivals=[])


class GatherStage(_Stage):
    peers = ('chips', 'sib')
    N_CP = 12

    def __init__(self, items):
        self.ranges = [(off, rows) for _, off, rows in items]
        self.inputs = [src for src, _, _ in items]
        self.out_shape = [jax.ShapeDtypeStruct((N_CHIPS, rows, D_MODEL), BF16) for _, rows in self.ranges]
        n = self.N_CP * len(items)
        self.scratch = [pltpu.VMEM((sum(r for _, r in self.ranges), D_MODEL), BF16), pltpu.SemaphoreType.DMA((n,)),
                        pltpu.SemaphoreType.DMA((n,)), pltpu.SemaphoreType.DMA((2 * len(items),))]

    def _plan(self, ins, outs, scr):
        buf, send, recv, lsem = scr
        x, y, c, _ = _place()
        me_q, q_x, q_y, q_d = 2 * x + y, 2 * (1 - x) + y, 2 * x + (1 - y), 2 * (1 - x) + (1 - y)
        to_x, to_y, sib = (1 - x, y, c), (x, 1 - y, c), (x, y, 1 - c)
        plan = dict(loads=[], stores=[], first=[], early=[], relays=[], late=[], hand_early=[], hand_late=[], final=[])
        boff = 0
        for w, ((off, rows), p_ref, o_ref) in enumerate(zip(self.ranges, ins, outs)):
            hr = rows // 2
            ch = hr // 2
            plan['loads'].append(pltpu.make_async_copy(p_ref.at[pl.ds(off, rows)], buf.at[pl.ds(boff, rows)], lsem.at[2 * w]))
            plan['stores'].append(pltpu.make_async_copy(buf.at[pl.ds(boff, rows)], o_ref.at[me_q], lsem.at[2 * w + 1]))
            boff += rows
            base = w * self.N_CP
            mine = [pl.ds(pl.multiple_of(c * hr + k * ch, 16), ch) for k in range(2)]
            theirs = [pl.ds(pl.multiple_of((1 - c) * hr + k * ch, 16), ch) for k in range(2)]
            src = [p_ref.at[pl.ds(pl.multiple_of(off + c * hr + k * ch, 16), ch)] for k in range(2)]

            def cp(k, s, d, to):
                return _rcopy(s, d, send.at[base + k], recv.at[base + k], to)

            def here(q, rows_):
                return o_ref.at[q, rows_]

            plan['first'] += [cp(0, src[0], here(me_q, mine[0]), to_x), cp(2, src[1], here(me_q, mine[1]), to_y),
                              cp(1, src[1], here(me_q, mine[1]), to_x), cp(3, src[0], here(me_q, mine[0]), to_y)]
            x_a, y_b = here(q_x, mine[0]), here(q_y, mine[1])
            plan['early'] += [cp(0, x_a, x_a, to_x), cp(2, y_b, y_b, to_y)]
            plan['relays'] += [cp(4, x_a, x_a, to_y), cp(5, y_b, y_b, to_x)]
            plan['hand_early'] += [cp(6, x_a, x_a, sib), cp(7, y_b, y_b, sib)]
            x_b, y_a, d_a, d_b = here(q_x, mine[1]), here(q_y, mine[0]), here(q_d, mine[0]), here(q_d, mine[1])
            plan['late'] += [cp(1, x_b, x_b, to_x), cp(3, y_a, y_a, to_y), cp(4, d_a, d_a, to_y), cp(5, d_b, d_b, to_x)]
            plan['hand_late'] += [cp(8, x_b, x_b, sib), cp(9, y_a, y_a, sib), cp(10, d_a, d_a, sib), cp(11, d_b, d_b, sib)]
            for k, (q, piece) in enumerate([(q_x, 0), (q_y, 1), (q_x, 1), (q_y, 0), (q_d, 0), (q_d, 1)]):
                got = here(q, theirs[piece])
                plan['final'].append(cp(6 + k, got, got, sib))
        return plan

    def start(self, ins, outs, scr):
        plan = self._plan(ins, outs, scr)
        for ld in plan['loads']:
            ld.start()
        for cp in plan['first']:
            cp.start()

    def relay(self, ins, outs, scr):
        plan = self._plan(ins, outs, scr)
        for arrived in plan['early']:
            arrived.wait_recv()
        for cp in plan['relays'] + plan['hand_early']:
            cp.start()

    def mid(self, ins, outs, scr):
        plan = self._plan(ins, outs, scr)
        for ld, st in zip(plan['loads'], plan['stores']):
            ld.wait()
            st.start()
        for arrived in plan['late']:
            arrived.wait_recv()
        for cp in plan['hand_late']:
            cp.start()

    def end(self, ins, outs, scr):
        plan = self._plan(ins, outs, scr)
        for st in plan['stores']:
            st.wait()
        for arrived in plan['final']:
            arrived.wait_recv()
        for cp in plan['first'] + plan['relays'] + plan['hand_early'] + plan['hand_late']:
            cp.wait_send()


class PairStage(_Stage):
    peers = ('sib',)

    def __init__(self, grads):
        self.inputs = list(grads)
        self.out_shape = [jax.ShapeDtypeStruct((N_CHIPS, 1) + g.shape[2:], BF16) for g in grads]
        n_cp = N_CHIPS * len(grads)
        self.scratch = [pltpu.SemaphoreType.DMA((n_cp,)), pltpu.SemaphoreType.DMA((n_cp,))]

    def _plan(self, ins, outs, scr):
        send, recv = scr
        x, y, c, _ = _place()
        plan = _empty_plan()
        for w, (g_ref, l_ref) in enumerate(zip(ins, outs)):
            for q in range(N_CHIPS):
                i = w * N_CHIPS + q
                plan['sends'].append(_rcopy(g_ref.at[q, pl.ds(1 - c, 1)], l_ref.at[q], send.at[i], recv.at[i], (x, y, 1 - c)))
        plan['arrivals'] = plan['sends']
        return plan


class ChipStage(_Stage):
    peers = ('chips',)

    def __init__(self, items):
        self.ranges = [(off, n) for _, off, n in items]
        self.inputs = [s for s, _, _ in items]
        self.out_shape = [jax.ShapeDtypeStruct((N_CHIPS, n, D_MODEL), BF16) for _, n in self.ranges]
        n_cp = 3 * len(items)
        self.scratch = [pltpu.VMEM((sum(n for _, n in self.ranges), D_MODEL), BF16), pltpu.SemaphoreType.DMA((n_cp,)),
                        pltpu.SemaphoreType.DMA((n_cp,)), pltpu.SemaphoreType.DMA((2 * len(items),))]

    def _plan(self, ins, outs, scr):
        buf, send, recv, lsem = scr
        x, y, c, chips = _place()
        me_q = 2 * x + y
        plan = _empty_plan()
        boff = 0
        for w, ((off, n), s_ref, l_ref) in enumerate(zip(self.ranges, ins, outs)):
            rows = pl.ds(off, n)
            plan['loads'].append(pltpu.make_async_copy(s_ref.at[me_q, rows], buf.at[pl.ds(boff, n)], lsem.at[2 * w]))
            plan['stores'].append(pltpu.make_async_copy(buf.at[pl.ds(boff, n)], l_ref.at[me_q], lsem.at[2 * w + 1]))
            boff += n
            for j, (cx, cy) in enumerate(chips):
                i = w * 3 + j
                got = l_ref.at[2 * cx + cy]
                plan['sends'].append(_rcopy(s_ref.at[2 * cx + cy, rows], l_ref.at[me_q], send.at[i], recv.at[i], (cx, cy, c)))
                plan['arrivals'].append(_rcopy(got, got, send.at[i], recv.at[i], (cx, cy, c)))
        return plan


class SwapStage(_Stage):
    peers = ('sib',)

    def __init__(self, items):
        n = len(items)
        self.inputs = list(items)
        self.out_shape = [jax.ShapeDtypeStruct((2,) + a.shape, a.dtype) for a in items]
        self.scratch = [pltpu.VMEM(a.shape, a.dtype) for a in items] + [
            pltpu.SemaphoreType.DMA((n,)), pltpu.SemaphoreType.DMA((n,)), pltpu.SemaphoreType.DMA((2 * n,))]

    def _plan(self, ins, outs, scr):
        bufs, (send, recv, lsem) = scr[:len(ins)], scr[len(ins):]
        x, y, c, _ = _place()
        plan = _empty_plan()
        for w, (h_ref, o_ref, buf) in enumerate(zip(ins, outs, bufs)):
            plan['loads'].append(pltpu.make_async_copy(h_ref, buf, lsem.at[2 * w]))
            plan['stores'].append(pltpu.make_async_copy(buf, o_ref.at[c], lsem.at[2 * w + 1]))
            got = o_ref.at[1 - c]
            plan['sends'].append(_rcopy(h_ref, o_ref.at[c], send.at[w], recv.at[w], (x, y, 1 - c)))
            plan['arrivals'].append(_rcopy(got, got, send.at[w], recv.at[w], (x, y, 1 - c)))
        return plan


class SmallGatherStage(_Stage):
    peers = ('chips', 'sib')

    def __init__(self, blk):
        self.inputs = [blk]
        self.out_shape = [jax.ShapeDtypeStruct((N_DEV,) + blk.shape, blk.dtype)]
        self.scratch = [pltpu.VMEM(blk.shape, blk.dtype), pltpu.SemaphoreType.DMA((7,)), pltpu.SemaphoreType.DMA((7,)),
                        pltpu.SemaphoreType.DMA((2,))]

    def _plan(self, ins, outs, scr):
        (x_ref,), (o_ref,), (buf, send, recv, lsem) = ins, outs, scr
        x, y, c, chips = _place()
        sib = (x, y, 1 - c)

        def slot(px, py, pc):
            return o_ref.at[4 * px + 2 * py + pc]

        plan = _empty_plan()
        plan['loads'].append(pltpu.make_async_copy(x_ref, buf, lsem.at[0]))
        plan['stores'].append(pltpu.make_async_copy(buf, slot(x, y, c), lsem.at[1]))
        from_sib = slot(x, y, 1 - c)
        plan['sends'].append(_rcopy(x_ref, slot(x, y, c), send.at[0], recv.at[0], sib))
        plan['final_arrivals'].append(_rcopy(from_sib, from_sib, send.at[0], recv.at[0], sib))
        for j, (cx, cy) in enumerate(chips):
            got, got_sib = slot(cx, cy, c), slot(cx, cy, 1 - c)
            plan['sends'].append(_rcopy(x_ref, slot(x, y, c), send.at[1 + j], recv.at[1 + j], (cx, cy, c)))
            plan['arrivals'].append(_rcopy(got, got, send.at[1 + j], recv.at[1 + j], (cx, cy, c)))
            plan['forwards'].append(_rcopy(got, got, send.at[4 + j], recv.at[4 + j], sib))
            plan['final_arrivals'].append(_rcopy(got_sib, got_sib, send.at[4 + j], recv.at[4 + j], sib))
        return plan


_HBM = pl.BlockSpec(memory_space=pltpu.HBM)
_SEM = pl.BlockSpec(memory_space=pltpu.SEMAPHORE)
_DATAFLOW = pltpu.CompilerParams(has_side_effects=pltpu.SideEffectType.DATAFLOW_SIDE_EFFECTING)


def chip_exchange_start(s):
    def body(s_ref, land_ref, send, recv, s_thru, land_thru, token):
        x, y, c, chips = _place()
        for j, (cx, cy) in enumerate(chips):
            _rcopy(s_ref.at[2 * cx + cy], land_ref.at[2 * x + y], send.at[j], recv.at[j], (cx, cy, c)).start()
        token[...] = jnp.zeros_like(token)

    return pl.pallas_call(
        body, name='chip_exchange_start',
        out_shape=(pltpu.SemaphoreType.DMA((3,)), pltpu.SemaphoreType.DMA((3,)), pltpu.HBM(s.shape, s.dtype),
                   pltpu.HBM(s.shape, s.dtype), jax.ShapeDtypeStruct((8, 128), F32)),
        in_specs=(_HBM, _HBM), out_specs=(_SEM, _SEM, _HBM, _HBM, pl.BlockSpec(memory_space=pltpu.VMEM)),
        input_output_aliases={0: 2, 1: 3}, compiler_params=_DATAFLOW,
    )(pltpu.with_memory_space_constraint(s, pltpu.HBM),
      pltpu.with_memory_space_constraint(lax.empty(s.shape, s.dtype), pltpu.HBM))


def chip_exchange_wait(send, recv, s_thru, land_thru, after):
    def body(s_ref, land_ref, send_sem, recv_sem, after_ref, s_out, land_out):
        x, y, c, chips = _place()
        for j, (cx, cy) in enumerate(chips):
            cp = _rcopy(s_ref.at[2 * cx + cy], land_ref.at[2 * cx + cy], send_sem.at[j], recv_sem.at[j], (cx, cy, c))
            cp.wait_send()
            cp.wait_recv()

    return pl.pallas_call(
        body, name='chip_exchange_wait',
        out_shape=(pltpu.HBM(s_thru.shape, s_thru.dtype), pltpu.HBM(land_thru.shape, land_thru.dtype)),
        in_specs=(_HBM, _HBM, _SEM, _SEM, ANY), out_specs=(_HBM, _HBM),
        input_output_aliases={0: 0, 1: 1}, compiler_params=_DATAFLOW,
    )(s_thru, land_thru, send, recv, after)


def comm_call(name, stages):
    def body():
        pass

    return _call(body, name=name, grid=(1,), in_specs=[], out_specs=[], out_shape=[], args=[], stages=stages)[1]


def pair_sum(g4, land, c_arr, name):
    hr = g4.shape[2]

    def body(c_ref, g_ref, l_ref, o_ref):
        o_ref[0] = (g_ref[0, 0].astype(F32) + l_ref[0, 0].astype(F32)).astype(BF16)

    return pl.pallas_call(
        body, name=name,
        grid_spec=pltpu.PrefetchScalarGridSpec(
            num_scalar_prefetch=1, grid=(N_CHIPS,),
            in_specs=[pl.BlockSpec((1, 1, hr, D_MODEL), lambda q, c: (q, c[0], 0, 0)),
                      pl.BlockSpec((1, 1, hr, D_MODEL), lambda q, c: (q, 0, 0, 0))],
            out_specs=pl.BlockSpec((1, hr, D_MODEL), lambda q, c: (q, 0, 0))),
        out_shape=jax.ShapeDtypeStruct((N_CHIPS, hr, D_MODEL), BF16),
        compiler_params=_params(1),
    )(c_arr, g4, land)


def small_sum(vec_parts, lru_parts):
    def body(v_ref, l_ref, o_ref):
        for p_ref, lo, n in ((v_ref, 0, ROW_WA), (l_ref, ROW_WA, SMALL_ROWS - ROW_WA)):
            acc = p_ref[0]
            for s in range(1, N_DEV):
                acc = acc + p_ref[s]
            o_ref[lo:lo + n, :] = acc

    return pl.pallas_call(
        body, name='small_sum', grid=(1,),
        in_specs=[pl.BlockSpec(vec_parts.shape, lambda i: (0, 0, 0)), pl.BlockSpec(lru_parts.shape, lambda i: (0, 0, 0))],
        out_specs=pl.BlockSpec((SMALL_ROWS, D_MODEL), lambda i: (0, 0)),
        out_shape=jax.ShapeDtypeStruct((SMALL_ROWS, D_MODEL), F32),
        compiler_params=_params(1),
    )(vec_parts, lru_parts)


def _adam_math(w, g, m, v):
    m2 = ADAM_B1 * m + (1.0 - ADAM_B1) * g
    v2 = ADAM_B2 * v + (1.0 - ADAM_B2) * (g * g)
    m_hat = m2 / (1.0 - ADAM_B1 ** ADAM_STEP)
    v_hat = v2 / (1.0 - ADAM_B2 ** ADAM_STEP)
    delta = -ADAM_LR * (m_hat / (jnp.sqrt(v_hat) + ADAM_EPS) + ADAM_WD * w)
    return delta, m2, v2


def _adam_body(n_parts, transposed, n_after):
    def body(*refs):
        refs = refs[n_after:]
        g_refs = refs[:n_parts]
        w_ref, m_ref, v_ref, go_ref, d_ref, mo_ref, vo_ref = refs[n_parts:]
        def chips_added(blk):
            acc = blk[0].astype(F32)
            for s in range(1, N_CHIPS):
                acc = acc + blk[s].astype(F32)
            return acc

        if transposed:
            g = jnp.concatenate([chips_added(g_ref[h]) for h in range(2) for g_ref in g_refs], axis=0).T
        else:
            rows = [chips_added(g_ref[0]) for g_ref in g_refs]
            g = jnp.concatenate(rows, axis=0) if n_parts > 1 else rows[0]
        go_ref[...] = g
        d_ref[...], mo_ref[...], vo_ref[...] = _adam_math(w_ref[...], g, m_ref[...], v_ref[...])
    return body


def adam_rows(fulls, name, w, m, v, after=()):
    hr = w.shape[0] // 2
    blk = pl.BlockSpec((hr, D_MODEL), lambda h: (h, 0))
    return pl.pallas_call(
        _adam_body(len(fulls), False, len(after)), name='adam_' + name, grid=(2,),
        in_specs=[ANY] * len(after)
        + [pl.BlockSpec((1, N_CHIPS, f.shape[2], D_MODEL), lambda h: (h, 0, 0, 0)) for f in fulls] + [blk, blk, blk],
        out_specs=[blk] * 4,
        out_shape=[jax.ShapeDtypeStruct(w.shape, F32)] * 4,
        compiler_params=_params(1),
    )(*after, *fulls, w, m, v)


def adam_cols(fulls, name, w, m, v, after=()):
    cols = w.shape[1]
    tr = 128
    blk = pl.BlockSpec((tr, cols), lambda i: (i, 0))
    return pl.pallas_call(
        _adam_body(len(fulls), True, len(after)), name='adam_' + name, grid=(D_MODEL // tr,),
        in_specs=[ANY] * len(after)
        + [pl.BlockSpec((2, N_CHIPS, f.shape[2], tr), lambda i: (0, 0, 0, i)) for f in fulls] + [blk, blk, blk],
        out_specs=[blk] * 4,
        out_shape=[jax.ShapeDtypeStruct(w.shape, F32)] * 4,
        compiler_params=_params(1),
    )(*after, *fulls, w, m, v)


def adam_small(g, w, m, v):
    def body(g_ref, w_ref, m_ref, v_ref, d_ref, mo_ref, vo_ref):
        d_ref[...], mo_ref[...], vo_ref[...] = _adam_math(w_ref[...], g_ref[...], m_ref[...], v_ref[...])

    blk = pl.BlockSpec(w.shape, lambda i: (0, 0))
    return pl.pallas_call(
        body, name='adam_small', grid=(1,), in_specs=[blk] * 4, out_specs=[blk] * 3,
        out_shape=[jax.ShapeDtypeStruct(w.shape, F32)] * 3, compiler_params=_params(1),
    )(g, w, m, v)


WEIGHTS = ('ffn1_pre_g', 'ffn1_w_gu', 'ffn1_w_down', 'ffn1_post_g', 'mix_pre_g', 'w_in', 'conv_w', 'conv_b',
           'lru_w_a', 'lru_b_a', 'lru_w_x', 'lru_b_x', 'lru_lambda', 'attn_sinks', 'w_proj_lru', 'w_proj_attn',
           'w_out', 'mix_post_g', 'ffn2_pre_g', 'ffn2_w_gu', 'ffn2_w_down', 'ffn2_post_g')
SMALL = tuple(n for n in WEIGHTS if n not in PACK_OFF)


def cast_t(w, name, stages=()):
    cols = w.shape[1]
    tc = 128

    def body(w_ref, o_ref):
        o_ref[...] = w_ref[...].T.astype(BF16)

    (out,), stage_out = _call(
        body, name=name, grid=(cols // tc,),
        in_specs=[pl.BlockSpec((D_MODEL, tc), lambda j: (0, j))],
        out_specs=[pl.BlockSpec((tc, D_MODEL), lambda j: (j, 0))],
        out_shape=[jax.ShapeDtypeStruct((cols, D_MODEL), BF16)],
        args=[w], stages=stages)
    return out, stage_out


def _pack_vecs(d, conv_rows):
    sinks = jnp.pad(d['attn_sinks'].reshape(1, N_Q_HEADS), ((0, 0), (0, D_MODEL - N_Q_HEADS)))
    conv = jnp.pad(conv_rows, ((0, ROW_WA - ROW_CONV - conv_rows.shape[0]), (0, 0)))
    return jnp.concatenate([d[n].reshape(1, D_MODEL) for n in SMALL_VECS] + [sinks, conv], axis=0)


def _pack_lru(d):
    return jnp.concatenate([d['lru_w_a'].reshape(64, D_MODEL), d['lru_w_x'].reshape(64, D_MODEL)], axis=0)


def _pack_small(d, conv_rows):
    return jnp.concatenate([_pack_vecs(d, conv_rows), _pack_lru(d)], axis=0)


def _unpack_small(p, shapes):
    out = {n: p[k:k + 1].reshape(shapes[n]) for k, n in enumerate(SMALL_VECS)}
    out['attn_sinks'] = p[ROW_SINKS:ROW_SINKS + 1, :N_Q_HEADS].reshape(shapes['attn_sinks'])
    out['conv_w'] = p[ROW_CONV:ROW_CONV + 1].reshape(shapes['conv_w'])
    out['lru_w_a'] = p[ROW_WA:ROW_WA + 64].reshape(shapes['lru_w_a'])
    out['lru_w_x'] = p[ROW_WX:ROW_WX + 64].reshape(shapes['lru_w_x'])
    return out


def kernel(x, ffn1_pre_g, ffn1_w_gu, ffn1_w_down, ffn1_post_g, mix_pre_g, w_in, conv_w, conv_b, lru_w_a, lru_b_a, lru_w_x, lru_b_x, lru_lambda, attn_sinks, w_proj_lru, w_proj_attn, w_out, mix_post_g, ffn2_pre_g, ffn2_w_gu, ffn2_w_down, ffn2_post_g, loss_target, m_ffn1_pre_g, m_ffn1_w_gu, m_ffn1_w_down, m_ffn1_post_g, m_mix_pre_g, m_w_in, m_conv_w, m_conv_b, m_lru_w_a, m_lru_b_a, m_lru_w_x, m_lru_b_x, m_lru_lambda, m_attn_sinks, m_w_proj_lru, m_w_proj_attn, m_w_out, m_mix_post_g, m_ffn2_pre_g, m_ffn2_w_gu, m_ffn2_w_down, m_ffn2_post_g, v_ffn1_pre_g, v_ffn1_w_gu, v_ffn1_w_down, v_ffn1_post_g, v_mix_pre_g, v_w_in, v_conv_w, v_conv_b, v_lru_w_a, v_lru_b_a, v_lru_w_x, v_lru_b_x, v_lru_lambda, v_attn_sinks, v_w_proj_lru, v_w_proj_attn, v_w_out, v_mix_post_g, v_ffn2_pre_g, v_ffn2_w_gu, v_ffn2_w_down, v_ffn2_post_g):
    given = dict(locals())
    w = {n: given[n] for n in WEIGHTS}
    mom = {n: given['m_' + n] for n in WEIGHTS}
    var = {n: given['v_' + n] for n in WEIGHTS}
    shapes = {n: w[n].shape for n in WEIGHTS}
    xq = lax.axis_index('x')
    yq = lax.axis_index('y')
    cq = lax.axis_index('c')
    me_q = 2 * xq + yq

    c_arr = cq.reshape(1).astype(jnp.int32)
    xs, target = x[0], loss_target[0]
    sw = {n: (w[n][0] if w[n].ndim > 2 else w[n]) for n in SMALL}
    cos, sin_signed = _rope_tables()
    wa_bd = _block_diag(sw['lru_w_a'])
    wx_bd = _block_diag(sw['lru_w_x'])
    sinks = sw['attn_sinks'].reshape(N_Q_HEADS)

    shard = {n: w[n][0].astype(BF16) for n, _, t in PACK if not t}
    conv_pad = jnp.pad(w['conv_w'][0], ((0, 4), (0, 0)))

    def whole(name):
        return (shard[name], 0, PACK_ROWS_OF[name])

    def part(name, p, n_parts=2):
        rows = PACK_ROWS_OF[name] // n_parts
        return (shard[name], p * rows, rows)

    shard['ffn1_w_gu'], _ = cast_t(w['ffn1_w_gu'][0], 'cast_ffn1_w_gu')
    shard['w_in'], ((w_gu1a,), (conv_all,)) = cast_t(
        w['w_in'][0], 'cast_w_in', stages=[GatherStage([part('ffn1_w_gu', 0)]), SmallGatherStage(conv_pad)])
    shard['ffn2_w_gu'], ((w_gu1b,),) = cast_t(w['ffn2_w_gu'][0], 'cast_ffn2_w_gu', stages=[GatherStage([part('ffn1_w_gu', 1)])])
    w_gu1 = [w_gu1a, w_gu1b]
    sw['conv_w'] = jnp.transpose(conv_all[0::2, :4, :], (1, 0, 2)).reshape(4, LRU_W)
    proj_names = ['w_proj_lru', 'w_proj_attn', 'w_out']

    (n1, g1, u1, a1), ((w_down1, w_in_a),) = ffn_fwd_a(xs, sw['ffn1_pre_g'], w_gu1, 'ffn1_fwd_a',
                                                        stages=[GatherStage([whole('ffn1_w_down'), part('w_in', 0)])])
    (f1, h1), ((w_in_b,),) = ffn_fwd_b(a1, w_down1, sw['ffn1_post_g'], xs, 'ffn1_fwd_b',
                                       stages=[GatherStage([part('w_in', 1)])])
    w_in_t = [w_in_a, w_in_b]
    (um, gate, xbr, q, k, v, g_lru, g_attn), ((w_gu2a,),) = mix_in(h1, sw['mix_pre_g'], w_in_t, 'mix_in',
                                                                   stages=[GatherStage([part('ffn2_w_gu', 0)])])
    (y_lru, h_lru), ((w_gu2b,),) = lru_fwd(gate, xbr, sw['conv_w'], sw['conv_b'], wa_bd, sw['lru_b_a'], wx_bd, sw['lru_b_x'],
                                           sw['lru_lambda'], 'lru_fwd', stages=[GatherStage([part('ffn2_w_gu', 1)])])
    (qr, kr, y_attn), (projs,) = attn_fwd(q, k, v, cos, sin_signed, sinks, 'attn_fwd',
                                          stages=[GatherStage([whole(n) for n in proj_names])])
    (p_l, p_a, merged, m, h2), ((w_down2,),) = merge_fwd(y_lru, y_attn, g_lru, g_attn, projs, sw['mix_post_g'], h1, 'merge_fwd',
                                                         stages=[GatherStage([whole('ffn2_w_down')])])
    w_gu2 = [w_gu2a, w_gu2b]
    (n2, g2, u2, a2), _ = ffn_fwd_a(h2, sw['ffn2_pre_g'], w_gu2, 'ffn2_fwd_a')
    (f2, dy, loss_blk), _ = ffn_fwd_b(a2, w_down2, sw['ffn2_post_g'], h2, 'ffn2_fwd_b', target=target)

    gs, full = {}, {}

    def pair_stage(names, grads):
        g4 = [g.reshape(N_CHIPS, 2, PACK_ROWS_OF[n] // 2, D_MODEL) for n, g in zip(names, grads)]
        return PairStage(g4), g4

    def pair_sums(names, g4, lands):
        return [pair_sum(g, l, c_arr, 'pair_sum_' + n) for n, g, l in zip(names, g4, lands)]

    def halves(s, n_parts=2):
        n = s.shape[1] // n_parts
        return [(s, p * n, n) for p in range(n_parts)]

    (df2, dgu2, gs['ffn2_post_g']), _ = ffn_bwd_a(dy, f2, sw['ffn2_post_g'], w_down2, g2, u2, 'ffn2_bwd_a')
    g_down2, _ = mm_tn([a2], df2, 1408, 'ffn2_dw_down')
    st, g4 = pair_stage(['ffn2_w_down'], [g_down2])
    g_gu2, (lands,) = mm_tn([dgu2], n2, 1408, 'ffn2_dw_gu', stages=[st])
    (s_down2,) = pair_sums(['ffn2_w_down'], g4, lands)
    st, g4 = pair_stage(['ffn2_w_gu'], [g_gu2])
    (dh2, gs['ffn2_pre_g']), ((l_down2,), lands) = norm_bwd([dgu2], w_gu2, h2, sw['ffn2_pre_g'], dy, 'ffn2_bwd_b',
                                                            stages=[ChipStage([(s_down2, 0, s_down2.shape[1])]), st])
    (s_gu2,) = pair_sums(['ffn2_w_gu'], g4, lands)

    (dm, dpl, dpa, dgl, dga, dya, dyl, gs['mix_post_g']), ((l_gu2a,),) = merge_bwd(
        dh2, m, sw['mix_post_g'], projs, g_lru, g_attn, p_l, p_a, 'merge_bwd', stages=[ChipStage(halves(s_gu2)[:1])])
    g_projs = [mm_tn([merged if n == 'w_out' else (y_lru if n == 'w_proj_lru' else y_attn)],
                     dm if n == 'w_out' else (dpl if n == 'w_proj_lru' else dpa), D_MODEL, 'd' + n)[0] for n in proj_names]
    st, g4 = pair_stage(proj_names, g_projs)
    (dq, dkv, dsk), ((l_gu2b,), lands, (full['ffn2_w_down'],)) = attn_bwd(
        qr, kr, v, dya, cos, sin_signed, sinks, 'attn_bwd', stages=[ChipStage(halves(s_gu2)[1:]), st, SwapStage([l_down2])])
    full['ffn2_w_down'] = [full['ffn2_w_down']]
    gs['attn_sinks'] = dsk[0:1, 0:N_Q_HEADS]
    s_projs = pair_sums(proj_names, g4, lands)
    (dgate, dxbr, vecs, dwa, dwx), (l_projs, full['ffn2_w_gu']) = lru_bwd(
        gate, xbr, h_lru, dyl, sw['conv_w'], sw['conv_b'], wa_bd, sw['lru_b_a'], wx_bd, sw['lru_b_x'], sw['lru_lambda'],
        'lru_bwd', stages=[ChipStage([(s, 0, s.shape[1]) for s in s_projs]), SwapStage([l_gu2a, l_gu2b])])
    gs['conv_w'] = vecs[0:4]
    gs['conv_b'], gs['lru_b_a'], gs['lru_b_x'], gs['lru_lambda'] = vecs[4:5], vecs[5:6], vecs[6:7], vecs[7:8]
    gs['lru_w_a'] = _diag_blocks(dwa)
    gs['lru_w_x'] = _diag_blocks(dwx)
    dz = [dgate, dxbr, dq, dkv, dgl, dga]
    g_in, ((lru_all,),) = mm_tn(dz, um, 512, 'dw_in', stages=[SmallGatherStage(_pack_lru(gs))])
    st, g4 = pair_stage(['w_in'], [g_in])
    (dh1, gs['mix_pre_g']), (lands, f_projs) = norm_bwd(dz, w_in_t, h1, sw['mix_pre_g'], dh2, 'mix_bwd_in',
                                                        stages=[st, SwapStage(l_projs)])
    for n, f in zip(proj_names, f_projs):
        full[n] = [f]
    (s_in,) = pair_sums(['w_in'], g4, lands)

    (df1, dgu1, gs['ffn1_post_g']), ((l_in_a,),) = ffn_bwd_a(dh1, f1, sw['ffn1_post_g'], w_down1, g1, u1, 'ffn1_bwd_a',
                                                             stages=[ChipStage(halves(s_in)[:1])])
    g_down1, _ = mm_tn([a1], df1, 1408, 'ffn1_dw_down')
    st, g4 = pair_stage(['ffn1_w_down'], [g_down1])
    g_gu1, ((l_in_b,), lands) = mm_tn([dgu1], n1, 1408, 'ffn1_dw_gu', stages=[ChipStage(halves(s_in)[1:]), st])
    (s_down1,) = pair_sums(['ffn1_w_down'], g4, lands)
    st, g4 = pair_stage(['ffn1_w_gu'], [g_gu1])
    (dx, gs['ffn1_pre_g']), ((l_down1,), lands, full['w_in']) = norm_bwd(
        [dgu1], w_gu1, xs, sw['ffn1_pre_g'], dh1, 'ffn1_bwd_b',
        stages=[ChipStage([(s_down1, 0, s_down1.shape[1])]), st, SwapStage([l_in_a, l_in_b])])
    (s_gu1,) = pair_sums(['ffn1_w_gu'], g4, lands)
    loss_row = jnp.pad(loss_blk[0:1], ((0, 0), (0, D_MODEL - loss_blk.shape[1])))
    vec_blk = _pack_vecs(gs, jnp.concatenate([gs['conv_w'], loss_row], axis=0))
    send, recv, s_thru, land_thru, token = chip_exchange_start(s_gu1)
    out_g, out_d, out_m, out_v = {}, {}, {}, {}

    def adam(n, after=()):
        fn = adam_cols if dict((k, t) for k, _, t in PACK)[n] else adam_rows
        g_, d_, m_, v_ = fn(full[n], n, w[n][0], mom[n][0], var[n][0], after=after)
        out_g[n], out_d[n], out_m[n], out_v[n] = g_[None], d_[None], m_[None], v_[None]

    behind = token
    for n in ['ffn2_w_gu', 'w_in', 'ffn2_w_down'] + proj_names:
        adam(n, after=(behind,))
        behind = out_v[n]
    s_back, l_gu1 = chip_exchange_wait(send, recv, s_thru, land_thru, after=behind)
    own = lax.dynamic_slice_in_dim(s_back, me_q, 1, axis=0)
    l_gu1 = lax.dynamic_update_slice_in_dim(l_gu1, own, me_q, axis=0)
    (vec_all,), (f_down1, f_gu1) = comm_call('swap_last', [SmallGatherStage(vec_blk), SwapStage([l_down1, l_gu1])])
    full['ffn1_w_down'] = [f_down1]
    full['ffn1_w_gu'] = [f_gu1]
    adam('ffn1_w_gu')
    adam('ffn1_w_down')

    tot = small_sum(vec_all, lru_all)
    loss = tot[ROW_WA - 1, 0]
    conv_g = lax.dynamic_slice(tot[ROW_CONV:ROW_CONV + 4], (0, me_q * (LRU_W // N_CHIPS)), (4, LRU_W // N_CHIPS))
    small_g = _unpack_small(tot, shapes)
    small_g['conv_w'] = conv_g.reshape(shapes['conv_w'])
    g_pack = jnp.concatenate([tot[:ROW_CONV], conv_g.reshape(1, D_MODEL), jnp.zeros((ROW_WA - ROW_CONV - 1, D_MODEL), F32),
                              tot[ROW_WA:]], axis=0)
    packs = [_pack_small({n: d[n] for n in SMALL}, d['conv_w'].reshape(1, D_MODEL)) for d in (w, mom, var)]
    d_p, m_p, v_p = adam_small(g_pack, *packs)
    for n in SMALL:
        out_g[n] = small_g[n]
    for dst, p in ((out_d, d_p), (out_m, m_p), (out_v, v_p)):
        dst.update(_unpack_small(p, shapes))

    return (loss, dx[None], *[out_g[n] for n in WEIGHTS], *[out_d[n] for n in WEIGHTS],
            *[out_m[n] for n in WEIGHTS], *[out_v[n] for n in WEIGHTS])
```

```python
import jax
import jax.numpy as jnp
import numpy as np
from jax import lax
from jax.experimental import pallas as pl
from jax.experimental.pallas import tpu as pltpu

F32 = jnp.float32
BF16 = jnp.bfloat16

SEQ = 2048
D_MODEL = 1024
D_FF = 2816
LRU_W = 1024
LRU_BLOCK_W = 64
HEAD_DIM = 64
N_Q_HEADS = 16
N_KV_HEADS = 4
KV_W = N_KV_HEADS * HEAD_DIM
ATTN_BLOCK = 128
N_ATTN_BLOCKS = SEQ // ATTN_BLOCK
IN_SEGS = (1024, 1024, 1024, 256, 256, 1024, 1024)
IN_W = sum(IN_SEGS)
NORM_EPS = 1e-6
MASK_VALUE = -1e30
ROPE_THETA = 10000.0
LRU_C = 8.0
MACARON = 0.5
ADAM_LR = 0.001
ADAM_B1 = 0.9
ADAM_B2 = 0.999
ADAM_EPS = 1e-08
ADAM_WD = 0.01
ADAM_STEP = 10

N_CHIPS = 4
N_DEV = 8
VMEM_LIMIT = 56 * 1024 * 1024
MM_ROWS = 256
MESH = pl.DeviceIdType.MESH
ANY = pl.BlockSpec(memory_space=pl.ANY)

PACK = (('ffn1_w_gu', 1408, True), ('w_in', 1408, True), ('ffn2_w_gu', 1408, True),
        ('ffn1_w_down', 704, False), ('ffn2_w_down', 704, False),
        ('w_proj_lru', 256, False), ('w_proj_attn', 256, False), ('w_out', 256, False))
PACK_ROWS_OF = {n: r for n, r, _ in PACK}
PACK_OFF = {}
_o = 0
for _n, _r, _t in PACK:
    PACK_OFF[_n] = _o
    _o += _r

SMALL_VECS = ('ffn1_pre_g', 'ffn1_post_g', 'mix_pre_g', 'conv_b', 'lru_b_a', 'lru_b_x', 'lru_lambda',
              'mix_post_g', 'ffn2_pre_g', 'ffn2_post_g')
SMALL_ROWS = 144
ROW_SINKS, ROW_CONV, ROW_WA, ROW_WX = 10, 11, 16, 80


def _dot(a, b):
    return jnp.dot(a, b, preferred_element_type=F32)


def _dot_nt(a, b):
    return lax.dot_general(a, b, (((1,), (1,)), ((), ())), preferred_element_type=F32)


def _dot_tn(a, b):
    return lax.dot_general(a, b, (((0,), (0,)), ((), ())), preferred_element_type=F32)


def _params(n_grid):
    return pltpu.CompilerParams(dimension_semantics=("arbitrary",) * n_grid, vmem_limit_bytes=VMEM_LIMIT)


def _sigmoid(x):
    return 1.0 / (1.0 + jnp.exp(-x))


def _rsqrt_mean_sq(x):
    return lax.rsqrt(jnp.mean(x * x, axis=-1, keepdims=True) + NORM_EPS)


def _expm1(x):
    poly = x * (1.0 + x * (0.5 + x * (1.0 / 6.0)))
    return jnp.where(jnp.abs(x) < 0.02, poly, jnp.exp(x) - 1.0)


_GELU_K = 0.7978845608028654
_GELU_C = 0.044715


def _gelu(x):
    t = jnp.tanh(_GELU_K * (x + _GELU_C * x * x * x))
    return 0.5 * x * (1.0 + t), t


def _gelu_grad(x, t):
    return 0.5 * (1.0 + t) + 0.5 * x * (1.0 - t * t) * _GELU_K * (1.0 + 3.0 * _GELU_C * x * x)


def _load_weight(w_refs, dst_ref, sem):
    w_refs = list(w_refs) if isinstance(w_refs, (list, tuple)) else [w_refs]
    rows = dst_ref.shape[0] // N_CHIPS
    rp = rows // len(w_refs)
    cps = [pltpu.make_async_copy(w_ref.at[q], dst_ref.at[pl.ds(q * rows + p * rp, rp)], sem.at[p * N_CHIPS + q])
           for p, w_ref in enumerate(w_refs) for q in range(N_CHIPS)]
    for cp in cps:
        cp.start()
    for cp in cps:
        cp.wait()


def _weight_scratch(rows_total, parts=1):
    return [pltpu.VMEM((rows_total, D_MODEL), BF16), pltpu.SemaphoreType.DMA((N_CHIPS * parts,))]


_ROW = lambda tm: pl.BlockSpec((tm, D_MODEL), lambda i: (i, 0))
_VEC = pl.BlockSpec((1, D_MODEL), lambda i: (0, 0))


def _call(body, *, name, grid, in_specs, out_specs, out_shape, args, scratch_shapes=(), stages=()):
    in_specs, out_specs, out_shape, scratch_shapes = list(in_specs), list(out_specs), list(out_shape), list(scratch_shapes)
    n_in, n_out, n_sc = len(in_specs), len(out_specs), len(scratch_shapes)
    k_in = [len(s.inputs) for s in stages]
    k_out = [len(s.out_shape) for s in stages]
    k_sc = [len(s.scratch) for s in stages]
    last = grid[0] - 1

    def split(refs, counts):
        parts, pos = [], 0
        for k in counts:
            parts.append(refs[pos:pos + k])
            pos += k
        return parts

    kinds = tuple(sorted({k for s in stages for k in s.peers}))
    collective_id = {(): None, ('sib',): 0, ('chips',): 1, ('chips', 'sib'): 2}[kinds]

    def full(*refs):
        ins, s_ins, outs, s_outs, scr, s_scr = split(refs, [n_in, sum(k_in), n_out, sum(k_out), n_sc, sum(k_sc)])
        per_stage = list(zip(stages, split(s_ins, k_in), split(s_outs, k_out), split(s_scr, k_sc)))
        i = pl.program_id(0)
        if stages:
            @pl.when(i == 0)
            def _():
                x, y, c, chips = _place()
                peers = ([(x, y, 1 - c)] if 'sib' in kinds else []) + ([(cx, cy, c) for cx, cy in chips] if 'chips' in kinds else [])
                barrier = pltpu.get_barrier_semaphore()
                for peer in peers:
                    pl.semaphore_signal(barrier, inc=1, device_id=peer, device_id_type=MESH)
                pl.semaphore_wait(barrier, len(peers))
                for s, a, b, c_ in per_stage:
                    s.start(a, b, c_)

        body(*ins, *outs, *scr)
        if stages:
            @pl.when(i == last // 2)
            def _():
                for s, a, b, c in per_stage:
                    s.relay(a, b, c)

            @pl.when(i == max(last - 1, 0))
            def _():
                for s, a, b, c in per_stage:
                    s.mid(a, b, c)

            @pl.when(i == last)
            def _():
                for s, a, b, c in per_stage:
                    s.end(a, b, c)

    res = pl.pallas_call(
        full, name=name, grid=grid,
        in_specs=in_specs + [ANY] * sum(k_in),
        out_specs=out_specs + [ANY] * sum(k_out),
        out_shape=out_shape + [o for s in stages for o in s.out_shape],
        scratch_shapes=scratch_shapes + [x for s in stages for x in s.scratch],
        compiler_params=pltpu.CompilerParams(dimension_semantics=("arbitrary",), vmem_limit_bytes=VMEM_LIMIT,
                                             collective_id=collective_id),
    )(*args, *[a for s in stages for a in s.inputs])
    return list(res[:n_out]), split(list(res[n_out:]), k_out)


def ffn_fwd_a(x, g_pre, w_gu_t, name, stages=()):
    tm, tn = MM_ROWS, 256
    n_w = len(w_gu_t)

    def body(x_ref, gp_ref, *refs):
        w_refs = refs[:n_w]
        n_ref, g_ref, u_ref, a_ref, wt_ref, sem = refs[n_w:]

        @pl.when(pl.program_id(0) == 0)
        def _():
            _load_weight(w_refs, wt_ref, sem)

        xv = x_ref[...]
        n = (xv * _rsqrt_mean_sq(xv) * gp_ref[...]).astype(BF16)
        n_ref[...] = n
        for j in range(D_FF // tn):
            g = _dot_nt(n, wt_ref[j * tn:(j + 1) * tn, :])
            u = _dot_nt(n, wt_ref[D_FF + j * tn:D_FF + (j + 1) * tn, :])
            g_ref[:, j * tn:(j + 1) * tn] = g.astype(BF16)
            u_ref[:, j * tn:(j + 1) * tn] = u.astype(BF16)
            a_ref[:, j * tn:(j + 1) * tn] = (g * _sigmoid(g) * u).astype(BF16)

    wide = pl.BlockSpec((tm, D_FF), lambda i: (i, 0))
    return _call(
        body, name=name, grid=(SEQ // tm,),
        in_specs=[_ROW(tm), _VEC] + [ANY] * n_w,
        out_specs=[_ROW(tm), wide, wide, wide],
        out_shape=[jax.ShapeDtypeStruct((SEQ, D_MODEL), BF16)] + [jax.ShapeDtypeStruct((SEQ, D_FF), BF16)] * 3,
        scratch_shapes=_weight_scratch(2 * D_FF, n_w),
        args=[x, g_pre, *w_gu_t], stages=stages)


def ffn_fwd_b(a, w_down, g_post, h_in, name, target=None, stages=()):
    tm = MM_ROWS
    final = target is not None

    def body(*refs):
        if final:
            a_ref, wf_ref, gp_ref, h_ref, t_ref, f_ref, o_ref, loss_ref, wd_ref, sem = refs
        else:
            a_ref, wf_ref, gp_ref, h_ref, f_ref, o_ref, wd_ref, sem = refs

        @pl.when(pl.program_id(0) == 0)
        def _():
            _load_weight(wf_ref, wd_ref, sem)
            if final:
                loss_ref[...] = jnp.zeros_like(loss_ref)

        f = _dot(a_ref[...], wd_ref[...])
        f_ref[...] = f
        y = h_ref[...] + MACARON * (f * _rsqrt_mean_sq(f) * gp_ref[...])
        if final:
            err = y - t_ref[...]
            o_ref[...] = err * (1.0 / D_MODEL)
            loss_ref[...] += 0.5 * jnp.sum(err * err) * (1.0 / D_MODEL)
        else:
            o_ref[...] = y

    row = _ROW(tm)
    in_specs = [pl.BlockSpec((tm, D_FF), lambda i: (i, 0)), ANY, _VEC, row]
    out_specs = [row, row]
    out_shape = [jax.ShapeDtypeStruct((SEQ, D_MODEL), F32)] * 2
    args = [a, w_down, g_post, h_in]
    if final:
        in_specs.append(row)
        args.append(target)
        out_specs.append(pl.BlockSpec((8, 128), lambda i: (0, 0)))
        out_shape.append(jax.ShapeDtypeStruct((8, 128), F32))
    return _call(body, name=name, grid=(SEQ // tm,), in_specs=in_specs, out_specs=out_specs,
                 out_shape=out_shape, scratch_shapes=_weight_scratch(D_FF), args=args, stages=stages)


def ffn_bwd_a(d_out, f, g_post, w_down, g, u, name, stages=()):
    tm = MM_ROWS
    tc = 256

    def body(do_ref, f_ref, gp_ref, wf_ref, g_ref, u_ref, df_ref, dgu_ref, dgp_ref, wd_ref, sem):
        @pl.when(pl.program_id(0) == 0)
        def _():
            _load_weight(wf_ref, wd_ref, sem)
            dgp_ref[...] = jnp.zeros_like(dgp_ref)

        fv = f_ref[...]
        rf = _rsqrt_mean_sq(fv)
        fh = fv * rf
        dn = MACARON * do_ref[...]
        dgp_ref[...] += jnp.sum(dn * fh, axis=0, keepdims=True)
        t = dn * gp_ref[...]
        df = (rf * (t - fh * jnp.mean(t * fh, axis=-1, keepdims=True))).astype(BF16)
        df_ref[...] = df
        for c0 in range(0, D_FF, tc):
            da = _dot_nt(df, wd_ref[c0:c0 + tc, :])
            gv = g_ref[:, c0:c0 + tc].astype(F32)
            uv = u_ref[:, c0:c0 + tc].astype(F32)
            s = _sigmoid(gv)
            dgu_ref[:, c0:c0 + tc] = (da * uv * s * (1.0 + gv * (1.0 - s))).astype(BF16)
            dgu_ref[:, D_FF + c0:D_FF + c0 + tc] = (da * gv * s).astype(BF16)

    row = _ROW(tm)
    wide = pl.BlockSpec((tm, D_FF), lambda i: (i, 0))
    return _call(
        body, name=name, grid=(SEQ // tm,),
        in_specs=[row, row, _VEC, ANY, wide, wide],
        out_specs=[row, pl.BlockSpec((tm, 2 * D_FF), lambda i: (i, 0)), _VEC],
        out_shape=[jax.ShapeDtypeStruct((SEQ, D_MODEL), BF16), jax.ShapeDtypeStruct((SEQ, 2 * D_FF), BF16),
                   jax.ShapeDtypeStruct((1, D_MODEL), F32)],
        scratch_shapes=_weight_scratch(D_FF),
        args=[d_out, f, g_post, w_down, g, u], stages=stages)


def norm_bwd(pieces, w_t, x, g_pre, d_res, name, stages=()):
    tm = MM_ROWS
    widths = [p.shape[1] for p in pieces]
    offs = [sum(widths[:k]) for k in range(len(widths))]
    n_p = len(pieces)
    n_w = len(w_t)

    def body(*refs):
        p_refs = refs[:n_p]
        w_refs = refs[n_p:n_p + n_w]
        x_ref, g_ref, r_ref, dx_ref, dg_ref, wt_ref, sem = refs[n_p + n_w:]

        @pl.when(pl.program_id(0) == 0)
        def _():
            _load_weight(w_refs, wt_ref, sem)
            dg_ref[...] = jnp.zeros_like(dg_ref)

        dn = None
        for p_ref, lo, wd in zip(p_refs, offs, widths):
            part = _dot(p_ref[...], wt_ref[lo:lo + wd, :])
            dn = part if dn is None else dn + part
        xv = x_ref[...]
        r = _rsqrt_mean_sq(xv)
        xh = xv * r
        dg_ref[...] += jnp.sum(dn * xh, axis=0, keepdims=True)
        t = dn * g_ref[...]
        dx_ref[...] = r_ref[...] + r * (t - xh * jnp.mean(t * xh, axis=-1, keepdims=True))

    row = _ROW(tm)
    return _call(
        body, name=name, grid=(SEQ // tm,),
        in_specs=[pl.BlockSpec((tm, wd), lambda i: (i, 0)) for wd in widths] + [ANY] * n_w + [row, _VEC, row],
        out_specs=[row, _VEC],
        out_shape=[jax.ShapeDtypeStruct((SEQ, D_MODEL), F32), jax.ShapeDtypeStruct((1, D_MODEL), F32)],
        scratch_shapes=_weight_scratch(sum(widths), n_w),
        args=[*pieces, *w_t, x, g_pre, d_res], stages=stages)


def mm_tn(pieces, b, tm, name, stages=()):
    widths = [p.shape[1] for p in pieces]
    m_total = sum(widths)
    n_p = len(pieces)
    starts = [sum(widths[:k]) // tm for k in range(n_p)]
    counts = [wd // tm for wd in widths]

    def body(*refs):
        p_refs = refs[:n_p]
        b_ref, o_ref = refs[n_p:]
        i = pl.program_id(0)
        for p_ref, st, ct in zip(p_refs, starts, counts):
            @pl.when((i >= st) & (i < st + ct))
            def _(p_ref=p_ref):
                o_ref[...] = _dot_tn(p_ref[...], b_ref[...]).astype(BF16)

    def piece_spec(st, ct):
        return pl.BlockSpec((SEQ, tm), lambda i: (0, jnp.clip(i - st, 0, ct - 1)))

    (out,), stage_out = _call(
        body, name=name, grid=(m_total // tm,),
        in_specs=[piece_spec(st, ct) for st, ct in zip(starts, counts)] + [pl.BlockSpec((SEQ, D_MODEL), lambda i: (0, 0))],
        out_specs=[pl.BlockSpec((tm, D_MODEL), lambda i: (i, 0))],
        out_shape=[jax.ShapeDtypeStruct((m_total, D_MODEL), BF16)],
        args=[*pieces, b], stages=stages)
    return out, stage_out


def mix_in(h, g_pre, w_in_t, name, stages=()):
    tm = MM_ROWS
    offs = [sum(IN_SEGS[:k]) for k in range(len(IN_SEGS))]
    dts = [F32, F32, F32, F32, BF16, F32, F32]
    n_o = len(IN_SEGS)
    n_w = len(w_in_t)

    def body(*refs):
        h_ref, g_ref = refs[:2]
        w_refs = refs[2:2 + n_w]
        um_ref = refs[2 + n_w]
        o_refs = refs[3 + n_w:3 + n_w + n_o]
        wt_ref, sem = refs[3 + n_w + n_o:]

        @pl.when(pl.program_id(0) == 0)
        def _():
            _load_weight(w_refs, wt_ref, sem)

        hv = h_ref[...]
        um = (hv * _rsqrt_mean_sq(hv) * g_ref[...]).astype(BF16)
        um_ref[...] = um
        for o_ref, lo, wd in zip(o_refs, offs, IN_SEGS):
            for c0 in range(0, wd, 256):
                o_ref[:, c0:c0 + 256] = _dot_nt(um, wt_ref[lo + c0:lo + c0 + 256, :]).astype(o_ref.dtype)

    return _call(
        body, name=name, grid=(SEQ // tm,),
        in_specs=[_ROW(tm), _VEC] + [ANY] * n_w,
        out_specs=[_ROW(tm)] + [pl.BlockSpec((tm, wd), lambda i: (i, 0)) for wd in IN_SEGS],
        out_shape=[jax.ShapeDtypeStruct((SEQ, D_MODEL), BF16)]
        + [jax.ShapeDtypeStruct((SEQ, wd), dt) for wd, dt in zip(IN_SEGS, dts)],
        scratch_shapes=_weight_scratch(IN_W, n_w),
        args=[h, g_pre, *w_in_t], stages=stages)


LRU_TC = 256


def _conv_fwd(xb, cw, cb, tt):
    xc = xb * cw[3:4, :] + cb
    shifted = []
    for s in (1, 2, 3):
        sh = jnp.where(tt >= s, pltpu.roll(xb, s, 0), 0.0)
        shifted.append(sh)
        xc = xc + sh * cw[3 - s:4 - s, :]
    return xc, shifted


def _lru_gates(xc, wa, ba, wx, bx, lam):
    xcb = xc.astype(BF16)
    r = _sigmoid(_dot(xcb, wa) + ba)
    i = _sigmoid(_dot(xcb, wx) + bx)
    nl = -lam
    sp = jnp.maximum(nl, 0.0) + jnp.log1p(jnp.exp(-jnp.abs(nl)))
    la = (-LRU_C * r) * sp
    a = jnp.exp(la)
    mult = jnp.sqrt(jnp.maximum(-_expm1(2.0 * la), 0.0))
    return xcb, r, i, sp, a, mult


def _scan(a, b, tt, reverse, a_s, b_s):
    n = a.shape[0]
    tg = tt & 7
    for s in (1, 2, 4):
        keep = (tg < 8 - s) if reverse else (tg >= s)
        shift = n - s if reverse else s
        b = a * jnp.where(keep, pltpu.roll(b, shift, 0), 0.0) + b
        a = a * jnp.where(keep, pltpu.roll(a, shift, 0), 1.0)
    a_s[...] = a
    b_s[...] = b
    groups = n // 8

    def step(g, carry):
        gi = (groups - 1 - g) if reverse else g
        rows = pl.ds(pl.multiple_of(gi * 8, 8), 8)
        hg = a_s[rows, :] * carry + b_s[rows, :]
        b_s[rows, :] = hg
        return hg[0:1, :] if reverse else hg[7:8, :]

    lax.fori_loop(0, groups, step, jnp.zeros((1, a.shape[1]), F32), unroll=8)
    return b_s[...]


def _lru_specs():
    col = pl.BlockSpec((SEQ, LRU_TC), lambda j: (0, j))
    vec = pl.BlockSpec((1, LRU_TC), lambda j: (0, j))
    bd = pl.BlockSpec((1, LRU_TC, LRU_TC), lambda j: (j, 0, 0))
    cw = pl.BlockSpec((4, LRU_TC), lambda j: (0, j))
    return col, vec, bd, cw


def lru_fwd(gate, xbr, conv_w, conv_b, wa_bd, b_a, wx_bd, b_x, lam, name, stages=()):
    col, vec, bd, cw = _lru_specs()

    def body(gate_ref, xbr_ref, cw_ref, cb_ref, wa_ref, ba_ref, wx_ref, bx_ref, lam_ref, y_ref, h_ref, a_s, b_s):
        tt = lax.broadcasted_iota(jnp.int32, (SEQ, LRU_TC), 0)
        xc, _ = _conv_fwd(xbr_ref[...], cw_ref[...], cb_ref[...], tt)
        _, r, i, sp, a, mult = _lru_gates(xc, wa_ref[0], ba_ref[...], wx_ref[0], bx_ref[...], lam_ref[...])
        h = _scan(a, mult * (i * xc), tt, False, a_s, b_s)
        h_ref[...] = h
        gl, _ = _gelu(gate_ref[...])
        y_ref[...] = (h * gl).astype(BF16)

    return _call(
        body, name=name, grid=(LRU_W // LRU_TC,),
        in_specs=[col, col, cw, vec, bd, vec, bd, vec, vec],
        out_specs=[col, col],
        out_shape=[jax.ShapeDtypeStruct((SEQ, LRU_W), BF16), jax.ShapeDtypeStruct((SEQ, LRU_W), F32)],
        scratch_shapes=[pltpu.VMEM((SEQ, LRU_TC), F32)] * 2,
        args=[gate, xbr, conv_w, conv_b, wa_bd, b_a, wx_bd, b_x, lam], stages=stages)


def lru_bwd(gate, xbr, h, dy, conv_w, conv_b, wa_bd, b_a, wx_bd, b_x, lam, name, stages=()):
    col, vec, bd, cw = _lru_specs()

    def body(gate_ref, xbr_ref, h_ref, dy_ref, cw_ref, cb_ref, wa_ref, ba_ref, wx_ref, bx_ref, lam_ref,
             dgate_ref, dxbr_ref, vecs_ref, dwa_ref, dwx_ref, a_s, b_s):
        tt = lax.broadcasted_iota(jnp.int32, (SEQ, LRU_TC), 0)
        cwv = cw_ref[...]
        lam = lam_ref[...]
        xb = xbr_ref[...]
        xc, shifted = _conv_fwd(xb, cwv, cb_ref[...], tt)
        wa = wa_ref[0]
        wx = wx_ref[0]
        xcb, r, i, sp, a, mult = _lru_gates(xc, wa, ba_ref[...], wx, bx_ref[...], lam)
        hv = h_ref[...]
        dyv = dy_ref[...]
        gv = gate_ref[...]
        gl, th = _gelu(gv)
        dgate_ref[...] = (dyv * hv * _gelu_grad(gv, th)).astype(BF16)
        a_next = jnp.where(tt < SEQ - 1, pltpu.roll(a, SEQ - 1, 0), 0.0)
        gsum = _scan(a_next, dyv * gl, tt, True, a_s, b_s)
        h_prev = jnp.where(tt >= 1, pltpu.roll(hv, 1, 0), 0.0)
        d_mult = gsum * i * xc
        d_i = gsum * mult * xc
        d_xc = gsum * mult * i
        d_la = gsum * h_prev * a - d_mult * (a * a) / mult
        d_pr = (d_la * (-LRU_C * sp)) * r * (1.0 - r)
        d_pi = d_i * i * (1.0 - i)
        d_lam = jnp.sum(d_la * r, axis=0, keepdims=True) * (LRU_C * _sigmoid(-lam))
        d_prb = d_pr.astype(BF16)
        d_pib = d_pi.astype(BF16)
        d_xc = d_xc + _dot_nt(d_prb, wa) + _dot_nt(d_pib, wx)
        dwa_ref[0] = _dot_tn(xcb, d_prb)
        dwx_ref[0] = _dot_tn(xcb, d_pib)
        rows = [jnp.sum(d_xc * shifted[2], axis=0, keepdims=True),
                jnp.sum(d_xc * shifted[1], axis=0, keepdims=True),
                jnp.sum(d_xc * shifted[0], axis=0, keepdims=True),
                jnp.sum(d_xc * xb, axis=0, keepdims=True),
                jnp.sum(d_xc, axis=0, keepdims=True),
                jnp.sum(d_pr, axis=0, keepdims=True),
                jnp.sum(d_pi, axis=0, keepdims=True),
                d_lam]
        ri = lax.broadcasted_iota(jnp.int32, (8, LRU_TC), 0)
        acc = jnp.zeros((8, LRU_TC), F32)
        for k, rv in enumerate(rows):
            acc = jnp.where(ri == k, rv, acc)
        vecs_ref[...] = acc
        d_xb = d_xc * cwv[3:4, :]
        for s in (1, 2, 3):
            d_xb = d_xb + jnp.where(tt < SEQ - s, pltpu.roll(d_xc, SEQ - s, 0), 0.0) * cwv[3 - s:4 - s, :]
        dxbr_ref[...] = d_xb.astype(BF16)

    return _call(
        body, name=name, grid=(LRU_W // LRU_TC,),
        in_specs=[col, col, col, col, cw, vec, bd, vec, bd, vec, vec],
        out_specs=[col, col, pl.BlockSpec((8, LRU_TC), lambda j: (0, j)), bd, bd],
        out_shape=[jax.ShapeDtypeStruct((SEQ, LRU_W), BF16), jax.ShapeDtypeStruct((SEQ, LRU_W), BF16),
                   jax.ShapeDtypeStruct((8, LRU_W), F32),
                   jax.ShapeDtypeStruct((LRU_W // LRU_TC, LRU_TC, LRU_TC), F32),
                   jax.ShapeDtypeStruct((LRU_W // LRU_TC, LRU_TC, LRU_TC), F32)],
        scratch_shapes=[pltpu.VMEM((SEQ, LRU_TC), F32)] * 2,
        args=[gate, xbr, h, dy, conv_w, conv_b, wa_bd, b_a, wx_bd, b_x, lam], stages=stages)


def _rope(x, cos, sin_signed):
    w = x.shape[1]
    reps = w // 128
    if reps > 1:
        cos = jnp.tile(cos, (1, reps))
        sin_signed = jnp.tile(sin_signed, (1, reps))
    lane = lax.broadcasted_iota(jnp.int32, x.shape, 1)
    first = (lane & 63) < 32
    partner = jnp.where(first, pltpu.roll(x, w - 32, 1), pltpu.roll(x, 32, 1))
    return x * cos + partner * sin_signed


def _both_halves(t, odd):
    lo = lax.broadcasted_iota(jnp.int32, t.shape, 1) < 64
    rolled = pltpu.roll(t, 64, 1)
    return jnp.where(lo, rolled, t) if odd else jnp.where(lo, t, rolled)


def _stack_heads(ta, tb):
    lo = lax.broadcasted_iota(jnp.int32, ta.shape, 1) < 64
    return jnp.concatenate([jnp.where(lo, ta, 0.0), jnp.where(lo, 0.0, ta),
                            jnp.where(lo, tb, 0.0), jnp.where(lo, 0.0, tb)], axis=0)


def _unstack_heads(o):
    lo = lax.broadcasted_iota(jnp.int32, (ATTN_BLOCK, 128), 1) < 64
    return (jnp.where(lo, o[0:128], o[128:256]), jnp.where(lo, o[256:384], o[384:512]))


def _window_upper_t():
    shape = (ATTN_BLOCK, 4 * ATTN_BLOCK)
    return lax.broadcasted_iota(jnp.int32, shape, 0) > (lax.broadcasted_iota(jnp.int32, shape, 1) & (ATTN_BLOCK - 1))


def _fold_t(t, upper_t):
    return jnp.where(upper_t, t[:ATTN_BLOCK], t[ATTN_BLOCK:])


def _unfold_t(t, upper_t):
    zero = jnp.zeros_like(t)
    return jnp.concatenate([jnp.where(upper_t, t, zero), jnp.where(upper_t, zero, t)], axis=0)


def _attn_probs_t(kd, qs, sinks_ref, hk, first_block, upper_t):
    s = _fold_t(_dot_nt(kd, qs), upper_t) * (HEAD_DIM ** -0.5)
    s = jnp.where(jnp.logical_and(upper_t, first_block), MASK_VALUE, s)
    rg = lax.broadcasted_iota(jnp.int32, (1, 4 * ATTN_BLOCK), 1) >> 7
    sink = jnp.where(rg == 0, sinks_ref[4 * hk],
                     jnp.where(rg == 1, sinks_ref[4 * hk + 1],
                               jnp.where(rg == 2, sinks_ref[4 * hk + 2], sinks_ref[4 * hk + 3])))
    m = jnp.maximum(jnp.max(s, axis=0, keepdims=True), sink)
    e = jnp.exp(s - m)
    es = jnp.exp(sink - m)
    inv = 1.0 / (jnp.sum(e, axis=0, keepdims=True) + es)
    return e * inv, es * inv


def _prev(i):
    return jnp.maximum(i - 1, 0)


def attn_fwd(q, k, v, cos, sin_signed, sinks, name, stages=()):
    nb = ATTN_BLOCK

    def body(q_ref, kc_ref, kp_ref, vc_ref, vp_ref, cc_ref, sc_ref, cp_ref, sp_ref, sinks_ref,
             qr_ref, kr_ref, y_ref):
        first_block = pl.program_id(0) == 0
        qr = _rope(q_ref[...], cc_ref[...], sc_ref[...])
        kc = _rope(kc_ref[...], cc_ref[...], sc_ref[...])
        kp = _rope(kp_ref[...], cp_ref[...], sp_ref[...])
        qr_ref[...] = qr.astype(BF16)
        kr_ref[...] = kc.astype(BF16)
        k2 = jnp.concatenate([kp, kc], axis=0)
        v2 = jnp.concatenate([vp_ref[...].astype(F32), vc_ref[...].astype(F32)], axis=0)
        upper_t = _window_upper_t()
        y_tiles = []
        for hk in range(N_KV_HEADS):
            kt = hk // 2
            kd = _both_halves(k2[:, kt * 128:(kt + 1) * 128], hk % 2).astype(BF16)
            vd = _both_halves(v2[:, kt * 128:(kt + 1) * 128], hk % 2).astype(BF16)
            qs = _stack_heads(qr[:, (2 * hk) * 128:(2 * hk + 1) * 128],
                              qr[:, (2 * hk + 1) * 128:(2 * hk + 2) * 128]).astype(BF16)
            p, _ = _attn_probs_t(kd, qs, sinks_ref, hk, first_block, upper_t)
            y_tiles += _unstack_heads(_dot_tn(_unfold_t(p.astype(BF16), upper_t), vd))
        y_ref[...] = jnp.concatenate(y_tiles, axis=1).astype(BF16)

    cur = lambda w: pl.BlockSpec((nb, w), lambda i: (i, 0))
    prv = lambda w: pl.BlockSpec((nb, w), lambda i: (_prev(i), 0))
    return _call(
        body, name=name, grid=(N_ATTN_BLOCKS,),
        in_specs=[cur(D_MODEL), cur(KV_W), prv(KV_W), cur(KV_W), prv(KV_W), cur(128), cur(128), prv(128), prv(128),
                  pl.BlockSpec(memory_space=pltpu.SMEM)],
        out_specs=[cur(D_MODEL), cur(KV_W), cur(D_MODEL)],
        out_shape=[jax.ShapeDtypeStruct((SEQ, D_MODEL), BF16), jax.ShapeDtypeStruct((SEQ, KV_W), BF16),
                   jax.ShapeDtypeStruct((SEQ, D_MODEL), BF16)],
        args=[q, k, k, v, v, cos, sin_signed, cos, sin_signed, sinks], stages=stages)


def attn_bwd(qr, kr, v, dy, cos, sin_signed, sinks, name, stages=()):
    nb = ATTN_BLOCK
    n_steps = N_ATTN_BLOCKS + 1
    scale = HEAD_DIM ** -0.5

    def body(q_ref, kc_ref, kp_ref, vc_ref, vp_ref, dy_ref, cc_ref, sc_ref, cp_ref, sp_ref, sinks_ref,
             dq_ref, dkv_ref, dsk_ref, ck_ref, cv_ref):
        dk_ref = dkv_ref.at[:, pl.ds(0, KV_W)]
        dv_ref = dkv_ref.at[:, pl.ds(KV_W, KV_W)]
        i = pl.program_id(0)

        @pl.when(i == 0)
        def _():
            dsk_ref[...] = jnp.zeros_like(dsk_ref)
            ck_ref[...] = jnp.zeros_like(ck_ref)
            cv_ref[...] = jnp.zeros_like(cv_ref)

        @pl.when(i < N_ATTN_BLOCKS)
        def _():
            qv = q_ref[...].astype(F32)
            dov = dy_ref[...].astype(F32)
            k2 = jnp.concatenate([kp_ref[...].astype(F32), kc_ref[...].astype(F32)], axis=0)
            v2 = jnp.concatenate([vp_ref[...].astype(F32), vc_ref[...].astype(F32)], axis=0)
            lane = lax.broadcasted_iota(jnp.int32, (8, 128), 1)
            lo = lax.broadcasted_iota(jnp.int32, (2 * nb, 128), 1) < 64
            dsk = jnp.zeros((8, 128), F32)
            dk_tiles = []
            dv_tiles = []
            dq_tiles = []
            upper_t = _window_upper_t()
            for hk in range(N_KV_HEADS):
                kt = hk // 2
                kd = _both_halves(k2[:, kt * 128:(kt + 1) * 128], hk % 2).astype(BF16)
                vd = _both_halves(v2[:, kt * 128:(kt + 1) * 128], hk % 2).astype(BF16)
                qs = _stack_heads(qv[:, (2 * hk) * 128:(2 * hk + 1) * 128],
                                  qv[:, (2 * hk + 1) * 128:(2 * hk + 2) * 128]).astype(BF16)
                dos = _stack_heads(dov[:, (2 * hk) * 128:(2 * hk + 1) * 128],
                                   dov[:, (2 * hk + 1) * 128:(2 * hk + 2) * 128]).astype(BF16)
                p, ps = _attn_probs_t(kd, qs, sinks_ref, hk, i == 0, upper_t)
                dp = _fold_t(_dot_nt(vd, dos), upper_t)
                delta = jnp.sum(p * dp, axis=0, keepdims=True)
                ds = _unfold_t((p * (dp - delta)).astype(BF16), upper_t)
                dsink = -ps * delta
                for g in range(4):
                    dsk = dsk + jnp.where(lane == 4 * hk + g, jnp.sum(dsink[:, g * nb:(g + 1) * nb]), 0.0)
                dq_tiles += _unstack_heads(_dot_tn(ds, kd) * scale)
                rk = _dot(ds, qs) * scale
                rv = _dot(_unfold_t(p.astype(BF16), upper_t), dos)
                dk_tiles.append(rk + pltpu.roll(rk, 64, 1))
                dv_tiles.append(rv + pltpu.roll(rv, 64, 1))
            dsk_ref[...] += dsk
            dq_ref[...] = _rope(jnp.concatenate(dq_tiles, axis=1), cc_ref[...], -sc_ref[...]).astype(BF16)
            dk_full = jnp.concatenate([jnp.where(lo, dk_tiles[0], dk_tiles[1]),
                                       jnp.where(lo, dk_tiles[2], dk_tiles[3])], axis=1)
            dv_full = jnp.concatenate([jnp.where(lo, dv_tiles[0], dv_tiles[1]),
                                       jnp.where(lo, dv_tiles[2], dv_tiles[3])], axis=1)
            dk_ref[...] = _rope(ck_ref[...] + dk_full[0:nb], cp_ref[...], -sp_ref[...]).astype(BF16)
            dv_ref[...] = (cv_ref[...] + dv_full[0:nb]).astype(BF16)
            ck_ref[...] = dk_full[nb:2 * nb]
            cv_ref[...] = dv_full[nb:2 * nb]

        @pl.when(i == N_ATTN_BLOCKS)
        def _():
            dk_ref[...] = _rope(ck_ref[...], cp_ref[...], -sp_ref[...]).astype(BF16)
            dv_ref[...] = cv_ref[...].astype(BF16)

    qi = lambda i: jnp.minimum(i, N_ATTN_BLOCKS - 1)
    cur = lambda w: pl.BlockSpec((nb, w), lambda i: (qi(i), 0))
    prv = lambda w: pl.BlockSpec((nb, w), lambda i: (_prev(qi(i)), 0))
    out_prev = lambda w: pl.BlockSpec((nb, w), lambda i: (_prev(i), 0))
    return _call(
        body, name=name, grid=(n_steps,),
        in_specs=[cur(D_MODEL), cur(KV_W), prv(KV_W), cur(KV_W), prv(KV_W), cur(D_MODEL),
                  cur(128), cur(128), out_prev(128), out_prev(128), pl.BlockSpec(memory_space=pltpu.SMEM)],
        out_specs=[cur(D_MODEL), out_prev(2 * KV_W), pl.BlockSpec((8, 128), lambda i: (0, 0))],
        out_shape=[jax.ShapeDtypeStruct((SEQ, D_MODEL), BF16), jax.ShapeDtypeStruct((SEQ, 2 * KV_W), BF16),
                   jax.ShapeDtypeStruct((8, 128), F32)],
        scratch_shapes=[pltpu.VMEM((nb, KV_W), F32), pltpu.VMEM((nb, KV_W), F32)],
        args=[qr, kr, kr, v, v, dy, cos, sin_signed, cos, sin_signed, sinks], stages=stages)


def _proj_scratch():
    return [pltpu.VMEM((D_MODEL, D_MODEL), BF16)] * 3 + [pltpu.SemaphoreType.DMA((3 * N_CHIPS,))]


def _load_projs(w_refs, wl_ref, wa_ref, wo_ref, sem):
    for k, (w_ref, dst) in enumerate(zip(w_refs, (wl_ref, wa_ref, wo_ref))):
        _load_weight(w_ref, dst, sem.at[pl.ds(k * N_CHIPS, N_CHIPS)])


def merge_fwd(y_lru, y_attn, g_lru, g_attn, projs, g_post, h_in, name, stages=()):
    tm = MM_ROWS

    def body(yl_ref, ya_ref, gl_ref, ga_ref, w1_ref, w2_ref, w3_ref, gp_ref, h_ref,
             pl_ref, pa_ref, mg_ref, m_ref, o_ref, wl_ref, wa_ref, wo_ref, sem):
        @pl.when(pl.program_id(0) == 0)
        def _():
            _load_projs((w1_ref, w2_ref, w3_ref), wl_ref, wa_ref, wo_ref, sem)

        p_l = _dot(yl_ref[...], wl_ref[...])
        p_a = _dot(ya_ref[...], wa_ref[...])
        pl_ref[...] = p_l.astype(BF16)
        pa_ref[...] = p_a.astype(BF16)
        merged = (_sigmoid(gl_ref[...]) * p_l + _sigmoid(ga_ref[...]) * p_a).astype(BF16)
        mg_ref[...] = merged
        m = _dot(merged, wo_ref[...])
        m_ref[...] = m
        o_ref[...] = h_ref[...] + m * _rsqrt_mean_sq(m) * gp_ref[...]

    row = _ROW(tm)
    return _call(
        body, name=name, grid=(SEQ // tm,),
        in_specs=[row, row, row, row, ANY, ANY, ANY, _VEC, row],
        out_specs=[row] * 5,
        out_shape=[jax.ShapeDtypeStruct((SEQ, D_MODEL), BF16)] * 3 + [jax.ShapeDtypeStruct((SEQ, D_MODEL), F32)] * 2,
        scratch_shapes=_proj_scratch(),
        args=[y_lru, y_attn, g_lru, g_attn, *projs, g_post, h_in], stages=stages)


def merge_bwd(d_out, m, g_post, projs, g_lru, g_attn, p_l, p_a, name, stages=()):
    tm = 256

    def body(do_ref, m_ref, gp_ref, w1_ref, w2_ref, w3_ref, gl_ref, ga_ref, pl_ref, pa_ref,
             dm_ref, dpl_ref, dpa_ref, dgl_ref, dga_ref, dya_ref, dyl_ref, dgp_ref, wl_ref, wa_ref, wo_ref, sem):
        @pl.when(pl.program_id(0) == 0)
        def _():
            _load_projs((w1_ref, w2_ref, w3_ref), wl_ref, wa_ref, wo_ref, sem)
            dgp_ref[...] = jnp.zeros_like(dgp_ref)

        mv = m_ref[...]
        rm = _rsqrt_mean_sq(mv)
        mh = mv * rm
        dn = do_ref[...]
        dgp_ref[...] += jnp.sum(dn * mh, axis=0, keepdims=True)
        t = dn * gp_ref[...]
        dm = (rm * (t - mh * jnp.mean(t * mh, axis=-1, keepdims=True))).astype(BF16)
        dm_ref[...] = dm
        dmg = _dot_nt(dm, wo_ref[...])
        sl = _sigmoid(gl_ref[...])
        sa = _sigmoid(ga_ref[...])
        dpl = (dmg * sl).astype(BF16)
        dpa = (dmg * sa).astype(BF16)
        dpl_ref[...] = dpl
        dpa_ref[...] = dpa
        dgl_ref[...] = (dmg * pl_ref[...].astype(F32) * sl * (1.0 - sl)).astype(BF16)
        dga_ref[...] = (dmg * pa_ref[...].astype(F32) * sa * (1.0 - sa)).astype(BF16)
        dyl_ref[...] = _dot_nt(dpl, wl_ref[...])
        dya_ref[...] = _dot_nt(dpa, wa_ref[...]).astype(BF16)

    row = _ROW(tm)
    return _call(
        body, name=name, grid=(SEQ // tm,),
        in_specs=[row, row, _VEC, ANY, ANY, ANY, row, row, row, row],
        out_specs=[row] * 7 + [_VEC],
        out_shape=[jax.ShapeDtypeStruct((SEQ, D_MODEL), BF16)] * 6 + [jax.ShapeDtypeStruct((SEQ, D_MODEL), F32),
                                                                       jax.ShapeDtypeStruct((1, D_MODEL), F32)],
        scratch_shapes=_proj_scratch(),
        args=[d_out, m, g_post, *projs, g_lru, g_attn, p_l, p_a], stages=stages)


def _rope_tables():
    half = HEAD_DIM // 2
    inv_freq = np.float32(ROPE_THETA) ** (-np.arange(half, dtype=np.float32) / np.float32(half))
    ang = np.arange(SEQ, dtype=np.float32)[:, None] * inv_freq[None, :]
    cos, sin = np.cos(ang), np.sin(ang)
    return (jnp.asarray(np.tile(np.concatenate([cos, cos], axis=1), (1, 2))),
            jnp.asarray(np.tile(np.concatenate([-sin, sin], axis=1), (1, 2))))


def _block_diag(w):
    per = LRU_TC // LRU_BLOCK_W
    w4 = w.reshape(LRU_W // LRU_TC, per, LRU_BLOCK_W, LRU_BLOCK_W)
    eye = jnp.eye(per, dtype=w.dtype)
    return jnp.einsum('jacd,ab->jacbd', w4, eye).reshape(LRU_W // LRU_TC, LRU_TC, LRU_TC).astype(BF16)


def _diag_blocks(p):
    per = LRU_TC // LRU_BLOCK_W
    p5 = p.reshape(LRU_W // LRU_TC, per, LRU_BLOCK_W, per, LRU_BLOCK_W)
    return jnp.stack([p5[:, a, :, a, :] for a in range(per)], axis=1).reshape(LRU_W // LRU_BLOCK_W, LRU_BLOCK_W, LRU_BLOCK_W)


def _place():
    x, y, c = lax.axis_index('x'), lax.axis_index('y'), lax.axis_index('c')
    chips = [(1 - x, y), (x, 1 - y), (1 - x, 1 - y)]
    return x, y, c, chips


def _rcopy(src, dst, send_sem, recv_sem, to):
    return pltpu.make_async_remote_copy(src_ref=src, dst_ref=dst, send_sem=send_sem, recv_sem=recv_sem,
                                        device_id=to, device_id_type=MESH)


class _Stage:
    inputs, out_shape, scratch, peers = (), (), (), ()

    def start(self, ins, outs, scr):
        plan = self._plan(ins, outs, scr)
        for ld in plan['loads']:
            ld.start()
        for cp in plan['sends']:
            cp.start()

    def relay(self, ins, outs, scr):
        pass

    def mid(self, ins, outs, scr):
        plan = self._plan(ins, outs, scr)
        for ld, st in zip(plan['loads'], plan['stores']):
            ld.wait()
            st.start()
        for arrived, onward in zip(plan['arrivals'], plan['forwards']):
            arrived.wait_recv()
            onward.start()

    def end(self, ins, outs, scr):
        plan = self._plan(ins, outs, scr)
        for st in plan['stores']:
            st.wait()
        for arrived in (plan['final_arrivals'] if plan['forwards'] else plan['arrivals']):
            arrived.wait_recv()
        for cp in plan['sends'] + plan['forwards']:
            cp.wait_send()


def _empty_plan():
    return dict(loads=[], stores=[], sends=[], arrivals=[], forwards=[], final_arrivals=[])


class GatherStage(_Stage):
    peers = ('chips', 'sib')
    N_CP = 12

    def __init__(self, items):
        self.ranges = [(off, rows) for _, off, rows in items]
        self.inputs = [src for src, _, _ in items]
        self.out_shape = [jax.ShapeDtypeStruct((N_CHIPS, rows, D_MODEL), BF16) for _, rows in self.ranges]
        n = self.N_CP * len(items)
        self.scratch = [pltpu.VMEM((sum(r for _, r in self.ranges), D_MODEL), BF16), pltpu.SemaphoreType.DMA((n,)),
                        pltpu.SemaphoreType.DMA((n,)), pltpu.SemaphoreType.DMA((2 * len(items),))]

    def _plan(self, ins, outs, scr):
        buf, send, recv, lsem = scr
        x, y, c, _ = _place()
        me_q, q_x, q_y, q_d = 2 * x + y, 2 * (1 - x) + y, 2 * x + (1 - y), 2 * (1 - x) + (1 - y)
        to_x, to_y, sib = (1 - x, y, c), (x, 1 - y, c), (x, y, 1 - c)
        plan = dict(loads=[], stores=[], first=[], early=[], relays=[], late=[], hand_early=[], hand_late=[], final=[])
        boff = 0
        for w, ((off, rows), p_ref, o_ref) in enumerate(zip(self.ranges, ins, outs)):
            hr = rows // 2
            ch = hr // 2
            plan['loads'].append(pltpu.make_async_copy(p_ref.at[pl.ds(off, rows)], buf.at[pl.ds(boff, rows)], lsem.at[2 * w]))
            plan['stores'].append(pltpu.make_async_copy(buf.at[pl.ds(boff, rows)], o_ref.at[me_q], lsem.at[2 * w + 1]))
            boff += rows
            base = w * self.N_CP
            mine = [pl.ds(pl.multiple_of(c * hr + k * ch, 16), ch) for k in range(2)]
            theirs = [pl.ds(pl.multiple_of((1 - c) * hr + k * ch, 16), ch) for k in range(2)]
            src = [p_ref.at[pl.ds(pl.multiple_of(off + c * hr + k * ch, 16), ch)] for k in range(2)]

            def cp(k, s, d, to):
                return _rcopy(s, d, send.at[base + k], recv.at[base + k], to)

            def here(q, rows_):
                return o_ref.at[q, rows_]

            plan['first'] += [cp(0, src[0], here(me_q, mine[0]), to_x), cp(2, src[1], here(me_q, mine[1]), to_y),
                              cp(1, src[1], here(me_q, mine[1]), to_x), cp(3, src[0], here(me_q, mine[0]), to_y)]
            x_a, y_b = here(q_x, mine[0]), here(q_y, mine[1])
            plan['early'] += [cp(0, x_a, x_a, to_x), cp(2, y_b, y_b, to_y)]
            plan['relays'] += [cp(4, x_a, x_a, to_y), cp(5, y_b, y_b, to_x)]
            plan['hand_early'] += [cp(6, x_a, x_a, sib), cp(7, y_b, y_b, sib)]
            x_b, y_a, d_a, d_b = here(q_x, mine[1]), here(q_y, mine[0]), here(q_d, mine[0]), here(q_d, mine[1])
            plan['late'] += [cp(1, x_b, x_b, to_x), cp(3, y_a, y_a, to_y), cp(4, d_a, d_a, to_y), cp(5, d_b, d_b, to_x)]
            plan['hand_late'] += [cp(8, x_b, x_b, sib), cp(9, y_a, y_a, sib), cp(10, d_a, d_a, sib), cp(11, d_b, d_b, sib)]
            for k, (q, piece) in enumerate([(q_x, 0), (q_y, 1), (q_x, 1), (q_y, 0), (q_d, 0), (q_d, 1)]):
                got = here(q, theirs[piece])
                plan['final'].append(cp(6 + k, got, got, sib))
        return plan

    def start(self, ins, outs, scr):
        plan = self._plan(ins, outs, scr)
        for ld in plan['loads']:
            ld.start()
        for cp in plan['first']:
            cp.start()

    def relay(self, ins, outs, scr):
        plan = self._plan(ins, outs, scr)
        for arrived in plan['early']:
            arrived.wait_recv()
        for cp in plan['relays'] + plan['hand_early']:
            cp.start()

    def mid(self, ins, outs, scr):
        plan = self._plan(ins, outs, scr)
        for ld, st in zip(plan['loads'], plan['stores']):
            ld.wait()
            st.start()
        for arrived in plan['late']:
            arrived.wait_recv()
        for cp in plan['hand_late']:
            cp.start()

    def end(self, ins, outs, scr):
        plan = self._plan(ins, outs, scr)
        for st in plan['stores']:
            st.wait()
        for arrived in plan['final']:
            arrived.wait_recv()
        for cp in plan['first'] + plan['relays'] + plan['hand_early'] + plan['hand_late']:
            cp.wait_send()


class PairStage(_Stage):
    peers = ('sib',)

    def __init__(self, grads):
        self.inputs = list(grads)
        self.out_shape = [jax.ShapeDtypeStruct((N_CHIPS, 1) + g.shape[2:], BF16) for g in grads]
        n_cp = N_CHIPS * len(grads)
        self.scratch = [pltpu.SemaphoreType.DMA((n_cp,)), pltpu.SemaphoreType.DMA((n_cp,))]

    def _plan(self, ins, outs, scr):
        send, recv = scr
        x, y, c, _ = _place()
        plan = _empty_plan()
        for w, (g_ref, l_ref) in enumerate(zip(ins, outs)):
            for q in range(N_CHIPS):
                i = w * N_CHIPS + q
                plan['sends'].append(_rcopy(g_ref.at[q, pl.ds(1 - c, 1)], l_ref.at[q], send.at[i], recv.at[i], (x, y, 1 - c)))
        plan['arrivals'] = plan['sends']
        return plan


class ChipStage(_Stage):
    peers = ('chips',)

    def __init__(self, items):
        self.ranges = [(off, n) for _, off, n in items]
        self.inputs = [s for s, _, _ in items]
        self.out_shape = [jax.ShapeDtypeStruct((N_CHIPS, n, D_MODEL), BF16) for _, n in self.ranges]
        n_cp = 3 * len(items)
        self.scratch = [pltpu.VMEM((sum(n for _, n in self.ranges), D_MODEL), BF16), pltpu.SemaphoreType.DMA((n_cp,)),
                        pltpu.SemaphoreType.DMA((n_cp,)), pltpu.SemaphoreType.DMA((2 * len(items),))]

    def _plan(self, ins, outs, scr):
        buf, send, recv, lsem = scr
        x, y, c, chips = _place()
        me_q = 2 * x + y
        plan = _empty_plan()
        boff = 0
        for w, ((off, n), s_ref, l_ref) in enumerate(zip(self.ranges, ins, outs)):
            rows = pl.ds(off, n)
            plan['loads'].append(pltpu.make_async_copy(s_ref.at[me_q, rows], buf.at[pl.ds(boff, n)], lsem.at[2 * w]))
            plan['stores'].append(pltpu.make_async_copy(buf.at[pl.ds(boff, n)], l_ref.at[me_q], lsem.at[2 * w + 1]))
            boff += n
            for j, (cx, cy) in enumerate(chips):
                i = w * 3 + j
                got = l_ref.at[2 * cx + cy]
                plan['sends'].append(_rcopy(s_ref.at[2 * cx + cy, rows], l_ref.at[me_q], send.at[i], recv.at[i], (cx, cy, c)))
                plan['arrivals'].append(_rcopy(got, got, send.at[i], recv.at[i], (cx, cy, c)))
        return plan


class SwapStage(_Stage):
    peers = ('sib',)

    def __init__(self, items):
        n = len(items)
        self.inputs = list(items)
        self.out_shape = [jax.ShapeDtypeStruct((2,) + a.shape, a.dtype) for a in items]
        self.scratch = [pltpu.VMEM(a.shape, a.dtype) for a in items] + [
            pltpu.SemaphoreType.DMA((n,)), pltpu.SemaphoreType.DMA((n,)), pltpu.SemaphoreType.DMA((2 * n,))]

    def _plan(self, ins, outs, scr):
        bufs, (send, recv, lsem) = scr[:len(ins)], scr[len(ins):]
        x, y, c, _ = _place()
        plan = _empty_plan()
        for w, (h_ref, o_ref, buf) in enumerate(zip(ins, outs, bufs)):
            plan['loads'].append(pltpu.make_async_copy(h_ref, buf, lsem.at[2 * w]))
            plan['stores'].append(pltpu.make_async_copy(buf, o_ref.at[c], lsem.at[2 * w + 1]))
            got = o_ref.at[1 - c]
            plan['sends'].append(_rcopy(h_ref, o_ref.at[c], send.at[w], recv.at[w], (x, y, 1 - c)))
            plan['arrivals'].append(_rcopy(got, got, send.at[w], recv.at[w], (x, y, 1 - c)))
        return plan


class SmallGatherStage(_Stage):
    peers = ('chips', 'sib')

    def __init__(self, blk):
        self.inputs = [blk]
        self.out_shape = [jax.ShapeDtypeStruct((N_DEV,) + blk.shape, blk.dtype)]
        self.scratch = [pltpu.VMEM(blk.shape, blk.dtype), pltpu.SemaphoreType.DMA((7,)), pltpu.SemaphoreType.DMA((7,)),
                        pltpu.SemaphoreType.DMA((2,))]

    def _plan(self, ins, outs, scr):
        (x_ref,), (o_ref,), (buf, send, recv, lsem) = ins, outs, scr
        x, y, c, chips = _place()
        sib = (x, y, 1 - c)

        def slot(px, py, pc):
            return o_ref.at[4 * px + 2 * py + pc]

        plan = _empty_plan()
        plan['loads'].append(pltpu.make_async_copy(x_ref, buf, lsem.at[0]))
        plan['stores'].append(pltpu.make_async_copy(buf, slot(x, y, c), lsem.at[1]))
        from_sib = slot(x, y, 1 - c)
        plan['sends'].append(_rcopy(x_ref, slot(x, y, c), send.at[0], recv.at[0], sib))
        plan['final_arrivals'].append(_rcopy(from_sib, from_sib, send.at[0], recv.at[0], sib))
        for j, (cx, cy) in enumerate(chips):
            got, got_sib = slot(cx, cy, c), slot(cx, cy, 1 - c)
            plan['sends'].append(_rcopy(x_ref, slot(x, y, c), send.at[1 + j], recv.at[1 + j], (cx, cy, c)))
            plan['arrivals'].append(_rcopy(got, got, send.at[1 + j], recv.at[1 + j], (cx, cy, c)))
            plan['forwards'].append(_rcopy(got, got, send.at[4 + j], recv.at[4 + j], sib))
            plan['final_arrivals'].append(_rcopy(got_sib, got_sib, send.at[4 + j], recv.at[4 + j], sib))
        return plan


_HBM = pl.BlockSpec(memory_space=pltpu.HBM)
_SEM = pl.BlockSpec(memory_space=pltpu.SEMAPHORE)
_DATAFLOW = pltpu.CompilerParams(has_side_effects=pltpu.SideEffectType.DATAFLOW_SIDE_EFFECTING)


def chip_exchange_start(s):
    def body(s_ref, land_ref, send, recv, s_thru, land_thru, token):
        x, y, c, chips = _place()
        for j, (cx, cy) in enumerate(chips):
            _rcopy(s_ref.at[2 * cx + cy], land_ref.at[2 * x + y], send.at[j], recv.at[j], (cx, cy, c)).start()
        token[...] = jnp.zeros_like(token)

    return pl.pallas_call(
        body, name='chip_exchange_start',
        out_shape=(pltpu.SemaphoreType.DMA((3,)), pltpu.SemaphoreType.DMA((3,)), pltpu.HBM(s.shape, s.dtype),
                   pltpu.HBM(s.shape, s.dtype), jax.ShapeDtypeStruct((8, 128), F32)),
        in_specs=(_HBM, _HBM), out_specs=(_SEM, _SEM, _HBM, _HBM, pl.BlockSpec(memory_space=pltpu.VMEM)),
        input_output_aliases={0: 2, 1: 3}, compiler_params=_DATAFLOW,
    )(pltpu.with_memory_space_constraint(s, pltpu.HBM),
      pltpu.with_memory_space_constraint(lax.empty(s.shape, s.dtype), pltpu.HBM))


def chip_exchange_wait(send, recv, s_thru, land_thru, after):
    def body(s_ref, land_ref, send_sem, recv_sem, after_ref, s_out, land_out):
        x, y, c, chips = _place()
        for j, (cx, cy) in enumerate(chips):
            cp = _rcopy(s_ref.at[2 * cx + cy], land_ref.at[2 * cx + cy], send_sem.at[j], recv_sem.at[j], (cx, cy, c))
            cp.wait_send()
            cp.wait_recv()

    return pl.pallas_call(
        body, name='chip_exchange_wait',
        out_shape=(pltpu.HBM(s_thru.shape, s_thru.dtype), pltpu.HBM(land_thru.shape, land_thru.dtype)),
        in_specs=(_HBM, _HBM, _SEM, _SEM, ANY), out_specs=(_HBM, _HBM),
        input_output_aliases={0: 0, 1: 1}, compiler_params=_DATAFLOW,
    )(s_thru, land_thru, send, recv, after)


def comm_call(name, stages):
    def body():
        pass

    return _call(body, name=name, grid=(1,), in_specs=[], out_specs=[], out_shape=[], args=[], stages=stages)[1]


def pair_sum(g4, land, c_arr, name):
    hr = g4.shape[2]

    def body(c_ref, g_ref, l_ref, o_ref):
        o_ref[0] = (g_ref[0, 0].astype(F32) + l_ref[0, 0].astype(F32)).astype(BF16)

    return pl.pallas_call(
        body, name=name,
        grid_spec=pltpu.PrefetchScalarGridSpec(
            num_scalar_prefetch=1, grid=(N_CHIPS,),
            in_specs=[pl.BlockSpec((1, 1, hr, D_MODEL), lambda q, c: (q, c[0], 0, 0)),
                      pl.BlockSpec((1, 1, hr, D_MODEL), lambda q, c: (q, 0, 0, 0))],
            out_specs=pl.BlockSpec((1, hr, D_MODEL), lambda q, c: (q, 0, 0))),
        out_shape=jax.ShapeDtypeStruct((N_CHIPS, hr, D_MODEL), BF16),
        compiler_params=_params(1),
    )(c_arr, g4, land)


def small_sum(vec_parts, lru_parts):
    def body(v_ref, l_ref, o_ref):
        for p_ref, lo, n in ((v_ref, 0, ROW_WA), (l_ref, ROW_WA, SMALL_ROWS - ROW_WA)):
            acc = p_ref[0]
            for s in range(1, N_DEV):
                acc = acc + p_ref[s]
            o_ref[lo:lo + n, :] = acc

    return pl.pallas_call(
        body, name='small_sum', grid=(1,),
        in_specs=[pl.BlockSpec(vec_parts.shape, lambda i: (0, 0, 0)), pl.BlockSpec(lru_parts.shape, lambda i: (0, 0, 0))],
        out_specs=pl.BlockSpec((SMALL_ROWS, D_MODEL), lambda i: (0, 0)),
        out_shape=jax.ShapeDtypeStruct((SMALL_ROWS, D_MODEL), F32),
        compiler_params=_params(1),
    )(vec_parts, lru_parts)


def _adam_math(w, g, m, v):
    m2 = ADAM_B1 * m + (1.0 - ADAM_B1) * g
    v2 = ADAM_B2 * v + (1.0 - ADAM_B2) * (g * g)
    m_hat = m2 / (1.0 - ADAM_B1 ** ADAM_STEP)
    v_hat = v2 / (1.0 - ADAM_B2 ** ADAM_STEP)
    delta = -ADAM_LR * (m_hat / (jnp.sqrt(v_hat) + ADAM_EPS) + ADAM_WD * w)
    return delta, m2, v2


def _adam_body(n_parts, transposed, n_after):
    def body(*refs):
        refs = refs[n_after:]
        g_refs = refs[:n_parts]
        w_ref, m_ref, v_ref, go_ref, d_ref, mo_ref, vo_ref = refs[n_parts:]
        def chips_added(blk):
            acc = blk[0].astype(F32)
            for s in range(1, N_CHIPS):
                acc = acc + blk[s].astype(F32)
            return acc

        if transposed:
            g = jnp.concatenate([chips_added(g_ref[h]) for h in range(2) for g_ref in g_refs], axis=0).T
        else:
            rows = [chips_added(g_ref[0]) for g_ref in g_refs]
            g = jnp.concatenate(rows, axis=0) if n_parts > 1 else rows[0]
        go_ref[...] = g
        d_ref[...], mo_ref[...], vo_ref[...] = _adam_math(w_ref[...], g, m_ref[...], v_ref[...])
    return body


def adam_rows(fulls, name, w, m, v, after=()):
    hr = w.shape[0] // 2
    blk = pl.BlockSpec((hr, D_MODEL), lambda h: (h, 0))
    return pl.pallas_call(
        _adam_body(len(fulls), False, len(after)), name='adam_' + name, grid=(2,),
        in_specs=[ANY] * len(after)
        + [pl.BlockSpec((1, N_CHIPS, f.shape[2], D_MODEL), lambda h: (h, 0, 0, 0)) for f in fulls] + [blk, blk, blk],
        out_specs=[blk] * 4,
        out_shape=[jax.ShapeDtypeStruct(w.shape, F32)] * 4,
        compiler_params=_params(1),
    )(*after, *fulls, w, m, v)


def adam_cols(fulls, name, w, m, v, after=()):
    cols = w.shape[1]
    tr = 128
    blk = pl.BlockSpec((tr, cols), lambda i: (i, 0))
    return pl.pallas_call(
        _adam_body(len(fulls), True, len(after)), name='adam_' + name, grid=(D_MODEL // tr,),
        in_specs=[ANY] * len(after)
        + [pl.BlockSpec((2, N_CHIPS, f.shape[2], tr), lambda i: (0, 0, 0, i)) for f in fulls] + [blk, blk, blk],
        out_specs=[blk] * 4,
        out_shape=[jax.ShapeDtypeStruct(w.shape, F32)] * 4,
        compiler_params=_params(1),
    )(*after, *fulls, w, m, v)


def adam_small(g, w, m, v):
    def body(g_ref, w_ref, m_ref, v_ref, d_ref, mo_ref, vo_ref):
        d_ref[...], mo_ref[...], vo_ref[...] = _adam_math(w_ref[...], g_ref[...], m_ref[...], v_ref[...])

    blk = pl.BlockSpec(w.shape, lambda i: (0, 0))
    return pl.pallas_call(
        body, name='adam_small', grid=(1,), in_specs=[blk] * 4, out_specs=[blk] * 3,
        out_shape=[jax.ShapeDtypeStruct(w.shape, F32)] * 3, compiler_params=_params(1),
    )(g, w, m, v)


WEIGHTS = ('ffn1_pre_g', 'ffn1_w_gu', 'ffn1_w_down', 'ffn1_post_g', 'mix_pre_g', 'w_in', 'conv_w', 'conv_b',
           'lru_w_a', 'lru_b_a', 'lru_w_x', 'lru_b_x', 'lru_lambda', 'attn_sinks', 'w_proj_lru', 'w_proj_attn',
           'w_out', 'mix_post_g', 'ffn2_pre_g', 'ffn2_w_gu', 'ffn2_w_down', 'ffn2_post_g')
SMALL = tuple(n for n in WEIGHTS if n not in PACK_OFF)


def cast_t(w, name, stages=()):
    cols = w.shape[1]
    tc = 128

    def body(w_ref, o_ref):
        o_ref[...] = w_ref[...].T.astype(BF16)

    (out,), stage_out = _call(
        body, name=name, grid=(cols // tc,),
        in_specs=[pl.BlockSpec((D_MODEL, tc), lambda j: (0, j))],
        out_specs=[pl.BlockSpec((tc, D_MODEL), lambda j: (j, 0))],
        out_shape=[jax.ShapeDtypeStruct((cols, D_MODEL), BF16)],
        args=[w], stages=stages)
    return out, stage_out


def _pack_vecs(d, conv_rows):
    sinks = jnp.pad(d['attn_sinks'].reshape(1, N_Q_HEADS), ((0, 0), (0, D_MODEL - N_Q_HEADS)))
    conv = jnp.pad(conv_rows, ((0, ROW_WA - ROW_CONV - conv_rows.shape[0]), (0, 0)))
    return jnp.concatenate([d[n].reshape(1, D_MODEL) for n in SMALL_VECS] + [sinks, conv], axis=0)


def _pack_lru(d):
    return jnp.concatenate([d['lru_w_a'].reshape(64, D_MODEL), d['lru_w_x'].reshape(64, D_MODEL)], axis=0)


def _pack_small(d, conv_rows):
    return jnp.concatenate([_pack_vecs(d, conv_rows), _pack_lru(d)], axis=0)


def _unpack_small(p, shapes):
    out = {n: p[k:k + 1].reshape(shapes[n]) for k, n in enumerate(SMALL_VECS)}
    out['attn_sinks'] = p[ROW_SINKS:ROW_SINKS + 1, :N_Q_HEADS].reshape(shapes['attn_sinks'])
    out['conv_w'] = p[ROW_CONV:ROW_CONV + 1].reshape(shapes['conv_w'])
    out['lru_w_a'] = p[ROW_WA:ROW_WA + 64].reshape(shapes['lru_w_a'])
    out['lru_w_x'] = p[ROW_WX:ROW_WX + 64].reshape(shapes['lru_w_x'])
    return out


def kernel(x, ffn1_pre_g, ffn1_w_gu, ffn1_w_down, ffn1_post_g, mix_pre_g, w_in, conv_w, conv_b, lru_w_a, lru_b_a, lru_w_x, lru_b_x, lru_lambda, attn_sinks, w_proj_lru, w_proj_attn, w_out, mix_post_g, ffn2_pre_g, ffn2_w_gu, ffn2_w_down, ffn2_post_g, loss_target, m_ffn1_pre_g, m_ffn1_w_gu, m_ffn1_w_down, m_ffn1_post_g, m_mix_pre_g, m_w_in, m_conv_w, m_conv_b, m_lru_w_a, m_lru_b_a, m_lru_w_x, m_lru_b_x, m_lru_lambda, m_attn_sinks, m_w_proj_lru, m_w_proj_attn, m_w_out, m_mix_post_g, m_ffn2_pre_g, m_ffn2_w_gu, m_ffn2_w_down, m_ffn2_post_g, v_ffn1_pre_g, v_ffn1_w_gu, v_ffn1_w_down, v_ffn1_post_g, v_mix_pre_g, v_w_in, v_conv_w, v_conv_b, v_lru_w_a, v_lru_b_a, v_lru_w_x, v_lru_b_x, v_lru_lambda, v_attn_sinks, v_w_proj_lru, v_w_proj_attn, v_w_out, v_mix_post_g, v_ffn2_pre_g, v_ffn2_w_gu, v_ffn2_w_down, v_ffn2_post_g):
    given = dict(locals())
    w = {n: given[n] for n in WEIGHTS}
    mom = {n: given['m_' + n] for n in WEIGHTS}
    var = {n: given['v_' + n] for n in WEIGHTS}
    shapes = {n: w[n].shape for n in WEIGHTS}
    xq = lax.axis_index('x')
    yq = lax.axis_index('y')
    cq = lax.axis_index('c')
    me_q = 2 * xq + yq

    c_arr = cq.reshape(1).astype(jnp.int32)
    xs, target = x[0], loss_target[0]
    sw = {n: (w[n][0] if w[n].ndim > 2 else w[n]) for n in SMALL}
    cos, sin_signed = _rope_tables()
    wa_bd = _block_diag(sw['lru_w_a'])
    wx_bd = _block_diag(sw['lru_w_x'])
    sinks = sw['attn_sinks'].reshape(N_Q_HEADS)

    shard = {n: w[n][0].astype(BF16) for n, _, t in PACK if not t}
    conv_pad = jnp.pad(w['conv_w'][0], ((0, 4), (0, 0)))

    def whole(name):
        return (shard[name], 0, PACK_ROWS_OF[name])

    def part(name, p, n_parts=2):
        rows = PACK_ROWS_OF[name] // n_parts
        return (shard[name], p * rows, rows)

    shard['ffn1_w_gu'], _ = cast_t(w['ffn1_w_gu'][0], 'cast_ffn1_w_gu')
    shard['w_in'], ((w_gu1a,), (conv_all,)) = cast_t(
        w['w_in'][0], 'cast_w_in', stages=[GatherStage([part('ffn1_w_gu', 0)]), SmallGatherStage(conv_pad)])
    shard['ffn2_w_gu'], ((w_gu1b,),) = cast_t(w['ffn2_w_gu'][0], 'cast_ffn2_w_gu', stages=[GatherStage([part('ffn1_w_gu', 1)])])
    w_gu1 = [w_gu1a, w_gu1b]
    sw['conv_w'] = jnp.transpose(conv_all[0::2, :4, :], (1, 0, 2)).reshape(4, LRU_W)
    proj_names = ['w_proj_lru', 'w_proj_attn', 'w_out']

    (n1, g1, u1, a1), ((w_down1, w_in_a),) = ffn_fwd_a(xs, sw['ffn1_pre_g'], w_gu1, 'ffn1_fwd_a',
                                                        stages=[GatherStage([whole('ffn1_w_down'), part('w_in', 0)])])
    (f1, h1), ((w_in_b,),) = ffn_fwd_b(a1, w_down1, sw['ffn1_post_g'], xs, 'ffn1_fwd_b',
                                       stages=[GatherStage([part('w_in', 1)])])
    w_in_t = [w_in_a, w_in_b]
    (um, gate, xbr, q, k, v, g_lru, g_attn), ((w_gu2a,),) = mix_in(h1, sw['mix_pre_g'], w_in_t, 'mix_in',
                                                                   stages=[GatherStage([part('ffn2_w_gu', 0)])])
    (y_lru, h_lru), ((w_gu2b,),) = lru_fwd(gate, xbr, sw['conv_w'], sw['conv_b'], wa_bd, sw['lru_b_a'], wx_bd, sw['lru_b_x'],
                                           sw['lru_lambda'], 'lru_fwd', stages=[GatherStage([part('ffn2_w_gu', 1)])])
    (qr, kr, y_attn), (projs,) = attn_fwd(q, k, v, cos, sin_signed, sinks, 'attn_fwd',
                                          stages=[GatherStage([whole(n) for n in proj_names])])
    (p_l, p_a, merged, m, h2), ((w_down2,),) = merge_fwd(y_lru, y_attn, g_lru, g_attn, projs, sw['mix_post_g'], h1, 'merge_fwd',
                                                         stages=[GatherStage([whole('ffn2_w_down')])])
    w_gu2 = [w_gu2a, w_gu2b]
    (n2, g2, u2, a2), _ = ffn_fwd_a(h2, sw['ffn2_pre_g'], w_gu2, 'ffn2_fwd_a')
    (f2, dy, loss_blk), _ = ffn_fwd_b(a2, w_down2, sw['ffn2_post_g'], h2, 'ffn2_fwd_b', target=target)

    gs, full = {}, {}

    def pair_stage(names, grads):
        g4 = [g.reshape(N_CHIPS, 2, PACK_ROWS_OF[n] // 2, D_MODEL) for n, g in zip(names, grads)]
        return PairStage(g4), g4

    def pair_sums(names, g4, lands):
        return [pair_sum(g, l, c_arr, 'pair_sum_' + n) for n, g, l in zip(names, g4, lands)]

    def halves(s, n_parts=2):
        n = s.shape[1] // n_parts
        return [(s, p * n, n) for p in range(n_parts)]

    (df2, dgu2, gs['ffn2_post_g']), _ = ffn_bwd_a(dy, f2, sw['ffn2_post_g'], w_down2, g2, u2, 'ffn2_bwd_a')
    g_down2, _ = mm_tn([a2], df2, 1408, 'ffn2_dw_down')
    st, g4 = pair_stage(['ffn2_w_down'], [g_down2])
    g_gu2, (lands,) = mm_tn([dgu2], n2, 1408, 'ffn2_dw_gu', stages=[st])
    (s_down2,) = pair_sums(['ffn2_w_down'], g4, lands)
    st, g4 = pair_stage(['ffn2_w_gu'], [g_gu2])
    (dh2, gs['ffn2_pre_g']), ((l_down2,), lands) = norm_bwd([dgu2], w_gu2, h2, sw['ffn2_pre_g'], dy, 'ffn2_bwd_b',
                                                            stages=[ChipStage([(s_down2, 0, s_down2.shape[1])]), st])
    (s_gu2,) = pair_sums(['ffn2_w_gu'], g4, lands)

    (dm, dpl, dpa, dgl, dga, dya, dyl, gs['mix_post_g']), ((l_gu2a,),) = merge_bwd(
        dh2, m, sw['mix_post_g'], projs, g_lru, g_attn, p_l, p_a, 'merge_bwd', stages=[ChipStage(halves(s_gu2)[:1])])
    g_projs = [mm_tn([merged if n == 'w_out' else (y_lru if n == 'w_proj_lru' else y_attn)],
                     dm if n == 'w_out' else (dpl if n == 'w_proj_lru' else dpa), D_MODEL, 'd' + n)[0] for n in proj_names]
    st, g4 = pair_stage(proj_names, g_projs)
    (dq, dkv, dsk), ((l_gu2b,), lands, (full['ffn2_w_down'],)) = attn_bwd(
        qr, kr, v, dya, cos, sin_signed, sinks, 'attn_bwd', stages=[ChipStage(halves(s_gu2)[1:]), st, SwapStage([l_down2])])
    full['ffn2_w_down'] = [full['ffn2_w_down']]
    gs['attn_sinks'] = dsk[0:1, 0:N_Q_HEADS]
    s_projs = pair_sums(proj_names, g4, lands)
    (dgate, dxbr, vecs, dwa, dwx), (l_projs, full['ffn2_w_gu']) = lru_bwd(
        gate, xbr, h_lru, dyl, sw['conv_w'], sw['conv_b'], wa_bd, sw['lru_b_a'], wx_bd, sw['lru_b_x'], sw['lru_lambda'],
        'lru_bwd', stages=[ChipStage([(s, 0, s.shape[1]) for s in s_projs]), SwapStage([l_gu2a, l_gu2b])])
    gs['conv_w'] = vecs[0:4]
    gs['conv_b'], gs['lru_b_a'], gs['lru_b_x'], gs['lru_lambda'] = vecs[4:5], vecs[5:6], vecs[6:7], vecs[7:8]
    gs['lru_w_a'] = _diag_blocks(dwa)
    gs['lru_w_x'] = _diag_blocks(dwx)
    dz = [dgate, dxbr, dq, dkv, dgl, dga]
    g_in, ((lru_all,),) = mm_tn(dz, um, 512, 'dw_in', stages=[SmallGatherStage(_pack_lru(gs))])
    st, g4 = pair_stage(['w_in'], [g_in])
    (dh1, gs['mix_pre_g']), (lands, f_projs) = norm_bwd(dz, w_in_t, h1, sw['mix_pre_g'], dh2, 'mix_bwd_in',
                                                        stages=[st, SwapStage(l_projs)])
    for n, f in zip(proj_names, f_projs):
        full[n] = [f]
    (s_in,) = pair_sums(['w_in'], g4, lands)

    (df1, dgu1, gs['ffn1_post_g']), ((l_in_a,),) = ffn_bwd_a(dh1, f1, sw['ffn1_post_g'], w_down1, g1, u1, 'ffn1_bwd_a',
                                                             stages=[ChipStage(halves(s_in)[:1])])
    g_down1, _ = mm_tn([a1], df1, 1408, 'ffn1_dw_down')
    st, g4 = pair_stage(['ffn1_w_down'], [g_down1])
    g_gu1, ((l_in_b,), lands) = mm_tn([dgu1], n1, 1408, 'ffn1_dw_gu', stages=[ChipStage(halves(s_in)[1:]), st])
    (s_down1,) = pair_sums(['ffn1_w_down'], g4, lands)
    st, g4 = pair_stage(['ffn1_w_gu'], [g_gu1])
    (dx, gs['ffn1_pre_g']), ((l_down1,), lands, full['w_in']) = norm_bwd(
        [dgu1], w_gu1, xs, sw['ffn1_pre_g'], dh1, 'ffn1_bwd_b',
        stages=[ChipStage([(s_down1, 0, s_down1.shape[1])]), st, SwapStage([l_in_a, l_in_b])])
    (s_gu1,) = pair_sums(['ffn1_w_gu'], g4, lands)
    loss_row = jnp.pad(loss_blk[0:1], ((0, 0), (0, D_MODEL - loss_blk.shape[1])))
    vec_blk = _pack_vecs(gs, jnp.concatenate([gs['conv_w'], loss_row], axis=0))
    send, recv, s_thru, land_thru, token = chip_exchange_start(s_gu1)
    out_g, out_d, out_m, out_v = {}, {}, {}, {}

    def adam(n, after=()):
        fn = adam_cols if dict((k, t) for k, _, t in PACK)[n] else adam_rows
        g_, d_, m_, v_ = fn(full[n], n, w[n][0], mom[n][0], var[n][0], after=after)
        out_g[n], out_d[n], out_m[n], out_v[n] = g_[None], d_[None], m_[None], v_[None]

    behind = token
    for n in ['ffn2_w_gu', 'w_in', 'ffn2_w_down'] + proj_names:
        adam(n, after=(behind,))
        behind = out_v[n]
    s_back, l_gu1 = chip_exchange_wait(send, recv, s_thru, land_thru, after=behind)
    own = lax.dynamic_slice_in_dim(s_back, me_q, 1, axis=0)
    l_gu1 = lax.dynamic_update_slice_in_dim(l_gu1, own, me_q, axis=0)
    (vec_all,), (f_down1, f_gu1) = comm_call('swap_last', [SmallGatherStage(vec_blk), SwapStage([l_down1, l_gu1])])
    full['ffn1_w_down'] = [f_down1]
    full['ffn1_w_gu'] = [f_gu1]
    adam('ffn1_w_gu')
    adam('ffn1_w_down')

    tot = small_sum(vec_all, lru_all)
    loss = tot[ROW_WA - 1, 0]
    conv_g = lax.dynamic_slice(tot[ROW_CONV:ROW_CONV + 4], (0, me_q * (LRU_W // N_CHIPS)), (4, LRU_W // N_CHIPS))
    small_g = _unpack_small(tot, shapes)
    small_g['conv_w'] = conv_g.reshape(shapes['conv_w'])
    g_pack = jnp.concatenate([tot[:ROW_CONV], conv_g.reshape(1, D_MODEL), jnp.zeros((ROW_WA - ROW_CONV - 1, D_MODEL), F32),
                              tot[ROW_WA:]], axis=0)
    packs = [_pack_small({n: d[n] for n in SMALL}, d['conv_w'].reshape(1, D_MODEL)) for d in (w, mom, var)]
    d_p, m_p, v_p = adam_small(g_pack, *packs)
    for n in SMALL:
        out_g[n] = small_g[n]
    for dst, p in ((out_d, d_p), (out_m, m_p), (out_v, v_p)):
        dst.update(_unpack_small(p, shapes))

    return (loss, dx[None], *[out_g[n] for n in WEIGHTS], *[out_d[n] for n in WEIGHTS],
            *[out_m[n] for n in WEIGHTS], *[out_v[n] for n in WEIGHTS])
```

```python
import jax
import jax.numpy as jnp
import numpy as np
from jax import lax
from jax.experimental import pallas as pl
from jax.experimental.pallas import tpu as pltpu

F32 = jnp.float32
BF16 = jnp.bfloat16

SEQ = 2048
D_MODEL = 1024
D_FF = 2816
LRU_W = 1024
LRU_BLOCK_W = 64
HEAD_DIM = 64
N_Q_HEADS = 16
N_KV_HEADS = 4
KV_W = N_KV_HEADS * HEAD_DIM
ATTN_BLOCK = 128
N_ATTN_BLOCKS = SEQ // ATTN_BLOCK
IN_SEGS = (1024, 1024, 1024, 256, 256, 1024, 1024)
IN_W = sum(IN_SEGS)
NORM_EPS = 1e-6
MASK_VALUE = -1e30
ROPE_THETA = 10000.0
LRU_C = 8.0
MACARON = 0.5
ADAM_LR = 0.001
ADAM_B1 = 0.9
ADAM_B2 = 0.999
ADAM_EPS = 1e-08
ADAM_WD = 0.01
ADAM_STEP = 10

N_CHIPS = 4
N_DEV = 8
VMEM_LIMIT = 56 * 1024 * 1024
MM_ROWS = 256
MESH = pl.DeviceIdType.MESH
ANY = pl.BlockSpec(memory_space=pl.ANY)

PACK = (('ffn1_w_gu', 1408, True), ('w_in', 1408, True), ('ffn2_w_gu', 1408, True),
        ('ffn1_w_down', 704, False), ('ffn2_w_down', 704, False),
        ('w_proj_lru', 256, False), ('w_proj_attn', 256, False), ('w_out', 256, False))
PACK_ROWS_OF = {n: r for n, r, _ in PACK}
PACK_OFF = {}
_o = 0
for _n, _r, _t in PACK:
    PACK_OFF[_n] = _o
    _o += _r

SMALL_VECS = ('ffn1_pre_g', 'ffn1_post_g', 'mix_pre_g', 'conv_b', 'lru_b_a', 'lru_b_x', 'lru_lambda',
              'mix_post_g', 'ffn2_pre_g', 'ffn2_post_g')
SMALL_ROWS = 144
ROW_SINKS, ROW_CONV, ROW_WA, ROW_WX = 10, 11, 16, 80


def _dot(a, b):
    return jnp.dot(a, b, preferred_element_type=F32)


def _dot_nt(a, b):
    return lax.dot_general(a, b, (((1,), (1,)), ((), ())), preferred_element_type=F32)


def _dot_tn(a, b):
    return lax.dot_general(a, b, (((0,), (0,)), ((), ())), preferred_element_type=F32)


def _params(n_grid):
    return pltpu.CompilerParams(dimension_semantics=("arbitrary",) * n_grid, vmem_limit_bytes=VMEM_LIMIT)


def _sigmoid(x):
    return 1.0 / (1.0 + jnp.exp(-x))


def _rsqrt_mean_sq(x):
    return lax.rsqrt(jnp.mean(x * x, axis=-1, keepdims=True) + NORM_EPS)


def _expm1(x):
    poly = x * (1.0 + x * (0.5 + x * (1.0 / 6.0)))
    return jnp.where(jnp.abs(x) < 0.02, poly, jnp.exp(x) - 1.0)


_GELU_K = 0.7978845608028654
_GELU_C = 0.044715


def _gelu(x):
    t = jnp.tanh(_GELU_K * (x + _GELU_C * x * x * x))
    return 0.5 * x * (1.0 + t), t


def _gelu_grad(x, t):
    return 0.5 * (1.0 + t) + 0.5 * x * (1.0 - t * t) * _GELU_K * (1.0 + 3.0 * _GELU_C * x * x)


def _load_weight(w_refs, dst_ref, sem):
    w_refs = list(w_refs) if isinstance(w_refs, (list, tuple)) else [w_refs]
    rows = dst_ref.shape[0] // N_CHIPS
    rp = rows // len(w_refs)
    cps = [pltpu.make_async_copy(w_ref.at[q], dst_ref.at[pl.ds(q * rows + p * rp, rp)], sem.at[p * N_CHIPS + q])
           for p, w_ref in enumerate(w_refs) for q in range(N_CHIPS)]
    for cp in cps:
        cp.start()
    for cp in cps:
        cp.wait()


def _weight_scratch(rows_total, parts=1):
    return [pltpu.VMEM((rows_total, D_MODEL), BF16), pltpu.SemaphoreType.DMA((N_CHIPS * parts,))]


_ROW = lambda tm: pl.BlockSpec((tm, D_MODEL), lambda i: (i, 0))
_VEC = pl.BlockSpec((1, D_MODEL), lambda i: (0, 0))


def _call(body, *, name, grid, in_specs, out_specs, out_shape, args, scratch_shapes=(), stages=()):
    in_specs, out_specs, out_shape, scratch_shapes = list(in_specs), list(out_specs), list(out_shape), list(scratch_shapes)
    n_in, n_out, n_sc = len(in_specs), len(out_specs), len(scratch_shapes)
    k_in = [len(s.inputs) for s in stages]
    k_out = [len(s.out_shape) for s in stages]
    k_sc = [len(s.scratch) for s in stages]
    last = grid[0] - 1

    def split(refs, counts):
        parts, pos = [], 0
        for k in counts:
            parts.append(refs[pos:pos + k])
            pos += k
        return parts

    kinds = tuple(sorted({k for s in stages for k in s.peers}))
    collective_id = {(): None, ('sib',): 0, ('chips',): 1, ('chips', 'sib'): 2}[kinds]

    def full(*refs):
        ins, s_ins, outs, s_outs, scr, s_scr = split(refs, [n_in, sum(k_in), n_out, sum(k_out), n_sc, sum(k_sc)])
        per_stage = list(zip(stages, split(s_ins, k_in), split(s_outs, k_out), split(s_scr, k_sc)))
        i = pl.program_id(0)
        if stages:
            @pl.when(i == 0)
            def _():
                x, y, c, chips = _place()
                peers = ([(x, y, 1 - c)] if 'sib' in kinds else []) + ([(cx, cy, c) for cx, cy in chips] if 'chips' in kinds else [])
                barrier = pltpu.get_barrier_semaphore()
                for peer in peers:
                    pl.semaphore_signal(barrier, inc=1, device_id=peer, device_id_type=MESH)
                pl.semaphore_wait(barrier, len(peers))
                for s, a, b, c_ in per_stage:
                    s.start(a, b, c_)

        body(*ins, *outs, *scr)
        if stages:
            @pl.when(i == last // 2)
            def _():
                for s, a, b, c in per_stage:
                    s.relay(a, b, c)

            @pl.when(i == max(last - 1, 0))
            def _():
                for s, a, b, c in per_stage:
                    s.mid(a, b, c)

            @pl.when(i == last)
            def _():
                for s, a, b, c in per_stage:
                    s.end(a, b, c)

    res = pl.pallas_call(
        full, name=name, grid=grid,
        in_specs=in_specs + [ANY] * sum(k_in),
        out_specs=out_specs + [ANY] * sum(k_out),
        out_shape=out_shape + [o for s in stages for o in s.out_shape],
        scratch_shapes=scratch_shapes + [x for s in stages for x in s.scratch],
        compiler_params=pltpu.CompilerParams(dimension_semantics=("arbitrary",), vmem_limit_bytes=VMEM_LIMIT,
                                             collective_id=collective_id),
    )(*args, *[a for s in stages for a in s.inputs])
    return list(res[:n_out]), split(list(res[n_out:]), k_out)


def ffn_fwd_a(x, g_pre, w_gu_t, name, stages=()):
    tm, tn = MM_ROWS, 256
    n_w = len(w_gu_t)

    def body(x_ref, gp_ref, *refs):
        w_refs = refs[:n_w]
        n_ref, g_ref, u_ref, a_ref, wt_ref, sem = refs[n_w:]

        @pl.when(pl.program_id(0) == 0)
        def _():
            _load_weight(w_refs, wt_ref, sem)

        xv = x_ref[...]
        n = (xv * _rsqrt_mean_sq(xv) * gp_ref[...]).astype(BF16)
        n_ref[...] = n
        for j in range(D_FF // tn):
            g = _dot_nt(n, wt_ref[j * tn:(j + 1) * tn, :])
            u = _dot_nt(n, wt_ref[D_FF + j * tn:D_FF + (j + 1) * tn, :])
            g_ref[:, j * tn:(j + 1) * tn] = g.astype(BF16)
            u_ref[:, j * tn:(j + 1) * tn] = u.astype(BF16)
            a_ref[:, j * tn:(j + 1) * tn] = (g * _sigmoid(g) * u).astype(BF16)

    wide = pl.BlockSpec((tm, D_FF), lambda i: (i, 0))
    return _call(
        body, name=name, grid=(SEQ // tm,),
        in_specs=[_ROW(tm), _VEC] + [ANY] * n_w,
        out_specs=[_ROW(tm), wide, wide, wide],
        out_shape=[jax.ShapeDtypeStruct((SEQ, D_MODEL), BF16)] + [jax.ShapeDtypeStruct((SEQ, D_FF), BF16)] * 3,
        scratch_shapes=_weight_scratch(2 * D_FF, n_w),
        args=[x, g_pre, *w_gu_t], stages=stages)


def ffn_fwd_b(a, w_down, g_post, h_in, name, target=None, stages=()):
    tm = MM_ROWS
    final = target is not None

    def body(*refs):
        if final:
            a_ref, wf_ref, gp_ref, h_ref, t_ref, f_ref, o_ref, loss_ref, wd_ref, sem = refs
        else:
            a_ref, wf_ref, gp_ref, h_ref, f_ref, o_ref, wd_ref, sem = refs

        @pl.when(pl.program_id(0) == 0)
        def _():
            _load_weight(wf_ref, wd_ref, sem)
            if final:
                loss_ref[...] = jnp.zeros_like(loss_ref)

        f = _dot(a_ref[...], wd_ref[...])
        f_ref[...] = f
        y = h_ref[...] + MACARON * (f * _rsqrt_mean_sq(f) * gp_ref[...])
        if final:
            err = y - t_ref[...]
            o_ref[...] = err * (1.0 / D_MODEL)
            loss_ref[...] += 0.5 * jnp.sum(err * err) * (1.0 / D_MODEL)
        else:
            o_ref[...] = y

    row = _ROW(tm)
    in_specs = [pl.BlockSpec((tm, D_FF), lambda i: (i, 0)), ANY, _VEC, row]
    out_specs = [row, row]
    out_shape = [jax.ShapeDtypeStruct((SEQ, D_MODEL), F32)] * 2
    args = [a, w_down, g_post, h_in]
    if final:
        in_specs.append(row)
        args.append(target)
        out_specs.append(pl.BlockSpec((8, 128), lambda i: (0, 0)))
        out_shape.append(jax.ShapeDtypeStruct((8, 128), F32))
    return _call(body, name=name, grid=(SEQ // tm,), in_specs=in_specs, out_specs=out_specs,
                 out_shape=out_shape, scratch_shapes=_weight_scratch(D_FF), args=args, stages=stages)


def ffn_bwd_a(d_out, f, g_post, w_down, g, u, name, stages=()):
    tm = MM_ROWS
    tc = 256

    def body(do_ref, f_ref, gp_ref, wf_ref, g_ref, u_ref, df_ref, dgu_ref, dgp_ref, wd_ref, sem):
        @pl.when(pl.program_id(0) == 0)
        def _():
            _load_weight(wf_ref, wd_ref, sem)
            dgp_ref[...] = jnp.zeros_like(dgp_ref)

        fv = f_ref[...]
        rf = _rsqrt_mean_sq(fv)
        fh = fv * rf
        dn = MACARON * do_ref[...]
        dgp_ref[...] += jnp.sum(dn * fh, axis=0, keepdims=True)
        t = dn * gp_ref[...]
        df = (rf * (t - fh * jnp.mean(t * fh, axis=-1, keepdims=True))).astype(BF16)
        df_ref[...] = df
        for c0 in range(0, D_FF, tc):
            da = _dot_nt(df, wd_ref[c0:c0 + tc, :])
            gv = g_ref[:, c0:c0 + tc].astype(F32)
            uv = u_ref[:, c0:c0 + tc].astype(F32)
            s = _sigmoid(gv)
            dgu_ref[:, c0:c0 + tc] = (da * uv * s * (1.0 + gv * (1.0 - s))).astype(BF16)
            dgu_ref[:, D_FF + c0:D_FF + c0 + tc] = (da * gv * s).astype(BF16)

    row = _ROW(tm)
    wide = pl.BlockSpec((tm, D_FF), lambda i: (i, 0))
    return _call(
        body, name=name, grid=(SEQ // tm,),
        in_specs=[row, row, _VEC, ANY, wide, wide],
        out_specs=[row, pl.BlockSpec((tm, 2 * D_FF), lambda i: (i, 0)), _VEC],
        out_shape=[jax.ShapeDtypeStruct((SEQ, D_MODEL), BF16), jax.ShapeDtypeStruct((SEQ, 2 * D_FF), BF16),
                   jax.ShapeDtypeStruct((1, D_MODEL), F32)],
        scratch_shapes=_weight_scratch(D_FF),
        args=[d_out, f, g_post, w_down, g, u], stages=stages)


def norm_bwd(pieces, w_t, x, g_pre, d_res, name, stages=()):
    tm = MM_ROWS
    widths = [p.shape[1] for p in pieces]
    offs = [sum(widths[:k]) for k in range(len(widths))]
    n_p = len(pieces)
    n_w = len(w_t)

    def body(*refs):
        p_refs = refs[:n_p]
        w_refs = refs[n_p:n_p + n_w]
        x_ref, g_ref, r_ref, dx_ref, dg_ref, wt_ref, sem = refs[n_p + n_w:]

        @pl.when(pl.program_id(0) == 0)
        def _():
            _load_weight(w_refs, wt_ref, sem)
            dg_ref[...] = jnp.zeros_like(dg_ref)

        dn = None
        for p_ref, lo, wd in zip(p_refs, offs, widths):
            part = _dot(p_ref[...], wt_ref[lo:lo + wd, :])
            dn = part if dn is None else dn + part
        xv = x_ref[...]
        r = _rsqrt_mean_sq(xv)
        xh = xv * r
        dg_ref[...] += jnp.sum(dn * xh, axis=0, keepdims=True)
        t = dn * g_ref[...]
        dx_ref[...] = r_ref[...] + r * (t - xh * jnp.mean(t * xh, axis=-1, keepdims=True))

    row = _ROW(tm)
    return _call(
        body, name=name, grid=(SEQ // tm,),
        in_specs=[pl.BlockSpec((tm, wd), lambda i: (i, 0)) for wd in widths] + [ANY] * n_w + [row, _VEC, row],
        out_specs=[row, _VEC],
        out_shape=[jax.ShapeDtypeStruct((SEQ, D_MODEL), F32), jax.ShapeDtypeStruct((1, D_MODEL), F32)],
        scratch_shapes=_weight_scratch(sum(widths), n_w),
        args=[*pieces, *w_t, x, g_pre, d_res], stages=stages)


def mm_tn(pieces, b, tm, name, stages=()):
    widths = [p.shape[1] for p in pieces]
    m_total = sum(widths)
    n_p = len(pieces)
    starts = [sum(widths[:k]) // tm for k in range(n_p)]
    counts = [wd // tm for wd in widths]

    def body(*refs):
        p_refs = refs[:n_p]
        b_ref, o_ref = refs[n_p:]
        i = pl.program_id(0)
        for p_ref, st, ct in zip(p_refs, starts, counts):
            @pl.when((i >= st) & (i < st + ct))
            def _(p_ref=p_ref):
                o_ref[...] = _dot_tn(p_ref[...], b_ref[...]).astype(BF16)

    def piece_spec(st, ct):
        return pl.BlockSpec((SEQ, tm), lambda i: (0, jnp.clip(i - st, 0, ct - 1)))

    (out,), stage_out = _call(
        body, name=name, grid=(m_total // tm,),
        in_specs=[piece_spec(st, ct) for st, ct in zip(starts, counts)] + [pl.BlockSpec((SEQ, D_MODEL), lambda i: (0, 0))],
        out_specs=[pl.BlockSpec((tm, D_MODEL), lambda i: (i, 0))],
        out_shape=[jax.ShapeDtypeStruct((m_total, D_MODEL), BF16)],
        args=[*pieces, b], stages=stages)
    return out, stage_out


def mix_in(h, g_pre, w_in_t, name, stages=()):
    tm = MM_ROWS
    offs = [sum(IN_SEGS[:k]) for k in range(len(IN_SEGS))]
    dts = [F32, F32, F32, F32, BF16, F32, F32]
    n_o = len(IN_SEGS)
    n_w = len(w_in_t)

    def body(*refs):
        h_ref, g_ref = refs[:2]
        w_refs = refs[2:2 + n_w]
        um_ref = refs[2 + n_w]
        o_refs = refs[3 + n_w:3 + n_w + n_o]
        wt_ref, sem = refs[3 + n_w + n_o:]

        @pl.when(pl.program_id(0) == 0)
        def _():
            _load_weight(w_refs, wt_ref, sem)

        hv = h_ref[...]
        um = (hv * _rsqrt_mean_sq(hv) * g_ref[...]).astype(BF16)
        um_ref[...] = um
        for o_ref, lo, wd in zip(o_refs, offs, IN_SEGS):
            for c0 in range(0, wd, 256):
                o_ref[:, c0:c0 + 256] = _dot_nt(um, wt_ref[lo + c0:lo + c0 + 256, :]).astype(o_ref.dtype)

    return _call(
        body, name=name, grid=(SEQ // tm,),
        in_specs=[_ROW(tm), _VEC] + [ANY] * n_w,
        out_specs=[_ROW(tm)] + [pl.BlockSpec((tm, wd), lambda i: (i, 0)) for wd in IN_SEGS],
        out_shape=[jax.ShapeDtypeStruct((SEQ, D_MODEL), BF16)]
        + [jax.ShapeDtypeStruct((SEQ, wd), dt) for wd, dt in zip(IN_SEGS, dts)],
        scratch_shapes=_weight_scratch(IN_W, n_w),
        args=[h, g_pre, *w_in_t], stages=stages)


LRU_TC = 256


def _conv_fwd(xb, cw, cb, tt):
    xc = xb * cw[3:4, :] + cb
    shifted = []
    for s in (1, 2, 3):
        sh = jnp.where(tt >= s, pltpu.roll(xb, s, 0), 0.0)
        shifted.append(sh)
        xc = xc + sh * cw[3 - s:4 - s, :]
    return xc, shifted


def _lru_gates(xc, wa, ba, wx, bx, lam):
    xcb = xc.astype(BF16)
    r = _sigmoid(_dot(xcb, wa) + ba)
    i = _sigmoid(_dot(xcb, wx) + bx)
    nl = -lam
    sp = jnp.maximum(nl, 0.0) + jnp.log1p(jnp.exp(-jnp.abs(nl)))
    la = (-LRU_C * r) * sp
    a = jnp.exp(la)
    mult = jnp.sqrt(jnp.maximum(-_expm1(2.0 * la), 0.0))
    return xcb, r, i, sp, a, mult


def _scan(a, b, tt, reverse, a_s, b_s):
    n = a.shape[0]
    tg = tt & 7
    for s in (1, 2, 4):
        keep = (tg < 8 - s) if reverse else (tg >= s)
        shift = n - s if reverse else s
        b = a * jnp.where(keep, pltpu.roll(b, shift, 0), 0.0) + b
        a = a * jnp.where(keep, pltpu.roll(a, shift, 0), 1.0)
    a_s[...] = a
    b_s[...] = b
    groups = n // 8

    def step(g, carry):
        gi = (groups - 1 - g) if reverse else g
        rows = pl.ds(pl.multiple_of(gi * 8, 8), 8)
        hg = a_s[rows, :] * carry + b_s[rows, :]
        b_s[rows, :] = hg
        return hg[0:1, :] if reverse else hg[7:8, :]

    lax.fori_loop(0, groups, step, jnp.zeros((1, a.shape[1]), F32), unroll=8)
    return b_s[...]


def _lru_specs():
    col = pl.BlockSpec((SEQ, LRU_TC), lambda j: (0, j))
    vec = pl.BlockSpec((1, LRU_TC), lambda j: (0, j))
    bd = pl.BlockSpec((1, LRU_TC, LRU_TC), lambda j: (j, 0, 0))
    cw = pl.BlockSpec((4, LRU_TC), lambda j: (0, j))
    return col, vec, bd, cw


def lru_fwd(gate, xbr, conv_w, conv_b, wa_bd, b_a, wx_bd, b_x, lam, name, stages=()):
    col, vec, bd, cw = _lru_specs()

    def body(gate_ref, xbr_ref, cw_ref, cb_ref, wa_ref, ba_ref, wx_ref, bx_ref, lam_ref, y_ref, h_ref, a_s, b_s):
        tt = lax.broadcasted_iota(jnp.int32, (SEQ, LRU_TC), 0)
        xc, _ = _conv_fwd(xbr_ref[...], cw_ref[...], cb_ref[...], tt)
        _, r, i, sp, a, mult = _lru_gates(xc, wa_ref[0], ba_ref[...], wx_ref[0], bx_ref[...], lam_ref[...])
        h = _scan(a, mult * (i * xc), tt, False, a_s, b_s)
        h_ref[...] = h
        gl, _ = _gelu(gate_ref[...])
        y_ref[...] = (h * gl).astype(BF16)

    return _call(
        body, name=name, grid=(LRU_W // LRU_TC,),
        in_specs=[col, col, cw, vec, bd, vec, bd, vec, vec],
        out_specs=[col, col],
        out_shape=[jax.ShapeDtypeStruct((SEQ, LRU_W), BF16), jax.ShapeDtypeStruct((SEQ, LRU_W), F32)],
        scratch_shapes=[pltpu.VMEM((SEQ, LRU_TC), F32)] * 2,
        args=[gate, xbr, conv_w, conv_b, wa_bd, b_a, wx_bd, b_x, lam], stages=stages)


def lru_bwd(gate, xbr, h, dy, conv_w, conv_b, wa_bd, b_a, wx_bd, b_x, lam, name, stages=()):
    col, vec, bd, cw = _lru_specs()

    def body(gate_ref, xbr_ref, h_ref, dy_ref, cw_ref, cb_ref, wa_ref, ba_ref, wx_ref, bx_ref, lam_ref,
             dgate_ref, dxbr_ref, vecs_ref, dwa_ref, dwx_ref, a_s, b_s):
        tt = lax.broadcasted_iota(jnp.int32, (SEQ, LRU_TC), 0)
        cwv = cw_ref[...]
        lam = lam_ref[...]
        xb = xbr_ref[...]
        xc, shifted = _conv_fwd(xb, cwv, cb_ref[...], tt)
        wa = wa_ref[0]
        wx = wx_ref[0]
        xcb, r, i, sp, a, mult = _lru_gates(xc, wa, ba_ref[...], wx, bx_ref[...], lam)
        hv = h_ref[...]
        dyv = dy_ref[...]
        gv = gate_ref[...]
        gl, th = _gelu(gv)
        dgate_ref[...] = (dyv * hv * _gelu_grad(gv, th)).astype(BF16)
        a_next = jnp.where(tt < SEQ - 1, pltpu.roll(a, SEQ - 1, 0), 0.0)
        gsum = _scan(a_next, dyv * gl, tt, True, a_s, b_s)
        h_prev = jnp.where(tt >= 1, pltpu.roll(hv, 1, 0), 0.0)
        d_mult = gsum * i * xc
        d_i = gsum * mult * xc
        d_xc = gsum * mult * i
        d_la = gsum * h_prev * a - d_mult * (a * a) / mult
        d_pr = (d_la * (-LRU_C * sp)) * r * (1.0 - r)
        d_pi = d_i * i * (1.0 - i)
        d_lam = jnp.sum(d_la * r, axis=0, keepdims=True) * (LRU_C * _sigmoid(-lam))
        d_prb = d_pr.astype(BF16)
        d_pib = d_pi.astype(BF16)
        d_xc = d_xc + _dot_nt(d_prb, wa) + _dot_nt(d_pib, wx)
        dwa_ref[0] = _dot_tn(xcb, d_prb)
        dwx_ref[0] = _dot_tn(xcb, d_pib)
        rows = [jnp.sum(d_xc * shifted[2], axis=0, keepdims=True),
                jnp.sum(d_xc * shifted[1], axis=0, keepdims=True),
                jnp.sum(d_xc * shifted[0], axis=0, keepdims=True),
                jnp.sum(d_xc * xb, axis=0, keepdims=True),
                jnp.sum(d_xc, axis=0, keepdims=True),
                jnp.sum(d_pr, axis=0, keepdims=True),
                jnp.sum(d_pi, axis=0, keepdims=True),
                d_lam]
        ri = lax.broadcasted_iota(jnp.int32, (8, LRU_TC), 0)
        acc = jnp.zeros((8, LRU_TC), F32)
        for k, rv in enumerate(rows):
            acc = jnp.where(ri == k, rv, acc)
        vecs_ref[...] = acc
        d_xb = d_xc * cwv[3:4, :]
        for s in (1, 2, 3):
            d_xb = d_xb + jnp.where(tt < SEQ - s, pltpu.roll(d_xc, SEQ - s, 0), 0.0) * cwv[3 - s:4 - s, :]
        dxbr_ref[...] = d_xb.astype(BF16)

    return _call(
        body, name=name, grid=(LRU_W // LRU_TC,),
        in_specs=[col, col, col, col, cw, vec, bd, vec, bd, vec, vec],
        out_specs=[col, col, pl.BlockSpec((8, LRU_TC), lambda j: (0, j)), bd, bd],
        out_shape=[jax.ShapeDtypeStruct((SEQ, LRU_W), BF16), jax.ShapeDtypeStruct((SEQ, LRU_W), BF16),
                   jax.ShapeDtypeStruct((8, LRU_W), F32),
                   jax.ShapeDtypeStruct((LRU_W // LRU_TC, LRU_TC, LRU_TC), F32),
                   jax.ShapeDtypeStruct((LRU_W // LRU_TC, LRU_TC, LRU_TC), F32)],
        scratch_shapes=[pltpu.VMEM((SEQ, LRU_TC), F32)] * 2,
        args=[gate, xbr, h, dy, conv_w, conv_b, wa_bd, b_a, wx_bd, b_x, lam], stages=stages)


def _rope(x, cos, sin_signed):
    w = x.shape[1]
    reps = w // 128
    if reps > 1:
        cos = jnp.tile(cos, (1, reps))
        sin_signed = jnp.tile(sin_signed, (1, reps))
    lane = lax.broadcasted_iota(jnp.int32, x.shape, 1)
    first = (lane & 63) < 32
    partner = jnp.where(first, pltpu.roll(x, w - 32, 1), pltpu.roll(x, 32, 1))
    return x * cos + partner * sin_signed


def _both_halves(t, odd):
    lo = lax.broadcasted_iota(jnp.int32, t.shape, 1) < 64
    rolled = pltpu.roll(t, 64, 1)
    return jnp.where(lo, rolled, t) if odd else jnp.where(lo, t, rolled)


def _stack_heads(ta, tb):
    lo = lax.broadcasted_iota(jnp.int32, ta.shape, 1) < 64
    return jnp.concatenate([jnp.where(lo, ta, 0.0), jnp.where(lo, 0.0, ta),
                            jnp.where(lo, tb, 0.0), jnp.where(lo, 0.0, tb)], axis=0)


def _unstack_heads(o):
    lo = lax.broadcasted_iota(jnp.int32, (ATTN_BLOCK, 128), 1) < 64
    return (jnp.where(lo, o[0:128], o[128:256]), jnp.where(lo, o[256:384], o[384:512]))


def _window_upper_t():
    shape = (ATTN_BLOCK, 4 * ATTN_BLOCK)
    return lax.broadcasted_iota(jnp.int32, shape, 0) > (lax.broadcasted_iota(jnp.int32, shape, 1) & (ATTN_BLOCK - 1))


def _fold_t(t, upper_t):
    return jnp.where(upper_t, t[:ATTN_BLOCK], t[ATTN_BLOCK:])


def _unfold_t(t, upper_t):
    zero = jnp.zeros_like(t)
    return jnp.concatenate([jnp.where(upper_t, t, zero), jnp.where(upper_t, zero, t)], axis=0)


def _attn_probs_t(kd, qs, sinks_ref, hk, first_block, upper_t):
    s = _fold_t(_dot_nt(kd, qs), upper_t) * (HEAD_DIM ** -0.5)
    s = jnp.where(jnp.logical_and(upper_t, first_block), MASK_VALUE, s)
    rg = lax.broadcasted_iota(jnp.int32, (1, 4 * ATTN_BLOCK), 1) >> 7
    sink = jnp.where(rg == 0, sinks_ref[4 * hk],
                     jnp.where(rg == 1, sinks_ref[4 * hk + 1],
                               jnp.where(rg == 2, sinks_ref[4 * hk + 2], sinks_ref[4 * hk + 3])))
    m = jnp.maximum(jnp.max(s, axis=0, keepdims=True), sink)
    e = jnp.exp(s - m)
    es = jnp.exp(sink - m)
    inv = 1.0 / (jnp.sum(e, axis=0, keepdims=True) + es)
    return e * inv, es * inv


def _prev(i):
    return jnp.maximum(i - 1, 0)


def attn_fwd(q, k, v, cos, sin_signed, sinks, name, stages=()):
    nb = ATTN_BLOCK

    def body(q_ref, kc_ref, kp_ref, vc_ref, vp_ref, cc_ref, sc_ref, cp_ref, sp_ref, sinks_ref,
             qr_ref, kr_ref, y_ref):
        first_block = pl.program_id(0) == 0
        qr = _rope(q_ref[...], cc_ref[...], sc_ref[...])
        kc = _rope(kc_ref[...], cc_ref[...], sc_ref[...])
        kp = _rope(kp_ref[...], cp_ref[...], sp_ref[...])
        qr_ref[...] = qr.astype(BF16)
        kr_ref[...] = kc.astype(BF16)
        k2 = jnp.concatenate([kp, kc], axis=0)
        v2 = jnp.concatenate([vp_ref[...].astype(F32), vc_ref[...].astype(F32)], axis=0)
        upper_t = _window_upper_t()
        for hk in range(N_KV_HEADS):
            kt = hk // 2
            kd = _both_halves(k2[:, kt * 128:(kt + 1) * 128], hk % 2).astype(BF16)
            vd = _both_halves(v2[:, kt * 128:(kt + 1) * 128], hk % 2).astype(BF16)
            qs = _stack_heads(qr[:, (2 * hk) * 128:(2 * hk + 1) * 128],
                              qr[:, (2 * hk + 1) * 128:(2 * hk + 2) * 128]).astype(BF16)
            p, _ = _attn_probs_t(kd, qs, sinks_ref, hk, first_block, upper_t)
            ta, tb = _unstack_heads(_dot_tn(_unfold_t(p.astype(BF16), upper_t), vd))
            y_ref[:, (2 * hk) * 128:(2 * hk + 1) * 128] = ta.astype(BF16)
            y_ref[:, (2 * hk + 1) * 128:(2 * hk + 2) * 128] = tb.astype(BF16)

    cur = lambda w: pl.BlockSpec((nb, w), lambda i: (i, 0))
    prv = lambda w: pl.BlockSpec((nb, w), lambda i: (_prev(i), 0))
    return _call(
        body, name=name, grid=(N_ATTN_BLOCKS,),
        in_specs=[cur(D_MODEL), cur(KV_W), prv(KV_W), cur(KV_W), prv(KV_W), cur(128), cur(128), prv(128), prv(128),
                  pl.BlockSpec(memory_space=pltpu.SMEM)],
        out_specs=[cur(D_MODEL), cur(KV_W), cur(D_MODEL)],
        out_shape=[jax.ShapeDtypeStruct((SEQ, D_MODEL), BF16), jax.ShapeDtypeStruct((SEQ, KV_W), BF16),
                   jax.ShapeDtypeStruct((SEQ, D_MODEL), BF16)],
        args=[q, k, k, v, v, cos, sin_signed, cos, sin_signed, sinks], stages=stages)


def attn_bwd(qr, kr, v, dy, cos, sin_signed, sinks, name, stages=()):
    nb = ATTN_BLOCK
    n_steps = N_ATTN_BLOCKS + 1
    scale = HEAD_DIM ** -0.5

    def body(q_ref, kc_ref, kp_ref, vc_ref, vp_ref, dy_ref, cc_ref, sc_ref, cp_ref, sp_ref, sinks_ref,
             dq_ref, dkv_ref, dsk_ref, ck_ref, cv_ref):
        dk_ref = dkv_ref.at[:, pl.ds(0, KV_W)]
        dv_ref = dkv_ref.at[:, pl.ds(KV_W, KV_W)]
        i = pl.program_id(0)

        @pl.when(i == 0)
        def _():
            dsk_ref[...] = jnp.zeros_like(dsk_ref)
            ck_ref[...] = jnp.zeros_like(ck_ref)
            cv_ref[...] = jnp.zeros_like(cv_ref)

        @pl.when(i < N_ATTN_BLOCKS)
        def _():
            qv = q_ref[...].astype(F32)
            dov = dy_ref[...].astype(F32)
            k2 = jnp.concatenate([kp_ref[...].astype(F32), kc_ref[...].astype(F32)], axis=0)
            v2 = jnp.concatenate([vp_ref[...].astype(F32), vc_ref[...].astype(F32)], axis=0)
            lane = lax.broadcasted_iota(jnp.int32, (8, 128), 1)
            lo = lax.broadcasted_iota(jnp.int32, (2 * nb, 128), 1) < 64
            dsk = jnp.zeros((8, 128), F32)
            dk_tiles = []
            dv_tiles = []
            upper_t = _window_upper_t()
            for hk in range(N_KV_HEADS):
                kt = hk // 2
                kd = _both_halves(k2[:, kt * 128:(kt + 1) * 128], hk % 2).astype(BF16)
                vd = _both_halves(v2[:, kt * 128:(kt + 1) * 128], hk % 2).astype(BF16)
                qs = _stack_heads(qv[:, (2 * hk) * 128:(2 * hk + 1) * 128],
                                  qv[:, (2 * hk + 1) * 128:(2 * hk + 2) * 128]).astype(BF16)
                dos = _stack_heads(dov[:, (2 * hk) * 128:(2 * hk + 1) * 128],
                                   dov[:, (2 * hk + 1) * 128:(2 * hk + 2) * 128]).astype(BF16)
                p, ps = _attn_probs_t(kd, qs, sinks_ref, hk, i == 0, upper_t)
                dp = _fold_t(_dot_nt(vd, dos), upper_t)
                delta = jnp.sum(p * dp, axis=0, keepdims=True)
                ds = _unfold_t((p * (dp - delta)).astype(BF16), upper_t)
                dsink = -ps * delta
                for g in range(4):
                    dsk = dsk + jnp.where(lane == 4 * hk + g, jnp.sum(dsink[:, g * nb:(g + 1) * nb]), 0.0)
                ta, tb = _unstack_heads(_dot_tn(ds, kd) * scale)
                dq_a = (2 * hk) * 128
                dq_ref[:, dq_a:dq_a + 128] = _rope(ta, cc_ref[...], -sc_ref[...]).astype(BF16)
                dq_ref[:, dq_a + 128:dq_a + 256] = _rope(tb, cc_ref[...], -sc_ref[...]).astype(BF16)
                rk = _dot(ds, qs) * scale
                rv = _dot(_unfold_t(p.astype(BF16), upper_t), dos)
                dk_tiles.append(rk + pltpu.roll(rk, 64, 1))
                dv_tiles.append(rv + pltpu.roll(rv, 64, 1))
            dsk_ref[...] += dsk
            dk_full = jnp.concatenate([jnp.where(lo, dk_tiles[0], dk_tiles[1]),
                                       jnp.where(lo, dk_tiles[2], dk_tiles[3])], axis=1)
            dv_full = jnp.concatenate([jnp.where(lo, dv_tiles[0], dv_tiles[1]),
                                       jnp.where(lo, dv_tiles[2], dv_tiles[3])], axis=1)
            dk_ref[...] = _rope(ck_ref[...] + dk_full[0:nb], cp_ref[...], -sp_ref[...]).astype(BF16)
            dv_ref[...] = (cv_ref[...] + dv_full[0:nb]).astype(BF16)
            ck_ref[...] = dk_full[nb:2 * nb]
            cv_ref[...] = dv_full[nb:2 * nb]

        @pl.when(i == N_ATTN_BLOCKS)
        def _():
            dk_ref[...] = _rope(ck_ref[...], cp_ref[...], -sp_ref[...]).astype(BF16)
            dv_ref[...] = cv_ref[...].astype(BF16)

    qi = lambda i: jnp.minimum(i, N_ATTN_BLOCKS - 1)
    cur = lambda w: pl.BlockSpec((nb, w), lambda i: (qi(i), 0))
    prv = lambda w: pl.BlockSpec((nb, w), lambda i: (_prev(qi(i)), 0))
    out_prev = lambda w: pl.BlockSpec((nb, w), lambda i: (_prev(i), 0))
    return _call(
        body, name=name, grid=(n_steps,),
        in_specs=[cur(D_MODEL), cur(KV_W), prv(KV_W), cur(KV_W), prv(KV_W), cur(D_MODEL),
                  cur(128), cur(128), out_prev(128), out_prev(128), pl.BlockSpec(memory_space=pltpu.SMEM)],
        out_specs=[cur(D_MODEL), out_prev(2 * KV_W), pl.BlockSpec((8, 128), lambda i: (0, 0))],
        out_shape=[jax.ShapeDtypeStruct((SEQ, D_MODEL), BF16), jax.ShapeDtypeStruct((SEQ, 2 * KV_W), BF16),
                   jax.ShapeDtypeStruct((8, 128), F32)],
        scratch_shapes=[pltpu.VMEM((nb, KV_W), F32), pltpu.VMEM((nb, KV_W), F32)],
        args=[qr, kr, kr, v, v, dy, cos, sin_signed, cos, sin_signed, sinks], stages=stages)


def _proj_scratch():
    return [pltpu.VMEM((D_MODEL, D_MODEL), BF16)] * 3 + [pltpu.SemaphoreType.DMA((3 * N_CHIPS,))]


def _load_projs(w_refs, wl_ref, wa_ref, wo_ref, sem):
    for k, (w_ref, dst) in enumerate(zip(w_refs, (wl_ref, wa_ref, wo_ref))):
        _load_weight(w_ref, dst, sem.at[pl.ds(k * N_CHIPS, N_CHIPS)])


def merge_fwd(y_lru, y_attn, g_lru, g_attn, projs, g_post, h_in, name, stages=()):
    tm = MM_ROWS

    def body(yl_ref, ya_ref, gl_ref, ga_ref, w1_ref, w2_ref, w3_ref, gp_ref, h_ref,
             pl_ref, pa_ref, mg_ref, m_ref, o_ref, wl_ref, wa_ref, wo_ref, sem):
        @pl.when(pl.program_id(0) == 0)
        def _():
            _load_projs((w1_ref, w2_ref, w3_ref), wl_ref, wa_ref, wo_ref, sem)

        p_l = _dot(yl_ref[...], wl_ref[...])
        p_a = _dot(ya_ref[...], wa_ref[...])
        pl_ref[...] = p_l.astype(BF16)
        pa_ref[...] = p_a.astype(BF16)
        merged = (_sigmoid(gl_ref[...]) * p_l + _sigmoid(ga_ref[...]) * p_a).astype(BF16)
        mg_ref[...] = merged
        m = _dot(merged, wo_ref[...])
        m_ref[...] = m
        o_ref[...] = h_ref[...] + m * _rsqrt_mean_sq(m) * gp_ref[...]

    row = _ROW(tm)
    return _call(
        body, name=name, grid=(SEQ // tm,),
        in_specs=[row, row, row, row, ANY, ANY, ANY, _VEC, row],
        out_specs=[row] * 5,
        out_shape=[jax.ShapeDtypeStruct((SEQ, D_MODEL), BF16)] * 3 + [jax.ShapeDtypeStruct((SEQ, D_MODEL), F32)] * 2,
        scratch_shapes=_proj_scratch(),
        args=[y_lru, y_attn, g_lru, g_attn, *projs, g_post, h_in], stages=stages)


def merge_bwd(d_out, m, g_post, projs, g_lru, g_attn, p_l, p_a, name, stages=()):
    tm = 256

    def body(do_ref, m_ref, gp_ref, w1_ref, w2_ref, w3_ref, gl_ref, ga_ref, pl_ref, pa_ref,
             dm_ref, dpl_ref, dpa_ref, dgl_ref, dga_ref, dya_ref, dyl_ref, dgp_ref, wl_ref, wa_ref, wo_ref, sem):
        @pl.when(pl.program_id(0) == 0)
        def _():
            _load_projs((w1_ref, w2_ref, w3_ref), wl_ref, wa_ref, wo_ref, sem)
            dgp_ref[...] = jnp.zeros_like(dgp_ref)

        mv = m_ref[...]
        rm = _rsqrt_mean_sq(mv)
        mh = mv * rm
        dn = do_ref[...]
        dgp_ref[...] += jnp.sum(dn * mh, axis=0, keepdims=True)
        t = dn * gp_ref[...]
        dm = (rm * (t - mh * jnp.mean(t * mh, axis=-1, keepdims=True))).astype(BF16)
        dm_ref[...] = dm
        dmg = _dot_nt(dm, wo_ref[...])
        sl = _sigmoid(gl_ref[...])
        sa = _sigmoid(ga_ref[...])
        dpl = (dmg * sl).astype(BF16)
        dpa = (dmg * sa).astype(BF16)
        dpl_ref[...] = dpl
        dpa_ref[...] = dpa
        dgl_ref[...] = (dmg * pl_ref[...].astype(F32) * sl * (1.0 - sl)).astype(BF16)
        dga_ref[...] = (dmg * pa_ref[...].astype(F32) * sa * (1.0 - sa)).astype(BF16)
        dyl_ref[...] = _dot_nt(dpl, wl_ref[...])
        dya_ref[...] = _dot_nt(dpa, wa_ref[...]).astype(BF16)

    row = _ROW(tm)
    return _call(
        body, name=name, grid=(SEQ // tm,),
        in_specs=[row, row, _VEC, ANY, ANY, ANY, row, row, row, row],
        out_specs=[row] * 7 + [_VEC],
        out_shape=[jax.ShapeDtypeStruct((SEQ, D_MODEL), BF16)] * 6 + [jax.ShapeDtypeStruct((SEQ, D_MODEL), F32),
                                                                       jax.ShapeDtypeStruct((1, D_MODEL), F32)],
        scratch_shapes=_proj_scratch(),
        args=[d_out, m, g_post, *projs, g_lru, g_attn, p_l, p_a], stages=stages)


def _rope_tables():
    half = HEAD_DIM // 2
    inv_freq = np.float32(ROPE_THETA) ** (-np.arange(half, dtype=np.float32) / np.float32(half))
    ang = np.arange(SEQ, dtype=np.float32)[:, None] * inv_freq[None, :]
    cos, sin = np.cos(ang), np.sin(ang)
    return (jnp.asarray(np.tile(np.concatenate([cos, cos], axis=1), (1, 2))),
            jnp.asarray(np.tile(np.concatenate([-sin, sin], axis=1), (1, 2))))


def _block_diag(w):
    per = LRU_TC // LRU_BLOCK_W
    w4 = w.reshape(LRU_W // LRU_TC, per, LRU_BLOCK_W, LRU_BLOCK_W)
    eye = jnp.eye(per, dtype=w.dtype)
    return jnp.einsum('jacd,ab->jacbd', w4, eye).reshape(LRU_W // LRU_TC, LRU_TC, LRU_TC).astype(BF16)


def _diag_blocks(p):
    per = LRU_TC // LRU_BLOCK_W
    p5 = p.reshape(LRU_W // LRU_TC, per, LRU_BLOCK_W, per, LRU_BLOCK_W)
    return jnp.stack([p5[:, a, :, a, :] for a in range(per)], axis=1).reshape(LRU_W // LRU_BLOCK_W, LRU_BLOCK_W, LRU_BLOCK_W)


def _place():
    x, y, c = lax.axis_index('x'), lax.axis_index('y'), lax.axis_index('c')
    chips = [(1 - x, y), (x, 1 - y), (1 - x, 1 - y)]
    return x, y, c, chips


def _rcopy(src, dst, send_sem, recv_sem, to):
    return pltpu.make_async_remote_copy(src_ref=src, dst_ref=dst, send_sem=send_sem, recv_sem=recv_sem,
                                        device_id=to, device_id_type=MESH)


class _Stage:
    inputs, out_shape, scratch, peers = (), (), (), ()

    def start(self, ins, outs, scr):
        plan = self._plan(ins, outs, scr)
        for ld in plan['loads']:
            ld.start()
        for cp in plan['sends']:
            cp.start()

    def relay(self, ins, outs, scr):
        pass

    def mid(self, ins, outs, scr):
        plan = self._plan(ins, outs, scr)
        for ld, st in zip(plan['loads'], plan['stores']):
            ld.wait()
            st.start()
        for arrived, onward in zip(plan['arrivals'], plan['forwards']):
            arrived.wait_recv()
            onward.start()

    def end(self, ins, outs, scr):
        plan = self._plan(ins, outs, scr)
        for st in plan['stores']:
            st.wait()
        for arrived in (plan['final_arrivals'] if plan['forwards'] else plan['arrivals']):
            arrived.wait_recv()
        for cp in plan['sends'] + plan['forwards']:
            cp.wait_send()


def _empty_plan():
    return dict(loads=[], stores=[], sends=[], arrivals=[], forwards=[], final_arrivals=[])


class GatherStage(_Stage):
    peers = ('chips', 'sib')
    N_CP = 12

    def __init__(self, items):
        self.ranges = [(off, rows) for _, off, rows in items]
        self.inputs = [src for src, _, _ in items]
        self.out_shape = [jax.ShapeDtypeStruct((N_CHIPS, rows, D_MODEL), BF16) for _, rows in self.ranges]
        n = self.N_CP * len(items)
        self.scratch = [pltpu.VMEM((sum(r for _, r in self.ranges), D_MODEL), BF16), pltpu.SemaphoreType.DMA((n,)),
                        pltpu.SemaphoreType.DMA((n,)), pltpu.SemaphoreType.DMA((2 * len(items),))]

    def _plan(self, ins, outs, scr):
        buf, send, recv, lsem = scr
        x, y, c, _ = _place()
        me_q, q_x, q_y, q_d = 2 * x + y, 2 * (1 - x) + y, 2 * x + (1 - y), 2 * (1 - x) + (1 - y)
        to_x, to_y, sib = (1 - x, y, c), (x, 1 - y, c), (x, y, 1 - c)
        plan = dict(loads=[], stores=[], first=[], early=[], relays=[], late=[], hand_early=[], hand_late=[], final=[])
        boff = 0
        for w, ((off, rows), p_ref, o_ref) in enumerate(zip(self.ranges, ins, outs)):
            hr = rows // 2
            ch = hr // 2
            plan['loads'].append(pltpu.make_async_copy(p_ref.at[pl.ds(off, rows)], buf.at[pl.ds(boff, rows)], lsem.at[2 * w]))
            plan['stores'].append(pltpu.make_async_copy(buf.at[pl.ds(boff, rows)], o_ref.at[me_q], lsem.at[2 * w + 1]))
            boff += rows
            base = w * self.N_CP
            mine = [pl.ds(pl.multiple_of(c * hr + k * ch, 16), ch) for k in range(2)]
            theirs = [pl.ds(pl.multiple_of((1 - c) * hr + k * ch, 16), ch) for k in range(2)]
            src = [p_ref.at[pl.ds(pl.multiple_of(off + c * hr + k * ch, 16), ch)] for k in range(2)]

            def cp(k, s, d, to):
                return _rcopy(s, d, send.at[base + k], recv.at[base + k], to)

            def here(q, rows_):
                return o_ref.at[q, rows_]

            plan['first'] += [cp(0, src[0], here(me_q, mine[0]), to_x), cp(2, src[1], here(me_q, mine[1]), to_y),
                              cp(1, src[1], here(me_q, mine[1]), to_x), cp(3, src[0], here(me_q, mine[0]), to_y)]
            x_a, y_b = here(q_x, mine[0]), here(q_y, mine[1])
            plan['early'] += [cp(0, x_a, x_a, to_x), cp(2, y_b, y_b, to_y)]
            plan['relays'] += [cp(4, x_a, x_a, to_y), cp(5, y_b, y_b, to_x)]
            plan['hand_early'] += [cp(6, x_a, x_a, sib), cp(7, y_b, y_b, sib)]
            x_b, y_a, d_a, d_b = here(q_x, mine[1]), here(q_y, mine[0]), here(q_d, mine[0]), here(q_d, mine[1])
            plan['late'] += [cp(1, x_b, x_b, to_x), cp(3, y_a, y_a, to_y), cp(4, d_a, d_a, to_y), cp(5, d_b, d_b, to_x)]
            plan['hand_late'] += [cp(8, x_b, x_b, sib), cp(9, y_a, y_a, sib), cp(10, d_a, d_a, sib), cp(11, d_b, d_b, sib)]
            for k, (q, piece) in enumerate([(q_x, 0), (q_y, 1), (q_x, 1), (q_y, 0), (q_d, 0), (q_d, 1)]):
                got = here(q, theirs[piece])
                plan['final'].append(cp(6 + k, got, got, sib))
        return plan

    def start(self, ins, outs, scr):
        plan = self._plan(ins, outs, scr)
        for ld in plan['loads']:
            ld.start()
        for cp in plan['first']:
            cp.start()

    def relay(self, ins, outs, scr):
        plan = self._plan(ins, outs, scr)
        for arrived in plan['early']:
            arrived.wait_recv()
        for cp in plan['relays'] + plan['hand_early']:
            cp.start()

    def mid(self, ins, outs, scr):
        plan = self._plan(ins, outs, scr)
        for ld, st in zip(plan['loads'], plan['stores']):
            ld.wait()
            st.start()
        for arrived in plan['late']:
            arrived.wait_recv()
        for cp in plan['hand_late']:
            cp.start()

    def end(self, ins, outs, scr):
        plan = self._plan(ins, outs, scr)
        for st in plan['stores']:
            st.wait()
        for arrived in plan['final']:
            arrived.wait_recv()
        for cp in plan['first'] + plan['relays'] + plan['hand_early'] + plan['hand_late']:
            cp.wait_send()


class PairStage(_Stage):
    peers = ('sib',)

    def __init__(self, grads):
        self.inputs = list(grads)
        self.out_shape = [jax.ShapeDtypeStruct((N_CHIPS, 1) + g.shape[2:], BF16) for g in grads]
        n_cp = N_CHIPS * len(grads)
        self.scratch = [pltpu.SemaphoreType.DMA((n_cp,)), pltpu.SemaphoreType.DMA((n_cp,))]

    def _plan(self, ins, outs, scr):
        send, recv = scr
        x, y, c, _ = _place()
        plan = _empty_plan()
        for w, (g_ref, l_ref) in enumerate(zip(ins, outs)):
            for q in range(N_CHIPS):
                i = w * N_CHIPS + q
                plan['sends'].append(_rcopy(g_ref.at[q, pl.ds(1 - c, 1)], l_ref.at[q], send.at[i], recv.at[i], (x, y, 1 - c)))
        plan['arrivals'] = plan['sends']
        return plan


class ChipStage(_Stage):
    peers = ('chips',)

    def __init__(self, items):
        self.ranges = [(off, n) for _, off, n in items]
        self.inputs = [s for s, _, _ in items]
        self.out_shape = [jax.ShapeDtypeStruct((N_CHIPS, n, D_MODEL), BF16) for _, n in self.ranges]
        n_cp = 3 * len(items)
        self.scratch = [pltpu.VMEM((sum(n for _, n in self.ranges), D_MODEL), BF16), pltpu.SemaphoreType.DMA((n_cp,)),
                        pltpu.SemaphoreType.DMA((n_cp,)), pltpu.SemaphoreType.DMA((2 * len(items),))]

    def _plan(self, ins, outs, scr):
        buf, send, recv, lsem = scr
        x, y, c, chips = _place()
        me_q = 2 * x + y
        plan = _empty_plan()
        boff = 0
        for w, ((off, n), s_ref, l_ref) in enumerate(zip(self.ranges, ins, outs)):
            rows = pl.ds(off, n)
            plan['loads'].append(pltpu.make_async_copy(s_ref.at[me_q, rows], buf.at[pl.ds(boff, n)], lsem.at[2 * w]))
            plan['stores'].append(pltpu.make_async_copy(buf.at[pl.ds(boff, n)], l_ref.at[me_q], lsem.at[2 * w + 1]))
            boff += n
            for j, (cx, cy) in enumerate(chips):
                i = w * 3 + j
                got = l_ref.at[2 * cx + cy]
                plan['sends'].append(_rcopy(s_ref.at[2 * cx + cy, rows], l_ref.at[me_q], send.at[i], recv.at[i], (cx, cy, c)))
                plan['arrivals'].append(_rcopy(got, got, send.at[i], recv.at[i], (cx, cy, c)))
        return plan


class SwapStage(_Stage):
    peers = ('sib',)

    def __init__(self, items):
        n = len(items)
        self.inputs = list(items)
        self.out_shape = [jax.ShapeDtypeStruct((2,) + a.shape, a.dtype) for a in items]
        self.scratch = [pltpu.VMEM(a.shape, a.dtype) for a in items] + [
            pltpu.SemaphoreType.DMA((n,)), pltpu.SemaphoreType.DMA((n,)), pltpu.SemaphoreType.DMA((2 * n,))]

    def _plan(self, ins, outs, scr):
        bufs, (send, recv, lsem) = scr[:len(ins)], scr[len(ins):]
        x, y, c, _ = _place()
        plan = _empty_plan()
        for w, (h_ref, o_ref, buf) in enumerate(zip(ins, outs, bufs)):
            plan['loads'].append(pltpu.make_async_copy(h_ref, buf, lsem.at[2 * w]))
            plan['stores'].append(pltpu.make_async_copy(buf, o_ref.at[c], lsem.at[2 * w + 1]))
            got = o_ref.at[1 - c]
            plan['sends'].append(_rcopy(h_ref, o_ref.at[c], send.at[w], recv.at[w], (x, y, 1 - c)))
            plan['arrivals'].append(_rcopy(got, got, send.at[w], recv.at[w], (x, y, 1 - c)))
        return plan


class SmallGatherStage(_Stage):
    peers = ('chips', 'sib')

    def __init__(self, blk):
        self.inputs = [blk]
        self.out_shape = [jax.ShapeDtypeStruct((N_DEV,) + blk.shape, blk.dtype)]
        self.scratch = [pltpu.VMEM(blk.shape, blk.dtype), pltpu.SemaphoreType.DMA((7,)), pltpu.SemaphoreType.DMA((7,)),
                        pltpu.SemaphoreType.DMA((2,))]

    def _plan(self, ins, outs, scr):
        (x_ref,), (o_ref,), (buf, send, recv, lsem) = ins, outs, scr
        x, y, c, chips = _place()
        sib = (x, y, 1 - c)

        def slot(px, py, pc):
            return o_ref.at[4 * px + 2 * py + pc]

        plan = _empty_plan()
        plan['loads'].append(pltpu.make_async_copy(x_ref, buf, lsem.at[0]))
        plan['stores'].append(pltpu.make_async_copy(buf, slot(x, y, c), lsem.at[1]))
        from_sib = slot(x, y, 1 - c)
        plan['sends'].append(_rcopy(x_ref, slot(x, y, c), send.at[0], recv.at[0], sib))
        plan['final_arrivals'].append(_rcopy(from_sib, from_sib, send.at[0], recv.at[0], sib))
        for j, (cx, cy) in enumerate(chips):
            got, got_sib = slot(cx, cy, c), slot(cx, cy, 1 - c)
            plan['sends'].append(_rcopy(x_ref, slot(x, y, c), send.at[1 + j], recv.at[1 + j], (cx, cy, c)))
            plan['arrivals'].append(_rcopy(got, got, send.at[1 + j], recv.at[1 + j], (cx, cy, c)))
            plan['forwards'].append(_rcopy(got, got, send.at[4 + j], recv.at[4 + j], sib))
            plan['final_arrivals'].append(_rcopy(got_sib, got_sib, send.at[4 + j], recv.at[4 + j], sib))
        return plan


_HBM = pl.BlockSpec(memory_space=pltpu.HBM)
_SEM = pl.BlockSpec(memory_space=pltpu.SEMAPHORE)
_DATAFLOW = pltpu.CompilerParams(has_side_effects=pltpu.SideEffectType.DATAFLOW_SIDE_EFFECTING)


def chip_exchange_start(s):
    def body(s_ref, land_ref, send, recv, s_thru, land_thru, token):
        x, y, c, chips = _place()
        for j, (cx, cy) in enumerate(chips):
            _rcopy(s_ref.at[2 * cx + cy], land_ref.at[2 * x + y], send.at[j], recv.at[j], (cx, cy, c)).start()
        token[...] = jnp.zeros_like(token)

    return pl.pallas_call(
        body, name='chip_exchange_start',
        out_shape=(pltpu.SemaphoreType.DMA((3,)), pltpu.SemaphoreType.DMA((3,)), pltpu.HBM(s.shape, s.dtype),
                   pltpu.HBM(s.shape, s.dtype), jax.ShapeDtypeStruct((8, 128), F32)),
        in_specs=(_HBM, _HBM), out_specs=(_SEM, _SEM, _HBM, _HBM, pl.BlockSpec(memory_space=pltpu.VMEM)),
        input_output_aliases={0: 2, 1: 3}, compiler_params=_DATAFLOW,
    )(pltpu.with_memory_space_constraint(s, pltpu.HBM),
      pltpu.with_memory_space_constraint(lax.empty(s.shape, s.dtype), pltpu.HBM))


def chip_exchange_wait(send, recv, s_thru, land_thru, after):
    def body(s_ref, land_ref, send_sem, recv_sem, after_ref, s_out, land_out):
        x, y, c, chips = _place()
        for j, (cx, cy) in enumerate(chips):
            cp = _rcopy(s_ref.at[2 * cx + cy], land_ref.at[2 * cx + cy], send_sem.at[j], recv_sem.at[j], (cx, cy, c))
            cp.wait_send()
            cp.wait_recv()

    return pl.pallas_call(
        body, name='chip_exchange_wait',
        out_shape=(pltpu.HBM(s_thru.shape, s_thru.dtype), pltpu.HBM(land_thru.shape, land_thru.dtype)),
        in_specs=(_HBM, _HBM, _SEM, _SEM, ANY), out_specs=(_HBM, _HBM),
        input_output_aliases={0: 0, 1: 1}, compiler_params=_DATAFLOW,
    )(s_thru, land_thru, send, recv, after)


def comm_call(name, stages):
    def body():
        pass

    return _call(body, name=name, grid=(1,), in_specs=[], out_specs=[], out_shape=[], args=[], stages=stages)[1]


def pair_sum(g4, land, c_arr, name):
    hr = g4.shape[2]

    def body(c_ref, g_ref, l_ref, o_ref):
        o_ref[0] = (g_ref[0, 0].astype(F32) + l_ref[0, 0].astype(F32)).astype(BF16)

    return pl.pallas_call(
        body, name=name,
        grid_spec=pltpu.PrefetchScalarGridSpec(
            num_scalar_prefetch=1, grid=(N_CHIPS,),
            in_specs=[pl.BlockSpec((1, 1, hr, D_MODEL), lambda q, c: (q, c[0], 0, 0)),
                      pl.BlockSpec((1, 1, hr, D_MODEL), lambda q, c: (q, 0, 0, 0))],
            out_specs=pl.BlockSpec((1, hr, D_MODEL), lambda q, c: (q, 0, 0))),
        out_shape=jax.ShapeDtypeStruct((N_CHIPS, hr, D_MODEL), BF16),
        compiler_params=_params(1),
    )(c_arr, g4, land)


def small_sum(vec_parts, lru_parts):
    def body(v_ref, l_ref, o_ref):
        for p_ref, lo, n in ((v_ref, 0, ROW_WA), (l_ref, ROW_WA, SMALL_ROWS - ROW_WA)):
            acc = p_ref[0]
            for s in range(1, N_DEV):
                acc = acc + p_ref[s]
            o_ref[lo:lo + n, :] = acc

    return pl.pallas_call(
        body, name='small_sum', grid=(1,),
        in_specs=[pl.BlockSpec(vec_parts.shape, lambda i: (0, 0, 0)), pl.BlockSpec(lru_parts.shape, lambda i: (0, 0, 0))],
        out_specs=pl.BlockSpec((SMALL_ROWS, D_MODEL), lambda i: (0, 0)),
        out_shape=jax.ShapeDtypeStruct((SMALL_ROWS, D_MODEL), F32),
        compiler_params=_params(1),
    )(vec_parts, lru_parts)


def _adam_math(w, g, m, v):
    m2 = ADAM_B1 * m + (1.0 - ADAM_B1) * g
    v2 = ADAM_B2 * v + (1.0 - ADAM_B2) * (g * g)
    m_hat = m2 / (1.0 - ADAM_B1 ** ADAM_STEP)
    v_hat = v2 / (1.0 - ADAM_B2 ** ADAM_STEP)
    delta = -ADAM_LR * (m_hat / (jnp.sqrt(v_hat) + ADAM_EPS) + ADAM_WD * w)
    return delta, m2, v2


def _adam_body(n_parts, transposed, n_after):
    def body(*refs):
        refs = refs[n_after:]
        g_refs = refs[:n_parts]
        w_ref, m_ref, v_ref, go_ref, d_ref, mo_ref, vo_ref = refs[n_parts:]
        def chips_added(blk):
            acc = blk[0].astype(F32)
            for s in range(1, N_CHIPS):
                acc = acc + blk[s].astype(F32)
            return acc

        if transposed:
            g = jnp.concatenate([chips_added(g_ref[h]) for h in range(2) for g_ref in g_refs], axis=0).T
        else:
            rows = [chips_added(g_ref[0]) for g_ref in g_refs]
            g = jnp.concatenate(rows, axis=0) if n_parts > 1 else rows[0]
        go_ref[...] = g
        d_ref[...], mo_ref[...], vo_ref[...] = _adam_math(w_ref[...], g, m_ref[...], v_ref[...])
    return body


def adam_rows(fulls, name, w, m, v, after=()):
    hr = w.shape[0] // 2
    blk = pl.BlockSpec((hr, D_MODEL), lambda h: (h, 0))
    return pl.pallas_call(
        _adam_body(len(fulls), False, len(after)), name='adam_' + name, grid=(2,),
        in_specs=[ANY] * len(after)
        + [pl.BlockSpec((1, N_CHIPS, f.shape[2], D_MODEL), lambda h: (h, 0, 0, 0)) for f in fulls] + [blk, blk, blk],
        out_specs=[blk] * 4,
        out_shape=[jax.ShapeDtypeStruct(w.shape, F32)] * 4,
        compiler_params=_params(1),
    )(*after, *fulls, w, m, v)


def adam_cols(fulls, name, w, m, v, after=()):
    cols = w.shape[1]
    tr = 256
    blk = pl.BlockSpec((tr, cols), lambda i: (i, 0))
    return pl.pallas_call(
        _adam_body(len(fulls), True, len(after)), name='adam_' + name, grid=(D_MODEL // tr,),
        in_specs=[ANY] * len(after)
        + [pl.BlockSpec((2, N_CHIPS, f.shape[2], tr), lambda i: (0, 0, 0, i)) for f in fulls] + [blk, blk, blk],
        out_specs=[blk] * 4,
        out_shape=[jax.ShapeDtypeStruct(w.shape, F32)] * 4,
        compiler_params=_params(1),
    )(*after, *fulls, w, m, v)


def adam_small(g, w, m, v):
    def body(g_ref, w_ref, m_ref, v_ref, d_ref, mo_ref, vo_ref):
        d_ref[...], mo_ref[...], vo_ref[...] = _adam_math(w_ref[...], g_ref[...], m_ref[...], v_ref[...])

    blk = pl.BlockSpec(w.shape, lambda i: (0, 0))
    return pl.pallas_call(
        body, name='adam_small', grid=(1,), in_specs=[blk] * 4, out_specs=[blk] * 3,
        out_shape=[jax.ShapeDtypeStruct(w.shape, F32)] * 3, compiler_params=_params(1),
    )(g, w, m, v)


WEIGHTS = ('ffn1_pre_g', 'ffn1_w_gu', 'ffn1_w_down', 'ffn1_post_g', 'mix_pre_g', 'w_in', 'conv_w', 'conv_b',
           'lru_w_a', 'lru_b_a', 'lru_w_x', 'lru_b_x', 'lru_lambda', 'attn_sinks', 'w_proj_lru', 'w_proj_attn',
           'w_out', 'mix_post_g', 'ffn2_pre_g', 'ffn2_w_gu', 'ffn2_w_down', 'ffn2_post_g')
SMALL = tuple(n for n in WEIGHTS if n not in PACK_OFF)


def cast_t(w, name, stages=()):
    cols = w.shape[1]
    tc = 128

    def body(w_ref, o_ref):
        o_ref[...] = w_ref[...].T.astype(BF16)

    (out,), stage_out = _call(
        body, name=name, grid=(cols // tc,),
        in_specs=[pl.BlockSpec((D_MODEL, tc), lambda j: (0, j))],
        out_specs=[pl.BlockSpec((tc, D_MODEL), lambda j: (j, 0))],
        out_shape=[jax.ShapeDtypeStruct((cols, D_MODEL), BF16)],
        args=[w], stages=stages)
    return out, stage_out


def _pack_vecs(d, conv_rows):
    sinks = jnp.pad(d['attn_sinks'].reshape(1, N_Q_HEADS), ((0, 0), (0, D_MODEL - N_Q_HEADS)))
    conv = jnp.pad(conv_rows, ((0, ROW_WA - ROW_CONV - conv_rows.shape[0]), (0, 0)))
    return jnp.concatenate([d[n].reshape(1, D_MODEL) for n in SMALL_VECS] + [sinks, conv], axis=0)


def _pack_lru(d):
    return jnp.concatenate([d['lru_w_a'].reshape(64, D_MODEL), d['lru_w_x'].reshape(64, D_MODEL)], axis=0)


def _pack_small(d, conv_rows):
    return jnp.concatenate([_pack_vecs(d, conv_rows), _pack_lru(d)], axis=0)


def _unpack_small(p, shapes):
    out = {n: p[k:k + 1].reshape(shapes[n]) for k, n in enumerate(SMALL_VECS)}
    out['attn_sinks'] = p[ROW_SINKS:ROW_SINKS + 1, :N_Q_HEADS].reshape(shapes['attn_sinks'])
    out['conv_w'] = p[ROW_CONV:ROW_CONV + 1].reshape(shapes['conv_w'])
    out['lru_w_a'] = p[ROW_WA:ROW_WA + 64].reshape(shapes['lru_w_a'])
    out['lru_w_x'] = p[ROW_WX:ROW_WX + 64].reshape(shapes['lru_w_x'])
    return out


def kernel(x, ffn1_pre_g, ffn1_w_gu, ffn1_w_down, ffn1_post_g, mix_pre_g, w_in, conv_w, conv_b, lru_w_a, lru_b_a, lru_w_x, lru_b_x, lru_lambda, attn_sinks, w_proj_lru, w_proj_attn, w_out, mix_post_g, ffn2_pre_g, ffn2_w_gu, ffn2_w_down, ffn2_post_g, loss_target, m_ffn1_pre_g, m_ffn1_w_gu, m_ffn1_w_down, m_ffn1_post_g, m_mix_pre_g, m_w_in, m_conv_w, m_conv_b, m_lru_w_a, m_lru_b_a, m_lru_w_x, m_lru_b_x, m_lru_lambda, m_attn_sinks, m_w_proj_lru, m_w_proj_attn, m_w_out, m_mix_post_g, m_ffn2_pre_g, m_ffn2_w_gu, m_ffn2_w_down, m_ffn2_post_g, v_ffn1_pre_g, v_ffn1_w_gu, v_ffn1_w_down, v_ffn1_post_g, v_mix_pre_g, v_w_in, v_conv_w, v_conv_b, v_lru_w_a, v_lru_b_a, v_lru_w_x, v_lru_b_x, v_lru_lambda, v_attn_sinks, v_w_proj_lru, v_w_proj_attn, v_w_out, v_mix_post_g, v_ffn2_pre_g, v_ffn2_w_gu, v_ffn2_w_down, v_ffn2_post_g):
    given = dict(locals())
    w = {n: given[n] for n in WEIGHTS}
    mom = {n: given['m_' + n] for n in WEIGHTS}
    var = {n: given['v_' + n] for n in WEIGHTS}
    shapes = {n: w[n].shape for n in WEIGHTS}
    xq = lax.axis_index('x')
    yq = lax.axis_index('y')
    cq = lax.axis_index('c')
    me_q = 2 * xq + yq

    c_arr = cq.reshape(1).astype(jnp.int32)
    xs, target = x[0], loss_target[0]
    sw = {n: (w[n][0] if w[n].ndim > 2 else w[n]) for n in SMALL}
    cos, sin_signed = _rope_tables()
    wa_bd = _block_diag(sw['lru_w_a'])
    wx_bd = _block_diag(sw['lru_w_x'])
    sinks = sw['attn_sinks'].reshape(N_Q_HEADS)

    shard = {n: w[n][0].astype(BF16) for n, _, t in PACK if not t}
    conv_pad = jnp.pad(w['conv_w'][0], ((0, 4), (0, 0)))

    def whole(name):
        return (shard[name], 0, PACK_ROWS_OF[name])

    def part(name, p, n_parts=2):
        rows = PACK_ROWS_OF[name] // n_parts
        return (shard[name], p * rows, rows)

    shard['ffn1_w_gu'], _ = cast_t(w['ffn1_w_gu'][0], 'cast_ffn1_w_gu')
    shard['w_in'], ((w_gu1a,), (conv_all,)) = cast_t(
        w['w_in'][0], 'cast_w_in', stages=[GatherStage([part('ffn1_w_gu', 0)]), SmallGatherStage(conv_pad)])
    shard['ffn2_w_gu'], ((w_gu1b,),) = cast_t(w['ffn2_w_gu'][0], 'cast_ffn2_w_gu', stages=[GatherStage([part('ffn1_w_gu', 1)])])
    w_gu1 = [w_gu1a, w_gu1b]
    sw['conv_w'] = jnp.transpose(conv_all[0::2, :4, :], (1, 0, 2)).reshape(4, LRU_W)
    proj_names = ['w_proj_lru', 'w_proj_attn', 'w_out']

    (n1, g1, u1, a1), ((w_down1, w_in_a),) = ffn_fwd_a(xs, sw['ffn1_pre_g'], w_gu1, 'ffn1_fwd_a',
                                                        stages=[GatherStage([whole('ffn1_w_down'), part('w_in', 0)])])
    (f1, h1), ((w_in_b,),) = ffn_fwd_b(a1, w_down1, sw['ffn1_post_g'], xs, 'ffn1_fwd_b',
                                       stages=[GatherStage([part('w_in', 1)])])
    w_in_t = [w_in_a, w_in_b]
    (um, gate, xbr, q, k, v, g_lru, g_attn), ((w_gu2a,),) = mix_in(h1, sw['mix_pre_g'], w_in_t, 'mix_in',
                                                                   stages=[GatherStage([part('ffn2_w_gu', 0)])])
    (y_lru, h_lru), ((w_gu2b,),) = lru_fwd(gate, xbr, sw['conv_w'], sw['conv_b'], wa_bd, sw['lru_b_a'], wx_bd, sw['lru_b_x'],
                                           sw['lru_lambda'], 'lru_fwd', stages=[GatherStage([part('ffn2_w_gu', 1)])])
    (qr, kr, y_attn), (projs,) = attn_fwd(q, k, v, cos, sin_signed, sinks, 'attn_fwd',
                                          stages=[GatherStage([whole(n) for n in proj_names])])
    (p_l, p_a, merged, m, h2), ((w_down2,),) = merge_fwd(y_lru, y_attn, g_lru, g_attn, projs, sw['mix_post_g'], h1, 'merge_fwd',
                                                         stages=[GatherStage([whole('ffn2_w_down')])])
    w_gu2 = [w_gu2a, w_gu2b]
    (n2, g2, u2, a2), _ = ffn_fwd_a(h2, sw['ffn2_pre_g'], w_gu2, 'ffn2_fwd_a')
    (f2, dy, loss_blk), _ = ffn_fwd_b(a2, w_down2, sw['ffn2_post_g'], h2, 'ffn2_fwd_b', target=target)

    gs, full = {}, {}

    def pair_stage(names, grads):
        g4 = [g.reshape(N_CHIPS, 2, PACK_ROWS_OF[n] // 2, D_MODEL) for n, g in zip(names, grads)]
        return PairStage(g4), g4

    def pair_sums(names, g4, lands):
        return [pair_sum(g, l, c_arr, 'pair_sum_' + n) for n, g, l in zip(names, g4, lands)]

    def halves(s, n_parts=2):
        n = s.shape[1] // n_parts
        return [(s, p * n, n) for p in range(n_parts)]

    (df2, dgu2, gs['ffn2_post_g']), _ = ffn_bwd_a(dy, f2, sw['ffn2_post_g'], w_down2, g2, u2, 'ffn2_bwd_a')
    g_down2, _ = mm_tn([a2], df2, 1408, 'ffn2_dw_down')
    st, g4 = pair_stage(['ffn2_w_down'], [g_down2])
    g_gu2, (lands,) = mm_tn([dgu2], n2, 1408, 'ffn2_dw_gu', stages=[st])
    (s_down2,) = pair_sums(['ffn2_w_down'], g4, lands)
    st, g4 = pair_stage(['ffn2_w_gu'], [g_gu2])
    (dh2, gs['ffn2_pre_g']), ((l_down2,), lands) = norm_bwd([dgu2], w_gu2, h2, sw['ffn2_pre_g'], dy, 'ffn2_bwd_b',
                                                            stages=[ChipStage([(s_down2, 0, s_down2.shape[1])]), st])
    (s_gu2,) = pair_sums(['ffn2_w_gu'], g4, lands)

    (dm, dpl, dpa, dgl, dga, dya, dyl, gs['mix_post_g']), ((l_gu2a,),) = merge_bwd(
        dh2, m, sw['mix_post_g'], projs, g_lru, g_attn, p_l, p_a, 'merge_bwd', stages=[ChipStage(halves(s_gu2)[:1])])
    g_projs = [mm_tn([merged if n == 'w_out' else (y_lru if n == 'w_proj_lru' else y_attn)],
                     dm if n == 'w_out' else (dpl if n == 'w_proj_lru' else dpa), D_MODEL, 'd' + n)[0] for n in proj_names]
    st, g4 = pair_stage(proj_names, g_projs)
    (dq, dkv, dsk), ((l_gu2b,), lands, (full['ffn2_w_down'],)) = attn_bwd(
        qr, kr, v, dya, cos, sin_signed, sinks, 'attn_bwd', stages=[ChipStage(halves(s_gu2)[1:]), st, SwapStage([l_down2])])
    full['ffn2_w_down'] = [full['ffn2_w_down']]
    gs['attn_sinks'] = dsk[0:1, 0:N_Q_HEADS]
    s_projs = pair_sums(proj_names, g4, lands)
    (dgate, dxbr, vecs, dwa, dwx), (l_projs, full['ffn2_w_gu']) = lru_bwd(
        gate, xbr, h_lru, dyl, sw['conv_w'], sw['conv_b'], wa_bd, sw['lru_b_a'], wx_bd, sw['lru_b_x'], sw['lru_lambda'],
        'lru_bwd', stages=[ChipStage([(s, 0, s.shape[1]) for s in s_projs]), SwapStage([l_gu2a, l_gu2b])])
    gs['conv_w'] = vecs[0:4]
    gs['conv_b'], gs['lru_b_a'], gs['lru_b_x'], gs['lru_lambda'] = vecs[4:5], vecs[5:6], vecs[6:7], vecs[7:8]
    gs['lru_w_a'] = _diag_blocks(dwa)
    gs['lru_w_x'] = _diag_blocks(dwx)
    dz = [dgate, dxbr, dq, dkv, dgl, dga]
    g_in, ((lru_all,),) = mm_tn(dz, um, 512, 'dw_in', stages=[SmallGatherStage(_pack_lru(gs))])
    st, g4 = pair_stage(['w_in'], [g_in])
    (dh1, gs['mix_pre_g']), (lands, f_projs) = norm_bwd(dz, w_in_t, h1, sw['mix_pre_g'], dh2, 'mix_bwd_in',
                                                        stages=[st, SwapStage(l_projs)])
    for n, f in zip(proj_names, f_projs):
        full[n] = [f]
    (s_in,) = pair_sums(['w_in'], g4, lands)

    (df1, dgu1, gs['ffn1_post_g']), ((l_in_a,),) = ffn_bwd_a(dh1, f1, sw['ffn1_post_g'], w_down1, g1, u1, 'ffn1_bwd_a',
                                                             stages=[ChipStage(halves(s_in)[:1])])
    g_down1, _ = mm_tn([a1], df1, 1408, 'ffn1_dw_down')
    st, g4 = pair_stage(['ffn1_w_down'], [g_down1])
    g_gu1, ((l_in_b,), lands) = mm_tn([dgu1], n1, 1408, 'ffn1_dw_gu', stages=[ChipStage(halves(s_in)[1:]), st])
    (s_down1,) = pair_sums(['ffn1_w_down'], g4, lands)
    st, g4 = pair_stage(['ffn1_w_gu'], [g_gu1])
    (dx, gs['ffn1_pre_g']), ((l_down1,), lands, full['w_in']) = norm_bwd(
        [dgu1], w_gu1, xs, sw['ffn1_pre_g'], dh1, 'ffn1_bwd_b',
        stages=[ChipStage([(s_down1, 0, s_down1.shape[1])]), st, SwapStage([l_in_a, l_in_b])])
    (s_gu1,) = pair_sums(['ffn1_w_gu'], g4, lands)
    loss_row = jnp.pad(loss_blk[0:1], ((0, 0), (0, D_MODEL - loss_blk.shape[1])))
    vec_blk = _pack_vecs(gs, jnp.concatenate([gs['conv_w'], loss_row], axis=0))
    send, recv, s_thru, land_thru, token = chip_exchange_start(s_gu1)
    out_g, out_d, out_m, out_v = {}, {}, {}, {}

    def adam(n, after=()):
        fn = adam_cols if dict((k, t) for k, _, t in PACK)[n] else adam_rows
        g_, d_, m_, v_ = fn(full[n], n, w[n][0], mom[n][0], var[n][0], after=after)
        out_g[n], out_d[n], out_m[n], out_v[n] = g_[None], d_[None], m_[None], v_[None]

    behind = token
    for n in ['ffn2_w_gu', 'w_in', 'ffn2_w_down'] + proj_names:
        adam(n, after=(behind,))
        behind = out_v[n]
    s_back, l_gu1 = chip_exchange_wait(send, recv, s_thru, land_thru, after=behind)
    own = lax.dynamic_slice_in_dim(s_back, me_q, 1, axis=0)
    l_gu1 = lax.dynamic_update_slice_in_dim(l_gu1, own, me_q, axis=0)
    (vec_all,), (f_down1, f_gu1) = comm_call('swap_last', [SmallGatherStage(vec_blk), SwapStage([l_down1, l_gu1])])
    full['ffn1_w_down'] = [f_down1]
    full['ffn1_w_gu'] = [f_gu1]
    adam('ffn1_w_gu')
    adam('ffn1_w_down')

    tot = small_sum(vec_all, lru_all)
    loss = tot[ROW_WA - 1, 0]
    conv_g = lax.dynamic_slice(tot[ROW_CONV:ROW_CONV + 4], (0, me_q * (LRU_W // N_CHIPS)), (4, LRU_W // N_CHIPS))
    small_g = _unpack_small(tot, shapes)
    small_g['conv_w'] = conv_g.reshape(shapes['conv_w'])
    g_pack = jnp.concatenate([tot[:ROW_CONV], conv_g.reshape(1, D_MODEL), jnp.zeros((ROW_WA - ROW_CONV - 1, D_MODEL), F32),
                              tot[ROW_WA:]], axis=0)
    packs = [_pack_small({n: d[n] for n in SMALL}, d['conv_w'].reshape(1, D_MODEL)) for d in (w, mom, var)]
    d_p, m_p, v_p = adam_small(g_pack, *packs)
    for n in SMALL:
        out_g[n] = small_g[n]
    for dst, p in ((out_d, d_p), (out_m, m_p), (out_v, v_p)):
        dst.update(_unpack_small(p, shapes))

    return (loss, dx[None], *[out_g[n] for n in WEIGHTS], *[out_d[n] for n in WEIGHTS],
            *[out_m[n] for n in WEIGHTS], *[out_v[n] for n in WEIGHTS])
```

```python
import jax
import jax.numpy as jnp
import numpy as np
from jax import lax
from jax.experimental import pallas as pl
from jax.experimental.pallas import tpu as pltpu

F32 = jnp.float32
BF16 = jnp.bfloat16

SEQ = 2048
D_MODEL = 1024
D_FF = 2816
LRU_W = 1024
LRU_BLOCK_W = 64
HEAD_DIM = 64
N_Q_HEADS = 16
N_KV_HEADS = 4
KV_W = N_KV_HEADS * HEAD_DIM
ATTN_BLOCK = 128
N_ATTN_BLOCKS = SEQ // ATTN_BLOCK
IN_SEGS = (1024, 1024, 1024, 256, 256, 1024, 1024)
IN_W = sum(IN_SEGS)
NORM_EPS = 1e-6
MASK_VALUE = -1e30
ROPE_THETA = 10000.0
LRU_C = 8.0
MACARON = 0.5
ADAM_LR = 0.001
ADAM_B1 = 0.9
ADAM_B2 = 0.999
ADAM_EPS = 1e-08
ADAM_WD = 0.01
ADAM_STEP = 10

N_CHIPS = 4
N_DEV = 8
VMEM_LIMIT = 56 * 1024 * 1024
MM_ROWS = 256
MESH = pl.DeviceIdType.MESH
ANY = pl.BlockSpec(memory_space=pl.ANY)

PACK = (('ffn1_w_gu', 1408, True), ('w_in', 1408, True), ('ffn2_w_gu', 1408, True),
        ('ffn1_w_down', 704, False), ('ffn2_w_down', 704, False),
        ('w_proj_lru', 256, False), ('w_proj_attn', 256, False), ('w_out', 256, False))
PACK_ROWS_OF = {n: r for n, r, _ in PACK}
PACK_OFF = {}
_o = 0
for _n, _r, _t in PACK:
    PACK_OFF[_n] = _o
    _o += _r

SMALL_VECS = ('ffn1_pre_g', 'ffn1_post_g', 'mix_pre_g', 'conv_b', 'lru_b_a', 'lru_b_x', 'lru_lambda',
              'mix_post_g', 'ffn2_pre_g', 'ffn2_post_g')
SMALL_ROWS = 144
ROW_SINKS, ROW_CONV, ROW_WA, ROW_WX = 10, 11, 16, 80


def _dot(a, b):
    return jnp.dot(a, b, preferred_element_type=F32)


def _dot_nt(a, b):
    return lax.dot_general(a, b, (((1,), (1,)), ((), ())), preferred_element_type=F32)


def _dot_tn(a, b):
    return lax.dot_general(a, b, (((0,), (0,)), ((), ())), preferred_element_type=F32)


def _params(n_grid):
    return pltpu.CompilerParams(dimension_semantics=("arbitrary",) * n_grid, vmem_limit_bytes=VMEM_LIMIT)


def _sigmoid(x):
    return 1.0 / (1.0 + jnp.exp(-x))


def _rsqrt_mean_sq(x):
    return lax.rsqrt(jnp.mean(x * x, axis=-1, keepdims=True) + NORM_EPS)


def _expm1(x):
    poly = x * (1.0 + x * (0.5 + x * (1.0 / 6.0)))
    return jnp.where(jnp.abs(x) < 0.02, poly, jnp.exp(x) - 1.0)


_GELU_K = 0.7978845608028654
_GELU_C = 0.044715


def _gelu(x):
    t = jnp.tanh(_GELU_K * (x + _GELU_C * x * x * x))
    return 0.5 * x * (1.0 + t), t


def _gelu_grad(x, t):
    return 0.5 * (1.0 + t) + 0.5 * x * (1.0 - t * t) * _GELU_K * (1.0 + 3.0 * _GELU_C * x * x)


def _load_weight(w_refs, dst_ref, sem):
    w_refs = list(w_refs) if isinstance(w_refs, (list, tuple)) else [w_refs]
    rows = dst_ref.shape[0] // N_CHIPS
    rp = rows // len(w_refs)
    cps = [pltpu.make_async_copy(w_ref.at[q], dst_ref.at[pl.ds(q * rows + p * rp, rp)], sem.at[p * N_CHIPS + q])
           for p, w_ref in enumerate(w_refs) for q in range(N_CHIPS)]
    for cp in cps:
        cp.start()
    for cp in cps:
        cp.wait()


def _weight_scratch(rows_total, parts=1):
    return [pltpu.VMEM((rows_total, D_MODEL), BF16), pltpu.SemaphoreType.DMA((N_CHIPS * parts,))]


_ROW = lambda tm: pl.BlockSpec((tm, D_MODEL), lambda i: (i, 0))
_VEC = pl.BlockSpec((1, D_MODEL), lambda i: (0, 0))


def _call(body, *, name, grid, in_specs, out_specs, out_shape, args, scratch_shapes=(), stages=()):
    in_specs, out_specs, out_shape, scratch_shapes = list(in_specs), list(out_specs), list(out_shape), list(scratch_shapes)
    n_in, n_out, n_sc = len(in_specs), len(out_specs), len(scratch_shapes)
    k_in = [len(s.inputs) for s in stages]
    k_out = [len(s.out_shape) for s in stages]
    k_sc = [len(s.scratch) for s in stages]
    last = grid[0] - 1

    def split(refs, counts):
        parts, pos = [], 0
        for k in counts:
            parts.append(refs[pos:pos + k])
            pos += k
        return parts

    kinds = tuple(sorted({k for s in stages for k in s.peers}))
    collective_id = {(): None, ('sib',): 0, ('chips',): 1, ('chips', 'sib'): 2}[kinds]

    def full(*refs):
        ins, s_ins, outs, s_outs, scr, s_scr = split(refs, [n_in, sum(k_in), n_out, sum(k_out), n_sc, sum(k_sc)])
        per_stage = list(zip(stages, split(s_ins, k_in), split(s_outs, k_out), split(s_scr, k_sc)))
        i = pl.program_id(0)
        if stages:
            @pl.when(i == 0)
            def _():
                x, y, c, chips = _place()
                peers = ([(x, y, 1 - c)] if 'sib' in kinds else []) + ([(cx, cy, c) for cx, cy in chips] if 'chips' in kinds else [])
                barrier = pltpu.get_barrier_semaphore()
                for peer in peers:
                    pl.semaphore_signal(barrier, inc=1, device_id=peer, device_id_type=MESH)
                pl.semaphore_wait(barrier, len(peers))
                for s, a, b, c_ in per_stage:
                    s.start(a, b, c_)

        body(*ins, *outs, *scr)
        if stages:
            @pl.when(i == last // 2)
            def _():
                for s, a, b, c in per_stage:
                    s.relay(a, b, c)

            @pl.when(i == max(last - 1, 0))
            def _():
                for s, a, b, c in per_stage:
                    s.mid(a, b, c)

            @pl.when(i == last)
            def _():
                for s, a, b, c in per_stage:
                    s.end(a, b, c)

    res = pl.pallas_call(
        full, name=name, grid=grid,
        in_specs=in_specs + [ANY] * sum(k_in),
        out_specs=out_specs + [ANY] * sum(k_out),
        out_shape=out_shape + [o for s in stages for o in s.out_shape],
        scratch_shapes=scratch_shapes + [x for s in stages for x in s.scratch],
        compiler_params=pltpu.CompilerParams(dimension_semantics=("arbitrary",), vmem_limit_bytes=VMEM_LIMIT,
                                             collective_id=collective_id),
    )(*args, *[a for s in stages for a in s.inputs])
    return list(res[:n_out]), split(list(res[n_out:]), k_out)


def ffn_fwd_a(x, g_pre, w_gu_t, name, stages=()):
    tm, tn = MM_ROWS, 256
    n_w = len(w_gu_t)

    def body(x_ref, gp_ref, *refs):
        w_refs = refs[:n_w]
        n_ref, g_ref, u_ref, a_ref, wt_ref, sem = refs[n_w:]

        @pl.when(pl.program_id(0) == 0)
        def _():
            _load_weight(w_refs, wt_ref, sem)

        xv = x_ref[...]
        n = (xv * _rsqrt_mean_sq(xv) * gp_ref[...]).astype(BF16)
        n_ref[...] = n
        for j in range(D_FF // tn):
            g = _dot_nt(n, wt_ref[j * tn:(j + 1) * tn, :])
            u = _dot_nt(n, wt_ref[D_FF + j * tn:D_FF + (j + 1) * tn, :])
            g_ref[:, j * tn:(j + 1) * tn] = g.astype(BF16)
            u_ref[:, j * tn:(j + 1) * tn] = u.astype(BF16)
            a_ref[:, j * tn:(j + 1) * tn] = (g * _sigmoid(g) * u).astype(BF16)

    wide = pl.BlockSpec((tm, D_FF), lambda i: (i, 0))
    return _call(
        body, name=name, grid=(SEQ // tm,),
        in_specs=[_ROW(tm), _VEC] + [ANY] * n_w,
        out_specs=[_ROW(tm), wide, wide, wide],
        out_shape=[jax.ShapeDtypeStruct((SEQ, D_MODEL), BF16)] + [jax.ShapeDtypeStruct((SEQ, D_FF), BF16)] * 3,
        scratch_shapes=_weight_scratch(2 * D_FF, n_w),
        args=[x, g_pre, *w_gu_t], stages=stages)


def ffn_fwd_b(a, w_down, g_post, h_in, name, target=None, stages=()):
    tm = MM_ROWS
    final = target is not None

    def body(*refs):
        if final:
            a_ref, wf_ref, gp_ref, h_ref, t_ref, f_ref, o_ref, loss_ref, wd_ref, sem = refs
        else:
            a_ref, wf_ref, gp_ref, h_ref, f_ref, o_ref, wd_ref, sem = refs

        @pl.when(pl.program_id(0) == 0)
        def _():
            _load_weight(wf_ref, wd_ref, sem)
            if final:
                loss_ref[...] = jnp.zeros_like(loss_ref)

        f = _dot(a_ref[...], wd_ref[...])
        f_ref[...] = f
        y = h_ref[...] + MACARON * (f * _rsqrt_mean_sq(f) * gp_ref[...])
        if final:
            err = y - t_ref[...]
            o_ref[...] = err * (1.0 / D_MODEL)
            loss_ref[...] += 0.5 * jnp.sum(err * err) * (1.0 / D_MODEL)
        else:
            o_ref[...] = y

    row = _ROW(tm)
    in_specs = [pl.BlockSpec((tm, D_FF), lambda i: (i, 0)), ANY, _VEC, row]
    out_specs = [row, row]
    out_shape = [jax.ShapeDtypeStruct((SEQ, D_MODEL), F32)] * 2
    args = [a, w_down, g_post, h_in]
    if final:
        in_specs.append(row)
        args.append(target)
        out_specs.append(pl.BlockSpec((8, 128), lambda i: (0, 0)))
        out_shape.append(jax.ShapeDtypeStruct((8, 128), F32))
    return _call(body, name=name, grid=(SEQ // tm,), in_specs=in_specs, out_specs=out_specs,
                 out_shape=out_shape, scratch_shapes=_weight_scratch(D_FF), args=args, stages=stages)


def ffn_bwd_a(d_out, f, g_post, w_down, g, u, name, stages=()):
    tm = MM_ROWS
    tc = 256

    def body(do_ref, f_ref, gp_ref, wf_ref, g_ref, u_ref, df_ref, dgu_ref, dgp_ref, wd_ref, sem):
        @pl.when(pl.program_id(0) == 0)
        def _():
            _load_weight(wf_ref, wd_ref, sem)
            dgp_ref[...] = jnp.zeros_like(dgp_ref)

        fv = f_ref[...]
        rf = _rsqrt_mean_sq(fv)
        fh = fv * rf
        dn = MACARON * do_ref[...]
        dgp_ref[...] += jnp.sum(dn * fh, axis=0, keepdims=True)
        t = dn * gp_ref[...]
        df = (rf * (t - fh * jnp.mean(t * fh, axis=-1, keepdims=True))).astype(BF16)
        df_ref[...] = df
        for c0 in range(0, D_FF, tc):
            da = _dot_nt(df, wd_ref[c0:c0 + tc, :])
            gv = g_ref[:, c0:c0 + tc].astype(F32)
            uv = u_ref[:, c0:c0 + tc].astype(F32)
            s = _sigmoid(gv)
            dgu_ref[:, c0:c0 + tc] = (da * uv * s * (1.0 + gv * (1.0 - s))).astype(BF16)
            dgu_ref[:, D_FF + c0:D_FF + c0 + tc] = (da * gv * s).astype(BF16)

    row = _ROW(tm)
    wide = pl.BlockSpec((tm, D_FF), lambda i: (i, 0))
    return _call(
        body, name=name, grid=(SEQ // tm,),
        in_specs=[row, row, _VEC, ANY, wide, wide],
        out_specs=[row, pl.BlockSpec((tm, 2 * D_FF), lambda i: (i, 0)), _VEC],
        out_shape=[jax.ShapeDtypeStruct((SEQ, D_MODEL), BF16), jax.ShapeDtypeStruct((SEQ, 2 * D_FF), BF16),
                   jax.ShapeDtypeStruct((1, D_MODEL), F32)],
        scratch_shapes=_weight_scratch(D_FF),
        args=[d_out, f, g_post, w_down, g, u], stages=stages)


def norm_bwd(pieces, w_t, x, g_pre, d_res, name, stages=()):
    tm = MM_ROWS
    widths = [p.shape[1] for p in pieces]
    offs = [sum(widths[:k]) for k in range(len(widths))]
    n_p = len(pieces)
    n_w = len(w_t)

    def body(*refs):
        p_refs = refs[:n_p]
        w_refs = refs[n_p:n_p + n_w]
        x_ref, g_ref, r_ref, dx_ref, dg_ref, wt_ref, sem = refs[n_p + n_w:]

        @pl.when(pl.program_id(0) == 0)
        def _():
            _load_weight(w_refs, wt_ref, sem)
            dg_ref[...] = jnp.zeros_like(dg_ref)

        dn = None
        for p_ref, lo, wd in zip(p_refs, offs, widths):
            part = _dot(p_ref[...], wt_ref[lo:lo + wd, :])
            dn = part if dn is None else dn + part
        xv = x_ref[...]
        r = _rsqrt_mean_sq(xv)
        xh = xv * r
        dg_ref[...] += jnp.sum(dn * xh, axis=0, keepdims=True)
        t = dn * g_ref[...]
        dx_ref[...] = r_ref[...] + r * (t - xh * jnp.mean(t * xh, axis=-1, keepdims=True))

    row = _ROW(tm)
    return _call(
        body, name=name, grid=(SEQ // tm,),
        in_specs=[pl.BlockSpec((tm, wd), lambda i: (i, 0)) for wd in widths] + [ANY] * n_w + [row, _VEC, row],
        out_specs=[row, _VEC],
        out_shape=[jax.ShapeDtypeStruct((SEQ, D_MODEL), F32), jax.ShapeDtypeStruct((1, D_MODEL), F32)],
        scratch_shapes=_weight_scratch(sum(widths), n_w),
        args=[*pieces, *w_t, x, g_pre, d_res], stages=stages)


def mm_tn(pieces, b, tm, name, stages=()):
    widths = [p.shape[1] for p in pieces]
    m_total = sum(widths)
    n_p = len(pieces)
    starts = [sum(widths[:k]) // tm for k in range(n_p)]
    counts = [wd // tm for wd in widths]

    def body(*refs):
        p_refs = refs[:n_p]
        b_ref, o_ref = refs[n_p:]
        i = pl.program_id(0)
        for p_ref, st, ct in zip(p_refs, starts, counts):
            @pl.when((i >= st) & (i < st + ct))
            def _(p_ref=p_ref):
                o_ref[...] = _dot_tn(p_ref[...], b_ref[...]).astype(BF16)

    def piece_spec(st, ct):
        return pl.BlockSpec((SEQ, tm), lambda i: (0, jnp.clip(i - st, 0, ct - 1)))

    (out,), stage_out = _call(
        body, name=name, grid=(m_total // tm,),
        in_specs=[piece_spec(st, ct) for st, ct in zip(starts, counts)] + [pl.BlockSpec((SEQ, D_MODEL), lambda i: (0, 0))],
        out_specs=[pl.BlockSpec((tm, D_MODEL), lambda i: (i, 0))],
        out_shape=[jax.ShapeDtypeStruct((m_total, D_MODEL), BF16)],
        args=[*pieces, b], stages=stages)
    return out, stage_out


def mix_in(h, g_pre, w_in_t, name, stages=()):
    tm = MM_ROWS
    offs = [sum(IN_SEGS[:k]) for k in range(len(IN_SEGS))]
    dts = [F32, F32, F32, F32, BF16, F32, F32]
    n_o = len(IN_SEGS)
    n_w = len(w_in_t)

    def body(*refs):
        h_ref, g_ref = refs[:2]
        w_refs = refs[2:2 + n_w]
        um_ref = refs[2 + n_w]
        o_refs = refs[3 + n_w:3 + n_w + n_o]
        wt_ref, sem = refs[3 + n_w + n_o:]

        @pl.when(pl.program_id(0) == 0)
        def _():
            _load_weight(w_refs, wt_ref, sem)

        hv = h_ref[...]
        um = (hv * _rsqrt_mean_sq(hv) * g_ref[...]).astype(BF16)
        um_ref[...] = um
        for o_ref, lo, wd in zip(o_refs, offs, IN_SEGS):
            for c0 in range(0, wd, 256):
                o_ref[:, c0:c0 + 256] = _dot_nt(um, wt_ref[lo + c0:lo + c0 + 256, :]).astype(o_ref.dtype)

    return _call(
        body, name=name, grid=(SEQ // tm,),
        in_specs=[_ROW(tm), _VEC] + [ANY] * n_w,
        out_specs=[_ROW(tm)] + [pl.BlockSpec((tm, wd), lambda i: (i, 0)) for wd in IN_SEGS],
        out_shape=[jax.ShapeDtypeStruct((SEQ, D_MODEL), BF16)]
        + [jax.ShapeDtypeStruct((SEQ, wd), dt) for wd, dt in zip(IN_SEGS, dts)],
        scratch_shapes=_weight_scratch(IN_W, n_w),
        args=[h, g_pre, *w_in_t], stages=stages)


LRU_TC = 256


def _conv_fwd(xb, cw, cb, tt):
    xc = xb * cw[3:4, :] + cb
    shifted = []
    for s in (1, 2, 3):
        sh = jnp.where(tt >= s, pltpu.roll(xb, s, 0), 0.0)
        shifted.append(sh)
        xc = xc + sh * cw[3 - s:4 - s, :]
    return xc, shifted


def _lru_gates(xc, wa, ba, wx, bx, lam):
    xcb = xc.astype(BF16)
    r = _sigmoid(_dot(xcb, wa) + ba)
    i = _sigmoid(_dot(xcb, wx) + bx)
    nl = -lam
    sp = jnp.maximum(nl, 0.0) + jnp.log1p(jnp.exp(-jnp.abs(nl)))
    la = (-LRU_C * r) * sp
    a = jnp.exp(la)
    mult = jnp.sqrt(jnp.maximum(-_expm1(2.0 * la), 0.0))
    return xcb, r, i, sp, a, mult


def _scan(a, b, tt, reverse, a_s, b_s):
    n = a.shape[0]
    tg = tt & 7
    for s in (1, 2, 4):
        keep = (tg < 8 - s) if reverse else (tg >= s)
        shift = n - s if reverse else s
        b = a * jnp.where(keep, pltpu.roll(b, shift, 0), 0.0) + b
        a = a * jnp.where(keep, pltpu.roll(a, shift, 0), 1.0)
    a_s[...] = a
    b_s[...] = b
    groups = n // 8

    def step(g, carry):
        gi = (groups - 1 - g) if reverse else g
        rows = pl.ds(pl.multiple_of(gi * 8, 8), 8)
        hg = a_s[rows, :] * carry + b_s[rows, :]
        b_s[rows, :] = hg
        return hg[0:1, :] if reverse else hg[7:8, :]

    lax.fori_loop(0, groups, step, jnp.zeros((1, a.shape[1]), F32), unroll=8)
    return b_s[...]


def _lru_specs():
    col = pl.BlockSpec((SEQ, LRU_TC), lambda j: (0, j))
    vec = pl.BlockSpec((1, LRU_TC), lambda j: (0, j))
    bd = pl.BlockSpec((1, LRU_TC, LRU_TC), lambda j: (j, 0, 0))
    cw = pl.BlockSpec((4, LRU_TC), lambda j: (0, j))
    return col, vec, bd, cw


def lru_fwd(gate, xbr, conv_w, conv_b, wa_bd, b_a, wx_bd, b_x, lam, name, stages=()):
    col, vec, bd, cw = _lru_specs()

    def body(gate_ref, xbr_ref, cw_ref, cb_ref, wa_ref, ba_ref, wx_ref, bx_ref, lam_ref, y_ref, h_ref, a_s, b_s):
        tt = lax.broadcasted_iota(jnp.int32, (SEQ, LRU_TC), 0)
        xc, _ = _conv_fwd(xbr_ref[...], cw_ref[...], cb_ref[...], tt)
        _, r, i, sp, a, mult = _lru_gates(xc, wa_ref[0], ba_ref[...], wx_ref[0], bx_ref[...], lam_ref[...])
        h = _scan(a, mult * (i * xc), tt, False, a_s, b_s)
        h_ref[...] = h
        gl, _ = _gelu(gate_ref[...])
        y_ref[...] = (h * gl).astype(BF16)

    return _call(
        body, name=name, grid=(LRU_W // LRU_TC,),
        in_specs=[col, col, cw, vec, bd, vec, bd, vec, vec],
        out_specs=[col, col],
        out_shape=[jax.ShapeDtypeStruct((SEQ, LRU_W), BF16), jax.ShapeDtypeStruct((SEQ, LRU_W), F32)],
        scratch_shapes=[pltpu.VMEM((SEQ, LRU_TC), F32)] * 2,
        args=[gate, xbr, conv_w, conv_b, wa_bd, b_a, wx_bd, b_x, lam], stages=stages)


def lru_bwd(gate, xbr, h, dy, conv_w, conv_b, wa_bd, b_a, wx_bd, b_x, lam, name, stages=()):
    col, vec, bd, cw = _lru_specs()

    def body(gate_ref, xbr_ref, h_ref, dy_ref, cw_ref, cb_ref, wa_ref, ba_ref, wx_ref, bx_ref, lam_ref,
             dgate_ref, dxbr_ref, vecs_ref, dwa_ref, dwx_ref, a_s, b_s):
        tt = lax.broadcasted_iota(jnp.int32, (SEQ, LRU_TC), 0)
        cwv = cw_ref[...]
        lam = lam_ref[...]
        xb = xbr_ref[...]
        xc, shifted = _conv_fwd(xb, cwv, cb_ref[...], tt)
        wa = wa_ref[0]
        wx = wx_ref[0]
        xcb, r, i, sp, a, mult = _lru_gates(xc, wa, ba_ref[...], wx, bx_ref[...], lam)
        hv = h_ref[...]
        dyv = dy_ref[...]
        gv = gate_ref[...]
        gl, th = _gelu(gv)
        dgate_ref[...] = (dyv * hv * _gelu_grad(gv, th)).astype(BF16)
        a_next = jnp.where(tt < SEQ - 1, pltpu.roll(a, SEQ - 1, 0), 0.0)
        gsum = _scan(a_next, dyv * gl, tt, True, a_s, b_s)
        h_prev = jnp.where(tt >= 1, pltpu.roll(hv, 1, 0), 0.0)
        d_mult = gsum * i * xc
        d_i = gsum * mult * xc
        d_xc = gsum * mult * i
        d_la = gsum * h_prev * a - d_mult * (a * a) / mult
        d_pr = (d_la * (-LRU_C * sp)) * r * (1.0 - r)
        d_pi = d_i * i * (1.0 - i)
        d_lam = jnp.sum(d_la * r, axis=0, keepdims=True) * (LRU_C * _sigmoid(-lam))
        d_prb = d_pr.astype(BF16)
        d_pib = d_pi.astype(BF16)
        d_xc = d_xc + _dot_nt(d_prb, wa) + _dot_nt(d_pib, wx)
        dwa_ref[0] = _dot_tn(xcb, d_prb)
        dwx_ref[0] = _dot_tn(xcb, d_pib)
        rows = [jnp.sum(d_xc * shifted[2], axis=0, keepdims=True),
                jnp.sum(d_xc * shifted[1], axis=0, keepdims=True),
                jnp.sum(d_xc * shifted[0], axis=0, keepdims=True),
                jnp.sum(d_xc * xb, axis=0, keepdims=True),
                jnp.sum(d_xc, axis=0, keepdims=True),
                jnp.sum(d_pr, axis=0, keepdims=True),
                jnp.sum(d_pi, axis=0, keepdims=True),
                d_lam]
        ri = lax.broadcasted_iota(jnp.int32, (8, LRU_TC), 0)
        acc = jnp.zeros((8, LRU_TC), F32)
        for k, rv in enumerate(rows):
            acc = jnp.where(ri == k, rv, acc)
        vecs_ref[...] = acc
        d_xb = d_xc * cwv[3:4, :]
        for s in (1, 2, 3):
            d_xb = d_xb + jnp.where(tt < SEQ - s, pltpu.roll(d_xc, SEQ - s, 0), 0.0) * cwv[3 - s:4 - s, :]
        dxbr_ref[...] = d_xb.astype(BF16)

    return _call(
        body, name=name, grid=(LRU_W // LRU_TC,),
        in_specs=[col, col, col, col, cw, vec, bd, vec, bd, vec, vec],
        out_specs=[col, col, pl.BlockSpec((8, LRU_TC), lambda j: (0, j)), bd, bd],
        out_shape=[jax.ShapeDtypeStruct((SEQ, LRU_W), BF16), jax.ShapeDtypeStruct((SEQ, LRU_W), BF16),
                   jax.ShapeDtypeStruct((8, LRU_W), F32),
                   jax.ShapeDtypeStruct((LRU_W // LRU_TC, LRU_TC, LRU_TC), F32),
                   jax.ShapeDtypeStruct((LRU_W // LRU_TC, LRU_TC, LRU_TC), F32)],
        scratch_shapes=[pltpu.VMEM((SEQ, LRU_TC), F32)] * 2,
        args=[gate, xbr, h, dy, conv_w, conv_b, wa_bd, b_a, wx_bd, b_x, lam], stages=stages)


def _rope(x, cos, sin_signed):
    w = x.shape[1]
    reps = w // 128
    if reps > 1:
        cos = jnp.tile(cos, (1, reps))
        sin_signed = jnp.tile(sin_signed, (1, reps))
    lane = lax.broadcasted_iota(jnp.int32, x.shape, 1)
    first = (lane & 63) < 32
    partner = jnp.where(first, pltpu.roll(x, w - 32, 1), pltpu.roll(x, 32, 1))
    return x * cos + partner * sin_signed


def _both_halves(t, odd):
    lo = lax.broadcasted_iota(jnp.int32, t.shape, 1) < 64
    rolled = pltpu.roll(t, 64, 1)
    return jnp.where(lo, rolled, t) if odd else jnp.where(lo, t, rolled)


def _stack_heads(ta, tb):
    lo = lax.broadcasted_iota(jnp.int32, ta.shape, 1) < 64
    return jnp.concatenate([jnp.where(lo, ta, 0.0), jnp.where(lo, 0.0, ta),
                            jnp.where(lo, tb, 0.0), jnp.where(lo, 0.0, tb)], axis=0)


def _unstack_heads(o):
    lo = lax.broadcasted_iota(jnp.int32, (ATTN_BLOCK, 128), 1) < 64
    return (jnp.where(lo, o[0:128], o[128:256]), jnp.where(lo, o[256:384], o[384:512]))


def _window_upper_t():
    shape = (ATTN_BLOCK, 4 * ATTN_BLOCK)
    return lax.broadcasted_iota(jnp.int32, shape, 0) > (lax.broadcasted_iota(jnp.int32, shape, 1) & (ATTN_BLOCK - 1))


def _fold_t(t, upper_t):
    return jnp.where(upper_t, t[:ATTN_BLOCK], t[ATTN_BLOCK:])


def _unfold_t(t, upper_t):
    zero = jnp.zeros_like(t)
    return jnp.concatenate([jnp.where(upper_t, t, zero), jnp.where(upper_t, zero, t)], axis=0)


def _attn_probs_t(kd, qs, sinks_ref, hk, first_block, upper_t):
    s = _fold_t(_dot_nt(kd, qs), upper_t) * (HEAD_DIM ** -0.5)
    s = jnp.where(jnp.logical_and(upper_t, first_block), MASK_VALUE, s)
    rg = lax.broadcasted_iota(jnp.int32, (1, 4 * ATTN_BLOCK), 1) >> 7
    sink = jnp.where(rg == 0, sinks_ref[4 * hk],
                     jnp.where(rg == 1, sinks_ref[4 * hk + 1],
                               jnp.where(rg == 2, sinks_ref[4 * hk + 2], sinks_ref[4 * hk + 3])))
    m = jnp.maximum(jnp.max(s, axis=0, keepdims=True), sink)
    e = jnp.exp(s - m)
    es = jnp.exp(sink - m)
    inv = 1.0 / (jnp.sum(e, axis=0, keepdims=True) + es)
    return e * inv, es * inv


def _prev(i):
    return jnp.maximum(i - 1, 0)


def attn_fwd(q, k, v, cos, sin_signed, sinks, name, stages=()):
    nb = ATTN_BLOCK

    def body(q_ref, kc_ref, kp_ref, vc_ref, vp_ref, cc_ref, sc_ref, cp_ref, sp_ref, sinks_ref,
             qr_ref, kr_ref, y_ref):
        first_block = pl.program_id(0) == 0
        qr = _rope(q_ref[...], cc_ref[...], sc_ref[...])
        kc = _rope(kc_ref[...], cc_ref[...], sc_ref[...])
        kp = _rope(kp_ref[...], cp_ref[...], sp_ref[...])
        qr_ref[...] = qr.astype(BF16)
        kr_ref[...] = kc.astype(BF16)
        k2 = jnp.concatenate([kp, kc], axis=0)
        v2 = jnp.concatenate([vp_ref[...].astype(F32), vc_ref[...].astype(F32)], axis=0)
        upper_t = _window_upper_t()
        for hk in range(N_KV_HEADS):
            kt = hk // 2
            kd = _both_halves(k2[:, kt * 128:(kt + 1) * 128], hk % 2).astype(BF16)
            vd = _both_halves(v2[:, kt * 128:(kt + 1) * 128], hk % 2).astype(BF16)
            qs = _stack_heads(qr[:, (2 * hk) * 128:(2 * hk + 1) * 128],
                              qr[:, (2 * hk + 1) * 128:(2 * hk + 2) * 128]).astype(BF16)
            p, _ = _attn_probs_t(kd, qs, sinks_ref, hk, first_block, upper_t)
            ta, tb = _unstack_heads(_dot_tn(_unfold_t(p.astype(BF16), upper_t), vd))
            y_ref[:, (2 * hk) * 128:(2 * hk + 1) * 128] = ta.astype(BF16)
            y_ref[:, (2 * hk + 1) * 128:(2 * hk + 2) * 128] = tb.astype(BF16)

    cur = lambda w: pl.BlockSpec((nb, w), lambda i: (i, 0))
    prv = lambda w: pl.BlockSpec((nb, w), lambda i: (_prev(i), 0))
    return _call(
        body, name=name, grid=(N_ATTN_BLOCKS,),
        in_specs=[cur(D_MODEL), cur(KV_W), prv(KV_W), cur(KV_W), prv(KV_W), cur(128), cur(128), prv(128), prv(128),
                  pl.BlockSpec(memory_space=pltpu.SMEM)],
        out_specs=[cur(D_MODEL), cur(KV_W), cur(D_MODEL)],
        out_shape=[jax.ShapeDtypeStruct((SEQ, D_MODEL), BF16), jax.ShapeDtypeStruct((SEQ, KV_W), BF16),
                   jax.ShapeDtypeStruct((SEQ, D_MODEL), BF16)],
        args=[q, k, k, v, v, cos, sin_signed, cos, sin_signed, sinks], stages=stages)


def attn_bwd(qr, kr, v, dy, cos, sin_signed, sinks, name, stages=()):
    nb = ATTN_BLOCK
    n_steps = N_ATTN_BLOCKS + 1
    scale = HEAD_DIM ** -0.5

    def body(q_ref, kc_ref, kp_ref, vc_ref, vp_ref, dy_ref, cc_ref, sc_ref, cp_ref, sp_ref, sinks_ref,
             dq_ref, dkv_ref, dsk_ref, ck_ref, cv_ref):
        dk_ref = dkv_ref.at[:, pl.ds(0, KV_W)]
        dv_ref = dkv_ref.at[:, pl.ds(KV_W, KV_W)]
        i = pl.program_id(0)

        @pl.when(i == 0)
        def _():
            dsk_ref[...] = jnp.zeros_like(dsk_ref)
            ck_ref[...] = jnp.zeros_like(ck_ref)
            cv_ref[...] = jnp.zeros_like(cv_ref)

        @pl.when(i < N_ATTN_BLOCKS)
        def _():
            qv = q_ref[...].astype(F32)
            dov = dy_ref[...].astype(F32)
            k2 = jnp.concatenate([kp_ref[...].astype(F32), kc_ref[...].astype(F32)], axis=0)
            v2 = jnp.concatenate([vp_ref[...].astype(F32), vc_ref[...].astype(F32)], axis=0)
            lane = lax.broadcasted_iota(jnp.int32, (8, 128), 1)
            lo = lax.broadcasted_iota(jnp.int32, (2 * nb, 128), 1) < 64
            dsk = jnp.zeros((8, 128), F32)
            dk_tiles = []
            dv_tiles = []
            upper_t = _window_upper_t()
            for hk in range(N_KV_HEADS):
                kt = hk // 2
                kd = _both_halves(k2[:, kt * 128:(kt + 1) * 128], hk % 2).astype(BF16)
                vd = _both_halves(v2[:, kt * 128:(kt + 1) * 128], hk % 2).astype(BF16)
                qs = _stack_heads(qv[:, (2 * hk) * 128:(2 * hk + 1) * 128],
                                  qv[:, (2 * hk + 1) * 128:(2 * hk + 2) * 128]).astype(BF16)
                dos = _stack_heads(dov[:, (2 * hk) * 128:(2 * hk + 1) * 128],
                                   dov[:, (2 * hk + 1) * 128:(2 * hk + 2) * 128]).astype(BF16)
                p, ps = _attn_probs_t(kd, qs, sinks_ref, hk, i == 0, upper_t)
                dp = _fold_t(_dot_nt(vd, dos), upper_t)
                delta = jnp.sum(p * dp, axis=0, keepdims=True)
                ds = _unfold_t((p * (dp - delta)).astype(BF16), upper_t)
                dsink = -ps * delta
                for g in range(4):
                    dsk = dsk + jnp.where(lane == 4 * hk + g, jnp.sum(dsink[:, g * nb:(g + 1) * nb]), 0.0)
                ta, tb = _unstack_heads(_dot_tn(ds, kd) * scale)
                dq_a = (2 * hk) * 128
                dq_ref[:, dq_a:dq_a + 128] = _rope(ta, cc_ref[...], -sc_ref[...]).astype(BF16)
                dq_ref[:, dq_a + 128:dq_a + 256] = _rope(tb, cc_ref[...], -sc_ref[...]).astype(BF16)
                rk = _dot(ds, qs) * scale
                rv = _dot(_unfold_t(p.astype(BF16), upper_t), dos)
                dk_tiles.append(rk + pltpu.roll(rk, 64, 1))
                dv_tiles.append(rv + pltpu.roll(rv, 64, 1))
            dsk_ref[...] += dsk
            dk_full = jnp.concatenate([jnp.where(lo, dk_tiles[0], dk_tiles[1]),
                                       jnp.where(lo, dk_tiles[2], dk_tiles[3])], axis=1)
            dv_full = jnp.concatenate([jnp.where(lo, dv_tiles[0], dv_tiles[1]),
                                       jnp.where(lo, dv_tiles[2], dv_tiles[3])], axis=1)
            dk_ref[...] = _rope(ck_ref[...] + dk_full[0:nb], cp_ref[...], -sp_ref[...]).astype(BF16)
            dv_ref[...] = (cv_ref[...] + dv_full[0:nb]).astype(BF16)
            ck_ref[...] = dk_full[nb:2 * nb]
            cv_ref[...] = dv_full[nb:2 * nb]

        @pl.when(i == N_ATTN_BLOCKS)
        def _():
            dk_ref[...] = _rope(ck_ref[...], cp_ref[...], -sp_ref[...]).astype(BF16)
            dv_ref[...] = cv_ref[...].astype(BF16)

    qi = lambda i: jnp.minimum(i, N_ATTN_BLOCKS - 1)
    cur = lambda w: pl.BlockSpec((nb, w), lambda i: (qi(i), 0))
    prv = lambda w: pl.BlockSpec((nb, w), lambda i: (_prev(qi(i)), 0))
    out_prev = lambda w: pl.BlockSpec((nb, w), lambda i: (_prev(i), 0))
    return _call(
        body, name=name, grid=(n_steps,),
        in_specs=[cur(D_MODEL), cur(KV_W), prv(KV_W), cur(KV_W), prv(KV_W), cur(D_MODEL),
                  cur(128), cur(128), out_prev(128), out_prev(128), pl.BlockSpec(memory_space=pltpu.SMEM)],
        out_specs=[cur(D_MODEL), out_prev(2 * KV_W), pl.BlockSpec((8, 128), lambda i: (0, 0))],
        out_shape=[jax.ShapeDtypeStruct((SEQ, D_MODEL), BF16), jax.ShapeDtypeStruct((SEQ, 2 * KV_W), BF16),
                   jax.ShapeDtypeStruct((8, 128), F32)],
        scratch_shapes=[pltpu.VMEM((nb, KV_W), F32), pltpu.VMEM((nb, KV_W), F32)],
        args=[qr, kr, kr, v, v, dy, cos, sin_signed, cos, sin_signed, sinks], stages=stages)


def _proj_scratch():
    return [pltpu.VMEM((D_MODEL, D_MODEL), BF16)] * 3 + [pltpu.SemaphoreType.DMA((3 * N_CHIPS,))]


def _load_projs(w_refs, wl_ref, wa_ref, wo_ref, sem):
    for k, (w_ref, dst) in enumerate(zip(w_refs, (wl_ref, wa_ref, wo_ref))):
        _load_weight(w_ref, dst, sem.at[pl.ds(k * N_CHIPS, N_CHIPS)])


def merge_fwd(y_lru, y_attn, g_lru, g_attn, projs, g_post, h_in, name, stages=()):
    tm = MM_ROWS

    def body(yl_ref, ya_ref, gl_ref, ga_ref, w1_ref, w2_ref, w3_ref, gp_ref, h_ref,
             pl_ref, pa_ref, mg_ref, m_ref, o_ref, wl_ref, wa_ref, wo_ref, sem):
        @pl.when(pl.program_id(0) == 0)
        def _():
            _load_projs((w1_ref, w2_ref, w3_ref), wl_ref, wa_ref, wo_ref, sem)

        p_l = _dot(yl_ref[...], wl_ref[...])
        p_a = _dot(ya_ref[...], wa_ref[...])
        pl_ref[...] = p_l.astype(BF16)
        pa_ref[...] = p_a.astype(BF16)
        merged = (_sigmoid(gl_ref[...]) * p_l + _sigmoid(ga_ref[...]) * p_a).astype(BF16)
        mg_ref[...] = merged
        m = _dot(merged, wo_ref[...])
        m_ref[...] = m
        o_ref[...] = h_ref[...] + m * _rsqrt_mean_sq(m) * gp_ref[...]

    row = _ROW(tm)
    return _call(
        body, name=name, grid=(SEQ // tm,),
        in_specs=[row, row, row, row, ANY, ANY, ANY, _VEC, row],
        out_specs=[row] * 5,
        out_shape=[jax.ShapeDtypeStruct((SEQ, D_MODEL), BF16)] * 3 + [jax.ShapeDtypeStruct((SEQ, D_MODEL), F32)] * 2,
        scratch_shapes=_proj_scratch(),
        args=[y_lru, y_attn, g_lru, g_attn, *projs, g_post, h_in], stages=stages)


def merge_bwd(d_out, m, g_post, projs, g_lru, g_attn, p_l, p_a, name, stages=()):
    tm = 256

    def body(do_ref, m_ref, gp_ref, w1_ref, w2_ref, w3_ref, gl_ref, ga_ref, pl_ref, pa_ref,
             dm_ref, dpl_ref, dpa_ref, dgl_ref, dga_ref, dya_ref, dyl_ref, dgp_ref, wl_ref, wa_ref, wo_ref, sem):
        @pl.when(pl.program_id(0) == 0)
        def _():
            _load_projs((w1_ref, w2_ref, w3_ref), wl_ref, wa_ref, wo_ref, sem)
            dgp_ref[...] = jnp.zeros_like(dgp_ref)

        mv = m_ref[...]
        rm = _rsqrt_mean_sq(mv)
        mh = mv * rm
        dn = do_ref[...]
        dgp_ref[...] += jnp.sum(dn * mh, axis=0, keepdims=True)
        t = dn * gp_ref[...]
        dm = (rm * (t - mh * jnp.mean(t * mh, axis=-1, keepdims=True))).astype(BF16)
        dm_ref[...] = dm
        dmg = _dot_nt(dm, wo_ref[...])
        sl = _sigmoid(gl_ref[...])
        sa = _sigmoid(ga_ref[...])
        dpl = (dmg * sl).astype(BF16)
        dpa = (dmg * sa).astype(BF16)
        dpl_ref[...] = dpl
        dpa_ref[...] = dpa
        dgl_ref[...] = (dmg * pl_ref[...].astype(F32) * sl * (1.0 - sl)).astype(BF16)
        dga_ref[...] = (dmg * pa_ref[...].astype(F32) * sa * (1.0 - sa)).astype(BF16)
        dyl_ref[...] = _dot_nt(dpl, wl_ref[...])
        dya_ref[...] = _dot_nt(dpa, wa_ref[...]).astype(BF16)

    row = _ROW(tm)
    return _call(
        body, name=name, grid=(SEQ // tm,),
        in_specs=[row, row, _VEC, ANY, ANY, ANY, row, row, row, row],
        out_specs=[row] * 7 + [_VEC],
        out_shape=[jax.ShapeDtypeStruct((SEQ, D_MODEL), BF16)] * 6 + [jax.ShapeDtypeStruct((SEQ, D_MODEL), F32),
                                                                       jax.ShapeDtypeStruct((1, D_MODEL), F32)],
        scratch_shapes=_proj_scratch(),
        args=[d_out, m, g_post, *projs, g_lru, g_attn, p_l, p_a], stages=stages)


def _rope_tables():
    half = HEAD_DIM // 2
    inv_freq = np.float32(ROPE_THETA) ** (-np.arange(half, dtype=np.float32) / np.float32(half))
    ang = np.arange(SEQ, dtype=np.float32)[:, None] * inv_freq[None, :]
    cos, sin = np.cos(ang), np.sin(ang)
    return (jnp.asarray(np.tile(np.concatenate([cos, cos], axis=1), (1, 2))),
            jnp.asarray(np.tile(np.concatenate([-sin, sin], axis=1), (1, 2))))


def _block_diag(w):
    per = LRU_TC // LRU_BLOCK_W
    w4 = w.reshape(LRU_W // LRU_TC, per, LRU_BLOCK_W, LRU_BLOCK_W)
    eye = jnp.eye(per, dtype=w.dtype)
    return jnp.einsum('jacd,ab->jacbd', w4, eye).reshape(LRU_W // LRU_TC, LRU_TC, LRU_TC).astype(BF16)


def _diag_blocks(p):
    per = LRU_TC // LRU_BLOCK_W
    p5 = p.reshape(LRU_W // LRU_TC, per, LRU_BLOCK_W, per, LRU_BLOCK_W)
    return jnp.stack([p5[:, a, :, a, :] for a in range(per)], axis=1).reshape(LRU_W // LRU_BLOCK_W, LRU_BLOCK_W, LRU_BLOCK_W)


def _place():
    x, y, c = lax.axis_index('x'), lax.axis_index('y'), lax.axis_index('c')
    chips = [(1 - x, y), (x, 1 - y), (1 - x, 1 - y)]
    return x, y, c, chips


def _rcopy(src, dst, send_sem, recv_sem, to):
    return pltpu.make_async_remote_copy(src_ref=src, dst_ref=dst, send_sem=send_sem, recv_sem=recv_sem,
                                        device_id=to, device_id_type=MESH)


class _Stage:
    inputs, out_shape, scratch, peers = (), (), (), ()

    def start(self, ins, outs, scr):
        plan = self._plan(ins, outs, scr)
        for ld in plan['loads']:
            ld.start()
        for cp in plan['sends']:
            cp.start()

    def relay(self, ins, outs, scr):
        pass

    def mid(self, ins, outs, scr):
        plan = self._plan(ins, outs, scr)
        for ld, st in zip(plan['loads'], plan['stores']):
            ld.wait()
            st.start()
        for arrived, onward in zip(plan['arrivals'], plan['forwards']):
            arrived.wait_recv()
            onward.start()

    def end(self, ins, outs, scr):
        plan = self._plan(ins, outs, scr)
        for st in plan['stores']:
            st.wait()
        for arrived in (plan['final_arrivals'] if plan['forwards'] else plan['arrivals']):
            arrived.wait_recv()
        for cp in plan['sends'] + plan['forwards']:
            cp.wait_send()


def _empty_plan():
    return dict(loads=[], stores=[], sends=[], arrivals=[], forwards=[], final_arrivals=[])


class GatherStage(_Stage):
    peers = ('chips', 'sib')
    N_CP = 12

    def __init__(self, items):
        self.ranges = [(off, rows) for _, off, rows in items]
        self.inputs = [src for src, _, _ in items]
        self.out_shape = [jax.ShapeDtypeStruct((N_CHIPS, rows, D_MODEL), BF16) for _, rows in self.ranges]
        n = self.N_CP * len(items)
        self.scratch = [pltpu.VMEM((sum(r for _, r in self.ranges), D_MODEL), BF16), pltpu.SemaphoreType.DMA((n,)),
                        pltpu.SemaphoreType.DMA((n,)), pltpu.SemaphoreType.DMA((2 * len(items),))]

    def _plan(self, ins, outs, scr):
        buf, send, recv, lsem = scr
        x, y, c, _ = _place()
        me_q, q_x, q_y, q_d = 2 * x + y, 2 * (1 - x) + y, 2 * x + (1 - y), 2 * (1 - x) + (1 - y)
        to_x, to_y, sib = (1 - x, y, c), (x, 1 - y, c), (x, y, 1 - c)
        plan = dict(loads=[], stores=[], first=[], early=[], relays=[], late=[], hand_early=[], hand_late=[], final=[])
        boff = 0
        for w, ((off, rows), p_ref, o_ref) in enumerate(zip(self.ranges, ins, outs)):
            hr = rows // 2
            ch = hr // 2
            plan['loads'].append(pltpu.make_async_copy(p_ref.at[pl.ds(off, rows)], buf.at[pl.ds(boff, rows)], lsem.at[2 * w]))
            plan['stores'].append(pltpu.make_async_copy(buf.at[pl.ds(boff, rows)], o_ref.at[me_q], lsem.at[2 * w + 1]))
            boff += rows
            base = w * self.N_CP
            mine = [pl.ds(pl.multiple_of(c * hr + k * ch, 16), ch) for k in range(2)]
            theirs = [pl.ds(pl.multiple_of((1 - c) * hr + k * ch, 16), ch) for k in range(2)]
            src = [p_ref.at[pl.ds(pl.multiple_of(off + c * hr + k * ch, 16), ch)] for k in range(2)]

            def cp(k, s, d, to):
                return _rcopy(s, d, send.at[base + k], recv.at[base + k], to)

            def here(q, rows_):
                return o_ref.at[q, rows_]

            plan['first'] += [cp(0, src[0], here(me_q, mine[0]), to_x), cp(2, src[1], here(me_q, mine[1]), to_y),
                              cp(1, src[1], here(me_q, mine[1]), to_x), cp(3, src[0], here(me_q, mine[0]), to_y)]
            x_a, y_b = here(q_x, mine[0]), here(q_y, mine[1])
            plan['early'] += [cp(0, x_a, x_a, to_x), cp(2, y_b, y_b, to_y)]
            plan['relays'] += [cp(4, x_a, x_a, to_y), cp(5, y_b, y_b, to_x)]
            plan['hand_early'] += [cp(6, x_a, x_a, sib), cp(7, y_b, y_b, sib)]
            x_b, y_a, d_a, d_b = here(q_x, mine[1]), here(q_y, mine[0]), here(q_d, mine[0]), here(q_d, mine[1])
            plan['late'] += [cp(1, x_b, x_b, to_x), cp(3, y_a, y_a, to_y), cp(4, d_a, d_a, to_y), cp(5, d_b, d_b, to_x)]
            plan['hand_late'] += [cp(8, x_b, x_b, sib), cp(9, y_a, y_a, sib), cp(10, d_a, d_a, sib), cp(11, d_b, d_b, sib)]
            for k, (q, piece) in enumerate([(q_x, 0), (q_y, 1), (q_x, 1), (q_y, 0), (q_d, 0), (q_d, 1)]):
                got = here(q, theirs[piece])
                plan['final'].append(cp(6 + k, got, got, sib))
        return plan

    def start(self, ins, outs, scr):
        plan = self._plan(ins, outs, scr)
        for ld in plan['loads']:
            ld.start()
        for cp in plan['first']:
            cp.start()

    def relay(self, ins, outs, scr):
        plan = self._plan(ins, outs, scr)
        for arrived in plan['early']:
            arrived.wait_recv()
        for cp in plan['relays'] + plan['hand_early']:
            cp.start()

    def mid(self, ins, outs, scr):
        plan = self._plan(ins, outs, scr)
        for ld, st in zip(plan['loads'], plan['stores']):
            ld.wait()
            st.start()
        for arrived in plan['late']:
            arrived.wait_recv()
        for cp in plan['hand_late']:
            cp.start()

    def end(self, ins, outs, scr):
        plan = self._plan(ins, outs, scr)
        for st in plan['stores']:
            st.wait()
        for arrived in plan['final']:
            arrived.wait_recv()
        for cp in plan['first'] + plan['relays'] + plan['hand_early'] + plan['hand_late']:
            cp.wait_send()


class PairStage(_Stage):
    peers = ('sib',)

    def __init__(self, grads):
        self.inputs = list(grads)
        self.out_shape = [jax.ShapeDtypeStruct((N_CHIPS, 1) + g.shape[2:], BF16) for g in grads]
        n_cp = N_CHIPS * len(grads)
        self.scratch = [pltpu.SemaphoreType.DMA((n_cp,)), pltpu.SemaphoreType.DMA((n_cp,))]

    def _plan(self, ins, outs, scr):
        send, recv = scr
        x, y, c, _ = _place()
        plan = _empty_plan()
        for w, (g_ref, l_ref) in enumerate(zip(ins, outs)):
            for q in range(N_CHIPS):
                i = w * N_CHIPS + q
                plan['sends'].append(_rcopy(g_ref.at[q, pl.ds(1 - c, 1)], l_ref.at[q], send.at[i], recv.at[i], (x, y, 1 - c)))
        plan['arrivals'] = plan['sends']
        return plan


class ChipStage(_Stage):
    peers = ('chips',)

    def __init__(self, items):
        self.ranges = [(off, n) for _, off, n in items]
        self.inputs = [s for s, _, _ in items]
        self.out_shape = [jax.ShapeDtypeStruct((N_CHIPS, n, D_MODEL), BF16) for _, n in self.ranges]
        n_cp = 3 * len(items)
        self.scratch = [pltpu.VMEM((sum(n for _, n in self.ranges), D_MODEL), BF16), pltpu.SemaphoreType.DMA((n_cp,)),
                        pltpu.SemaphoreType.DMA((n_cp,)), pltpu.SemaphoreType.DMA((2 * len(items),))]

    def _plan(self, ins, outs, scr):
        buf, send, recv, lsem = scr
        x, y, c, chips = _place()
        me_q = 2 * x + y
        plan = _empty_plan()
        boff = 0
        for w, ((off, n), s_ref, l_ref) in enumerate(zip(self.ranges, ins, outs)):
            rows = pl.ds(off, n)
            plan['loads'].append(pltpu.make_async_copy(s_ref.at[me_q, rows], buf.at[pl.ds(boff, n)], lsem.at[2 * w]))
            plan['stores'].append(pltpu.make_async_copy(buf.at[pl.ds(boff, n)], l_ref.at[me_q], lsem.at[2 * w + 1]))
            boff += n
            for j, (cx, cy) in enumerate(chips):
                i = w * 3 + j
                got = l_ref.at[2 * cx + cy]
                plan['sends'].append(_rcopy(s_ref.at[2 * cx + cy, rows], l_ref.at[me_q], send.at[i], recv.at[i], (cx, cy, c)))
                plan['arrivals'].append(_rcopy(got, got, send.at[i], recv.at[i], (cx, cy, c)))
        return plan


class SwapStage(_Stage):
    peers = ('sib',)

    def __init__(self, items):
        n = len(items)
        self.inputs = list(items)
        self.out_shape = [jax.ShapeDtypeStruct((2,) + a.shape, a.dtype) for a in items]
        self.scratch = [pltpu.VMEM(a.shape, a.dtype) for a in items] + [
            pltpu.SemaphoreType.DMA((n,)), pltpu.SemaphoreType.DMA((n,)), pltpu.SemaphoreType.DMA((2 * n,))]

    def _plan(self, ins, outs, scr):
        bufs, (send, recv, lsem) = scr[:len(ins)], scr[len(ins):]
        x, y, c, _ = _place()
        plan = _empty_plan()
        for w, (h_ref, o_ref, buf) in enumerate(zip(ins, outs, bufs)):
            plan['loads'].append(pltpu.make_async_copy(h_ref, buf, lsem.at[2 * w]))
            plan['stores'].append(pltpu.make_async_copy(buf, o_ref.at[c], lsem.at[2 * w + 1]))
            got = o_ref.at[1 - c]
            plan['sends'].append(_rcopy(h_ref, o_ref.at[c], send.at[w], recv.at[w], (x, y, 1 - c)))
            plan['arrivals'].append(_rcopy(got, got, send.at[w], recv.at[w], (x, y, 1 - c)))
        return plan


class SmallGatherStage(_Stage):
    peers = ('chips', 'sib')

    def __init__(self, blk):
        self.inputs = [blk]
        self.out_shape = [jax.ShapeDtypeStruct((N_DEV,) + blk.shape, blk.dtype)]
        self.scratch = [pltpu.VMEM(blk.shape, blk.dtype), pltpu.SemaphoreType.DMA((7,)), pltpu.SemaphoreType.DMA((7,)),
                        pltpu.SemaphoreType.DMA((2,))]

    def _plan(self, ins, outs, scr):
        (x_ref,), (o_ref,), (buf, send, recv, lsem) = ins, outs, scr
        x, y, c, chips = _place()
        sib = (x, y, 1 - c)

        def slot(px, py, pc):
            return o_ref.at[4 * px + 2 * py + pc]

        plan = _empty_plan()
        plan['loads'].append(pltpu.make_async_copy(x_ref, buf, lsem.at[0]))
        plan['stores'].append(pltpu.make_async_copy(buf, slot(x, y, c), lsem.at[1]))
        from_sib = slot(x, y, 1 - c)
        plan['sends'].append(_rcopy(x_ref, slot(x, y, c), send.at[0], recv.at[0], sib))
        plan['final_arrivals'].append(_rcopy(from_sib, from_sib, send.at[0], recv.at[0], sib))
        for j, (cx, cy) in enumerate(chips):
            got, got_sib = slot(cx, cy, c), slot(cx, cy, 1 - c)
            plan['sends'].append(_rcopy(x_ref, slot(x, y, c), send.at[1 + j], recv.at[1 + j], (cx, cy, c)))
            plan['arrivals'].append(_rcopy(got, got, send.at[1 + j], recv.at[1 + j], (cx, cy, c)))
            plan['forwards'].append(_rcopy(got, got, send.at[4 + j], recv.at[4 + j], sib))
            plan['final_arrivals'].append(_rcopy(got_sib, got_sib, send.at[4 + j], recv.at[4 + j], sib))
        return plan


_HBM = pl.BlockSpec(memory_space=pltpu.HBM)
_SEM = pl.BlockSpec(memory_space=pltpu.SEMAPHORE)
_DATAFLOW = pltpu.CompilerParams(has_side_effects=pltpu.SideEffectType.DATAFLOW_SIDE_EFFECTING)


def chip_exchange_start(s):
    def body(s_ref, land_ref, send, recv, s_thru, land_thru, token):
        x, y, c, chips = _place()
        for j, (cx, cy) in enumerate(chips):
            _rcopy(s_ref.at[2 * cx + cy], land_ref.at[2 * x + y], send.at[j], recv.at[j], (cx, cy, c)).start()
        token[...] = jnp.zeros_like(token)

    return pl.pallas_call(
        body, name='chip_exchange_start',
        out_shape=(pltpu.SemaphoreType.DMA((3,)), pltpu.SemaphoreType.DMA((3,)), pltpu.HBM(s.shape, s.dtype),
                   pltpu.HBM(s.shape, s.dtype), jax.ShapeDtypeStruct((8, 128), F32)),
        in_specs=(_HBM, _HBM), out_specs=(_SEM, _SEM, _HBM, _HBM, pl.BlockSpec(memory_space=pltpu.VMEM)),
        input_output_aliases={0: 2, 1: 3}, compiler_params=_DATAFLOW,
    )(pltpu.with_memory_space_constraint(s, pltpu.HBM),
      pltpu.with_memory_space_constraint(lax.empty(s.shape, s.dtype), pltpu.HBM))


def chip_exchange_wait(send, recv, s_thru, land_thru, after):
    def body(s_ref, land_ref, send_sem, recv_sem, after_ref, s_out, land_out):
        x, y, c, chips = _place()
        for j, (cx, cy) in enumerate(chips):
            cp = _rcopy(s_ref.at[2 * cx + cy], land_ref.at[2 * cx + cy], send_sem.at[j], recv_sem.at[j], (cx, cy, c))
            cp.wait_send()
            cp.wait_recv()

    return pl.pallas_call(
        body, name='chip_exchange_wait',
        out_shape=(pltpu.HBM(s_thru.shape, s_thru.dtype), pltpu.HBM(land_thru.shape, land_thru.dtype)),
        in_specs=(_HBM, _HBM, _SEM, _SEM, ANY), out_specs=(_HBM, _HBM),
        input_output_aliases={0: 0, 1: 1}, compiler_params=_DATAFLOW,
    )(s_thru, land_thru, send, recv, after)


def comm_call(name, stages):
    def body():
        pass

    return _call(body, name=name, grid=(1,), in_specs=[], out_specs=[], out_shape=[], args=[], stages=stages)[1]


def pair_sum(g4s, lands, c_arr, name):
    hr = g4s[0].shape[2]
    n = len(g4s)

    def body(c_ref, *refs):
        for g_ref, l_ref, o_ref in zip(refs[:n], refs[n:2 * n], refs[2 * n:]):
            o_ref[0] = (g_ref[0, 0].astype(F32) + l_ref[0, 0].astype(F32)).astype(BF16)

    return pl.pallas_call(
        body, name=name,
        grid_spec=pltpu.PrefetchScalarGridSpec(
            num_scalar_prefetch=1, grid=(N_CHIPS,),
            in_specs=[pl.BlockSpec((1, 1, hr, D_MODEL), lambda q, c: (q, c[0], 0, 0))] * n
            + [pl.BlockSpec((1, 1, hr, D_MODEL), lambda q, c: (q, 0, 0, 0))] * n,
            out_specs=[pl.BlockSpec((1, hr, D_MODEL), lambda q, c: (q, 0, 0))] * n),
        out_shape=[jax.ShapeDtypeStruct((N_CHIPS, hr, D_MODEL), BF16)] * n,
        compiler_params=_params(1),
    )(c_arr, *g4s, *lands)


def small_sum(vec_parts, lru_parts):
    def body(v_ref, l_ref, o_ref):
        for p_ref, lo, n in ((v_ref, 0, ROW_WA), (l_ref, ROW_WA, SMALL_ROWS - ROW_WA)):
            acc = p_ref[0]
            for s in range(1, N_DEV):
                acc = acc + p_ref[s]
            o_ref[lo:lo + n, :] = acc

    return pl.pallas_call(
        body, name='small_sum', grid=(1,),
        in_specs=[pl.BlockSpec(vec_parts.shape, lambda i: (0, 0, 0)), pl.BlockSpec(lru_parts.shape, lambda i: (0, 0, 0))],
        out_specs=pl.BlockSpec((SMALL_ROWS, D_MODEL), lambda i: (0, 0)),
        out_shape=jax.ShapeDtypeStruct((SMALL_ROWS, D_MODEL), F32),
        compiler_params=_params(1),
    )(vec_parts, lru_parts)


def _adam_math(w, g, m, v):
    m2 = ADAM_B1 * m + (1.0 - ADAM_B1) * g
    v2 = ADAM_B2 * v + (1.0 - ADAM_B2) * (g * g)
    m_hat = m2 / (1.0 - ADAM_B1 ** ADAM_STEP)
    v_hat = v2 / (1.0 - ADAM_B2 ** ADAM_STEP)
    delta = -ADAM_LR * (m_hat / (jnp.sqrt(v_hat) + ADAM_EPS) + ADAM_WD * w)
    return delta, m2, v2


def _adam_body(n_parts, transposed, n_after):
    def body(*refs):
        refs = refs[n_after:]
        g_refs = refs[:n_parts]
        w_ref, m_ref, v_ref, go_ref, d_ref, mo_ref, vo_ref = refs[n_parts:]
        def chips_added(blk):
            acc = blk[0].astype(F32)
            for s in range(1, N_CHIPS):
                acc = acc + blk[s].astype(F32)
            return acc

        if transposed:
            g = jnp.concatenate([chips_added(g_ref[h]) for h in range(2) for g_ref in g_refs], axis=0).T
        else:
            rows = [chips_added(g_ref[0]) for g_ref in g_refs]
            g = jnp.concatenate(rows, axis=0) if n_parts > 1 else rows[0]
        go_ref[...] = g
        d_ref[...], mo_ref[...], vo_ref[...] = _adam_math(w_ref[...], g, m_ref[...], v_ref[...])
    return body


def adam_rows(fulls, name, w, m, v, after=()):
    hr = w.shape[0] // 2
    blk = pl.BlockSpec((hr, D_MODEL), lambda h: (h, 0))
    return pl.pallas_call(
        _adam_body(len(fulls), False, len(after)), name='adam_' + name, grid=(2,),
        in_specs=[ANY] * len(after)
        + [pl.BlockSpec((1, N_CHIPS, f.shape[2], D_MODEL), lambda h: (h, 0, 0, 0)) for f in fulls] + [blk, blk, blk],
        out_specs=[blk] * 4,
        out_shape=[jax.ShapeDtypeStruct(w.shape, F32)] * 4,
        compiler_params=_params(1),
    )(*after, *fulls, w, m, v)


def adam_cols(fulls, name, w, m, v, after=()):
    cols = w.shape[1]
    tr = 256
    blk = pl.BlockSpec((tr, cols), lambda i: (i, 0))
    return pl.pallas_call(
        _adam_body(len(fulls), True, len(after)), name='adam_' + name, grid=(D_MODEL // tr,),
        in_specs=[ANY] * len(after)
        + [pl.BlockSpec((2, N_CHIPS, f.shape[2], tr), lambda i: (0, 0, 0, i)) for f in fulls] + [blk, blk, blk],
        out_specs=[blk] * 4,
        out_shape=[jax.ShapeDtypeStruct(w.shape, F32)] * 4,
        compiler_params=_params(1),
    )(*after, *fulls, w, m, v)


def adam_small(g, w, m, v):
    def body(g_ref, w_ref, m_ref, v_ref, d_ref, mo_ref, vo_ref):
        d_ref[...], mo_ref[...], vo_ref[...] = _adam_math(w_ref[...], g_ref[...], m_ref[...], v_ref[...])

    blk = pl.BlockSpec(w.shape, lambda i: (0, 0))
    return pl.pallas_call(
        body, name='adam_small', grid=(1,), in_specs=[blk] * 4, out_specs=[blk] * 3,
        out_shape=[jax.ShapeDtypeStruct(w.shape, F32)] * 3, compiler_params=_params(1),
    )(g, w, m, v)


WEIGHTS = ('ffn1_pre_g', 'ffn1_w_gu', 'ffn1_w_down', 'ffn1_post_g', 'mix_pre_g', 'w_in', 'conv_w', 'conv_b',
           'lru_w_a', 'lru_b_a', 'lru_w_x', 'lru_b_x', 'lru_lambda', 'attn_sinks', 'w_proj_lru', 'w_proj_attn',
           'w_out', 'mix_post_g', 'ffn2_pre_g', 'ffn2_w_gu', 'ffn2_w_down', 'ffn2_post_g')
SMALL = tuple(n for n in WEIGHTS if n not in PACK_OFF)


def cast_t(w, name, stages=()):
    cols = w.shape[1]
    tc = 128

    def body(w_ref, o_ref):
        o_ref[...] = w_ref[...].T.astype(BF16)

    (out,), stage_out = _call(
        body, name=name, grid=(cols // tc,),
        in_specs=[pl.BlockSpec((D_MODEL, tc), lambda j: (0, j))],
        out_specs=[pl.BlockSpec((tc, D_MODEL), lambda j: (j, 0))],
        out_shape=[jax.ShapeDtypeStruct((cols, D_MODEL), BF16)],
        args=[w], stages=stages)
    return out, stage_out


def _pack_vecs(d, conv_rows):
    sinks = jnp.pad(d['attn_sinks'].reshape(1, N_Q_HEADS), ((0, 0), (0, D_MODEL - N_Q_HEADS)))
    conv = jnp.pad(conv_rows, ((0, ROW_WA - ROW_CONV - conv_rows.shape[0]), (0, 0)))
    return jnp.concatenate([d[n].reshape(1, D_MODEL) for n in SMALL_VECS] + [sinks, conv], axis=0)


def _pack_lru(d):
    return jnp.concatenate([d['lru_w_a'].reshape(64, D_MODEL), d['lru_w_x'].reshape(64, D_MODEL)], axis=0)


def _pack_small(d, conv_rows):
    return jnp.concatenate([_pack_vecs(d, conv_rows), _pack_lru(d)], axis=0)


def _unpack_small(p, shapes):
    out = {n: p[k:k + 1].reshape(shapes[n]) for k, n in enumerate(SMALL_VECS)}
    out['attn_sinks'] = p[ROW_SINKS:ROW_SINKS + 1, :N_Q_HEADS].reshape(shapes['attn_sinks'])
    out['conv_w'] = p[ROW_CONV:ROW_CONV + 1].reshape(shapes['conv_w'])
    out['lru_w_a'] = p[ROW_WA:ROW_WA + 64].reshape(shapes['lru_w_a'])
    out['lru_w_x'] = p[ROW_WX:ROW_WX + 64].reshape(shapes['lru_w_x'])
    return out


def kernel(x, ffn1_pre_g, ffn1_w_gu, ffn1_w_down, ffn1_post_g, mix_pre_g, w_in, conv_w, conv_b, lru_w_a, lru_b_a, lru_w_x, lru_b_x, lru_lambda, attn_sinks, w_proj_lru, w_proj_attn, w_out, mix_post_g, ffn2_pre_g, ffn2_w_gu, ffn2_w_down, ffn2_post_g, loss_target, m_ffn1_pre_g, m_ffn1_w_gu, m_ffn1_w_down, m_ffn1_post_g, m_mix_pre_g, m_w_in, m_conv_w, m_conv_b, m_lru_w_a, m_lru_b_a, m_lru_w_x, m_lru_b_x, m_lru_lambda, m_attn_sinks, m_w_proj_lru, m_w_proj_attn, m_w_out, m_mix_post_g, m_ffn2_pre_g, m_ffn2_w_gu, m_ffn2_w_down, m_ffn2_post_g, v_ffn1_pre_g, v_ffn1_w_gu, v_ffn1_w_down, v_ffn1_post_g, v_mix_pre_g, v_w_in, v_conv_w, v_conv_b, v_lru_w_a, v_lru_b_a, v_lru_w_x, v_lru_b_x, v_lru_lambda, v_attn_sinks, v_w_proj_lru, v_w_proj_attn, v_w_out, v_mix_post_g, v_ffn2_pre_g, v_ffn2_w_gu, v_ffn2_w_down, v_ffn2_post_g):
    given = dict(locals())
    w = {n: given[n] for n in WEIGHTS}
    mom = {n: given['m_' + n] for n in WEIGHTS}
    var = {n: given['v_' + n] for n in WEIGHTS}
    shapes = {n: w[n].shape for n in WEIGHTS}
    xq = lax.axis_index('x')
    yq = lax.axis_index('y')
    cq = lax.axis_index('c')
    me_q = 2 * xq + yq

    c_arr = cq.reshape(1).astype(jnp.int32)
    xs, target = x[0], loss_target[0]
    sw = {n: (w[n][0] if w[n].ndim > 2 else w[n]) for n in SMALL}
    cos, sin_signed = _rope_tables()
    wa_bd = _block_diag(sw['lru_w_a'])
    wx_bd = _block_diag(sw['lru_w_x'])
    sinks = sw['attn_sinks'].reshape(N_Q_HEADS)

    shard = {n: w[n][0].astype(BF16) for n, _, t in PACK if not t}
    conv_pad = jnp.pad(w['conv_w'][0], ((0, 4), (0, 0)))

    def whole(name):
        return (shard[name], 0, PACK_ROWS_OF[name])

    def part(name, p, n_parts=2):
        rows = PACK_ROWS_OF[name] // n_parts
        return (shard[name], p * rows, rows)

    shard['ffn1_w_gu'], _ = cast_t(w['ffn1_w_gu'][0], 'cast_ffn1_w_gu')
    shard['w_in'], ((w_gu1a,), (conv_all,)) = cast_t(
        w['w_in'][0], 'cast_w_in', stages=[GatherStage([part('ffn1_w_gu', 0)]), SmallGatherStage(conv_pad)])
    shard['ffn2_w_gu'], ((w_gu1b,),) = cast_t(w['ffn2_w_gu'][0], 'cast_ffn2_w_gu', stages=[GatherStage([part('ffn1_w_gu', 1)])])
    w_gu1 = [w_gu1a, w_gu1b]
    sw['conv_w'] = jnp.transpose(conv_all[0::2, :4, :], (1, 0, 2)).reshape(4, LRU_W)
    proj_names = ['w_proj_lru', 'w_proj_attn', 'w_out']

    (n1, g1, u1, a1), ((w_down1, w_in_a),) = ffn_fwd_a(xs, sw['ffn1_pre_g'], w_gu1, 'ffn1_fwd_a',
                                                        stages=[GatherStage([whole('ffn1_w_down'), part('w_in', 0)])])
    (f1, h1), ((w_in_b,),) = ffn_fwd_b(a1, w_down1, sw['ffn1_post_g'], xs, 'ffn1_fwd_b',
                                       stages=[GatherStage([part('w_in', 1)])])
    w_in_t = [w_in_a, w_in_b]
    (um, gate, xbr, q, k, v, g_lru, g_attn), ((w_gu2a,),) = mix_in(h1, sw['mix_pre_g'], w_in_t, 'mix_in',
                                                                   stages=[GatherStage([part('ffn2_w_gu', 0)])])
    (y_lru, h_lru), ((w_gu2b,),) = lru_fwd(gate, xbr, sw['conv_w'], sw['conv_b'], wa_bd, sw['lru_b_a'], wx_bd, sw['lru_b_x'],
                                           sw['lru_lambda'], 'lru_fwd', stages=[GatherStage([part('ffn2_w_gu', 1)])])
    (qr, kr, y_attn), (projs,) = attn_fwd(q, k, v, cos, sin_signed, sinks, 'attn_fwd',
                                          stages=[GatherStage([whole(n) for n in proj_names])])
    (p_l, p_a, merged, m, h2), ((w_down2,),) = merge_fwd(y_lru, y_attn, g_lru, g_attn, projs, sw['mix_post_g'], h1, 'merge_fwd',
                                                         stages=[GatherStage([whole('ffn2_w_down')])])
    w_gu2 = [w_gu2a, w_gu2b]
    (n2, g2, u2, a2), _ = ffn_fwd_a(h2, sw['ffn2_pre_g'], w_gu2, 'ffn2_fwd_a')
    (f2, dy, loss_blk), _ = ffn_fwd_b(a2, w_down2, sw['ffn2_post_g'], h2, 'ffn2_fwd_b', target=target)

    gs, full = {}, {}

    def pair_stage(names, grads):
        g4 = [g.reshape(N_CHIPS, 2, PACK_ROWS_OF[n] // 2, D_MODEL) for n, g in zip(names, grads)]
        return PairStage(g4), g4

    def pair_sums(names, g4, lands):
        return list(pair_sum(list(g4), list(lands), c_arr, 'pair_sum_' + names[0]))

    def halves(s, n_parts=2):
        n = s.shape[1] // n_parts
        return [(s, p * n, n) for p in range(n_parts)]

    (df2, dgu2, gs['ffn2_post_g']), _ = ffn_bwd_a(dy, f2, sw['ffn2_post_g'], w_down2, g2, u2, 'ffn2_bwd_a')
    g_down2, _ = mm_tn([a2], df2, 1408, 'ffn2_dw_down')
    st, g4 = pair_stage(['ffn2_w_down'], [g_down2])
    g_gu2, (lands,) = mm_tn([dgu2], n2, 1408, 'ffn2_dw_gu', stages=[st])
    (s_down2,) = pair_sums(['ffn2_w_down'], g4, lands)
    st, g4 = pair_stage(['ffn2_w_gu'], [g_gu2])
    (dh2, gs['ffn2_pre_g']), ((l_down2,), lands) = norm_bwd([dgu2], w_gu2, h2, sw['ffn2_pre_g'], dy, 'ffn2_bwd_b',
                                                            stages=[ChipStage([(s_down2, 0, s_down2.shape[1])]), st])
    (s_gu2,) = pair_sums(['ffn2_w_gu'], g4, lands)

    (dm, dpl, dpa, dgl, dga, dya, dyl, gs['mix_post_g']), ((l_gu2a,),) = merge_bwd(
        dh2, m, sw['mix_post_g'], projs, g_lru, g_attn, p_l, p_a, 'merge_bwd', stages=[ChipStage(halves(s_gu2)[:1])])
    g_projs = [mm_tn([merged if n == 'w_out' else (y_lru if n == 'w_proj_lru' else y_attn)],
                     dm if n == 'w_out' else (dpl if n == 'w_proj_lru' else dpa), D_MODEL, 'd' + n)[0] for n in proj_names]
    st, g4 = pair_stage(proj_names, g_projs)
    (dq, dkv, dsk), ((l_gu2b,), lands, (full['ffn2_w_down'],)) = attn_bwd(
        qr, kr, v, dya, cos, sin_signed, sinks, 'attn_bwd', stages=[ChipStage(halves(s_gu2)[1:]), st, SwapStage([l_down2])])
    full['ffn2_w_down'] = [full['ffn2_w_down']]
    gs['attn_sinks'] = dsk[0:1, 0:N_Q_HEADS]
    s_projs = pair_sums(proj_names, g4, lands)
    (dgate, dxbr, vecs, dwa, dwx), (l_projs, full['ffn2_w_gu']) = lru_bwd(
        gate, xbr, h_lru, dyl, sw['conv_w'], sw['conv_b'], wa_bd, sw['lru_b_a'], wx_bd, sw['lru_b_x'], sw['lru_lambda'],
        'lru_bwd', stages=[ChipStage([(s, 0, s.shape[1]) for s in s_projs]), SwapStage([l_gu2a, l_gu2b])])
    gs['conv_w'] = vecs[0:4]
    gs['conv_b'], gs['lru_b_a'], gs['lru_b_x'], gs['lru_lambda'] = vecs[4:5], vecs[5:6], vecs[6:7], vecs[7:8]
    gs['lru_w_a'] = _diag_blocks(dwa)
    gs['lru_w_x'] = _diag_blocks(dwx)
    dz = [dgate, dxbr, dq, dkv, dgl, dga]
    g_in, ((lru_all,),) = mm_tn(dz, um, 512, 'dw_in', stages=[SmallGatherStage(_pack_lru(gs))])
    st, g4 = pair_stage(['w_in'], [g_in])
    (dh1, gs['mix_pre_g']), (lands, f_projs) = norm_bwd(dz, w_in_t, h1, sw['mix_pre_g'], dh2, 'mix_bwd_in',
                                                        stages=[st, SwapStage(l_projs)])
    for n, f in zip(proj_names, f_projs):
        full[n] = [f]
    (s_in,) = pair_sums(['w_in'], g4, lands)

    (df1, dgu1, gs['ffn1_post_g']), ((l_in_a,),) = ffn_bwd_a(dh1, f1, sw['ffn1_post_g'], w_down1, g1, u1, 'ffn1_bwd_a',
                                                             stages=[ChipStage(halves(s_in)[:1])])
    g_down1, _ = mm_tn([a1], df1, 1408, 'ffn1_dw_down')
    st, g4 = pair_stage(['ffn1_w_down'], [g_down1])
    g_gu1, ((l_in_b,), lands) = mm_tn([dgu1], n1, 1408, 'ffn1_dw_gu', stages=[ChipStage(halves(s_in)[1:]), st])
    (s_down1,) = pair_sums(['ffn1_w_down'], g4, lands)
    st, g4 = pair_stage(['ffn1_w_gu'], [g_gu1])
    (dx, gs['ffn1_pre_g']), ((l_down1,), lands, full['w_in']) = norm_bwd(
        [dgu1], w_gu1, xs, sw['ffn1_pre_g'], dh1, 'ffn1_bwd_b',
        stages=[ChipStage([(s_down1, 0, s_down1.shape[1])]), st, SwapStage([l_in_a, l_in_b])])
    (s_gu1,) = pair_sums(['ffn1_w_gu'], g4, lands)
    loss_row = jnp.pad(loss_blk[0:1], ((0, 0), (0, D_MODEL - loss_blk.shape[1])))
    vec_blk = _pack_vecs(gs, jnp.concatenate([gs['conv_w'], loss_row], axis=0))
    send, recv, s_thru, land_thru, token = chip_exchange_start(s_gu1)
    out_g, out_d, out_m, out_v = {}, {}, {}, {}

    def adam(n, after=()):
        fn = adam_cols if dict((k, t) for k, _, t in PACK)[n] else adam_rows
        g_, d_, m_, v_ = fn(full[n], n, w[n][0], mom[n][0], var[n][0], after=after)
        out_g[n], out_d[n], out_m[n], out_v[n] = g_[None], d_[None], m_[None], v_[None]

    behind = token
    for n in ['ffn2_w_gu', 'w_in', 'ffn2_w_down'] + proj_names:
        adam(n, after=(behind,))
        behind = out_v[n]
    s_back, l_gu1 = chip_exchange_wait(send, recv, s_thru, land_thru, after=behind)
    own = lax.dynamic_slice_in_dim(s_back, me_q, 1, axis=0)
    l_gu1 = lax.dynamic_update_slice_in_dim(l_gu1, own, me_q, axis=0)
    (vec_all,), (f_down1, f_gu1) = comm_call('swap_last', [SmallGatherStage(vec_blk), SwapStage([l_down1, l_gu1])])
    full['ffn1_w_down'] = [f_down1]
    full['ffn1_w_gu'] = [f_gu1]
    adam('ffn1_w_gu')
    adam('ffn1_w_down')

    tot = small_sum(vec_all, lru_all)
    loss = tot[ROW_WA - 1, 0]
    conv_g = lax.dynamic_slice(tot[ROW_CONV:ROW_CONV + 4], (0, me_q * (LRU_W // N_CHIPS)), (4, LRU_W // N_CHIPS))
    small_g = _unpack_small(tot, shapes)
    small_g['conv_w'] = conv_g.reshape(shapes['conv_w'])
    g_pack = jnp.concatenate([tot[:ROW_CONV], conv_g.reshape(1, D_MODEL), jnp.zeros((ROW_WA - ROW_CONV - 1, D_MODEL), F32),
                              tot[ROW_WA:]], axis=0)
    packs = [_pack_small({n: d[n] for n in SMALL}, d['conv_w'].reshape(1, D_MODEL)) for d in (w, mom, var)]
    d_p, m_p, v_p = adam_small(g_pack, *packs)
    for n in SMALL:
        out_g[n] = small_g[n]
    for dst, p in ((out_d, d_p), (out_m, m_p), (out_v, v_p)):
        dst.update(_unpack_small(p, shapes))

    return (loss, dx[None], *[out_g[n] for n in WEIGHTS], *[out_d[n] for n in WEIGHTS],
            *[out_m[n] for n in WEIGHTS], *[out_v[n] for n in WEIGHTS])
```
